```python
import math
import jax, jax.numpy as jnp
from jax import lax
import numpy as np

D_MODEL = 1024
BATCH = 8
SEQ = 4096
DEPTH = 1

EPS = 1e-6
Q_BLOCK = 128
H_A = 8
QK_NOPE = 128
QK_ROPE = 64
V_DIM = 128
Q_LORA = 256
KV_LORA = 128
ROPE_THETA = 10000.0
H_B = 16
KV_B = 4
GROUP = H_B // KV_B
HD_B = 64
WINDOW = 128
NUM_BUCKETS = 32
MAX_DISTANCE = 128
D_FF = 2816
CONV_W = 3

W_IN_SIZES = (Q_LORA, KV_LORA + QK_ROPE, H_B * HD_B, KV_B * HD_B, KV_B * HD_B, D_MODEL, D_MODEL)
W_IN_COLS = sum(W_IN_SIZES)
W_IN_SPLITS = tuple(int(s) for s in np.cumsum(W_IN_SIZES)[:-1])

kernel_name = "hybrid_mla_swa_gated_convffn_encoder"


def rms_norm(x, g):
    xf = x.astype(jnp.float32)
    y = xf * lax.rsqrt(jnp.mean(xf * xf, axis=-1, keepdims=True) + EPS)
    return (y * g.astype(jnp.float32)).astype(x.dtype)


def apply_rope(x, positions):
    half = QK_ROPE // 2
    inv_freq = ROPE_THETA ** (-jnp.arange(half, dtype=jnp.float32) / half)
    ang = positions.astype(jnp.float32)[:, None] * inv_freq[None, :]
    cos = jnp.cos(ang)[None, :, None, :]
    sin = jnp.sin(ang)[None, :, None, :]
    xf = x.astype(jnp.float32)
    x1, x2 = xf[..., :half], xf[..., half:]
    out = jnp.concatenate([x1 * cos - x2 * sin, x2 * cos + x1 * sin], axis=-1)
    return out.astype(x.dtype)


def t5_bucket(rel):
    nb = NUM_BUCKETS // 2
    max_exact = nb // 2
    base = (rel > 0).astype(jnp.int32) * nb
    n = jnp.abs(rel)
    nf = jnp.maximum(n, 1).astype(jnp.float32)
    large = max_exact + (jnp.log(nf / max_exact) / math.log(MAX_DISTANCE / max_exact)
                         * (nb - max_exact)).astype(jnp.int32)
    large = jnp.minimum(large, nb - 1)
    return base + jnp.where(n < max_exact, n, large)


def mla_branch(q_lat, kv_lat, positions, q_a_norm_g, w_q_b, kv_a_norm_g, w_kv_b):
    B, S, _ = q_lat.shape
    q = (rms_norm(q_lat, q_a_norm_g) @ w_q_b).reshape(B, S, H_A, QK_NOPE + QK_ROPE)
    q = jnp.concatenate([q[..., :QK_NOPE], apply_rope(q[..., QK_NOPE:], positions)], axis=-1)
    c_kv, k_rope = kv_lat[..., :KV_LORA], kv_lat[..., KV_LORA:]
    kv = (rms_norm(c_kv, kv_a_norm_g) @ w_kv_b).reshape(B, S, H_A, QK_NOPE + V_DIM)
    k_nope, v = kv[..., :QK_NOPE], kv[..., QK_NOPE:]
    k_rope = apply_rope(k_rope[:, :, None, :], positions)
    k = jnp.concatenate([k_nope, jnp.broadcast_to(k_rope, (B, S, H_A, QK_ROPE))], axis=-1)
    scale = 1.0 / math.sqrt(QK_NOPE + QK_ROPE)
    nblk = S // Q_BLOCK
    qb = q.reshape(B, nblk, Q_BLOCK, H_A, QK_NOPE + QK_ROPE).transpose(1, 0, 2, 3, 4)

    def attend(q_blk):
        s = jnp.einsum('bqhd,bkhd->bhqk', q_blk, k, preferred_element_type=jnp.float32) * scale
        p = jax.nn.softmax(s, axis=-1).astype(v.dtype)
        return jnp.einsum('bhqk,bkhd->bqhd', p, v)

    o = lax.map(attend, qb)
    return o.transpose(1, 0, 2, 3, 4).reshape(B, S, H_A * V_DIM)


def window_branch(q, k, v, rel_bias, sinks):
    B, S, _ = q.shape
    q = q.reshape(B, S, KV_B, GROUP, HD_B)
    k = k.reshape(B, S, KV_B, HD_B)
    v = v.reshape(B, S, KV_B, HD_B)
    pad = ((0, 0), (WINDOW, WINDOW), (0, 0), (0, 0))
    kp = jnp.pad(k, pad)
    vp = jnp.pad(v, pad)
    span = Q_BLOCK + 2 * WINDOW
    a = jnp.arange(Q_BLOCK, dtype=jnp.int32)[:, None]
    c = jnp.arange(span, dtype=jnp.int32)[None, :]
    rel = c - WINDOW - a
    in_band = jnp.abs(rel) <= WINDOW
    bias = rel_bias[t5_bucket(rel)].astype(jnp.float32)
    bias = bias.transpose(2, 0, 1).reshape(KV_B, GROUP, Q_BLOCK, span)
    sink = sinks.astype(jnp.float32).reshape(1, KV_B, GROUP, 1, 1)
    scale = 1.0 / math.sqrt(HD_B)
    nblk = S // Q_BLOCK
    qb = q.reshape(B, nblk, Q_BLOCK, KV_B, GROUP, HD_B).transpose(1, 0, 2, 3, 4, 5)

    def attend(args):
        q_blk, n = args
        start = n * Q_BLOCK
        k_blk = lax.dynamic_slice_in_dim(kp, start, span, axis=1)
        v_blk = lax.dynamic_slice_in_dim(vp, start, span, axis=1)
        key_pos = start - WINDOW + c
        valid = in_band & (key_pos >= 0) & (key_pos < S)
        s = jnp.einsum('bqhgd,bkhd->bhgqk', q_blk, k_blk,
                       preferred_element_type=jnp.float32) * scale + bias
        s = jnp.where(valid, s, -1e30)
        sink_col = jnp.broadcast_to(sink, (B, KV_B, GROUP, Q_BLOCK, 1))
        p = jax.nn.softmax(jnp.concatenate([s, sink_col], axis=-1), axis=-1)[..., :span]
        return jnp.einsum('bhgqk,bkhd->bqhgd', p.astype(v_blk.dtype), v_blk)

    o = lax.map(attend, (qb, jnp.arange(nblk, dtype=jnp.int32)))
    return o.transpose(1, 0, 2, 3, 4, 5).reshape(B, S, H_B * HD_B)


def conv_ffn(h, w_up, conv_w, conv_b, w_down):
    u = h @ w_up
    up = jnp.pad(u, ((0, 0), (1, 1), (0, 0)))
    u = up[:, :-2] * conv_w[0] + up[:, 1:-1] * conv_w[1] + up[:, 2:] * conv_w[2] + conv_b
    g, val = u[..., :D_FF], u[..., D_FF:]
    return (jax.nn.silu(g) * val) @ w_down


def _fwd_setup_inputs(seed: int = 0) -> dict:
    key = jax.random.key(seed)
    ks = jax.random.split(key, 20)
    f32 = jnp.float32
    L = DEPTH

    def nrm(k, shape, scale):
        return jax.random.normal(k, shape, f32) * scale

    def gain(k, shape):
        return 1.0 + 0.05 * jax.random.normal(k, shape, f32)

    return {
        "x": jax.random.normal(ks[0], (BATCH, SEQ, D_MODEL), f32),
        "positions": jnp.arange(SEQ, dtype=jnp.int32),
        "norm1_g": gain(ks[1], (L, D_MODEL)),
        "w_in": nrm(ks[2], (L, D_MODEL, W_IN_COLS), D_MODEL ** -0.5),
        "q_a_norm_g": gain(ks[3], (L, Q_LORA)),
        "w_q_b": nrm(ks[4], (L, Q_LORA, H_A * (QK_NOPE + QK_ROPE)), Q_LORA ** -0.5),
        "kv_a_norm_g": gain(ks[5], (L, KV_LORA)),
        "w_kv_b": nrm(ks[6], (L, KV_LORA, H_A * (QK_NOPE + V_DIM)), KV_LORA ** -0.5),
        "rel_bias": nrm(ks[7], (NUM_BUCKETS, H_B), 0.5),
        "sinks": nrm(ks[8], (L, H_B), 0.5),
        "w_out": nrm(ks[9], (L, D_MODEL, D_MODEL), D_MODEL ** -0.5),
        "norm2_g": gain(ks[10], (L, D_MODEL)),
        "w_up": nrm(ks[11], (L, D_MODEL, 2 * D_FF), D_MODEL ** -0.5),
        "conv_w": nrm(ks[12], (L, CONV_W, 2 * D_FF), CONV_W ** -0.5),
        "conv_b": nrm(ks[13], (L, 2 * D_FF), 0.02),
        "w_down": nrm(ks[14], (L, D_FF, D_MODEL), D_FF ** -0.5),
        "final_norm_g": gain(ks[15], (D_MODEL,)),
    }


def _fwd_reference(x, positions, norm1_g, w_in, q_a_norm_g, w_q_b, kv_a_norm_g, w_kv_b, rel_bias,
              sinks, w_out, norm2_g, w_up, conv_w, conv_b, w_down, final_norm_g):
    for l in range(DEPTH):
        h = rms_norm(x, norm1_g[l])
        proj = h @ w_in[l]
        q_lat, kv_lat, q_b, k_b, v_b, gate_a, gate_b = jnp.split(proj, W_IN_SPLITS, axis=-1)
        o_a = mla_branch(q_lat, kv_lat, positions, q_a_norm_g[l], w_q_b[l],
                         kv_a_norm_g[l], w_kv_b[l])
        o_b = window_branch(q_b, k_b, v_b, rel_bias, sinks[l])
        mixed = jax.nn.sigmoid(gate_a) * o_a + jax.nn.sigmoid(gate_b) * o_b
        x = x + mixed @ w_out[l]
        x = x + conv_ffn(rms_norm(x, norm2_g[l]), w_up[l], conv_w[l], conv_b[l], w_down[l])
    return rms_norm(x, final_norm_g)


import jax as _jax
import jax.numpy as _jnp

TWIN_FORMAT = 'train_step'
FWD_PARAMS = ['x', 'positions', 'norm1_g', 'w_in', 'q_a_norm_g', 'w_q_b', 'kv_a_norm_g', 'w_kv_b', 'rel_bias', 'sinks', 'w_out', 'norm2_g', 'w_up', 'conv_w', 'conv_b', 'w_down', 'final_norm_g']
TWIN_WEIGHTS = ['norm1_g', 'w_in', 'q_a_norm_g', 'w_q_b', 'kv_a_norm_g', 'w_kv_b', 'rel_bias', 'sinks', 'w_out', 'norm2_g', 'w_up', 'conv_w', 'conv_b', 'w_down', 'final_norm_g']
TWIN_DIFF_INPUT = 'x'
TWIN_INPUTS = ['x', 'positions', 'norm1_g', 'w_in', 'q_a_norm_g', 'w_q_b', 'kv_a_norm_g', 'w_kv_b', 'rel_bias', 'sinks', 'w_out', 'norm2_g', 'w_up', 'conv_w', 'conv_b', 'w_down', 'final_norm_g', 'loss_target', 'm_norm1_g', 'm_w_in', 'm_q_a_norm_g', 'm_w_q_b', 'm_kv_a_norm_g', 'm_w_kv_b', 'm_rel_bias', 'm_sinks', 'm_w_out', 'm_norm2_g', 'm_w_up', 'm_conv_w', 'm_conv_b', 'm_w_down', 'm_final_norm_g', 'v_norm1_g', 'v_w_in', 'v_q_a_norm_g', 'v_w_q_b', 'v_kv_a_norm_g', 'v_w_kv_b', 'v_rel_bias', 'v_sinks', 'v_w_out', 'v_norm2_g', 'v_w_up', 'v_conv_w', 'v_conv_b', 'v_w_down', 'v_final_norm_g']
TWIN_OUTPUTS = ['loss', 'grad_x', 'grad_norm1_g', 'grad_w_in', 'grad_q_a_norm_g', 'grad_w_q_b', 'grad_kv_a_norm_g', 'grad_w_kv_b', 'grad_rel_bias', 'grad_sinks', 'grad_w_out', 'grad_norm2_g', 'grad_w_up', 'grad_conv_w', 'grad_conv_b', 'grad_w_down', 'grad_final_norm_g', 'delta_norm1_g', 'delta_w_in', 'delta_q_a_norm_g', 'delta_w_q_b', 'delta_kv_a_norm_g', 'delta_w_kv_b', 'delta_rel_bias', 'delta_sinks', 'delta_w_out', 'delta_norm2_g', 'delta_w_up', 'delta_conv_w', 'delta_conv_b', 'delta_w_down', 'delta_final_norm_g', 'new_m_norm1_g', 'new_m_w_in', 'new_m_q_a_norm_g', 'new_m_w_q_b', 'new_m_kv_a_norm_g', 'new_m_w_kv_b', 'new_m_rel_bias', 'new_m_sinks', 'new_m_w_out', 'new_m_norm2_g', 'new_m_w_up', 'new_m_conv_w', 'new_m_conv_b', 'new_m_w_down', 'new_m_final_norm_g', 'new_v_norm1_g', 'new_v_w_in', 'new_v_q_a_norm_g', 'new_v_w_q_b', 'new_v_kv_a_norm_g', 'new_v_w_kv_b', 'new_v_rel_bias', 'new_v_sinks', 'new_v_w_out', 'new_v_norm2_g', 'new_v_w_up', 'new_v_conv_w', 'new_v_conv_b', 'new_v_w_down', 'new_v_final_norm_g']
TWIN_LEAF_KINDS = {'loss': 'loss', 'grad_x': 'grad_x', 'grad_norm1_g': 'grad_w', 'grad_w_in': 'grad_w', 'grad_q_a_norm_g': 'grad_w', 'grad_w_q_b': 'grad_w', 'grad_kv_a_norm_g': 'grad_w', 'grad_w_kv_b': 'grad_w', 'grad_rel_bias': 'grad_w', 'grad_sinks': 'grad_w', 'grad_w_out': 'grad_w', 'grad_norm2_g': 'grad_w', 'grad_w_up': 'grad_w', 'grad_conv_w': 'grad_w', 'grad_conv_b': 'grad_w', 'grad_w_down': 'grad_w', 'grad_final_norm_g': 'grad_w', 'delta_norm1_g': 'delta_w', 'delta_w_in': 'delta_w', 'delta_q_a_norm_g': 'delta_w', 'delta_w_q_b': 'delta_w', 'delta_kv_a_norm_g': 'delta_w', 'delta_w_kv_b': 'delta_w', 'delta_rel_bias': 'delta_w', 'delta_sinks': 'delta_w', 'delta_w_out': 'delta_w', 'delta_norm2_g': 'delta_w', 'delta_w_up': 'delta_w', 'delta_conv_w': 'delta_w', 'delta_conv_b': 'delta_w', 'delta_w_down': 'delta_w', 'delta_final_norm_g': 'delta_w', 'new_m_norm1_g': 'new_m', 'new_m_w_in': 'new_m', 'new_m_q_a_norm_g': 'new_m', 'new_m_w_q_b': 'new_m', 'new_m_kv_a_norm_g': 'new_m', 'new_m_w_kv_b': 'new_m', 'new_m_rel_bias': 'new_m', 'new_m_sinks': 'new_m', 'new_m_w_out': 'new_m', 'new_m_norm2_g': 'new_m', 'new_m_w_up': 'new_m', 'new_m_conv_w': 'new_m', 'new_m_conv_b': 'new_m', 'new_m_w_down': 'new_m', 'new_m_final_norm_g': 'new_m', 'new_v_norm1_g': 'new_v', 'new_v_w_in': 'new_v', 'new_v_q_a_norm_g': 'new_v', 'new_v_w_q_b': 'new_v', 'new_v_kv_a_norm_g': 'new_v', 'new_v_w_kv_b': 'new_v', 'new_v_rel_bias': 'new_v', 'new_v_sinks': 'new_v', 'new_v_w_out': 'new_v', 'new_v_norm2_g': 'new_v', 'new_v_w_up': 'new_v', 'new_v_conv_w': 'new_v', 'new_v_conv_b': 'new_v', 'new_v_w_down': 'new_v', 'new_v_final_norm_g': 'new_v'}


def _forward(args):
    return _fwd_reference(*[args[k] for k in FWD_PARAMS])


def _output_shape():
    out = _jax.eval_shape(lambda: _forward(_fwd_setup_inputs(0)))
    return out.shape, out.dtype

N_MICROBATCH = 1
ADAM_LR = 0.001
ADAM_B1 = 0.9
ADAM_B2 = 0.999
ADAM_EPS = 1e-08
ADAM_WD = 0.01
ADAM_STEP = 10
PER_EXAMPLE_BATCH_AXIS = {'x': 0, 'loss_target': 0}
SHARED_INPUTS = ['positions']
_WEIGHT_DTYPES = {'norm1_g': _jnp.float32, 'w_in': _jnp.float32, 'q_a_norm_g': _jnp.float32, 'w_q_b': _jnp.float32, 'kv_a_norm_g': _jnp.float32, 'w_kv_b': _jnp.float32, 'rel_bias': _jnp.float32, 'sinks': _jnp.float32, 'w_out': _jnp.float32, 'norm2_g': _jnp.float32, 'w_up': _jnp.float32, 'conv_w': _jnp.float32, 'conv_b': _jnp.float32, 'w_down': _jnp.float32, 'final_norm_g': _jnp.float32}
MOMENT_SCALE = {'norm1_g': 3.358238e-02, 'w_in': 1.680411e-02, 'q_a_norm_g': 2.525301e-02, 'w_q_b': 1.109501e-02, 'kv_a_norm_g': 5.847186e-02, 'w_kv_b': 1.296965e-02, 'rel_bias': 1.736799e-02, 'sinks': 3.047354e-04, 'w_out': 1.999219e-02, 'norm2_g': 1.352390e-01, 'w_up': 5.660017e-02, 'conv_w': 5.706658e-02, 'conv_b': 5.864864e-02, 'w_down': 9.385283e-02, 'final_norm_g': 3.197066e+01}


def _to_microbatches(a, axis):
    t = _jnp.moveaxis(a, axis, 0)
    t = t.reshape((N_MICROBATCH, t.shape[0] // N_MICROBATCH) + t.shape[1:])
    return _jnp.moveaxis(t, 1, axis + 1)


def setup_inputs(seed: int = 0) -> dict:
    inp = _fwd_setup_inputs(seed)
    key = _jax.random.fold_in(_jax.random.key(seed), 7919)
    shape, _ = _output_shape()
    out = dict(inp)
    out["loss_target"] = _jax.random.normal(_jax.random.fold_in(key, 0), shape, _jnp.float32)
    for i, name in enumerate(TWIN_WEIGHTS):
        w = inp[name].astype(_jnp.float32)
        if MOMENT_SCALE is None:
            s = _jnp.sqrt(_jnp.mean(_jnp.square(w)) + 1e-30)
        else:
            s = MOMENT_SCALE[name]
        km, kv = _jax.random.split(_jax.random.fold_in(key, i + 1))
        out[name] = w
        out["m_" + name] = s * _jax.random.normal(km, w.shape, _jnp.float32)
        out["v_" + name] = (s * s) * _jax.random.uniform(kv, w.shape, _jnp.float32, 0.5, 1.5)
    if N_MICROBATCH > 1:
        for name, axis in PER_EXAMPLE_BATCH_AXIS.items():
            out[name] = _to_microbatches(out[name], axis)
    return {'x': out['x'], 'positions': out['positions'], 'norm1_g': out['norm1_g'], 'w_in': out['w_in'], 'q_a_norm_g': out['q_a_norm_g'], 'w_q_b': out['w_q_b'], 'kv_a_norm_g': out['kv_a_norm_g'], 'w_kv_b': out['w_kv_b'], 'rel_bias': out['rel_bias'], 'sinks': out['sinks'], 'w_out': out['w_out'], 'norm2_g': out['norm2_g'], 'w_up': out['w_up'], 'conv_w': out['conv_w'], 'conv_b': out['conv_b'], 'w_down': out['w_down'], 'final_norm_g': out['final_norm_g'], 'loss_target': out['loss_target'], 'm_norm1_g': out['m_norm1_g'], 'm_w_in': out['m_w_in'], 'm_q_a_norm_g': out['m_q_a_norm_g'], 'm_w_q_b': out['m_w_q_b'], 'm_kv_a_norm_g': out['m_kv_a_norm_g'], 'm_w_kv_b': out['m_w_kv_b'], 'm_rel_bias': out['m_rel_bias'], 'm_sinks': out['m_sinks'], 'm_w_out': out['m_w_out'], 'm_norm2_g': out['m_norm2_g'], 'm_w_up': out['m_w_up'], 'm_conv_w': out['m_conv_w'], 'm_conv_b': out['m_conv_b'], 'm_w_down': out['m_w_down'], 'm_final_norm_g': out['m_final_norm_g'], 'v_norm1_g': out['v_norm1_g'], 'v_w_in': out['v_w_in'], 'v_q_a_norm_g': out['v_q_a_norm_g'], 'v_w_q_b': out['v_w_q_b'], 'v_kv_a_norm_g': out['v_kv_a_norm_g'], 'v_w_kv_b': out['v_w_kv_b'], 'v_rel_bias': out['v_rel_bias'], 'v_sinks': out['v_sinks'], 'v_w_out': out['v_w_out'], 'v_norm2_g': out['v_norm2_g'], 'v_w_up': out['v_w_up'], 'v_conv_w': out['v_conv_w'], 'v_conv_b': out['v_conv_b'], 'v_w_down': out['v_w_down'], 'v_final_norm_g': out['v_final_norm_g']}


def _loss(weights, diff, rest, loss_target):
    with _jax.named_scope("forward"):
        args = {**rest, TWIN_DIFF_INPUT: diff, **{k: w.astype(_WEIGHT_DTYPES[k]) for k, w in weights.items()}}
        y = _forward(args)
    with _jax.named_scope("loss_head"):
        err = _jnp.square(y.astype(_jnp.float32) - loss_target)
        return 0.5 * _jnp.sum(_jnp.mean(err, axis=-1)) if err.ndim else 0.5 * err


def _adamw(w, g, m, v):
    m = ADAM_B1 * m + (1.0 - ADAM_B1) * g
    v = ADAM_B2 * v + (1.0 - ADAM_B2) * _jnp.square(g)
    m_hat = m / (1.0 - ADAM_B1 ** ADAM_STEP)
    v_hat = v / (1.0 - ADAM_B2 ** ADAM_STEP)
    delta = -ADAM_LR * (m_hat / (_jnp.sqrt(v_hat) + ADAM_EPS) + ADAM_WD * w)
    return delta, m, v


def reference(x, positions, norm1_g, w_in, q_a_norm_g, w_q_b, kv_a_norm_g, w_kv_b, rel_bias, sinks, w_out, norm2_g, w_up, conv_w, conv_b, w_down, final_norm_g, loss_target, m_norm1_g, m_w_in, m_q_a_norm_g, m_w_q_b, m_kv_a_norm_g, m_w_kv_b, m_rel_bias, m_sinks, m_w_out, m_norm2_g, m_w_up, m_conv_w, m_conv_b, m_w_down, m_final_norm_g, v_norm1_g, v_w_in, v_q_a_norm_g, v_w_q_b, v_kv_a_norm_g, v_w_kv_b, v_rel_bias, v_sinks, v_w_out, v_norm2_g, v_w_up, v_conv_w, v_conv_b, v_w_down, v_final_norm_g):
    given = dict(x=x, positions=positions, norm1_g=norm1_g, w_in=w_in, q_a_norm_g=q_a_norm_g, w_q_b=w_q_b, kv_a_norm_g=kv_a_norm_g, w_kv_b=w_kv_b, rel_bias=rel_bias, sinks=sinks, w_out=w_out, norm2_g=norm2_g, w_up=w_up, conv_w=conv_w, conv_b=conv_b, w_down=w_down, final_norm_g=final_norm_g, loss_target=loss_target, m_norm1_g=m_norm1_g, m_w_in=m_w_in, m_q_a_norm_g=m_q_a_norm_g, m_w_q_b=m_w_q_b, m_kv_a_norm_g=m_kv_a_norm_g, m_w_kv_b=m_w_kv_b, m_rel_bias=m_rel_bias, m_sinks=m_sinks, m_w_out=m_w_out, m_norm2_g=m_norm2_g, m_w_up=m_w_up, m_conv_w=m_conv_w, m_conv_b=m_conv_b, m_w_down=m_w_down, m_final_norm_g=m_final_norm_g, v_norm1_g=v_norm1_g, v_w_in=v_w_in, v_q_a_norm_g=v_q_a_norm_g, v_w_q_b=v_w_q_b, v_kv_a_norm_g=v_kv_a_norm_g, v_w_kv_b=v_w_kv_b, v_rel_bias=v_rel_bias, v_sinks=v_sinks, v_w_out=v_w_out, v_norm2_g=v_norm2_g, v_w_up=v_w_up, v_conv_w=v_conv_w, v_conv_b=v_conv_b, v_w_down=v_w_down, v_final_norm_g=v_final_norm_g)
    weights = {n: given[n] for n in TWIN_WEIGHTS}
    shared = {n: given[n] for n in SHARED_INPUTS}
    per_example = {n: given[n] for n in ['x']}
    grad_fn = _jax.value_and_grad(_loss, argnums=(0, 1))

    def one_microbatch(ex, loss_target):
        ex = dict(ex)
        diff = ex.pop(TWIN_DIFF_INPUT)
        return grad_fn(weights, diff, {**shared, **ex}, loss_target)

    if N_MICROBATCH == 1:
        loss, (grad_w, grad_x) = one_microbatch(per_example, given["loss_target"])
    else:
        def body(carry, xs):
            loss_sum, grad_sum = carry
            l_k, (gw_k, gx_k) = one_microbatch(xs[0], xs[1])
            with _jax.named_scope("update"):
                return (loss_sum + l_k, _jax.tree.map(_jnp.add, grad_sum, gw_k)), gx_k

        init = (_jnp.zeros((), _jnp.float32), _jax.tree.map(_jnp.zeros_like, weights))
        (loss, grad_w), grad_x = _jax.lax.scan(body, init, (per_example, given["loss_target"]))
    with _jax.named_scope("update"):
        delta_w, new_m, new_v = {}, {}, {}
        for n in TWIN_WEIGHTS:
            delta_w[n], new_m[n], new_v[n] = _adamw(weights[n], grad_w[n], given["m_" + n], given["v_" + n])
    return (loss, grad_x, *[grad_w[n] for n in TWIN_WEIGHTS], *[delta_w[n] for n in TWIN_WEIGHTS],
            *[new_m[n] for n in TWIN_WEIGHTS], *[new_v[n] for n in TWIN_WEIGHTS])
```

```python
import math

import jax
import jax.numpy as jnp
from jax import lax
from jax.experimental import pallas as pl
from jax.experimental.pallas import tpu as pltpu

F32 = jnp.float32
BF16 = jnp.bfloat16

N_DEV = 8
D_MODEL = 1024
EPS = 1e-6
H_A, QK_NOPE, QK_ROPE, V_DIM, Q_LORA, KV_LORA = 8, 128, 64, 128, 256, 128
QK_HEAD = QK_NOPE + QK_ROPE
ROPE_THETA = 10000.0
H_B, KV_B, GROUP, HD_B, WINDOW, Q_BLOCK = 16, 4, 4, 64, 128, 128
SPAN = Q_BLOCK + 2 * WINDOW
NUM_BUCKETS, MAX_DISTANCE = 32, 128
D_FF = 2816
ADAM_LR, ADAM_B1, ADAM_B2, ADAM_EPS, ADAM_WD, ADAM_STEP = 0.001, 0.9, 0.999, 1e-08, 0.01, 10

W_IN_SIZES = (Q_LORA, KV_LORA + QK_ROPE, H_B * HD_B, KV_B * HD_B, KV_B * HD_B, D_MODEL, D_MODEL)
W_IN_COLS = sum(W_IN_SIZES)
PROJ_P = 4096

VMEM_LIMIT = 56 * 1024 * 1024

NN = (((1,), (0,)), ((), ()))
NT = (((1,), (1,)), ((), ()))
TN = (((0,), (0,)), ((), ()))


def _pcall(body, *, dims=None, **kw):
    params = pltpu.CompilerParams(dimension_semantics=dims, vmem_limit_bytes=VMEM_LIMIT)
    return pl.pallas_call(body, compiler_params=params, **kw)


def _dot(a, b, dn):
    return lax.dot_general(a, b, dn, preferred_element_type=F32)


def _tile(n, target):
    best = None
    for t in range(128, min(n, target) + 1, 128):
        if n % t == 0:
            best = t
    return n if best is None else best


def _matmul(a, b, mode, out_dtype, name, residual=None, tm=512, tn=512):
    if mode == "nn":
        (M, K), N = a.shape, b.shape[1]
    elif mode == "nt":
        (M, K), N = a.shape, b.shape[0]
    else:
        (K, M), N = a.shape, b.shape[1]
    tm, tn = _tile(M, tm), _tile(N, tn)
    a_spec = pl.BlockSpec((K, tm), lambda i, j: (0, i)) if mode == "tn" else pl.BlockSpec((tm, K), lambda i, j: (i, 0))
    b_spec = pl.BlockSpec((tn, K), lambda i, j: (j, 0)) if mode == "nt" else pl.BlockSpec((K, tn), lambda i, j: (0, j))
    o_spec = pl.BlockSpec((tm, tn), lambda i, j: (i, j))
    in_specs, args = [a_spec, b_spec], [a, b]
    if residual is not None:
        in_specs.append(o_spec)
        args.append(residual)
    scratch = [pltpu.VMEM((tm, K), a.dtype)] if mode == "tn" else []

    def body(*refs):
        a_ref, b_ref = refs[0], refs[1]
        o_ref = refs[len(args)]
        if mode == "tn":
            at_ref = refs[len(args) + 1]

            @pl.when(pl.program_id(1) == 0)
            def _():
                at_ref[...] = a_ref[...].T

            acc = _dot(at_ref[...], b_ref[...], NN)
        else:
            acc = _dot(a_ref[...], b_ref[...], NT if mode == "nt" else NN)
        if residual is not None:
            acc = acc + refs[2][...]
        o_ref[...] = acc.astype(out_dtype)

    return _pcall(body, name=name, grid=(M // tm, N // tn), in_specs=in_specs, out_specs=o_spec,
                  out_shape=jax.ShapeDtypeStruct((M, N), out_dtype), scratch_shapes=scratch,
                  dims=("parallel", "arbitrary"))(*args)


def _rows(arr, width=None, col=0):
    width = arr.shape[1] if width is None else width
    return (arr, lambda tm: pl.BlockSpec((tm, width), lambda i: (i, col)))


def _heads(arr):
    return (arr, lambda tm: pl.BlockSpec((arr.shape[0], tm, arr.shape[2]), lambda i: (0, i, 0)))


def _whole(arr):
    nd = arr.ndim
    return (arr, lambda tm: pl.BlockSpec(arr.shape, lambda i: (0,) * nd))


def _rowwise(fn, name, n_rows, tm, ins, outs):
    tm = min(tm, n_rows)
    assert n_rows % tm == 0
    in_specs = [mk(tm) for _, mk in ins]
    out_specs, out_shape, is_acc = [], [], []
    for o in outs:
        if o[0] == "rows":
            out_specs.append(pl.BlockSpec((tm, o[1]), lambda i: (i, 0)))
            out_shape.append(jax.ShapeDtypeStruct((n_rows, o[1]), o[2]))
        elif o[0] == "heads":
            out_specs.append(pl.BlockSpec((o[1], tm, o[2]), lambda i: (0, i, 0)))
            out_shape.append(jax.ShapeDtypeStruct((o[1], n_rows, o[2]), o[3]))
        else:
            out_specs.append(pl.BlockSpec((o[1], o[2]), lambda i: (0, 0)))
            out_shape.append(jax.ShapeDtypeStruct((o[1], o[2]), F32))
        is_acc.append(o[0] == "acc")
    n_in = len(ins)

    def body(*refs):
        vals = fn(*[r[...] for r in refs[:n_in]])
        for r, v, acc in zip(refs[n_in:], vals, is_acc):
            if acc:
                @pl.when(pl.program_id(0) == 0)
                def _():
                    r[...] = jnp.zeros_like(r)

                r[...] += v
            else:
                r[...] = v.astype(r.dtype)

    return _pcall(body, name=name, grid=(n_rows // tm,), in_specs=in_specs, out_specs=out_specs,
                  out_shape=out_shape, dims=("arbitrary",))(*[a for a, _ in ins])


def _rms(x, g):
    r = lax.rsqrt(jnp.mean(x * x, axis=-1, keepdims=True) + EPS)
    return x * r * g


def _rms_bwd(dy, x, g):
    r = lax.rsqrt(jnp.mean(x * x, axis=-1, keepdims=True) + EPS)
    xhat = x * r
    dxhat = dy * g
    dx = r * (dxhat - xhat * jnp.mean(dxhat * xhat, axis=-1, keepdims=True))
    return dx, jnp.sum(dy * xhat, axis=0, keepdims=True)


def _rope(x1, x2, cos, sin):
    return x1 * cos - x2 * sin, x2 * cos + x1 * sin


def _rope_bwd(d1, d2, cos, sin):
    return d1 * cos + d2 * sin, d2 * cos - d1 * sin


def _sigmoid(x):
    return 1.0 / (1.0 + jnp.exp(-x))


def _mla_fwd(q_full, k_full, kv, S):
    tq = min(256, S)
    scale = 1.0 / math.sqrt(QK_HEAD)

    def body(q_ref, k_ref, v_ref, o_ref, lse_ref):
        s = _dot(q_ref[0], k_ref[0], NT) * scale
        m = jnp.max(s, axis=-1, keepdims=True)
        p = jnp.exp(s - m)
        l = jnp.sum(p, axis=-1, keepdims=True)
        o_ref[...] = _dot(p.astype(BF16), v_ref[...], NN) / l
        lse_ref[0] = m + jnp.log(l)

    return _pcall(
        body, name="mla_fwd", grid=(H_A, S // tq),
        in_specs=[pl.BlockSpec((1, tq, QK_HEAD), lambda h, i: (h, i, 0)),
                  pl.BlockSpec((1, S, QK_HEAD), lambda h, i: (h, 0, 0)),
                  pl.BlockSpec((S, V_DIM), lambda h, i: (0, 2 * h + 1))],
        out_specs=[pl.BlockSpec((tq, V_DIM), lambda h, i: (i, h)),
                   pl.BlockSpec((1, tq, 1), lambda h, i: (h, i, 0))],
        out_shape=[jax.ShapeDtypeStruct((S, H_A * V_DIM), F32), jax.ShapeDtypeStruct((H_A, S, 1), F32)],
        dims=("parallel", "parallel"))(q_full, k_full, kv)


def _mla_bwd(q_full, k_full, kv, do_a, o_a, lse, S):
    tq = min(256, S)
    kc = min(1024, S)
    scale = 1.0 / math.sqrt(QK_HEAD)

    def body(q_ref, k_ref, v_ref, do_ref, o_ref, lse_ref, dq_ref, dk_ref, dv_ref):
        @pl.when(pl.program_id(1) == 0)
        def _():
            dk_ref[...] = jnp.zeros_like(dk_ref)
            dv_ref[...] = jnp.zeros_like(dv_ref)

        q = q_ref[0]
        do = do_ref[...]
        lse_q = lse_ref[0]
        delta = jnp.sum(do.astype(F32) * o_ref[...], axis=-1, keepdims=True)
        dq = jnp.zeros((tq, QK_HEAD), F32)
        for c in range(S // kc):
            k = k_ref[0, c * kc:(c + 1) * kc, :]
            v = v_ref[c * kc:(c + 1) * kc, :]
            p = jnp.exp(_dot(q, k, NT) * scale - lse_q)
            ds = (p * (_dot(do, v, NT) - delta) * scale).astype(BF16)
            dq = dq + _dot(ds, k, NN)
            dk_ref[0, c * kc:(c + 1) * kc, :] += _dot(ds, q, TN)
            dv_ref[0, c * kc:(c + 1) * kc, :] += _dot(p.astype(BF16), do, TN)
        dq_ref[0] = dq

    return _pcall(
        body, name="mla_bwd", grid=(H_A, S // tq),
        in_specs=[pl.BlockSpec((1, tq, QK_HEAD), lambda h, i: (h, i, 0)),
                  pl.BlockSpec((1, S, QK_HEAD), lambda h, i: (h, 0, 0)),
                  pl.BlockSpec((S, V_DIM), lambda h, i: (0, 2 * h + 1)),
                  pl.BlockSpec((tq, V_DIM), lambda h, i: (i, h)),
                  pl.BlockSpec((tq, V_DIM), lambda h, i: (i, h)),
                  pl.BlockSpec((1, tq, 1), lambda h, i: (h, i, 0))],
        out_specs=[pl.BlockSpec((1, tq, QK_HEAD), lambda h, i: (h, i, 0)),
                   pl.BlockSpec((1, S, QK_HEAD), lambda h, i: (h, 0, 0)),
                   pl.BlockSpec((1, S, V_DIM), lambda h, i: (h, 0, 0))],
        out_shape=[jax.ShapeDtypeStruct((H_A, S, QK_HEAD), F32), jax.ShapeDtypeStruct((H_A, S, QK_HEAD), F32),
                   jax.ShapeDtypeStruct((H_A, S, V_DIM), F32)],
        dims=("parallel", "arbitrary"))(q_full, k_full, kv, do_a, o_a, lse)


def _win_scores(q4, kg, bias4, sink4, edge_ok):
    s = _dot(q4, kg, NT) * (1.0 / math.sqrt(HD_B)) + bias4
    s = jnp.where(edge_ok, s, -1e30)
    m = jnp.maximum(jnp.max(s, axis=-1, keepdims=True), sink4)
    p = jnp.exp(s - m)
    e_sink = jnp.exp(sink4 - m)
    inv_l = 1.0 / (jnp.sum(p, axis=-1, keepdims=True) + e_sink)
    return p * inv_l, e_sink * inv_l


def _win_common(n, n_blk, q_ref, k_refs, v_refs, sink_ref):
    col = lax.broadcasted_iota(jnp.int32, (1, SPAN), 1)
    edge_ok = jnp.logical_not(((n == 0) & (col < WINDOW)) | ((n == n_blk - 1) & (col >= SPAN - WINDOW)))
    k = jnp.concatenate([r[...] for r in k_refs], axis=0)
    v = jnp.concatenate([r[...] for r in v_refs], axis=0)
    q = q_ref[...].astype(BF16)
    sink = sink_ref[...]
    return edge_ok, q, k, v, sink


def _stack_heads(x, g):
    return jnp.concatenate([x[:, HD_B * (GROUP * g + j):HD_B * (GROUP * g + j + 1)] for j in range(GROUP)], axis=0)


def _stack_sinks(sink, g):
    return jnp.concatenate([jnp.broadcast_to(sink[GROUP * g + j:GROUP * g + j + 1, :], (Q_BLOCK, 1)) for j in range(GROUP)], axis=0)


def _win_specs(S):
    qspec = pl.BlockSpec((Q_BLOCK, H_B * HD_B), lambda n: (n, 0))
    kspecs = [pl.BlockSpec((Q_BLOCK, KV_B * HD_B), lambda n, d=d: (n + d, 0)) for d in range(3)]
    bias_spec = pl.BlockSpec((H_B, Q_BLOCK, SPAN), lambda n: (0, 0, 0))
    sink_spec = pl.BlockSpec((H_B, 1), lambda n: (0, 0))
    return qspec, kspecs, bias_spec, sink_spec


def _win_fwd(proj, kp, vp, bias, sinks_col, S):
    n_blk = S // Q_BLOCK
    qspec, kspecs, bias_spec, sink_spec = _win_specs(S)

    def body(q_ref, k0, k1, k2, v0, v1, v2, bias_ref, sink_ref, o_ref):
        n = pl.program_id(0)
        edge_ok, q, k, v, sink = _win_common(n, n_blk, q_ref, (k0, k1, k2), (v0, v1, v2), sink_ref)
        outs = []
        for g in range(KV_B):
            kg, vg = k[:, HD_B * g:HD_B * (g + 1)], v[:, HD_B * g:HD_B * (g + 1)]
            bias4 = bias_ref[GROUP * g:GROUP * (g + 1)].reshape(GROUP * Q_BLOCK, SPAN)
            p, _ = _win_scores(_stack_heads(q, g), kg, bias4, _stack_sinks(sink, g), edge_ok)
            o4 = _dot(p.astype(BF16), vg, NN)
            outs += [o4[Q_BLOCK * j:Q_BLOCK * (j + 1), :] for j in range(GROUP)]
        o_ref[...] = jnp.concatenate(outs, axis=1)

    return _pcall(body, name="win_fwd", grid=(n_blk,),
                  in_specs=[qspec, *kspecs, *kspecs, bias_spec, sink_spec],
                  out_specs=pl.BlockSpec((Q_BLOCK, H_B * HD_B), lambda n: (n, 0)),
                  out_shape=jax.ShapeDtypeStruct((S, H_B * HD_B), F32),
                  dims=("parallel",))(proj, kp, kp, kp, vp, vp, vp, bias, sinks_col)


def _win_bwd(proj, kp, vp, bias, sinks_col, do_b, S):
    n_blk = S // Q_BLOCK
    qspec, kspecs, bias_spec, sink_spec = _win_specs(S)
    scale = 1.0 / math.sqrt(HD_B)

    def body(q_ref, k0, k1, k2, v0, v1, v2, bias_ref, sink_ref, do_ref, dq_ref, dk_ref, dv_ref, dbias_ref, dsink_ref):
        n = pl.program_id(0)

        @pl.when(n == 0)
        def _():
            dk_ref[...] = jnp.zeros_like(dk_ref)
            dv_ref[...] = jnp.zeros_like(dv_ref)
            dbias_ref[...] = jnp.zeros_like(dbias_ref)
            dsink_ref[...] = jnp.zeros_like(dsink_ref)

        edge_ok, q, k, v, sink = _win_common(n, n_blk, q_ref, (k0, k1, k2), (v0, v1, v2), sink_ref)
        do = do_ref[...]
        dqs, dks, dvs, dsinks = [], [], [], []
        for g in range(KV_B):
            kg, vg = k[:, HD_B * g:HD_B * (g + 1)], v[:, HD_B * g:HD_B * (g + 1)]
            bias4 = bias_ref[GROUP * g:GROUP * (g + 1)].reshape(GROUP * Q_BLOCK, SPAN)
            q4, do4 = _stack_heads(q, g), _stack_heads(do, g)
            p, p_sink = _win_scores(q4, kg, bias4, _stack_sinks(sink, g), edge_ok)
            dp = _dot(do4, vg, NT)
            delta = jnp.sum(p * dp, axis=-1, keepdims=True)
            ds = p * (dp - delta)
            dbias_ref[GROUP * g:GROUP * (g + 1)] += ds.reshape(GROUP, Q_BLOCK, SPAN)
            d_sink = -p_sink * delta
            dsinks += [jnp.sum(d_sink[Q_BLOCK * j:Q_BLOCK * (j + 1), :], axis=0, keepdims=True) for j in range(GROUP)]
            dsb = (ds * scale).astype(BF16)
            dq4 = _dot(dsb, kg, NN)
            dqs += [dq4[Q_BLOCK * j:Q_BLOCK * (j + 1), :] for j in range(GROUP)]
            dks.append(_dot(dsb, q4, TN))
            dvs.append(_dot(p.astype(BF16), do4, TN))
        dq_ref[...] = jnp.concatenate(dqs, axis=1).astype(BF16)
        rows = pl.ds(pl.multiple_of(n * Q_BLOCK, Q_BLOCK), SPAN)
        dk_ref[rows, :] += jnp.concatenate(dks, axis=1)
        dv_ref[rows, :] += jnp.concatenate(dvs, axis=1)
        dsink_ref[...] += jnp.broadcast_to(jnp.concatenate(dsinks, axis=0), (H_B, 128))

    whole = lambda shape: pl.BlockSpec(shape, lambda n: (0,) * len(shape))
    return _pcall(
        body, name="win_bwd", grid=(n_blk,),
        in_specs=[qspec, *kspecs, *kspecs, bias_spec, sink_spec, pl.BlockSpec((Q_BLOCK, H_B * HD_B), lambda n: (n, 0))],
        out_specs=[pl.BlockSpec((Q_BLOCK, H_B * HD_B), lambda n: (n, 0)), whole((S + 2 * WINDOW, KV_B * HD_B)),
                   whole((S + 2 * WINDOW, KV_B * HD_B)), whole((H_B, Q_BLOCK, SPAN)), whole((H_B, 128))],
        out_shape=[jax.ShapeDtypeStruct((S, H_B * HD_B), BF16), jax.ShapeDtypeStruct((S + 2 * WINDOW, KV_B * HD_B), F32),
                   jax.ShapeDtypeStruct((S + 2 * WINDOW, KV_B * HD_B), F32), jax.ShapeDtypeStruct((H_B, Q_BLOCK, SPAN), F32),
                   jax.ShapeDtypeStruct((H_B, 128), F32)],
        dims=("arbitrary",))(proj, kp, kp, kp, vp, vp, vp, bias, sinks_col, do_b)


def _bias_table(rel_bias_t, onehot_t, in_band):
    def body(rb_ref, oh_ref, band_ref, o_ref):
        t = lax.dot_general(rb_ref[...], oh_ref[...], NN, preferred_element_type=F32, precision=lax.Precision.HIGHEST)
        o_ref[...] = jnp.where(band_ref[...] > 0.5, t, -1e30)

    n = onehot_t.shape[1]
    tn = _tile(n, 8192)
    return _pcall(body, name="bias_table", grid=(n // tn,),
                  in_specs=[pl.BlockSpec((H_B, NUM_BUCKETS), lambda j: (0, 0)), pl.BlockSpec((NUM_BUCKETS, tn), lambda j: (0, j)),
                            pl.BlockSpec((1, tn), lambda j: (0, j))],
                  out_specs=pl.BlockSpec((H_B, tn), lambda j: (0, j)),
                  out_shape=jax.ShapeDtypeStruct((H_B, n), F32), dims=("parallel",))(rel_bias_t, onehot_t, in_band)


def _bias_table_bwd(dbias, onehot_t):
    n = onehot_t.shape[1]
    tk = _tile(n, 8192)

    def body(d_ref, oh_ref, o_ref):
        @pl.when(pl.program_id(0) == 0)
        def _():
            o_ref[...] = jnp.zeros_like(o_ref)

        o_ref[...] += lax.dot_general(d_ref[...], oh_ref[...], NT, preferred_element_type=F32, precision=lax.Precision.HIGHEST)

    return _pcall(body, name="bias_table_bwd", grid=(n // tk,),
                  in_specs=[pl.BlockSpec((H_B, tk), lambda j: (0, j)), pl.BlockSpec((NUM_BUCKETS, tk), lambda j: (0, j))],
                  out_specs=pl.BlockSpec((H_B, NUM_BUCKETS), lambda j: (0, 0)),
                  out_shape=jax.ShapeDtypeStruct((H_B, NUM_BUCKETS), F32), dims=("arbitrary",))(dbias, onehot_t)


def _shift_down(u):
    row = lax.broadcasted_iota(jnp.int32, u.shape, 0)
    return jnp.where(row == 0, 0.0, pltpu.roll(u, 1, axis=0))


def _shift_up(u):
    n = u.shape[0]
    row = lax.broadcasted_iota(jnp.int32, u.shape, 0)
    return jnp.where(row == n - 1, 0.0, pltpu.roll(u, n - 1, axis=0))


def _conv(u, w, b):
    return _shift_down(u) * w[0:1, :] + u * w[1:2, :] + _shift_up(u) * w[2:3, :] + b


CONV_STRIP = 128


def _conv_specs(S):
    nj = D_FF // CONV_STRIP
    ug = pl.BlockSpec((S, CONV_STRIP), lambda j: (0, j))
    uv = pl.BlockSpec((S, CONV_STRIP), lambda j: (0, j + nj))
    wg = pl.BlockSpec((3, CONV_STRIP), lambda j: (0, j))
    wv = pl.BlockSpec((3, CONV_STRIP), lambda j: (0, j + nj))
    bg = pl.BlockSpec((1, CONV_STRIP), lambda j: (0, j))
    bv = pl.BlockSpec((1, CONV_STRIP), lambda j: (0, j + nj))
    return nj, ug, uv, wg, wv, bg, bv


def _conv_gate_fwd(u, conv_w, conv_b, S):
    nj, ug, uv, wg, wv, bg, bv = _conv_specs(S)

    def body(ug_ref, uv_ref, wg_ref, wv_ref, bg_ref, bv_ref, a_ref):
        g = _conv(ug_ref[...], wg_ref[...], bg_ref[...])
        val = _conv(uv_ref[...], wv_ref[...], bv_ref[...])
        a_ref[...] = (g * _sigmoid(g) * val).astype(BF16)

    return _pcall(body, name="conv_gate_fwd", grid=(nj,), in_specs=[ug, uv, wg, wv, bg, bv],
                  out_specs=pl.BlockSpec((S, CONV_STRIP), lambda j: (0, j)),
                  out_shape=jax.ShapeDtypeStruct((S, D_FF), BF16), dims=("parallel",))(u, u, conv_w, conv_w, conv_b, conv_b)


def _conv_gate_bwd(u, conv_w, conv_b, da, S):
    nj, ug, uv, wg, wv, bg, bv = _conv_specs(S)

    def conv_bwd(duc, u_in, w):
        du = _shift_up(duc) * w[0:1, :] + duc * w[1:2, :] + _shift_down(duc) * w[2:3, :]
        dw = jnp.concatenate([jnp.sum(duc * _shift_down(u_in), axis=0, keepdims=True),
                              jnp.sum(duc * u_in, axis=0, keepdims=True),
                              jnp.sum(duc * _shift_up(u_in), axis=0, keepdims=True)], axis=0)
        return du, dw, jnp.sum(duc, axis=0, keepdims=True)

    def body(ug_ref, uv_ref, wg_ref, wv_ref, bg_ref, bv_ref, da_ref, dug_ref, duv_ref, dwg_ref, dwv_ref, dbg_ref, dbv_ref):
        u_g, u_v, w_g, w_v = ug_ref[...], uv_ref[...], wg_ref[...], wv_ref[...]
        g = _conv(u_g, w_g, bg_ref[...])
        val = _conv(u_v, w_v, bv_ref[...])
        da = da_ref[...]
        sg = _sigmoid(g)
        dval = da * (g * sg)
        dg = da * val * (sg * (1.0 + g * (1.0 - sg)))
        du_g, dw_g, db_g = conv_bwd(dg, u_g, w_g)
        du_v, dw_v, db_v = conv_bwd(dval, u_v, w_v)
        dug_ref[...] = du_g.astype(BF16)
        duv_ref[...] = du_v.astype(BF16)
        dwg_ref[...] = dw_g
        dwv_ref[...] = dw_v
        dbg_ref[...] = db_g
        dbv_ref[...] = db_v

    strip = lambda r, dt: (pl.BlockSpec((r, CONV_STRIP), lambda j: (0, j)), jax.ShapeDtypeStruct((r, D_FF), dt))
    outs = [strip(S, BF16), strip(S, BF16), strip(3, F32), strip(3, F32), strip(1, F32), strip(1, F32)]
    du_g, du_v, dw_g, dw_v, db_g, db_v = _pcall(
        body, name="conv_gate_bwd", grid=(nj,),
        in_specs=[ug, uv, wg, wv, bg, bv, pl.BlockSpec((S, CONV_STRIP), lambda j: (0, j))],
        out_specs=[o[0] for o in outs], out_shape=[o[1] for o in outs],
        dims=("parallel",))(u, u, conv_w, conv_w, conv_b, conv_b, da)
    return du_g, du_v, jnp.concatenate([dw_g, dw_v], axis=1), jnp.concatenate([db_g, db_v], axis=1)


MESH = pl.DeviceIdType.MESH
ANY = pl.BlockSpec(memory_space=pl.ANY)


def _place():
    return lax.axis_index("x"), lax.axis_index("y"), lax.axis_index("c")


def _all_gather(shards):
    n_arr = len(shards)

    def body(*refs):
        ins, outs = refs[:n_arr], refs[n_arr:2 * n_arr]
        send_sems, recv_sems, local_sems = refs[2 * n_arr:]
        x, y, c = _place()
        me, sibling = (x, y, c), (x, y, 1 - c)
        chips = [(1 - x, y), (x, 1 - y), (1 - x, 1 - y)]

        def slot(a, p):
            return outs[a].at[4 * p[0] + 2 * p[1] + p[2]]

        def copy(a, k, block, to, src=None):
            return pltpu.make_async_remote_copy(
                src_ref=slot(a, block) if src is None else src, dst_ref=slot(a, block),
                send_sem=send_sems.at[a, k], recv_sem=recv_sems.at[a, k], device_id=to, device_id_type=MESH)

        mine = [pltpu.make_async_copy(ins[a], slot(a, me), local_sems.at[a]) for a in range(n_arr)]
        for cp in mine:
            cp.start()
        first = []
        for a in range(n_arr):
            first.append(copy(a, 0, me, sibling, src=ins[a]))
            first += [copy(a, 1 + j, me, (*chip, c), src=ins[a]) for j, chip in enumerate(chips)]
        for cp in first:
            cp.start()
        passed = []
        for j, chip in enumerate(chips):
            for a in range(n_arr):
                copy(a, 1 + j, (*chip, c), me).wait_recv()
                cp = copy(a, 4 + j, (*chip, c), sibling)
                cp.start()
                passed.append(cp)
        for a in range(n_arr):
            copy(a, 0, sibling, me).wait_recv()
            for j, chip in enumerate(chips):
                copy(a, 4 + j, (*chip, 1 - c), me).wait_recv()
        for cp in first + passed:
            cp.wait_send()
        for cp in mine:
            cp.wait()

    return pl.pallas_call(
        body, name="all_gather_weights",
        in_specs=[ANY] * n_arr, out_specs=[ANY] * n_arr,
        out_shape=[jax.ShapeDtypeStruct((N_DEV, *s.shape), s.dtype) for s in shards],
        scratch_shapes=[pltpu.SemaphoreType.DMA((n_arr, 7)), pltpu.SemaphoreType.DMA((n_arr, 7)), pltpu.SemaphoreType.DMA((n_arr,))],
    )(*shards)


def _exchange(stacked, replicated):
    arrs = list(stacked) + list(replicated)
    n_st, n_arr = len(stacked), len(arrs)

    def body(*refs):
        ins, outs = refs[:n_arr], refs[n_arr:2 * n_arr]
        send_sems, recv_sems, local_sems = refs[2 * n_arr:]
        x, y, c = _place()
        me = 4 * x + 2 * y + c

        def src(a, idx):
            return ins[a].at[idx] if a < n_st else ins[a]

        mine = [pltpu.make_async_copy(src(a, me), outs[a].at[me], local_sems.at[a]) for a in range(n_arr)]
        for cp in mine:
            cp.start()
        copies = []
        for k in range(1, N_DEV):
            px, py, pc = x ^ (k >> 2), y ^ ((k >> 1) & 1), c ^ (k & 1)
            peer = 4 * px + 2 * py + pc
            for a in range(n_arr):
                cp = pltpu.make_async_remote_copy(
                    src_ref=src(a, peer), dst_ref=outs[a].at[me], send_sem=send_sems.at[a, k - 1],
                    recv_sem=recv_sems.at[a, k - 1], device_id=(px, py, pc), device_id_type=MESH)
                cp.start()
                copies.append((cp, a, k, peer))
        for cp, a, k, peer in copies:
            pltpu.make_async_remote_copy(
                src_ref=src(a, peer), dst_ref=outs[a].at[peer], send_sem=send_sems.at[a, k - 1],
                recv_sem=recv_sems.at[a, k - 1], device_id=(x, y, c), device_id_type=MESH).wait_recv()
        for cp, a, k, peer in copies:
            cp.wait_send()
        for cp in mine:
            cp.wait()

    out_shape = [jax.ShapeDtypeStruct(s.shape, s.dtype) for s in stacked]
    out_shape += [jax.ShapeDtypeStruct((N_DEV, *r.shape), r.dtype) for r in replicated]
    return pl.pallas_call(
        body, name="exchange_grads",
        in_specs=[ANY] * n_arr, out_specs=[ANY] * n_arr, out_shape=out_shape,
        scratch_shapes=[pltpu.SemaphoreType.DMA((n_arr, 7)), pltpu.SemaphoreType.DMA((n_arr, 7)), pltpu.SemaphoreType.DMA((n_arr,))],
    )(*arrs)


def _adamw(parts, w, m, v, name):
    R, C = w.shape
    tr = R if (R <= 512 or R % 256) else 256

    def body(p_ref, w_ref, m_ref, v_ref, g_ref, d_ref, nm_ref, nv_ref):
        g = p_ref[0].astype(F32)
        for s in range(1, N_DEV):
            g = g + p_ref[s].astype(F32)
        m2 = ADAM_B1 * m_ref[...] + (1.0 - ADAM_B1) * g
        v2 = ADAM_B2 * v_ref[...] + (1.0 - ADAM_B2) * (g * g)
        m_hat = m2 / (1.0 - ADAM_B1 ** ADAM_STEP)
        v_hat = v2 / (1.0 - ADAM_B2 ** ADAM_STEP)
        g_ref[...] = g
        d_ref[...] = -ADAM_LR * (m_hat / (jnp.sqrt(v_hat) + ADAM_EPS) + ADAM_WD * w_ref[...])
        nm_ref[...] = m2
        nv_ref[...] = v2

    blk = pl.BlockSpec((tr, C), lambda i: (i, 0))
    return _pcall(body, name=name, grid=(R // tr,),
                  in_specs=[pl.BlockSpec((N_DEV, tr, C), lambda i: (0, i, 0)), blk, blk, blk],
                  out_specs=[blk] * 4, out_shape=[jax.ShapeDtypeStruct((R, C), F32)] * 4,
                  dims=("parallel",))(parts, w, m, v)


def _t5_bucket(rel):
    nb = NUM_BUCKETS // 2
    max_exact = nb // 2
    base = (rel > 0).astype(jnp.int32) * nb
    n = jnp.abs(rel)
    nf = jnp.maximum(n, 1).astype(jnp.float32)
    large = max_exact + (jnp.log(nf / max_exact) / math.log(MAX_DISTANCE / max_exact) * (nb - max_exact)).astype(jnp.int32)
    large = jnp.minimum(large, nb - 1)
    return base + jnp.where(n < max_exact, n, large)


def _unstack_cols(g):
    return jnp.transpose(g, (1, 0, 2)).reshape(g.shape[1], N_DEV * g.shape[2])


def _stack_cols(w):
    R = w.shape[0]
    return jnp.transpose(w.reshape(R, N_DEV, w.shape[1] // N_DEV), (1, 0, 2))


def kernel(x, positions, norm1_g, w_in, q_a_norm_g, w_q_b, kv_a_norm_g, w_kv_b, rel_bias, sinks, w_out, norm2_g, w_up, conv_w, conv_b, w_down, final_norm_g, loss_target, m_norm1_g, m_w_in, m_q_a_norm_g, m_w_q_b, m_kv_a_norm_g, m_w_kv_b, m_rel_bias, m_sinks, m_w_out, m_norm2_g, m_w_up, m_conv_w, m_conv_b, m_w_down, m_final_norm_g, v_norm1_g, v_w_in, v_q_a_norm_g, v_w_q_b, v_kv_a_norm_g, v_w_kv_b, v_rel_bias, v_sinks, v_w_out, v_norm2_g, v_w_up, v_conv_w, v_conv_b, v_w_down, v_final_norm_g):
    S = x.shape[1]
    x = x[0]
    target = loss_target[0]
    TM = 256

    g_in, g_qb, g_kvb, g_out, g_up, g_down, g_cw = _all_gather(
        [w_in[0].astype(BF16), w_q_b[0].astype(BF16), w_kv_b[0].astype(BF16), w_out[0].astype(BF16),
         w_up[0].astype(BF16), w_down[0].astype(BF16), conv_w[0]])
    wi = _unstack_cols(g_in)
    c0, c1, c2, c3, c4, c5 = (sum(W_IN_SIZES[:i + 1]) for i in range(6))
    w_in_p = jnp.concatenate([wi[:, c1:c2], wi[:, c4:c5], wi[:, c5:], wi[:, :c0], wi[:, c2:c3], wi[:, c3:c4],
                              wi[:, c0:c0 + KV_LORA], wi[:, c0 + KV_LORA:c1], jnp.zeros((D_MODEL, 64), BF16)], axis=1)
    wq = _unstack_cols(g_qb).reshape(Q_LORA, H_A, QK_HEAD)
    w_qb_p = jnp.concatenate([wq[:, :, :QK_NOPE].reshape(Q_LORA, H_A * QK_NOPE), wq[:, :, QK_NOPE:].reshape(Q_LORA, H_A * QK_ROPE)], axis=1)
    w_kvb = _unstack_cols(g_kvb)
    w_out_f = g_out.reshape(D_MODEL, D_MODEL)
    w_up_f = _unstack_cols(g_up)
    w_down_f = g_down.reshape(D_FF, D_MODEL)
    conv_w_f = _unstack_cols(g_cw)

    half = QK_ROPE // 2
    inv_freq = ROPE_THETA ** (-jnp.arange(half, dtype=F32) / half)
    ang = positions.astype(F32)[:, None] * inv_freq[None, :]
    cos, sin = jnp.cos(ang), jnp.sin(ang)
    qa = jnp.arange(Q_BLOCK, dtype=jnp.int32)[:, None]
    kc = jnp.arange(SPAN, dtype=jnp.int32)[None, :]
    rel = kc - WINDOW - qa
    in_band = (jnp.abs(rel) <= WINDOW).astype(F32).reshape(1, Q_BLOCK * SPAN)
    onehot_t = (_t5_bucket(rel).reshape(1, Q_BLOCK * SPAN) == jnp.arange(NUM_BUCKETS, dtype=jnp.int32)[:, None]).astype(F32)
    bias = _bias_table(rel_bias.T, onehot_t, in_band).reshape(H_B, Q_BLOCK, SPAN)
    sinks_col = sinks.reshape(H_B, 1)

    (h1,) = _rowwise(lambda a, g: (_rms(a, g),), "norm1", S, TM, [_rows(x), _whole(norm1_g)], [("rows", D_MODEL, BF16)])
    proj = _matmul(h1, w_in_p, "nn", F32, "proj")

    def lat_fn(qlat, ckv, kr, gq, gkv, cs, sn):
        r1, r2 = _rope(kr[:, :half], kr[:, half:QK_ROPE], cs, sn)
        return _rms(qlat, gq), _rms(ckv, gkv), jnp.concatenate([r1, r2], axis=1)

    qn, ckvn, k_rope = _rowwise(lat_fn, "latents", S, TM,
                                [_rows(proj, 256, 12), _rows(proj, 128, 30), _rows(proj, 128, 31), _whole(q_a_norm_g), _whole(kv_a_norm_g),
                                 _rows(cos), _rows(sin)],
                                [("rows", Q_LORA, BF16), ("rows", KV_LORA, BF16), ("rows", QK_ROPE, BF16)])
    q_p = _matmul(qn, w_qb_p, "nn", F32, "q_up")
    kv = _matmul(ckvn, w_kvb, "nn", BF16, "kv_up")

    def q_heads_fn(q, cs, sn):
        outs = []
        for h in range(H_A):
            o = H_A * QK_NOPE + QK_ROPE * h
            r1, r2 = _rope(q[:, o:o + half], q[:, o + half:o + QK_ROPE], cs, sn)
            outs.append(jnp.concatenate([q[:, QK_NOPE * h:QK_NOPE * (h + 1)], r1, r2], axis=1)[None])
        return (jnp.concatenate(outs, axis=0),)

    (q_full,) = _rowwise(q_heads_fn, "q_heads", S, TM, [_rows(q_p), _rows(cos), _rows(sin)], [("heads", H_A, QK_HEAD, BF16)])

    def k_heads_fn(kvb, kr):
        return (jnp.concatenate([jnp.concatenate([kvb[:, 256 * h:256 * h + QK_NOPE], kr], axis=1)[None] for h in range(H_A)], axis=0),)

    (k_full,) = _rowwise(k_heads_fn, "k_heads", S, TM, [_rows(kv), _rows(k_rope)], [("heads", H_A, QK_HEAD, BF16)])
    o_a, lse = _mla_fwd(q_full, k_full, kv, S)

    (kb, vb) = _rowwise(lambda a, b: (a, b), "kv_b_cast", S, TM, [_rows(proj, 256, 13), _rows(proj, 256, 14)],
                        [("rows", KV_B * HD_B, BF16), ("rows", KV_B * HD_B, BF16)])
    kp = jnp.pad(kb, ((WINDOW, WINDOW), (0, 0)))
    vp = jnp.pad(vb, ((WINDOW, WINDOW), (0, 0)))
    o_b = _win_fwd(proj, kp, vp, bias, sinks_col, S)

    (mixed,) = _rowwise(lambda ga, gb, oa, ob: (_sigmoid(ga) * oa + _sigmoid(gb) * ob,), "gate_mix", S, TM,
                        [_rows(proj, 1024, 1), _rows(proj, 1024, 2), _rows(o_a), _rows(o_b)], [("rows", D_MODEL, BF16)])
    x1 = _matmul(mixed, w_out_f, "nn", F32, "out_proj", residual=x)
    (h2,) = _rowwise(lambda a, g: (_rms(a, g),), "norm2", S, TM, [_rows(x1), _whole(norm2_g)], [("rows", D_MODEL, BF16)])
    u = _matmul(h2, w_up_f, "nn", F32, "ffn_up")
    act = _conv_gate_fwd(u, conv_w_f, conv_b, S)
    x2 = _matmul(act, w_down_f, "nn", F32, "ffn_down", residual=x1)

    def final_fn(a, g, t):
        err = _rms(a, g) - t
        loss = 0.5 * jnp.sum(jnp.mean(err * err, axis=-1, keepdims=True), axis=0, keepdims=True)
        dx, dg = _rms_bwd(err * (1.0 / D_MODEL), a, g)
        return dx, dx, dg, jnp.broadcast_to(loss, (1, 128))

    gfin = final_norm_g.reshape(1, D_MODEL)
    dx2, dx2_b, d_gfin, loss_row = _rowwise(final_fn, "loss_head", S, TM, [_rows(x2), _whole(gfin), _rows(target)],
                                            [("rows", D_MODEL, F32), ("rows", D_MODEL, BF16), ("acc", 1, D_MODEL), ("acc", 1, 128)])
    d_act = _matmul(dx2_b, w_down_f, "nt", F32, "ffn_down_dx")
    d_w_down = _matmul(act, dx2_b, "tn", F32, "ffn_down_dw")
    du_g, du_v, d_conv_w, d_conv_b = _conv_gate_bwd(u, conv_w_f, conv_b, d_act, S)
    du = jnp.concatenate([du_g, du_v], axis=1)
    d_h2 = _matmul(du, w_up_f, "nt", F32, "ffn_up_dx")
    d_w_up = _matmul(h2, du, "tn", F32, "ffn_up_dw")

    def norm_bwd_fn(dh, a, g, dres):
        dx, dg = _rms_bwd(dh, a, g)
        dx = dx + dres
        return dx, dx, dg

    dx1, dx1_b, d_g2 = _rowwise(norm_bwd_fn, "norm2_bwd", S, TM, [_rows(d_h2), _rows(x1), _whole(norm2_g), _rows(dx2)],
                                [("rows", D_MODEL, F32), ("rows", D_MODEL, BF16), ("acc", 1, D_MODEL)])
    d_mixed = _matmul(dx1_b, w_out_f, "nt", F32, "out_proj_dx")
    d_w_out = _matmul(mixed, dx1_b, "tn", F32, "out_proj_dw")

    def gate_bwd_fn(dm, ga, gb, oa, ob):
        sa, sb = _sigmoid(ga), _sigmoid(gb)
        return jnp.concatenate([dm * oa * sa * (1.0 - sa), dm * ob * sb * (1.0 - sb)], axis=1), dm * sa, dm * sb

    d_gates, do_a, do_b = _rowwise(gate_bwd_fn, "gate_bwd", S, TM,
                                   [_rows(d_mixed), _rows(proj, 1024, 1), _rows(proj, 1024, 2), _rows(o_a), _rows(o_b)],
                                   [("rows", 2 * D_MODEL, BF16), ("rows", D_MODEL, BF16), ("rows", D_MODEL, BF16)])

    dq_b, dk_acc, dv_acc, d_bias, d_sinks_b = _win_bwd(proj, kp, vp, bias, sinks_col, do_b, S)
    d_rel_bias = _bias_table_bwd(d_bias.reshape(H_B, Q_BLOCK * SPAN), onehot_t).T
    d_sinks = d_sinks_b[:, 0].reshape(1, H_B)

    dq_full, dk_full, dv_full = _mla_bwd(q_full, k_full, kv, do_a, o_a, lse, S)

    def dq_post_fn(dq, cs, sn):
        nope = [dq[h, :, :QK_NOPE] for h in range(H_A)]
        rope = []
        for h in range(H_A):
            rope += list(_rope_bwd(dq[h, :, QK_NOPE:QK_NOPE + half], dq[h, :, QK_NOPE + half:], cs, sn))
        return (jnp.concatenate(nope + rope, axis=1),)

    (dq_p,) = _rowwise(dq_post_fn, "dq_post", S, TM, [_heads(dq_full), _rows(cos), _rows(sin)], [("rows", H_A * QK_HEAD, BF16)])

    def dkv_post_fn(dk, dv, cs, sn):
        dkv = jnp.concatenate([jnp.concatenate([dk[h, :, :QK_NOPE], dv[h]], axis=1) for h in range(H_A)], axis=1)
        dkr = dk[0, :, QK_NOPE:]
        for h in range(1, H_A):
            dkr = dkr + dk[h, :, QK_NOPE:]
        r1, r2 = _rope_bwd(dkr[:, :half], dkr[:, half:], cs, sn)
        return dkv, jnp.concatenate([r1, r2], axis=1)

    dkv, d_krope = _rowwise(dkv_post_fn, "dkv_post", S, TM, [_heads(dk_full), _heads(dv_full), _rows(cos), _rows(sin)],
                            [("rows", H_A * (QK_NOPE + V_DIM), BF16), ("rows", QK_ROPE, F32)])
    d_qn = _matmul(dq_p, w_qb_p, "nt", F32, "q_up_dx")
    d_w_qb_p = _matmul(qn, dq_p, "tn", F32, "q_up_dw")
    d_ckvn = _matmul(dkv, w_kvb, "nt", F32, "kv_up_dx")
    d_w_kvb = _matmul(ckvn, dkv, "tn", F32, "kv_up_dw")

    def lat_bwd_fn(dqn, dckvn, dkr, qlat, ckv, gq, gkv, dkb, dvb):
        dql, dgq = _rms_bwd(dqn, qlat, gq)
        dck, dgkv = _rms_bwd(dckvn, ckv, gkv)
        tail = jnp.concatenate([dql, dkb, dvb, dck, dkr, jnp.zeros_like(dkr)], axis=1)
        return tail, dgq, dgkv

    shifted = lambda arr: (arr, lambda tm: pl.BlockSpec((tm, arr.shape[1]), lambda i: (i + WINDOW // tm, 0)))
    TL = min(128, S)
    d_tail, d_gq, d_gkv = _rowwise(lat_bwd_fn, "latents_bwd", S, TL,
                                   [_rows(d_qn), _rows(d_ckvn), _rows(d_krope), _rows(proj, 256, 12), _rows(proj, 128, 30),
                                    _whole(q_a_norm_g), _whole(kv_a_norm_g), shifted(dk_acc), shifted(dv_acc)],
                                   [("rows", 1024, BF16), ("acc", 1, Q_LORA), ("acc", 1, KV_LORA)])
    d_proj = jnp.concatenate([dq_b, d_gates, d_tail], axis=1)
    d_h1 = _matmul(d_proj, w_in_p, "nt", F32, "proj_dx")
    d_w_in_p = _matmul(h1, d_proj, "tn", F32, "proj_dw")
    grad_x, _, d_g1 = _rowwise(norm_bwd_fn, "norm1_bwd", S, TM, [_rows(d_h1), _rows(x), _whole(norm1_g), _rows(dx1)],
                               [("rows", D_MODEL, F32), ("rows", D_MODEL, BF16), ("acc", 1, D_MODEL)])

    dp = d_w_in_p
    d_w_in = jnp.concatenate([dp[:, 3072:3328], dp[:, 3840:3968], dp[:, 3968:4032], dp[:, 0:1024], dp[:, 3328:3584],
                              dp[:, 3584:3840], dp[:, 1024:2048], dp[:, 2048:3072]], axis=1)
    d_w_qb = jnp.concatenate([d_w_qb_p[:, :H_A * QK_NOPE].reshape(Q_LORA, H_A, QK_NOPE),
                              d_w_qb_p[:, H_A * QK_NOPE:].reshape(Q_LORA, H_A, QK_ROPE)], axis=2).reshape(Q_LORA, H_A * QK_HEAD)
    stacked = [_stack_cols(d_w_in).astype(BF16), _stack_cols(d_w_qb).astype(BF16), _stack_cols(d_w_kvb).astype(BF16),
               d_w_out.reshape(N_DEV, D_MODEL // N_DEV, D_MODEL).astype(BF16), _stack_cols(d_w_up).astype(BF16),
               d_w_down.reshape(N_DEV, D_FF // N_DEV, D_MODEL).astype(BF16), _stack_cols(d_conv_w)]
    small_parts = [d_g1, d_gq, d_gkv, d_rel_bias.reshape(1, NUM_BUCKETS * H_B), d_sinks, d_g2, d_conv_b, d_gfin]
    small = jnp.concatenate(small_parts, axis=1)
    n_small = small.shape[1]
    pad = (-n_small) % 128
    small = jnp.pad(small, ((0, 0), (0, pad)))
    *recv, recv_small = _exchange(stacked, [small])

    def flat(a):
        return a.reshape(1, -1)

    small_w = [norm1_g, q_a_norm_g, kv_a_norm_g, rel_bias, sinks, norm2_g, conv_b, final_norm_g]
    small_m = [m_norm1_g, m_q_a_norm_g, m_kv_a_norm_g, m_rel_bias, m_sinks, m_norm2_g, m_conv_b, m_final_norm_g]
    small_v = [v_norm1_g, v_q_a_norm_g, v_kv_a_norm_g, v_rel_bias, v_sinks, v_norm2_g, v_conv_b, v_final_norm_g]
    cat = lambda parts: jnp.pad(jnp.concatenate([flat(a) for a in parts], axis=1), ((0, 0), (0, pad)))
    sm = _adamw(recv_small, cat(small_w), cat(small_m), jnp.pad(jnp.concatenate([flat(a) for a in small_v], axis=1), ((0, 0), (0, pad)), constant_values=1.0), "adamw_small")
    big_names = ["w_in", "w_q_b", "w_kv_b", "w_out", "w_up", "w_down", "conv_w"]
    big_w = [w_in, w_q_b, w_kv_b, w_out, w_up, w_down, conv_w]
    big_m = [m_w_in, m_w_q_b, m_w_kv_b, m_w_out, m_w_up, m_w_down, m_conv_w]
    big_v = [v_w_in, v_w_q_b, v_w_kv_b, v_w_out, v_w_up, v_w_down, v_conv_w]
    big = {n: _adamw(r, w[0], m[0], v[0], "adamw_" + n) for n, r, w, m, v in zip(big_names, recv, big_w, big_m, big_v)}

    loss = lax.psum(loss_row[0, 0], ("x", "y", "c"))
    order = ["norm1_g", "w_in", "q_a_norm_g", "w_q_b", "kv_a_norm_g", "w_kv_b", "rel_bias", "sinks", "w_out", "norm2_g", "w_up",
             "conv_w", "conv_b", "w_down", "final_norm_g"]
    small_names = ["norm1_g", "q_a_norm_g", "kv_a_norm_g", "rel_bias", "sinks", "norm2_g", "conv_b", "final_norm_g"]
    offs, o = {}, 0
    for n, a in zip(small_names, small_w):
        offs[n] = (o, a.size, a.shape)
        o += a.size
    outs = [loss, grad_x[None]]
    for kind in range(4):
        for n in order:
            if n in big:
                outs.append(big[n][kind][None])
            else:
                o, size, shape = offs[n]
                outs.append(sm[kind][0, o:o + size].reshape(shape))
    return tuple(outs)
```

```python
import math

import jax
import jax.numpy as jnp
from jax import lax
from jax.experimental import pallas as pl
from jax.experimental.pallas import tpu as pltpu

F32 = jnp.float32
BF16 = jnp.bfloat16

N_DEV = 8
D_MODEL = 1024
EPS = 1e-6
H_A, QK_NOPE, QK_ROPE, V_DIM, Q_LORA, KV_LORA = 8, 128, 64, 128, 256, 128
QK_HEAD = QK_NOPE + QK_ROPE
ROPE_THETA = 10000.0
H_B, KV_B, GROUP, HD_B, WINDOW, Q_BLOCK = 16, 4, 4, 64, 128, 128
SPAN = Q_BLOCK + 2 * WINDOW
NUM_BUCKETS, MAX_DISTANCE = 32, 128
D_FF = 2816
ADAM_LR, ADAM_B1, ADAM_B2, ADAM_EPS, ADAM_WD, ADAM_STEP = 0.001, 0.9, 0.999, 1e-08, 0.01, 10

W_IN_SIZES = (Q_LORA, KV_LORA + QK_ROPE, H_B * HD_B, KV_B * HD_B, KV_B * HD_B, D_MODEL, D_MODEL)
W_IN_COLS = sum(W_IN_SIZES)
PROJ_P = 4096

VMEM_LIMIT = 56 * 1024 * 1024

NN = (((1,), (0,)), ((), ()))
NT = (((1,), (1,)), ((), ()))
TN = (((0,), (0,)), ((), ()))


def _pcall(body, *, dims=None, **kw):
    params = pltpu.CompilerParams(dimension_semantics=dims, vmem_limit_bytes=VMEM_LIMIT)
    return pl.pallas_call(body, compiler_params=params, **kw)


def _dot(a, b, dn):
    return lax.dot_general(a, b, dn, preferred_element_type=F32)


def _tile(n, target):
    best = None
    for t in range(128, min(n, target) + 1, 128):
        if n % t == 0:
            best = t
    return n if best is None else best


def _matmul(a, b, mode, out_dtype, name, residual=None, tm=1024, tn=1024):
    if mode == "nn":
        (M, K), N = a.shape, b.shape[1]
    elif mode == "nt":
        (M, K), N = a.shape, b.shape[0]
    else:
        (K, M), N = a.shape, b.shape[1]
    tm, tn = _tile(M, tm), _tile(N, tn)
    a_spec = pl.BlockSpec((K, tm), lambda i, j: (0, i)) if mode == "tn" else pl.BlockSpec((tm, K), lambda i, j: (i, 0))
    b_spec = pl.BlockSpec((tn, K), lambda i, j: (j, 0)) if mode == "nt" else pl.BlockSpec((K, tn), lambda i, j: (0, j))
    o_spec = pl.BlockSpec((tm, tn), lambda i, j: (i, j))
    in_specs, args = [a_spec, b_spec], [a, b]
    if residual is not None:
        in_specs.append(o_spec)
        args.append(residual)
    scratch = [pltpu.VMEM((tm, K), a.dtype)] if mode == "tn" else []

    def body(*refs):
        a_ref, b_ref = refs[0], refs[1]
        o_ref = refs[len(args)]
        if mode == "tn":
            at_ref = refs[len(args) + 1]

            @pl.when(pl.program_id(1) == 0)
            def _():
                at_ref[...] = a_ref[...].T

            acc = _dot(at_ref[...], b_ref[...], NN)
        else:
            acc = _dot(a_ref[...], b_ref[...], NT if mode == "nt" else NN)
        if residual is not None:
            acc = acc + refs[2][...]
        o_ref[...] = acc.astype(out_dtype)

    return _pcall(body, name=name, grid=(M // tm, N // tn), in_specs=in_specs, out_specs=o_spec,
                  out_shape=jax.ShapeDtypeStruct((M, N), out_dtype), scratch_shapes=scratch,
                  dims=("parallel", "arbitrary"))(*args)


def _rows(arr, width=None, col=0):
    width = arr.shape[1] if width is None else width
    return (arr, lambda tm: pl.BlockSpec((tm, width), lambda i: (i, col)))


def _heads(arr):
    return (arr, lambda tm: pl.BlockSpec((arr.shape[0], tm, arr.shape[2]), lambda i: (0, i, 0)))


def _whole(arr):
    nd = arr.ndim
    return (arr, lambda tm: pl.BlockSpec(arr.shape, lambda i: (0,) * nd))


def _rowwise(fn, name, n_rows, tm, ins, outs):
    tm = min(tm, n_rows)
    assert n_rows % tm == 0
    in_specs = [mk(tm) for _, mk in ins]
    out_specs, out_shape, is_acc = [], [], []
    for o in outs:
        if o[0] == "rows":
            out_specs.append(pl.BlockSpec((tm, o[1]), lambda i: (i, 0)))
            out_shape.append(jax.ShapeDtypeStruct((n_rows, o[1]), o[2]))
        elif o[0] == "heads":
            out_specs.append(pl.BlockSpec((o[1], tm, o[2]), lambda i: (0, i, 0)))
            out_shape.append(jax.ShapeDtypeStruct((o[1], n_rows, o[2]), o[3]))
        else:
            out_specs.append(pl.BlockSpec((o[1], o[2]), lambda i: (0, 0)))
            out_shape.append(jax.ShapeDtypeStruct((o[1], o[2]), F32))
        is_acc.append(o[0] == "acc")
    n_in = len(ins)

    def body(*refs):
        vals = fn(*[r[...] for r in refs[:n_in]])
        for r, v, acc in zip(refs[n_in:], vals, is_acc):
            if acc:
                @pl.when(pl.program_id(0) == 0)
                def _():
                    r[...] = jnp.zeros_like(r)

                r[...] += v
            else:
                r[...] = v.astype(r.dtype)

    return _pcall(body, name=name, grid=(n_rows // tm,), in_specs=in_specs, out_specs=out_specs,
                  out_shape=out_shape, dims=("arbitrary",))(*[a for a, _ in ins])


def _rms(x, g):
    r = lax.rsqrt(jnp.mean(x * x, axis=-1, keepdims=True) + EPS)
    return x * r * g


def _rms_bwd(dy, x, g):
    r = lax.rsqrt(jnp.mean(x * x, axis=-1, keepdims=True) + EPS)
    xhat = x * r
    dxhat = dy * g
    dx = r * (dxhat - xhat * jnp.mean(dxhat * xhat, axis=-1, keepdims=True))
    return dx, jnp.sum(dy * xhat, axis=0, keepdims=True)


def _rope(x1, x2, cos, sin):
    return x1 * cos - x2 * sin, x2 * cos + x1 * sin


def _rope_bwd(d1, d2, cos, sin):
    return d1 * cos + d2 * sin, d2 * cos - d1 * sin


def _sigmoid(x):
    return 1.0 / (1.0 + jnp.exp(-x))


def _mla_fwd(q_full, k_full, kv, S):
    tq = min(256, S)
    scale = 1.0 / math.sqrt(QK_HEAD)

    def body(q_ref, k_ref, v_ref, o_ref, lse_ref):
        s = _dot(q_ref[0], k_ref[0], NT) * scale
        m = jnp.max(s, axis=-1, keepdims=True)
        p = jnp.exp(s - m)
        l = jnp.sum(p, axis=-1, keepdims=True)
        o_ref[...] = _dot(p.astype(BF16), v_ref[...], NN) / l
        lse_ref[0] = m + jnp.log(l)

    return _pcall(
        body, name="mla_fwd", grid=(H_A, S // tq),
        in_specs=[pl.BlockSpec((1, tq, QK_HEAD), lambda h, i: (h, i, 0)),
                  pl.BlockSpec((1, S, QK_HEAD), lambda h, i: (h, 0, 0)),
                  pl.BlockSpec((S, V_DIM), lambda h, i: (0, 2 * h + 1))],
        out_specs=[pl.BlockSpec((tq, V_DIM), lambda h, i: (i, h)),
                   pl.BlockSpec((1, tq, 1), lambda h, i: (h, i, 0))],
        out_shape=[jax.ShapeDtypeStruct((S, H_A * V_DIM), F32), jax.ShapeDtypeStruct((H_A, S, 1), F32)],
        dims=("parallel", "parallel"))(q_full, k_full, kv)


def _mla_bwd(q_full, k_full, kv, do_a, o_a, lse, S):
    tq = min(256, S)
    kc = min(1024, S)
    scale = 1.0 / math.sqrt(QK_HEAD)

    def body(q_ref, k_ref, v_ref, do_ref, o_ref, lse_ref, dq_ref, dk_ref, dv_ref):
        @pl.when(pl.program_id(1) == 0)
        def _():
            dk_ref[...] = jnp.zeros_like(dk_ref)
            dv_ref[...] = jnp.zeros_like(dv_ref)

        q = q_ref[0]
        do = do_ref[...]
        lse_q = lse_ref[0]
        delta = jnp.sum(do.astype(F32) * o_ref[...], axis=-1, keepdims=True)
        dq = jnp.zeros((tq, QK_HEAD), F32)
        for c in range(S // kc):
            k = k_ref[0, c * kc:(c + 1) * kc, :]
            v = v_ref[c * kc:(c + 1) * kc, :]
            p = jnp.exp(_dot(q, k, NT) * scale - lse_q)
            ds = (p * (_dot(do, v, NT) - delta) * scale).astype(BF16)
            dq = dq + _dot(ds, k, NN)
            dk_ref[0, c * kc:(c + 1) * kc, :] += _dot(ds, q, TN)
            dv_ref[0, c * kc:(c + 1) * kc, :] += _dot(p.astype(BF16), do, TN)
        dq_ref[0] = dq

    return _pcall(
        body, name="mla_bwd", grid=(H_A, S // tq),
        in_specs=[pl.BlockSpec((1, tq, QK_HEAD), lambda h, i: (h, i, 0)),
                  pl.BlockSpec((1, S, QK_HEAD), lambda h, i: (h, 0, 0)),
                  pl.BlockSpec((S, V_DIM), lambda h, i: (0, 2 * h + 1)),
                  pl.BlockSpec((tq, V_DIM), lambda h, i: (i, h)),
                  pl.BlockSpec((tq, V_DIM), lambda h, i: (i, h)),
                  pl.BlockSpec((1, tq, 1), lambda h, i: (h, i, 0))],
        out_specs=[pl.BlockSpec((1, tq, QK_HEAD), lambda h, i: (h, i, 0)),
                   pl.BlockSpec((1, S, QK_HEAD), lambda h, i: (h, 0, 0)),
                   pl.BlockSpec((1, S, V_DIM), lambda h, i: (h, 0, 0))],
        out_shape=[jax.ShapeDtypeStruct((H_A, S, QK_HEAD), F32), jax.ShapeDtypeStruct((H_A, S, QK_HEAD), F32),
                   jax.ShapeDtypeStruct((H_A, S, V_DIM), F32)],
        dims=("parallel", "arbitrary"))(q_full, k_full, kv, do_a, o_a, lse)


WIN_SCALE = 1.0 / math.sqrt(HD_B)


def _win_specs(S):
    qspec = pl.BlockSpec((Q_BLOCK, H_B * HD_B), lambda n: (n, 0))
    kspecs = [pl.BlockSpec((Q_BLOCK, KV_B * HD_B), lambda n, d=d: (n + d, 0)) for d in range(3)]
    bias_spec = pl.BlockSpec((H_B, SPAN, Q_BLOCK), lambda n: (0, 0, 0))
    sink_spec = pl.BlockSpec((H_B, Q_BLOCK), lambda n: (0, 0))
    return qspec, kspecs, bias_spec, sink_spec


def _win_edge_ok(n, n_blk):
    row = lax.broadcasted_iota(jnp.int32, (SPAN, 1), 0)
    return jnp.logical_not(((n == 0) & (row < WINDOW)) | ((n == n_blk - 1) & (row >= SPAN - WINDOW)))


def _lanes4(pieces):
    return jnp.concatenate(pieces, axis=1)


def _win_probs(kg, q4t, bias_ref, sink_ref, g, edge_ok):
    bias4 = _lanes4([bias_ref[GROUP * g + j] for j in range(GROUP)])
    sink4 = _lanes4([sink_ref[GROUP * g + j:GROUP * g + j + 1, :] for j in range(GROUP)])
    s = jnp.where(edge_ok, _dot(kg, q4t, NN) + bias4, -1e30)
    m = jnp.maximum(jnp.max(s, axis=0, keepdims=True), sink4)
    p = jnp.exp(s - m)
    e_sink = jnp.exp(sink4 - m)
    inv_l = 1.0 / (jnp.sum(p, axis=0, keepdims=True) + e_sink)
    return p * inv_l, e_sink * inv_l


def _group_t(xt, g):
    return _lanes4([xt[HD_B * (GROUP * g + j):HD_B * (GROUP * g + j + 1), :] for j in range(GROUP)])


def _win_fwd(proj, kp, vp, bias_t, sinks_b, S):
    n_blk = S // Q_BLOCK
    qspec, kspecs, bias_spec, sink_spec = _win_specs(S)

    def body(q_ref, k0, k1, k2, v0, v1, v2, bias_ref, sink_ref, o_ref):
        n = pl.program_id(0)
        edge_ok = _win_edge_ok(n, n_blk)
        k = jnp.concatenate([k0[...], k1[...], k2[...]], axis=0)
        vt = jnp.concatenate([v0[...], v1[...], v2[...]], axis=0).T
        qt = (q_ref[...] * WIN_SCALE).T.astype(BF16)
        parts = []
        for g in range(KV_B):
            p, _ = _win_probs(k[:, HD_B * g:HD_B * (g + 1)], _group_t(qt, g), bias_ref, sink_ref, g, edge_ok)
            o4t = _dot(vt[HD_B * g:HD_B * (g + 1), :], p.astype(BF16), NN)
            parts += [o4t[:, Q_BLOCK * j:Q_BLOCK * (j + 1)] for j in range(GROUP)]
        o_ref[...] = jnp.concatenate(parts, axis=0).T

    return _pcall(body, name="win_fwd", grid=(n_blk,),
                  in_specs=[qspec, *kspecs, *kspecs, bias_spec, sink_spec],
                  out_specs=pl.BlockSpec((Q_BLOCK, H_B * HD_B), lambda n: (n, 0)),
                  out_shape=jax.ShapeDtypeStruct((S, H_B * HD_B), F32),
                  dims=("parallel",))(proj, kp, kp, kp, vp, vp, vp, bias_t, sinks_b)


def _win_bwd(proj, kp, vp, bias_t, sinks_b, do_b, S):
    n_blk = S // Q_BLOCK
    qspec, kspecs, bias_spec, sink_spec = _win_specs(S)

    def body(q_ref, k0, k1, k2, v0, v1, v2, bias_ref, sink_ref, do_ref, dq_ref, dk_ref, dv_ref, dbias_ref, dsink_ref, dsink_acc):
        n = pl.program_id(0)

        @pl.when(n == 0)
        def _():
            dk_ref[...] = jnp.zeros_like(dk_ref)
            dv_ref[...] = jnp.zeros_like(dv_ref)
            dbias_ref[...] = jnp.zeros_like(dbias_ref)
            dsink_acc[...] = jnp.zeros_like(dsink_acc)

        edge_ok = _win_edge_ok(n, n_blk)
        k = jnp.concatenate([k0[...], k1[...], k2[...]], axis=0)
        v = jnp.concatenate([v0[...], v1[...], v2[...]], axis=0)
        kt = k.T
        qt = (q_ref[...] * WIN_SCALE).T.astype(BF16)
        dot_ = do_ref[...].astype(F32).T.astype(BF16)
        dq_parts, dks, dvs = [], [], []
        for g in range(KV_B):
            kg, vg = k[:, HD_B * g:HD_B * (g + 1)], v[:, HD_B * g:HD_B * (g + 1)]
            q4t, do4t = _group_t(qt, g), _group_t(dot_, g)
            p, p_sink = _win_probs(kg, q4t, bias_ref, sink_ref, g, edge_ok)
            dp = _dot(vg, do4t, NN)
            delta = jnp.sum(p * dp, axis=0, keepdims=True)
            ds = p * (dp - delta)
            for j in range(GROUP):
                dbias_ref[GROUP * g + j] += ds[:, Q_BLOCK * j:Q_BLOCK * (j + 1)]
            dsink_acc[g:g + 1, :] += -p_sink * delta
            dsb = ds.astype(BF16)
            dq4t = _dot(kt[HD_B * g:HD_B * (g + 1), :], dsb, NN) * WIN_SCALE
            dq_parts += [dq4t[:, Q_BLOCK * j:Q_BLOCK * (j + 1)] for j in range(GROUP)]
            dks.append(_dot(dsb, q4t, NT))
            dvs.append(_dot(p.astype(BF16), do4t, NT))
        dq_ref[...] = jnp.concatenate(dq_parts, axis=0).T.astype(BF16)
        rows = pl.ds(pl.multiple_of(n * Q_BLOCK, Q_BLOCK), SPAN)
        dk_ref[rows, :] += jnp.concatenate(dks, axis=1)
        dv_ref[rows, :] += jnp.concatenate(dvs, axis=1)

        @pl.when(n == n_blk - 1)
        def _():
            acc = dsink_acc[...]
            dsink_ref[...] = jnp.concatenate(
                [jnp.sum(acc[:, Q_BLOCK * j:Q_BLOCK * (j + 1)], axis=1, keepdims=True) for j in range(GROUP)], axis=1)

    whole = lambda shape: pl.BlockSpec(shape, lambda n: (0,) * len(shape))
    return _pcall(
        body, name="win_bwd", grid=(n_blk,),
        in_specs=[qspec, *kspecs, *kspecs, bias_spec, sink_spec, pl.BlockSpec((Q_BLOCK, H_B * HD_B), lambda n: (n, 0))],
        out_specs=[pl.BlockSpec((Q_BLOCK, H_B * HD_B), lambda n: (n, 0)), whole((S + 2 * WINDOW, KV_B * HD_B)),
                   whole((S + 2 * WINDOW, KV_B * HD_B)), whole((H_B, SPAN, Q_BLOCK)), whole((KV_B, GROUP))],
        out_shape=[jax.ShapeDtypeStruct((S, H_B * HD_B), BF16), jax.ShapeDtypeStruct((S + 2 * WINDOW, KV_B * HD_B), F32),
                   jax.ShapeDtypeStruct((S + 2 * WINDOW, KV_B * HD_B), F32), jax.ShapeDtypeStruct((H_B, SPAN, Q_BLOCK), F32),
                   jax.ShapeDtypeStruct((KV_B, GROUP), F32)],
        scratch_shapes=[pltpu.VMEM((KV_B, GROUP * Q_BLOCK), F32)],
        dims=("arbitrary",))(proj, kp, kp, kp, vp, vp, vp, bias_t, sinks_b, do_b)


def _bias_table(rel_bias_t, onehot_t, in_band):
    def body(rb_ref, oh_ref, band_ref, o_ref):
        t = lax.dot_general(rb_ref[...], oh_ref[...], NN, preferred_element_type=F32, precision=lax.Precision.HIGHEST)
        o_ref[...] = jnp.where(band_ref[...] > 0.5, t, -1e30)

    n = onehot_t.shape[1]
    tn = _tile(n, 8192)
    return _pcall(body, name="bias_table", grid=(n // tn,),
                  in_specs=[pl.BlockSpec((H_B, NUM_BUCKETS), lambda j: (0, 0)), pl.BlockSpec((NUM_BUCKETS, tn), lambda j: (0, j)),
                            pl.BlockSpec((1, tn), lambda j: (0, j))],
                  out_specs=pl.BlockSpec((H_B, tn), lambda j: (0, j)),
                  out_shape=jax.ShapeDtypeStruct((H_B, n), F32), dims=("parallel",))(rel_bias_t, onehot_t, in_band)


def _bias_table_bwd(dbias, onehot_t):
    n = onehot_t.shape[1]
    tk = _tile(n, 8192)

    def body(d_ref, oh_ref, o_ref):
        @pl.when(pl.program_id(0) == 0)
        def _():
            o_ref[...] = jnp.zeros_like(o_ref)

        o_ref[...] += lax.dot_general(d_ref[...], oh_ref[...], NT, preferred_element_type=F32, precision=lax.Precision.HIGHEST)

    return _pcall(body, name="bias_table_bwd", grid=(n // tk,),
                  in_specs=[pl.BlockSpec((H_B, tk), lambda j: (0, j)), pl.BlockSpec((NUM_BUCKETS, tk), lambda j: (0, j))],
                  out_specs=pl.BlockSpec((H_B, NUM_BUCKETS), lambda j: (0, 0)),
                  out_shape=jax.ShapeDtypeStruct((H_B, NUM_BUCKETS), F32), dims=("arbitrary",))(dbias, onehot_t)


def _shift_down(u):
    row = lax.broadcasted_iota(jnp.int32, u.shape, 0)
    return jnp.where(row == 0, 0.0, pltpu.roll(u, 1, axis=0))


def _shift_up(u):
    n = u.shape[0]
    row = lax.broadcasted_iota(jnp.int32, u.shape, 0)
    return jnp.where(row == n - 1, 0.0, pltpu.roll(u, n - 1, axis=0))


def _conv(u, w, b):
    return _shift_down(u) * w[0:1, :] + u * w[1:2, :] + _shift_up(u) * w[2:3, :] + b


CONV_STRIP = 128


def _conv_specs(S):
    nj = D_FF // CONV_STRIP
    ug = pl.BlockSpec((S, CONV_STRIP), lambda j: (0, j))
    uv = pl.BlockSpec((S, CONV_STRIP), lambda j: (0, j + nj))
    wg = pl.BlockSpec((3, CONV_STRIP), lambda j: (0, j))
    wv = pl.BlockSpec((3, CONV_STRIP), lambda j: (0, j + nj))
    bg = pl.BlockSpec((1, CONV_STRIP), lambda j: (0, j))
    bv = pl.BlockSpec((1, CONV_STRIP), lambda j: (0, j + nj))
    return nj, ug, uv, wg, wv, bg, bv


def _conv_gate_fwd(u, conv_w, conv_b, S):
    nj, ug, uv, wg, wv, bg, bv = _conv_specs(S)

    def body(ug_ref, uv_ref, wg_ref, wv_ref, bg_ref, bv_ref, a_ref):
        g = _conv(ug_ref[...], wg_ref[...], bg_ref[...])
        val = _conv(uv_ref[...], wv_ref[...], bv_ref[...])
        a_ref[...] = (g * _sigmoid(g) * val).astype(BF16)

    return _pcall(body, name="conv_gate_fwd", grid=(nj,), in_specs=[ug, uv, wg, wv, bg, bv],
                  out_specs=pl.BlockSpec((S, CONV_STRIP), lambda j: (0, j)),
                  out_shape=jax.ShapeDtypeStruct((S, D_FF), BF16), dims=("parallel",))(u, u, conv_w, conv_w, conv_b, conv_b)


def _conv_gate_bwd(u, conv_w, conv_b, da, S):
    nj, ug, uv, wg, wv, bg, bv = _conv_specs(S)

    def conv_bwd(duc, u_in, w):
        du = _shift_up(duc) * w[0:1, :] + duc * w[1:2, :] + _shift_down(duc) * w[2:3, :]
        dw = jnp.concatenate([jnp.sum(duc * _shift_down(u_in), axis=0, keepdims=True),
                              jnp.sum(duc * u_in, axis=0, keepdims=True),
                              jnp.sum(duc * _shift_up(u_in), axis=0, keepdims=True)], axis=0)
        return du, dw, jnp.sum(duc, axis=0, keepdims=True)

    def body(ug_ref, uv_ref, wg_ref, wv_ref, bg_ref, bv_ref, da_ref, dug_ref, duv_ref, dwg_ref, dwv_ref, dbg_ref, dbv_ref):
        u_g, u_v, w_g, w_v = ug_ref[...], uv_ref[...], wg_ref[...], wv_ref[...]
        g = _conv(u_g, w_g, bg_ref[...])
        val = _conv(u_v, w_v, bv_ref[...])
        da = da_ref[...]
        sg = _sigmoid(g)
        dval = da * (g * sg)
        dg = da * val * (sg * (1.0 + g * (1.0 - sg)))
        du_g, dw_g, db_g = conv_bwd(dg, u_g, w_g)
        du_v, dw_v, db_v = conv_bwd(dval, u_v, w_v)
        dug_ref[...] = du_g.astype(BF16)
        duv_ref[...] = du_v.astype(BF16)
        dwg_ref[...] = dw_g
        dwv_ref[...] = dw_v
        dbg_ref[...] = db_g
        dbv_ref[...] = db_v

    strip = lambda r, dt: (pl.BlockSpec((r, CONV_STRIP), lambda j: (0, j)), jax.ShapeDtypeStruct((r, D_FF), dt))
    outs = [strip(S, BF16), strip(S, BF16), strip(3, F32), strip(3, F32), strip(1, F32), strip(1, F32)]
    du_g, du_v, dw_g, dw_v, db_g, db_v = _pcall(
        body, name="conv_gate_bwd", grid=(nj,),
        in_specs=[ug, uv, wg, wv, bg, bv, pl.BlockSpec((S, CONV_STRIP), lambda j: (0, j))],
        out_specs=[o[0] for o in outs], out_shape=[o[1] for o in outs],
        dims=("parallel",))(u, u, conv_w, conv_w, conv_b, conv_b, da)
    return du_g, du_v, jnp.concatenate([dw_g, dw_v], axis=1), jnp.concatenate([db_g, db_v], axis=1)


MESH = pl.DeviceIdType.MESH
ANY = pl.BlockSpec(memory_space=pl.ANY)


def _place():
    return lax.axis_index("x"), lax.axis_index("y"), lax.axis_index("c")


def _all_gather(shards):
    n_arr = len(shards)

    def body(*refs):
        ins, outs = refs[:n_arr], refs[n_arr:2 * n_arr]
        send_sems, recv_sems, local_sems = refs[2 * n_arr:]
        x, y, c = _place()
        me, sibling = (x, y, c), (x, y, 1 - c)
        chips = [(1 - x, y), (x, 1 - y), (1 - x, 1 - y)]

        def slot(a, p):
            return outs[a].at[4 * p[0] + 2 * p[1] + p[2]]

        def copy(a, k, block, to, src=None):
            return pltpu.make_async_remote_copy(
                src_ref=slot(a, block) if src is None else src, dst_ref=slot(a, block),
                send_sem=send_sems.at[a, k], recv_sem=recv_sems.at[a, k], device_id=to, device_id_type=MESH)

        mine = [pltpu.make_async_copy(ins[a], slot(a, me), local_sems.at[a]) for a in range(n_arr)]
        for cp in mine:
            cp.start()
        first = []
        for a in range(n_arr):
            first.append(copy(a, 0, me, sibling, src=ins[a]))
            first += [copy(a, 1 + j, me, (*chip, c), src=ins[a]) for j, chip in enumerate(chips)]
        for cp in first:
            cp.start()
        passed = []
        for j, chip in enumerate(chips):
            for a in range(n_arr):
                copy(a, 1 + j, (*chip, c), me).wait_recv()
                cp = copy(a, 4 + j, (*chip, c), sibling)
                cp.start()
                passed.append(cp)
        for a in range(n_arr):
            copy(a, 0, sibling, me).wait_recv()
            for j, chip in enumerate(chips):
                copy(a, 4 + j, (*chip, 1 - c), me).wait_recv()
        for cp in first + passed:
            cp.wait_send()
        for cp in mine:
            cp.wait()

    return pl.pallas_call(
        body, name="all_gather_weights",
        in_specs=[ANY] * n_arr, out_specs=[ANY] * n_arr,
        out_shape=[jax.ShapeDtypeStruct((N_DEV, *s.shape), s.dtype) for s in shards],
        scratch_shapes=[pltpu.SemaphoreType.DMA((n_arr, 7)), pltpu.SemaphoreType.DMA((n_arr, 7)), pltpu.SemaphoreType.DMA((n_arr,))],
    )(*shards)


def _exchange(stacked, replicated):
    arrs = list(stacked) + list(replicated)
    n_st, n_arr = len(stacked), len(arrs)

    def body(*refs):
        ins, outs = refs[:n_arr], refs[n_arr:2 * n_arr]
        send_sems, recv_sems, local_sems = refs[2 * n_arr:]
        x, y, c = _place()
        me = 4 * x + 2 * y + c

        def src(a, idx):
            return ins[a].at[idx] if a < n_st else ins[a]

        mine = [pltpu.make_async_copy(src(a, me), outs[a].at[me], local_sems.at[a]) for a in range(n_arr)]
        for cp in mine:
            cp.start()
        copies = []
        for k in range(1, N_DEV):
            px, py, pc = x ^ (k >> 2), y ^ ((k >> 1) & 1), c ^ (k & 1)
            peer = 4 * px + 2 * py + pc
            for a in range(n_arr):
                cp = pltpu.make_async_remote_copy(
                    src_ref=src(a, peer), dst_ref=outs[a].at[me], send_sem=send_sems.at[a, k - 1],
                    recv_sem=recv_sems.at[a, k - 1], device_id=(px, py, pc), device_id_type=MESH)
                cp.start()
                copies.append((cp, a, k, peer))
        for cp, a, k, peer in copies:
            pltpu.make_async_remote_copy(
                src_ref=src(a, peer), dst_ref=outs[a].at[peer], send_sem=send_sems.at[a, k - 1],
                recv_sem=recv_sems.at[a, k - 1], device_id=(x, y, c), device_id_type=MESH).wait_recv()
        for cp, a, k, peer in copies:
            cp.wait_send()
        for cp in mine:
            cp.wait()

    out_shape = [jax.ShapeDtypeStruct(s.shape, s.dtype) for s in stacked]
    out_shape += [jax.ShapeDtypeStruct((N_DEV, *r.shape), r.dtype) for r in replicated]
    return pl.pallas_call(
        body, name="exchange_grads",
        in_specs=[ANY] * n_arr, out_specs=[ANY] * n_arr, out_shape=out_shape,
        scratch_shapes=[pltpu.SemaphoreType.DMA((n_arr, 7)), pltpu.SemaphoreType.DMA((n_arr, 7)), pltpu.SemaphoreType.DMA((n_arr,))],
    )(*arrs)


def _adamw(parts, w, m, v, name):
    R, C = w.shape
    tr = R if (R <= 512 or R % 256) else 256

    def body(p_ref, w_ref, m_ref, v_ref, g_ref, d_ref, nm_ref, nv_ref):
        g = p_ref[0].astype(F32)
        for s in range(1, N_DEV):
            g = g + p_ref[s].astype(F32)
        m2 = ADAM_B1 * m_ref[...] + (1.0 - ADAM_B1) * g
        v2 = ADAM_B2 * v_ref[...] + (1.0 - ADAM_B2) * (g * g)
        m_hat = m2 / (1.0 - ADAM_B1 ** ADAM_STEP)
        v_hat = v2 / (1.0 - ADAM_B2 ** ADAM_STEP)
        g_ref[...] = g
        d_ref[...] = -ADAM_LR * (m_hat / (jnp.sqrt(v_hat) + ADAM_EPS) + ADAM_WD * w_ref[...])
        nm_ref[...] = m2
        nv_ref[...] = v2

    blk = pl.BlockSpec((tr, C), lambda i: (i, 0))
    return _pcall(body, name=name, grid=(R // tr,),
                  in_specs=[pl.BlockSpec((N_DEV, tr, C), lambda i: (0, i, 0)), blk, blk, blk],
                  out_specs=[blk] * 4, out_shape=[jax.ShapeDtypeStruct((R, C), F32)] * 4,
                  dims=("parallel",))(parts, w, m, v)


def _t5_bucket(rel):
    nb = NUM_BUCKETS // 2
    max_exact = nb // 2
    base = (rel > 0).astype(jnp.int32) * nb
    n = jnp.abs(rel)
    nf = jnp.maximum(n, 1).astype(jnp.float32)
    large = max_exact + (jnp.log(nf / max_exact) / math.log(MAX_DISTANCE / max_exact) * (nb - max_exact)).astype(jnp.int32)
    large = jnp.minimum(large, nb - 1)
    return base + jnp.where(n < max_exact, n, large)


def _unstack_cols(g):
    return jnp.transpose(g, (1, 0, 2)).reshape(g.shape[1], N_DEV * g.shape[2])


def _stack_cols(w):
    R = w.shape[0]
    return jnp.transpose(w.reshape(R, N_DEV, w.shape[1] // N_DEV), (1, 0, 2))


def kernel(x, positions, norm1_g, w_in, q_a_norm_g, w_q_b, kv_a_norm_g, w_kv_b, rel_bias, sinks, w_out, norm2_g, w_up, conv_w, conv_b, w_down, final_norm_g, loss_target, m_norm1_g, m_w_in, m_q_a_norm_g, m_w_q_b, m_kv_a_norm_g, m_w_kv_b, m_rel_bias, m_sinks, m_w_out, m_norm2_g, m_w_up, m_conv_w, m_conv_b, m_w_down, m_final_norm_g, v_norm1_g, v_w_in, v_q_a_norm_g, v_w_q_b, v_kv_a_norm_g, v_w_kv_b, v_rel_bias, v_sinks, v_w_out, v_norm2_g, v_w_up, v_conv_w, v_conv_b, v_w_down, v_final_norm_g):
    S = x.shape[1]
    x = x[0]
    target = loss_target[0]
    TM = 256

    g_in, g_qb, g_kvb, g_out, g_up, g_down, g_cw = _all_gather(
        [w_in[0].astype(BF16), w_q_b[0].astype(BF16), w_kv_b[0].astype(BF16), w_out[0].astype(BF16),
         w_up[0].astype(BF16), w_down[0].astype(BF16), conv_w[0]])
    wi = _unstack_cols(g_in)
    c0, c1, c2, c3, c4, c5 = (sum(W_IN_SIZES[:i + 1]) for i in range(6))
    w_in_p = jnp.concatenate([wi[:, c1:c2], wi[:, c4:c5], wi[:, c5:], wi[:, :c0], wi[:, c2:c3], wi[:, c3:c4],
                              wi[:, c0:c0 + KV_LORA], wi[:, c0 + KV_LORA:c1], jnp.zeros((D_MODEL, 64), BF16)], axis=1)
    wq = _unstack_cols(g_qb).reshape(Q_LORA, H_A, QK_HEAD)
    w_qb_p = jnp.concatenate([wq[:, :, :QK_NOPE].reshape(Q_LORA, H_A * QK_NOPE), wq[:, :, QK_NOPE:].reshape(Q_LORA, H_A * QK_ROPE)], axis=1)
    w_kvb = _unstack_cols(g_kvb)
    w_out_f = g_out.reshape(D_MODEL, D_MODEL)
    w_up_f = _unstack_cols(g_up)
    w_down_f = g_down.reshape(D_FF, D_MODEL)
    conv_w_f = _unstack_cols(g_cw)

    half = QK_ROPE // 2
    inv_freq = ROPE_THETA ** (-jnp.arange(half, dtype=F32) / half)
    ang = positions.astype(F32)[:, None] * inv_freq[None, :]
    cos, sin = jnp.cos(ang), jnp.sin(ang)
    qa = jnp.arange(Q_BLOCK, dtype=jnp.int32)[:, None]
    kc = jnp.arange(SPAN, dtype=jnp.int32)[None, :]
    rel = (kc - WINDOW - qa).T
    in_band = (jnp.abs(rel) <= WINDOW).astype(F32).reshape(1, Q_BLOCK * SPAN)
    onehot_t = (_t5_bucket(rel).reshape(1, Q_BLOCK * SPAN) == jnp.arange(NUM_BUCKETS, dtype=jnp.int32)[:, None]).astype(F32)
    bias_t = _bias_table(rel_bias.T, onehot_t, in_band).reshape(H_B, SPAN, Q_BLOCK)
    sinks_b = jnp.broadcast_to(sinks.reshape(H_B, 1), (H_B, Q_BLOCK))

    (h1,) = _rowwise(lambda a, g: (_rms(a, g),), "norm1", S, TM, [_rows(x), _whole(norm1_g)], [("rows", D_MODEL, BF16)])
    proj = _matmul(h1, w_in_p, "nn", F32, "proj")

    def lat_fn(qlat, ckv, kr, gq, gkv, cs, sn):
        r1, r2 = _rope(kr[:, :half], kr[:, half:QK_ROPE], cs, sn)
        return _rms(qlat, gq), _rms(ckv, gkv), jnp.concatenate([r1, r2], axis=1)

    qn, ckvn, k_rope = _rowwise(lat_fn, "latents", S, TM,
                                [_rows(proj, 256, 12), _rows(proj, 128, 30), _rows(proj, 128, 31), _whole(q_a_norm_g), _whole(kv_a_norm_g),
                                 _rows(cos), _rows(sin)],
                                [("rows", Q_LORA, BF16), ("rows", KV_LORA, BF16), ("rows", QK_ROPE, BF16)])
    q_p = _matmul(qn, w_qb_p, "nn", F32, "q_up", tn=1536)
    kv = _matmul(ckvn, w_kvb, "nn", BF16, "kv_up", tn=2048)

    def q_heads_fn(q, cs, sn):
        outs = []
        for h in range(H_A):
            o = H_A * QK_NOPE + QK_ROPE * h
            r1, r2 = _rope(q[:, o:o + half], q[:, o + half:o + QK_ROPE], cs, sn)
            outs.append(jnp.concatenate([q[:, QK_NOPE * h:QK_NOPE * (h + 1)], r1, r2], axis=1)[None])
        return (jnp.concatenate(outs, axis=0),)

    (q_full,) = _rowwise(q_heads_fn, "q_heads", S, TM, [_rows(q_p), _rows(cos), _rows(sin)], [("heads", H_A, QK_HEAD, BF16)])

    def k_heads_fn(kvb, kr):
        return (jnp.concatenate([jnp.concatenate([kvb[:, 256 * h:256 * h + QK_NOPE], kr], axis=1)[None] for h in range(H_A)], axis=0),)

    (k_full,) = _rowwise(k_heads_fn, "k_heads", S, TM, [_rows(kv), _rows(k_rope)], [("heads", H_A, QK_HEAD, BF16)])
    o_a, lse = _mla_fwd(q_full, k_full, kv, S)

    (kb, vb) = _rowwise(lambda a, b: (a, b), "kv_b_cast", S, TM, [_rows(proj, 256, 13), _rows(proj, 256, 14)],
                        [("rows", KV_B * HD_B, BF16), ("rows", KV_B * HD_B, BF16)])
    kp = jnp.pad(kb, ((WINDOW, WINDOW), (0, 0)))
    vp = jnp.pad(vb, ((WINDOW, WINDOW), (0, 0)))
    o_b = _win_fwd(proj, kp, vp, bias_t, sinks_b, S)

    (mixed,) = _rowwise(lambda ga, gb, oa, ob: (_sigmoid(ga) * oa + _sigmoid(gb) * ob,), "gate_mix", S, TM,
                        [_rows(proj, 1024, 1), _rows(proj, 1024, 2), _rows(o_a), _rows(o_b)], [("rows", D_MODEL, BF16)])
    x1 = _matmul(mixed, w_out_f, "nn", F32, "out_proj", residual=x)
    (h2,) = _rowwise(lambda a, g: (_rms(a, g),), "norm2", S, TM, [_rows(x1), _whole(norm2_g)], [("rows", D_MODEL, BF16)])
    u = _matmul(h2, w_up_f, "nn", F32, "ffn_up", tn=1408)
    act = _conv_gate_fwd(u, conv_w_f, conv_b, S)
    x2 = _matmul(act, w_down_f, "nn", F32, "ffn_down", residual=x1, tm=512)

    def final_fn(a, g, t):
        err = _rms(a, g) - t
        loss = 0.5 * jnp.sum(jnp.mean(err * err, axis=-1, keepdims=True), axis=0, keepdims=True)
        dx, dg = _rms_bwd(err * (1.0 / D_MODEL), a, g)
        return dx, dx, dg, jnp.broadcast_to(loss, (1, 128))

    gfin = final_norm_g.reshape(1, D_MODEL)
    dx2, dx2_b, d_gfin, loss_row = _rowwise(final_fn, "loss_head", S, TM, [_rows(x2), _whole(gfin), _rows(target)],
                                            [("rows", D_MODEL, F32), ("rows", D_MODEL, BF16), ("acc", 1, D_MODEL), ("acc", 1, 128)])
    d_act = _matmul(dx2_b, w_down_f, "nt", F32, "ffn_down_dx", tn=1408)
    d_w_down = _matmul(act, dx2_b, "tn", F32, "ffn_down_dw")
    du_g, du_v, d_conv_w, d_conv_b = _conv_gate_bwd(u, conv_w_f, conv_b, d_act, S)
    du = jnp.concatenate([du_g, du_v], axis=1)
    d_h2 = _matmul(du, w_up_f, "nt", F32, "ffn_up_dx", tm=512)
    d_w_up = _matmul(h2, du, "tn", F32, "ffn_up_dw", tm=512, tn=1408)

    def norm_bwd_fn(dh, a, g, dres):
        dx, dg = _rms_bwd(dh, a, g)
        dx = dx + dres
        return dx, dx, dg

    dx1, dx1_b, d_g2 = _rowwise(norm_bwd_fn, "norm2_bwd", S, TM, [_rows(d_h2), _rows(x1), _whole(norm2_g), _rows(dx2)],
                                [("rows", D_MODEL, F32), ("rows", D_MODEL, BF16), ("acc", 1, D_MODEL)])
    d_mixed = _matmul(dx1_b, w_out_f, "nt", F32, "out_proj_dx")
    d_w_out = _matmul(mixed, dx1_b, "tn", F32, "out_proj_dw", tm=512)

    def gate_bwd_fn(dm, ga, gb, oa, ob):
        sa, sb = _sigmoid(ga), _sigmoid(gb)
        return jnp.concatenate([dm * oa * sa * (1.0 - sa), dm * ob * sb * (1.0 - sb)], axis=1), dm * sa, dm * sb

    d_gates, do_a, do_b = _rowwise(gate_bwd_fn, "gate_bwd", S, TM,
                                   [_rows(d_mixed), _rows(proj, 1024, 1), _rows(proj, 1024, 2), _rows(o_a), _rows(o_b)],
                                   [("rows", 2 * D_MODEL, BF16), ("rows", D_MODEL, BF16), ("rows", D_MODEL, BF16)])

    dq_b, dk_acc, dv_acc, d_bias, d_sinks_g = _win_bwd(proj, kp, vp, bias_t, sinks_b, do_b, S)
    d_rel_bias = _bias_table_bwd(d_bias.reshape(H_B, Q_BLOCK * SPAN), onehot_t).T
    d_sinks = d_sinks_g.reshape(1, H_B)

    dq_full, dk_full, dv_full = _mla_bwd(q_full, k_full, kv, do_a, o_a, lse, S)

    def dq_post_fn(dq, cs, sn):
        nope = [dq[h, :, :QK_NOPE] for h in range(H_A)]
        rope = []
        for h in range(H_A):
            rope += list(_rope_bwd(dq[h, :, QK_NOPE:QK_NOPE + half], dq[h, :, QK_NOPE + half:], cs, sn))
        return (jnp.concatenate(nope + rope, axis=1),)

    (dq_p,) = _rowwise(dq_post_fn, "dq_post", S, TM, [_heads(dq_full), _rows(cos), _rows(sin)], [("rows", H_A * QK_HEAD, BF16)])

    def dkv_post_fn(dk, dv, cs, sn):
        dkv = jnp.concatenate([jnp.concatenate([dk[h, :, :QK_NOPE], dv[h]], axis=1) for h in range(H_A)], axis=1)
        dkr = dk[0, :, QK_NOPE:]
        for h in range(1, H_A):
            dkr = dkr + dk[h, :, QK_NOPE:]
        r1, r2 = _rope_bwd(dkr[:, :half], dkr[:, half:], cs, sn)
        return dkv, jnp.concatenate([r1, r2], axis=1)

    dkv, d_krope = _rowwise(dkv_post_fn, "dkv_post", S, TM, [_heads(dk_full), _heads(dv_full), _rows(cos), _rows(sin)],
                            [("rows", H_A * (QK_NOPE + V_DIM), BF16), ("rows", QK_ROPE, F32)])
    d_qn = _matmul(dq_p, w_qb_p, "nt", F32, "q_up_dx")
    d_w_qb_p = _matmul(qn, dq_p, "tn", F32, "q_up_dw", tn=1536)
    d_ckvn = _matmul(dkv, w_kvb, "nt", F32, "kv_up_dx")
    d_w_kvb = _matmul(ckvn, dkv, "tn", F32, "kv_up_dw", tn=2048)

    def lat_bwd_fn(dqn, dckvn, dkr, qlat, ckv, gq, gkv, dkb, dvb):
        dql, dgq = _rms_bwd(dqn, qlat, gq)
        dck, dgkv = _rms_bwd(dckvn, ckv, gkv)
        tail = jnp.concatenate([dql, dkb, dvb, dck, dkr, jnp.zeros_like(dkr)], axis=1)
        return tail, dgq, dgkv

    shifted = lambda arr: (arr, lambda tm: pl.BlockSpec((tm, arr.shape[1]), lambda i: (i + WINDOW // tm, 0)))
    TL = min(128, S)
    d_tail, d_gq, d_gkv = _rowwise(lat_bwd_fn, "latents_bwd", S, TL,
                                   [_rows(d_qn), _rows(d_ckvn), _rows(d_krope), _rows(proj, 256, 12), _rows(proj, 128, 30),
                                    _whole(q_a_norm_g), _whole(kv_a_norm_g), shifted(dk_acc), shifted(dv_acc)],
                                   [("rows", 1024, BF16), ("acc", 1, Q_LORA), ("acc", 1, KV_LORA)])
    d_proj = jnp.concatenate([dq_b, d_gates, d_tail], axis=1)
    d_h1 = _matmul(d_proj, w_in_p, "nt", F32, "proj_dx", tm=512)
    d_w_in_p = _matmul(h1, d_proj, "tn", F32, "proj_dw", tm=512)
    grad_x, _, d_g1 = _rowwise(norm_bwd_fn, "norm1_bwd", S, TM, [_rows(d_h1), _rows(x), _whole(norm1_g), _rows(dx1)],
                               [("rows", D_MODEL, F32), ("rows", D_MODEL, BF16), ("acc", 1, D_MODEL)])

    dp = d_w_in_p
    d_w_in = jnp.concatenate([dp[:, 3072:3328], dp[:, 3840:3968], dp[:, 3968:4032], dp[:, 0:1024], dp[:, 3328:3584],
                              dp[:, 3584:3840], dp[:, 1024:2048], dp[:, 2048:3072]], axis=1)
    d_w_qb = jnp.concatenate([d_w_qb_p[:, :H_A * QK_NOPE].reshape(Q_LORA, H_A, QK_NOPE),
                              d_w_qb_p[:, H_A * QK_NOPE:].reshape(Q_LORA, H_A, QK_ROPE)], axis=2).reshape(Q_LORA, H_A * QK_HEAD)
    stacked = [_stack_cols(d_w_in).astype(BF16), _stack_cols(d_w_qb).astype(BF16), _stack_cols(d_w_kvb).astype(BF16),
               d_w_out.reshape(N_DEV, D_MODEL // N_DEV, D_MODEL).astype(BF16), _stack_cols(d_w_up).astype(BF16),
               d_w_down.reshape(N_DEV, D_FF // N_DEV, D_MODEL).astype(BF16), _stack_cols(d_conv_w)]
    small_parts = [d_g1, d_gq, d_gkv, d_rel_bias.reshape(1, NUM_BUCKETS * H_B), d_sinks, d_g2, d_conv_b, d_gfin]
    small = jnp.concatenate(small_parts, axis=1)
    n_small = small.shape[1]
    pad = (-n_small) % 128
    small = jnp.pad(small, ((0, 0), (0, pad)))
    *recv, recv_small = _exchange(stacked, [small])

    def flat(a):
        return a.reshape(1, -1)

    small_w = [norm1_g, q_a_norm_g, kv_a_norm_g, rel_bias, sinks, norm2_g, conv_b, final_norm_g]
    small_m = [m_norm1_g, m_q_a_norm_g, m_kv_a_norm_g, m_rel_bias, m_sinks, m_norm2_g, m_conv_b, m_final_norm_g]
    small_v = [v_norm1_g, v_q_a_norm_g, v_kv_a_norm_g, v_rel_bias, v_sinks, v_norm2_g, v_conv_b, v_final_norm_g]
    cat = lambda parts: jnp.pad(jnp.concatenate([flat(a) for a in parts], axis=1), ((0, 0), (0, pad)))
    sm = _adamw(recv_small, cat(small_w), cat(small_m), jnp.pad(jnp.concatenate([flat(a) for a in small_v], axis=1), ((0, 0), (0, pad)), constant_values=1.0), "adamw_small")
    big_names = ["w_in", "w_q_b", "w_kv_b", "w_out", "w_up", "w_down", "conv_w"]
    big_w = [w_in, w_q_b, w_kv_b, w_out, w_up, w_down, conv_w]
    big_m = [m_w_in, m_w_q_b, m_w_kv_b, m_w_out, m_w_up, m_w_down, m_conv_w]
    big_v = [v_w_in, v_w_q_b, v_w_kv_b, v_w_out, v_w_up, v_w_down, v_conv_w]
    big = {n: _adamw(r, w[0], m[0], v[0], "adamw_" + n) for n, r, w, m, v in zip(big_names, recv, big_w, big_m, big_v)}

    loss = lax.psum(loss_row[0, 0], ("x", "y", "c"))
    order = ["norm1_g", "w_in", "q_a_norm_g", "w_q_b", "kv_a_norm_g", "w_kv_b", "rel_bias", "sinks", "w_out", "norm2_g", "w_up",
             "conv_w", "conv_b", "w_down", "final_norm_g"]
    small_names = ["norm1_g", "q_a_norm_g", "kv_a_norm_g", "rel_bias", "sinks", "norm2_g", "conv_b", "final_norm_g"]
    offs, o = {}, 0
    for n, a in zip(small_names, small_w):
        offs[n] = (o, a.size, a.shape)
        o += a.size
    outs = [loss, grad_x[None]]
    for kind in range(4):
        for n in order:
            if n in big:
                outs.append(big[n][kind][None])
            else:
                o, size, shape = offs[n]
                outs.append(sm[kind][0, o:o + size].reshape(shape))
    return tuple(outs)
```

```python
import math

import jax
import jax.numpy as jnp
from jax import lax
from jax.experimental import pallas as pl
from jax.experimental.pallas import tpu as pltpu

F32 = jnp.float32
BF16 = jnp.bfloat16

N_DEV = 8
D_MODEL = 1024
EPS = 1e-6
H_A, QK_NOPE, QK_ROPE, V_DIM, Q_LORA, KV_LORA = 8, 128, 64, 128, 256, 128
QK_HEAD = QK_NOPE + QK_ROPE
ROPE_THETA = 10000.0
H_B, KV_B, GROUP, HD_B, WINDOW, Q_BLOCK = 16, 4, 4, 64, 128, 128
SPAN = Q_BLOCK + 2 * WINDOW
NUM_BUCKETS, MAX_DISTANCE = 32, 128
D_FF = 2816
ADAM_LR, ADAM_B1, ADAM_B2, ADAM_EPS, ADAM_WD, ADAM_STEP = 0.001, 0.9, 0.999, 1e-08, 0.01, 10

W_IN_SIZES = (Q_LORA, KV_LORA + QK_ROPE, H_B * HD_B, KV_B * HD_B, KV_B * HD_B, D_MODEL, D_MODEL)
W_IN_COLS = sum(W_IN_SIZES)
PROJ_P = 4096

VMEM_LIMIT = 56 * 1024 * 1024

NN = (((1,), (0,)), ((), ()))
NT = (((1,), (1,)), ((), ()))
TN = (((0,), (0,)), ((), ()))


def _pcall(body, *, name, grid, in_specs, out_specs, out_shape, scratch_shapes=(), dims=None, comm=None):
    if comm is None:
        params = pltpu.CompilerParams(dimension_semantics=dims, vmem_limit_bytes=VMEM_LIMIT)
        return pl.pallas_call(body, name=name, grid=grid, in_specs=in_specs, out_specs=out_specs, out_shape=out_shape,
                              scratch_shapes=list(scratch_shapes), compiler_params=params)
    stacked, replicated = comm
    arrs = [*stacked, *replicated]
    n_st, n_arr = len(stacked), len(arrs)
    single = not isinstance(out_specs, (list, tuple))
    o_specs, o_shape = ([out_specs], [out_shape]) if single else (list(out_specs), list(out_shape))
    n_in, n_out = len(in_specs), len(o_specs)

    def wrapped(*refs):
        c_in = refs[n_in:n_in + n_arr]
        c_out = refs[n_in + n_arr + n_out:n_in + 2 * n_arr + n_out]
        sems = refs[len(refs) - 3:]
        own = (*refs[:n_in], *refs[n_in + n_arr:n_in + n_arr + n_out], *refs[n_in + 2 * n_arr + n_out:len(refs) - 3])
        if not grid:
            _xchg_start(c_in, c_out, sems, n_st)
            _xchg_finish(c_in, c_out, sems, n_st)
            return
        first = last = None
        for d, n in enumerate(grid):
            pid = pl.program_id(d)
            first = (pid == 0) if first is None else first & (pid == 0)
            last = (pid == n - 1) if last is None else last & (pid == n - 1)

        @pl.when(first)
        def _():
            _xchg_start(c_in, c_out, sems, n_st)

        body(*own)

        @pl.when(last)
        def _():
            _xchg_finish(c_in, c_out, sems, n_st)

    params = pltpu.CompilerParams(dimension_semantics=("arbitrary",) * len(grid), vmem_limit_bytes=VMEM_LIMIT)
    call = pl.pallas_call(wrapped, name=name, grid=grid, in_specs=[*in_specs, *[ANY] * n_arr], out_specs=[*o_specs, *[ANY] * n_arr],
                          out_shape=[*o_shape, *_xchg_out_shapes(stacked, replicated)],
                          scratch_shapes=[*scratch_shapes, *_xchg_sems(n_arr)], compiler_params=params)

    def run(*args):
        res = call(*args, *arrs)
        outs, landed = res[:n_out], res[n_out:]
        return (outs[0] if single else outs), landed

    return run


def _dot(a, b, dn):
    return lax.dot_general(a, b, dn, preferred_element_type=F32)


def _tile(n, target):
    best = None
    for t in range(128, min(n, target) + 1, 128):
        if n % t == 0:
            best = t
    return n if best is None else best


def _matmul(a, b, mode, out_dtype, name, residual=None, tm=1024, tn=1024, comm=None):
    if mode == "nn":
        (M, K), N = a.shape, b.shape[1]
    elif mode == "nt":
        (M, K), N = a.shape, b.shape[0]
    else:
        (K, M), N = a.shape, b.shape[1]
    tm, tn = _tile(M, tm), _tile(N, tn)
    a_spec = pl.BlockSpec((K, tm), lambda i, j: (0, i)) if mode == "tn" else pl.BlockSpec((tm, K), lambda i, j: (i, 0))
    b_spec = pl.BlockSpec((tn, K), lambda i, j: (j, 0)) if mode == "nt" else pl.BlockSpec((K, tn), lambda i, j: (0, j))
    o_spec = pl.BlockSpec((tm, tn), lambda i, j: (i, j))
    in_specs, args = [a_spec, b_spec], [a, b]
    if residual is not None:
        in_specs.append(o_spec)
        args.append(residual)
    scratch = [pltpu.VMEM((tm, K), a.dtype)] if mode == "tn" else []

    def body(*refs):
        a_ref, b_ref = refs[0], refs[1]
        o_ref = refs[len(args)]
        if mode == "tn":
            at_ref = refs[len(args) + 1]

            @pl.when(pl.program_id(1) == 0)
            def _():
                at_ref[...] = a_ref[...].T

            acc = _dot(at_ref[...], b_ref[...], NN)
        else:
            acc = _dot(a_ref[...], b_ref[...], NT if mode == "nt" else NN)
        if residual is not None:
            acc = acc + refs[2][...]
        o_ref[...] = acc.astype(out_dtype)

    return _pcall(body, name=name, grid=(M // tm, N // tn), in_specs=in_specs, out_specs=o_spec,
                  out_shape=jax.ShapeDtypeStruct((M, N), out_dtype), scratch_shapes=scratch,
                  dims=("parallel", "arbitrary"), comm=comm)(*args)


def _rows(arr, width=None, col=0):
    width = arr.shape[1] if width is None else width
    return (arr, lambda tm: pl.BlockSpec((tm, width), lambda i: (i, col)))


def _heads(arr):
    return (arr, lambda tm: pl.BlockSpec((arr.shape[0], tm, arr.shape[2]), lambda i: (0, i, 0)))


def _whole(arr):
    nd = arr.ndim
    return (arr, lambda tm: pl.BlockSpec(arr.shape, lambda i: (0,) * nd))


def _rowwise(fn, name, n_rows, tm, ins, outs):
    tm = min(tm, n_rows)
    assert n_rows % tm == 0
    in_specs = [mk(tm) for _, mk in ins]
    out_specs, out_shape, is_acc = [], [], []
    for o in outs:
        if o[0] == "rows":
            out_specs.append(pl.BlockSpec((tm, o[1]), lambda i: (i, 0)))
            out_shape.append(jax.ShapeDtypeStruct((n_rows, o[1]), o[2]))
        elif o[0] == "heads":
            out_specs.append(pl.BlockSpec((o[1], tm, o[2]), lambda i: (0, i, 0)))
            out_shape.append(jax.ShapeDtypeStruct((o[1], n_rows, o[2]), o[3]))
        else:
            out_specs.append(pl.BlockSpec((o[1], o[2]), lambda i: (0, 0)))
            out_shape.append(jax.ShapeDtypeStruct((o[1], o[2]), F32))
        is_acc.append(o[0] == "acc")
    n_in = len(ins)

    def body(*refs):
        vals = fn(*[r[...] for r in refs[:n_in]])
        for r, v, acc in zip(refs[n_in:], vals, is_acc):
            if acc:
                @pl.when(pl.program_id(0) == 0)
                def _():
                    r[...] = jnp.zeros_like(r)

                r[...] += v
            else:
                r[...] = v.astype(r.dtype)

    return _pcall(body, name=name, grid=(n_rows // tm,), in_specs=in_specs, out_specs=out_specs,
                  out_shape=out_shape, dims=("arbitrary",))(*[a for a, _ in ins])


def _rms(x, g):
    r = lax.rsqrt(jnp.mean(x * x, axis=-1, keepdims=True) + EPS)
    return x * r * g


def _rms_bwd(dy, x, g):
    r = lax.rsqrt(jnp.mean(x * x, axis=-1, keepdims=True) + EPS)
    xhat = x * r
    dxhat = dy * g
    dx = r * (dxhat - xhat * jnp.mean(dxhat * xhat, axis=-1, keepdims=True))
    return dx, jnp.sum(dy * xhat, axis=0, keepdims=True)


def _rope(x1, x2, cos, sin):
    return x1 * cos - x2 * sin, x2 * cos + x1 * sin


def _rope_bwd(d1, d2, cos, sin):
    return d1 * cos + d2 * sin, d2 * cos - d1 * sin


def _sigmoid(x):
    return 1.0 / (1.0 + jnp.exp(-x))


def _mla_fwd(q_full, k_full, kv, S, comm=None):
    tq = min(256, S)
    scale = 1.0 / math.sqrt(QK_HEAD)

    def body(q_ref, k_ref, v_ref, o_ref, lse_ref):
        s = _dot(q_ref[0], k_ref[0], NT) * scale
        m = jnp.max(s, axis=-1, keepdims=True)
        p = jnp.exp(s - m)
        l = jnp.sum(p, axis=-1, keepdims=True)
        o_ref[...] = _dot(p.astype(BF16), v_ref[...], NN) / l
        lse_ref[0] = m + jnp.log(l)

    return _pcall(
        body, name="mla_fwd", grid=(H_A, S // tq),
        in_specs=[pl.BlockSpec((1, tq, QK_HEAD), lambda h, i: (h, i, 0)),
                  pl.BlockSpec((1, S, QK_HEAD), lambda h, i: (h, 0, 0)),
                  pl.BlockSpec((S, V_DIM), lambda h, i: (0, 2 * h + 1))],
        out_specs=[pl.BlockSpec((tq, V_DIM), lambda h, i: (i, h)),
                   pl.BlockSpec((1, tq, 1), lambda h, i: (h, i, 0))],
        out_shape=[jax.ShapeDtypeStruct((S, H_A * V_DIM), F32), jax.ShapeDtypeStruct((H_A, S, 1), F32)],
        dims=("parallel", "parallel"), comm=comm)(q_full, k_full, kv)


def _mla_bwd(q_full, k_full, kv, do_a, o_a, lse, S, comm=None):
    tq = min(256, S)
    kc = min(1024, S)
    scale = 1.0 / math.sqrt(QK_HEAD)

    def body(q_ref, k_ref, v_ref, do_ref, o_ref, lse_ref, dq_ref, dk_ref, dv_ref):
        @pl.when(pl.program_id(1) == 0)
        def _():
            dk_ref[...] = jnp.zeros_like(dk_ref)
            dv_ref[...] = jnp.zeros_like(dv_ref)

        q = q_ref[0]
        do = do_ref[...]
        lse_q = lse_ref[0]
        delta = jnp.sum(do.astype(F32) * o_ref[...], axis=-1, keepdims=True)
        dq = jnp.zeros((tq, QK_HEAD), F32)
        for c in range(S // kc):
            k = k_ref[0, c * kc:(c + 1) * kc, :]
            v = v_ref[c * kc:(c + 1) * kc, :]
            p = jnp.exp(_dot(q, k, NT) * scale - lse_q)
            ds = (p * (_dot(do, v, NT) - delta) * scale).astype(BF16)
            dq = dq + _dot(ds, k, NN)
            dk_ref[0, c * kc:(c + 1) * kc, :] += _dot(ds, q, TN)
            dv_ref[0, c * kc:(c + 1) * kc, :] += _dot(p.astype(BF16), do, TN)
        dq_ref[0] = dq

    return _pcall(
        body, name="mla_bwd", grid=(H_A, S // tq),
        in_specs=[pl.BlockSpec((1, tq, QK_HEAD), lambda h, i: (h, i, 0)),
                  pl.BlockSpec((1, S, QK_HEAD), lambda h, i: (h, 0, 0)),
                  pl.BlockSpec((S, V_DIM), lambda h, i: (0, 2 * h + 1)),
                  pl.BlockSpec((tq, V_DIM), lambda h, i: (i, h)),
                  pl.BlockSpec((tq, V_DIM), lambda h, i: (i, h)),
                  pl.BlockSpec((1, tq, 1), lambda h, i: (h, i, 0))],
        out_specs=[pl.BlockSpec((1, tq, QK_HEAD), lambda h, i: (h, i, 0)),
                   pl.BlockSpec((1, S, QK_HEAD), lambda h, i: (h, 0, 0)),
                   pl.BlockSpec((1, S, V_DIM), lambda h, i: (h, 0, 0))],
        out_shape=[jax.ShapeDtypeStruct((H_A, S, QK_HEAD), F32), jax.ShapeDtypeStruct((H_A, S, QK_HEAD), F32),
                   jax.ShapeDtypeStruct((H_A, S, V_DIM), F32)],
        dims=("parallel", "arbitrary"), comm=comm)(q_full, k_full, kv, do_a, o_a, lse)


WIN_SCALE = 1.0 / math.sqrt(HD_B)


def _win_specs(S):
    qspec = pl.BlockSpec((Q_BLOCK, H_B * HD_B), lambda n: (n, 0))
    kspecs = [pl.BlockSpec((Q_BLOCK, KV_B * HD_B), lambda n, d=d: (n + d, 0)) for d in range(3)]
    bias_spec = pl.BlockSpec((H_B, SPAN, Q_BLOCK), lambda n: (0, 0, 0))
    sink_spec = pl.BlockSpec((H_B, Q_BLOCK), lambda n: (0, 0))
    return qspec, kspecs, bias_spec, sink_spec


def _win_edge_ok(n, n_blk):
    row = lax.broadcasted_iota(jnp.int32, (SPAN, 1), 0)
    return jnp.logical_not(((n == 0) & (row < WINDOW)) | ((n == n_blk - 1) & (row >= SPAN - WINDOW)))


def _lanes4(pieces):
    return jnp.concatenate(pieces, axis=1)


def _win_probs(kg, q4t, bias_ref, sink_ref, g, edge_ok):
    bias4 = _lanes4([bias_ref[GROUP * g + j] for j in range(GROUP)])
    sink4 = _lanes4([sink_ref[GROUP * g + j:GROUP * g + j + 1, :] for j in range(GROUP)])
    s = jnp.where(edge_ok, _dot(kg, q4t, NN) + bias4, -1e30)
    m = jnp.maximum(jnp.max(s, axis=0, keepdims=True), sink4)
    p = jnp.exp(s - m)
    e_sink = jnp.exp(sink4 - m)
    inv_l = 1.0 / (jnp.sum(p, axis=0, keepdims=True) + e_sink)
    return p * inv_l, e_sink * inv_l


def _group_t(xt, g):
    return _lanes4([xt[HD_B * (GROUP * g + j):HD_B * (GROUP * g + j + 1), :] for j in range(GROUP)])


def _win_fwd(proj, kp, vp, bias_t, sinks_b, S):
    n_blk = S // Q_BLOCK
    qspec, kspecs, bias_spec, sink_spec = _win_specs(S)

    def body(q_ref, k0, k1, k2, v0, v1, v2, bias_ref, sink_ref, o_ref):
        n = pl.program_id(0)
        edge_ok = _win_edge_ok(n, n_blk)
        k = jnp.concatenate([k0[...], k1[...], k2[...]], axis=0)
        vt = jnp.concatenate([v0[...], v1[...], v2[...]], axis=0).T
        qt = (q_ref[...] * WIN_SCALE).T.astype(BF16)
        parts = []
        for g in range(KV_B):
            p, _ = _win_probs(k[:, HD_B * g:HD_B * (g + 1)], _group_t(qt, g), bias_ref, sink_ref, g, edge_ok)
            o4t = _dot(vt[HD_B * g:HD_B * (g + 1), :], p.astype(BF16), NN)
            parts += [o4t[:, Q_BLOCK * j:Q_BLOCK * (j + 1)] for j in range(GROUP)]
        o_ref[...] = jnp.concatenate(parts, axis=0).T

    return _pcall(body, name="win_fwd", grid=(n_blk,),
                  in_specs=[qspec, *kspecs, *kspecs, bias_spec, sink_spec],
                  out_specs=pl.BlockSpec((Q_BLOCK, H_B * HD_B), lambda n: (n, 0)),
                  out_shape=jax.ShapeDtypeStruct((S, H_B * HD_B), F32),
                  dims=("parallel",))(proj, kp, kp, kp, vp, vp, vp, bias_t, sinks_b)


def _win_bwd(proj, kp, vp, bias_t, sinks_b, do_b, S):
    n_blk = S // Q_BLOCK
    qspec, kspecs, bias_spec, sink_spec = _win_specs(S)

    def body(q_ref, k0, k1, k2, v0, v1, v2, bias_ref, sink_ref, do_ref, dq_ref, dk_ref, dv_ref, dbias_ref, dsink_ref, dsink_acc):
        n = pl.program_id(0)

        @pl.when(n == 0)
        def _():
            dk_ref[...] = jnp.zeros_like(dk_ref)
            dv_ref[...] = jnp.zeros_like(dv_ref)
            dbias_ref[...] = jnp.zeros_like(dbias_ref)
            dsink_acc[...] = jnp.zeros_like(dsink_acc)

        edge_ok = _win_edge_ok(n, n_blk)
        k = jnp.concatenate([k0[...], k1[...], k2[...]], axis=0)
        v = jnp.concatenate([v0[...], v1[...], v2[...]], axis=0)
        kt = k.T
        qt = (q_ref[...] * WIN_SCALE).T.astype(BF16)
        dot_ = do_ref[...].astype(F32).T.astype(BF16)
        dq_parts, dks, dvs = [], [], []
        for g in range(KV_B):
            kg, vg = k[:, HD_B * g:HD_B * (g + 1)], v[:, HD_B * g:HD_B * (g + 1)]
            q4t, do4t = _group_t(qt, g), _group_t(dot_, g)
            p, p_sink = _win_probs(kg, q4t, bias_ref, sink_ref, g, edge_ok)
            dp = _dot(vg, do4t, NN)
            delta = jnp.sum(p * dp, axis=0, keepdims=True)
            ds = p * (dp - delta)
            for j in range(GROUP):
                dbias_ref[GROUP * g + j] += ds[:, Q_BLOCK * j:Q_BLOCK * (j + 1)]
            dsink_acc[g:g + 1, :] += -p_sink * delta
            dsb = ds.astype(BF16)
            dq4t = _dot(kt[HD_B * g:HD_B * (g + 1), :], dsb, NN) * WIN_SCALE
            dq_parts += [dq4t[:, Q_BLOCK * j:Q_BLOCK * (j + 1)] for j in range(GROUP)]
            dks.append(_dot(dsb, q4t, NT))
            dvs.append(_dot(p.astype(BF16), do4t, NT))
        dq_ref[...] = jnp.concatenate(dq_parts, axis=0).T.astype(BF16)
        rows = pl.ds(pl.multiple_of(n * Q_BLOCK, Q_BLOCK), SPAN)
        dk_ref[rows, :] += jnp.concatenate(dks, axis=1)
        dv_ref[rows, :] += jnp.concatenate(dvs, axis=1)

        @pl.when(n == n_blk - 1)
        def _():
            acc = dsink_acc[...]
            dsink_ref[...] = jnp.concatenate(
                [jnp.sum(acc[:, Q_BLOCK * j:Q_BLOCK * (j + 1)], axis=1, keepdims=True) for j in range(GROUP)], axis=1)

    whole = lambda shape: pl.BlockSpec(shape, lambda n: (0,) * len(shape))
    return _pcall(
        body, name="win_bwd", grid=(n_blk,),
        in_specs=[qspec, *kspecs, *kspecs, bias_spec, sink_spec, pl.BlockSpec((Q_BLOCK, H_B * HD_B), lambda n: (n, 0))],
        out_specs=[pl.BlockSpec((Q_BLOCK, H_B * HD_B), lambda n: (n, 0)), whole((S + 2 * WINDOW, KV_B * HD_B)),
                   whole((S + 2 * WINDOW, KV_B * HD_B)), whole((H_B, SPAN, Q_BLOCK)), whole((KV_B, GROUP))],
        out_shape=[jax.ShapeDtypeStruct((S, H_B * HD_B), BF16), jax.ShapeDtypeStruct((S + 2 * WINDOW, KV_B * HD_B), F32),
                   jax.ShapeDtypeStruct((S + 2 * WINDOW, KV_B * HD_B), F32), jax.ShapeDtypeStruct((H_B, SPAN, Q_BLOCK), F32),
                   jax.ShapeDtypeStruct((KV_B, GROUP), F32)],
        scratch_shapes=[pltpu.VMEM((KV_B, GROUP * Q_BLOCK), F32)],
        dims=("arbitrary",))(proj, kp, kp, kp, vp, vp, vp, bias_t, sinks_b, do_b)


def _bias_table(rel_bias_t, onehot_t, in_band):
    def body(rb_ref, oh_ref, band_ref, o_ref):
        t = lax.dot_general(rb_ref[...], oh_ref[...], NN, preferred_element_type=F32, precision=lax.Precision.HIGHEST)
        o_ref[...] = jnp.where(band_ref[...] > 0.5, t, -1e30)

    n = onehot_t.shape[1]
    tn = _tile(n, 8192)
    return _pcall(body, name="bias_table", grid=(n // tn,),
                  in_specs=[pl.BlockSpec((H_B, NUM_BUCKETS), lambda j: (0, 0)), pl.BlockSpec((NUM_BUCKETS, tn), lambda j: (0, j)),
                            pl.BlockSpec((1, tn), lambda j: (0, j))],
                  out_specs=pl.BlockSpec((H_B, tn), lambda j: (0, j)),
                  out_shape=jax.ShapeDtypeStruct((H_B, n), F32), dims=("parallel",))(rel_bias_t, onehot_t, in_band)


def _bias_table_bwd(dbias, onehot_t):
    n = onehot_t.shape[1]
    tk = _tile(n, 8192)

    def body(d_ref, oh_ref, o_ref):
        @pl.when(pl.program_id(0) == 0)
        def _():
            o_ref[...] = jnp.zeros_like(o_ref)

        o_ref[...] += lax.dot_general(d_ref[...], oh_ref[...], NT, preferred_element_type=F32, precision=lax.Precision.HIGHEST)

    return _pcall(body, name="bias_table_bwd", grid=(n // tk,),
                  in_specs=[pl.BlockSpec((H_B, tk), lambda j: (0, j)), pl.BlockSpec((NUM_BUCKETS, tk), lambda j: (0, j))],
                  out_specs=pl.BlockSpec((H_B, NUM_BUCKETS), lambda j: (0, 0)),
                  out_shape=jax.ShapeDtypeStruct((H_B, NUM_BUCKETS), F32), dims=("arbitrary",))(dbias, onehot_t)


def _shift_down(u):
    row = lax.broadcasted_iota(jnp.int32, u.shape, 0)
    return jnp.where(row == 0, 0.0, pltpu.roll(u, 1, axis=0))


def _shift_up(u):
    n = u.shape[0]
    row = lax.broadcasted_iota(jnp.int32, u.shape, 0)
    return jnp.where(row == n - 1, 0.0, pltpu.roll(u, n - 1, axis=0))


def _conv(u, w, b):
    return _shift_down(u) * w[0:1, :] + u * w[1:2, :] + _shift_up(u) * w[2:3, :] + b


CONV_STRIP = 128


def _conv_specs(S):
    nj = D_FF // CONV_STRIP
    ug = pl.BlockSpec((S, CONV_STRIP), lambda j: (0, j))
    uv = pl.BlockSpec((S, CONV_STRIP), lambda j: (0, j + nj))
    wg = pl.BlockSpec((3, CONV_STRIP), lambda j: (0, j))
    wv = pl.BlockSpec((3, CONV_STRIP), lambda j: (0, j + nj))
    bg = pl.BlockSpec((1, CONV_STRIP), lambda j: (0, j))
    bv = pl.BlockSpec((1, CONV_STRIP), lambda j: (0, j + nj))
    return nj, ug, uv, wg, wv, bg, bv


def _conv_gate_fwd(u, conv_w, conv_b, S):
    nj, ug, uv, wg, wv, bg, bv = _conv_specs(S)

    def body(ug_ref, uv_ref, wg_ref, wv_ref, bg_ref, bv_ref, a_ref):
        g = _conv(ug_ref[...], wg_ref[...], bg_ref[...])
        val = _conv(uv_ref[...], wv_ref[...], bv_ref[...])
        a_ref[...] = (g * _sigmoid(g) * val).astype(BF16)

    return _pcall(body, name="conv_gate_fwd", grid=(nj,), in_specs=[ug, uv, wg, wv, bg, bv],
                  out_specs=pl.BlockSpec((S, CONV_STRIP), lambda j: (0, j)),
                  out_shape=jax.ShapeDtypeStruct((S, D_FF), BF16), dims=("parallel",))(u, u, conv_w, conv_w, conv_b, conv_b)


def _conv_gate_bwd(u, conv_w, conv_b, da, S):
    nj, ug, uv, wg, wv, bg, bv = _conv_specs(S)

    def conv_bwd(duc, u_in, w):
        du = _shift_up(duc) * w[0:1, :] + duc * w[1:2, :] + _shift_down(duc) * w[2:3, :]
        dw = jnp.concatenate([jnp.sum(duc * _shift_down(u_in), axis=0, keepdims=True),
                              jnp.sum(duc * u_in, axis=0, keepdims=True),
                              jnp.sum(duc * _shift_up(u_in), axis=0, keepdims=True)], axis=0)
        return du, dw, jnp.sum(duc, axis=0, keepdims=True)

    def body(ug_ref, uv_ref, wg_ref, wv_ref, bg_ref, bv_ref, da_ref, dug_ref, duv_ref, dwg_ref, dwv_ref, dbg_ref, dbv_ref):
        u_g, u_v, w_g, w_v = ug_ref[...], uv_ref[...], wg_ref[...], wv_ref[...]
        g = _conv(u_g, w_g, bg_ref[...])
        val = _conv(u_v, w_v, bv_ref[...])
        da = da_ref[...]
        sg = _sigmoid(g)
        dval = da * (g * sg)
        dg = da * val * (sg * (1.0 + g * (1.0 - sg)))
        du_g, dw_g, db_g = conv_bwd(dg, u_g, w_g)
        du_v, dw_v, db_v = conv_bwd(dval, u_v, w_v)
        dug_ref[...] = du_g.astype(BF16)
        duv_ref[...] = du_v.astype(BF16)
        dwg_ref[...] = dw_g
        dwv_ref[...] = dw_v
        dbg_ref[...] = db_g
        dbv_ref[...] = db_v

    strip = lambda r, dt: (pl.BlockSpec((r, CONV_STRIP), lambda j: (0, j)), jax.ShapeDtypeStruct((r, D_FF), dt))
    outs = [strip(S, BF16), strip(S, BF16), strip(3, F32), strip(3, F32), strip(1, F32), strip(1, F32)]
    du_g, du_v, dw_g, dw_v, db_g, db_v = _pcall(
        body, name="conv_gate_bwd", grid=(nj,),
        in_specs=[ug, uv, wg, wv, bg, bv, pl.BlockSpec((S, CONV_STRIP), lambda j: (0, j))],
        out_specs=[o[0] for o in outs], out_shape=[o[1] for o in outs],
        dims=("parallel",))(u, u, conv_w, conv_w, conv_b, conv_b, da)
    return du_g, du_v, jnp.concatenate([dw_g, dw_v], axis=1), jnp.concatenate([db_g, db_v], axis=1)


MESH = pl.DeviceIdType.MESH
ANY = pl.BlockSpec(memory_space=pl.ANY)


def _place():
    return lax.axis_index("x"), lax.axis_index("y"), lax.axis_index("c")


def _all_gather(shards):
    n_arr = len(shards)

    def body(*refs):
        ins, outs = refs[:n_arr], refs[n_arr:2 * n_arr]
        send_sems, recv_sems, local_sems = refs[2 * n_arr:]
        x, y, c = _place()
        me, sibling = (x, y, c), (x, y, 1 - c)
        chips = [(1 - x, y), (x, 1 - y), (1 - x, 1 - y)]

        def slot(a, p):
            return outs[a].at[4 * p[0] + 2 * p[1] + p[2]]

        def copy(a, k, block, to, src=None):
            return pltpu.make_async_remote_copy(
                src_ref=slot(a, block) if src is None else src, dst_ref=slot(a, block),
                send_sem=send_sems.at[a, k], recv_sem=recv_sems.at[a, k], device_id=to, device_id_type=MESH)

        mine = [pltpu.make_async_copy(ins[a], slot(a, me), local_sems.at[a]) for a in range(n_arr)]
        for cp in mine:
            cp.start()
        first = []
        for a in range(n_arr):
            first.append(copy(a, 0, me, sibling, src=ins[a]))
            first += [copy(a, 1 + j, me, (*chip, c), src=ins[a]) for j, chip in enumerate(chips)]
        for cp in first:
            cp.start()
        passed = []
        for j, chip in enumerate(chips):
            for a in range(n_arr):
                copy(a, 1 + j, (*chip, c), me).wait_recv()
                cp = copy(a, 4 + j, (*chip, c), sibling)
                cp.start()
                passed.append(cp)
        for a in range(n_arr):
            copy(a, 0, sibling, me).wait_recv()
            for j, chip in enumerate(chips):
                copy(a, 4 + j, (*chip, 1 - c), me).wait_recv()
        for cp in first + passed:
            cp.wait_send()
        for cp in mine:
            cp.wait()

    return pl.pallas_call(
        body, name="all_gather_weights",
        in_specs=[ANY] * n_arr, out_specs=[ANY] * n_arr,
        out_shape=[jax.ShapeDtypeStruct((N_DEV, *s.shape), s.dtype) for s in shards],
        scratch_shapes=[pltpu.SemaphoreType.DMA((n_arr, 7)), pltpu.SemaphoreType.DMA((n_arr, 7)), pltpu.SemaphoreType.DMA((n_arr,))],
    )(*shards)


def _xchg_out_shapes(stacked, replicated):
    return ([jax.ShapeDtypeStruct(s.shape, s.dtype) for s in stacked]
            + [jax.ShapeDtypeStruct((N_DEV, *r.shape), r.dtype) for r in replicated])


def _xchg_sems(n_arr):
    return [pltpu.SemaphoreType.DMA((n_arr, 7)), pltpu.SemaphoreType.DMA((n_arr, 7)), pltpu.SemaphoreType.DMA((n_arr,))]


def _xchg_copies(ins, outs, sems, n_st, with_recv):
    send_sems, recv_sems, local_sems = sems
    n_arr = len(ins)
    x, y, c = _place()
    me = 4 * x + 2 * y + c

    def src(a, idx):
        return ins[a].at[idx] if a < n_st else ins[a]

    mine = [pltpu.make_async_copy(src(a, me), outs[a].at[me], local_sems.at[a]) for a in range(n_arr)]
    pairs = []
    for k in range(1, N_DEV):
        px, py, pc = x ^ (k >> 2), y ^ ((k >> 1) & 1), c ^ (k & 1)
        peer = 4 * px + 2 * py + pc
        for a in range(n_arr):
            sems_k = dict(send_sem=send_sems.at[a, k - 1], recv_sem=recv_sems.at[a, k - 1], device_id_type=MESH)
            send = pltpu.make_async_remote_copy(src_ref=src(a, peer), dst_ref=outs[a].at[me], device_id=(px, py, pc), **sems_k)
            recv = None
            if with_recv:
                recv = pltpu.make_async_remote_copy(src_ref=src(a, peer), dst_ref=outs[a].at[peer], device_id=(x, y, c), **sems_k)
            pairs.append((send, recv))
    return mine, pairs


def _xchg_start(ins, outs, sems, n_st):
    mine, pairs = _xchg_copies(ins, outs, sems, n_st, False)
    for cp in mine:
        cp.start()
    for send, _ in pairs:
        send.start()


def _xchg_finish(ins, outs, sems, n_st):
    mine, pairs = _xchg_copies(ins, outs, sems, n_st, True)
    for _, recv in pairs:
        recv.wait_recv()
    for send, _ in pairs:
        send.wait_send()
    for cp in mine:
        cp.wait()


def _exchange(stacked, replicated, name):
    _, landed = _pcall(lambda: None, name=name, grid=(), in_specs=[], out_specs=[], out_shape=[], comm=(stacked, replicated))()
    return landed


def _adamw(parts, w, m, v, name):
    R, C = w.shape
    tr = R if (R <= 512 or R % 256) else 256

    def body(p_ref, w_ref, m_ref, v_ref, g_ref, d_ref, nm_ref, nv_ref):
        g = p_ref[0].astype(F32)
        for s in range(1, N_DEV):
            g = g + p_ref[s].astype(F32)
        m2 = ADAM_B1 * m_ref[...] + (1.0 - ADAM_B1) * g
        v2 = ADAM_B2 * v_ref[...] + (1.0 - ADAM_B2) * (g * g)
        m_hat = m2 / (1.0 - ADAM_B1 ** ADAM_STEP)
        v_hat = v2 / (1.0 - ADAM_B2 ** ADAM_STEP)
        g_ref[...] = g
        d_ref[...] = -ADAM_LR * (m_hat / (jnp.sqrt(v_hat) + ADAM_EPS) + ADAM_WD * w_ref[...])
        nm_ref[...] = m2
        nv_ref[...] = v2

    blk = pl.BlockSpec((tr, C), lambda i: (i, 0))
    return _pcall(body, name=name, grid=(R // tr,),
                  in_specs=[pl.BlockSpec((N_DEV, tr, C), lambda i: (0, i, 0)), blk, blk, blk],
                  out_specs=[blk] * 4, out_shape=[jax.ShapeDtypeStruct((R, C), F32)] * 4,
                  dims=("parallel",))(parts, w, m, v)


def _t5_bucket(rel):
    nb = NUM_BUCKETS // 2
    max_exact = nb // 2
    base = (rel > 0).astype(jnp.int32) * nb
    n = jnp.abs(rel)
    nf = jnp.maximum(n, 1).astype(jnp.float32)
    large = max_exact + (jnp.log(nf / max_exact) / math.log(MAX_DISTANCE / max_exact) * (nb - max_exact)).astype(jnp.int32)
    large = jnp.minimum(large, nb - 1)
    return base + jnp.where(n < max_exact, n, large)


def _unstack_cols(g):
    return jnp.transpose(g, (1, 0, 2)).reshape(g.shape[1], N_DEV * g.shape[2])


def _stack_cols(w):
    R = w.shape[0]
    return jnp.transpose(w.reshape(R, N_DEV, w.shape[1] // N_DEV), (1, 0, 2))


def kernel(x, positions, norm1_g, w_in, q_a_norm_g, w_q_b, kv_a_norm_g, w_kv_b, rel_bias, sinks, w_out, norm2_g, w_up, conv_w, conv_b, w_down, final_norm_g, loss_target, m_norm1_g, m_w_in, m_q_a_norm_g, m_w_q_b, m_kv_a_norm_g, m_w_kv_b, m_rel_bias, m_sinks, m_w_out, m_norm2_g, m_w_up, m_conv_w, m_conv_b, m_w_down, m_final_norm_g, v_norm1_g, v_w_in, v_q_a_norm_g, v_w_q_b, v_kv_a_norm_g, v_w_kv_b, v_rel_bias, v_sinks, v_w_out, v_norm2_g, v_w_up, v_conv_w, v_conv_b, v_w_down, v_final_norm_g):
    S = x.shape[1]
    x = x[0]
    target = loss_target[0]
    TM = 256

    g_in, g_qb, g_kvb = _all_gather([w_in[0].astype(BF16), w_q_b[0].astype(BF16), w_kv_b[0].astype(BF16)])
    late_weights = [w_out[0].astype(BF16), w_up[0].astype(BF16), w_down[0].astype(BF16), conv_w[0]]
    wi = _unstack_cols(g_in)
    c0, c1, c2, c3, c4, c5 = (sum(W_IN_SIZES[:i + 1]) for i in range(6))
    w_in_p = jnp.concatenate([wi[:, c1:c2], wi[:, c4:c5], wi[:, c5:], wi[:, :c0], wi[:, c2:c3], wi[:, c3:c4],
                              wi[:, c0:c0 + KV_LORA], wi[:, c0 + KV_LORA:c1], jnp.zeros((D_MODEL, 64), BF16)], axis=1)
    wq = _unstack_cols(g_qb).reshape(Q_LORA, H_A, QK_HEAD)
    w_qb_p = jnp.concatenate([wq[:, :, :QK_NOPE].reshape(Q_LORA, H_A * QK_NOPE), wq[:, :, QK_NOPE:].reshape(Q_LORA, H_A * QK_ROPE)], axis=1)
    w_kvb = _unstack_cols(g_kvb)

    half = QK_ROPE // 2
    inv_freq = ROPE_THETA ** (-jnp.arange(half, dtype=F32) / half)
    ang = positions.astype(F32)[:, None] * inv_freq[None, :]
    cos, sin = jnp.cos(ang), jnp.sin(ang)
    qa = jnp.arange(Q_BLOCK, dtype=jnp.int32)[:, None]
    kc = jnp.arange(SPAN, dtype=jnp.int32)[None, :]
    rel = (kc - WINDOW - qa).T
    in_band = (jnp.abs(rel) <= WINDOW).astype(F32).reshape(1, Q_BLOCK * SPAN)
    onehot_t = (_t5_bucket(rel).reshape(1, Q_BLOCK * SPAN) == jnp.arange(NUM_BUCKETS, dtype=jnp.int32)[:, None]).astype(F32)
    bias_t = _bias_table(rel_bias.T, onehot_t, in_band).reshape(H_B, SPAN, Q_BLOCK)
    sinks_b = jnp.broadcast_to(sinks.reshape(H_B, 1), (H_B, Q_BLOCK))

    (h1,) = _rowwise(lambda a, g: (_rms(a, g),), "norm1", S, TM, [_rows(x), _whole(norm1_g)], [("rows", D_MODEL, BF16)])
    proj = _matmul(h1, w_in_p, "nn", F32, "proj")

    def lat_fn(qlat, ckv, kr, gq, gkv, cs, sn):
        r1, r2 = _rope(kr[:, :half], kr[:, half:QK_ROPE], cs, sn)
        return _rms(qlat, gq), _rms(ckv, gkv), jnp.concatenate([r1, r2], axis=1)

    qn, ckvn, k_rope = _rowwise(lat_fn, "latents", S, TM,
                                [_rows(proj, 256, 12), _rows(proj, 128, 30), _rows(proj, 128, 31), _whole(q_a_norm_g), _whole(kv_a_norm_g),
                                 _rows(cos), _rows(sin)],
                                [("rows", Q_LORA, BF16), ("rows", KV_LORA, BF16), ("rows", QK_ROPE, BF16)])
    q_p = _matmul(qn, w_qb_p, "nn", F32, "q_up", tn=1536)
    kv = _matmul(ckvn, w_kvb, "nn", BF16, "kv_up", tn=2048)

    def q_heads_fn(q, cs, sn):
        outs = []
        for h in range(H_A):
            o = H_A * QK_NOPE + QK_ROPE * h
            r1, r2 = _rope(q[:, o:o + half], q[:, o + half:o + QK_ROPE], cs, sn)
            outs.append(jnp.concatenate([q[:, QK_NOPE * h:QK_NOPE * (h + 1)], r1, r2], axis=1)[None])
        return (jnp.concatenate(outs, axis=0),)

    (q_full,) = _rowwise(q_heads_fn, "q_heads", S, TM, [_rows(q_p), _rows(cos), _rows(sin)], [("heads", H_A, QK_HEAD, BF16)])

    def k_heads_fn(kvb, kr):
        return (jnp.concatenate([jnp.concatenate([kvb[:, 256 * h:256 * h + QK_NOPE], kr], axis=1)[None] for h in range(H_A)], axis=0),)

    (k_full,) = _rowwise(k_heads_fn, "k_heads", S, TM, [_rows(kv), _rows(k_rope)], [("heads", H_A, QK_HEAD, BF16)])
    (o_a, lse), (g_out, g_up, g_down, g_cw) = _mla_fwd(q_full, k_full, kv, S, comm=([], late_weights))
    w_out_f = g_out.reshape(D_MODEL, D_MODEL)
    w_up_f = _unstack_cols(g_up)
    w_down_f = g_down.reshape(D_FF, D_MODEL)
    conv_w_f = _unstack_cols(g_cw)

    (kb, vb) = _rowwise(lambda a, b: (a, b), "kv_b_cast", S, TM, [_rows(proj, 256, 13), _rows(proj, 256, 14)],
                        [("rows", KV_B * HD_B, BF16), ("rows", KV_B * HD_B, BF16)])
    kp = jnp.pad(kb, ((WINDOW, WINDOW), (0, 0)))
    vp = jnp.pad(vb, ((WINDOW, WINDOW), (0, 0)))
    o_b = _win_fwd(proj, kp, vp, bias_t, sinks_b, S)

    (mixed,) = _rowwise(lambda ga, gb, oa, ob: (_sigmoid(ga) * oa + _sigmoid(gb) * ob,), "gate_mix", S, TM,
                        [_rows(proj, 1024, 1), _rows(proj, 1024, 2), _rows(o_a), _rows(o_b)], [("rows", D_MODEL, BF16)])
    x1 = _matmul(mixed, w_out_f, "nn", F32, "out_proj", residual=x)
    (h2,) = _rowwise(lambda a, g: (_rms(a, g),), "norm2", S, TM, [_rows(x1), _whole(norm2_g)], [("rows", D_MODEL, BF16)])
    u = _matmul(h2, w_up_f, "nn", F32, "ffn_up", tn=1408)
    act = _conv_gate_fwd(u, conv_w_f, conv_b, S)
    x2 = _matmul(act, w_down_f, "nn", F32, "ffn_down", residual=x1, tm=512)

    def final_fn(a, g, t):
        err = _rms(a, g) - t
        loss = 0.5 * jnp.sum(jnp.mean(err * err, axis=-1, keepdims=True), axis=0, keepdims=True)
        dx, dg = _rms_bwd(err * (1.0 / D_MODEL), a, g)
        return dx, dx, dg, jnp.broadcast_to(loss, (1, 128))

    gfin = final_norm_g.reshape(1, D_MODEL)
    dx2, dx2_b, d_gfin, loss_row = _rowwise(final_fn, "loss_head", S, TM, [_rows(x2), _whole(gfin), _rows(target)],
                                            [("rows", D_MODEL, F32), ("rows", D_MODEL, BF16), ("acc", 1, D_MODEL), ("acc", 1, 128)])
    d_act = _matmul(dx2_b, w_down_f, "nt", F32, "ffn_down_dx", tn=1408)
    d_w_down = _matmul(act, dx2_b, "tn", F32, "ffn_down_dw")
    du_g, du_v, d_conv_w, d_conv_b = _conv_gate_bwd(u, conv_w_f, conv_b, d_act, S)
    du = jnp.concatenate([du_g, du_v], axis=1)
    d_h2 = _matmul(du, w_up_f, "nt", F32, "ffn_up_dx", tm=512)
    d_w_up = _matmul(h2, du, "tn", F32, "ffn_up_dw", tm=512, tn=1408)

    def norm_bwd_fn(dh, a, g, dres):
        dx, dg = _rms_bwd(dh, a, g)
        dx = dx + dres
        return dx, dx, dg

    dx1, dx1_b, d_g2 = _rowwise(norm_bwd_fn, "norm2_bwd", S, TM, [_rows(d_h2), _rows(x1), _whole(norm2_g), _rows(dx2)],
                                [("rows", D_MODEL, F32), ("rows", D_MODEL, BF16), ("acc", 1, D_MODEL)])
    d_mixed = _matmul(dx1_b, w_out_f, "nt", F32, "out_proj_dx")
    d_w_out = _matmul(mixed, dx1_b, "tn", F32, "out_proj_dw", tm=512)

    def gate_bwd_fn(dm, ga, gb, oa, ob):
        sa, sb = _sigmoid(ga), _sigmoid(gb)
        return jnp.concatenate([dm * oa * sa * (1.0 - sa), dm * ob * sb * (1.0 - sb)], axis=1), dm * sa, dm * sb

    d_gates, do_a, do_b = _rowwise(gate_bwd_fn, "gate_bwd", S, TM,
                                   [_rows(d_mixed), _rows(proj, 1024, 1), _rows(proj, 1024, 2), _rows(o_a), _rows(o_b)],
                                   [("rows", 2 * D_MODEL, BF16), ("rows", D_MODEL, BF16), ("rows", D_MODEL, BF16)])

    dq_b, dk_acc, dv_acc, d_bias, d_sinks_g = _win_bwd(proj, kp, vp, bias_t, sinks_b, do_b, S)
    d_rel_bias = _bias_table_bwd(d_bias.reshape(H_B, Q_BLOCK * SPAN), onehot_t).T
    d_sinks = d_sinks_g.reshape(1, H_B)

    early = [d_w_out.reshape(N_DEV, D_MODEL // N_DEV, D_MODEL).astype(BF16), _stack_cols(d_w_up).astype(BF16),
             d_w_down.reshape(N_DEV, D_FF // N_DEV, D_MODEL).astype(BF16), _stack_cols(d_conv_w)]
    (dq_full, dk_full, dv_full), recv_early = _mla_bwd(q_full, k_full, kv, do_a, o_a, lse, S, comm=(early, []))

    def dq_post_fn(dq, cs, sn):
        nope = [dq[h, :, :QK_NOPE] for h in range(H_A)]
        rope = []
        for h in range(H_A):
            rope += list(_rope_bwd(dq[h, :, QK_NOPE:QK_NOPE + half], dq[h, :, QK_NOPE + half:], cs, sn))
        return (jnp.concatenate(nope + rope, axis=1),)

    (dq_p,) = _rowwise(dq_post_fn, "dq_post", S, TM, [_heads(dq_full), _rows(cos), _rows(sin)], [("rows", H_A * QK_HEAD, BF16)])

    def dkv_post_fn(dk, dv, cs, sn):
        dkv = jnp.concatenate([jnp.concatenate([dk[h, :, :QK_NOPE], dv[h]], axis=1) for h in range(H_A)], axis=1)
        dkr = dk[0, :, QK_NOPE:]
        for h in range(1, H_A):
            dkr = dkr + dk[h, :, QK_NOPE:]
        r1, r2 = _rope_bwd(dkr[:, :half], dkr[:, half:], cs, sn)
        return dkv, jnp.concatenate([r1, r2], axis=1)

    dkv, d_krope = _rowwise(dkv_post_fn, "dkv_post", S, TM, [_heads(dk_full), _heads(dv_full), _rows(cos), _rows(sin)],
                            [("rows", H_A * (QK_NOPE + V_DIM), BF16), ("rows", QK_ROPE, F32)])
    d_qn = _matmul(dq_p, w_qb_p, "nt", F32, "q_up_dx")
    d_w_qb_p = _matmul(qn, dq_p, "tn", F32, "q_up_dw", tn=1536)
    d_ckvn = _matmul(dkv, w_kvb, "nt", F32, "kv_up_dx")
    d_w_kvb = _matmul(ckvn, dkv, "tn", F32, "kv_up_dw", tn=2048)

    def lat_bwd_fn(dqn, dckvn, dkr, qlat, ckv, gq, gkv, dkb, dvb):
        dql, dgq = _rms_bwd(dqn, qlat, gq)
        dck, dgkv = _rms_bwd(dckvn, ckv, gkv)
        tail = jnp.concatenate([dql, dkb, dvb, dck, dkr, jnp.zeros_like(dkr)], axis=1)
        return tail, dgq, dgkv

    shifted = lambda arr: (arr, lambda tm: pl.BlockSpec((tm, arr.shape[1]), lambda i: (i + WINDOW // tm, 0)))
    TL = min(128, S)
    d_tail, d_gq, d_gkv = _rowwise(lat_bwd_fn, "latents_bwd", S, TL,
                                   [_rows(d_qn), _rows(d_ckvn), _rows(d_krope), _rows(proj, 256, 12), _rows(proj, 128, 30),
                                    _whole(q_a_norm_g), _whole(kv_a_norm_g), shifted(dk_acc), shifted(dv_acc)],
                                   [("rows", 1024, BF16), ("acc", 1, Q_LORA), ("acc", 1, KV_LORA)])
    d_proj = jnp.concatenate([dq_b, d_gates, d_tail], axis=1)
    d_w_in_p = _matmul(h1, d_proj, "tn", F32, "proj_dw", tm=512)

    dp = d_w_in_p
    d_w_in = jnp.concatenate([dp[:, 3072:3328], dp[:, 3840:3968], dp[:, 3968:4032], dp[:, 0:1024], dp[:, 3328:3584],
                              dp[:, 3584:3840], dp[:, 1024:2048], dp[:, 2048:3072]], axis=1)
    d_w_qb = jnp.concatenate([d_w_qb_p[:, :H_A * QK_NOPE].reshape(Q_LORA, H_A, QK_NOPE),
                              d_w_qb_p[:, H_A * QK_NOPE:].reshape(Q_LORA, H_A, QK_ROPE)], axis=2).reshape(Q_LORA, H_A * QK_HEAD)
    late = [_stack_cols(d_w_in).astype(BF16), _stack_cols(d_w_qb).astype(BF16), _stack_cols(d_w_kvb).astype(BF16)]
    d_h1, recv_late = _matmul(d_proj, w_in_p, "nt", F32, "proj_dx", tm=512, comm=(late, []))
    grad_x, _, d_g1 = _rowwise(norm_bwd_fn, "norm1_bwd", S, TM, [_rows(d_h1), _rows(x), _whole(norm1_g), _rows(dx1)],
                               [("rows", D_MODEL, F32), ("rows", D_MODEL, BF16), ("acc", 1, D_MODEL)])
    recv = [*recv_late, *recv_early]

    small_parts = [d_g1, d_gq, d_gkv, d_rel_bias.reshape(1, NUM_BUCKETS * H_B), d_sinks, d_g2, d_conv_b, d_gfin]
    small = jnp.concatenate(small_parts, axis=1)
    n_small = small.shape[1]
    pad = (-n_small) % 128
    small = jnp.pad(small, ((0, 0), (0, pad)))
    (recv_small,) = _exchange([], [small], "exchange_small_grads")

    def flat(a):
        return a.reshape(1, -1)

    small_w = [norm1_g, q_a_norm_g, kv_a_norm_g, rel_bias, sinks, norm2_g, conv_b, final_norm_g]
    small_m = [m_norm1_g, m_q_a_norm_g, m_kv_a_norm_g, m_rel_bias, m_sinks, m_norm2_g, m_conv_b, m_final_norm_g]
    small_v = [v_norm1_g, v_q_a_norm_g, v_kv_a_norm_g, v_rel_bias, v_sinks, v_norm2_g, v_conv_b, v_final_norm_g]
    cat = lambda parts: jnp.pad(jnp.concatenate([flat(a) for a in parts], axis=1), ((0, 0), (0, pad)))
    sm = _adamw(recv_small, cat(small_w), cat(small_m), jnp.pad(jnp.concatenate([flat(a) for a in small_v], axis=1), ((0, 0), (0, pad)), constant_values=1.0), "adamw_small")
    big_names = ["w_in", "w_q_b", "w_kv_b", "w_out", "w_up", "w_down", "conv_w"]
    big_w = [w_in, w_q_b, w_kv_b, w_out, w_up, w_down, conv_w]
    big_m = [m_w_in, m_w_q_b, m_w_kv_b, m_w_out, m_w_up, m_w_down, m_conv_w]
    big_v = [v_w_in, v_w_q_b, v_w_kv_b, v_w_out, v_w_up, v_w_down, v_conv_w]
    big = {n: _adamw(r, w[0], m[0], v[0], "adamw_" + n) for n, r, w, m, v in zip(big_names, recv, big_w, big_m, big_v)}

    loss = lax.psum(loss_row[0, 0], ("x", "y", "c"))
    order = ["norm1_g", "w_in", "q_a_norm_g", "w_q_b", "kv_a_norm_g", "w_kv_b", "rel_bias", "sinks", "w_out", "norm2_g", "w_up",
             "conv_w", "conv_b", "w_down", "final_norm_g"]
    small_names = ["norm1_g", "q_a_norm_g", "kv_a_norm_g", "rel_bias", "sinks", "norm2_g", "conv_b", "final_norm_g"]
    offs, o = {}, 0
    for n, a in zip(small_names, small_w):
        offs[n] = (o, a.size, a.shape)
        o += a.size
    outs = [loss, grad_x[None]]
    for kind in range(4):
        for n in order:
            if n in big:
                outs.append(big[n][kind][None])
            else:
                o, size, shape = offs[n]
                outs.append(sm[kind][0, o:o + size].reshape(shape))
    return tuple(outs)
```

```python
import math

import jax
import jax.numpy as jnp
from jax import lax
from jax.experimental import pallas as pl
from jax.experimental.pallas import tpu as pltpu

F32 = jnp.float32
BF16 = jnp.bfloat16

N_DEV = 8
D_MODEL = 1024
EPS = 1e-6
H_A, QK_NOPE, QK_ROPE, V_DIM, Q_LORA, KV_LORA = 8, 128, 64, 128, 256, 128
QK_HEAD = QK_NOPE + QK_ROPE
ROPE_THETA = 10000.0
H_B, KV_B, GROUP, HD_B, WINDOW, Q_BLOCK = 16, 4, 4, 64, 128, 128
SPAN = Q_BLOCK + 2 * WINDOW
NUM_BUCKETS, MAX_DISTANCE = 32, 128
D_FF = 2816
ADAM_LR, ADAM_B1, ADAM_B2, ADAM_EPS, ADAM_WD, ADAM_STEP = 0.001, 0.9, 0.999, 1e-08, 0.01, 10

W_IN_SIZES = (Q_LORA, KV_LORA + QK_ROPE, H_B * HD_B, KV_B * HD_B, KV_B * HD_B, D_MODEL, D_MODEL)
W_IN_COLS = sum(W_IN_SIZES)
PROJ_P = 4096
PROJ_GA, PROJ_GB, PROJ_QB, PROJ_QLAT, PROJ_KB, PROJ_VB, PROJ_CKV, PROJ_KROPE = 0, 1, 2, 12, 13, 14, 30, 31

VMEM_LIMIT = 56 * 1024 * 1024

NN = (((1,), (0,)), ((), ()))
NT = (((1,), (1,)), ((), ()))
TN = (((0,), (0,)), ((), ()))


def _pcall(body, *, name, grid, in_specs, out_specs, out_shape, scratch_shapes=(), dims=None, comm=None, aliases=None):
    if comm is None:
        params = pltpu.CompilerParams(dimension_semantics=dims, vmem_limit_bytes=VMEM_LIMIT)
        return pl.pallas_call(body, name=name, grid=grid, in_specs=in_specs, out_specs=out_specs, out_shape=out_shape,
                              scratch_shapes=list(scratch_shapes), input_output_aliases=aliases or {}, compiler_params=params)
    assert not aliases
    stacked, replicated = comm
    arrs = [*stacked, *replicated]
    n_st, n_arr = len(stacked), len(arrs)
    single = not isinstance(out_specs, (list, tuple))
    o_specs, o_shape = ([out_specs], [out_shape]) if single else (list(out_specs), list(out_shape))
    n_in, n_out = len(in_specs), len(o_specs)

    def wrapped(*refs):
        c_in = refs[n_in:n_in + n_arr]
        c_out = refs[n_in + n_arr + n_out:n_in + 2 * n_arr + n_out]
        sems = refs[len(refs) - 3:]
        own = (*refs[:n_in], *refs[n_in + n_arr:n_in + n_arr + n_out], *refs[n_in + 2 * n_arr + n_out:len(refs) - 3])
        if not grid:
            _xchg_start(c_in, c_out, sems, n_st)
            _xchg_finish(c_in, c_out, sems, n_st)
            return
        first = last = None
        for d, n in enumerate(grid):
            pid = pl.program_id(d)
            first = (pid == 0) if first is None else first & (pid == 0)
            last = (pid == n - 1) if last is None else last & (pid == n - 1)

        @pl.when(first)
        def _():
            _xchg_start(c_in, c_out, sems, n_st)

        body(*own)

        @pl.when(last)
        def _():
            _xchg_finish(c_in, c_out, sems, n_st)

    params = pltpu.CompilerParams(dimension_semantics=("arbitrary",) * len(grid), vmem_limit_bytes=VMEM_LIMIT)
    call = pl.pallas_call(wrapped, name=name, grid=grid, in_specs=[*in_specs, *[ANY] * n_arr], out_specs=[*o_specs, *[ANY] * n_arr],
                          out_shape=[*o_shape, *_xchg_out_shapes(stacked, replicated)],
                          scratch_shapes=[*scratch_shapes, *_xchg_sems(n_arr)], compiler_params=params)

    def run(*args):
        res = call(*args, *arrs)
        outs, landed = res[:n_out], res[n_out:]
        return (outs[0] if single else outs), landed

    return run


def _dot(a, b, dn):
    return lax.dot_general(a, b, dn, preferred_element_type=F32)


def _tile(n, target):
    best = None
    for t in range(128, min(n, target) + 1, 128):
        if n % t == 0:
            best = t
    return n if best is None else best


def _matmul(a, b, mode, out_dtype, name, residual=None, tm=1024, tn=1024, comm=None):
    if mode == "nn":
        (M, K), N = a.shape, b.shape[1]
    elif mode == "nt":
        (M, K), N = a.shape, b.shape[0]
    else:
        (K, M), N = a.shape, b.shape[1]
    tm, tn = _tile(M, tm), _tile(N, tn)
    a_spec = pl.BlockSpec((K, tm), lambda i, j: (0, i)) if mode == "tn" else pl.BlockSpec((tm, K), lambda i, j: (i, 0))
    b_spec = pl.BlockSpec((tn, K), lambda i, j: (j, 0)) if mode == "nt" else pl.BlockSpec((K, tn), lambda i, j: (0, j))
    o_spec = pl.BlockSpec((tm, tn), lambda i, j: (i, j))
    in_specs, args = [a_spec, b_spec], [a, b]
    if residual is not None:
        in_specs.append(o_spec)
        args.append(residual)
    scratch = [pltpu.VMEM((tm, K), a.dtype)] if mode == "tn" else []

    def body(*refs):
        a_ref, b_ref = refs[0], refs[1]
        o_ref = refs[len(args)]
        if mode == "tn":
            at_ref = refs[len(args) + 1]

            @pl.when(pl.program_id(1) == 0)
            def _():
                at_ref[...] = a_ref[...].T

            acc = _dot(at_ref[...], b_ref[...], NN)
        else:
            acc = _dot(a_ref[...], b_ref[...], NT if mode == "nt" else NN)
        if residual is not None:
            acc = acc + refs[2][...]
        o_ref[...] = acc.astype(out_dtype)

    return _pcall(body, name=name, grid=(M // tm, N // tn), in_specs=in_specs, out_specs=o_spec,
                  out_shape=jax.ShapeDtypeStruct((M, N), out_dtype), scratch_shapes=scratch,
                  dims=("parallel", "arbitrary"), comm=comm)(*args)


def _rows(arr, width=None, col=0):
    width = arr.shape[1] if width is None else width
    return (arr, lambda tm: pl.BlockSpec((tm, width), lambda i: (i, col)))


def _heads(arr):
    return (arr, lambda tm: pl.BlockSpec((arr.shape[0], tm, arr.shape[2]), lambda i: (0, i, 0)))


def _whole(arr):
    nd = arr.ndim
    return (arr, lambda tm: pl.BlockSpec(arr.shape, lambda i: (0,) * nd))


def _rowwise(fn, name, n_rows, tm, ins, outs, upcast=True, into=None):
    tm = min(tm, n_rows)
    assert n_rows % tm == 0
    in_specs = [mk(tm) for _, mk in ins]
    out_specs, out_shape, is_acc = [], [], []
    for o in outs:
        if o[0] == "rows":
            out_specs.append(pl.BlockSpec((tm, o[1]), lambda i: (i, 0)))
            out_shape.append(jax.ShapeDtypeStruct((n_rows, o[1]), o[2]))
        elif o[0] == "cols":
            out_specs.append(pl.BlockSpec((tm, o[1]), lambda i, c=o[2]: (i, c)))
            out_shape.append(jax.ShapeDtypeStruct((n_rows, o[3]), o[4]))
        elif o[0] == "heads":
            out_specs.append(pl.BlockSpec((o[1], tm, o[2]), lambda i: (0, i, 0)))
            out_shape.append(jax.ShapeDtypeStruct((o[1], n_rows, o[2]), o[3]))
        else:
            out_specs.append(pl.BlockSpec((o[1], o[2]), lambda i: (0, 0)))
            out_shape.append(jax.ShapeDtypeStruct((o[1], o[2]), F32))
        is_acc.append(o[0] == "acc")
    n_in = len(ins)
    args = [a for a, _ in ins]
    aliases = {}
    if into is not None:
        in_specs.append(ANY)
        args.append(into[0])
        aliases = {n_in: into[1]}

    def load(r):
        v = r[...]
        return v.astype(F32) if upcast and v.dtype == BF16 else v

    def body(*refs):
        vals = fn(*[load(r) for r in refs[:n_in]])
        for r, v, acc in zip(refs[len(args):], vals, is_acc):
            if acc:
                @pl.when(pl.program_id(0) == 0)
                def _():
                    r[...] = jnp.zeros_like(r)

                r[...] += v
            else:
                r[...] = v.astype(r.dtype)

    return _pcall(body, name=name, grid=(n_rows // tm,), in_specs=in_specs, out_specs=out_specs,
                  out_shape=out_shape, dims=("arbitrary",), aliases=aliases)(*args)


def _rms(x, g):
    r = lax.rsqrt(jnp.mean(x * x, axis=-1, keepdims=True) + EPS)
    return x * r * g


def _rms_bwd(dy, x, g):
    r = lax.rsqrt(jnp.mean(x * x, axis=-1, keepdims=True) + EPS)
    xhat = x * r
    dxhat = dy * g
    dx = r * (dxhat - xhat * jnp.mean(dxhat * xhat, axis=-1, keepdims=True))
    return dx, jnp.sum(dy * xhat, axis=0, keepdims=True)


def _rope(x1, x2, cos, sin):
    return x1 * cos - x2 * sin, x2 * cos + x1 * sin


def _rope_bwd(d1, d2, cos, sin):
    return d1 * cos + d2 * sin, d2 * cos - d1 * sin


def _sigmoid(x):
    return 1.0 / (1.0 + jnp.exp(-x))


def _mla_fwd(q_full, k_full, kv, S, comm=None):
    tq = min(256, S)
    scale = 1.0 / math.sqrt(QK_HEAD)

    def body(q_ref, k_ref, v_ref, o_ref, lse_ref):
        s = _dot(q_ref[0], k_ref[0], NT) * scale
        m = jnp.max(s, axis=-1, keepdims=True)
        p = jnp.exp(s - m)
        l = jnp.sum(p, axis=-1, keepdims=True)
        o_ref[...] = (_dot(p.astype(BF16), v_ref[...], NN) / l).astype(BF16)
        lse_ref[0] = m + jnp.log(l)

    return _pcall(
        body, name="mla_fwd", grid=(H_A, S // tq),
        in_specs=[pl.BlockSpec((1, tq, QK_HEAD), lambda h, i: (h, i, 0)),
                  pl.BlockSpec((1, S, QK_HEAD), lambda h, i: (h, 0, 0)),
                  pl.BlockSpec((S, V_DIM), lambda h, i: (0, 2 * h + 1))],
        out_specs=[pl.BlockSpec((tq, V_DIM), lambda h, i: (i, h)),
                   pl.BlockSpec((1, tq, 1), lambda h, i: (h, i, 0))],
        out_shape=[jax.ShapeDtypeStruct((S, H_A * V_DIM), BF16), jax.ShapeDtypeStruct((H_A, S, 1), F32)],
        dims=("parallel", "parallel"), comm=comm)(q_full, k_full, kv)


def _mla_bwd(q_full, k_full, kv, do_a, o_a, lse, S, comm=None):
    tq = min(256, S)
    kc = min(1024, S)
    scale = 1.0 / math.sqrt(QK_HEAD)

    def body(q_ref, k_ref, v_ref, do_ref, o_ref, lse_ref, dq_ref, dk_ref, dv_ref):
        @pl.when(pl.program_id(1) == 0)
        def _():
            dk_ref[...] = jnp.zeros_like(dk_ref)
            dv_ref[...] = jnp.zeros_like(dv_ref)

        q = q_ref[0]
        do = do_ref[...]
        lse_q = lse_ref[0]
        delta = jnp.sum(do.astype(F32) * o_ref[...].astype(F32), axis=-1, keepdims=True)
        dq = jnp.zeros((tq, QK_HEAD), F32)
        for c in range(S // kc):
            k = k_ref[0, c * kc:(c + 1) * kc, :]
            v = v_ref[c * kc:(c + 1) * kc, :]
            p = jnp.exp(_dot(q, k, NT) * scale - lse_q)
            ds = (p * (_dot(do, v, NT) - delta) * scale).astype(BF16)
            dq = dq + _dot(ds, k, NN)
            dk_ref[0, c * kc:(c + 1) * kc, :] += _dot(ds, q, TN)
            dv_ref[0, c * kc:(c + 1) * kc, :] += _dot(p.astype(BF16), do, TN)
        dq_ref[0] = dq

    return _pcall(
        body, name="mla_bwd", grid=(H_A, S // tq),
        in_specs=[pl.BlockSpec((1, tq, QK_HEAD), lambda h, i: (h, i, 0)),
                  pl.BlockSpec((1, S, QK_HEAD), lambda h, i: (h, 0, 0)),
                  pl.BlockSpec((S, V_DIM), lambda h, i: (0, 2 * h + 1)),
                  pl.BlockSpec((tq, V_DIM), lambda h, i: (i, h)),
                  pl.BlockSpec((tq, V_DIM), lambda h, i: (i, h)),
                  pl.BlockSpec((1, tq, 1), lambda h, i: (h, i, 0))],
        out_specs=[pl.BlockSpec((1, tq, QK_HEAD), lambda h, i: (h, i, 0)),
                   pl.BlockSpec((1, S, QK_HEAD), lambda h, i: (h, 0, 0)),
                   pl.BlockSpec((1, S, V_DIM), lambda h, i: (h, 0, 0))],
        out_shape=[jax.ShapeDtypeStruct((H_A, S, QK_HEAD), F32), jax.ShapeDtypeStruct((H_A, S, QK_HEAD), F32),
                   jax.ShapeDtypeStruct((H_A, S, V_DIM), F32)],
        dims=("parallel", "arbitrary"), comm=comm)(q_full, k_full, kv, do_a, o_a, lse)


WIN_SCALE = 1.0 / math.sqrt(HD_B)


def _win_specs(S):
    qspec = pl.BlockSpec((Q_BLOCK, H_B * HD_B), lambda n: (n, PROJ_QB))
    kspecs = [pl.BlockSpec((Q_BLOCK, KV_B * HD_B), lambda n, d=d: (n + d, 0)) for d in range(3)]
    bias_spec = pl.BlockSpec((H_B, SPAN, Q_BLOCK), lambda n: (0, 0, 0))
    sink_spec = pl.BlockSpec((H_B, Q_BLOCK), lambda n: (0, 0))
    return qspec, kspecs, bias_spec, sink_spec


def _win_edge_ok(n, n_blk):
    row = lax.broadcasted_iota(jnp.int32, (SPAN, 1), 0)
    return jnp.logical_not(((n == 0) & (row < WINDOW)) | ((n == n_blk - 1) & (row >= SPAN - WINDOW)))


def _lanes4(pieces):
    return jnp.concatenate(pieces, axis=1)


def _win_probs(kg, q4t, bias_ref, sink_ref, g, edge_ok):
    bias4 = _lanes4([bias_ref[GROUP * g + j] for j in range(GROUP)])
    sink4 = _lanes4([sink_ref[GROUP * g + j:GROUP * g + j + 1, :] for j in range(GROUP)])
    s = jnp.where(edge_ok, _dot(kg, q4t, NN) + bias4, -1e30)
    m = jnp.maximum(jnp.max(s, axis=0, keepdims=True), sink4)
    p = jnp.exp(s - m)
    e_sink = jnp.exp(sink4 - m)
    inv_l = 1.0 / (jnp.sum(p, axis=0, keepdims=True) + e_sink)
    return p * inv_l, e_sink * inv_l


def _group_t(xt, g):
    return _lanes4([xt[HD_B * (GROUP * g + j):HD_B * (GROUP * g + j + 1), :] for j in range(GROUP)])


def _win_fwd(proj, kp, vp, bias_t, sinks_b, S):
    n_blk = S // Q_BLOCK
    qspec, kspecs, bias_spec, sink_spec = _win_specs(S)

    def body(q_ref, k0, k1, k2, v0, v1, v2, bias_ref, sink_ref, o_ref):
        n = pl.program_id(0)
        edge_ok = _win_edge_ok(n, n_blk)
        k = jnp.concatenate([k0[...], k1[...], k2[...]], axis=0)
        vt = jnp.concatenate([v0[...], v1[...], v2[...]], axis=0).T
        qt = (q_ref[...].astype(F32) * WIN_SCALE).T.astype(BF16)
        parts = []
        for g in range(KV_B):
            p, _ = _win_probs(k[:, HD_B * g:HD_B * (g + 1)], _group_t(qt, g), bias_ref, sink_ref, g, edge_ok)
            o4t = _dot(vt[HD_B * g:HD_B * (g + 1), :], p.astype(BF16), NN)
            parts += [o4t[:, Q_BLOCK * j:Q_BLOCK * (j + 1)] for j in range(GROUP)]
        o_ref[...] = jnp.concatenate(parts, axis=0).T.astype(BF16)

    return _pcall(body, name="win_fwd", grid=(n_blk,),
                  in_specs=[qspec, *kspecs, *kspecs, bias_spec, sink_spec],
                  out_specs=pl.BlockSpec((Q_BLOCK, H_B * HD_B), lambda n: (n, 0)),
                  out_shape=jax.ShapeDtypeStruct((S, H_B * HD_B), BF16),
                  dims=("parallel",))(proj, kp, kp, kp, vp, vp, vp, bias_t, sinks_b)


def _win_bwd(proj, kp, vp, bias_t, sinks_b, do_b, d_proj, S):
    n_blk = S // Q_BLOCK
    qspec, kspecs, bias_spec, sink_spec = _win_specs(S)

    def body(q_ref, k0, k1, k2, v0, v1, v2, bias_ref, sink_ref, do_ref, _, dq_ref, dk_ref, dv_ref, dbias_ref, dsink_ref, dsink_acc):
        n = pl.program_id(0)

        @pl.when(n == 0)
        def _():
            dk_ref[...] = jnp.zeros_like(dk_ref)
            dv_ref[...] = jnp.zeros_like(dv_ref)
            dbias_ref[...] = jnp.zeros_like(dbias_ref)
            dsink_acc[...] = jnp.zeros_like(dsink_acc)

        edge_ok = _win_edge_ok(n, n_blk)
        k = jnp.concatenate([k0[...], k1[...], k2[...]], axis=0)
        v = jnp.concatenate([v0[...], v1[...], v2[...]], axis=0)
        kt = k.T
        qt = (q_ref[...].astype(F32) * WIN_SCALE).T.astype(BF16)
        dot_ = do_ref[...].astype(F32).T.astype(BF16)
        dq_parts, dks, dvs = [], [], []
        for g in range(KV_B):
            kg, vg = k[:, HD_B * g:HD_B * (g + 1)], v[:, HD_B * g:HD_B * (g + 1)]
            q4t, do4t = _group_t(qt, g), _group_t(dot_, g)
            p, p_sink = _win_probs(kg, q4t, bias_ref, sink_ref, g, edge_ok)
            dp = _dot(vg, do4t, NN)
            delta = jnp.sum(p * dp, axis=0, keepdims=True)
            ds = p * (dp - delta)
            for j in range(GROUP):
                dbias_ref[GROUP * g + j] += ds[:, Q_BLOCK * j:Q_BLOCK * (j + 1)]
            dsink_acc[g:g + 1, :] += -p_sink * delta
            dsb = ds.astype(BF16)
            dq4t = _dot(kt[HD_B * g:HD_B * (g + 1), :], dsb, NN) * WIN_SCALE
            dq_parts += [dq4t[:, Q_BLOCK * j:Q_BLOCK * (j + 1)] for j in range(GROUP)]
            dks.append(_dot(dsb, q4t, NT))
            dvs.append(_dot(p.astype(BF16), do4t, NT))
        dq_ref[...] = jnp.concatenate(dq_parts, axis=0).T.astype(BF16)
        rows = pl.ds(pl.multiple_of(n * Q_BLOCK, Q_BLOCK), SPAN)
        dk_ref[rows, :] += jnp.concatenate(dks, axis=1)
        dv_ref[rows, :] += jnp.concatenate(dvs, axis=1)

        @pl.when(n == n_blk - 1)
        def _():
            acc = dsink_acc[...]
            dsink_ref[...] = jnp.concatenate(
                [jnp.sum(acc[:, Q_BLOCK * j:Q_BLOCK * (j + 1)], axis=1, keepdims=True) for j in range(GROUP)], axis=1)

    whole = lambda shape: pl.BlockSpec(shape, lambda n: (0,) * len(shape))
    return _pcall(
        body, name="win_bwd", grid=(n_blk,),
        in_specs=[qspec, *kspecs, *kspecs, bias_spec, sink_spec, pl.BlockSpec((Q_BLOCK, H_B * HD_B), lambda n: (n, 0)), ANY],
        out_specs=[qspec, whole((S + 2 * WINDOW, KV_B * HD_B)),
                   whole((S + 2 * WINDOW, KV_B * HD_B)), whole((H_B, SPAN, Q_BLOCK)), whole((KV_B, GROUP))],
        out_shape=[jax.ShapeDtypeStruct((S, PROJ_P), BF16), jax.ShapeDtypeStruct((S + 2 * WINDOW, KV_B * HD_B), F32),
                   jax.ShapeDtypeStruct((S + 2 * WINDOW, KV_B * HD_B), F32), jax.ShapeDtypeStruct((H_B, SPAN, Q_BLOCK), F32),
                   jax.ShapeDtypeStruct((KV_B, GROUP), F32)],
        scratch_shapes=[pltpu.VMEM((KV_B, GROUP * Q_BLOCK), F32)],
        dims=("arbitrary",), aliases={10: 0})(proj, kp, kp, kp, vp, vp, vp, bias_t, sinks_b, do_b, d_proj)


def _bias_table(rel_bias_t, onehot_t, in_band):
    def body(rb_ref, oh_ref, band_ref, o_ref):
        t = lax.dot_general(rb_ref[...], oh_ref[...], NN, preferred_element_type=F32, precision=lax.Precision.HIGHEST)
        o_ref[...] = jnp.where(band_ref[...] > 0.5, t, -1e30)

    n = onehot_t.shape[1]
    tn = _tile(n, 8192)
    return _pcall(body, name="bias_table", grid=(n // tn,),
                  in_specs=[pl.BlockSpec((H_B, NUM_BUCKETS), lambda j: (0, 0)), pl.BlockSpec((NUM_BUCKETS, tn), lambda j: (0, j)),
                            pl.BlockSpec((1, tn), lambda j: (0, j))],
                  out_specs=pl.BlockSpec((H_B, tn), lambda j: (0, j)),
                  out_shape=jax.ShapeDtypeStruct((H_B, n), F32), dims=("parallel",))(rel_bias_t, onehot_t, in_band)


def _bias_table_bwd(dbias, onehot_t):
    n = onehot_t.shape[1]
    tk = _tile(n, 8192)

    def body(d_ref, oh_ref, o_ref):
        @pl.when(pl.program_id(0) == 0)
        def _():
            o_ref[...] = jnp.zeros_like(o_ref)

        o_ref[...] += lax.dot_general(d_ref[...], oh_ref[...], NT, preferred_element_type=F32, precision=lax.Precision.HIGHEST)

    return _pcall(body, name="bias_table_bwd", grid=(n // tk,),
                  in_specs=[pl.BlockSpec((H_B, tk), lambda j: (0, j)), pl.BlockSpec((NUM_BUCKETS, tk), lambda j: (0, j))],
                  out_specs=pl.BlockSpec((H_B, NUM_BUCKETS), lambda j: (0, 0)),
                  out_shape=jax.ShapeDtypeStruct((H_B, NUM_BUCKETS), F32), dims=("arbitrary",))(dbias, onehot_t)


def _shift_down(u):
    row = lax.broadcasted_iota(jnp.int32, u.shape, 0)
    return jnp.where(row == 0, 0.0, pltpu.roll(u, 1, axis=0))


def _shift_up(u):
    n = u.shape[0]
    row = lax.broadcasted_iota(jnp.int32, u.shape, 0)
    return jnp.where(row == n - 1, 0.0, pltpu.roll(u, n - 1, axis=0))


def _conv(u, w, b):
    return _shift_down(u) * w[0:1, :] + u * w[1:2, :] + _shift_up(u) * w[2:3, :] + b


CONV_STRIP = 128
N_STRIPS = D_FF // CONV_STRIP


def _interleave(w):
    return jnp.transpose(w.reshape(w.shape[0], 2, N_STRIPS, CONV_STRIP), (0, 2, 1, 3)).reshape(w.shape[0], 2 * D_FF)


def _deinterleave(w):
    return jnp.transpose(w.reshape(w.shape[0], N_STRIPS, 2, CONV_STRIP), (0, 2, 1, 3)).reshape(w.shape[0], 2 * D_FF)


def _pair(rows):
    return pl.BlockSpec((rows, 2 * CONV_STRIP), lambda j: (0, j))


def _conv_gate_fwd(u, conv_w, conv_b, S):
    def body(u_ref, w_ref, b_ref, a_ref):
        u, w, b = u_ref[...].astype(F32), w_ref[...], b_ref[...]
        g = _conv(u[:, :CONV_STRIP], w[:, :CONV_STRIP], b[:, :CONV_STRIP])
        val = _conv(u[:, CONV_STRIP:], w[:, CONV_STRIP:], b[:, CONV_STRIP:])
        a_ref[...] = (g * _sigmoid(g) * val).astype(BF16)

    return _pcall(body, name="conv_gate_fwd", grid=(N_STRIPS,), in_specs=[_pair(S), _pair(3), _pair(1)],
                  out_specs=pl.BlockSpec((S, CONV_STRIP), lambda j: (0, j)),
                  out_shape=jax.ShapeDtypeStruct((S, D_FF), BF16), dims=("parallel",))(u, conv_w, conv_b)


def _conv_gate_bwd(u, conv_w, conv_b, da, S):
    def conv_bwd(duc, u_in, w):
        du = _shift_up(duc) * w[0:1, :] + duc * w[1:2, :] + _shift_down(duc) * w[2:3, :]
        dw = jnp.concatenate([jnp.sum(duc * _shift_down(u_in), axis=0, keepdims=True),
                              jnp.sum(duc * u_in, axis=0, keepdims=True),
                              jnp.sum(duc * _shift_up(u_in), axis=0, keepdims=True)], axis=0)
        return du, dw, jnp.sum(duc, axis=0, keepdims=True)

    def body(u_ref, w_ref, b_ref, da_ref, du_ref, dw_ref, db_ref):
        u, w, b = u_ref[...].astype(F32), w_ref[...], b_ref[...]
        u_g, u_v, w_g, w_v = u[:, :CONV_STRIP], u[:, CONV_STRIP:], w[:, :CONV_STRIP], w[:, CONV_STRIP:]
        g = _conv(u_g, w_g, b[:, :CONV_STRIP])
        val = _conv(u_v, w_v, b[:, CONV_STRIP:])
        da = da_ref[...].astype(F32)
        sg = _sigmoid(g)
        dval = da * (g * sg)
        dg = da * val * (sg * (1.0 + g * (1.0 - sg)))
        du_g, dw_g, db_g = conv_bwd(dg, u_g, w_g)
        du_v, dw_v, db_v = conv_bwd(dval, u_v, w_v)
        du_ref[...] = jnp.concatenate([du_g, du_v], axis=1).astype(BF16)
        dw_ref[...] = jnp.concatenate([dw_g, dw_v], axis=1)
        db_ref[...] = jnp.concatenate([db_g, db_v], axis=1)

    return _pcall(
        body, name="conv_gate_bwd", grid=(N_STRIPS,),
        in_specs=[_pair(S), _pair(3), _pair(1), pl.BlockSpec((S, CONV_STRIP), lambda j: (0, j))],
        out_specs=[_pair(S), _pair(3), _pair(1)],
        out_shape=[jax.ShapeDtypeStruct((S, 2 * D_FF), BF16), jax.ShapeDtypeStruct((3, 2 * D_FF), F32),
                   jax.ShapeDtypeStruct((1, 2 * D_FF), F32)],
        dims=("parallel",))(u, conv_w, conv_b, da)


MESH = pl.DeviceIdType.MESH
ANY = pl.BlockSpec(memory_space=pl.ANY)


def _place():
    return lax.axis_index("x"), lax.axis_index("y"), lax.axis_index("c")


def _all_gather(shards):
    n_arr = len(shards)

    def body(*refs):
        ins, outs = refs[:n_arr], refs[n_arr:2 * n_arr]
        send_sems, recv_sems, local_sems = refs[2 * n_arr:]
        x, y, c = _place()
        me, sibling = (x, y, c), (x, y, 1 - c)
        chips = [(1 - x, y), (x, 1 - y), (1 - x, 1 - y)]

        def slot(a, p):
            return outs[a].at[4 * p[0] + 2 * p[1] + p[2]]

        def copy(a, k, block, to, src=None):
            return pltpu.make_async_remote_copy(
                src_ref=slot(a, block) if src is None else src, dst_ref=slot(a, block),
                send_sem=send_sems.at[a, k], recv_sem=recv_sems.at[a, k], device_id=to, device_id_type=MESH)

        mine = [pltpu.make_async_copy(ins[a], slot(a, me), local_sems.at[a]) for a in range(n_arr)]
        for cp in mine:
            cp.start()
        first = []
        for a in range(n_arr):
            first.append(copy(a, 0, me, sibling, src=ins[a]))
            first += [copy(a, 1 + j, me, (*chip, c), src=ins[a]) for j, chip in enumerate(chips)]
        for cp in first:
            cp.start()
        passed = []
        for j, chip in enumerate(chips):
            for a in range(n_arr):
                copy(a, 1 + j, (*chip, c), me).wait_recv()
                cp = copy(a, 4 + j, (*chip, c), sibling)
                cp.start()
                passed.append(cp)
        for a in range(n_arr):
            copy(a, 0, sibling, me).wait_recv()
            for j, chip in enumerate(chips):
                copy(a, 4 + j, (*chip, 1 - c), me).wait_recv()
        for cp in first + passed:
            cp.wait_send()
        for cp in mine:
            cp.wait()

    return pl.pallas_call(
        body, name="all_gather_weights",
        in_specs=[ANY] * n_arr, out_specs=[ANY] * n_arr,
        out_shape=[jax.ShapeDtypeStruct((N_DEV, *s.shape), s.dtype) for s in shards],
        scratch_shapes=[pltpu.SemaphoreType.DMA((n_arr, 7)), pltpu.SemaphoreType.DMA((n_arr, 7)), pltpu.SemaphoreType.DMA((n_arr,))],
    )(*shards)


def _xchg_out_shapes(stacked, replicated):
    return ([jax.ShapeDtypeStruct(s.shape, s.dtype) for s in stacked]
            + [jax.ShapeDtypeStruct((N_DEV, *r.shape), r.dtype) for r in replicated])


def _xchg_sems(n_arr):
    return [pltpu.SemaphoreType.DMA((n_arr, 7)), pltpu.SemaphoreType.DMA((n_arr, 7)), pltpu.SemaphoreType.DMA((n_arr,))]


def _xchg_copies(ins, outs, sems, n_st, with_recv):
    send_sems, recv_sems, local_sems = sems
    n_arr = len(ins)
    x, y, c = _place()
    me = 4 * x + 2 * y + c

    def src(a, idx):
        return ins[a].at[idx] if a < n_st else ins[a]

    mine = [pltpu.make_async_copy(src(a, me), outs[a].at[me], local_sems.at[a]) for a in range(n_arr)]
    pairs = []
    for k in range(1, N_DEV):
        px, py, pc = x ^ (k >> 2), y ^ ((k >> 1) & 1), c ^ (k & 1)
        peer = 4 * px + 2 * py + pc
        for a in range(n_arr):
            sems_k = dict(send_sem=send_sems.at[a, k - 1], recv_sem=recv_sems.at[a, k - 1], device_id_type=MESH)
            send = pltpu.make_async_remote_copy(src_ref=src(a, peer), dst_ref=outs[a].at[me], device_id=(px, py, pc), **sems_k)
            recv = None
            if with_recv:
                recv = pltpu.make_async_remote_copy(src_ref=src(a, peer), dst_ref=outs[a].at[peer], device_id=(x, y, c), **sems_k)
            pairs.append((send, recv))
    return mine, pairs


def _xchg_start(ins, outs, sems, n_st):
    mine, pairs = _xchg_copies(ins, outs, sems, n_st, False)
    for cp in mine:
        cp.start()
    for send, _ in pairs:
        send.start()


def _xchg_finish(ins, outs, sems, n_st):
    mine, pairs = _xchg_copies(ins, outs, sems, n_st, True)
    for _, recv in pairs:
        recv.wait_recv()
    for send, _ in pairs:
        send.wait_send()
    for cp in mine:
        cp.wait()


def _exchange(stacked, replicated, name):
    _, landed = _pcall(lambda: None, name=name, grid=(), in_specs=[], out_specs=[], out_shape=[], comm=(stacked, replicated))()
    return landed


def _adamw(parts, w, m, v, name):
    R, C = w.shape
    tr = R if (R <= 512 or R % 256) else 256

    def body(p_ref, w_ref, m_ref, v_ref, g_ref, d_ref, nm_ref, nv_ref):
        g = p_ref[0].astype(F32)
        for s in range(1, N_DEV):
            g = g + p_ref[s].astype(F32)
        m2 = ADAM_B1 * m_ref[...] + (1.0 - ADAM_B1) * g
        v2 = ADAM_B2 * v_ref[...] + (1.0 - ADAM_B2) * (g * g)
        m_hat = m2 / (1.0 - ADAM_B1 ** ADAM_STEP)
        v_hat = v2 / (1.0 - ADAM_B2 ** ADAM_STEP)
        g_ref[...] = g
        d_ref[...] = -ADAM_LR * (m_hat / (jnp.sqrt(v_hat) + ADAM_EPS) + ADAM_WD * w_ref[...])
        nm_ref[...] = m2
        nv_ref[...] = v2

    blk = pl.BlockSpec((tr, C), lambda i: (i, 0))
    return _pcall(body, name=name, grid=(R // tr,),
                  in_specs=[pl.BlockSpec((N_DEV, tr, C), lambda i: (0, i, 0)), blk, blk, blk],
                  out_specs=[blk] * 4, out_shape=[jax.ShapeDtypeStruct((R, C), F32)] * 4,
                  dims=("parallel",))(parts, w, m, v)


def _t5_bucket(rel):
    nb = NUM_BUCKETS // 2
    max_exact = nb // 2
    base = (rel > 0).astype(jnp.int32) * nb
    n = jnp.abs(rel)
    nf = jnp.maximum(n, 1).astype(jnp.float32)
    large = max_exact + (jnp.log(nf / max_exact) / math.log(MAX_DISTANCE / max_exact) * (nb - max_exact)).astype(jnp.int32)
    large = jnp.minimum(large, nb - 1)
    return base + jnp.where(n < max_exact, n, large)


def _unstack_cols(g):
    return jnp.transpose(g, (1, 0, 2)).reshape(g.shape[1], N_DEV * g.shape[2])


def _stack_cols(w):
    R = w.shape[0]
    return jnp.transpose(w.reshape(R, N_DEV, w.shape[1] // N_DEV), (1, 0, 2))


def kernel(x, positions, norm1_g, w_in, q_a_norm_g, w_q_b, kv_a_norm_g, w_kv_b, rel_bias, sinks, w_out, norm2_g, w_up, conv_w, conv_b, w_down, final_norm_g, loss_target, m_norm1_g, m_w_in, m_q_a_norm_g, m_w_q_b, m_kv_a_norm_g, m_w_kv_b, m_rel_bias, m_sinks, m_w_out, m_norm2_g, m_w_up, m_conv_w, m_conv_b, m_w_down, m_final_norm_g, v_norm1_g, v_w_in, v_q_a_norm_g, v_w_q_b, v_kv_a_norm_g, v_w_kv_b, v_rel_bias, v_sinks, v_w_out, v_norm2_g, v_w_up, v_conv_w, v_conv_b, v_w_down, v_final_norm_g):
    S = x.shape[1]
    x = x[0]
    target = loss_target[0]
    TM = 256

    g_in, g_qb, g_kvb = _all_gather([w_in[0].astype(BF16), w_q_b[0].astype(BF16), w_kv_b[0].astype(BF16)])
    late_weights = [w_out[0].astype(BF16), w_up[0].astype(BF16), w_down[0].astype(BF16), conv_w[0]]
    wi = _unstack_cols(g_in)
    c0, c1, c2, c3, c4, c5 = (sum(W_IN_SIZES[:i + 1]) for i in range(6))
    w_in_p = jnp.concatenate([wi[:, c4:c5], wi[:, c5:], wi[:, c1:c2], wi[:, :c0], wi[:, c2:c3], wi[:, c3:c4],
                              wi[:, c0:c0 + KV_LORA], wi[:, c0 + KV_LORA:c1], jnp.zeros((D_MODEL, 64), BF16)], axis=1)
    wq = _unstack_cols(g_qb).reshape(Q_LORA, H_A, QK_HEAD)
    w_qb_p = jnp.concatenate([wq[:, :, :QK_NOPE].reshape(Q_LORA, H_A * QK_NOPE), wq[:, :, QK_NOPE:].reshape(Q_LORA, H_A * QK_ROPE)], axis=1)
    w_kvb = _unstack_cols(g_kvb)

    half = QK_ROPE // 2
    inv_freq = ROPE_THETA ** (-jnp.arange(half, dtype=F32) / half)
    ang = positions.astype(F32)[:, None] * inv_freq[None, :]
    cos, sin = jnp.cos(ang), jnp.sin(ang)
    qa = jnp.arange(Q_BLOCK, dtype=jnp.int32)[:, None]
    kc = jnp.arange(SPAN, dtype=jnp.int32)[None, :]
    rel = (kc - WINDOW - qa).T
    in_band = (jnp.abs(rel) <= WINDOW).astype(F32).reshape(1, Q_BLOCK * SPAN)
    onehot_t = (_t5_bucket(rel).reshape(1, Q_BLOCK * SPAN) == jnp.arange(NUM_BUCKETS, dtype=jnp.int32)[:, None]).astype(F32)
    bias_t = _bias_table(rel_bias.T, onehot_t, in_band).reshape(H_B, SPAN, Q_BLOCK)
    sinks_b = jnp.broadcast_to(sinks.reshape(H_B, 1), (H_B, Q_BLOCK))

    (h1,) = _rowwise(lambda a, g: (_rms(a, g),), "norm1", S, TM, [_rows(x), _whole(norm1_g)], [("rows", D_MODEL, BF16)])
    proj = _matmul(h1, w_in_p, "nn", BF16, "proj")

    def lat_fn(qlat, ckv, kr, gq, gkv, cs, sn):
        r1, r2 = _rope(kr[:, :half], kr[:, half:QK_ROPE], cs, sn)
        return _rms(qlat, gq), _rms(ckv, gkv), jnp.concatenate([r1, r2], axis=1)

    qn, ckvn, k_rope = _rowwise(lat_fn, "latents", S, TM,
                                [_rows(proj, 256, PROJ_QLAT), _rows(proj, 128, PROJ_CKV), _rows(proj, 128, PROJ_KROPE),
                                 _whole(q_a_norm_g), _whole(kv_a_norm_g), _rows(cos), _rows(sin)],
                                [("rows", Q_LORA, BF16), ("rows", KV_LORA, BF16), ("rows", QK_ROPE, BF16)])
    q_p = _matmul(qn, w_qb_p, "nn", F32, "q_up", tn=1536)
    kv = _matmul(ckvn, w_kvb, "nn", BF16, "kv_up", tn=2048)

    def q_heads_fn(q, cs, sn):
        outs = []
        for h in range(H_A):
            o = H_A * QK_NOPE + QK_ROPE * h
            r1, r2 = _rope(q[:, o:o + half], q[:, o + half:o + QK_ROPE], cs, sn)
            outs.append(jnp.concatenate([q[:, QK_NOPE * h:QK_NOPE * (h + 1)], r1, r2], axis=1)[None])
        return (jnp.concatenate(outs, axis=0),)

    (q_full,) = _rowwise(q_heads_fn, "q_heads", S, TM, [_rows(q_p), _rows(cos), _rows(sin)], [("heads", H_A, QK_HEAD, BF16)])

    def k_heads_fn(kvb, kr):
        return (jnp.concatenate([jnp.concatenate([kvb[:, 256 * h:256 * h + QK_NOPE], kr], axis=1)[None] for h in range(H_A)], axis=0),)

    (k_full,) = _rowwise(k_heads_fn, "k_heads", S, TM, [_rows(kv), _rows(k_rope)], [("heads", H_A, QK_HEAD, BF16)], upcast=False)
    (o_a, lse), (g_out, g_up, g_down, g_cw) = _mla_fwd(q_full, k_full, kv, S, comm=([], late_weights))
    w_out_f = g_out.reshape(D_MODEL, D_MODEL)
    w_up_f = _interleave(_unstack_cols(g_up))
    w_down_f = g_down.reshape(D_FF, D_MODEL)
    conv_w_f = _interleave(_unstack_cols(g_cw))
    conv_b_f = _interleave(conv_b)

    kp = jnp.pad(proj[:, 256 * PROJ_KB:256 * (PROJ_KB + 1)], ((WINDOW, WINDOW), (0, 0)))
    vp = jnp.pad(proj[:, 256 * PROJ_VB:256 * (PROJ_VB + 1)], ((WINDOW, WINDOW), (0, 0)))
    o_b = _win_fwd(proj, kp, vp, bias_t, sinks_b, S)

    (mixed,) = _rowwise(lambda ga, gb, oa, ob: (_sigmoid(ga) * oa + _sigmoid(gb) * ob,), "gate_mix", S, TM,
                        [_rows(proj, 1024, PROJ_GA), _rows(proj, 1024, PROJ_GB), _rows(o_a), _rows(o_b)], [("rows", D_MODEL, BF16)])
    x1 = _matmul(mixed, w_out_f, "nn", F32, "out_proj", residual=x)
    (h2,) = _rowwise(lambda a, g: (_rms(a, g),), "norm2", S, TM, [_rows(x1), _whole(norm2_g)], [("rows", D_MODEL, BF16)])
    u = _matmul(h2, w_up_f, "nn", BF16, "ffn_up", tn=1408)
    act = _conv_gate_fwd(u, conv_w_f, conv_b_f, S)
    x2 = _matmul(act, w_down_f, "nn", F32, "ffn_down", residual=x1, tm=512)

    def final_fn(a, g, t):
        err = _rms(a, g) - t
        loss = 0.5 * jnp.sum(jnp.mean(err * err, axis=-1, keepdims=True), axis=0, keepdims=True)
        dx, dg = _rms_bwd(err * (1.0 / D_MODEL), a, g)
        return dx, dx, dg, jnp.broadcast_to(loss, (1, 128))

    gfin = final_norm_g.reshape(1, D_MODEL)
    dx2, dx2_b, d_gfin, loss_row = _rowwise(final_fn, "loss_head", S, TM, [_rows(x2), _whole(gfin), _rows(target)],
                                            [("rows", D_MODEL, F32), ("rows", D_MODEL, BF16), ("acc", 1, D_MODEL), ("acc", 1, 128)])
    d_act = _matmul(dx2_b, w_down_f, "nt", BF16, "ffn_down_dx", tn=1408)
    d_w_down = _matmul(act, dx2_b, "tn", F32, "ffn_down_dw")
    du, d_conv_w_p, d_conv_b_p = _conv_gate_bwd(u, conv_w_f, conv_b_f, d_act, S)
    d_conv_w, d_conv_b = _deinterleave(d_conv_w_p), _deinterleave(d_conv_b_p)
    d_h2 = _matmul(du, w_up_f, "nt", F32, "ffn_up_dx", tm=512)
    d_w_up = _deinterleave(_matmul(h2, du, "tn", F32, "ffn_up_dw", tm=512, tn=1408))

    def norm_bwd_fn(dh, a, g, dres):
        dx, dg = _rms_bwd(dh, a, g)
        dx = dx + dres
        return dx, dx, dg

    dx1, dx1_b, d_g2 = _rowwise(norm_bwd_fn, "norm2_bwd", S, TM, [_rows(d_h2), _rows(x1), _whole(norm2_g), _rows(dx2)],
                                [("rows", D_MODEL, F32), ("rows", D_MODEL, BF16), ("acc", 1, D_MODEL)])
    d_mixed = _matmul(dx1_b, w_out_f, "nt", BF16, "out_proj_dx")
    d_w_out = _matmul(mixed, dx1_b, "tn", F32, "out_proj_dw", tm=512)

    def gate_bwd_fn(dm, ga, gb, oa, ob):
        sa, sb = _sigmoid(ga), _sigmoid(gb)
        return jnp.concatenate([dm * oa * sa * (1.0 - sa), dm * ob * sb * (1.0 - sb)], axis=1), dm * sa, dm * sb

    d_proj, do_a, do_b = _rowwise(gate_bwd_fn, "gate_bwd", S, TM,
                                  [_rows(d_mixed), _rows(proj, 1024, PROJ_GA), _rows(proj, 1024, PROJ_GB), _rows(o_a), _rows(o_b)],
                                  [("cols", 2 * D_MODEL, 0, PROJ_P, BF16), ("rows", D_MODEL, BF16), ("rows", D_MODEL, BF16)])

    d_proj, dk_acc, dv_acc, d_bias, d_sinks_g = _win_bwd(proj, kp, vp, bias_t, sinks_b, do_b, d_proj, S)
    d_rel_bias = _bias_table_bwd(d_bias.reshape(H_B, Q_BLOCK * SPAN), onehot_t).T
    d_sinks = d_sinks_g.reshape(1, H_B)

    early = [d_w_out.reshape(N_DEV, D_MODEL // N_DEV, D_MODEL).astype(BF16), _stack_cols(d_w_up).astype(BF16),
             d_w_down.reshape(N_DEV, D_FF // N_DEV, D_MODEL).astype(BF16), _stack_cols(d_conv_w)]
    (dq_full, dk_full, dv_full), recv_early = _mla_bwd(q_full, k_full, kv, do_a, o_a, lse, S, comm=(early, []))

    def dq_post_fn(dq, cs, sn):
        nope = [dq[h, :, :QK_NOPE] for h in range(H_A)]
        rope = []
        for h in range(H_A):
            rope += list(_rope_bwd(dq[h, :, QK_NOPE:QK_NOPE + half], dq[h, :, QK_NOPE + half:], cs, sn))
        return (jnp.concatenate(nope + rope, axis=1),)

    (dq_p,) = _rowwise(dq_post_fn, "dq_post", S, TM, [_heads(dq_full), _rows(cos), _rows(sin)], [("rows", H_A * QK_HEAD, BF16)])

    def dkv_post_fn(dk, dv, cs, sn):
        dkv = jnp.concatenate([jnp.concatenate([dk[h, :, :QK_NOPE], dv[h]], axis=1) for h in range(H_A)], axis=1)
        dkr = dk[0, :, QK_NOPE:]
        for h in range(1, H_A):
            dkr = dkr + dk[h, :, QK_NOPE:]
        r1, r2 = _rope_bwd(dkr[:, :half], dkr[:, half:], cs, sn)
        return dkv, jnp.concatenate([r1, r2], axis=1)

    dkv, d_krope = _rowwise(dkv_post_fn, "dkv_post", S, TM, [_heads(dk_full), _heads(dv_full), _rows(cos), _rows(sin)],
                            [("rows", H_A * (QK_NOPE + V_DIM), BF16), ("rows", QK_ROPE, F32)])
    d_qn = _matmul(dq_p, w_qb_p, "nt", F32, "q_up_dx")
    d_w_qb_p = _matmul(qn, dq_p, "tn", F32, "q_up_dw", tn=1536)
    d_ckvn = _matmul(dkv, w_kvb, "nt", F32, "kv_up_dx")
    d_w_kvb = _matmul(ckvn, dkv, "tn", F32, "kv_up_dw", tn=2048)

    def lat_bwd_fn(dqn, dckvn, dkr, qlat, ckv, gq, gkv, dkb, dvb):
        dql, dgq = _rms_bwd(dqn, qlat, gq)
        dck, dgkv = _rms_bwd(dckvn, ckv, gkv)
        tail = jnp.concatenate([dql, dkb, dvb, dck, dkr, jnp.zeros_like(dkr)], axis=1)
        return tail, dgq, dgkv

    shifted = lambda arr: (arr, lambda tm: pl.BlockSpec((tm, arr.shape[1]), lambda i: (i + WINDOW // tm, 0)))
    TL = min(128, S)
    d_proj, d_gq, d_gkv = _rowwise(lat_bwd_fn, "latents_bwd", S, TL,
                                   [_rows(d_qn), _rows(d_ckvn), _rows(d_krope), _rows(proj, 256, PROJ_QLAT), _rows(proj, 128, PROJ_CKV),
                                    _whole(q_a_norm_g), _whole(kv_a_norm_g), shifted(dk_acc), shifted(dv_acc)],
                                   [("cols", 1024, 3, PROJ_P, BF16), ("acc", 1, Q_LORA), ("acc", 1, KV_LORA)], into=(d_proj, 0))
    d_w_in_p = _matmul(h1, d_proj, "tn", F32, "proj_dw", tm=512)

    dp = d_w_in_p
    d_w_in = jnp.concatenate([dp[:, 3072:3328], dp[:, 3840:3968], dp[:, 3968:4032], dp[:, 2048:3072], dp[:, 3328:3584],
                              dp[:, 3584:3840], dp[:, 0:1024], dp[:, 1024:2048]], axis=1)
    d_w_qb = jnp.concatenate([d_w_qb_p[:, :H_A * QK_NOPE].reshape(Q_LORA, H_A, QK_NOPE),
                              d_w_qb_p[:, H_A * QK_NOPE:].reshape(Q_LORA, H_A, QK_ROPE)], axis=2).reshape(Q_LORA, H_A * QK_HEAD)
    late = [_stack_cols(d_w_in).astype(BF16), _stack_cols(d_w_qb).astype(BF16), _stack_cols(d_w_kvb).astype(BF16)]
    d_h1, recv_late = _matmul(d_proj, w_in_p, "nt", F32, "proj_dx", tm=512, comm=(late, []))
    grad_x, _, d_g1 = _rowwise(norm_bwd_fn, "norm1_bwd", S, TM, [_rows(d_h1), _rows(x), _whole(norm1_g), _rows(dx1)],
                               [("rows", D_MODEL, F32), ("rows", D_MODEL, BF16), ("acc", 1, D_MODEL)])
    recv = [*recv_late, *recv_early]

    small_parts = [d_g1, d_gq, d_gkv, d_rel_bias.reshape(1, NUM_BUCKETS * H_B), d_sinks, d_g2, d_conv_b, d_gfin]
    small = jnp.concatenate(small_parts, axis=1)
    n_small = small.shape[1]
    pad = (-n_small) % 128
    small = jnp.pad(small, ((0, 0), (0, pad)))
    (recv_small,) = _exchange([], [small], "exchange_small_grads")

    def flat(a):
        return a.reshape(1, -1)

    small_w = [norm1_g, q_a_norm_g, kv_a_norm_g, rel_bias, sinks, norm2_g, conv_b, final_norm_g]
    small_m = [m_norm1_g, m_q_a_norm_g, m_kv_a_norm_g, m_rel_bias, m_sinks, m_norm2_g, m_conv_b, m_final_norm_g]
    small_v = [v_norm1_g, v_q_a_norm_g, v_kv_a_norm_g, v_rel_bias, v_sinks, v_norm2_g, v_conv_b, v_final_norm_g]
    cat = lambda parts: jnp.pad(jnp.concatenate([flat(a) for a in parts], axis=1), ((0, 0), (0, pad)))
    sm = _adamw(recv_small, cat(small_w), cat(small_m), jnp.pad(jnp.concatenate([flat(a) for a in small_v], axis=1), ((0, 0), (0, pad)), constant_values=1.0), "adamw_small")
    big_names = ["w_in", "w_q_b", "w_kv_b", "w_out", "w_up", "w_down", "conv_w"]
    big_w = [w_in, w_q_b, w_kv_b, w_out, w_up, w_down, conv_w]
    big_m = [m_w_in, m_w_q_b, m_w_kv_b, m_w_out, m_w_up, m_w_down, m_conv_w]
    big_v = [v_w_in, v_w_q_b, v_w_kv_b, v_w_out, v_w_up, v_w_down, v_conv_w]
    big = {n: _adamw(r, w[0], m[0], v[0], "adamw_" + n) for n, r, w, m, v in zip(big_names, recv, big_w, big_m, big_v)}

    loss = lax.psum(loss_row[0, 0], ("x", "y", "c"))
    order = ["norm1_g", "w_in", "q_a_norm_g", "w_q_b", "kv_a_norm_g", "w_kv_b", "rel_bias", "sinks", "w_out", "norm2_g", "w_up",
             "conv_w", "conv_b", "w_down", "final_norm_g"]
    small_names = ["norm1_g", "q_a_norm_g", "kv_a_norm_g", "rel_bias", "sinks", "norm2_g", "conv_b", "final_norm_g"]
    offs, o = {}, 0
    for n, a in zip(small_names, small_w):
        offs[n] = (o, a.size, a.shape)
        o += a.size
    outs = [loss, grad_x[None]]
    for kind in range(4):
        for n in order:
            if n in big:
                outs.append(big[n][kind][None])
            else:
                o, size, shape = offs[n]
                outs.append(sm[kind][0, o:o + size].reshape(shape))
    return tuple(outs)
```

```python
import math

import jax
import jax.numpy as jnp
from jax import lax
from jax.experimental import pallas as pl
from jax.experimental.pallas import tpu as pltpu

F32 = jnp.float32
BF16 = jnp.bfloat16

N_DEV = 8
D_MODEL = 1024
EPS = 1e-6
H_A, QK_NOPE, QK_ROPE, V_DIM, Q_LORA, KV_LORA = 8, 128, 64, 128, 256, 128
QK_HEAD = QK_NOPE + QK_ROPE
ROPE_THETA = 10000.0
H_B, KV_B, GROUP, HD_B, WINDOW, Q_BLOCK = 16, 4, 4, 64, 128, 128
SPAN = Q_BLOCK + 2 * WINDOW
NUM_BUCKETS, MAX_DISTANCE = 32, 128
D_FF = 2816
ADAM_LR, ADAM_B1, ADAM_B2, ADAM_EPS, ADAM_WD, ADAM_STEP = 0.001, 0.9, 0.999, 1e-08, 0.01, 10

W_IN_SIZES = (Q_LORA, KV_LORA + QK_ROPE, H_B * HD_B, KV_B * HD_B, KV_B * HD_B, D_MODEL, D_MODEL)
W_IN_COLS = sum(W_IN_SIZES)
PROJ_P = 4096
PROJ_GA, PROJ_GB, PROJ_QB, PROJ_QLAT, PROJ_KB, PROJ_VB, PROJ_CKV, PROJ_KROPE = 0, 1, 2, 12, 13, 14, 30, 31

VMEM_LIMIT = 56 * 1024 * 1024

NN = (((1,), (0,)), ((), ()))
NT = (((1,), (1,)), ((), ()))
TN = (((0,), (0,)), ((), ()))


def _pcall(body, *, name, grid, in_specs, out_specs, out_shape, scratch_shapes=(), dims=None, comm=None, aliases=None):
    if comm is None:
        params = pltpu.CompilerParams(dimension_semantics=dims, vmem_limit_bytes=VMEM_LIMIT)
        return pl.pallas_call(body, name=name, grid=grid, in_specs=in_specs, out_specs=out_specs, out_shape=out_shape,
                              scratch_shapes=list(scratch_shapes), input_output_aliases=aliases or {}, compiler_params=params)
    assert not aliases
    stacked, replicated = comm
    arrs = [*stacked, *replicated]
    n_st, n_arr = len(stacked), len(arrs)
    single = not isinstance(out_specs, (list, tuple))
    o_specs, o_shape = ([out_specs], [out_shape]) if single else (list(out_specs), list(out_shape))
    n_in, n_out = len(in_specs), len(o_specs)

    def wrapped(*refs):
        c_in = refs[n_in:n_in + n_arr]
        c_out = refs[n_in + n_arr + n_out:n_in + 2 * n_arr + n_out]
        sems = refs[len(refs) - 3:]
        own = (*refs[:n_in], *refs[n_in + n_arr:n_in + n_arr + n_out], *refs[n_in + 2 * n_arr + n_out:len(refs) - 3])
        if not grid:
            _xchg_start(c_in, c_out, sems, n_st)
            _xchg_finish(c_in, c_out, sems, n_st)
            return
        first = last = None
        for d, n in enumerate(grid):
            pid = pl.program_id(d)
            first = (pid == 0) if first is None else first & (pid == 0)
            last = (pid == n - 1) if last is None else last & (pid == n - 1)

        @pl.when(first)
        def _():
            _xchg_start(c_in, c_out, sems, n_st)

        body(*own)

        @pl.when(last)
        def _():
            _xchg_finish(c_in, c_out, sems, n_st)

    params = pltpu.CompilerParams(dimension_semantics=("arbitrary",) * len(grid), vmem_limit_bytes=VMEM_LIMIT)
    call = pl.pallas_call(wrapped, name=name, grid=grid, in_specs=[*in_specs, *[ANY] * n_arr], out_specs=[*o_specs, *[ANY] * n_arr],
                          out_shape=[*o_shape, *_xchg_out_shapes(stacked, replicated)],
                          scratch_shapes=[*scratch_shapes, *_xchg_sems(n_arr)], compiler_params=params)

    def run(*args):
        res = call(*args, *arrs)
        outs, landed = res[:n_out], res[n_out:]
        return (outs[0] if single else outs), landed

    return run


def _dot(a, b, dn):
    return lax.dot_general(a, b, dn, preferred_element_type=F32)


def _tile(n, target):
    best = None
    for t in range(128, min(n, target) + 1, 128):
        if n % t == 0:
            best = t
    return n if best is None else best


def _matmul(a, b, mode, out_dtype, name, residual=None, tm=1024, tn=1024, comm=None, a2=None):
    if mode == "nn":
        (M, K), N = a.shape, b.shape[1]
    elif mode == "nt":
        (M, K), N = a.shape, b.shape[0]
    else:
        (K, M), N = a.shape, b.shape[1]
    tm, tn = _tile(M, tm), _tile(N, tn)
    a_spec = pl.BlockSpec((K, tm), lambda i, j: (0, i)) if mode == "tn" else pl.BlockSpec((tm, K), lambda i, j: (i, 0))
    b_spec = pl.BlockSpec((tn, b.shape[1]), lambda i, j: (j, 0)) if mode == "nt" else pl.BlockSpec((K, tn), lambda i, j: (0, j))
    o_spec = pl.BlockSpec((tm, tn), lambda i, j: (i, j))
    in_specs, args = [a_spec, b_spec], [a, b]
    if a2 is not None:
        assert mode == "nt" and K + a2.shape[1] == b.shape[1]
        in_specs.append(pl.BlockSpec((tm, a2.shape[1]), lambda i, j: (i, 0)))
        args.append(a2)
    if residual is not None:
        in_specs.append(o_spec)
        args.append(residual)
    scratch = [pltpu.VMEM((tm, K), a.dtype)] if mode == "tn" else []

    def body(*refs):
        a_ref, b_ref = refs[0], refs[1]
        o_ref = refs[len(args)]
        if mode == "tn":
            at_ref = refs[len(args) + 1]

            @pl.when(pl.program_id(1) == 0)
            def _():
                at_ref[...] = a_ref[...].T

            acc = _dot(at_ref[...], b_ref[...], NN)
        elif a2 is not None:
            acc = _dot(a_ref[...], b_ref[:, :K], NT) + _dot(refs[2][...], b_ref[:, K:], NT)
        else:
            acc = _dot(a_ref[...], b_ref[...], NT if mode == "nt" else NN)
        if residual is not None:
            acc = acc + refs[len(args) - 1][...]
        o_ref[...] = acc.astype(out_dtype)

    return _pcall(body, name=name, grid=(M // tm, N // tn), in_specs=in_specs, out_specs=o_spec,
                  out_shape=jax.ShapeDtypeStruct((M, N), out_dtype), scratch_shapes=scratch,
                  dims=("parallel", "arbitrary"), comm=comm)(*args)


def _rows(arr, width=None, col=0):
    width = arr.shape[1] if width is None else width
    return (arr, lambda tm: pl.BlockSpec((tm, width), lambda i: (i, col)))


def _heads(arr):
    return (arr, lambda tm: pl.BlockSpec((arr.shape[0], tm, arr.shape[2]), lambda i: (0, i, 0)))


def _whole(arr):
    nd = arr.ndim
    return (arr, lambda tm: pl.BlockSpec(arr.shape, lambda i: (0,) * nd))


def _rowwise(fn, name, n_rows, tm, ins, outs, upcast=True, into=None):
    tm = min(tm, n_rows)
    assert n_rows % tm == 0
    in_specs = [mk(tm) for _, mk in ins]
    out_specs, out_shape, is_acc = [], [], []
    for o in outs:
        if o[0] == "rows":
            out_specs.append(pl.BlockSpec((tm, o[1]), lambda i: (i, 0)))
            out_shape.append(jax.ShapeDtypeStruct((n_rows, o[1]), o[2]))
        elif o[0] == "cols":
            out_specs.append(pl.BlockSpec((tm, o[1]), lambda i, c=o[2]: (i, c)))
            out_shape.append(jax.ShapeDtypeStruct((n_rows, o[3]), o[4]))
        elif o[0] == "heads":
            out_specs.append(pl.BlockSpec((o[1], tm, o[2]), lambda i: (0, i, 0)))
            out_shape.append(jax.ShapeDtypeStruct((o[1], n_rows, o[2]), o[3]))
        else:
            out_specs.append(pl.BlockSpec((o[1], o[2]), lambda i: (0, 0)))
            out_shape.append(jax.ShapeDtypeStruct((o[1], o[2]), F32))
        is_acc.append(o[0] == "acc")
    n_in = len(ins)
    args = [a for a, _ in ins]
    aliases = {}
    if into is not None:
        in_specs.append(ANY)
        args.append(into[0])
        aliases = {n_in: into[1]}

    def load(r):
        v = r[...]
        return v.astype(F32) if upcast and v.dtype == BF16 else v

    def body(*refs):
        vals = fn(*[load(r) for r in refs[:n_in]])
        for r, v, acc in zip(refs[len(args):], vals, is_acc):
            if acc:
                @pl.when(pl.program_id(0) == 0)
                def _():
                    r[...] = jnp.zeros_like(r)

                r[...] += v
            else:
                r[...] = v.astype(r.dtype)

    return _pcall(body, name=name, grid=(n_rows // tm,), in_specs=in_specs, out_specs=out_specs,
                  out_shape=out_shape, dims=("arbitrary",), aliases=aliases)(*args)


def _rms(x, g):
    r = lax.rsqrt(jnp.mean(x * x, axis=-1, keepdims=True) + EPS)
    return x * r * g


def _rms_bwd(dy, x, g):
    r = lax.rsqrt(jnp.mean(x * x, axis=-1, keepdims=True) + EPS)
    xhat = x * r
    dxhat = dy * g
    dx = r * (dxhat - xhat * jnp.mean(dxhat * xhat, axis=-1, keepdims=True))
    return dx, jnp.sum(dy * xhat, axis=0, keepdims=True)


def _rope(x1, x2, cos, sin):
    return x1 * cos - x2 * sin, x2 * cos + x1 * sin


def _rope_bwd(d1, d2, cos, sin):
    return d1 * cos + d2 * sin, d2 * cos - d1 * sin


def _sigmoid(x):
    return 1.0 / (1.0 + jnp.exp(-x))


def _mla_fwd(q_full, k_full, kv, S, comm=None):
    tq = min(256, S)
    scale = 1.0 / math.sqrt(QK_HEAD)

    def body(q_ref, k_ref, v_ref, o_ref, lse_ref):
        s = _dot(q_ref[0], k_ref[0], NT) * scale
        m = jnp.max(s, axis=-1, keepdims=True)
        p = jnp.exp(s - m)
        l = jnp.sum(p, axis=-1, keepdims=True)
        o_ref[...] = (_dot(p.astype(BF16), v_ref[...], NN) / l).astype(BF16)
        lse_ref[0] = m + jnp.log(l)

    return _pcall(
        body, name="mla_fwd", grid=(H_A, S // tq),
        in_specs=[pl.BlockSpec((1, tq, QK_HEAD), lambda h, i: (h, i, 0)),
                  pl.BlockSpec((1, S, QK_HEAD), lambda h, i: (h, 0, 0)),
                  pl.BlockSpec((S, V_DIM), lambda h, i: (0, 2 * h + 1))],
        out_specs=[pl.BlockSpec((tq, V_DIM), lambda h, i: (i, h)),
                   pl.BlockSpec((1, tq, 1), lambda h, i: (h, i, 0))],
        out_shape=[jax.ShapeDtypeStruct((S, H_A * V_DIM), BF16), jax.ShapeDtypeStruct((H_A, S, 1), F32)],
        dims=("parallel", "parallel"), comm=comm)(q_full, k_full, kv)


def _mla_bwd(q_full, k_full, kv, do_a, o_a, lse, S, comm=None):
    tq = min(256, S)
    kc = min(1024, S)
    scale = 1.0 / math.sqrt(QK_HEAD)

    def body(q_ref, k_ref, v_ref, do_ref, o_ref, lse_ref, dq_ref, dk_ref, dv_ref):
        @pl.when(pl.program_id(1) == 0)
        def _():
            dk_ref[...] = jnp.zeros_like(dk_ref)
            dv_ref[...] = jnp.zeros_like(dv_ref)

        q = q_ref[0]
        do = do_ref[...]
        lse_q = lse_ref[0]
        delta = jnp.sum(do.astype(F32) * o_ref[...].astype(F32), axis=-1, keepdims=True)
        dq = jnp.zeros((tq, QK_HEAD), F32)
        for c in range(S // kc):
            k = k_ref[0, c * kc:(c + 1) * kc, :]
            v = v_ref[c * kc:(c + 1) * kc, :]
            p = jnp.exp(_dot(q, k, NT) * scale - lse_q)
            ds = (p * (_dot(do, v, NT) - delta) * scale).astype(BF16)
            dq = dq + _dot(ds, k, NN)
            dk_ref[0, c * kc:(c + 1) * kc, :] += _dot(ds, q, TN)
            dv_ref[0, c * kc:(c + 1) * kc, :] += _dot(p.astype(BF16), do, TN)
        dq_ref[0] = dq

    return _pcall(
        body, name="mla_bwd", grid=(H_A, S // tq),
        in_specs=[pl.BlockSpec((1, tq, QK_HEAD), lambda h, i: (h, i, 0)),
                  pl.BlockSpec((1, S, QK_HEAD), lambda h, i: (h, 0, 0)),
                  pl.BlockSpec((S, V_DIM), lambda h, i: (0, 2 * h + 1)),
                  pl.BlockSpec((tq, V_DIM), lambda h, i: (i, h)),
                  pl.BlockSpec((tq, V_DIM), lambda h, i: (i, h)),
                  pl.BlockSpec((1, tq, 1), lambda h, i: (h, i, 0))],
        out_specs=[pl.BlockSpec((1, tq, QK_HEAD), lambda h, i: (h, i, 0)),
                   pl.BlockSpec((1, S, QK_HEAD), lambda h, i: (h, 0, 0)),
                   pl.BlockSpec((1, S, V_DIM), lambda h, i: (h, 0, 0))],
        out_shape=[jax.ShapeDtypeStruct((H_A, S, QK_HEAD), F32), jax.ShapeDtypeStruct((H_A, S, QK_HEAD), F32),
                   jax.ShapeDtypeStruct((H_A, S, V_DIM), F32)],
        dims=("parallel", "arbitrary"), comm=comm)(q_full, k_full, kv, do_a, o_a, lse)


WIN_SCALE = 1.0 / math.sqrt(HD_B)


def _win_specs(S):
    qspec = pl.BlockSpec((Q_BLOCK, H_B * HD_B), lambda n: (n, PROJ_QB))
    kspecs = [pl.BlockSpec((Q_BLOCK, KV_B * HD_B), lambda n, d=d: (n + d, 0)) for d in range(3)]
    bias_spec = pl.BlockSpec((H_B, SPAN, Q_BLOCK), lambda n: (0, 0, 0))
    sink_spec = pl.BlockSpec((H_B, Q_BLOCK), lambda n: (0, 0))
    return qspec, kspecs, bias_spec, sink_spec


def _win_edge_ok(n, n_blk):
    row = lax.broadcasted_iota(jnp.int32, (SPAN, 1), 0)
    return jnp.logical_not(((n == 0) & (row < WINDOW)) | ((n == n_blk - 1) & (row >= SPAN - WINDOW)))


def _lanes4(pieces):
    return jnp.concatenate(pieces, axis=1)


def _win_probs(kg, q4t, bias_ref, sink_ref, g, edge_ok):
    bias4 = _lanes4([bias_ref[GROUP * g + j] for j in range(GROUP)])
    sink4 = _lanes4([sink_ref[GROUP * g + j:GROUP * g + j + 1, :] for j in range(GROUP)])
    s = jnp.where(edge_ok, _dot(kg, q4t, NN) + bias4, -1e30)
    m = jnp.maximum(jnp.max(s, axis=0, keepdims=True), sink4)
    p = jnp.exp(s - m)
    e_sink = jnp.exp(sink4 - m)
    inv_l = 1.0 / (jnp.sum(p, axis=0, keepdims=True) + e_sink)
    return p * inv_l, e_sink * inv_l


def _group_t(xt, g):
    return _lanes4([xt[HD_B * (GROUP * g + j):HD_B * (GROUP * g + j + 1), :] for j in range(GROUP)])


def _win_fwd(proj, kp, vp, bias_t, sinks_b, S):
    n_blk = S // Q_BLOCK
    qspec, kspecs, bias_spec, sink_spec = _win_specs(S)

    def body(q_ref, k0, k1, k2, v0, v1, v2, bias_ref, sink_ref, o_ref):
        n = pl.program_id(0)
        edge_ok = _win_edge_ok(n, n_blk)
        k = jnp.concatenate([k0[...], k1[...], k2[...]], axis=0)
        vt = jnp.concatenate([v0[...], v1[...], v2[...]], axis=0).T
        qt = (q_ref[...].astype(F32) * WIN_SCALE).T.astype(BF16)
        parts = []
        for g in range(KV_B):
            p, _ = _win_probs(k[:, HD_B * g:HD_B * (g + 1)], _group_t(qt, g), bias_ref, sink_ref, g, edge_ok)
            o4t = _dot(vt[HD_B * g:HD_B * (g + 1), :], p.astype(BF16), NN)
            parts += [o4t[:, Q_BLOCK * j:Q_BLOCK * (j + 1)] for j in range(GROUP)]
        o_ref[...] = jnp.concatenate(parts, axis=0).T.astype(BF16)

    return _pcall(body, name="win_fwd", grid=(n_blk,),
                  in_specs=[qspec, *kspecs, *kspecs, bias_spec, sink_spec],
                  out_specs=pl.BlockSpec((Q_BLOCK, H_B * HD_B), lambda n: (n, 0)),
                  out_shape=jax.ShapeDtypeStruct((S, H_B * HD_B), BF16),
                  dims=("parallel",))(proj, kp, kp, kp, vp, vp, vp, bias_t, sinks_b)


def _win_bwd(proj, kp, vp, bias_t, sinks_b, do_b, d_proj, S):
    n_blk = S // Q_BLOCK
    qspec, kspecs, bias_spec, sink_spec = _win_specs(S)

    def body(q_ref, k0, k1, k2, v0, v1, v2, bias_ref, sink_ref, do_ref, _, dq_ref, dk_ref, dv_ref, dbias_ref, dsink_ref, dsink_acc):
        n = pl.program_id(0)

        @pl.when(n == 0)
        def _():
            dk_ref[...] = jnp.zeros_like(dk_ref)
            dv_ref[...] = jnp.zeros_like(dv_ref)
            dbias_ref[...] = jnp.zeros_like(dbias_ref)
            dsink_acc[...] = jnp.zeros_like(dsink_acc)

        edge_ok = _win_edge_ok(n, n_blk)
        k = jnp.concatenate([k0[...], k1[...], k2[...]], axis=0)
        v = jnp.concatenate([v0[...], v1[...], v2[...]], axis=0)
        kt = k.T
        qt = (q_ref[...].astype(F32) * WIN_SCALE).T.astype(BF16)
        dot_ = do_ref[...].astype(F32).T.astype(BF16)
        dq_parts, dks, dvs = [], [], []
        for g in range(KV_B):
            kg, vg = k[:, HD_B * g:HD_B * (g + 1)], v[:, HD_B * g:HD_B * (g + 1)]
            q4t, do4t = _group_t(qt, g), _group_t(dot_, g)
            p, p_sink = _win_probs(kg, q4t, bias_ref, sink_ref, g, edge_ok)
            dp = _dot(vg, do4t, NN)
            delta = jnp.sum(p * dp, axis=0, keepdims=True)
            ds = p * (dp - delta)
            for j in range(GROUP):
                dbias_ref[GROUP * g + j] += ds[:, Q_BLOCK * j:Q_BLOCK * (j + 1)]
            dsink_acc[g:g + 1, :] += -p_sink * delta
            dsb = ds.astype(BF16)
            dq4t = _dot(kt[HD_B * g:HD_B * (g + 1), :], dsb, NN) * WIN_SCALE
            dq_parts += [dq4t[:, Q_BLOCK * j:Q_BLOCK * (j + 1)] for j in range(GROUP)]
            dks.append(_dot(dsb, q4t, NT))
            dvs.append(_dot(p.astype(BF16), do4t, NT))
        dq_ref[...] = jnp.concatenate(dq_parts, axis=0).T.astype(BF16)
        rows = pl.ds(pl.multiple_of(n * Q_BLOCK, Q_BLOCK), SPAN)
        dk_ref[rows, :] += jnp.concatenate(dks, axis=1)
        dv_ref[rows, :] += jnp.concatenate(dvs, axis=1)

        @pl.when(n == n_blk - 1)
        def _():
            acc = dsink_acc[...]
            dsink_ref[...] = jnp.concatenate(
                [jnp.sum(acc[:, Q_BLOCK * j:Q_BLOCK * (j + 1)], axis=1, keepdims=True) for j in range(GROUP)], axis=1)

    whole = lambda shape: pl.BlockSpec(shape, lambda n: (0,) * len(shape))
    return _pcall(
        body, name="win_bwd", grid=(n_blk,),
        in_specs=[qspec, *kspecs, *kspecs, bias_spec, sink_spec, pl.BlockSpec((Q_BLOCK, H_B * HD_B), lambda n: (n, 0)), ANY],
        out_specs=[qspec, whole((S + 2 * WINDOW, KV_B * HD_B)),
                   whole((S + 2 * WINDOW, KV_B * HD_B)), whole((H_B, SPAN, Q_BLOCK)), whole((KV_B, GROUP))],
        out_shape=[jax.ShapeDtypeStruct((S, PROJ_P), BF16), jax.ShapeDtypeStruct((S + 2 * WINDOW, KV_B * HD_B), F32),
                   jax.ShapeDtypeStruct((S + 2 * WINDOW, KV_B * HD_B), F32), jax.ShapeDtypeStruct((H_B, SPAN, Q_BLOCK), F32),
                   jax.ShapeDtypeStruct((KV_B, GROUP), F32)],
        scratch_shapes=[pltpu.VMEM((KV_B, GROUP * Q_BLOCK), F32)],
        dims=("arbitrary",), aliases={10: 0})(proj, kp, kp, kp, vp, vp, vp, bias_t, sinks_b, do_b, d_proj)


def _bias_table(rel_bias_t, onehot_t, in_band):
    def body(rb_ref, oh_ref, band_ref, o_ref):
        t = lax.dot_general(rb_ref[...], oh_ref[...], NN, preferred_element_type=F32, precision=lax.Precision.HIGHEST)
        o_ref[...] = jnp.where(band_ref[...] > 0.5, t, -1e30)

    n = onehot_t.shape[1]
    tn = _tile(n, 8192)
    return _pcall(body, name="bias_table", grid=(n // tn,),
                  in_specs=[pl.BlockSpec((H_B, NUM_BUCKETS), lambda j: (0, 0)), pl.BlockSpec((NUM_BUCKETS, tn), lambda j: (0, j)),
                            pl.BlockSpec((1, tn), lambda j: (0, j))],
                  out_specs=pl.BlockSpec((H_B, tn), lambda j: (0, j)),
                  out_shape=jax.ShapeDtypeStruct((H_B, n), F32), dims=("parallel",))(rel_bias_t, onehot_t, in_band)


def _bias_table_bwd(dbias, onehot_t):
    n = onehot_t.shape[1]
    tk = _tile(n, 8192)

    def body(d_ref, oh_ref, o_ref):
        @pl.when(pl.program_id(0) == 0)
        def _():
            o_ref[...] = jnp.zeros_like(o_ref)

        o_ref[...] += lax.dot_general(d_ref[...], oh_ref[...], NT, preferred_element_type=F32, precision=lax.Precision.HIGHEST)

    return _pcall(body, name="bias_table_bwd", grid=(n // tk,),
                  in_specs=[pl.BlockSpec((H_B, tk), lambda j: (0, j)), pl.BlockSpec((NUM_BUCKETS, tk), lambda j: (0, j))],
                  out_specs=pl.BlockSpec((H_B, NUM_BUCKETS), lambda j: (0, 0)),
                  out_shape=jax.ShapeDtypeStruct((H_B, NUM_BUCKETS), F32), dims=("arbitrary",))(dbias, onehot_t)


def _shift_down(u):
    row = lax.broadcasted_iota(jnp.int32, u.shape, 0)
    return jnp.where(row == 0, 0.0, pltpu.roll(u, 1, axis=0))


def _shift_up(u):
    n = u.shape[0]
    row = lax.broadcasted_iota(jnp.int32, u.shape, 0)
    return jnp.where(row == n - 1, 0.0, pltpu.roll(u, n - 1, axis=0))


def _conv(u, w, b):
    return _shift_down(u) * w[0:1, :] + u * w[1:2, :] + _shift_up(u) * w[2:3, :] + b


CONV_STRIP = 128
N_STRIPS = D_FF // CONV_STRIP


def _strip(rows, half):
    return pl.BlockSpec((rows, CONV_STRIP), lambda j: (0, j + half * N_STRIPS))


def _conv_gate_fwd(u, conv_w, conv_b, S):
    def body(ug_ref, uv_ref, wg_ref, wv_ref, bg_ref, bv_ref, a_ref):
        g = _conv(ug_ref[...].astype(F32), wg_ref[...], bg_ref[...])
        val = _conv(uv_ref[...].astype(F32), wv_ref[...], bv_ref[...])
        a_ref[...] = (g * _sigmoid(g) * val).astype(BF16)

    return _pcall(body, name="conv_gate_fwd", grid=(N_STRIPS,),
                  in_specs=[_strip(S, 0), _strip(S, 1), _strip(3, 0), _strip(3, 1), _strip(1, 0), _strip(1, 1)],
                  out_specs=_strip(S, 0), out_shape=jax.ShapeDtypeStruct((S, D_FF), BF16),
                  dims=("parallel",))(u, u, conv_w, conv_w, conv_b, conv_b)


def _conv_gate_bwd(u, conv_w, conv_b, da, S):
    def conv_bwd(duc, u_in, w):
        du = _shift_up(duc) * w[0:1, :] + duc * w[1:2, :] + _shift_down(duc) * w[2:3, :]
        dw = jnp.concatenate([jnp.sum(duc * _shift_down(u_in), axis=0, keepdims=True),
                              jnp.sum(duc * u_in, axis=0, keepdims=True),
                              jnp.sum(duc * _shift_up(u_in), axis=0, keepdims=True)], axis=0)
        return du, dw, jnp.sum(duc, axis=0, keepdims=True)

    def body(ug_ref, uv_ref, wg_ref, wv_ref, bg_ref, bv_ref, da_ref, dug_ref, duv_ref, dwg_ref, dwv_ref, dbg_ref, dbv_ref):
        u_g, u_v, w_g, w_v = ug_ref[...].astype(F32), uv_ref[...].astype(F32), wg_ref[...], wv_ref[...]
        g = _conv(u_g, w_g, bg_ref[...])
        val = _conv(u_v, w_v, bv_ref[...])
        da = da_ref[...].astype(F32)
        sg = _sigmoid(g)
        dval = da * (g * sg)
        dg = da * val * (sg * (1.0 + g * (1.0 - sg)))
        du_g, dwg_ref[...], dbg_ref[...] = conv_bwd(dg, u_g, w_g)
        du_v, dwv_ref[...], dbv_ref[...] = conv_bwd(dval, u_v, w_v)
        dug_ref[...] = du_g.astype(BF16)
        duv_ref[...] = du_v.astype(BF16)

    half = lambda r, dt: (_strip(r, 0), jax.ShapeDtypeStruct((r, D_FF), dt))
    outs = [half(S, BF16), half(S, BF16), half(3, F32), half(3, F32), half(1, F32), half(1, F32)]
    return _pcall(
        body, name="conv_gate_bwd", grid=(N_STRIPS,),
        in_specs=[_strip(S, 0), _strip(S, 1), _strip(3, 0), _strip(3, 1), _strip(1, 0), _strip(1, 1), _strip(S, 0)],
        out_specs=[o[0] for o in outs], out_shape=[o[1] for o in outs],
        dims=("parallel",))(u, u, conv_w, conv_w, conv_b, conv_b, da)


MESH = pl.DeviceIdType.MESH
ANY = pl.BlockSpec(memory_space=pl.ANY)


def _place():
    return lax.axis_index("x"), lax.axis_index("y"), lax.axis_index("c")


def _all_gather(shards):
    n_arr = len(shards)

    def body(*refs):
        ins, outs = refs[:n_arr], refs[n_arr:2 * n_arr]
        send_sems, recv_sems, local_sems = refs[2 * n_arr:]
        x, y, c = _place()
        me, sibling = (x, y, c), (x, y, 1 - c)
        chips = [(1 - x, y), (x, 1 - y), (1 - x, 1 - y)]

        def slot(a, p):
            return outs[a].at[4 * p[0] + 2 * p[1] + p[2]]

        def copy(a, k, block, to, src=None):
            return pltpu.make_async_remote_copy(
                src_ref=slot(a, block) if src is None else src, dst_ref=slot(a, block),
                send_sem=send_sems.at[a, k], recv_sem=recv_sems.at[a, k], device_id=to, device_id_type=MESH)

        mine = [pltpu.make_async_copy(ins[a], slot(a, me), local_sems.at[a]) for a in range(n_arr)]
        for cp in mine:
            cp.start()
        first = []
        for a in range(n_arr):
            first.append(copy(a, 0, me, sibling, src=ins[a]))
            first += [copy(a, 1 + j, me, (*chip, c), src=ins[a]) for j, chip in enumerate(chips)]
        for cp in first:
            cp.start()
        passed = []
        for j, chip in enumerate(chips):
            for a in range(n_arr):
                copy(a, 1 + j, (*chip, c), me).wait_recv()
                cp = copy(a, 4 + j, (*chip, c), sibling)
                cp.start()
                passed.append(cp)
        for a in range(n_arr):
            copy(a, 0, sibling, me).wait_recv()
            for j, chip in enumerate(chips):
                copy(a, 4 + j, (*chip, 1 - c), me).wait_recv()
        for cp in first + passed:
            cp.wait_send()
        for cp in mine:
            cp.wait()

    return pl.pallas_call(
        body, name="all_gather_weights",
        in_specs=[ANY] * n_arr, out_specs=[ANY] * n_arr,
        out_shape=[jax.ShapeDtypeStruct((N_DEV, *s.shape), s.dtype) for s in shards],
        scratch_shapes=[pltpu.SemaphoreType.DMA((n_arr, 7)), pltpu.SemaphoreType.DMA((n_arr, 7)), pltpu.SemaphoreType.DMA((n_arr,))],
    )(*shards)


def _xchg_out_shapes(stacked, replicated):
    return ([jax.ShapeDtypeStruct(s.shape, s.dtype) for s in stacked]
            + [jax.ShapeDtypeStruct((N_DEV, *r.shape), r.dtype) for r in replicated])


def _xchg_sems(n_arr):
    return [pltpu.SemaphoreType.DMA((n_arr, 7)), pltpu.SemaphoreType.DMA((n_arr, 7)), pltpu.SemaphoreType.DMA((n_arr,))]


def _xchg_copies(ins, outs, sems, n_st, with_recv):
    send_sems, recv_sems, local_sems = sems
    n_arr = len(ins)
    x, y, c = _place()
    me = 4 * x + 2 * y + c

    def src(a, idx):
        return ins[a].at[idx] if a < n_st else ins[a]

    mine = [pltpu.make_async_copy(src(a, me), outs[a].at[me], local_sems.at[a]) for a in range(n_arr)]
    pairs = []
    for k in range(1, N_DEV):
        px, py, pc = x ^ (k >> 2), y ^ ((k >> 1) & 1), c ^ (k & 1)
        peer = 4 * px + 2 * py + pc
        for a in range(n_arr):
            sems_k = dict(send_sem=send_sems.at[a, k - 1], recv_sem=recv_sems.at[a, k - 1], device_id_type=MESH)
            send = pltpu.make_async_remote_copy(src_ref=src(a, peer), dst_ref=outs[a].at[me], device_id=(px, py, pc), **sems_k)
            recv = None
            if with_recv:
                recv = pltpu.make_async_remote_copy(src_ref=src(a, peer), dst_ref=outs[a].at[peer], device_id=(x, y, c), **sems_k)
            pairs.append((send, recv))
    return mine, pairs


def _xchg_start(ins, outs, sems, n_st):
    mine, pairs = _xchg_copies(ins, outs, sems, n_st, False)
    for cp in mine:
        cp.start()
    for send, _ in pairs:
        send.start()


def _xchg_finish(ins, outs, sems, n_st):
    mine, pairs = _xchg_copies(ins, outs, sems, n_st, True)
    for _, recv in pairs:
        recv.wait_recv()
    for send, _ in pairs:
        send.wait_send()
    for cp in mine:
        cp.wait()


def _exchange(stacked, replicated, name):
    _, landed = _pcall(lambda: None, name=name, grid=(), in_specs=[], out_specs=[], out_shape=[], comm=(stacked, replicated))()
    return landed


def _adamw(parts, w, m, v, name):
    R, C = w.shape
    tr = R if (R <= 512 or R % 256) else 256

    def body(p_ref, w_ref, m_ref, v_ref, g_ref, d_ref, nm_ref, nv_ref):
        g = p_ref[0].astype(F32)
        for s in range(1, N_DEV):
            g = g + p_ref[s].astype(F32)
        m2 = ADAM_B1 * m_ref[...] + (1.0 - ADAM_B1) * g
        v2 = ADAM_B2 * v_ref[...] + (1.0 - ADAM_B2) * (g * g)
        m_hat = m2 / (1.0 - ADAM_B1 ** ADAM_STEP)
        v_hat = v2 / (1.0 - ADAM_B2 ** ADAM_STEP)
        g_ref[...] = g
        d_ref[...] = -ADAM_LR * (m_hat / (jnp.sqrt(v_hat) + ADAM_EPS) + ADAM_WD * w_ref[...])
        nm_ref[...] = m2
        nv_ref[...] = v2

    blk = pl.BlockSpec((tr, C), lambda i: (i, 0))
    return _pcall(body, name=name, grid=(R // tr,),
                  in_specs=[pl.BlockSpec((N_DEV, tr, C), lambda i: (0, i, 0)), blk, blk, blk],
                  out_specs=[blk] * 4, out_shape=[jax.ShapeDtypeStruct((R, C), F32)] * 4,
                  dims=("parallel",))(parts, w, m, v)


def _t5_bucket(rel):
    nb = NUM_BUCKETS // 2
    max_exact = nb // 2
    base = (rel > 0).astype(jnp.int32) * nb
    n = jnp.abs(rel)
    nf = jnp.maximum(n, 1).astype(jnp.float32)
    large = max_exact + (jnp.log(nf / max_exact) / math.log(MAX_DISTANCE / max_exact) * (nb - max_exact)).astype(jnp.int32)
    large = jnp.minimum(large, nb - 1)
    return base + jnp.where(n < max_exact, n, large)


def _unstack_cols(g):
    return jnp.transpose(g, (1, 0, 2)).reshape(g.shape[1], N_DEV * g.shape[2])


def _stack_cols(w, n=N_DEV):
    R = w.shape[0]
    return jnp.transpose(w.reshape(R, n, w.shape[1] // n), (1, 0, 2))


def _stack_halves(g, v):
    return jnp.concatenate([_stack_cols(g, N_DEV // 2), _stack_cols(v, N_DEV // 2)], axis=0)


def kernel(x, positions, norm1_g, w_in, q_a_norm_g, w_q_b, kv_a_norm_g, w_kv_b, rel_bias, sinks, w_out, norm2_g, w_up, conv_w, conv_b, w_down, final_norm_g, loss_target, m_norm1_g, m_w_in, m_q_a_norm_g, m_w_q_b, m_kv_a_norm_g, m_w_kv_b, m_rel_bias, m_sinks, m_w_out, m_norm2_g, m_w_up, m_conv_w, m_conv_b, m_w_down, m_final_norm_g, v_norm1_g, v_w_in, v_q_a_norm_g, v_w_q_b, v_kv_a_norm_g, v_w_kv_b, v_rel_bias, v_sinks, v_w_out, v_norm2_g, v_w_up, v_conv_w, v_conv_b, v_w_down, v_final_norm_g):
    S = x.shape[1]
    x = x[0]
    target = loss_target[0]
    TM = 256

    g_in, g_qb, g_kvb = _all_gather([w_in[0].astype(BF16), w_q_b[0].astype(BF16), w_kv_b[0].astype(BF16)])
    late_weights = [w_out[0].astype(BF16), w_up[0].astype(BF16), w_down[0].astype(BF16), conv_w[0]]
    wi = _unstack_cols(g_in)
    c0, c1, c2, c3, c4, c5 = (sum(W_IN_SIZES[:i + 1]) for i in range(6))
    w_in_p = jnp.concatenate([wi[:, c4:c5], wi[:, c5:], wi[:, c1:c2], wi[:, :c0], wi[:, c2:c3], wi[:, c3:c4],
                              wi[:, c0:c0 + KV_LORA], wi[:, c0 + KV_LORA:c1], jnp.zeros((D_MODEL, 64), BF16)], axis=1)
    wq = _unstack_cols(g_qb).reshape(Q_LORA, H_A, QK_HEAD)
    w_qb_p = jnp.concatenate([wq[:, :, :QK_NOPE].reshape(Q_LORA, H_A * QK_NOPE), wq[:, :, QK_NOPE:].reshape(Q_LORA, H_A * QK_ROPE)], axis=1)
    w_kvb = _unstack_cols(g_kvb)

    half = QK_ROPE // 2
    inv_freq = ROPE_THETA ** (-jnp.arange(half, dtype=F32) / half)
    ang = positions.astype(F32)[:, None] * inv_freq[None, :]
    cos, sin = jnp.cos(ang), jnp.sin(ang)
    qa = jnp.arange(Q_BLOCK, dtype=jnp.int32)[:, None]
    kc = jnp.arange(SPAN, dtype=jnp.int32)[None, :]
    rel = (kc - WINDOW - qa).T
    in_band = (jnp.abs(rel) <= WINDOW).astype(F32).reshape(1, Q_BLOCK * SPAN)
    onehot_t = (_t5_bucket(rel).reshape(1, Q_BLOCK * SPAN) == jnp.arange(NUM_BUCKETS, dtype=jnp.int32)[:, None]).astype(F32)
    bias_t = _bias_table(rel_bias.T, onehot_t, in_band).reshape(H_B, SPAN, Q_BLOCK)
    sinks_b = jnp.broadcast_to(sinks.reshape(H_B, 1), (H_B, Q_BLOCK))

    (h1,) = _rowwise(lambda a, g: (_rms(a, g),), "norm1", S, TM, [_rows(x), _whole(norm1_g)], [("rows", D_MODEL, BF16)])
    proj = _matmul(h1, w_in_p, "nn", BF16, "proj")

    def lat_fn(qlat, ckv, kr, gq, gkv, cs, sn):
        r1, r2 = _rope(kr[:, :half], kr[:, half:QK_ROPE], cs, sn)
        return _rms(qlat, gq), _rms(ckv, gkv), jnp.concatenate([r1, r2], axis=1)

    qn, ckvn, k_rope = _rowwise(lat_fn, "latents", S, TM,
                                [_rows(proj, 256, PROJ_QLAT), _rows(proj, 128, PROJ_CKV), _rows(proj, 128, PROJ_KROPE),
                                 _whole(q_a_norm_g), _whole(kv_a_norm_g), _rows(cos), _rows(sin)],
                                [("rows", Q_LORA, BF16), ("rows", KV_LORA, BF16), ("rows", QK_ROPE, BF16)])
    q_p = _matmul(qn, w_qb_p, "nn", F32, "q_up", tn=1536)
    kv = _matmul(ckvn, w_kvb, "nn", BF16, "kv_up", tn=2048)

    def q_heads_fn(q, cs, sn):
        outs = []
        for h in range(H_A):
            o = H_A * QK_NOPE + QK_ROPE * h
            r1, r2 = _rope(q[:, o:o + half], q[:, o + half:o + QK_ROPE], cs, sn)
            outs.append(jnp.concatenate([q[:, QK_NOPE * h:QK_NOPE * (h + 1)], r1, r2], axis=1)[None])
        return (jnp.concatenate(outs, axis=0),)

    (q_full,) = _rowwise(q_heads_fn, "q_heads", S, TM, [_rows(q_p), _rows(cos), _rows(sin)], [("heads", H_A, QK_HEAD, BF16)])

    def k_heads_fn(kvb, kr):
        return (jnp.concatenate([jnp.concatenate([kvb[:, 256 * h:256 * h + QK_NOPE], kr], axis=1)[None] for h in range(H_A)], axis=0),)

    (k_full,) = _rowwise(k_heads_fn, "k_heads", S, TM, [_rows(kv), _rows(k_rope)], [("heads", H_A, QK_HEAD, BF16)], upcast=False)
    (o_a, lse), (g_out, g_up, g_down, g_cw) = _mla_fwd(q_full, k_full, kv, S, comm=([], late_weights))
    w_out_f = g_out.reshape(D_MODEL, D_MODEL)
    w_up_f = _unstack_cols(g_up)
    w_down_f = g_down.reshape(D_FF, D_MODEL)
    conv_w_f = _unstack_cols(g_cw)

    kp = jnp.pad(proj[:, 256 * PROJ_KB:256 * (PROJ_KB + 1)], ((WINDOW, WINDOW), (0, 0)))
    vp = jnp.pad(proj[:, 256 * PROJ_VB:256 * (PROJ_VB + 1)], ((WINDOW, WINDOW), (0, 0)))
    o_b = _win_fwd(proj, kp, vp, bias_t, sinks_b, S)

    (mixed,) = _rowwise(lambda ga, gb, oa, ob: (_sigmoid(ga) * oa + _sigmoid(gb) * ob,), "gate_mix", S, TM,
                        [_rows(proj, 1024, PROJ_GA), _rows(proj, 1024, PROJ_GB), _rows(o_a), _rows(o_b)], [("rows", D_MODEL, BF16)])
    x1 = _matmul(mixed, w_out_f, "nn", F32, "out_proj", residual=x)
    (h2,) = _rowwise(lambda a, g: (_rms(a, g),), "norm2", S, TM, [_rows(x1), _whole(norm2_g)], [("rows", D_MODEL, BF16)])
    u = _matmul(h2, w_up_f, "nn", BF16, "ffn_up", tn=1408)
    act = _conv_gate_fwd(u, conv_w_f, conv_b, S)
    x2 = _matmul(act, w_down_f, "nn", F32, "ffn_down", residual=x1, tm=512)

    def final_fn(a, g, t):
        err = _rms(a, g) - t
        loss = 0.5 * jnp.sum(jnp.mean(err * err, axis=-1, keepdims=True), axis=0, keepdims=True)
        dx, dg = _rms_bwd(err * (1.0 / D_MODEL), a, g)
        return dx, dx, dg, jnp.broadcast_to(loss, (1, 128))

    gfin = final_norm_g.reshape(1, D_MODEL)
    dx2, dx2_b, d_gfin, loss_row = _rowwise(final_fn, "loss_head", S, TM, [_rows(x2), _whole(gfin), _rows(target)],
                                            [("rows", D_MODEL, F32), ("rows", D_MODEL, BF16), ("acc", 1, D_MODEL), ("acc", 1, 128)])
    d_act = _matmul(dx2_b, w_down_f, "nt", BF16, "ffn_down_dx", tn=1408)
    d_w_down = _matmul(act, dx2_b, "tn", F32, "ffn_down_dw")
    du_g, du_v, dcw_g, dcw_v, dcb_g, dcb_v = _conv_gate_bwd(u, conv_w_f, conv_b, d_act, S)
    d_conv_b = jnp.concatenate([dcb_g, dcb_v], axis=1)
    d_h2 = _matmul(du_g, w_up_f, "nt", F32, "ffn_up_dx", tm=512, a2=du_v)
    d_w_up_g = _matmul(h2, du_g, "tn", F32, "ffn_up_dw_gate", tm=512, tn=1408)
    d_w_up_v = _matmul(h2, du_v, "tn", F32, "ffn_up_dw_value", tm=512, tn=1408)

    def norm_bwd_fn(dh, a, g, dres):
        dx, dg = _rms_bwd(dh, a, g)
        dx = dx + dres
        return dx, dx, dg

    dx1, dx1_b, d_g2 = _rowwise(norm_bwd_fn, "norm2_bwd", S, TM, [_rows(d_h2), _rows(x1), _whole(norm2_g), _rows(dx2)],
                                [("rows", D_MODEL, F32), ("rows", D_MODEL, BF16), ("acc", 1, D_MODEL)])
    d_mixed = _matmul(dx1_b, w_out_f, "nt", BF16, "out_proj_dx")
    d_w_out = _matmul(mixed, dx1_b, "tn", F32, "out_proj_dw", tm=512)

    def gate_bwd_fn(dm, ga, gb, oa, ob):
        sa, sb = _sigmoid(ga), _sigmoid(gb)
        return jnp.concatenate([dm * oa * sa * (1.0 - sa), dm * ob * sb * (1.0 - sb)], axis=1), dm * sa, dm * sb

    d_proj, do_a, do_b = _rowwise(gate_bwd_fn, "gate_bwd", S, TM,
                                  [_rows(d_mixed), _rows(proj, 1024, PROJ_GA), _rows(proj, 1024, PROJ_GB), _rows(o_a), _rows(o_b)],
                                  [("cols", 2 * D_MODEL, 0, PROJ_P, BF16), ("rows", D_MODEL, BF16), ("rows", D_MODEL, BF16)])

    d_proj, dk_acc, dv_acc, d_bias, d_sinks_g = _win_bwd(proj, kp, vp, bias_t, sinks_b, do_b, d_proj, S)
    d_rel_bias = _bias_table_bwd(d_bias.reshape(H_B, Q_BLOCK * SPAN), onehot_t).T
    d_sinks = d_sinks_g.reshape(1, H_B)

    early = [d_w_out.reshape(N_DEV, D_MODEL // N_DEV, D_MODEL).astype(BF16), _stack_halves(d_w_up_g, d_w_up_v).astype(BF16),
             d_w_down.reshape(N_DEV, D_FF // N_DEV, D_MODEL).astype(BF16), _stack_halves(dcw_g, dcw_v)]
    (dq_full, dk_full, dv_full), recv_early = _mla_bwd(q_full, k_full, kv, do_a, o_a, lse, S, comm=(early, []))

    def dq_post_fn(dq, cs, sn):
        nope = [dq[h, :, :QK_NOPE] for h in range(H_A)]
        rope = []
        for h in range(H_A):
            rope += list(_rope_bwd(dq[h, :, QK_NOPE:QK_NOPE + half], dq[h, :, QK_NOPE + half:], cs, sn))
        return (jnp.concatenate(nope + rope, axis=1),)

    (dq_p,) = _rowwise(dq_post_fn, "dq_post", S, TM, [_heads(dq_full), _rows(cos), _rows(sin)], [("rows", H_A * QK_HEAD, BF16)])

    def dkv_post_fn(dk, dv, cs, sn):
        dkv = jnp.concatenate([jnp.concatenate([dk[h, :, :QK_NOPE], dv[h]], axis=1) for h in range(H_A)], axis=1)
        dkr = dk[0, :, QK_NOPE:]
        for h in range(1, H_A):
            dkr = dkr + dk[h, :, QK_NOPE:]
        r1, r2 = _rope_bwd(dkr[:, :half], dkr[:, half:], cs, sn)
        return dkv, jnp.concatenate([r1, r2], axis=1)

    dkv, d_krope = _rowwise(dkv_post_fn, "dkv_post", S, TM, [_heads(dk_full), _heads(dv_full), _rows(cos), _rows(sin)],
                            [("rows", H_A * (QK_NOPE + V_DIM), BF16), ("rows", QK_ROPE, F32)])
    d_qn = _matmul(dq_p, w_qb_p, "nt", F32, "q_up_dx")
    d_w_qb_p = _matmul(qn, dq_p, "tn", F32, "q_up_dw", tn=1536)
    d_ckvn = _matmul(dkv, w_kvb, "nt", F32, "kv_up_dx")
    d_w_kvb = _matmul(ckvn, dkv, "tn", F32, "kv_up_dw", tn=2048)

    def lat_bwd_fn(dqn, dckvn, dkr, qlat, ckv, gq, gkv, dkb, dvb):
        dql, dgq = _rms_bwd(dqn, qlat, gq)
        dck, dgkv = _rms_bwd(dckvn, ckv, gkv)
        tail = jnp.concatenate([dql, dkb, dvb, dck, dkr, jnp.zeros_like(dkr)], axis=1)
        return tail, dgq, dgkv

    shifted = lambda arr: (arr, lambda tm: pl.BlockSpec((tm, arr.shape[1]), lambda i: (i + WINDOW // tm, 0)))
    TL = min(128, S)
    d_proj, d_gq, d_gkv = _rowwise(lat_bwd_fn, "latents_bwd", S, TL,
                                   [_rows(d_qn), _rows(d_ckvn), _rows(d_krope), _rows(proj, 256, PROJ_QLAT), _rows(proj, 128, PROJ_CKV),
                                    _whole(q_a_norm_g), _whole(kv_a_norm_g), shifted(dk_acc), shifted(dv_acc)],
                                   [("cols", 1024, 3, PROJ_P, BF16), ("acc", 1, Q_LORA), ("acc", 1, KV_LORA)], into=(d_proj, 0))
    d_w_in_p = _matmul(h1, d_proj, "tn", F32, "proj_dw", tm=512)

    dp = d_w_in_p
    d_w_in = jnp.concatenate([dp[:, 3072:3328], dp[:, 3840:3968], dp[:, 3968:4032], dp[:, 2048:3072], dp[:, 3328:3584],
                              dp[:, 3584:3840], dp[:, 0:1024], dp[:, 1024:2048]], axis=1)
    d_w_qb = jnp.concatenate([d_w_qb_p[:, :H_A * QK_NOPE].reshape(Q_LORA, H_A, QK_NOPE),
                              d_w_qb_p[:, H_A * QK_NOPE:].reshape(Q_LORA, H_A, QK_ROPE)], axis=2).reshape(Q_LORA, H_A * QK_HEAD)
    late = [_stack_cols(d_w_in).astype(BF16), _stack_cols(d_w_qb).astype(BF16), _stack_cols(d_w_kvb).astype(BF16)]
    d_h1, recv_late = _matmul(d_proj, w_in_p, "nt", F32, "proj_dx", tm=512, comm=(late, []))
    grad_x, _, d_g1 = _rowwise(norm_bwd_fn, "norm1_bwd", S, TM, [_rows(d_h1), _rows(x), _whole(norm1_g), _rows(dx1)],
                               [("rows", D_MODEL, F32), ("rows", D_MODEL, BF16), ("acc", 1, D_MODEL)])
    recv = [*recv_late, *recv_early]

    small_parts = [d_g1, d_gq, d_gkv, d_rel_bias.reshape(1, NUM_BUCKETS * H_B), d_sinks, d_g2, d_conv_b, d_gfin]
    small = jnp.concatenate(small_parts, axis=1)
    n_small = small.shape[1]
    pad = (-n_small) % 128
    small = jnp.pad(small, ((0, 0), (0, pad)))
    (recv_small,) = _exchange([], [small], "exchange_small_grads")

    def flat(a):
        return a.reshape(1, -1)

    small_w = [norm1_g, q_a_norm_g, kv_a_norm_g, rel_bias, sinks, norm2_g, conv_b, final_norm_g]
    small_m = [m_norm1_g, m_q_a_norm_g, m_kv_a_norm_g, m_rel_bias, m_sinks, m_norm2_g, m_conv_b, m_final_norm_g]
    small_v = [v_norm1_g, v_q_a_norm_g, v_kv_a_norm_g, v_rel_bias, v_sinks, v_norm2_g, v_conv_b, v_final_norm_g]
    cat = lambda parts: jnp.pad(jnp.concatenate([flat(a) for a in parts], axis=1), ((0, 0), (0, pad)))
    sm = _adamw(recv_small, cat(small_w), cat(small_m), jnp.pad(jnp.concatenate([flat(a) for a in small_v], axis=1), ((0, 0), (0, pad)), constant_values=1.0), "adamw_small")
    big_names = ["w_in", "w_q_b", "w_kv_b", "w_out", "w_up", "w_down", "conv_w"]
    big_w = [w_in, w_q_b, w_kv_b, w_out, w_up, w_down, conv_w]
    big_m = [m_w_in, m_w_q_b, m_w_kv_b, m_w_out, m_w_up, m_w_down, m_conv_w]
    big_v = [v_w_in, v_w_q_b, v_w_kv_b, v_w_out, v_w_up, v_w_down, v_conv_w]
    big = {n: _adamw(r, w[0], m[0], v[0], "adamw_" + n) for n, r, w, m, v in zip(big_names, recv, big_w, big_m, big_v)}

    loss = lax.psum(loss_row[0, 0], ("x", "y", "c"))
    order = ["norm1_g", "w_in", "q_a_norm_g", "w_q_b", "kv_a_norm_g", "w_kv_b", "rel_bias", "sinks", "w_out", "norm2_g", "w_up",
             "conv_w", "conv_b", "w_down", "final_norm_g"]
    small_names = ["norm1_g", "q_a_norm_g", "kv_a_norm_g", "rel_bias", "sinks", "norm2_g", "conv_b", "final_norm_g"]
    offs, o = {}, 0
    for n, a in zip(small_names, small_w):
        offs[n] = (o, a.size, a.shape)
        o += a.size
    outs = [loss, grad_x[None]]
    for kind in range(4):
        for n in order:
            if n in big:
                outs.append(big[n][kind][None])
            else:
                o, size, shape = offs[n]
                outs.append(sm[kind][0, o:o + size].reshape(shape))
    return tuple(outs)
```

```python
import math

import jax
import jax.numpy as jnp
from jax import lax
from jax.experimental import pallas as pl
from jax.experimental.pallas import tpu as pltpu

F32 = jnp.float32
BF16 = jnp.bfloat16

N_DEV = 8
D_MODEL = 1024
EPS = 1e-6
H_A, QK_NOPE, QK_ROPE, V_DIM, Q_LORA, KV_LORA = 8, 128, 64, 128, 256, 128
QK_HEAD = QK_NOPE + QK_ROPE
ROPE_THETA = 10000.0
H_B, KV_B, GROUP, HD_B, WINDOW, Q_BLOCK = 16, 4, 4, 64, 128, 128
SPAN = Q_BLOCK + 2 * WINDOW
NUM_BUCKETS, MAX_DISTANCE = 32, 128
D_FF = 2816
ADAM_LR, ADAM_B1, ADAM_B2, ADAM_EPS, ADAM_WD, ADAM_STEP = 0.001, 0.9, 0.999, 1e-08, 0.01, 10

W_IN_SIZES = (Q_LORA, KV_LORA + QK_ROPE, H_B * HD_B, KV_B * HD_B, KV_B * HD_B, D_MODEL, D_MODEL)
W_IN_COLS = sum(W_IN_SIZES)
PROJ_P = 4096
PROJ_GA, PROJ_GB, PROJ_QB, PROJ_QLAT, PROJ_KB, PROJ_VB, PROJ_CKV, PROJ_KROPE = 0, 1, 2, 12, 13, 14, 30, 31

VMEM_LIMIT = 56 * 1024 * 1024

NN = (((1,), (0,)), ((), ()))
NT = (((1,), (1,)), ((), ()))
TN = (((0,), (0,)), ((), ()))


def _pcall(body, *, name, grid, in_specs, out_specs, out_shape, scratch_shapes=(), dims=None, comm=None, aliases=None):
    if comm is None:
        params = pltpu.CompilerParams(dimension_semantics=dims, vmem_limit_bytes=VMEM_LIMIT)
        return pl.pallas_call(body, name=name, grid=grid, in_specs=in_specs, out_specs=out_specs, out_shape=out_shape,
                              scratch_shapes=list(scratch_shapes), input_output_aliases=aliases or {}, compiler_params=params)
    assert not aliases
    stacked, replicated = comm
    arrs = [*stacked, *replicated]
    n_st, n_arr = len(stacked), len(arrs)
    single = not isinstance(out_specs, (list, tuple))
    o_specs, o_shape = ([out_specs], [out_shape]) if single else (list(out_specs), list(out_shape))
    n_in, n_out = len(in_specs), len(o_specs)

    def wrapped(*refs):
        c_in = refs[n_in:n_in + n_arr]
        c_out = refs[n_in + n_arr + n_out:n_in + 2 * n_arr + n_out]
        sems = refs[len(refs) - 3:]
        own = (*refs[:n_in], *refs[n_in + n_arr:n_in + n_arr + n_out], *refs[n_in + 2 * n_arr + n_out:len(refs) - 3])
        if not grid:
            _xchg_start(c_in, c_out, sems, n_st)
            _xchg_finish(c_in, c_out, sems, n_st)
            return
        first = last = None
        for d, n in enumerate(grid):
            pid = pl.program_id(d)
            first = (pid == 0) if first is None else first & (pid == 0)
            last = (pid == n - 1) if last is None else last & (pid == n - 1)

        @pl.when(first)
        def _():
            _xchg_start(c_in, c_out, sems, n_st)

        body(*own)

        @pl.when(last)
        def _():
            _xchg_finish(c_in, c_out, sems, n_st)

    params = pltpu.CompilerParams(dimension_semantics=("arbitrary",) * len(grid), vmem_limit_bytes=VMEM_LIMIT)
    call = pl.pallas_call(wrapped, name=name, grid=grid, in_specs=[*in_specs, *[ANY] * n_arr], out_specs=[*o_specs, *[ANY] * n_arr],
                          out_shape=[*o_shape, *_xchg_out_shapes(stacked, replicated)],
                          scratch_shapes=[*scratch_shapes, *_xchg_sems(n_arr)], compiler_params=params)

    def run(*args):
        res = call(*args, *arrs)
        outs, landed = res[:n_out], res[n_out:]
        return (outs[0] if single else outs), landed

    return run


def _dot(a, b, dn):
    return lax.dot_general(a, b, dn, preferred_element_type=F32)


def _tile(n, target):
    best = None
    for t in range(128, min(n, target) + 1, 128):
        if n % t == 0:
            best = t
    return n if best is None else best


def _matmul(a, b, mode, out_dtype, name, residual=None, tm=1024, tn=1024, comm=None, a2=None):
    if mode == "nn":
        (M, K), N = a.shape, b.shape[1]
    elif mode == "nt":
        (M, K), N = a.shape, b.shape[0]
    else:
        (K, M), N = a.shape, b.shape[1]
    tm, tn = _tile(M, tm), _tile(N, tn)
    a_spec = pl.BlockSpec((K, tm), lambda i, j: (0, i)) if mode == "tn" else pl.BlockSpec((tm, K), lambda i, j: (i, 0))
    b_spec = pl.BlockSpec((tn, b.shape[1]), lambda i, j: (j, 0)) if mode == "nt" else pl.BlockSpec((K, tn), lambda i, j: (0, j))
    o_spec = pl.BlockSpec((tm, tn), lambda i, j: (i, j))
    in_specs, args = [a_spec, b_spec], [a, b]
    if a2 is not None:
        assert mode == "nt" and K + a2.shape[1] == b.shape[1]
        in_specs.append(pl.BlockSpec((tm, a2.shape[1]), lambda i, j: (i, 0)))
        args.append(a2)
    if residual is not None:
        in_specs.append(o_spec)
        args.append(residual)
    scratch = [pltpu.VMEM((tm, K), a.dtype)] if mode == "tn" else []

    def body(*refs):
        a_ref, b_ref = refs[0], refs[1]
        o_ref = refs[len(args)]
        if mode == "tn":
            at_ref = refs[len(args) + 1]

            @pl.when(pl.program_id(1) == 0)
            def _():
                at_ref[...] = a_ref[...].T

            acc = _dot(at_ref[...], b_ref[...], NN)
        elif a2 is not None:
            acc = _dot(a_ref[...], b_ref[:, :K], NT) + _dot(refs[2][...], b_ref[:, K:], NT)
        else:
            acc = _dot(a_ref[...], b_ref[...], NT if mode == "nt" else NN)
        if residual is not None:
            acc = acc + refs[len(args) - 1][...]
        o_ref[...] = acc.astype(out_dtype)

    return _pcall(body, name=name, grid=(M // tm, N // tn), in_specs=in_specs, out_specs=o_spec,
                  out_shape=jax.ShapeDtypeStruct((M, N), out_dtype), scratch_shapes=scratch,
                  dims=("parallel", "arbitrary"), comm=comm)(*args)


def _rows(arr, width=None, col=0):
    width = arr.shape[1] if width is None else width
    return (arr, lambda tm: pl.BlockSpec((tm, width), lambda i: (i, col)))


def _heads(arr):
    return (arr, lambda tm: pl.BlockSpec((arr.shape[0], tm, arr.shape[2]), lambda i: (0, i, 0)))


def _whole(arr):
    nd = arr.ndim
    return (arr, lambda tm: pl.BlockSpec(arr.shape, lambda i: (0,) * nd))


def _rowwise(fn, name, n_rows, tm, ins, outs, upcast=True, into=None):
    tm = min(tm, n_rows)
    assert n_rows % tm == 0
    in_specs = [mk(tm) for _, mk in ins]
    out_specs, out_shape, is_acc = [], [], []
    for o in outs:
        if o[0] == "rows":
            out_specs.append(pl.BlockSpec((tm, o[1]), lambda i: (i, 0)))
            out_shape.append(jax.ShapeDtypeStruct((n_rows, o[1]), o[2]))
        elif o[0] == "cols":
            out_specs.append(pl.BlockSpec((tm, o[1]), lambda i, c=o[2]: (i, c)))
            out_shape.append(jax.ShapeDtypeStruct((n_rows, o[3]), o[4]))
        elif o[0] == "heads":
            out_specs.append(pl.BlockSpec((o[1], tm, o[2]), lambda i: (0, i, 0)))
            out_shape.append(jax.ShapeDtypeStruct((o[1], n_rows, o[2]), o[3]))
        else:
            out_specs.append(pl.BlockSpec((o[1], o[2]), lambda i: (0, 0)))
            out_shape.append(jax.ShapeDtypeStruct((o[1], o[2]), F32))
        is_acc.append(o[0] == "acc")
    n_in = len(ins)
    args = [a for a, _ in ins]
    aliases = {}
    if into is not None:
        in_specs.append(ANY)
        args.append(into[0])
        aliases = {n_in: into[1]}

    def load(r):
        v = r[...]
        return v.astype(F32) if upcast and v.dtype == BF16 else v

    def body(*refs):
        vals = fn(*[load(r) for r in refs[:n_in]])
        for r, v, acc in zip(refs[len(args):], vals, is_acc):
            if acc:
                @pl.when(pl.program_id(0) == 0)
                def _():
                    r[...] = jnp.zeros_like(r)

                r[...] += v
            else:
                r[...] = v.astype(r.dtype)

    return _pcall(body, name=name, grid=(n_rows // tm,), in_specs=in_specs, out_specs=out_specs,
                  out_shape=out_shape, dims=("arbitrary",), aliases=aliases)(*args)


def _rms(x, g):
    r = lax.rsqrt(jnp.mean(x * x, axis=-1, keepdims=True) + EPS)
    return x * r * g


def _rms_bwd(dy, x, g):
    r = lax.rsqrt(jnp.mean(x * x, axis=-1, keepdims=True) + EPS)
    xhat = x * r
    dxhat = dy * g
    dx = r * (dxhat - xhat * jnp.mean(dxhat * xhat, axis=-1, keepdims=True))
    return dx, jnp.sum(dy * xhat, axis=0, keepdims=True)


def _rope(x1, x2, cos, sin):
    return x1 * cos - x2 * sin, x2 * cos + x1 * sin


def _rope_bwd(d1, d2, cos, sin):
    return d1 * cos + d2 * sin, d2 * cos - d1 * sin


def _sigmoid(x):
    return 1.0 / (1.0 + jnp.exp(-x))


MLA_SCALE = 1.0 / math.sqrt(QK_HEAD)
MLA_PRESCALE = MLA_SCALE * math.log2(math.e)
MLA_TQ, MLA_KC = 1024, 1024


def _mla_fwd(q_full, k_full, kv, S, comm=None):
    tq, kc = min(MLA_TQ, S), min(MLA_KC, S)

    def body(q_ref, k_ref, v_ref, o_ref, lse_ref):
        q = q_ref[0]
        m = jnp.full((tq, 1), -1e30, F32)
        l = jnp.zeros((tq, 1), F32)
        acc = jnp.zeros((tq, V_DIM), F32)
        for c in range(S // kc):
            s = _dot(q, k_ref[0, c * kc:(c + 1) * kc, :], NT)
            m_new = jnp.maximum(m, jnp.max(s, axis=-1, keepdims=True))
            alpha = jnp.exp2(m - m_new)
            p = jnp.exp2(s - m_new)
            l = alpha * l + jnp.sum(p, axis=-1, keepdims=True)
            acc = alpha * acc + _dot(p.astype(BF16), v_ref[c * kc:(c + 1) * kc, :], NN)
            m = m_new
        o_ref[...] = (acc / l).astype(BF16)
        lse_ref[0] = m + jnp.log2(l)

    return _pcall(
        body, name="mla_fwd", grid=(H_A, S // tq),
        in_specs=[pl.BlockSpec((1, tq, QK_HEAD), lambda h, i: (h, i, 0)),
                  pl.BlockSpec((1, S, QK_HEAD), lambda h, i: (h, 0, 0)),
                  pl.BlockSpec((S, V_DIM), lambda h, i: (0, 2 * h + 1))],
        out_specs=[pl.BlockSpec((tq, V_DIM), lambda h, i: (i, h)),
                   pl.BlockSpec((1, tq, 1), lambda h, i: (h, i, 0))],
        out_shape=[jax.ShapeDtypeStruct((S, H_A * V_DIM), BF16), jax.ShapeDtypeStruct((H_A, S, 1), F32)],
        dims=("parallel", "parallel"), comm=comm)(q_full, k_full, kv)


def _mla_bwd(q_full, k_full, kv, do_a, o_a, lse, S, comm=None):
    tq, kc = min(MLA_TQ, S), min(MLA_KC, S)

    def body(q_ref, k_ref, v_ref, do_ref, o_ref, lse_ref, dq_ref, dk_ref, dv_ref):
        @pl.when(pl.program_id(1) == 0)
        def _():
            dk_ref[...] = jnp.zeros_like(dk_ref)
            dv_ref[...] = jnp.zeros_like(dv_ref)

        q = q_ref[0]
        do = do_ref[...]
        lse_q = lse_ref[0]
        delta = jnp.sum(do.astype(F32) * o_ref[...].astype(F32), axis=-1, keepdims=True)
        dq = jnp.zeros((tq, QK_HEAD), F32)
        for c in range(S // kc):
            k = k_ref[0, c * kc:(c + 1) * kc, :]
            v = v_ref[c * kc:(c + 1) * kc, :]
            p = jnp.exp2(_dot(q, k, NT) - lse_q)
            ds = (p * (_dot(do, v, NT) - delta)).astype(BF16)
            dq = dq + _dot(ds, k, NN)
            dk_ref[0, c * kc:(c + 1) * kc, :] += _dot(ds, q, TN)
            dv_ref[0, c * kc:(c + 1) * kc, :] += _dot(p.astype(BF16), do, TN)
        dq_ref[0] = dq * MLA_SCALE

    return _pcall(
        body, name="mla_bwd", grid=(H_A, S // tq),
        in_specs=[pl.BlockSpec((1, tq, QK_HEAD), lambda h, i: (h, i, 0)),
                  pl.BlockSpec((1, S, QK_HEAD), lambda h, i: (h, 0, 0)),
                  pl.BlockSpec((S, V_DIM), lambda h, i: (0, 2 * h + 1)),
                  pl.BlockSpec((tq, V_DIM), lambda h, i: (i, h)),
                  pl.BlockSpec((tq, V_DIM), lambda h, i: (i, h)),
                  pl.BlockSpec((1, tq, 1), lambda h, i: (h, i, 0))],
        out_specs=[pl.BlockSpec((1, tq, QK_HEAD), lambda h, i: (h, i, 0)),
                   pl.BlockSpec((1, S, QK_HEAD), lambda h, i: (h, 0, 0)),
                   pl.BlockSpec((1, S, V_DIM), lambda h, i: (h, 0, 0))],
        out_shape=[jax.ShapeDtypeStruct((H_A, S, QK_HEAD), F32), jax.ShapeDtypeStruct((H_A, S, QK_HEAD), F32),
                   jax.ShapeDtypeStruct((H_A, S, V_DIM), F32)],
        dims=("parallel", "arbitrary"), comm=comm)(q_full, k_full, kv, do_a, o_a, lse)


WIN_SCALE = 1.0 / math.sqrt(HD_B)


def _win_specs(S):
    qspec = pl.BlockSpec((Q_BLOCK, H_B * HD_B), lambda n: (n, PROJ_QB))
    kspecs = [pl.BlockSpec((Q_BLOCK, KV_B * HD_B), lambda n, d=d: (n + d, 0)) for d in range(3)]
    bias_spec = pl.BlockSpec((H_B, SPAN, Q_BLOCK), lambda n: (0, 0, 0))
    sink_spec = pl.BlockSpec((H_B, Q_BLOCK), lambda n: (0, 0))
    return qspec, kspecs, bias_spec, sink_spec


def _win_edge_ok(n, n_blk):
    row = lax.broadcasted_iota(jnp.int32, (SPAN, 1), 0)
    return jnp.logical_not(((n == 0) & (row < WINDOW)) | ((n == n_blk - 1) & (row >= SPAN - WINDOW)))


def _lanes4(pieces):
    return jnp.concatenate(pieces, axis=1)


def _win_probs(kg, q4t, bias_ref, sink_ref, g, edge_ok):
    bias4 = _lanes4([bias_ref[GROUP * g + j] for j in range(GROUP)])
    sink4 = _lanes4([sink_ref[GROUP * g + j:GROUP * g + j + 1, :] for j in range(GROUP)])
    s = jnp.where(edge_ok, _dot(kg, q4t, NN) + bias4, -1e30)
    m = jnp.maximum(jnp.max(s, axis=0, keepdims=True), sink4)
    p = jnp.exp(s - m)
    e_sink = jnp.exp(sink4 - m)
    inv_l = 1.0 / (jnp.sum(p, axis=0, keepdims=True) + e_sink)
    return p * inv_l, e_sink * inv_l


def _group_t(xt, g):
    return _lanes4([xt[HD_B * (GROUP * g + j):HD_B * (GROUP * g + j + 1), :] for j in range(GROUP)])


def _win_fwd(proj, kp, vp, bias_t, sinks_b, S):
    n_blk = S // Q_BLOCK
    qspec, kspecs, bias_spec, sink_spec = _win_specs(S)

    def body(q_ref, k0, k1, k2, v0, v1, v2, bias_ref, sink_ref, o_ref):
        n = pl.program_id(0)
        edge_ok = _win_edge_ok(n, n_blk)
        k = jnp.concatenate([k0[...], k1[...], k2[...]], axis=0)
        vt = jnp.concatenate([v0[...], v1[...], v2[...]], axis=0).T
        qt = (q_ref[...].astype(F32) * WIN_SCALE).T.astype(BF16)
        parts = []
        for g in range(KV_B):
            p, _ = _win_probs(k[:, HD_B * g:HD_B * (g + 1)], _group_t(qt, g), bias_ref, sink_ref, g, edge_ok)
            o4t = _dot(vt[HD_B * g:HD_B * (g + 1), :], p.astype(BF16), NN)
            parts += [o4t[:, Q_BLOCK * j:Q_BLOCK * (j + 1)] for j in range(GROUP)]
        o_ref[...] = jnp.concatenate(parts, axis=0).T.astype(BF16)

    return _pcall(body, name="win_fwd", grid=(n_blk,),
                  in_specs=[qspec, *kspecs, *kspecs, bias_spec, sink_spec],
                  out_specs=pl.BlockSpec((Q_BLOCK, H_B * HD_B), lambda n: (n, 0)),
                  out_shape=jax.ShapeDtypeStruct((S, H_B * HD_B), BF16),
                  dims=("parallel",))(proj, kp, kp, kp, vp, vp, vp, bias_t, sinks_b)


def _win_bwd(proj, kp, vp, bias_t, sinks_b, do_b, d_proj, S):
    n_blk = S // Q_BLOCK
    qspec, kspecs, bias_spec, sink_spec = _win_specs(S)

    def body(q_ref, k0, k1, k2, v0, v1, v2, bias_ref, sink_ref, do_ref, _, dq_ref, dk_ref, dv_ref, dbias_ref, dsink_ref, dsink_acc):
        n = pl.program_id(0)

        @pl.when(n == 0)
        def _():
            dk_ref[...] = jnp.zeros_like(dk_ref)
            dv_ref[...] = jnp.zeros_like(dv_ref)
            dbias_ref[...] = jnp.zeros_like(dbias_ref)
            dsink_acc[...] = jnp.zeros_like(dsink_acc)

        edge_ok = _win_edge_ok(n, n_blk)
        k = jnp.concatenate([k0[...], k1[...], k2[...]], axis=0)
        v = jnp.concatenate([v0[...], v1[...], v2[...]], axis=0)
        kt = k.T
        qt = (q_ref[...].astype(F32) * WIN_SCALE).T.astype(BF16)
        dot_ = do_ref[...].astype(F32).T.astype(BF16)
        dq_parts, dks, dvs = [], [], []
        for g in range(KV_B):
            kg, vg = k[:, HD_B * g:HD_B * (g + 1)], v[:, HD_B * g:HD_B * (g + 1)]
            q4t, do4t = _group_t(qt, g), _group_t(dot_, g)
            p, p_sink = _win_probs(kg, q4t, bias_ref, sink_ref, g, edge_ok)
            dp = _dot(vg, do4t, NN)
            delta = jnp.sum(p * dp, axis=0, keepdims=True)
            ds = p * (dp - delta)
            for j in range(GROUP):
                dbias_ref[GROUP * g + j] += ds[:, Q_BLOCK * j:Q_BLOCK * (j + 1)]
            dsink_acc[g:g + 1, :] += -p_sink * delta
            dsb = ds.astype(BF16)
            dq4t = _dot(kt[HD_B * g:HD_B * (g + 1), :], dsb, NN) * WIN_SCALE
            dq_parts += [dq4t[:, Q_BLOCK * j:Q_BLOCK * (j + 1)] for j in range(GROUP)]
            dks.append(_dot(dsb, q4t, NT))
            dvs.append(_dot(p.astype(BF16), do4t, NT))
        dq_ref[...] = jnp.concatenate(dq_parts, axis=0).T.astype(BF16)
        rows = pl.ds(pl.multiple_of(n * Q_BLOCK, Q_BLOCK), SPAN)
        dk_ref[rows, :] += jnp.concatenate(dks, axis=1)
        dv_ref[rows, :] += jnp.concatenate(dvs, axis=1)

        @pl.when(n == n_blk - 1)
        def _():
            acc = dsink_acc[...]
            dsink_ref[...] = jnp.concatenate(
                [jnp.sum(acc[:, Q_BLOCK * j:Q_BLOCK * (j + 1)], axis=1, keepdims=True) for j in range(GROUP)], axis=1)

    whole = lambda shape: pl.BlockSpec(shape, lambda n: (0,) * len(shape))
    return _pcall(
        body, name="win_bwd", grid=(n_blk,),
        in_specs=[qspec, *kspecs, *kspecs, bias_spec, sink_spec, pl.BlockSpec((Q_BLOCK, H_B * HD_B), lambda n: (n, 0)), ANY],
        out_specs=[qspec, whole((S + 2 * WINDOW, KV_B * HD_B)),
                   whole((S + 2 * WINDOW, KV_B * HD_B)), whole((H_B, SPAN, Q_BLOCK)), whole((KV_B, GROUP))],
        out_shape=[jax.ShapeDtypeStruct((S, PROJ_P), BF16), jax.ShapeDtypeStruct((S + 2 * WINDOW, KV_B * HD_B), F32),
                   jax.ShapeDtypeStruct((S + 2 * WINDOW, KV_B * HD_B), F32), jax.ShapeDtypeStruct((H_B, SPAN, Q_BLOCK), F32),
                   jax.ShapeDtypeStruct((KV_B, GROUP), F32)],
        scratch_shapes=[pltpu.VMEM((KV_B, GROUP * Q_BLOCK), F32)],
        dims=("arbitrary",), aliases={10: 0})(proj, kp, kp, kp, vp, vp, vp, bias_t, sinks_b, do_b, d_proj)


def _bias_table(rel_bias_t, onehot_t, in_band):
    def body(rb_ref, oh_ref, band_ref, o_ref):
        t = lax.dot_general(rb_ref[...], oh_ref[...], NN, preferred_element_type=F32, precision=lax.Precision.HIGHEST)
        o_ref[...] = jnp.where(band_ref[...] > 0.5, t, -1e30)

    n = onehot_t.shape[1]
    tn = _tile(n, 8192)
    return _pcall(body, name="bias_table", grid=(n // tn,),
                  in_specs=[pl.BlockSpec((H_B, NUM_BUCKETS), lambda j: (0, 0)), pl.BlockSpec((NUM_BUCKETS, tn), lambda j: (0, j)),
                            pl.BlockSpec((1, tn), lambda j: (0, j))],
                  out_specs=pl.BlockSpec((H_B, tn), lambda j: (0, j)),
                  out_shape=jax.ShapeDtypeStruct((H_B, n), F32), dims=("parallel",))(rel_bias_t, onehot_t, in_band)


def _bias_table_bwd(dbias, onehot_t):
    n = onehot_t.shape[1]
    tk = _tile(n, 8192)

    def body(d_ref, oh_ref, o_ref):
        @pl.when(pl.program_id(0) == 0)
        def _():
            o_ref[...] = jnp.zeros_like(o_ref)

        o_ref[...] += lax.dot_general(d_ref[...], oh_ref[...], NT, preferred_element_type=F32, precision=lax.Precision.HIGHEST)

    return _pcall(body, name="bias_table_bwd", grid=(n // tk,),
                  in_specs=[pl.BlockSpec((H_B, tk), lambda j: (0, j)), pl.BlockSpec((NUM_BUCKETS, tk), lambda j: (0, j))],
                  out_specs=pl.BlockSpec((H_B, NUM_BUCKETS), lambda j: (0, 0)),
                  out_shape=jax.ShapeDtypeStruct((H_B, NUM_BUCKETS), F32), dims=("arbitrary",))(dbias, onehot_t)


def _shift_down(u):
    row = lax.broadcasted_iota(jnp.int32, u.shape, 0)
    return jnp.where(row == 0, 0.0, pltpu.roll(u, 1, axis=0))


def _shift_up(u):
    n = u.shape[0]
    row = lax.broadcasted_iota(jnp.int32, u.shape, 0)
    return jnp.where(row == n - 1, 0.0, pltpu.roll(u, n - 1, axis=0))


def _conv(u, w, b):
    return _shift_down(u) * w[0:1, :] + u * w[1:2, :] + _shift_up(u) * w[2:3, :] + b


CONV_STRIP = 128
N_STRIPS = D_FF // CONV_STRIP


def _strip(rows, half):
    return pl.BlockSpec((rows, CONV_STRIP), lambda j: (0, j + half * N_STRIPS))


def _conv_gate_fwd(u, conv_w, conv_b, S):
    def body(ug_ref, uv_ref, wg_ref, wv_ref, bg_ref, bv_ref, a_ref):
        g = _conv(ug_ref[...].astype(F32), wg_ref[...], bg_ref[...])
        val = _conv(uv_ref[...].astype(F32), wv_ref[...], bv_ref[...])
        a_ref[...] = (g * _sigmoid(g) * val).astype(BF16)

    return _pcall(body, name="conv_gate_fwd", grid=(N_STRIPS,),
                  in_specs=[_strip(S, 0), _strip(S, 1), _strip(3, 0), _strip(3, 1), _strip(1, 0), _strip(1, 1)],
                  out_specs=_strip(S, 0), out_shape=jax.ShapeDtypeStruct((S, D_FF), BF16),
                  dims=("parallel",))(u, u, conv_w, conv_w, conv_b, conv_b)


def _conv_gate_bwd(u, conv_w, conv_b, da, S):
    def conv_bwd(duc, u_in, w):
        du = _shift_up(duc) * w[0:1, :] + duc * w[1:2, :] + _shift_down(duc) * w[2:3, :]
        dw = jnp.concatenate([jnp.sum(duc * _shift_down(u_in), axis=0, keepdims=True),
                              jnp.sum(duc * u_in, axis=0, keepdims=True),
                              jnp.sum(duc * _shift_up(u_in), axis=0, keepdims=True)], axis=0)
        return du, dw, jnp.sum(duc, axis=0, keepdims=True)

    def body(ug_ref, uv_ref, wg_ref, wv_ref, bg_ref, bv_ref, da_ref, dug_ref, duv_ref, dwg_ref, dwv_ref, dbg_ref, dbv_ref):
        u_g, u_v, w_g, w_v = ug_ref[...].astype(F32), uv_ref[...].astype(F32), wg_ref[...], wv_ref[...]
        g = _conv(u_g, w_g, bg_ref[...])
        val = _conv(u_v, w_v, bv_ref[...])
        da = da_ref[...].astype(F32)
        sg = _sigmoid(g)
        dval = da * (g * sg)
        dg = da * val * (sg * (1.0 + g * (1.0 - sg)))
        du_g, dwg_ref[...], dbg_ref[...] = conv_bwd(dg, u_g, w_g)
        du_v, dwv_ref[...], dbv_ref[...] = conv_bwd(dval, u_v, w_v)
        dug_ref[...] = du_g.astype(BF16)
        duv_ref[...] = du_v.astype(BF16)

    half = lambda r, dt: (_strip(r, 0), jax.ShapeDtypeStruct((r, D_FF), dt))
    outs = [half(S, BF16), half(S, BF16), half(3, F32), half(3, F32), half(1, F32), half(1, F32)]
    return _pcall(
        body, name="conv_gate_bwd", grid=(N_STRIPS,),
        in_specs=[_strip(S, 0), _strip(S, 1), _strip(3, 0), _strip(3, 1), _strip(1, 0), _strip(1, 1), _strip(S, 0)],
        out_specs=[o[0] for o in outs], out_shape=[o[1] for o in outs],
        dims=("parallel",))(u, u, conv_w, conv_w, conv_b, conv_b, da)


MESH = pl.DeviceIdType.MESH
ANY = pl.BlockSpec(memory_space=pl.ANY)


def _place():
    return lax.axis_index("x"), lax.axis_index("y"), lax.axis_index("c")


def _all_gather(shards):
    n_arr = len(shards)

    def body(*refs):
        ins, outs = refs[:n_arr], refs[n_arr:2 * n_arr]
        send_sems, recv_sems, local_sems = refs[2 * n_arr:]
        x, y, c = _place()
        me, sibling = (x, y, c), (x, y, 1 - c)
        chips = [(1 - x, y), (x, 1 - y), (1 - x, 1 - y)]

        def slot(a, p):
            return outs[a].at[4 * p[0] + 2 * p[1] + p[2]]

        def copy(a, k, block, to, src=None):
            return pltpu.make_async_remote_copy(
                src_ref=slot(a, block) if src is None else src, dst_ref=slot(a, block),
                send_sem=send_sems.at[a, k], recv_sem=recv_sems.at[a, k], device_id=to, device_id_type=MESH)

        mine = [pltpu.make_async_copy(ins[a], slot(a, me), local_sems.at[a]) for a in range(n_arr)]
        for cp in mine:
            cp.start()
        first = []
        for a in range(n_arr):
            first.append(copy(a, 0, me, sibling, src=ins[a]))
            first += [copy(a, 1 + j, me, (*chip, c), src=ins[a]) for j, chip in enumerate(chips)]
        for cp in first:
            cp.start()
        passed = []
        for j, chip in enumerate(chips):
            for a in range(n_arr):
                copy(a, 1 + j, (*chip, c), me).wait_recv()
                cp = copy(a, 4 + j, (*chip, c), sibling)
                cp.start()
                passed.append(cp)
        for a in range(n_arr):
            copy(a, 0, sibling, me).wait_recv()
            for j, chip in enumerate(chips):
                copy(a, 4 + j, (*chip, 1 - c), me).wait_recv()
        for cp in first + passed:
            cp.wait_send()
        for cp in mine:
            cp.wait()

    return pl.pallas_call(
        body, name="all_gather_weights",
        in_specs=[ANY] * n_arr, out_specs=[ANY] * n_arr,
        out_shape=[jax.ShapeDtypeStruct((N_DEV, *s.shape), s.dtype) for s in shards],
        scratch_shapes=[pltpu.SemaphoreType.DMA((n_arr, 7)), pltpu.SemaphoreType.DMA((n_arr, 7)), pltpu.SemaphoreType.DMA((n_arr,))],
    )(*shards)


def _xchg_out_shapes(stacked, replicated):
    return ([jax.ShapeDtypeStruct(s.shape, s.dtype) for s in stacked]
            + [jax.ShapeDtypeStruct((N_DEV, *r.shape), r.dtype) for r in replicated])


def _xchg_sems(n_arr):
    return [pltpu.SemaphoreType.DMA((n_arr, 7)), pltpu.SemaphoreType.DMA((n_arr, 7)), pltpu.SemaphoreType.DMA((n_arr,))]


def _xchg_copies(ins, outs, sems, n_st, with_recv):
    send_sems, recv_sems, local_sems = sems
    n_arr = len(ins)
    x, y, c = _place()
    me = 4 * x + 2 * y + c

    def src(a, idx):
        return ins[a].at[idx] if a < n_st else ins[a]

    mine = [pltpu.make_async_copy(src(a, me), outs[a].at[me], local_sems.at[a]) for a in range(n_arr)]
    pairs = []
    for k in range(1, N_DEV):
        px, py, pc = x ^ (k >> 2), y ^ ((k >> 1) & 1), c ^ (k & 1)
        peer = 4 * px + 2 * py + pc
        for a in range(n_arr):
            sems_k = dict(send_sem=send_sems.at[a, k - 1], recv_sem=recv_sems.at[a, k - 1], device_id_type=MESH)
            send = pltpu.make_async_remote_copy(src_ref=src(a, peer), dst_ref=outs[a].at[me], device_id=(px, py, pc), **sems_k)
            recv = None
            if with_recv:
                recv = pltpu.make_async_remote_copy(src_ref=src(a, peer), dst_ref=outs[a].at[peer], device_id=(x, y, c), **sems_k)
            pairs.append((send, recv))
    return mine, pairs


def _xchg_start(ins, outs, sems, n_st):
    mine, pairs = _xchg_copies(ins, outs, sems, n_st, False)
    for cp in mine:
        cp.start()
    for send, _ in pairs:
        send.start()


def _xchg_finish(ins, outs, sems, n_st):
    mine, pairs = _xchg_copies(ins, outs, sems, n_st, True)
    for _, recv in pairs:
        recv.wait_recv()
    for send, _ in pairs:
        send.wait_send()
    for cp in mine:
        cp.wait()


def _exchange(stacked, replicated, name):
    _, landed = _pcall(lambda: None, name=name, grid=(), in_specs=[], out_specs=[], out_shape=[], comm=(stacked, replicated))()
    return landed


def _adamw(parts, w, m, v, name):
    R, C = w.shape
    tr = R if (R <= 512 or R % 256) else 256

    def body(p_ref, w_ref, m_ref, v_ref, g_ref, d_ref, nm_ref, nv_ref):
        g = p_ref[0].astype(F32)
        for s in range(1, N_DEV):
            g = g + p_ref[s].astype(F32)
        m2 = ADAM_B1 * m_ref[...] + (1.0 - ADAM_B1) * g
        v2 = ADAM_B2 * v_ref[...] + (1.0 - ADAM_B2) * (g * g)
        m_hat = m2 / (1.0 - ADAM_B1 ** ADAM_STEP)
        v_hat = v2 / (1.0 - ADAM_B2 ** ADAM_STEP)
        g_ref[...] = g
        d_ref[...] = -ADAM_LR * (m_hat / (jnp.sqrt(v_hat) + ADAM_EPS) + ADAM_WD * w_ref[...])
        nm_ref[...] = m2
        nv_ref[...] = v2

    blk = pl.BlockSpec((tr, C), lambda i: (i, 0))
    return _pcall(body, name=name, grid=(R // tr,),
                  in_specs=[pl.BlockSpec((N_DEV, tr, C), lambda i: (0, i, 0)), blk, blk, blk],
                  out_specs=[blk] * 4, out_shape=[jax.ShapeDtypeStruct((R, C), F32)] * 4,
                  dims=("parallel",))(parts, w, m, v)


def _t5_bucket(rel):
    nb = NUM_BUCKETS // 2
    max_exact = nb // 2
    base = (rel > 0).astype(jnp.int32) * nb
    n = jnp.abs(rel)
    nf = jnp.maximum(n, 1).astype(jnp.float32)
    large = max_exact + (jnp.log(nf / max_exact) / math.log(MAX_DISTANCE / max_exact) * (nb - max_exact)).astype(jnp.int32)
    large = jnp.minimum(large, nb - 1)
    return base + jnp.where(n < max_exact, n, large)


def _unstack_cols(g):
    return jnp.transpose(g, (1, 0, 2)).reshape(g.shape[1], N_DEV * g.shape[2])


def _stack_cols(w, n=N_DEV):
    R = w.shape[0]
    return jnp.transpose(w.reshape(R, n, w.shape[1] // n), (1, 0, 2))


def _stack_halves(g, v):
    return jnp.concatenate([_stack_cols(g, N_DEV // 2), _stack_cols(v, N_DEV // 2)], axis=0)


def kernel(x, positions, norm1_g, w_in, q_a_norm_g, w_q_b, kv_a_norm_g, w_kv_b, rel_bias, sinks, w_out, norm2_g, w_up, conv_w, conv_b, w_down, final_norm_g, loss_target, m_norm1_g, m_w_in, m_q_a_norm_g, m_w_q_b, m_kv_a_norm_g, m_w_kv_b, m_rel_bias, m_sinks, m_w_out, m_norm2_g, m_w_up, m_conv_w, m_conv_b, m_w_down, m_final_norm_g, v_norm1_g, v_w_in, v_q_a_norm_g, v_w_q_b, v_kv_a_norm_g, v_w_kv_b, v_rel_bias, v_sinks, v_w_out, v_norm2_g, v_w_up, v_conv_w, v_conv_b, v_w_down, v_final_norm_g):
    S = x.shape[1]
    x = x[0]
    target = loss_target[0]
    TM = 256

    g_in, g_qb, g_kvb = _all_gather([w_in[0].astype(BF16), w_q_b[0].astype(BF16), w_kv_b[0].astype(BF16)])
    late_weights = [w_out[0].astype(BF16), w_up[0].astype(BF16), w_down[0].astype(BF16), conv_w[0]]
    wi = _unstack_cols(g_in)
    c0, c1, c2, c3, c4, c5 = (sum(W_IN_SIZES[:i + 1]) for i in range(6))
    w_in_p = jnp.concatenate([wi[:, c4:c5], wi[:, c5:], wi[:, c1:c2], wi[:, :c0], wi[:, c2:c3], wi[:, c3:c4],
                              wi[:, c0:c0 + KV_LORA], wi[:, c0 + KV_LORA:c1], jnp.zeros((D_MODEL, 64), BF16)], axis=1)
    wq = _unstack_cols(g_qb).reshape(Q_LORA, H_A, QK_HEAD)
    w_qb_p = jnp.concatenate([wq[:, :, :QK_NOPE].reshape(Q_LORA, H_A * QK_NOPE), wq[:, :, QK_NOPE:].reshape(Q_LORA, H_A * QK_ROPE)], axis=1)
    w_kvb = _unstack_cols(g_kvb)

    half = QK_ROPE // 2
    inv_freq = ROPE_THETA ** (-jnp.arange(half, dtype=F32) / half)
    ang = positions.astype(F32)[:, None] * inv_freq[None, :]
    cos, sin = jnp.cos(ang), jnp.sin(ang)
    qa = jnp.arange(Q_BLOCK, dtype=jnp.int32)[:, None]
    kc = jnp.arange(SPAN, dtype=jnp.int32)[None, :]
    rel = (kc - WINDOW - qa).T
    in_band = (jnp.abs(rel) <= WINDOW).astype(F32).reshape(1, Q_BLOCK * SPAN)
    onehot_t = (_t5_bucket(rel).reshape(1, Q_BLOCK * SPAN) == jnp.arange(NUM_BUCKETS, dtype=jnp.int32)[:, None]).astype(F32)
    bias_t = _bias_table(rel_bias.T, onehot_t, in_band).reshape(H_B, SPAN, Q_BLOCK)
    sinks_b = jnp.broadcast_to(sinks.reshape(H_B, 1), (H_B, Q_BLOCK))

    (h1,) = _rowwise(lambda a, g: (_rms(a, g),), "norm1", S, TM, [_rows(x), _whole(norm1_g)], [("rows", D_MODEL, BF16)])
    proj = _matmul(h1, w_in_p, "nn", BF16, "proj")

    def lat_fn(qlat, ckv, kr, gq, gkv, cs, sn):
        r1, r2 = _rope(kr[:, :half], kr[:, half:QK_ROPE], cs, sn)
        return _rms(qlat, gq), _rms(ckv, gkv), jnp.concatenate([r1, r2], axis=1)

    qn, ckvn, k_rope = _rowwise(lat_fn, "latents", S, TM,
                                [_rows(proj, 256, PROJ_QLAT), _rows(proj, 128, PROJ_CKV), _rows(proj, 128, PROJ_KROPE),
                                 _whole(q_a_norm_g), _whole(kv_a_norm_g), _rows(cos), _rows(sin)],
                                [("rows", Q_LORA, BF16), ("rows", KV_LORA, BF16), ("rows", QK_ROPE, BF16)])
    q_p = _matmul(qn, w_qb_p, "nn", F32, "q_up", tn=1536)
    kv = _matmul(ckvn, w_kvb, "nn", BF16, "kv_up", tn=2048)

    def q_heads_fn(q, cs, sn):
        q = q * MLA_PRESCALE
        outs = []
        for h in range(H_A):
            o = H_A * QK_NOPE + QK_ROPE * h
            r1, r2 = _rope(q[:, o:o + half], q[:, o + half:o + QK_ROPE], cs, sn)
            outs.append(jnp.concatenate([q[:, QK_NOPE * h:QK_NOPE * (h + 1)], r1, r2], axis=1)[None])
        return (jnp.concatenate(outs, axis=0),)

    (q_full,) = _rowwise(q_heads_fn, "q_heads", S, TM, [_rows(q_p), _rows(cos), _rows(sin)], [("heads", H_A, QK_HEAD, BF16)])

    def k_heads_fn(kvb, kr):
        return (jnp.concatenate([jnp.concatenate([kvb[:, 256 * h:256 * h + QK_NOPE], kr], axis=1)[None] for h in range(H_A)], axis=0),)

    (k_full,) = _rowwise(k_heads_fn, "k_heads", S, TM, [_rows(kv), _rows(k_rope)], [("heads", H_A, QK_HEAD, BF16)], upcast=False)
    (o_a, lse), (g_out, g_up, g_down, g_cw) = _mla_fwd(q_full, k_full, kv, S, comm=([], late_weights))
    w_out_f = g_out.reshape(D_MODEL, D_MODEL)
    w_up_f = _unstack_cols(g_up)
    w_down_f = g_down.reshape(D_FF, D_MODEL)
    conv_w_f = _unstack_cols(g_cw)

    kp = jnp.pad(proj[:, 256 * PROJ_KB:256 * (PROJ_KB + 1)], ((WINDOW, WINDOW), (0, 0)))
    vp = jnp.pad(proj[:, 256 * PROJ_VB:256 * (PROJ_VB + 1)], ((WINDOW, WINDOW), (0, 0)))
    o_b = _win_fwd(proj, kp, vp, bias_t, sinks_b, S)

    (mixed,) = _rowwise(lambda ga, gb, oa, ob: (_sigmoid(ga) * oa + _sigmoid(gb) * ob,), "gate_mix", S, TM,
                        [_rows(proj, 1024, PROJ_GA), _rows(proj, 1024, PROJ_GB), _rows(o_a), _rows(o_b)], [("rows", D_MODEL, BF16)])
    x1 = _matmul(mixed, w_out_f, "nn", F32, "out_proj", residual=x)
    (h2,) = _rowwise(lambda a, g: (_rms(a, g),), "norm2", S, TM, [_rows(x1), _whole(norm2_g)], [("rows", D_MODEL, BF16)])
    u = _matmul(h2, w_up_f, "nn", BF16, "ffn_up", tn=1408)
    act = _conv_gate_fwd(u, conv_w_f, conv_b, S)
    x2 = _matmul(act, w_down_f, "nn", F32, "ffn_down", residual=x1, tm=512)

    def final_fn(a, g, t):
        err = _rms(a, g) - t
        loss = 0.5 * jnp.sum(jnp.mean(err * err, axis=-1, keepdims=True), axis=0, keepdims=True)
        dx, dg = _rms_bwd(err * (1.0 / D_MODEL), a, g)
        return dx, dx, dg, jnp.broadcast_to(loss, (1, 128))

    gfin = final_norm_g.reshape(1, D_MODEL)
    dx2, dx2_b, d_gfin, loss_row = _rowwise(final_fn, "loss_head", S, TM, [_rows(x2), _whole(gfin), _rows(target)],
                                            [("rows", D_MODEL, F32), ("rows", D_MODEL, BF16), ("acc", 1, D_MODEL), ("acc", 1, 128)])
    d_act = _matmul(dx2_b, w_down_f, "nt", BF16, "ffn_down_dx", tn=1408)
    d_w_down = _matmul(act, dx2_b, "tn", F32, "ffn_down_dw")
    du_g, du_v, dcw_g, dcw_v, dcb_g, dcb_v = _conv_gate_bwd(u, conv_w_f, conv_b, d_act, S)
    d_conv_b = jnp.concatenate([dcb_g, dcb_v], axis=1)
    d_h2 = _matmul(du_g, w_up_f, "nt", F32, "ffn_up_dx", tm=512, a2=du_v)
    d_w_up_g = _matmul(h2, du_g, "tn", F32, "ffn_up_dw_gate", tm=512, tn=1408)
    d_w_up_v = _matmul(h2, du_v, "tn", F32, "ffn_up_dw_value", tm=512, tn=1408)

    def norm_bwd_fn(dh, a, g, dres):
        dx, dg = _rms_bwd(dh, a, g)
        dx = dx + dres
        return dx, dx, dg

    dx1, dx1_b, d_g2 = _rowwise(norm_bwd_fn, "norm2_bwd", S, TM, [_rows(d_h2), _rows(x1), _whole(norm2_g), _rows(dx2)],
                                [("rows", D_MODEL, F32), ("rows", D_MODEL, BF16), ("acc", 1, D_MODEL)])
    d_mixed = _matmul(dx1_b, w_out_f, "nt", BF16, "out_proj_dx")
    d_w_out = _matmul(mixed, dx1_b, "tn", F32, "out_proj_dw", tm=512)

    def gate_bwd_fn(dm, ga, gb, oa, ob):
        sa, sb = _sigmoid(ga), _sigmoid(gb)
        return jnp.concatenate([dm * oa * sa * (1.0 - sa), dm * ob * sb * (1.0 - sb)], axis=1), dm * sa, dm * sb

    d_proj, do_a, do_b = _rowwise(gate_bwd_fn, "gate_bwd", S, TM,
                                  [_rows(d_mixed), _rows(proj, 1024, PROJ_GA), _rows(proj, 1024, PROJ_GB), _rows(o_a), _rows(o_b)],
                                  [("cols", 2 * D_MODEL, 0, PROJ_P, BF16), ("rows", D_MODEL, BF16), ("rows", D_MODEL, BF16)])

    d_proj, dk_acc, dv_acc, d_bias, d_sinks_g = _win_bwd(proj, kp, vp, bias_t, sinks_b, do_b, d_proj, S)
    d_rel_bias = _bias_table_bwd(d_bias.reshape(H_B, Q_BLOCK * SPAN), onehot_t).T
    d_sinks = d_sinks_g.reshape(1, H_B)

    early = [d_w_out.reshape(N_DEV, D_MODEL // N_DEV, D_MODEL).astype(BF16), _stack_halves(d_w_up_g, d_w_up_v).astype(BF16),
             d_w_down.reshape(N_DEV, D_FF // N_DEV, D_MODEL).astype(BF16), _stack_halves(dcw_g, dcw_v)]
    (dq_full, dk_full, dv_full), recv_early = _mla_bwd(q_full, k_full, kv, do_a, o_a, lse, S, comm=(early, []))

    def dq_post_fn(dq, cs, sn):
        nope = [dq[h, :, :QK_NOPE] for h in range(H_A)]
        rope = []
        for h in range(H_A):
            rope += list(_rope_bwd(dq[h, :, QK_NOPE:QK_NOPE + half], dq[h, :, QK_NOPE + half:], cs, sn))
        return (jnp.concatenate(nope + rope, axis=1),)

    (dq_p,) = _rowwise(dq_post_fn, "dq_post", S, TM, [_heads(dq_full), _rows(cos), _rows(sin)], [("rows", H_A * QK_HEAD, BF16)])

    def dkv_post_fn(dk, dv, cs, sn):
        dk = dk * math.log(2.0)
        dkv = jnp.concatenate([jnp.concatenate([dk[h, :, :QK_NOPE], dv[h]], axis=1) for h in range(H_A)], axis=1)
        dkr = dk[0, :, QK_NOPE:]
        for h in range(1, H_A):
            dkr = dkr + dk[h, :, QK_NOPE:]
        r1, r2 = _rope_bwd(dkr[:, :half], dkr[:, half:], cs, sn)
        return dkv, jnp.concatenate([r1, r2], axis=1)

    dkv, d_krope = _rowwise(dkv_post_fn, "dkv_post", S, TM, [_heads(dk_full), _heads(dv_full), _rows(cos), _rows(sin)],
                            [("rows", H_A * (QK_NOPE + V_DIM), BF16), ("rows", QK_ROPE, F32)])
    d_qn = _matmul(dq_p, w_qb_p, "nt", F32, "q_up_dx")
    d_w_qb_p = _matmul(qn, dq_p, "tn", F32, "q_up_dw", tn=1536)
    d_ckvn = _matmul(dkv, w_kvb, "nt", F32, "kv_up_dx")
    d_w_kvb = _matmul(ckvn, dkv, "tn", F32, "kv_up_dw", tn=2048)

    def lat_bwd_fn(dqn, dckvn, dkr, qlat, ckv, gq, gkv, dkb, dvb):
        dql, dgq = _rms_bwd(dqn, qlat, gq)
        dck, dgkv = _rms_bwd(dckvn, ckv, gkv)
        tail = jnp.concatenate([dql, dkb, dvb, dck, dkr, jnp.zeros_like(dkr)], axis=1)
        return tail, dgq, dgkv

    shifted = lambda arr: (arr, lambda tm: pl.BlockSpec((tm, arr.shape[1]), lambda i: (i + WINDOW // tm, 0)))
    TL = min(128, S)
    d_proj, d_gq, d_gkv = _rowwise(lat_bwd_fn, "latents_bwd", S, TL,
                                   [_rows(d_qn), _rows(d_ckvn), _rows(d_krope), _rows(proj, 256, PROJ_QLAT), _rows(proj, 128, PROJ_CKV),
                                    _whole(q_a_norm_g), _whole(kv_a_norm_g), shifted(dk_acc), shifted(dv_acc)],
                                   [("cols", 1024, 3, PROJ_P, BF16), ("acc", 1, Q_LORA), ("acc", 1, KV_LORA)], into=(d_proj, 0))
    d_w_in_p = _matmul(h1, d_proj, "tn", F32, "proj_dw", tm=512)

    dp = d_w_in_p
    d_w_in = jnp.concatenate([dp[:, 3072:3328], dp[:, 3840:3968], dp[:, 3968:4032], dp[:, 2048:3072], dp[:, 3328:3584],
                              dp[:, 3584:3840], dp[:, 0:1024], dp[:, 1024:2048]], axis=1)
    d_w_qb = jnp.concatenate([d_w_qb_p[:, :H_A * QK_NOPE].reshape(Q_LORA, H_A, QK_NOPE),
                              d_w_qb_p[:, H_A * QK_NOPE:].reshape(Q_LORA, H_A, QK_ROPE)], axis=2).reshape(Q_LORA, H_A * QK_HEAD)
    late = [_stack_cols(d_w_in).astype(BF16), _stack_cols(d_w_qb).astype(BF16), _stack_cols(d_w_kvb).astype(BF16)]
    d_h1, recv_late = _matmul(d_proj, w_in_p, "nt", F32, "proj_dx", tm=512, comm=(late, []))
    grad_x, _, d_g1 = _rowwise(norm_bwd_fn, "norm1_bwd", S, TM, [_rows(d_h1), _rows(x), _whole(norm1_g), _rows(dx1)],
                               [("rows", D_MODEL, F32), ("rows", D_MODEL, BF16), ("acc", 1, D_MODEL)])
    recv = [*recv_late, *recv_early]

    small_parts = [d_g1, d_gq, d_gkv, d_rel_bias.reshape(1, NUM_BUCKETS * H_B), d_sinks, d_g2, d_conv_b, d_gfin]
    small = jnp.concatenate(small_parts, axis=1)
    n_small = small.shape[1]
    pad = (-n_small) % 128
    small = jnp.pad(small, ((0, 0), (0, pad)))
    (recv_small,) = _exchange([], [small], "exchange_small_grads")

    def flat(a):
        return a.reshape(1, -1)

    small_w = [norm1_g, q_a_norm_g, kv_a_norm_g, rel_bias, sinks, norm2_g, conv_b, final_norm_g]
    small_m = [m_norm1_g, m_q_a_norm_g, m_kv_a_norm_g, m_rel_bias, m_sinks, m_norm2_g, m_conv_b, m_final_norm_g]
    small_v = [v_norm1_g, v_q_a_norm_g, v_kv_a_norm_g, v_rel_bias, v_sinks, v_norm2_g, v_conv_b, v_final_norm_g]
    cat = lambda parts: jnp.pad(jnp.concatenate([flat(a) for a in parts], axis=1), ((0, 0), (0, pad)))
    sm = _adamw(recv_small, cat(small_w), cat(small_m), jnp.pad(jnp.concatenate([flat(a) for a in small_v], axis=1), ((0, 0), (0, pad)), constant_values=1.0), "adamw_small")
    big_names = ["w_in", "w_q_b", "w_kv_b", "w_out", "w_up", "w_down", "conv_w"]
    big_w = [w_in, w_q_b, w_kv_b, w_out, w_up, w_down, conv_w]
    big_m = [m_w_in, m_w_q_b, m_w_kv_b, m_w_out, m_w_up, m_w_down, m_conv_w]
    big_v = [v_w_in, v_w_q_b, v_w_kv_b, v_w_out, v_w_up, v_w_down, v_conv_w]
    big = {n: _adamw(r, w[0], m[0], v[0], "adamw_" + n) for n, r, w, m, v in zip(big_names, recv, big_w, big_m, big_v)}

    loss = lax.psum(loss_row[0, 0], ("x", "y", "c"))
    order = ["norm1_g", "w_in", "q_a_norm_g", "w_q_b", "kv_a_norm_g", "w_kv_b", "rel_bias", "sinks", "w_out", "norm2_g", "w_up",
             "conv_w", "conv_b", "w_down", "final_norm_g"]
    small_names = ["norm1_g", "q_a_norm_g", "kv_a_norm_g", "rel_bias", "sinks", "norm2_g", "conv_b", "final_norm_g"]
    offs, o = {}, 0
    for n, a in zip(small_names, small_w):
        offs[n] = (o, a.size, a.shape)
        o += a.size
    outs = [loss, grad_x[None]]
    for kind in range(4):
        for n in order:
            if n in big:
                outs.append(big[n][kind][None])
            else:
                o, size, shape = offs[n]
                outs.append(sm[kind][0, o:o + size].reshape(shape))
    return tuple(outs)
```

```python
import math

import jax
import jax.numpy as jnp
from jax import lax
from jax.experimental import pallas as pl
from jax.experimental.pallas import tpu as pltpu

F32 = jnp.float32
BF16 = jnp.bfloat16

N_DEV = 8
D_MODEL = 1024
EPS = 1e-6
H_A, QK_NOPE, QK_ROPE, V_DIM, Q_LORA, KV_LORA = 8, 128, 64, 128, 256, 128
QK_HEAD = QK_NOPE + QK_ROPE
ROPE_THETA = 10000.0
H_B, KV_B, GROUP, HD_B, WINDOW, Q_BLOCK = 16, 4, 4, 64, 128, 128
SPAN = Q_BLOCK + 2 * WINDOW
NUM_BUCKETS, MAX_DISTANCE = 32, 128
D_FF = 2816
ADAM_LR, ADAM_B1, ADAM_B2, ADAM_EPS, ADAM_WD, ADAM_STEP = 0.001, 0.9, 0.999, 1e-08, 0.01, 10

W_IN_SIZES = (Q_LORA, KV_LORA + QK_ROPE, H_B * HD_B, KV_B * HD_B, KV_B * HD_B, D_MODEL, D_MODEL)
W_IN_COLS = sum(W_IN_SIZES)
PROJ_P = 4096
PROJ_GA, PROJ_GB, PROJ_QB, PROJ_QLAT, PROJ_KB, PROJ_VB, PROJ_CKV, PROJ_KROPE = 0, 1, 2, 12, 13, 14, 30, 31

VMEM_LIMIT = 56 * 1024 * 1024

NN = (((1,), (0,)), ((), ()))
NT = (((1,), (1,)), ((), ()))
TN = (((0,), (0,)), ((), ()))


def _pcall(body, *, name, grid, in_specs, out_specs, out_shape, scratch_shapes=(), dims=None, comm=None, aliases=None):
    if comm is None:
        params = pltpu.CompilerParams(dimension_semantics=dims, vmem_limit_bytes=VMEM_LIMIT)
        return pl.pallas_call(body, name=name, grid=grid, in_specs=in_specs, out_specs=out_specs, out_shape=out_shape,
                              scratch_shapes=list(scratch_shapes), input_output_aliases=aliases or {}, compiler_params=params)
    assert not aliases
    stacked, replicated = comm
    arrs = [*stacked, *replicated]
    n_st, n_arr = len(stacked), len(arrs)
    single = not isinstance(out_specs, (list, tuple))
    o_specs, o_shape = ([out_specs], [out_shape]) if single else (list(out_specs), list(out_shape))
    n_in, n_out = len(in_specs), len(o_specs)

    def wrapped(*refs):
        c_in = refs[n_in:n_in + n_arr]
        c_out = refs[n_in + n_arr + n_out:n_in + 2 * n_arr + n_out]
        sems = refs[len(refs) - 3:]
        own = (*refs[:n_in], *refs[n_in + n_arr:n_in + n_arr + n_out], *refs[n_in + 2 * n_arr + n_out:len(refs) - 3])
        if not grid:
            _xchg_start(c_in, c_out, sems, n_st)
            _xchg_finish(c_in, c_out, sems, n_st)
            return
        first = last = None
        for d, n in enumerate(grid):
            pid = pl.program_id(d)
            first = (pid == 0) if first is None else first & (pid == 0)
            last = (pid == n - 1) if last is None else last & (pid == n - 1)

        @pl.when(first)
        def _():
            _xchg_start(c_in, c_out, sems, n_st)

        body(*own)

        @pl.when(last)
        def _():
            _xchg_finish(c_in, c_out, sems, n_st)

    params = pltpu.CompilerParams(dimension_semantics=("arbitrary",) * len(grid), vmem_limit_bytes=VMEM_LIMIT)
    call = pl.pallas_call(wrapped, name=name, grid=grid, in_specs=[*in_specs, *[ANY] * n_arr], out_specs=[*o_specs, *[ANY] * n_arr],
                          out_shape=[*o_shape, *_xchg_out_shapes(stacked, replicated)],
                          scratch_shapes=[*scratch_shapes, *_xchg_sems(n_arr)], compiler_params=params)

    def run(*args):
        res = call(*args, *arrs)
        outs, landed = res[:n_out], res[n_out:]
        return (outs[0] if single else outs), landed

    return run


def _dot(a, b, dn):
    return lax.dot_general(a, b, dn, preferred_element_type=F32)


def _tile(n, target):
    best = None
    for t in range(128, min(n, target) + 1, 128):
        if n % t == 0:
            best = t
    return n if best is None else best


def _matmul(a, b, mode, out_dtype, name, residual=None, tm=1024, tn=1024, comm=None, a2=None, epi=None):
    if mode == "nn":
        (M, K), N = a.shape, b.shape[1]
    elif mode == "nt":
        (M, K), N = a.shape, b.shape[0]
    else:
        (K, M), N = a.shape, b.shape[1]
    tm, tn = _tile(M, tm), _tile(N, tn)
    a_spec = pl.BlockSpec((K, tm), lambda i, j: (0, i)) if mode == "tn" else pl.BlockSpec((tm, K), lambda i, j: (i, 0))
    b_spec = pl.BlockSpec((tn, b.shape[1]), lambda i, j: (j, 0)) if mode == "nt" else pl.BlockSpec((K, tn), lambda i, j: (0, j))
    o_spec = pl.BlockSpec((tm, tn), lambda i, j: (i, j))
    in_specs, args = [a_spec, b_spec], [a, b]
    if a2 is not None:
        assert mode == "nt" and K + a2.shape[1] == b.shape[1]
        in_specs.append(pl.BlockSpec((tm, a2.shape[1]), lambda i, j: (i, 0)))
        args.append(a2)
    if residual is not None:
        in_specs.append(o_spec)
        args.append(residual)
    n_mm = len(args)
    scratch = [pltpu.VMEM((tm, K), a.dtype)] if mode == "tn" else []
    if epi is None:
        out_specs, out_shape, is_acc = o_spec, jax.ShapeDtypeStruct((M, N), out_dtype), None
    else:
        assert tn == N
        fn, epi_ins, epi_outs = epi
        in_specs += [mk(tm) for _, mk in epi_ins]
        args += [arr for arr, _ in epi_ins]
        out_specs, out_shape, is_acc = _row_out_specs(epi_outs, M, tm)

    def body(*refs):
        a_ref, b_ref = refs[0], refs[1]
        n_out = 1 if epi is None else len(is_acc)
        out_refs = refs[len(args):len(args) + n_out]
        if mode == "tn":
            at_ref = refs[len(args) + n_out]

            @pl.when(pl.program_id(1) == 0)
            def _():
                at_ref[...] = a_ref[...].T

            acc = _dot(at_ref[...], b_ref[...], NN)
        elif a2 is not None:
            acc = _dot(a_ref[...], b_ref[:, :K], NT) + _dot(refs[2][...], b_ref[:, K:], NT)
        else:
            acc = _dot(a_ref[...], b_ref[...], NT if mode == "nt" else NN)
        if residual is not None:
            acc = acc + refs[n_mm - 1][...]
        if epi is None:
            out_refs[0][...] = acc.astype(out_dtype)
        else:
            _store_rows(out_refs, fn(acc, *[_load_f32(r) for r in refs[n_mm:len(args)]]), is_acc)

    return _pcall(body, name=name, grid=(M // tm, N // tn), in_specs=in_specs, out_specs=out_specs,
                  out_shape=out_shape, scratch_shapes=scratch,
                  dims=("arbitrary" if epi is not None else "parallel", "arbitrary"), comm=comm)(*args)


def _rows(arr, width=None, col=0):
    width = arr.shape[1] if width is None else width
    return (arr, lambda tm: pl.BlockSpec((tm, width), lambda i, *_: (i, col)))


def _heads(arr):
    return (arr, lambda tm: pl.BlockSpec((arr.shape[0], tm, arr.shape[2]), lambda i, *_: (0, i, 0)))


def _whole(arr):
    nd = arr.ndim
    return (arr, lambda tm: pl.BlockSpec(arr.shape, lambda i, *_: (0,) * nd))


def _row_out_specs(outs, n_rows, tm):
    out_specs, out_shape, is_acc = [], [], []
    for o in outs:
        if o[0] == "rows":
            out_specs.append(pl.BlockSpec((tm, o[1]), lambda i, *_: (i, 0)))
            out_shape.append(jax.ShapeDtypeStruct((n_rows, o[1]), o[2]))
        elif o[0] == "cols":
            out_specs.append(pl.BlockSpec((tm, o[1]), lambda i, *_, c=o[2]: (i, c)))
            out_shape.append(jax.ShapeDtypeStruct((n_rows, o[3]), o[4]))
        elif o[0] == "heads":
            out_specs.append(pl.BlockSpec((o[1], tm, o[2]), lambda i, *_: (0, i, 0)))
            out_shape.append(jax.ShapeDtypeStruct((o[1], n_rows, o[2]), o[3]))
        else:
            out_specs.append(pl.BlockSpec((o[1], o[2]), lambda i, *_: (0, 0)))
            out_shape.append(jax.ShapeDtypeStruct((o[1], o[2]), F32))
        is_acc.append(o[0] == "acc")
    return out_specs, out_shape, is_acc


def _load_f32(r):
    v = r[...]
    return v.astype(F32) if v.dtype == BF16 else v


def _store_rows(out_refs, vals, is_acc):
    for r, v, acc in zip(out_refs, vals, is_acc):
        if acc:
            @pl.when(pl.program_id(0) == 0)
            def _():
                r[...] = jnp.zeros_like(r)

            r[...] += v
        else:
            r[...] = v.astype(r.dtype)


def _rowwise(fn, name, n_rows, tm, ins, outs, upcast=True, into=None):
    tm = min(tm, n_rows)
    assert n_rows % tm == 0
    in_specs = [mk(tm) for _, mk in ins]
    out_specs, out_shape, is_acc = _row_out_specs(outs, n_rows, tm)
    n_in = len(ins)
    args = [a for a, _ in ins]
    aliases = {}
    if into is not None:
        in_specs.append(ANY)
        args.append(into[0])
        aliases = {n_in: into[1]}

    def body(*refs):
        vals = fn(*[_load_f32(r) if upcast else r[...] for r in refs[:n_in]])
        _store_rows(refs[len(args):], vals, is_acc)

    return _pcall(body, name=name, grid=(n_rows // tm,), in_specs=in_specs, out_specs=out_specs,
                  out_shape=out_shape, dims=("arbitrary",), aliases=aliases)(*args)


def _rms(x, g):
    r = lax.rsqrt(jnp.mean(x * x, axis=-1, keepdims=True) + EPS)
    return x * r * g


def _rms_bwd(dy, x, g):
    r = lax.rsqrt(jnp.mean(x * x, axis=-1, keepdims=True) + EPS)
    xhat = x * r
    dxhat = dy * g
    dx = r * (dxhat - xhat * jnp.mean(dxhat * xhat, axis=-1, keepdims=True))
    return dx, jnp.sum(dy * xhat, axis=0, keepdims=True)


def _rope(x1, x2, cos, sin):
    return x1 * cos - x2 * sin, x2 * cos + x1 * sin


def _rope_bwd(d1, d2, cos, sin):
    return d1 * cos + d2 * sin, d2 * cos - d1 * sin


def _sigmoid(x):
    return 1.0 / (1.0 + jnp.exp(-x))


MLA_SCALE = 1.0 / math.sqrt(QK_HEAD)
MLA_PRESCALE = MLA_SCALE * math.log2(math.e)
MLA_TQ, MLA_KC = 1024, 1024


def _mla_fwd(q_full, k_full, kv, S, comm=None):
    tq, kc = min(MLA_TQ, S), min(MLA_KC, S)

    def body(q_ref, k_ref, v_ref, o_ref, lse_ref):
        q = q_ref[0]
        m = jnp.full((tq, 1), -1e30, F32)
        l = jnp.zeros((tq, 1), F32)
        acc = jnp.zeros((tq, V_DIM), F32)
        for c in range(S // kc):
            s = _dot(q, k_ref[0, c * kc:(c + 1) * kc, :], NT)
            m_new = jnp.maximum(m, jnp.max(s, axis=-1, keepdims=True))
            alpha = jnp.exp2(m - m_new)
            p = jnp.exp2(s - m_new)
            l = alpha * l + jnp.sum(p, axis=-1, keepdims=True)
            acc = alpha * acc + _dot(p.astype(BF16), v_ref[c * kc:(c + 1) * kc, :], NN)
            m = m_new
        o_ref[...] = (acc / l).astype(BF16)
        lse_ref[0] = m + jnp.log2(l)

    return _pcall(
        body, name="mla_fwd", grid=(H_A, S // tq),
        in_specs=[pl.BlockSpec((1, tq, QK_HEAD), lambda h, i: (h, i, 0)),
                  pl.BlockSpec((1, S, QK_HEAD), lambda h, i: (h, 0, 0)),
                  pl.BlockSpec((S, V_DIM), lambda h, i: (0, 2 * h + 1))],
        out_specs=[pl.BlockSpec((tq, V_DIM), lambda h, i: (i, h)),
                   pl.BlockSpec((1, tq, 1), lambda h, i: (h, i, 0))],
        out_shape=[jax.ShapeDtypeStruct((S, H_A * V_DIM), BF16), jax.ShapeDtypeStruct((H_A, S, 1), F32)],
        dims=("parallel", "parallel"), comm=comm)(q_full, k_full, kv)


def _mla_bwd(q_full, k_full, kv, do_a, o_a, lse, S, comm=None):
    tq, kc = min(MLA_TQ, S), min(MLA_KC, S)

    def body(q_ref, k_ref, v_ref, do_ref, o_ref, lse_ref, dq_ref, dk_ref, dv_ref):
        @pl.when(pl.program_id(1) == 0)
        def _():
            dk_ref[...] = jnp.zeros_like(dk_ref)
            dv_ref[...] = jnp.zeros_like(dv_ref)

        q = q_ref[0]
        do = do_ref[...]
        lse_q = lse_ref[0]
        delta = jnp.sum(do.astype(F32) * o_ref[...].astype(F32), axis=-1, keepdims=True)
        dq = jnp.zeros((tq, QK_HEAD), F32)
        for c in range(S // kc):
            k = k_ref[0, c * kc:(c + 1) * kc, :]
            v = v_ref[c * kc:(c + 1) * kc, :]
            p = jnp.exp2(_dot(q, k, NT) - lse_q)
            ds = (p * (_dot(do, v, NT) - delta)).astype(BF16)
            dq = dq + _dot(ds, k, NN)
            dk_ref[0, c * kc:(c + 1) * kc, :] += _dot(ds, q, TN)
            dv_ref[0, c * kc:(c + 1) * kc, :] += _dot(p.astype(BF16), do, TN)
        dq_ref[0] = dq * MLA_SCALE

    return _pcall(
        body, name="mla_bwd", grid=(H_A, S // tq),
        in_specs=[pl.BlockSpec((1, tq, QK_HEAD), lambda h, i: (h, i, 0)),
                  pl.BlockSpec((1, S, QK_HEAD), lambda h, i: (h, 0, 0)),
                  pl.BlockSpec((S, V_DIM), lambda h, i: (0, 2 * h + 1)),
                  pl.BlockSpec((tq, V_DIM), lambda h, i: (i, h)),
                  pl.BlockSpec((tq, V_DIM), lambda h, i: (i, h)),
                  pl.BlockSpec((1, tq, 1), lambda h, i: (h, i, 0))],
        out_specs=[pl.BlockSpec((1, tq, QK_HEAD), lambda h, i: (h, i, 0)),
                   pl.BlockSpec((1, S, QK_HEAD), lambda h, i: (h, 0, 0)),
                   pl.BlockSpec((1, S, V_DIM), lambda h, i: (h, 0, 0))],
        out_shape=[jax.ShapeDtypeStruct((H_A, S, QK_HEAD), F32), jax.ShapeDtypeStruct((H_A, S, QK_HEAD), F32),
                   jax.ShapeDtypeStruct((H_A, S, V_DIM), F32)],
        dims=("parallel", "arbitrary"), comm=comm)(q_full, k_full, kv, do_a, o_a, lse)


WIN_SCALE = 1.0 / math.sqrt(HD_B)


def _win_specs(S):
    qspec = pl.BlockSpec((Q_BLOCK, H_B * HD_B), lambda n: (n, PROJ_QB))
    kspecs = [pl.BlockSpec((Q_BLOCK, KV_B * HD_B), lambda n, d=d: (n + d, 0)) for d in range(3)]
    bias_spec = pl.BlockSpec((H_B, SPAN, Q_BLOCK), lambda n: (0, 0, 0))
    sink_spec = pl.BlockSpec((H_B, Q_BLOCK), lambda n: (0, 0))
    return qspec, kspecs, bias_spec, sink_spec


def _win_edge_ok(n, n_blk):
    row = lax.broadcasted_iota(jnp.int32, (SPAN, 1), 0)
    return jnp.logical_not(((n == 0) & (row < WINDOW)) | ((n == n_blk - 1) & (row >= SPAN - WINDOW)))


def _lanes4(pieces):
    return jnp.concatenate(pieces, axis=1)


def _win_probs(kg, q4t, bias_ref, sink_ref, g, edge_ok):
    bias4 = _lanes4([bias_ref[GROUP * g + j] for j in range(GROUP)])
    sink4 = _lanes4([sink_ref[GROUP * g + j:GROUP * g + j + 1, :] for j in range(GROUP)])
    s = jnp.where(edge_ok, _dot(kg, q4t, NN) + bias4, -1e30)
    m = jnp.maximum(jnp.max(s, axis=0, keepdims=True), sink4)
    p = jnp.exp(s - m)
    e_sink = jnp.exp(sink4 - m)
    inv_l = 1.0 / (jnp.sum(p, axis=0, keepdims=True) + e_sink)
    return p * inv_l, e_sink * inv_l


def _group_t(xt, g):
    return _lanes4([xt[HD_B * (GROUP * g + j):HD_B * (GROUP * g + j + 1), :] for j in range(GROUP)])


def _win_fwd(proj, kp, vp, bias_t, sinks_b, S, comm=None):
    n_blk = S // Q_BLOCK
    qspec, kspecs, bias_spec, sink_spec = _win_specs(S)

    def body(q_ref, k0, k1, k2, v0, v1, v2, bias_ref, sink_ref, o_ref):
        n = pl.program_id(0)
        edge_ok = _win_edge_ok(n, n_blk)
        k = jnp.concatenate([k0[...], k1[...], k2[...]], axis=0)
        vt = jnp.concatenate([v0[...], v1[...], v2[...]], axis=0).T
        qt = (q_ref[...].astype(F32) * WIN_SCALE).T.astype(BF16)
        parts = []
        for g in range(KV_B):
            p, _ = _win_probs(k[:, HD_B * g:HD_B * (g + 1)], _group_t(qt, g), bias_ref, sink_ref, g, edge_ok)
            o4t = _dot(vt[HD_B * g:HD_B * (g + 1), :], p.astype(BF16), NN)
            parts += [o4t[:, Q_BLOCK * j:Q_BLOCK * (j + 1)] for j in range(GROUP)]
        o_ref[...] = jnp.concatenate(parts, axis=0).T.astype(BF16)

    return _pcall(body, name="win_fwd", grid=(n_blk,),
                  in_specs=[qspec, *kspecs, *kspecs, bias_spec, sink_spec],
                  out_specs=pl.BlockSpec((Q_BLOCK, H_B * HD_B), lambda n: (n, 0)),
                  out_shape=jax.ShapeDtypeStruct((S, H_B * HD_B), BF16),
                  dims=("parallel",), comm=comm)(proj, kp, kp, kp, vp, vp, vp, bias_t, sinks_b)


def _win_bwd(proj, kp, vp, bias_t, sinks_b, do_b, d_proj, S):
    n_blk = S // Q_BLOCK
    qspec, kspecs, bias_spec, sink_spec = _win_specs(S)

    def body(q_ref, k0, k1, k2, v0, v1, v2, bias_ref, sink_ref, do_ref, _, dq_ref, dk_ref, dv_ref, dbias_ref, dsink_ref, dsink_acc):
        n = pl.program_id(0)

        @pl.when(n == 0)
        def _():
            dk_ref[...] = jnp.zeros_like(dk_ref)
            dv_ref[...] = jnp.zeros_like(dv_ref)
            dbias_ref[...] = jnp.zeros_like(dbias_ref)
            dsink_acc[...] = jnp.zeros_like(dsink_acc)

        edge_ok = _win_edge_ok(n, n_blk)
        k = jnp.concatenate([k0[...], k1[...], k2[...]], axis=0)
        v = jnp.concatenate([v0[...], v1[...], v2[...]], axis=0)
        kt = k.T
        qt = (q_ref[...].astype(F32) * WIN_SCALE).T.astype(BF16)
        dot_ = do_ref[...].astype(F32).T.astype(BF16)
        dq_parts, dks, dvs = [], [], []
        for g in range(KV_B):
            kg, vg = k[:, HD_B * g:HD_B * (g + 1)], v[:, HD_B * g:HD_B * (g + 1)]
            q4t, do4t = _group_t(qt, g), _group_t(dot_, g)
            p, p_sink = _win_probs(kg, q4t, bias_ref, sink_ref, g, edge_ok)
            dp = _dot(vg, do4t, NN)
            delta = jnp.sum(p * dp, axis=0, keepdims=True)
            ds = p * (dp - delta)
            for j in range(GROUP):
                dbias_ref[GROUP * g + j] += ds[:, Q_BLOCK * j:Q_BLOCK * (j + 1)]
            dsink_acc[g:g + 1, :] += -p_sink * delta
            dsb = ds.astype(BF16)
            dq4t = _dot(kt[HD_B * g:HD_B * (g + 1), :], dsb, NN) * WIN_SCALE
            dq_parts += [dq4t[:, Q_BLOCK * j:Q_BLOCK * (j + 1)] for j in range(GROUP)]
            dks.append(_dot(dsb, q4t, NT))
            dvs.append(_dot(p.astype(BF16), do4t, NT))
        dq_ref[...] = jnp.concatenate(dq_parts, axis=0).T.astype(BF16)
        rows = pl.ds(pl.multiple_of(n * Q_BLOCK, Q_BLOCK), SPAN)
        dk_ref[rows, :] += jnp.concatenate(dks, axis=1)
        dv_ref[rows, :] += jnp.concatenate(dvs, axis=1)

        @pl.when(n == n_blk - 1)
        def _():
            acc = dsink_acc[...]
            dsink_ref[...] = jnp.concatenate(
                [jnp.sum(acc[:, Q_BLOCK * j:Q_BLOCK * (j + 1)], axis=1, keepdims=True) for j in range(GROUP)], axis=1)

    whole = lambda shape: pl.BlockSpec(shape, lambda n: (0,) * len(shape))
    return _pcall(
        body, name="win_bwd", grid=(n_blk,),
        in_specs=[qspec, *kspecs, *kspecs, bias_spec, sink_spec, pl.BlockSpec((Q_BLOCK, H_B * HD_B), lambda n: (n, 0)), ANY],
        out_specs=[qspec, whole((S + 2 * WINDOW, KV_B * HD_B)),
                   whole((S + 2 * WINDOW, KV_B * HD_B)), whole((H_B, SPAN, Q_BLOCK)), whole((KV_B, GROUP))],
        out_shape=[jax.ShapeDtypeStruct((S, PROJ_P), BF16), jax.ShapeDtypeStruct((S + 2 * WINDOW, KV_B * HD_B), F32),
                   jax.ShapeDtypeStruct((S + 2 * WINDOW, KV_B * HD_B), F32), jax.ShapeDtypeStruct((H_B, SPAN, Q_BLOCK), F32),
                   jax.ShapeDtypeStruct((KV_B, GROUP), F32)],
        scratch_shapes=[pltpu.VMEM((KV_B, GROUP * Q_BLOCK), F32)],
        dims=("arbitrary",), aliases={10: 0})(proj, kp, kp, kp, vp, vp, vp, bias_t, sinks_b, do_b, d_proj)


def _bias_table(rel_bias_t, onehot_t, in_band):
    def body(rb_ref, oh_ref, band_ref, o_ref):
        t = lax.dot_general(rb_ref[...], oh_ref[...], NN, preferred_element_type=F32, precision=lax.Precision.HIGHEST)
        o_ref[...] = jnp.where(band_ref[...] > 0.5, t, -1e30)

    n = onehot_t.shape[1]
    tn = _tile(n, 8192)
    return _pcall(body, name="bias_table", grid=(n // tn,),
                  in_specs=[pl.BlockSpec((H_B, NUM_BUCKETS), lambda j: (0, 0)), pl.BlockSpec((NUM_BUCKETS, tn), lambda j: (0, j)),
                            pl.BlockSpec((1, tn), lambda j: (0, j))],
                  out_specs=pl.BlockSpec((H_B, tn), lambda j: (0, j)),
                  out_shape=jax.ShapeDtypeStruct((H_B, n), F32), dims=("parallel",))(rel_bias_t, onehot_t, in_band)


def _bias_table_bwd(dbias, onehot_t):
    n = onehot_t.shape[1]
    tk = _tile(n, 8192)

    def body(d_ref, oh_ref, o_ref):
        @pl.when(pl.program_id(0) == 0)
        def _():
            o_ref[...] = jnp.zeros_like(o_ref)

        o_ref[...] += lax.dot_general(d_ref[...], oh_ref[...], NT, preferred_element_type=F32, precision=lax.Precision.HIGHEST)

    return _pcall(body, name="bias_table_bwd", grid=(n // tk,),
                  in_specs=[pl.BlockSpec((H_B, tk), lambda j: (0, j)), pl.BlockSpec((NUM_BUCKETS, tk), lambda j: (0, j))],
                  out_specs=pl.BlockSpec((H_B, NUM_BUCKETS), lambda j: (0, 0)),
                  out_shape=jax.ShapeDtypeStruct((H_B, NUM_BUCKETS), F32), dims=("arbitrary",))(dbias, onehot_t)


def _shift_down(u):
    row = lax.broadcasted_iota(jnp.int32, u.shape, 0)
    return jnp.where(row == 0, 0.0, pltpu.roll(u, 1, axis=0))


def _shift_up(u):
    n = u.shape[0]
    row = lax.broadcasted_iota(jnp.int32, u.shape, 0)
    return jnp.where(row == n - 1, 0.0, pltpu.roll(u, n - 1, axis=0))


def _conv(u, w, b):
    return _shift_down(u) * w[0:1, :] + u * w[1:2, :] + _shift_up(u) * w[2:3, :] + b


CONV_STRIP = 128
N_STRIPS = D_FF // CONV_STRIP


def _strip(rows, half):
    return pl.BlockSpec((rows, CONV_STRIP), lambda j: (0, j + half * N_STRIPS))


def _conv_gate_fwd(u, conv_w, conv_b, S):
    def body(ug_ref, uv_ref, wg_ref, wv_ref, bg_ref, bv_ref, a_ref):
        g = _conv(ug_ref[...].astype(F32), wg_ref[...], bg_ref[...])
        val = _conv(uv_ref[...].astype(F32), wv_ref[...], bv_ref[...])
        a_ref[...] = (g * _sigmoid(g) * val).astype(BF16)

    return _pcall(body, name="conv_gate_fwd", grid=(N_STRIPS,),
                  in_specs=[_strip(S, 0), _strip(S, 1), _strip(3, 0), _strip(3, 1), _strip(1, 0), _strip(1, 1)],
                  out_specs=_strip(S, 0), out_shape=jax.ShapeDtypeStruct((S, D_FF), BF16),
                  dims=("parallel",))(u, u, conv_w, conv_w, conv_b, conv_b)


def _conv_gate_bwd(u, conv_w, conv_b, da, S):
    def conv_bwd(duc, u_in, w):
        du = _shift_up(duc) * w[0:1, :] + duc * w[1:2, :] + _shift_down(duc) * w[2:3, :]
        dw = jnp.concatenate([jnp.sum(duc * _shift_down(u_in), axis=0, keepdims=True),
                              jnp.sum(duc * u_in, axis=0, keepdims=True),
                              jnp.sum(duc * _shift_up(u_in), axis=0, keepdims=True)], axis=0)
        return du, dw, jnp.sum(duc, axis=0, keepdims=True)

    def body(ug_ref, uv_ref, wg_ref, wv_ref, bg_ref, bv_ref, da_ref, dug_ref, duv_ref, dwg_ref, dwv_ref, dbg_ref, dbv_ref):
        u_g, u_v, w_g, w_v = ug_ref[...].astype(F32), uv_ref[...].astype(F32), wg_ref[...], wv_ref[...]
        g = _conv(u_g, w_g, bg_ref[...])
        val = _conv(u_v, w_v, bv_ref[...])
        da = da_ref[...].astype(F32)
        sg = _sigmoid(g)
        dval = da * (g * sg)
        dg = da * val * (sg * (1.0 + g * (1.0 - sg)))
        du_g, dwg_ref[...], dbg_ref[...] = conv_bwd(dg, u_g, w_g)
        du_v, dwv_ref[...], dbv_ref[...] = conv_bwd(dval, u_v, w_v)
        dug_ref[...] = du_g.astype(BF16)
        duv_ref[...] = du_v.astype(BF16)

    half = lambda r, dt: (_strip(r, 0), jax.ShapeDtypeStruct((r, D_FF), dt))
    outs = [half(S, BF16), half(S, BF16), half(3, F32), half(3, F32), half(1, F32), half(1, F32)]
    return _pcall(
        body, name="conv_gate_bwd", grid=(N_STRIPS,),
        in_specs=[_strip(S, 0), _strip(S, 1), _strip(3, 0), _strip(3, 1), _strip(1, 0), _strip(1, 1), _strip(S, 0)],
        out_specs=[o[0] for o in outs], out_shape=[o[1] for o in outs],
        dims=("parallel",))(u, u, conv_w, conv_w, conv_b, conv_b, da)


MESH = pl.DeviceIdType.MESH
ANY = pl.BlockSpec(memory_space=pl.ANY)


def _place():
    return lax.axis_index("x"), lax.axis_index("y"), lax.axis_index("c")


def _all_gather(shards):
    n_arr = len(shards)

    def body(*refs):
        ins, outs = refs[:n_arr], refs[n_arr:2 * n_arr]
        send_sems, recv_sems, local_sems = refs[2 * n_arr:]
        x, y, c = _place()
        me, sibling = (x, y, c), (x, y, 1 - c)
        chips = [(1 - x, y), (x, 1 - y), (1 - x, 1 - y)]

        def slot(a, p):
            return outs[a].at[4 * p[0] + 2 * p[1] + p[2]]

        def copy(a, k, block, to, src=None):
            return pltpu.make_async_remote_copy(
                src_ref=slot(a, block) if src is None else src, dst_ref=slot(a, block),
                send_sem=send_sems.at[a, k], recv_sem=recv_sems.at[a, k], device_id=to, device_id_type=MESH)

        mine = [pltpu.make_async_copy(ins[a], slot(a, me), local_sems.at[a]) for a in range(n_arr)]
        for cp in mine:
            cp.start()
        first = []
        for a in range(n_arr):
            first.append(copy(a, 0, me, sibling, src=ins[a]))
            first += [copy(a, 1 + j, me, (*chip, c), src=ins[a]) for j, chip in enumerate(chips)]
        for cp in first:
            cp.start()
        passed = []
        for j, chip in enumerate(chips):
            for a in range(n_arr):
                copy(a, 1 + j, (*chip, c), me).wait_recv()
                cp = copy(a, 4 + j, (*chip, c), sibling)
                cp.start()
                passed.append(cp)
        for a in range(n_arr):
            copy(a, 0, sibling, me).wait_recv()
            for j, chip in enumerate(chips):
                copy(a, 4 + j, (*chip, 1 - c), me).wait_recv()
        for cp in first + passed:
            cp.wait_send()
        for cp in mine:
            cp.wait()

    return pl.pallas_call(
        body, name="all_gather_weights",
        in_specs=[ANY] * n_arr, out_specs=[ANY] * n_arr,
        out_shape=[jax.ShapeDtypeStruct((N_DEV, *s.shape), s.dtype) for s in shards],
        scratch_shapes=[pltpu.SemaphoreType.DMA((n_arr, 7)), pltpu.SemaphoreType.DMA((n_arr, 7)), pltpu.SemaphoreType.DMA((n_arr,))],
    )(*shards)


def _xchg_out_shapes(stacked, replicated):
    return ([jax.ShapeDtypeStruct(s.shape, s.dtype) for s in stacked]
            + [jax.ShapeDtypeStruct((N_DEV, *r.shape), r.dtype) for r in replicated])


def _xchg_sems(n_arr):
    return [pltpu.SemaphoreType.DMA((n_arr, 7)), pltpu.SemaphoreType.DMA((n_arr, 7)), pltpu.SemaphoreType.DMA((n_arr,))]


def _xchg_copies(ins, outs, sems, n_st, with_recv):
    send_sems, recv_sems, local_sems = sems
    n_arr = len(ins)
    x, y, c = _place()
    me = 4 * x + 2 * y + c

    def src(a, idx):
        return ins[a].at[idx] if a < n_st else ins[a]

    mine = [pltpu.make_async_copy(src(a, me), outs[a].at[me], local_sems.at[a]) for a in range(n_arr)]
    pairs = []
    for k in range(1, N_DEV):
        px, py, pc = x ^ (k >> 2), y ^ ((k >> 1) & 1), c ^ (k & 1)
        peer = 4 * px + 2 * py + pc
        for a in range(n_arr):
            sems_k = dict(send_sem=send_sems.at[a, k - 1], recv_sem=recv_sems.at[a, k - 1], device_id_type=MESH)
            send = pltpu.make_async_remote_copy(src_ref=src(a, peer), dst_ref=outs[a].at[me], device_id=(px, py, pc), **sems_k)
            recv = None
            if with_recv:
                recv = pltpu.make_async_remote_copy(src_ref=src(a, peer), dst_ref=outs[a].at[peer], device_id=(x, y, c), **sems_k)
            pairs.append((send, recv))
    return mine, pairs


def _xchg_start(ins, outs, sems, n_st):
    mine, pairs = _xchg_copies(ins, outs, sems, n_st, False)
    for cp in mine:
        cp.start()
    for send, _ in pairs:
        send.start()


def _xchg_finish(ins, outs, sems, n_st):
    mine, pairs = _xchg_copies(ins, outs, sems, n_st, True)
    for _, recv in pairs:
        recv.wait_recv()
    for send, _ in pairs:
        send.wait_send()
    for cp in mine:
        cp.wait()


def _exchange(stacked, replicated, name):
    _, landed = _pcall(lambda: None, name=name, grid=(), in_specs=[], out_specs=[], out_shape=[], comm=(stacked, replicated))()
    return landed


def _adamw(parts, w, m, v, name):
    _, R, C = w.shape
    tr = R if (R <= 512 or R % 256) else 256

    def body(p_ref, w_ref, m_ref, v_ref, g_ref, d_ref, nm_ref, nv_ref):
        g = p_ref[0].astype(F32)
        for s in range(1, N_DEV):
            g = g + p_ref[s].astype(F32)
        m2 = ADAM_B1 * m_ref[0] + (1.0 - ADAM_B1) * g
        v2 = ADAM_B2 * v_ref[0] + (1.0 - ADAM_B2) * (g * g)
        m_hat = m2 / (1.0 - ADAM_B1 ** ADAM_STEP)
        v_hat = v2 / (1.0 - ADAM_B2 ** ADAM_STEP)
        g_ref[0] = g
        d_ref[0] = -ADAM_LR * (m_hat / (jnp.sqrt(v_hat) + ADAM_EPS) + ADAM_WD * w_ref[0])
        nm_ref[0] = m2
        nv_ref[0] = v2

    blk = pl.BlockSpec((1, tr, C), lambda i: (0, i, 0))
    return _pcall(body, name=name, grid=(R // tr,),
                  in_specs=[pl.BlockSpec((N_DEV, tr, C), lambda i: (0, i, 0)), blk, blk, blk],
                  out_specs=[blk] * 4, out_shape=[jax.ShapeDtypeStruct((1, R, C), F32)] * 4,
                  dims=("parallel",))(parts, w, m, v)


def _t5_bucket(rel):
    nb = NUM_BUCKETS // 2
    max_exact = nb // 2
    base = (rel > 0).astype(jnp.int32) * nb
    n = jnp.abs(rel)
    nf = jnp.maximum(n, 1).astype(jnp.float32)
    large = max_exact + (jnp.log(nf / max_exact) / math.log(MAX_DISTANCE / max_exact) * (nb - max_exact)).astype(jnp.int32)
    large = jnp.minimum(large, nb - 1)
    return base + jnp.where(n < max_exact, n, large)


def _unstack_cols(g):
    return jnp.transpose(g, (1, 0, 2)).reshape(g.shape[1], N_DEV * g.shape[2])


def _stack_cols(w, n=N_DEV):
    R = w.shape[0]
    return jnp.transpose(w.reshape(R, n, w.shape[1] // n), (1, 0, 2))


def _stack_halves(g, v):
    return jnp.concatenate([_stack_cols(g, N_DEV // 2), _stack_cols(v, N_DEV // 2)], axis=0)


def kernel(x, positions, norm1_g, w_in, q_a_norm_g, w_q_b, kv_a_norm_g, w_kv_b, rel_bias, sinks, w_out, norm2_g, w_up, conv_w, conv_b, w_down, final_norm_g, loss_target, m_norm1_g, m_w_in, m_q_a_norm_g, m_w_q_b, m_kv_a_norm_g, m_w_kv_b, m_rel_bias, m_sinks, m_w_out, m_norm2_g, m_w_up, m_conv_w, m_conv_b, m_w_down, m_final_norm_g, v_norm1_g, v_w_in, v_q_a_norm_g, v_w_q_b, v_kv_a_norm_g, v_w_kv_b, v_rel_bias, v_sinks, v_w_out, v_norm2_g, v_w_up, v_conv_w, v_conv_b, v_w_down, v_final_norm_g):
    S = x.shape[1]
    x = x[0]
    target = loss_target[0]
    TM = 256

    g_in, g_qb, g_kvb = _all_gather([w_in[0].astype(BF16), w_q_b[0].astype(BF16), w_kv_b[0].astype(BF16)])
    late_weights = [w_out[0].astype(BF16), w_up[0].astype(BF16), conv_w[0]]
    wi = _unstack_cols(g_in)
    c0, c1, c2, c3, c4, c5 = (sum(W_IN_SIZES[:i + 1]) for i in range(6))
    w_in_p = jnp.concatenate([wi[:, c4:c5], wi[:, c5:], wi[:, c1:c2], wi[:, :c0], wi[:, c2:c3], wi[:, c3:c4],
                              wi[:, c0:c0 + KV_LORA], wi[:, c0 + KV_LORA:c1], jnp.zeros((D_MODEL, 64), BF16)], axis=1)
    wq = _unstack_cols(g_qb).reshape(Q_LORA, H_A, QK_HEAD)
    w_qb_p = jnp.concatenate([wq[:, :, :QK_NOPE].reshape(Q_LORA, H_A * QK_NOPE), wq[:, :, QK_NOPE:].reshape(Q_LORA, H_A * QK_ROPE)], axis=1)
    w_kvb = _unstack_cols(g_kvb)

    half = QK_ROPE // 2
    inv_freq = ROPE_THETA ** (-jnp.arange(half, dtype=F32) / half)
    ang = positions.astype(F32)[:, None] * inv_freq[None, :]
    cos, sin = jnp.cos(ang), jnp.sin(ang)
    qa = jnp.arange(Q_BLOCK, dtype=jnp.int32)[:, None]
    kc = jnp.arange(SPAN, dtype=jnp.int32)[None, :]
    rel = (kc - WINDOW - qa).T
    in_band = (jnp.abs(rel) <= WINDOW).astype(F32).reshape(1, Q_BLOCK * SPAN)
    onehot_t = (_t5_bucket(rel).reshape(1, Q_BLOCK * SPAN) == jnp.arange(NUM_BUCKETS, dtype=jnp.int32)[:, None]).astype(F32)
    bias_t = _bias_table(rel_bias.T, onehot_t, in_band).reshape(H_B, SPAN, Q_BLOCK)
    sinks_b = jnp.broadcast_to(sinks.reshape(H_B, 1), (H_B, Q_BLOCK))

    (h1,) = _rowwise(lambda a, g: (_rms(a, g),), "norm1", S, TM, [_rows(x), _whole(norm1_g)], [("rows", D_MODEL, BF16)])
    proj = _matmul(h1, w_in_p, "nn", BF16, "proj")

    def lat_fn(qlat, ckv, kr, gq, gkv, cs, sn):
        r1, r2 = _rope(kr[:, :half], kr[:, half:QK_ROPE], cs, sn)
        return _rms(qlat, gq), _rms(ckv, gkv), jnp.concatenate([r1, r2], axis=1)

    qn, ckvn, k_rope = _rowwise(lat_fn, "latents", S, TM,
                                [_rows(proj, 256, PROJ_QLAT), _rows(proj, 128, PROJ_CKV), _rows(proj, 128, PROJ_KROPE),
                                 _whole(q_a_norm_g), _whole(kv_a_norm_g), _rows(cos), _rows(sin)],
                                [("rows", Q_LORA, BF16), ("rows", KV_LORA, BF16), ("rows", QK_ROPE, BF16)])
    q_p = _matmul(qn, w_qb_p, "nn", F32, "q_up", tn=1536)
    kv = _matmul(ckvn, w_kvb, "nn", BF16, "kv_up", tn=2048)

    def q_heads_fn(q, cs, sn):
        q = q * MLA_PRESCALE
        outs = []
        for h in range(H_A):
            o = H_A * QK_NOPE + QK_ROPE * h
            r1, r2 = _rope(q[:, o:o + half], q[:, o + half:o + QK_ROPE], cs, sn)
            outs.append(jnp.concatenate([q[:, QK_NOPE * h:QK_NOPE * (h + 1)], r1, r2], axis=1)[None])
        return (jnp.concatenate(outs, axis=0),)

    (q_full,) = _rowwise(q_heads_fn, "q_heads", S, TM, [_rows(q_p), _rows(cos), _rows(sin)], [("heads", H_A, QK_HEAD, BF16)])

    def k_heads_fn(kvb, kr):
        return (jnp.concatenate([jnp.concatenate([kvb[:, 256 * h:256 * h + QK_NOPE], kr], axis=1)[None] for h in range(H_A)], axis=0),)

    (k_full,) = _rowwise(k_heads_fn, "k_heads", S, TM, [_rows(kv), _rows(k_rope)], [("heads", H_A, QK_HEAD, BF16)], upcast=False)
    (o_a, lse), (g_out, g_up, g_cw) = _mla_fwd(q_full, k_full, kv, S, comm=([], late_weights))
    w_out_f = g_out.reshape(D_MODEL, D_MODEL)
    w_up_f = _unstack_cols(g_up)
    conv_w_f = _unstack_cols(g_cw)

    kp = jnp.pad(proj[:, 256 * PROJ_KB:256 * (PROJ_KB + 1)], ((WINDOW, WINDOW), (0, 0)))
    vp = jnp.pad(proj[:, 256 * PROJ_VB:256 * (PROJ_VB + 1)], ((WINDOW, WINDOW), (0, 0)))
    o_b, (g_down,) = _win_fwd(proj, kp, vp, bias_t, sinks_b, S, comm=([], [w_down[0].astype(BF16)]))
    w_down_f = g_down.reshape(D_FF, D_MODEL)

    (mixed,) = _rowwise(lambda ga, gb, oa, ob: (_sigmoid(ga) * oa + _sigmoid(gb) * ob,), "gate_mix", S, TM,
                        [_rows(proj, 1024, PROJ_GA), _rows(proj, 1024, PROJ_GB), _rows(o_a), _rows(o_b)], [("rows", D_MODEL, BF16)])
    x1, h2 = _matmul(mixed, w_out_f, "nn", None, "out_proj", residual=x, tm=512,
                     epi=(lambda a, g: (a, _rms(a, g)), [_whole(norm2_g)], [("rows", D_MODEL, F32), ("rows", D_MODEL, BF16)]))
    u = _matmul(h2, w_up_f, "nn", BF16, "ffn_up", tn=1408)
    act = _conv_gate_fwd(u, conv_w_f, conv_b, S)

    def final_fn(a, g, t):
        err = _rms(a, g) - t
        loss = 0.5 * jnp.sum(jnp.mean(err * err, axis=-1, keepdims=True), axis=0, keepdims=True)
        dx, dg = _rms_bwd(err * (1.0 / D_MODEL), a, g)
        return dx, dx, dg, jnp.broadcast_to(loss, (1, 128))

    gfin = final_norm_g.reshape(1, D_MODEL)
    dx2, dx2_b, d_gfin, loss_row = _matmul(
        act, w_down_f, "nn", None, "ffn_down_loss", residual=x1, tm=512,
        epi=(final_fn, [_whole(gfin), _rows(target)],
             [("rows", D_MODEL, F32), ("rows", D_MODEL, BF16), ("acc", 1, D_MODEL), ("acc", 1, 128)]))
    d_act = _matmul(dx2_b, w_down_f, "nt", BF16, "ffn_down_dx", tn=1408)
    d_w_down = _matmul(act, dx2_b, "tn", F32, "ffn_down_dw")
    du_g, du_v, dcw_g, dcw_v, dcb_g, dcb_v = _conv_gate_bwd(u, conv_w_f, conv_b, d_act, S)
    d_conv_b = jnp.concatenate([dcb_g, dcb_v], axis=1)

    def norm_bwd_fn(dh, a, g, dres):
        dx, dg = _rms_bwd(dh, a, g)
        dx = dx + dres
        return dx, dx, dg

    dx1, dx1_b, d_g2 = _matmul(du_g, w_up_f, "nt", None, "ffn_up_dx_norm2_bwd", tm=256, a2=du_v,
                               epi=(norm_bwd_fn, [_rows(x1), _whole(norm2_g), _rows(dx2)],
                                    [("rows", D_MODEL, F32), ("rows", D_MODEL, BF16), ("acc", 1, D_MODEL)]))
    d_w_up_g = _matmul(h2, du_g, "tn", F32, "ffn_up_dw_gate", tm=512, tn=1408)
    d_w_up_v = _matmul(h2, du_v, "tn", F32, "ffn_up_dw_value", tm=512, tn=1408)
    d_w_out = _matmul(mixed, dx1_b, "tn", F32, "out_proj_dw", tm=512)

    def gate_bwd_fn(dm, ga, gb, oa, ob):
        sa, sb = _sigmoid(ga), _sigmoid(gb)
        return jnp.concatenate([dm * oa * sa * (1.0 - sa), dm * ob * sb * (1.0 - sb)], axis=1), dm * sa, dm * sb

    d_proj, do_a, do_b = _matmul(
        dx1_b, w_out_f, "nt", None, "out_proj_dx_gate_bwd", tm=512,
        epi=(gate_bwd_fn, [_rows(proj, 1024, PROJ_GA), _rows(proj, 1024, PROJ_GB), _rows(o_a), _rows(o_b)],
             [("cols", 2 * D_MODEL, 0, PROJ_P, BF16), ("rows", D_MODEL, BF16), ("rows", D_MODEL, BF16)]))

    d_proj, dk_acc, dv_acc, d_bias, d_sinks_g = _win_bwd(proj, kp, vp, bias_t, sinks_b, do_b, d_proj, S)
    d_rel_bias = _bias_table_bwd(d_bias.reshape(H_B, Q_BLOCK * SPAN), onehot_t).T
    d_sinks = d_sinks_g.reshape(1, H_B)

    early = [d_w_out.reshape(N_DEV, D_MODEL // N_DEV, D_MODEL).astype(BF16), _stack_halves(d_w_up_g, d_w_up_v).astype(BF16),
             d_w_down.reshape(N_DEV, D_FF // N_DEV, D_MODEL).astype(BF16), _stack_halves(dcw_g, dcw_v)]
    (dq_full, dk_full, dv_full), recv_early = _mla_bwd(q_full, k_full, kv, do_a, o_a, lse, S, comm=(early, []))

    def dq_post_fn(dq, cs, sn):
        nope = [dq[h, :, :QK_NOPE] for h in range(H_A)]
        rope = []
        for h in range(H_A):
            rope += list(_rope_bwd(dq[h, :, QK_NOPE:QK_NOPE + half], dq[h, :, QK_NOPE + half:], cs, sn))
        return (jnp.concatenate(nope + rope, axis=1),)

    (dq_p,) = _rowwise(dq_post_fn, "dq_post", S, TM, [_heads(dq_full), _rows(cos), _rows(sin)], [("rows", H_A * QK_HEAD, BF16)])

    def dkv_post_fn(dk, dv, cs, sn):
        dk = dk * math.log(2.0)
        dkv = jnp.concatenate([jnp.concatenate([dk[h, :, :QK_NOPE], dv[h]], axis=1) for h in range(H_A)], axis=1)
        dkr = dk[0, :, QK_NOPE:]
        for h in range(1, H_A):
            dkr = dkr + dk[h, :, QK_NOPE:]
        r1, r2 = _rope_bwd(dkr[:, :half], dkr[:, half:], cs, sn)
        return dkv, jnp.concatenate([r1, r2], axis=1)

    dkv, d_krope = _rowwise(dkv_post_fn, "dkv_post", S, TM, [_heads(dk_full), _heads(dv_full), _rows(cos), _rows(sin)],
                            [("rows", H_A * (QK_NOPE + V_DIM), BF16), ("rows", QK_ROPE, F32)])
    d_qn = _matmul(dq_p, w_qb_p, "nt", F32, "q_up_dx")
    d_w_qb_p = _matmul(qn, dq_p, "tn", F32, "q_up_dw", tn=1536)
    d_ckvn = _matmul(dkv, w_kvb, "nt", F32, "kv_up_dx")
    d_w_kvb = _matmul(ckvn, dkv, "tn", F32, "kv_up_dw", tn=2048)

    def lat_bwd_fn(dqn, dckvn, dkr, qlat, ckv, gq, gkv, dkb, dvb):
        dql, dgq = _rms_bwd(dqn, qlat, gq)
        dck, dgkv = _rms_bwd(dckvn, ckv, gkv)
        tail = jnp.concatenate([dql, dkb, dvb, dck, dkr, jnp.zeros_like(dkr)], axis=1)
        return tail, dgq, dgkv

    shifted = lambda arr: (arr, lambda tm: pl.BlockSpec((tm, arr.shape[1]), lambda i, *_: (i + WINDOW // tm, 0)))
    TL = min(128, S)
    d_proj, d_gq, d_gkv = _rowwise(lat_bwd_fn, "latents_bwd", S, TL,
                                   [_rows(d_qn), _rows(d_ckvn), _rows(d_krope), _rows(proj, 256, PROJ_QLAT), _rows(proj, 128, PROJ_CKV),
                                    _whole(q_a_norm_g), _whole(kv_a_norm_g), shifted(dk_acc), shifted(dv_acc)],
                                   [("cols", 1024, 3, PROJ_P, BF16), ("acc", 1, Q_LORA), ("acc", 1, KV_LORA)], into=(d_proj, 0))
    d_w_in_p = _matmul(h1, d_proj, "tn", F32, "proj_dw", tm=512)

    dp = d_w_in_p
    d_w_in = jnp.concatenate([dp[:, 3072:3328], dp[:, 3840:3968], dp[:, 3968:4032], dp[:, 2048:3072], dp[:, 3328:3584],
                              dp[:, 3584:3840], dp[:, 0:1024], dp[:, 1024:2048]], axis=1)
    d_w_qb = jnp.concatenate([d_w_qb_p[:, :H_A * QK_NOPE].reshape(Q_LORA, H_A, QK_NOPE),
                              d_w_qb_p[:, H_A * QK_NOPE:].reshape(Q_LORA, H_A, QK_ROPE)], axis=2).reshape(Q_LORA, H_A * QK_HEAD)
    late = [_stack_cols(d_w_in).astype(BF16), _stack_cols(d_w_qb).astype(BF16), _stack_cols(d_w_kvb).astype(BF16)]
    def norm1_bwd_fn(dh, a, g, dres):
        dx, dg = _rms_bwd(dh, a, g)
        return dx + dres, dg

    (grad_x, d_g1), recv_late = _matmul(
        d_proj, w_in_p, "nt", None, "proj_dx_norm1_bwd", tm=512, comm=(late, []),
        epi=(norm1_bwd_fn, [_rows(x), _whole(norm1_g), _rows(dx1)], [("rows", D_MODEL, F32), ("acc", 1, D_MODEL)]))
    recv = [*recv_late, *recv_early]

    small_parts = [d_g1, d_gq, d_gkv, d_rel_bias.reshape(1, NUM_BUCKETS * H_B), d_sinks, d_g2, d_conv_b, d_gfin]
    small = jnp.concatenate(small_parts, axis=1)
    n_small = small.shape[1]
    pad = (-n_small) % 128
    small = jnp.pad(small, ((0, 0), (0, pad)))
    (recv_small,) = _exchange([], [small], "exchange_small_grads")

    def flat(a):
        return a.reshape(1, -1)

    small_w = [norm1_g, q_a_norm_g, kv_a_norm_g, rel_bias, sinks, norm2_g, conv_b, final_norm_g]
    small_m = [m_norm1_g, m_q_a_norm_g, m_kv_a_norm_g, m_rel_bias, m_sinks, m_norm2_g, m_conv_b, m_final_norm_g]
    small_v = [v_norm1_g, v_q_a_norm_g, v_kv_a_norm_g, v_rel_bias, v_sinks, v_norm2_g, v_conv_b, v_final_norm_g]
    cat = lambda parts: jnp.pad(jnp.concatenate([flat(a) for a in parts], axis=1), ((0, 0), (0, pad)))[None]
    sm = _adamw(recv_small, cat(small_w), cat(small_m), cat(small_v), "adamw_small")
    big_names = ["w_in", "w_q_b", "w_kv_b", "w_out", "w_up", "w_down", "conv_w"]
    big_w = [w_in, w_q_b, w_kv_b, w_out, w_up, w_down, conv_w]
    big_m = [m_w_in, m_w_q_b, m_w_kv_b, m_w_out, m_w_up, m_w_down, m_conv_w]
    big_v = [v_w_in, v_w_q_b, v_w_kv_b, v_w_out, v_w_up, v_w_down, v_conv_w]
    big = {n: _adamw(r, w, m, v, "adamw_" + n) for n, r, w, m, v in zip(big_names, recv, big_w, big_m, big_v)}

    loss = lax.psum(loss_row[0, 0], ("x", "y", "c"))
    order = ["norm1_g", "w_in", "q_a_norm_g", "w_q_b", "kv_a_norm_g", "w_kv_b", "rel_bias", "sinks", "w_out", "norm2_g", "w_up",
             "conv_w", "conv_b", "w_down", "final_norm_g"]
    small_names = ["norm1_g", "q_a_norm_g", "kv_a_norm_g", "rel_bias", "sinks", "norm2_g", "conv_b", "final_norm_g"]
    offs, o = {}, 0
    for n, a in zip(small_names, small_w):
        offs[n] = (o, a.size, a.shape)
        o += a.size
    outs = [loss, grad_x[None]]
    for kind in range(4):
        for n in order:
            if n in big:
                outs.append(big[n][kind])
            else:
                o, size, shape = offs[n]
                outs.append(sm[kind][0, 0, o:o + size].reshape(shape))
    return tuple(outs)
```

```python
import math

import jax
import jax.numpy as jnp
from jax import lax
from jax.experimental import pallas as pl
from jax.experimental.pallas import tpu as pltpu

F32 = jnp.float32
BF16 = jnp.bfloat16

N_DEV = 8
D_MODEL = 1024
EPS = 1e-6
H_A, QK_NOPE, QK_ROPE, V_DIM, Q_LORA, KV_LORA = 8, 128, 64, 128, 256, 128
QK_HEAD = QK_NOPE + QK_ROPE
ROPE_THETA = 10000.0
H_B, KV_B, GROUP, HD_B, WINDOW, Q_BLOCK = 16, 4, 4, 64, 128, 128
SPAN = Q_BLOCK + 2 * WINDOW
NUM_BUCKETS, MAX_DISTANCE = 32, 128
D_FF = 2816
ADAM_LR, ADAM_B1, ADAM_B2, ADAM_EPS, ADAM_WD, ADAM_STEP = 0.001, 0.9, 0.999, 1e-08, 0.01, 10

W_IN_SIZES = (Q_LORA, KV_LORA + QK_ROPE, H_B * HD_B, KV_B * HD_B, KV_B * HD_B, D_MODEL, D_MODEL)
W_IN_COLS = sum(W_IN_SIZES)
PROJ_P = 4096
PROJ_GA, PROJ_GB, PROJ_QB, PROJ_QLAT, PROJ_KB, PROJ_VB, PROJ_CKV, PROJ_KROPE = 0, 1, 2, 12, 13, 14, 30, 31

VMEM_LIMIT = 56 * 1024 * 1024

NN = (((1,), (0,)), ((), ()))
NT = (((1,), (1,)), ((), ()))
TN = (((0,), (0,)), ((), ()))


def _pcall(body, *, name, grid, in_specs, out_specs, out_shape, scratch_shapes=(), dims=None, comm=None, aliases=None):
    if comm is None:
        params = pltpu.CompilerParams(dimension_semantics=dims, vmem_limit_bytes=VMEM_LIMIT)
        return pl.pallas_call(body, name=name, grid=grid, in_specs=in_specs, out_specs=out_specs, out_shape=out_shape,
                              scratch_shapes=list(scratch_shapes), input_output_aliases=aliases or {}, compiler_params=params)
    assert not aliases
    stacked, replicated = comm
    arrs = [*stacked, *replicated]
    n_st, n_arr = len(stacked), len(arrs)
    single = not isinstance(out_specs, (list, tuple))
    o_specs, o_shape = ([out_specs], [out_shape]) if single else (list(out_specs), list(out_shape))
    n_in, n_out = len(in_specs), len(o_specs)

    def wrapped(*refs):
        c_in = refs[n_in:n_in + n_arr]
        c_out = refs[n_in + n_arr + n_out:n_in + 2 * n_arr + n_out]
        sems = refs[len(refs) - 3:]
        own = (*refs[:n_in], *refs[n_in + n_arr:n_in + n_arr + n_out], *refs[n_in + 2 * n_arr + n_out:len(refs) - 3])
        if not grid:
            _xchg_start(c_in, c_out, sems, n_st)
            _xchg_finish(c_in, c_out, sems, n_st)
            return
        first = last = None
        for d, n in enumerate(grid):
            pid = pl.program_id(d)
            first = (pid == 0) if first is None else first & (pid == 0)
            last = (pid == n - 1) if last is None else last & (pid == n - 1)

        @pl.when(first)
        def _():
            _xchg_start(c_in, c_out, sems, n_st)

        body(*own)

        @pl.when(last)
        def _():
            _xchg_finish(c_in, c_out, sems, n_st)

    params = pltpu.CompilerParams(dimension_semantics=("arbitrary",) * len(grid), vmem_limit_bytes=VMEM_LIMIT)
    call = pl.pallas_call(wrapped, name=name, grid=grid, in_specs=[*in_specs, *[ANY] * n_arr], out_specs=[*o_specs, *[ANY] * n_arr],
                          out_shape=[*o_shape, *_xchg_out_shapes(stacked, replicated)],
                          scratch_shapes=[*scratch_shapes, *_xchg_sems(n_arr)], compiler_params=params)

    def run(*args):
        res = call(*args, *arrs)
        outs, landed = res[:n_out], res[n_out:]
        return (outs[0] if single else outs), landed

    return run


def _dot(a, b, dn):
    return lax.dot_general(a, b, dn, preferred_element_type=F32)


def _tile(n, target):
    best = None
    for t in range(128, min(n, target) + 1, 128):
        if n % t == 0:
            best = t
    return n if best is None else best


def _matmul(a, b, mode, out_dtype, name, residual=None, tm=1024, tn=1024, comm=None, a2=None, epi=None):
    if mode == "nn":
        (M, K), N = a.shape, b.shape[1]
    elif mode == "nt":
        (M, K), N = a.shape, b.shape[0]
    else:
        (K, M), N = a.shape, b.shape[1]
    tm, tn = _tile(M, tm), _tile(N, tn)
    a_spec = pl.BlockSpec((K, tm), lambda i, j: (0, i)) if mode == "tn" else pl.BlockSpec((tm, K), lambda i, j: (i, 0))
    b_spec = pl.BlockSpec((tn, b.shape[1]), lambda i, j: (j, 0)) if mode == "nt" else pl.BlockSpec((K, tn), lambda i, j: (0, j))
    o_spec = pl.BlockSpec((tm, tn), lambda i, j: (i, j))
    in_specs, args = [a_spec, b_spec], [a, b]
    if a2 is not None:
        assert (mode == "nt" and K + a2.shape[1] == b.shape[1]) or (mode == "nn" and K + a2.shape[1] == b.shape[0])
        if mode == "nn":
            b_spec = in_specs[1] = pl.BlockSpec((b.shape[0], tn), lambda i, j: (0, j))
        in_specs.append(pl.BlockSpec((tm, a2.shape[1]), lambda i, j: (i, 0)))
        args.append(a2)
    if residual is not None:
        in_specs.append(o_spec)
        args.append(residual)
    n_mm = len(args)
    scratch = [pltpu.VMEM((tm, K), a.dtype)] if mode == "tn" else []
    if epi is None:
        out_specs, out_shape, is_acc = o_spec, jax.ShapeDtypeStruct((M, N), out_dtype), None
    else:
        assert tn == N
        fn, epi_ins, epi_outs = epi
        in_specs += [mk(tm) for _, mk in epi_ins]
        args += [arr for arr, _ in epi_ins]
        out_specs, out_shape, is_acc = _row_out_specs(epi_outs, M, tm)

    def body(*refs):
        a_ref, b_ref = refs[0], refs[1]
        n_out = 1 if epi is None else len(is_acc)
        out_refs = refs[len(args):len(args) + n_out]
        if mode == "tn":
            at_ref = refs[len(args) + n_out]

            @pl.when(pl.program_id(1) == 0)
            def _():
                at_ref[...] = a_ref[...].T

            acc = _dot(at_ref[...], b_ref[...], NN)
        elif a2 is not None and mode == "nt":
            acc = _dot(a_ref[...], b_ref[:, :K], NT) + _dot(refs[2][...], b_ref[:, K:], NT)
        elif a2 is not None:
            acc = _dot(a_ref[...], b_ref[:K, :], NN) + _dot(refs[2][...], b_ref[K:, :], NN)
        else:
            acc = _dot(a_ref[...], b_ref[...], NT if mode == "nt" else NN)
        if residual is not None:
            acc = acc + refs[n_mm - 1][...]
        if epi is None:
            out_refs[0][...] = acc.astype(out_dtype)
        else:
            _store_rows(out_refs, fn(acc, *[_load_f32(r) for r in refs[n_mm:len(args)]]), is_acc)

    return _pcall(body, name=name, grid=(M // tm, N // tn), in_specs=in_specs, out_specs=out_specs,
                  out_shape=out_shape, scratch_shapes=scratch,
                  dims=("arbitrary" if epi is not None else "parallel", "arbitrary"), comm=comm)(*args)


def _rows(arr, width=None, col=0):
    width = arr.shape[1] if width is None else width
    return (arr, lambda tm: pl.BlockSpec((tm, width), lambda i, *_: (i, col)))


def _heads(arr):
    return (arr, lambda tm: pl.BlockSpec((arr.shape[0], tm, arr.shape[2]), lambda i, *_: (0, i, 0)))


def _whole(arr):
    nd = arr.ndim
    return (arr, lambda tm: pl.BlockSpec(arr.shape, lambda i, *_: (0,) * nd))


def _row_out_specs(outs, n_rows, tm):
    out_specs, out_shape, is_acc = [], [], []
    for o in outs:
        if o[0] == "rows":
            out_specs.append(pl.BlockSpec((tm, o[1]), lambda i, *_: (i, 0)))
            out_shape.append(jax.ShapeDtypeStruct((n_rows, o[1]), o[2]))
        elif o[0] == "cols":
            out_specs.append(pl.BlockSpec((tm, o[1]), lambda i, *_, c=o[2]: (i, c)))
            out_shape.append(jax.ShapeDtypeStruct((n_rows, o[3]), o[4]))
        elif o[0] == "heads":
            out_specs.append(pl.BlockSpec((o[1], tm, o[2]), lambda i, *_: (0, i, 0)))
            out_shape.append(jax.ShapeDtypeStruct((o[1], n_rows, o[2]), o[3]))
        else:
            out_specs.append(pl.BlockSpec((o[1], o[2]), lambda i, *_: (0, 0)))
            out_shape.append(jax.ShapeDtypeStruct((o[1], o[2]), F32))
        is_acc.append(o[0] == "acc")
    return out_specs, out_shape, is_acc


def _load_f32(r):
    v = r[...]
    return v.astype(F32) if v.dtype == BF16 else v


def _store_rows(out_refs, vals, is_acc):
    for r, v, acc in zip(out_refs, vals, is_acc):
        if acc:
            @pl.when(pl.program_id(0) == 0)
            def _():
                r[...] = jnp.zeros_like(r)

            r[...] += v
        else:
            r[...] = v.astype(r.dtype)


def _rowwise(fn, name, n_rows, tm, ins, outs, upcast=True, into=None):
    tm = min(tm, n_rows)
    assert n_rows % tm == 0
    in_specs = [mk(tm) for _, mk in ins]
    out_specs, out_shape, is_acc = _row_out_specs(outs, n_rows, tm)
    n_in = len(ins)
    args = [a for a, _ in ins]
    aliases = {}
    if into is not None:
        in_specs.append(ANY)
        args.append(into[0])
        aliases = {n_in: into[1]}

    def body(*refs):
        vals = fn(*[_load_f32(r) if upcast else r[...] for r in refs[:n_in]])
        _store_rows(refs[len(args):], vals, is_acc)

    return _pcall(body, name=name, grid=(n_rows // tm,), in_specs=in_specs, out_specs=out_specs,
                  out_shape=out_shape, dims=("arbitrary",), aliases=aliases)(*args)


def _rms(x, g):
    r = lax.rsqrt(jnp.mean(x * x, axis=-1, keepdims=True) + EPS)
    return x * r * g


def _rms_bwd(dy, x, g):
    r = lax.rsqrt(jnp.mean(x * x, axis=-1, keepdims=True) + EPS)
    xhat = x * r
    dxhat = dy * g
    dx = r * (dxhat - xhat * jnp.mean(dxhat * xhat, axis=-1, keepdims=True))
    return dx, jnp.sum(dy * xhat, axis=0, keepdims=True)


def _rope(x1, x2, cos, sin):
    return x1 * cos - x2 * sin, x2 * cos + x1 * sin


def _rope_bwd(d1, d2, cos, sin):
    return d1 * cos + d2 * sin, d2 * cos - d1 * sin


def _sigmoid(x):
    return 1.0 / (1.0 + jnp.exp(-x))


MLA_SCALE = 1.0 / math.sqrt(QK_HEAD)
MLA_PRESCALE = MLA_SCALE * math.log2(math.e)
MLA_TQ, MLA_KC = 1024, 1024


def _mla_fwd(q_full, k_full, kv, S, comm=None):
    tq, kc = min(MLA_TQ, S), min(MLA_KC, S)

    def body(q_ref, k_ref, v_ref, o_ref, lse_ref):
        q = q_ref[0]
        m = jnp.full((tq, 1), -1e30, F32)
        l = jnp.zeros((tq, 1), F32)
        acc = jnp.zeros((tq, V_DIM), F32)
        for c in range(S // kc):
            s = _dot(q, k_ref[0, c * kc:(c + 1) * kc, :], NT)
            m_new = jnp.maximum(m, jnp.max(s, axis=-1, keepdims=True))
            alpha = jnp.exp2(m - m_new)
            p = jnp.exp2(s - m_new)
            l = alpha * l + jnp.sum(p, axis=-1, keepdims=True)
            acc = alpha * acc + _dot(p.astype(BF16), v_ref[c * kc:(c + 1) * kc, :], NN)
            m = m_new
        o_ref[...] = (acc / l).astype(BF16)
        lse_ref[0] = m + jnp.log2(l)

    return _pcall(
        body, name="mla_fwd", grid=(H_A, S // tq),
        in_specs=[pl.BlockSpec((1, tq, QK_HEAD), lambda h, i: (h, i, 0)),
                  pl.BlockSpec((1, S, QK_HEAD), lambda h, i: (h, 0, 0)),
                  pl.BlockSpec((S, V_DIM), lambda h, i: (0, 2 * h + 1))],
        out_specs=[pl.BlockSpec((tq, V_DIM), lambda h, i: (i, h)),
                   pl.BlockSpec((1, tq, 1), lambda h, i: (h, i, 0))],
        out_shape=[jax.ShapeDtypeStruct((S, H_A * V_DIM), BF16), jax.ShapeDtypeStruct((H_A, S, 1), F32)],
        dims=("parallel", "parallel"), comm=comm)(q_full, k_full, kv)


def _mla_bwd(q_full, k_full, kv, do_a, o_a, lse, S, comm=None):
    tq, kc = min(MLA_TQ, S), min(MLA_KC, S)

    def body(q_ref, k_ref, v_ref, do_ref, o_ref, lse_ref, dq_ref, dk_ref, dv_ref):
        @pl.when(pl.program_id(1) == 0)
        def _():
            dk_ref[...] = jnp.zeros_like(dk_ref)
            dv_ref[...] = jnp.zeros_like(dv_ref)

        q = q_ref[0]
        do = do_ref[...]
        lse_q = lse_ref[0]
        delta = jnp.sum(do.astype(F32) * o_ref[...].astype(F32), axis=-1, keepdims=True)
        dq = jnp.zeros((tq, QK_HEAD), F32)
        for c in range(S // kc):
            k = k_ref[0, c * kc:(c + 1) * kc, :]
            v = v_ref[c * kc:(c + 1) * kc, :]
            p = jnp.exp2(_dot(q, k, NT) - lse_q)
            ds = (p * (_dot(do, v, NT) - delta)).astype(BF16)
            dq = dq + _dot(ds, k, NN)
            dk_ref[0, c * kc:(c + 1) * kc, :] += _dot(ds, q, TN)
            dv_ref[0, c * kc:(c + 1) * kc, :] += _dot(p.astype(BF16), do, TN)
        dq_ref[0] = dq * MLA_SCALE

    return _pcall(
        body, name="mla_bwd", grid=(H_A, S // tq),
        in_specs=[pl.BlockSpec((1, tq, QK_HEAD), lambda h, i: (h, i, 0)),
                  pl.BlockSpec((1, S, QK_HEAD), lambda h, i: (h, 0, 0)),
                  pl.BlockSpec((S, V_DIM), lambda h, i: (0, 2 * h + 1)),
                  pl.BlockSpec((tq, V_DIM), lambda h, i: (i, h)),
                  pl.BlockSpec((tq, V_DIM), lambda h, i: (i, h)),
                  pl.BlockSpec((1, tq, 1), lambda h, i: (h, i, 0))],
        out_specs=[pl.BlockSpec((1, tq, QK_HEAD), lambda h, i: (h, i, 0)),
                   pl.BlockSpec((1, S, QK_HEAD), lambda h, i: (h, 0, 0)),
                   pl.BlockSpec((1, S, V_DIM), lambda h, i: (h, 0, 0))],
        out_shape=[jax.ShapeDtypeStruct((H_A, S, QK_HEAD), F32), jax.ShapeDtypeStruct((H_A, S, QK_HEAD), F32),
                   jax.ShapeDtypeStruct((H_A, S, V_DIM), F32)],
        dims=("parallel", "arbitrary"), comm=comm)(q_full, k_full, kv, do_a, o_a, lse)


WIN_SCALE = 1.0 / math.sqrt(HD_B)


def _win_specs(S):
    qspec = pl.BlockSpec((Q_BLOCK, H_B * HD_B), lambda n: (n, PROJ_QB))
    kspecs = [pl.BlockSpec((Q_BLOCK, KV_B * HD_B), lambda n, d=d: (n + d, 0)) for d in range(3)]
    bias_spec = pl.BlockSpec((H_B, SPAN, Q_BLOCK), lambda n: (0, 0, 0))
    sink_spec = pl.BlockSpec((H_B, Q_BLOCK), lambda n: (0, 0))
    return qspec, kspecs, bias_spec, sink_spec


def _win_edge_ok(n, n_blk):
    row = lax.broadcasted_iota(jnp.int32, (SPAN, 1), 0)
    return jnp.logical_not(((n == 0) & (row < WINDOW)) | ((n == n_blk - 1) & (row >= SPAN - WINDOW)))


def _lanes4(pieces):
    return jnp.concatenate(pieces, axis=1)


def _win_probs(kg, q4t, bias_ref, sink_ref, g, edge_ok):
    bias4 = _lanes4([bias_ref[GROUP * g + j] for j in range(GROUP)])
    sink4 = _lanes4([sink_ref[GROUP * g + j:GROUP * g + j + 1, :] for j in range(GROUP)])
    s = jnp.where(edge_ok, _dot(kg, q4t, NN) + bias4, -1e30)
    m = jnp.maximum(jnp.max(s, axis=0, keepdims=True), sink4)
    p = jnp.exp(s - m)
    e_sink = jnp.exp(sink4 - m)
    inv_l = 1.0 / (jnp.sum(p, axis=0, keepdims=True) + e_sink)
    return p * inv_l, e_sink * inv_l


def _group_t(xt, g):
    return _lanes4([xt[HD_B * (GROUP * g + j):HD_B * (GROUP * g + j + 1), :] for j in range(GROUP)])


def _win_fwd(proj, kp, vp, bias_t, sinks_b, S, comm=None):
    n_blk = S // Q_BLOCK
    qspec, kspecs, bias_spec, sink_spec = _win_specs(S)

    def body(q_ref, k0, k1, k2, v0, v1, v2, bias_ref, sink_ref, o_ref):
        n = pl.program_id(0)
        edge_ok = _win_edge_ok(n, n_blk)
        k = jnp.concatenate([k0[...], k1[...], k2[...]], axis=0)
        vt = jnp.concatenate([v0[...], v1[...], v2[...]], axis=0).T
        qt = (q_ref[...].astype(F32) * WIN_SCALE).T.astype(BF16)
        parts = []
        for g in range(KV_B):
            p, _ = _win_probs(k[:, HD_B * g:HD_B * (g + 1)], _group_t(qt, g), bias_ref, sink_ref, g, edge_ok)
            o4t = _dot(vt[HD_B * g:HD_B * (g + 1), :], p.astype(BF16), NN)
            parts += [o4t[:, Q_BLOCK * j:Q_BLOCK * (j + 1)] for j in range(GROUP)]
        o_ref[...] = jnp.concatenate(parts, axis=0).T.astype(BF16)

    return _pcall(body, name="win_fwd", grid=(n_blk,),
                  in_specs=[qspec, *kspecs, *kspecs, bias_spec, sink_spec],
                  out_specs=pl.BlockSpec((Q_BLOCK, H_B * HD_B), lambda n: (n, 0)),
                  out_shape=jax.ShapeDtypeStruct((S, H_B * HD_B), BF16),
                  dims=("parallel",), comm=comm)(proj, kp, kp, kp, vp, vp, vp, bias_t, sinks_b)


def _win_bwd(proj, kp, vp, bias_t, sinks_b, do_b, d_proj, S):
    n_blk = S // Q_BLOCK
    qspec, kspecs, bias_spec, sink_spec = _win_specs(S)

    def body(q_ref, k0, k1, k2, v0, v1, v2, bias_ref, sink_ref, do_ref, _, dq_ref, dk_ref, dv_ref, dbias_ref, dsink_ref, dsink_acc):
        n = pl.program_id(0)

        @pl.when(n == 0)
        def _():
            dk_ref[...] = jnp.zeros_like(dk_ref)
            dv_ref[...] = jnp.zeros_like(dv_ref)
            dbias_ref[...] = jnp.zeros_like(dbias_ref)
            dsink_acc[...] = jnp.zeros_like(dsink_acc)

        edge_ok = _win_edge_ok(n, n_blk)
        k = jnp.concatenate([k0[...], k1[...], k2[...]], axis=0)
        v = jnp.concatenate([v0[...], v1[...], v2[...]], axis=0)
        kt = k.T
        qt = (q_ref[...].astype(F32) * WIN_SCALE).T.astype(BF16)
        dot_ = do_ref[...].astype(F32).T.astype(BF16)
        dq_parts, dks, dvs = [], [], []
        for g in range(KV_B):
            kg, vg = k[:, HD_B * g:HD_B * (g + 1)], v[:, HD_B * g:HD_B * (g + 1)]
            q4t, do4t = _group_t(qt, g), _group_t(dot_, g)
            p, p_sink = _win_probs(kg, q4t, bias_ref, sink_ref, g, edge_ok)
            dp = _dot(vg, do4t, NN)
            delta = jnp.sum(p * dp, axis=0, keepdims=True)
            ds = p * (dp - delta)
            for j in range(GROUP):
                dbias_ref[GROUP * g + j] += ds[:, Q_BLOCK * j:Q_BLOCK * (j + 1)]
            dsink_acc[g:g + 1, :] += -p_sink * delta
            dsb = ds.astype(BF16)
            dq4t = _dot(kt[HD_B * g:HD_B * (g + 1), :], dsb, NN) * WIN_SCALE
            dq_parts += [dq4t[:, Q_BLOCK * j:Q_BLOCK * (j + 1)] for j in range(GROUP)]
            dks.append(_dot(dsb, q4t, NT))
            dvs.append(_dot(p.astype(BF16), do4t, NT))
        dq_ref[...] = jnp.concatenate(dq_parts, axis=0).T.astype(BF16)
        rows = pl.ds(pl.multiple_of(n * Q_BLOCK, Q_BLOCK), SPAN)
        dk_ref[rows, :] += jnp.concatenate(dks, axis=1)
        dv_ref[rows, :] += jnp.concatenate(dvs, axis=1)

        @pl.when(n == n_blk - 1)
        def _():
            acc = dsink_acc[...]
            dsink_ref[...] = jnp.concatenate(
                [jnp.sum(acc[:, Q_BLOCK * j:Q_BLOCK * (j + 1)], axis=1, keepdims=True) for j in range(GROUP)], axis=1)

    whole = lambda shape: pl.BlockSpec(shape, lambda n: (0,) * len(shape))
    return _pcall(
        body, name="win_bwd", grid=(n_blk,),
        in_specs=[qspec, *kspecs, *kspecs, bias_spec, sink_spec, pl.BlockSpec((Q_BLOCK, H_B * HD_B), lambda n: (n, 0)), ANY],
        out_specs=[qspec, whole((S + 2 * WINDOW, KV_B * HD_B)),
                   whole((S + 2 * WINDOW, KV_B * HD_B)), whole((H_B, SPAN, Q_BLOCK)), whole((KV_B, GROUP))],
        out_shape=[jax.ShapeDtypeStruct((S, PROJ_P), BF16), jax.ShapeDtypeStruct((S + 2 * WINDOW, KV_B * HD_B), F32),
                   jax.ShapeDtypeStruct((S + 2 * WINDOW, KV_B * HD_B), F32), jax.ShapeDtypeStruct((H_B, SPAN, Q_BLOCK), F32),
                   jax.ShapeDtypeStruct((KV_B, GROUP), F32)],
        scratch_shapes=[pltpu.VMEM((KV_B, GROUP * Q_BLOCK), F32)],
        dims=("arbitrary",), aliases={10: 0})(proj, kp, kp, kp, vp, vp, vp, bias_t, sinks_b, do_b, d_proj)


def _bias_table(rel_bias_t, onehot_t, in_band):
    def body(rb_ref, oh_ref, band_ref, o_ref):
        t = lax.dot_general(rb_ref[...], oh_ref[...], NN, preferred_element_type=F32, precision=lax.Precision.HIGHEST)
        o_ref[...] = jnp.where(band_ref[...] > 0.5, t, -1e30)

    n = onehot_t.shape[1]
    tn = _tile(n, 8192)
    return _pcall(body, name="bias_table", grid=(n // tn,),
                  in_specs=[pl.BlockSpec((H_B, NUM_BUCKETS), lambda j: (0, 0)), pl.BlockSpec((NUM_BUCKETS, tn), lambda j: (0, j)),
                            pl.BlockSpec((1, tn), lambda j: (0, j))],
                  out_specs=pl.BlockSpec((H_B, tn), lambda j: (0, j)),
                  out_shape=jax.ShapeDtypeStruct((H_B, n), F32), dims=("parallel",))(rel_bias_t, onehot_t, in_band)


def _bias_table_bwd(dbias, onehot_t):
    n = onehot_t.shape[1]
    tk = _tile(n, 8192)

    def body(d_ref, oh_ref, o_ref):
        @pl.when(pl.program_id(0) == 0)
        def _():
            o_ref[...] = jnp.zeros_like(o_ref)

        o_ref[...] += lax.dot_general(d_ref[...], oh_ref[...], NT, preferred_element_type=F32, precision=lax.Precision.HIGHEST)

    return _pcall(body, name="bias_table_bwd", grid=(n // tk,),
                  in_specs=[pl.BlockSpec((H_B, tk), lambda j: (0, j)), pl.BlockSpec((NUM_BUCKETS, tk), lambda j: (0, j))],
                  out_specs=pl.BlockSpec((H_B, NUM_BUCKETS), lambda j: (0, 0)),
                  out_shape=jax.ShapeDtypeStruct((H_B, NUM_BUCKETS), F32), dims=("arbitrary",))(dbias, onehot_t)


def _shift_down(u):
    row = lax.broadcasted_iota(jnp.int32, u.shape, 0)
    return jnp.where(row == 0, 0.0, pltpu.roll(u, 1, axis=0))


def _shift_up(u):
    n = u.shape[0]
    row = lax.broadcasted_iota(jnp.int32, u.shape, 0)
    return jnp.where(row == n - 1, 0.0, pltpu.roll(u, n - 1, axis=0))


def _conv(u, w, b):
    return _shift_down(u) * w[0:1, :] + u * w[1:2, :] + _shift_up(u) * w[2:3, :] + b


CONV_STRIP = 128
N_STRIPS = D_FF // CONV_STRIP


def _strip(rows, half):
    return pl.BlockSpec((rows, CONV_STRIP), lambda j: (0, j + half * N_STRIPS))


def _conv_gate_fwd(u, conv_w, conv_b, S):
    def body(ug_ref, uv_ref, wg_ref, wv_ref, bg_ref, bv_ref, a_ref):
        g = _conv(ug_ref[...].astype(F32), wg_ref[...], bg_ref[...])
        val = _conv(uv_ref[...].astype(F32), wv_ref[...], bv_ref[...])
        a_ref[...] = (g * _sigmoid(g) * val).astype(BF16)

    return _pcall(body, name="conv_gate_fwd", grid=(N_STRIPS,),
                  in_specs=[_strip(S, 0), _strip(S, 1), _strip(3, 0), _strip(3, 1), _strip(1, 0), _strip(1, 1)],
                  out_specs=_strip(S, 0), out_shape=jax.ShapeDtypeStruct((S, D_FF), BF16),
                  dims=("parallel",))(u, u, conv_w, conv_w, conv_b, conv_b)


def _conv_gate_bwd(u, conv_w, conv_b, da, S):
    def conv_bwd(duc, u_in, w):
        du = _shift_up(duc) * w[0:1, :] + duc * w[1:2, :] + _shift_down(duc) * w[2:3, :]
        dw = jnp.concatenate([jnp.sum(duc * _shift_down(u_in), axis=0, keepdims=True),
                              jnp.sum(duc * u_in, axis=0, keepdims=True),
                              jnp.sum(duc * _shift_up(u_in), axis=0, keepdims=True)], axis=0)
        return du, dw, jnp.sum(duc, axis=0, keepdims=True)

    def body(ug_ref, uv_ref, wg_ref, wv_ref, bg_ref, bv_ref, da_ref, dug_ref, duv_ref, dwg_ref, dwv_ref, dbg_ref, dbv_ref):
        u_g, u_v, w_g, w_v = ug_ref[...].astype(F32), uv_ref[...].astype(F32), wg_ref[...], wv_ref[...]
        g = _conv(u_g, w_g, bg_ref[...])
        val = _conv(u_v, w_v, bv_ref[...])
        da = da_ref[...].astype(F32)
        sg = _sigmoid(g)
        dval = da * (g * sg)
        dg = da * val * (sg * (1.0 + g * (1.0 - sg)))
        du_g, dwg_ref[...], dbg_ref[...] = conv_bwd(dg, u_g, w_g)
        du_v, dwv_ref[...], dbv_ref[...] = conv_bwd(dval, u_v, w_v)
        dug_ref[...] = du_g.astype(BF16)
        duv_ref[...] = du_v.astype(BF16)

    half = lambda r, dt: (_strip(r, 0), jax.ShapeDtypeStruct((r, D_FF), dt))
    outs = [half(S, BF16), half(S, BF16), half(3, F32), half(3, F32), half(1, F32), half(1, F32)]
    return _pcall(
        body, name="conv_gate_bwd", grid=(N_STRIPS,),
        in_specs=[_strip(S, 0), _strip(S, 1), _strip(3, 0), _strip(3, 1), _strip(1, 0), _strip(1, 1), _strip(S, 0)],
        out_specs=[o[0] for o in outs], out_shape=[o[1] for o in outs],
        dims=("parallel",))(u, u, conv_w, conv_w, conv_b, conv_b, da)


MESH = pl.DeviceIdType.MESH
ANY = pl.BlockSpec(memory_space=pl.ANY)


def _place():
    return lax.axis_index("x"), lax.axis_index("y"), lax.axis_index("c")


def _all_gather(shards):
    n_arr = len(shards)

    def body(*refs):
        ins, outs = refs[:n_arr], refs[n_arr:2 * n_arr]
        send_sems, recv_sems, local_sems = refs[2 * n_arr:]
        x, y, c = _place()
        me, sibling = (x, y, c), (x, y, 1 - c)
        chips = [(1 - x, y), (x, 1 - y), (1 - x, 1 - y)]

        def slot(a, p):
            return outs[a].at[4 * p[0] + 2 * p[1] + p[2]]

        def copy(a, k, block, to, src=None):
            return pltpu.make_async_remote_copy(
                src_ref=slot(a, block) if src is None else src, dst_ref=slot(a, block),
                send_sem=send_sems.at[a, k], recv_sem=recv_sems.at[a, k], device_id=to, device_id_type=MESH)

        mine = [pltpu.make_async_copy(ins[a], slot(a, me), local_sems.at[a]) for a in range(n_arr)]
        for cp in mine:
            cp.start()
        first = []
        for a in range(n_arr):
            first.append(copy(a, 0, me, sibling, src=ins[a]))
            first += [copy(a, 1 + j, me, (*chip, c), src=ins[a]) for j, chip in enumerate(chips)]
        for cp in first:
            cp.start()
        passed = []
        for j, chip in enumerate(chips):
            for a in range(n_arr):
                copy(a, 1 + j, (*chip, c), me).wait_recv()
                cp = copy(a, 4 + j, (*chip, c), sibling)
                cp.start()
                passed.append(cp)
        for a in range(n_arr):
            copy(a, 0, sibling, me).wait_recv()
            for j, chip in enumerate(chips):
                copy(a, 4 + j, (*chip, 1 - c), me).wait_recv()
        for cp in first + passed:
            cp.wait_send()
        for cp in mine:
            cp.wait()

    return pl.pallas_call(
        body, name="all_gather_weights",
        in_specs=[ANY] * n_arr, out_specs=[ANY] * n_arr,
        out_shape=[jax.ShapeDtypeStruct((N_DEV, *s.shape), s.dtype) for s in shards],
        scratch_shapes=[pltpu.SemaphoreType.DMA((n_arr, 7)), pltpu.SemaphoreType.DMA((n_arr, 7)), pltpu.SemaphoreType.DMA((n_arr,))],
    )(*shards)


def _xchg_out_shapes(stacked, replicated):
    return ([jax.ShapeDtypeStruct(s.shape, s.dtype) for s in stacked]
            + [jax.ShapeDtypeStruct((N_DEV, *r.shape), r.dtype) for r in replicated])


def _xchg_sems(n_arr):
    return [pltpu.SemaphoreType.DMA((n_arr, 7)), pltpu.SemaphoreType.DMA((n_arr, 7)), pltpu.SemaphoreType.DMA((n_arr,))]


def _xchg_copies(ins, outs, sems, n_st, with_recv):
    send_sems, recv_sems, local_sems = sems
    n_arr = len(ins)
    x, y, c = _place()
    me = 4 * x + 2 * y + c

    def src(a, idx):
        return ins[a].at[idx] if a < n_st else ins[a]

    mine = [pltpu.make_async_copy(src(a, me), outs[a].at[me], local_sems.at[a]) for a in range(n_arr)]
    pairs = []
    for k in range(1, N_DEV):
        px, py, pc = x ^ (k >> 2), y ^ ((k >> 1) & 1), c ^ (k & 1)
        peer = 4 * px + 2 * py + pc
        for a in range(n_arr):
            sems_k = dict(send_sem=send_sems.at[a, k - 1], recv_sem=recv_sems.at[a, k - 1], device_id_type=MESH)
            send = pltpu.make_async_remote_copy(src_ref=src(a, peer), dst_ref=outs[a].at[me], device_id=(px, py, pc), **sems_k)
            recv = None
            if with_recv:
                recv = pltpu.make_async_remote_copy(src_ref=src(a, peer), dst_ref=outs[a].at[peer], device_id=(x, y, c), **sems_k)
            pairs.append((send, recv))
    return mine, pairs


def _xchg_start(ins, outs, sems, n_st):
    mine, pairs = _xchg_copies(ins, outs, sems, n_st, False)
    for cp in mine:
        cp.start()
    for send, _ in pairs:
        send.start()


def _xchg_finish(ins, outs, sems, n_st):
    mine, pairs = _xchg_copies(ins, outs, sems, n_st, True)
    for _, recv in pairs:
        recv.wait_recv()
    for send, _ in pairs:
        send.wait_send()
    for cp in mine:
        cp.wait()


def _exchange(stacked, replicated, name):
    _, landed = _pcall(lambda: None, name=name, grid=(), in_specs=[], out_specs=[], out_shape=[], comm=(stacked, replicated))()
    return landed


def _adamw(parts, w, m, v, name):
    _, R, C = w.shape
    tr = R if R <= 512 else max(t for t in range(16, 513, 16) if R % t == 0)

    def body(p_ref, w_ref, m_ref, v_ref, g_ref, d_ref, nm_ref, nv_ref):
        g = p_ref[0].astype(F32)
        for s in range(1, N_DEV):
            g = g + p_ref[s].astype(F32)
        m2 = ADAM_B1 * m_ref[0] + (1.0 - ADAM_B1) * g
        v2 = ADAM_B2 * v_ref[0] + (1.0 - ADAM_B2) * (g * g)
        m_hat = m2 / (1.0 - ADAM_B1 ** ADAM_STEP)
        v_hat = v2 / (1.0 - ADAM_B2 ** ADAM_STEP)
        g_ref[0] = g
        d_ref[0] = -ADAM_LR * (m_hat / (jnp.sqrt(v_hat) + ADAM_EPS) + ADAM_WD * w_ref[0])
        nm_ref[0] = m2
        nv_ref[0] = v2

    blk = pl.BlockSpec((1, tr, C), lambda i: (0, i, 0))
    return _pcall(body, name=name, grid=(R // tr,),
                  in_specs=[pl.BlockSpec((N_DEV, tr, C), lambda i: (0, i, 0)), blk, blk, blk],
                  out_specs=[blk] * 4, out_shape=[jax.ShapeDtypeStruct((1, R, C), F32)] * 4,
                  dims=("parallel",))(parts, w, m, v)


def _t5_bucket(rel):
    nb = NUM_BUCKETS // 2
    max_exact = nb // 2
    base = (rel > 0).astype(jnp.int32) * nb
    n = jnp.abs(rel)
    nf = jnp.maximum(n, 1).astype(jnp.float32)
    large = max_exact + (jnp.log(nf / max_exact) / math.log(MAX_DISTANCE / max_exact) * (nb - max_exact)).astype(jnp.int32)
    large = jnp.minimum(large, nb - 1)
    return base + jnp.where(n < max_exact, n, large)


def _unstack_cols(g):
    return jnp.transpose(g, (1, 0, 2)).reshape(g.shape[1], N_DEV * g.shape[2])


def _stack_cols(w, n=N_DEV):
    R = w.shape[0]
    return jnp.transpose(w.reshape(R, n, w.shape[1] // n), (1, 0, 2))


def _stack_halves(g, v):
    return jnp.concatenate([_stack_cols(g, N_DEV // 2), _stack_cols(v, N_DEV // 2)], axis=0)


def kernel(x, positions, norm1_g, w_in, q_a_norm_g, w_q_b, kv_a_norm_g, w_kv_b, rel_bias, sinks, w_out, norm2_g, w_up, conv_w, conv_b, w_down, final_norm_g, loss_target, m_norm1_g, m_w_in, m_q_a_norm_g, m_w_q_b, m_kv_a_norm_g, m_w_kv_b, m_rel_bias, m_sinks, m_w_out, m_norm2_g, m_w_up, m_conv_w, m_conv_b, m_w_down, m_final_norm_g, v_norm1_g, v_w_in, v_q_a_norm_g, v_w_q_b, v_kv_a_norm_g, v_w_kv_b, v_rel_bias, v_sinks, v_w_out, v_norm2_g, v_w_up, v_conv_w, v_conv_b, v_w_down, v_final_norm_g):
    S = x.shape[1]
    x = x[0]
    target = loss_target[0]
    TM = 256

    tr = lambda w: jnp.swapaxes(w, 1, 2)
    g_in, g_qb, g_kvb = _all_gather([tr(w_in)[0].astype(BF16), tr(w_q_b)[0].astype(BF16), w_kv_b[0].astype(BF16)])
    late_weights = [w_out[0].astype(BF16), tr(w_up)[0].astype(BF16), conv_w[0]]
    wi = g_in.reshape(W_IN_COLS, D_MODEL)
    c0, c1, c2, c3, c4, c5 = (sum(W_IN_SIZES[:i + 1]) for i in range(6))
    w_in_pt = jnp.concatenate([wi[c4:c5], wi[c5:], wi[c1:c2], wi[:c0], wi[c2:c3], wi[c3:c4],
                               wi[c0:c0 + KV_LORA], wi[c0 + KV_LORA:c1], jnp.zeros((64, D_MODEL), BF16)], axis=0)
    wq = g_qb.reshape(H_A, QK_HEAD, Q_LORA)
    w_qb_pt = jnp.concatenate([wq[:, :QK_NOPE].reshape(H_A * QK_NOPE, Q_LORA), wq[:, QK_NOPE:].reshape(H_A * QK_ROPE, Q_LORA)], axis=0)
    w_kvb = _unstack_cols(g_kvb)

    half = QK_ROPE // 2
    inv_freq = ROPE_THETA ** (-jnp.arange(half, dtype=F32) / half)
    ang = positions.astype(F32)[:, None] * inv_freq[None, :]
    cos, sin = jnp.cos(ang), jnp.sin(ang)
    qa = jnp.arange(Q_BLOCK, dtype=jnp.int32)[:, None]
    kc = jnp.arange(SPAN, dtype=jnp.int32)[None, :]
    rel = (kc - WINDOW - qa).T
    in_band = (jnp.abs(rel) <= WINDOW).astype(F32).reshape(1, Q_BLOCK * SPAN)
    onehot_t = (_t5_bucket(rel).reshape(1, Q_BLOCK * SPAN) == jnp.arange(NUM_BUCKETS, dtype=jnp.int32)[:, None]).astype(F32)
    bias_t = _bias_table(rel_bias.T, onehot_t, in_band).reshape(H_B, SPAN, Q_BLOCK)
    sinks_b = jnp.broadcast_to(sinks.reshape(H_B, 1), (H_B, Q_BLOCK))

    (h1,) = _rowwise(lambda a, g: (_rms(a, g),), "norm1", S, TM, [_rows(x), _whole(norm1_g)], [("rows", D_MODEL, BF16)])
    proj = _matmul(h1, w_in_pt, "nt", BF16, "proj")

    def lat_fn(qlat, ckv, kr, gq, gkv, cs, sn):
        r1, r2 = _rope(kr[:, :half], kr[:, half:QK_ROPE], cs, sn)
        return _rms(qlat, gq), _rms(ckv, gkv), jnp.concatenate([r1, r2], axis=1)

    qn, ckvn, k_rope = _rowwise(lat_fn, "latents", S, TM,
                                [_rows(proj, 256, PROJ_QLAT), _rows(proj, 128, PROJ_CKV), _rows(proj, 128, PROJ_KROPE),
                                 _whole(q_a_norm_g), _whole(kv_a_norm_g), _rows(cos), _rows(sin)],
                                [("rows", Q_LORA, BF16), ("rows", KV_LORA, BF16), ("rows", QK_ROPE, BF16)])
    q_p = _matmul(qn, w_qb_pt, "nt", F32, "q_up", tn=1536)
    kv = _matmul(ckvn, w_kvb, "nn", BF16, "kv_up", tn=2048)

    def q_heads_fn(q, cs, sn):
        q = q * MLA_PRESCALE
        outs = []
        for h in range(H_A):
            o = H_A * QK_NOPE + QK_ROPE * h
            r1, r2 = _rope(q[:, o:o + half], q[:, o + half:o + QK_ROPE], cs, sn)
            outs.append(jnp.concatenate([q[:, QK_NOPE * h:QK_NOPE * (h + 1)], r1, r2], axis=1)[None])
        return (jnp.concatenate(outs, axis=0),)

    (q_full,) = _rowwise(q_heads_fn, "q_heads", S, TM, [_rows(q_p), _rows(cos), _rows(sin)], [("heads", H_A, QK_HEAD, BF16)])

    def k_heads_fn(kvb, kr):
        return (jnp.concatenate([jnp.concatenate([kvb[:, 256 * h:256 * h + QK_NOPE], kr], axis=1)[None] for h in range(H_A)], axis=0),)

    (k_full,) = _rowwise(k_heads_fn, "k_heads", S, TM, [_rows(kv), _rows(k_rope)], [("heads", H_A, QK_HEAD, BF16)], upcast=False)
    (o_a, lse), (g_out, g_up, g_cw) = _mla_fwd(q_full, k_full, kv, S, comm=([], late_weights))
    w_out_f = g_out.reshape(D_MODEL, D_MODEL)
    w_up_t = g_up.reshape(2 * D_FF, D_MODEL)
    conv_w_f = _unstack_cols(g_cw)

    kp = jnp.pad(proj[:, 256 * PROJ_KB:256 * (PROJ_KB + 1)], ((WINDOW, WINDOW), (0, 0)))
    vp = jnp.pad(proj[:, 256 * PROJ_VB:256 * (PROJ_VB + 1)], ((WINDOW, WINDOW), (0, 0)))
    o_b, (g_down,) = _win_fwd(proj, kp, vp, bias_t, sinks_b, S, comm=([], [w_down[0].astype(BF16)]))
    w_down_f = g_down.reshape(D_FF, D_MODEL)

    (mixed,) = _rowwise(lambda ga, gb, oa, ob: (_sigmoid(ga) * oa + _sigmoid(gb) * ob,), "gate_mix", S, TM,
                        [_rows(proj, 1024, PROJ_GA), _rows(proj, 1024, PROJ_GB), _rows(o_a), _rows(o_b)], [("rows", D_MODEL, BF16)])
    x1, h2 = _matmul(mixed, w_out_f, "nn", None, "out_proj", residual=x, tm=512,
                     epi=(lambda a, g: (a, _rms(a, g)), [_whole(norm2_g)], [("rows", D_MODEL, F32), ("rows", D_MODEL, BF16)]))
    u = _matmul(h2, w_up_t, "nt", BF16, "ffn_up", tn=1408)
    act = _conv_gate_fwd(u, conv_w_f, conv_b, S)

    def final_fn(a, g, t):
        err = _rms(a, g) - t
        loss = 0.5 * jnp.sum(jnp.mean(err * err, axis=-1, keepdims=True), axis=0, keepdims=True)
        dx, dg = _rms_bwd(err * (1.0 / D_MODEL), a, g)
        return dx, dx, dg, jnp.broadcast_to(loss, (1, 128))

    gfin = final_norm_g.reshape(1, D_MODEL)
    dx2, dx2_b, d_gfin, loss_row = _matmul(
        act, w_down_f, "nn", None, "ffn_down_loss", residual=x1, tm=512,
        epi=(final_fn, [_whole(gfin), _rows(target)],
             [("rows", D_MODEL, F32), ("rows", D_MODEL, BF16), ("acc", 1, D_MODEL), ("acc", 1, 128)]))
    d_act = _matmul(dx2_b, w_down_f, "nt", BF16, "ffn_down_dx", tn=1408)
    d_w_down = _matmul(act, dx2_b, "tn", F32, "ffn_down_dw")
    du_g, du_v, dcw_g, dcw_v, dcb_g, dcb_v = _conv_gate_bwd(u, conv_w_f, conv_b, d_act, S)
    d_conv_b = jnp.concatenate([dcb_g, dcb_v], axis=1)

    def norm_bwd_fn(dh, a, g, dres):
        dx, dg = _rms_bwd(dh, a, g)
        dx = dx + dres
        return dx, dx, dg

    dx1, dx1_b, d_g2 = _matmul(du_g, w_up_t, "nn", None, "ffn_up_dx_norm2_bwd", tm=256, a2=du_v,
                               epi=(norm_bwd_fn, [_rows(x1), _whole(norm2_g), _rows(dx2)],
                                    [("rows", D_MODEL, F32), ("rows", D_MODEL, BF16), ("acc", 1, D_MODEL)]))
    d_w_up_g = _matmul(du_g, h2, "tn", F32, "ffn_up_dw_gate", tm=256)
    d_w_up_v = _matmul(du_v, h2, "tn", F32, "ffn_up_dw_value", tm=256)
    d_w_out = _matmul(mixed, dx1_b, "tn", F32, "out_proj_dw", tm=512)

    def gate_bwd_fn(dm, ga, gb, oa, ob):
        sa, sb = _sigmoid(ga), _sigmoid(gb)
        return jnp.concatenate([dm * oa * sa * (1.0 - sa), dm * ob * sb * (1.0 - sb)], axis=1), dm * sa, dm * sb

    d_proj, do_a, do_b = _matmul(
        dx1_b, w_out_f, "nt", None, "out_proj_dx_gate_bwd", tm=512,
        epi=(gate_bwd_fn, [_rows(proj, 1024, PROJ_GA), _rows(proj, 1024, PROJ_GB), _rows(o_a), _rows(o_b)],
             [("cols", 2 * D_MODEL, 0, PROJ_P, BF16), ("rows", D_MODEL, BF16), ("rows", D_MODEL, BF16)]))

    d_proj, dk_acc, dv_acc, d_bias, d_sinks_g = _win_bwd(proj, kp, vp, bias_t, sinks_b, do_b, d_proj, S)
    d_rel_bias = _bias_table_bwd(d_bias.reshape(H_B, Q_BLOCK * SPAN), onehot_t).T
    d_sinks = d_sinks_g.reshape(1, H_B)

    d_w_up_t = jnp.concatenate([d_w_up_g, d_w_up_v], axis=0).reshape(N_DEV, 2 * D_FF // N_DEV, D_MODEL)
    early = [d_w_out.reshape(N_DEV, D_MODEL // N_DEV, D_MODEL).astype(BF16), d_w_up_t.astype(BF16),
             d_w_down.reshape(N_DEV, D_FF // N_DEV, D_MODEL).astype(BF16), _stack_halves(dcw_g, dcw_v)]
    (dq_full, dk_full, dv_full), recv_early = _mla_bwd(q_full, k_full, kv, do_a, o_a, lse, S, comm=(early, []))

    def dq_post_fn(dq, cs, sn):
        nope = [dq[h, :, :QK_NOPE] for h in range(H_A)]
        rope = []
        for h in range(H_A):
            rope += list(_rope_bwd(dq[h, :, QK_NOPE:QK_NOPE + half], dq[h, :, QK_NOPE + half:], cs, sn))
        return (jnp.concatenate(nope + rope, axis=1),)

    (dq_p,) = _rowwise(dq_post_fn, "dq_post", S, TM, [_heads(dq_full), _rows(cos), _rows(sin)], [("rows", H_A * QK_HEAD, BF16)])

    def dkv_post_fn(dk, dv, cs, sn):
        dk = dk * math.log(2.0)
        dkv = jnp.concatenate([jnp.concatenate([dk[h, :, :QK_NOPE], dv[h]], axis=1) for h in range(H_A)], axis=1)
        dkr = dk[0, :, QK_NOPE:]
        for h in range(1, H_A):
            dkr = dkr + dk[h, :, QK_NOPE:]
        r1, r2 = _rope_bwd(dkr[:, :half], dkr[:, half:], cs, sn)
        return dkv, jnp.concatenate([r1, r2], axis=1)

    dkv, d_krope = _rowwise(dkv_post_fn, "dkv_post", S, TM, [_heads(dk_full), _heads(dv_full), _rows(cos), _rows(sin)],
                            [("rows", H_A * (QK_NOPE + V_DIM), BF16), ("rows", QK_ROPE, F32)])
    d_qn = _matmul(dq_p, w_qb_pt, "nn", F32, "q_up_dx")
    d_w_qb_pt = _matmul(dq_p, qn, "tn", F32, "q_up_dw", tm=512)
    d_ckvn = _matmul(dkv, w_kvb, "nt", F32, "kv_up_dx")
    d_w_kvb = _matmul(ckvn, dkv, "tn", F32, "kv_up_dw", tn=2048)

    def lat_bwd_fn(dqn, dckvn, dkr, qlat, ckv, gq, gkv, dkb, dvb):
        dql, dgq = _rms_bwd(dqn, qlat, gq)
        dck, dgkv = _rms_bwd(dckvn, ckv, gkv)
        tail = jnp.concatenate([dql, dkb, dvb, dck, dkr, jnp.zeros_like(dkr)], axis=1)
        return tail, dgq, dgkv

    shifted = lambda arr: (arr, lambda tm: pl.BlockSpec((tm, arr.shape[1]), lambda i, *_: (i + WINDOW // tm, 0)))
    TL = min(128, S)
    d_proj, d_gq, d_gkv = _rowwise(lat_bwd_fn, "latents_bwd", S, TL,
                                   [_rows(d_qn), _rows(d_ckvn), _rows(d_krope), _rows(proj, 256, PROJ_QLAT), _rows(proj, 128, PROJ_CKV),
                                    _whole(q_a_norm_g), _whole(kv_a_norm_g), shifted(dk_acc), shifted(dv_acc)],
                                   [("cols", 1024, 3, PROJ_P, BF16), ("acc", 1, Q_LORA), ("acc", 1, KV_LORA)], into=(d_proj, 0))
    dp = _matmul(d_proj, h1, "tn", F32, "proj_dw", tm=512)

    d_w_in_t = jnp.concatenate([dp[3072:3328], dp[3840:3968], dp[3968:4032], dp[2048:3072], dp[3328:3584],
                                dp[3584:3840], dp[0:1024], dp[1024:2048]], axis=0)
    d_w_qb_t = jnp.concatenate([d_w_qb_pt[:H_A * QK_NOPE].reshape(H_A, QK_NOPE, Q_LORA),
                                d_w_qb_pt[H_A * QK_NOPE:].reshape(H_A, QK_ROPE, Q_LORA)], axis=1)
    late = [d_w_in_t.reshape(N_DEV, W_IN_COLS // N_DEV, D_MODEL).astype(BF16), d_w_qb_t.astype(BF16), _stack_cols(d_w_kvb).astype(BF16)]
    def norm1_bwd_fn(dh, a, g, dres):
        dx, dg = _rms_bwd(dh, a, g)
        return dx + dres, dg

    (grad_x, d_g1), recv_late = _matmul(
        d_proj, w_in_pt, "nn", None, "proj_dx_norm1_bwd", tm=512, comm=(late, []),
        epi=(norm1_bwd_fn, [_rows(x), _whole(norm1_g), _rows(dx1)], [("rows", D_MODEL, F32), ("acc", 1, D_MODEL)]))
    recv = [*recv_late, *recv_early]

    small_parts = [d_g1, d_gq, d_gkv, d_rel_bias.reshape(1, NUM_BUCKETS * H_B), d_sinks, d_g2, d_conv_b, d_gfin]
    small = jnp.concatenate(small_parts, axis=1)
    n_small = small.shape[1]
    pad = (-n_small) % 128
    small = jnp.pad(small, ((0, 0), (0, pad)))
    (recv_small,) = _exchange([], [small], "exchange_small_grads")

    def flat(a):
        return a.reshape(1, -1)

    small_w = [norm1_g, q_a_norm_g, kv_a_norm_g, rel_bias, sinks, norm2_g, conv_b, final_norm_g]
    small_m = [m_norm1_g, m_q_a_norm_g, m_kv_a_norm_g, m_rel_bias, m_sinks, m_norm2_g, m_conv_b, m_final_norm_g]
    small_v = [v_norm1_g, v_q_a_norm_g, v_kv_a_norm_g, v_rel_bias, v_sinks, v_norm2_g, v_conv_b, v_final_norm_g]
    cat = lambda parts: jnp.pad(jnp.concatenate([flat(a) for a in parts], axis=1), ((0, 0), (0, pad)))[None]
    sm = _adamw(recv_small, cat(small_w), cat(small_m), cat(small_v), "adamw_small")
    big_names = ["w_in", "w_q_b", "w_kv_b", "w_out", "w_up", "w_down", "conv_w"]
    transposed = ("w_in", "w_q_b", "w_up")
    big_w = [tr(w_in), tr(w_q_b), w_kv_b, w_out, tr(w_up), w_down, conv_w]
    big_m = [tr(m_w_in), tr(m_w_q_b), m_w_kv_b, m_w_out, tr(m_w_up), m_w_down, m_conv_w]
    big_v = [tr(v_w_in), tr(v_w_q_b), v_w_kv_b, v_w_out, tr(v_w_up), v_w_down, v_conv_w]
    big = {n: _adamw(r, w, m, v, "adamw_" + n) for n, r, w, m, v in zip(big_names, recv, big_w, big_m, big_v)}

    loss = lax.psum(loss_row[0, 0], ("x", "y", "c"))
    order = ["norm1_g", "w_in", "q_a_norm_g", "w_q_b", "kv_a_norm_g", "w_kv_b", "rel_bias", "sinks", "w_out", "norm2_g", "w_up",
             "conv_w", "conv_b", "w_down", "final_norm_g"]
    small_names = ["norm1_g", "q_a_norm_g", "kv_a_norm_g", "rel_bias", "sinks", "norm2_g", "conv_b", "final_norm_g"]
    offs, o = {}, 0
    for n, a in zip(small_names, small_w):
        offs[n] = (o, a.size, a.shape)
        o += a.size
    outs = [loss, grad_x[None]]
    for kind in range(4):
        for n in order:
            if n in big:
                outs.append(tr(big[n][kind]) if n in transposed else big[n][kind])
            else:
                o, size, shape = offs[n]
                outs.append(sm[kind][0, 0, o:o + size].reshape(shape))
    return tuple(outs)
```

```python
import math

import jax
import jax.numpy as jnp
from jax import lax
from jax.experimental import pallas as pl
from jax.experimental.pallas import tpu as pltpu

F32 = jnp.float32
BF16 = jnp.bfloat16

N_DEV = 8
D_MODEL = 1024
EPS = 1e-6
H_A, QK_NOPE, QK_ROPE, V_DIM, Q_LORA, KV_LORA = 8, 128, 64, 128, 256, 128
QK_HEAD = QK_NOPE + QK_ROPE
ROPE_THETA = 10000.0
H_B, KV_B, GROUP, HD_B, WINDOW, Q_BLOCK = 16, 4, 4, 64, 128, 128
SPAN = Q_BLOCK + 2 * WINDOW
NUM_BUCKETS, MAX_DISTANCE = 32, 128
D_FF = 2816
ADAM_LR, ADAM_B1, ADAM_B2, ADAM_EPS, ADAM_WD, ADAM_STEP = 0.001, 0.9, 0.999, 1e-08, 0.01, 10

W_IN_SIZES = (Q_LORA, KV_LORA + QK_ROPE, H_B * HD_B, KV_B * HD_B, KV_B * HD_B, D_MODEL, D_MODEL)
W_IN_COLS = sum(W_IN_SIZES)
PROJ_P = 4096
PROJ_GA, PROJ_GB, PROJ_QB, PROJ_QLAT, PROJ_KB, PROJ_VB, PROJ_CKV, PROJ_KROPE = 0, 1, 2, 12, 13, 14, 30, 31

VMEM_LIMIT = 56 * 1024 * 1024

NN = (((1,), (0,)), ((), ()))
NT = (((1,), (1,)), ((), ()))
TN = (((0,), (0,)), ((), ()))


def _pcall(body, *, name, grid, in_specs, out_specs, out_shape, scratch_shapes=(), dims=None, comm=None, aliases=None):
    if comm is None:
        params = pltpu.CompilerParams(dimension_semantics=dims, vmem_limit_bytes=VMEM_LIMIT)
        return pl.pallas_call(body, name=name, grid=grid, in_specs=in_specs, out_specs=out_specs, out_shape=out_shape,
                              scratch_shapes=list(scratch_shapes), input_output_aliases=aliases or {}, compiler_params=params)
    assert not aliases
    stacked, replicated = comm
    arrs = [*stacked, *replicated]
    n_st, n_arr = len(stacked), len(arrs)
    single = not isinstance(out_specs, (list, tuple))
    o_specs, o_shape = ([out_specs], [out_shape]) if single else (list(out_specs), list(out_shape))
    n_in, n_out = len(in_specs), len(o_specs)

    def wrapped(*refs):
        c_in = refs[n_in:n_in + n_arr]
        c_out = refs[n_in + n_arr + n_out:n_in + 2 * n_arr + n_out]
        sems = refs[len(refs) - 3:]
        own = (*refs[:n_in], *refs[n_in + n_arr:n_in + n_arr + n_out], *refs[n_in + 2 * n_arr + n_out:len(refs) - 3])
        if not grid:
            _xchg_start(c_in, c_out, sems, n_st)
            _xchg_finish(c_in, c_out, sems, n_st)
            return
        first = last = None
        for d, n in enumerate(grid):
            pid = pl.program_id(d)
            first = (pid == 0) if first is None else first & (pid == 0)
            last = (pid == n - 1) if last is None else last & (pid == n - 1)

        @pl.when(first)
        def _():
            _xchg_start(c_in, c_out, sems, n_st)

        body(*own)

        @pl.when(last)
        def _():
            _xchg_finish(c_in, c_out, sems, n_st)

    params = pltpu.CompilerParams(dimension_semantics=("arbitrary",) * len(grid), vmem_limit_bytes=VMEM_LIMIT)
    call = pl.pallas_call(wrapped, name=name, grid=grid, in_specs=[*in_specs, *[ANY] * n_arr], out_specs=[*o_specs, *[ANY] * n_arr],
                          out_shape=[*o_shape, *_xchg_out_shapes(stacked, replicated)],
                          scratch_shapes=[*scratch_shapes, *_xchg_sems(n_arr)], compiler_params=params)

    def run(*args):
        res = call(*args, *arrs)
        outs, landed = res[:n_out], res[n_out:]
        return (outs[0] if single else outs), landed

    return run


def _dot(a, b, dn):
    return lax.dot_general(a, b, dn, preferred_element_type=F32)


def _tile(n, target):
    best = None
    for t in range(128, min(n, target) + 1, 128):
        if n % t == 0:
            best = t
    return n if best is None else best


def _matmul(a, b, mode, out_dtype, name, residual=None, tm=1024, tn=1024, comm=None, a2=None, epi=None):
    if mode == "nn":
        (M, K), N = a.shape, b.shape[1]
    elif mode == "nt":
        (M, K), N = a.shape, b.shape[0]
    else:
        (K, M), N = a.shape, b.shape[1]
    tm, tn = _tile(M, tm), _tile(N, tn)
    a_spec = pl.BlockSpec((K, tm), lambda i, j: (0, i)) if mode == "tn" else pl.BlockSpec((tm, K), lambda i, j: (i, 0))
    b_spec = pl.BlockSpec((tn, b.shape[1]), lambda i, j: (j, 0)) if mode == "nt" else pl.BlockSpec((K, tn), lambda i, j: (0, j))
    o_spec = pl.BlockSpec((tm, tn), lambda i, j: (i, j))
    in_specs, args = [a_spec, b_spec], [a, b]
    if a2 is not None:
        assert (mode == "nt" and K + a2.shape[1] == b.shape[1]) or (mode == "nn" and K + a2.shape[1] == b.shape[0])
        if mode == "nn":
            b_spec = in_specs[1] = pl.BlockSpec((b.shape[0], tn), lambda i, j: (0, j))
        in_specs.append(pl.BlockSpec((tm, a2.shape[1]), lambda i, j: (i, 0)))
        args.append(a2)
    if residual is not None:
        in_specs.append(o_spec)
        args.append(residual)
    n_mm = len(args)
    scratch = [pltpu.VMEM((tm, K), a.dtype)] if mode == "tn" else []
    if epi is None:
        out_specs, out_shape, is_acc = o_spec, jax.ShapeDtypeStruct((M, N), out_dtype), None
    else:
        assert tn == N
        fn, epi_ins, epi_outs = epi
        in_specs += [mk(tm) for _, mk in epi_ins]
        args += [arr for arr, _ in epi_ins]
        out_specs, out_shape, is_acc = _row_out_specs(epi_outs, M, tm)

    def body(*refs):
        a_ref, b_ref = refs[0], refs[1]
        n_out = 1 if epi is None else len(is_acc)
        out_refs = refs[len(args):len(args) + n_out]
        if mode == "tn":
            at_ref = refs[len(args) + n_out]

            @pl.when(pl.program_id(1) == 0)
            def _():
                at_ref[...] = a_ref[...].T

            acc = _dot(at_ref[...], b_ref[...], NN)
        elif a2 is not None and mode == "nt":
            acc = _dot(a_ref[...], b_ref[:, :K], NT) + _dot(refs[2][...], b_ref[:, K:], NT)
        elif a2 is not None:
            acc = _dot(a_ref[...], b_ref[:K, :], NN) + _dot(refs[2][...], b_ref[K:, :], NN)
        else:
            acc = _dot(a_ref[...], b_ref[...], NT if mode == "nt" else NN)
        if residual is not None:
            acc = acc + refs[n_mm - 1][...]
        if epi is None:
            out_refs[0][...] = acc.astype(out_dtype)
        else:
            _store_rows(out_refs, fn(acc, *[_load_f32(r) for r in refs[n_mm:len(args)]]), is_acc)

    return _pcall(body, name=name, grid=(M // tm, N // tn), in_specs=in_specs, out_specs=out_specs,
                  out_shape=out_shape, scratch_shapes=scratch,
                  dims=("arbitrary" if epi is not None else "parallel", "arbitrary"), comm=comm)(*args)


def _rows(arr, width=None, col=0):
    width = arr.shape[1] if width is None else width
    return (arr, lambda tm: pl.BlockSpec((tm, width), lambda i, *_: (i, col)))


def _heads(arr):
    return (arr, lambda tm: pl.BlockSpec((arr.shape[0], tm, arr.shape[2]), lambda i, *_: (0, i, 0)))


def _whole(arr):
    nd = arr.ndim
    return (arr, lambda tm: pl.BlockSpec(arr.shape, lambda i, *_: (0,) * nd))


def _row_out_specs(outs, n_rows, tm):
    out_specs, out_shape, is_acc = [], [], []
    for o in outs:
        if o[0] == "rows":
            out_specs.append(pl.BlockSpec((tm, o[1]), lambda i, *_: (i, 0)))
            out_shape.append(jax.ShapeDtypeStruct((n_rows, o[1]), o[2]))
        elif o[0] == "cols":
            out_specs.append(pl.BlockSpec((tm, o[1]), lambda i, *_, c=o[2]: (i, c)))
            out_shape.append(jax.ShapeDtypeStruct((n_rows, o[3]), o[4]))
        elif o[0] == "heads":
            out_specs.append(pl.BlockSpec((o[1], tm, o[2]), lambda i, *_: (0, i, 0)))
            out_shape.append(jax.ShapeDtypeStruct((o[1], n_rows, o[2]), o[3]))
        else:
            out_specs.append(pl.BlockSpec((o[1], o[2]), lambda i, *_: (0, 0)))
            out_shape.append(jax.ShapeDtypeStruct((o[1], o[2]), F32))
        is_acc.append(o[0] == "acc")
    return out_specs, out_shape, is_acc


def _load_f32(r):
    v = r[...]
    return v.astype(F32) if v.dtype == BF16 else v


def _store_rows(out_refs, vals, is_acc):
    for r, v, acc in zip(out_refs, vals, is_acc):
        if acc:
            @pl.when(pl.program_id(0) == 0)
            def _():
                r[...] = jnp.zeros_like(r)

            r[...] += v
        else:
            r[...] = v.astype(r.dtype)


def _rowwise(fn, name, n_rows, tm, ins, outs, upcast=True, into=None):
    tm = min(tm, n_rows)
    assert n_rows % tm == 0
    in_specs = [mk(tm) for _, mk in ins]
    out_specs, out_shape, is_acc = _row_out_specs(outs, n_rows, tm)
    n_in = len(ins)
    args = [a for a, _ in ins]
    aliases = {}
    if into is not None:
        in_specs.append(ANY)
        args.append(into[0])
        aliases = {n_in: into[1]}

    def body(*refs):
        vals = fn(*[_load_f32(r) if upcast else r[...] for r in refs[:n_in]])
        _store_rows(refs[len(args):], vals, is_acc)

    return _pcall(body, name=name, grid=(n_rows // tm,), in_specs=in_specs, out_specs=out_specs,
                  out_shape=out_shape, dims=("arbitrary",), aliases=aliases)(*args)


def _rms(x, g):
    r = lax.rsqrt(jnp.mean(x * x, axis=-1, keepdims=True) + EPS)
    return x * r * g


def _rms_bwd(dy, x, g):
    r = lax.rsqrt(jnp.mean(x * x, axis=-1, keepdims=True) + EPS)
    xhat = x * r
    dxhat = dy * g
    dx = r * (dxhat - xhat * jnp.mean(dxhat * xhat, axis=-1, keepdims=True))
    return dx, jnp.sum(dy * xhat, axis=0, keepdims=True)


def _rope(x1, x2, cos, sin):
    return x1 * cos - x2 * sin, x2 * cos + x1 * sin


def _rope_bwd(d1, d2, cos, sin):
    return d1 * cos + d2 * sin, d2 * cos - d1 * sin


def _sigmoid(x):
    return 1.0 / (1.0 + jnp.exp(-x))


MLA_SCALE = 1.0 / math.sqrt(QK_HEAD)
MLA_PRESCALE = MLA_SCALE * math.log2(math.e)
MLA_TQ, MLA_KC = 1024, 1024


def _mla_fwd(q_full, k_full, kv, S, comm=None):
    tq, kc = min(MLA_TQ, S), min(MLA_KC, S)

    def body(q_ref, k_ref, v_ref, o_ref, lse_ref):
        q = q_ref[0]
        m = jnp.full((tq, 1), -1e30, F32)
        l = jnp.zeros((tq, 1), F32)
        acc = jnp.zeros((tq, V_DIM), F32)
        for c in range(S // kc):
            s = _dot(q, k_ref[0, c * kc:(c + 1) * kc, :], NT)
            m_new = jnp.maximum(m, jnp.max(s, axis=-1, keepdims=True))
            alpha = jnp.exp2(m - m_new)
            p = jnp.exp2(s - m_new)
            l = alpha * l + jnp.sum(p, axis=-1, keepdims=True)
            acc = alpha * acc + _dot(p.astype(BF16), v_ref[c * kc:(c + 1) * kc, :], NN)
            m = m_new
        o_ref[...] = (acc / l).astype(BF16)
        lse_ref[0] = m + jnp.log2(l)

    return _pcall(
        body, name="mla_fwd", grid=(H_A, S // tq),
        in_specs=[pl.BlockSpec((1, tq, QK_HEAD), lambda h, i: (h, i, 0)),
                  pl.BlockSpec((1, S, QK_HEAD), lambda h, i: (h, 0, 0)),
                  pl.BlockSpec((S, V_DIM), lambda h, i: (0, 2 * h + 1))],
        out_specs=[pl.BlockSpec((tq, V_DIM), lambda h, i: (i, h)),
                   pl.BlockSpec((1, tq, 1), lambda h, i: (h, i, 0))],
        out_shape=[jax.ShapeDtypeStruct((S, H_A * V_DIM), BF16), jax.ShapeDtypeStruct((H_A, S, 1), F32)],
        dims=("parallel", "parallel"), comm=comm)(q_full, k_full, kv)


def _mla_bwd(q_full, k_full, kv, do_a, o_a, lse, S, comm=None):
    tq, kc = min(MLA_TQ, S), min(MLA_KC, S)

    def body(q_ref, k_ref, v_ref, do_ref, o_ref, lse_ref, dq_ref, dk_ref, dv_ref):
        @pl.when(pl.program_id(1) == 0)
        def _():
            dk_ref[...] = jnp.zeros_like(dk_ref)
            dv_ref[...] = jnp.zeros_like(dv_ref)

        q = q_ref[0]
        do = do_ref[...]
        lse_q = lse_ref[0]
        delta = jnp.sum(do.astype(F32) * o_ref[...].astype(F32), axis=-1, keepdims=True)
        dq = jnp.zeros((tq, QK_HEAD), F32)
        for c in range(S // kc):
            k = k_ref[0, c * kc:(c + 1) * kc, :]
            v = v_ref[c * kc:(c + 1) * kc, :]
            p = jnp.exp2(_dot(q, k, NT) - lse_q)
            ds = (p * (_dot(do, v, NT) - delta)).astype(BF16)
            dq = dq + _dot(ds, k, NN)
            dk_ref[0, c * kc:(c + 1) * kc, :] += _dot(ds, q, TN)
            dv_ref[0, c * kc:(c + 1) * kc, :] += _dot(p.astype(BF16), do, TN)
        dq_ref[0] = dq * MLA_SCALE

    return _pcall(
        body, name="mla_bwd", grid=(H_A, S // tq),
        in_specs=[pl.BlockSpec((1, tq, QK_HEAD), lambda h, i: (h, i, 0)),
                  pl.BlockSpec((1, S, QK_HEAD), lambda h, i: (h, 0, 0)),
                  pl.BlockSpec((S, V_DIM), lambda h, i: (0, 2 * h + 1)),
                  pl.BlockSpec((tq, V_DIM), lambda h, i: (i, h)),
                  pl.BlockSpec((tq, V_DIM), lambda h, i: (i, h)),
                  pl.BlockSpec((1, tq, 1), lambda h, i: (h, i, 0))],
        out_specs=[pl.BlockSpec((1, tq, QK_HEAD), lambda h, i: (h, i, 0)),
                   pl.BlockSpec((1, S, QK_HEAD), lambda h, i: (h, 0, 0)),
                   pl.BlockSpec((1, S, V_DIM), lambda h, i: (h, 0, 0))],
        out_shape=[jax.ShapeDtypeStruct((H_A, S, QK_HEAD), F32), jax.ShapeDtypeStruct((H_A, S, QK_HEAD), F32),
                   jax.ShapeDtypeStruct((H_A, S, V_DIM), F32)],
        dims=("parallel", "arbitrary"), comm=comm)(q_full, k_full, kv, do_a, o_a, lse)


WIN_SCALE = 1.0 / math.sqrt(HD_B)


def _win_specs(S):
    qspec = pl.BlockSpec((Q_BLOCK, H_B * HD_B), lambda n: (n, PROJ_QB))
    kspecs = [pl.BlockSpec((Q_BLOCK, KV_B * HD_B), lambda n, d=d: (n + d, 0)) for d in range(3)]
    bias_spec = pl.BlockSpec((H_B, SPAN, Q_BLOCK), lambda n: (0, 0, 0))
    sink_spec = pl.BlockSpec((H_B, Q_BLOCK), lambda n: (0, 0))
    return qspec, kspecs, bias_spec, sink_spec


def _win_edge_ok(n, n_blk):
    row = lax.broadcasted_iota(jnp.int32, (SPAN, 1), 0)
    return jnp.logical_not(((n == 0) & (row < WINDOW)) | ((n == n_blk - 1) & (row >= SPAN - WINDOW)))


def _lanes4(pieces):
    return jnp.concatenate(pieces, axis=1)


def _win_probs(kg, q4t, bias_ref, sink_ref, g, edge_ok):
    bias4 = _lanes4([bias_ref[GROUP * g + j] for j in range(GROUP)])
    sink4 = _lanes4([sink_ref[GROUP * g + j:GROUP * g + j + 1, :] for j in range(GROUP)])
    s = jnp.where(edge_ok, _dot(kg, q4t, NN) + bias4, -1e30)
    m = jnp.maximum(jnp.max(s, axis=0, keepdims=True), sink4)
    p = jnp.exp(s - m)
    e_sink = jnp.exp(sink4 - m)
    inv_l = 1.0 / (jnp.sum(p, axis=0, keepdims=True) + e_sink)
    return p * inv_l, e_sink * inv_l


def _group_t(xt, g):
    return _lanes4([xt[HD_B * (GROUP * g + j):HD_B * (GROUP * g + j + 1), :] for j in range(GROUP)])


def _win_fwd(proj, kp, vp, bias_t, sinks_b, S, comm=None):
    n_blk = S // Q_BLOCK
    qspec, kspecs, bias_spec, sink_spec = _win_specs(S)

    def body(q_ref, k0, k1, k2, v0, v1, v2, bias_ref, sink_ref, o_ref):
        n = pl.program_id(0)
        edge_ok = _win_edge_ok(n, n_blk)
        k = jnp.concatenate([k0[...], k1[...], k2[...]], axis=0)
        vt = jnp.concatenate([v0[...], v1[...], v2[...]], axis=0).T
        qt = (q_ref[...].astype(F32) * WIN_SCALE).T.astype(BF16)
        parts = []
        for g in range(KV_B):
            p, _ = _win_probs(k[:, HD_B * g:HD_B * (g + 1)], _group_t(qt, g), bias_ref, sink_ref, g, edge_ok)
            o4t = _dot(vt[HD_B * g:HD_B * (g + 1), :], p.astype(BF16), NN)
            parts += [o4t[:, Q_BLOCK * j:Q_BLOCK * (j + 1)] for j in range(GROUP)]
        o_ref[...] = jnp.concatenate(parts, axis=0).T.astype(BF16)

    return _pcall(body, name="win_fwd", grid=(n_blk,),
                  in_specs=[qspec, *kspecs, *kspecs, bias_spec, sink_spec],
                  out_specs=pl.BlockSpec((Q_BLOCK, H_B * HD_B), lambda n: (n, 0)),
                  out_shape=jax.ShapeDtypeStruct((S, H_B * HD_B), BF16),
                  dims=("parallel",), comm=comm)(proj, kp, kp, kp, vp, vp, vp, bias_t, sinks_b)


def _win_bwd(proj, kp, vp, bias_t, sinks_b, do_b, d_proj, S):
    n_blk = S // Q_BLOCK
    qspec, kspecs, bias_spec, sink_spec = _win_specs(S)

    def body(q_ref, k0, k1, k2, v0, v1, v2, bias_ref, sink_ref, do_ref, _, dq_ref, dk_ref, dv_ref, dbias_ref, dsink_ref, dsink_acc):
        n = pl.program_id(0)

        @pl.when(n == 0)
        def _():
            dk_ref[...] = jnp.zeros_like(dk_ref)
            dv_ref[...] = jnp.zeros_like(dv_ref)
            dbias_ref[...] = jnp.zeros_like(dbias_ref)
            dsink_acc[...] = jnp.zeros_like(dsink_acc)

        edge_ok = _win_edge_ok(n, n_blk)
        k = jnp.concatenate([k0[...], k1[...], k2[...]], axis=0)
        v = jnp.concatenate([v0[...], v1[...], v2[...]], axis=0)
        kt = k.T
        qt = (q_ref[...].astype(F32) * WIN_SCALE).T.astype(BF16)
        dot_ = do_ref[...].astype(F32).T.astype(BF16)
        dq_parts, dks, dvs = [], [], []
        for g in range(KV_B):
            kg, vg = k[:, HD_B * g:HD_B * (g + 1)], v[:, HD_B * g:HD_B * (g + 1)]
            q4t, do4t = _group_t(qt, g), _group_t(dot_, g)
            p, p_sink = _win_probs(kg, q4t, bias_ref, sink_ref, g, edge_ok)
            dp = _dot(vg, do4t, NN)
            delta = jnp.sum(p * dp, axis=0, keepdims=True)
            ds = p * (dp - delta)
            for j in range(GROUP):
                dbias_ref[GROUP * g + j] += ds[:, Q_BLOCK * j:Q_BLOCK * (j + 1)]
            dsink_acc[g:g + 1, :] += -p_sink * delta
            dsb = ds.astype(BF16)
            dq4t = _dot(kt[HD_B * g:HD_B * (g + 1), :], dsb, NN) * WIN_SCALE
            dq_parts += [dq4t[:, Q_BLOCK * j:Q_BLOCK * (j + 1)] for j in range(GROUP)]
            dks.append(_dot(dsb, q4t, NT))
            dvs.append(_dot(p.astype(BF16), do4t, NT))
        dq_ref[...] = jnp.concatenate(dq_parts, axis=0).T.astype(BF16)
        rows = pl.ds(pl.multiple_of(n * Q_BLOCK, Q_BLOCK), SPAN)
        dk_ref[rows, :] += jnp.concatenate(dks, axis=1)
        dv_ref[rows, :] += jnp.concatenate(dvs, axis=1)

        @pl.when(n == n_blk - 1)
        def _():
            acc = dsink_acc[...]
            dsink_ref[...] = jnp.concatenate(
                [jnp.sum(acc[:, Q_BLOCK * j:Q_BLOCK * (j + 1)], axis=1, keepdims=True) for j in range(GROUP)], axis=1)

    whole = lambda shape: pl.BlockSpec(shape, lambda n: (0,) * len(shape))
    return _pcall(
        body, name="win_bwd", grid=(n_blk,),
        in_specs=[qspec, *kspecs, *kspecs, bias_spec, sink_spec, pl.BlockSpec((Q_BLOCK, H_B * HD_B), lambda n: (n, 0)), ANY],
        out_specs=[qspec, whole((S + 2 * WINDOW, KV_B * HD_B)),
                   whole((S + 2 * WINDOW, KV_B * HD_B)), whole((H_B, SPAN, Q_BLOCK)), whole((KV_B, GROUP))],
        out_shape=[jax.ShapeDtypeStruct((S, PROJ_P), BF16), jax.ShapeDtypeStruct((S + 2 * WINDOW, KV_B * HD_B), F32),
                   jax.ShapeDtypeStruct((S + 2 * WINDOW, KV_B * HD_B), F32), jax.ShapeDtypeStruct((H_B, SPAN, Q_BLOCK), F32),
                   jax.ShapeDtypeStruct((KV_B, GROUP), F32)],
        scratch_shapes=[pltpu.VMEM((KV_B, GROUP * Q_BLOCK), F32)],
        dims=("arbitrary",), aliases={10: 0})(proj, kp, kp, kp, vp, vp, vp, bias_t, sinks_b, do_b, d_proj)


def _bias_table(rel_bias_t, onehot_t, in_band):
    def body(rb_ref, oh_ref, band_ref, o_ref):
        t = lax.dot_general(rb_ref[...], oh_ref[...], NN, preferred_element_type=F32, precision=lax.Precision.HIGHEST)
        o_ref[...] = jnp.where(band_ref[...] > 0.5, t, -1e30)

    n = onehot_t.shape[1]
    tn = _tile(n, 8192)
    return _pcall(body, name="bias_table", grid=(n // tn,),
                  in_specs=[pl.BlockSpec((H_B, NUM_BUCKETS), lambda j: (0, 0)), pl.BlockSpec((NUM_BUCKETS, tn), lambda j: (0, j)),
                            pl.BlockSpec((1, tn), lambda j: (0, j))],
                  out_specs=pl.BlockSpec((H_B, tn), lambda j: (0, j)),
                  out_shape=jax.ShapeDtypeStruct((H_B, n), F32), dims=("parallel",))(rel_bias_t, onehot_t, in_band)


def _bias_table_bwd(dbias, onehot_t):
    n = onehot_t.shape[1]
    tk = _tile(n, 8192)

    def body(d_ref, oh_ref, o_ref):
        @pl.when(pl.program_id(0) == 0)
        def _():
            o_ref[...] = jnp.zeros_like(o_ref)

        o_ref[...] += lax.dot_general(d_ref[...], oh_ref[...], NT, preferred_element_type=F32, precision=lax.Precision.HIGHEST)

    return _pcall(body, name="bias_table_bwd", grid=(n // tk,),
                  in_specs=[pl.BlockSpec((H_B, tk), lambda j: (0, j)), pl.BlockSpec((NUM_BUCKETS, tk), lambda j: (0, j))],
                  out_specs=pl.BlockSpec((H_B, NUM_BUCKETS), lambda j: (0, 0)),
                  out_shape=jax.ShapeDtypeStruct((H_B, NUM_BUCKETS), F32), dims=("arbitrary",))(dbias, onehot_t)


CONV_STRIP = 128
N_STRIPS = D_FF // CONV_STRIP
CONV_ROWS = 128
HALO = 8


def _strip(rows, half):
    return pl.BlockSpec((rows, CONV_STRIP), lambda j: (0, j + half * N_STRIPS))


def _fill_padded(pad_ref, src_ref, halo, S):
    pad_ref[0:halo, :] = jnp.zeros((halo, CONV_STRIP), F32)
    pad_ref[halo + S:2 * halo + S, :] = jnp.zeros((halo, CONV_STRIP), F32)
    pad_ref[halo:halo + S, :] = src_ref[...].astype(F32)


def _taps(ext, n):
    m = ext.shape[0]
    return pltpu.roll(ext, 1, axis=0)[HALO:HALO + n], ext[HALO:HALO + n], pltpu.roll(ext, m - 1, axis=0)[HALO:HALO + n]


def _conv_gate_fwd(u, conv_w, conv_b, S):
    R = min(CONV_ROWS, S)

    def body(ug_ref, uv_ref, wg_ref, wv_ref, bg_ref, bv_ref, a_ref, gpad, vpad):
        _fill_padded(gpad, ug_ref, HALO, S)
        _fill_padded(vpad, uv_ref, HALO, S)
        wg, wv, bg, bv = wg_ref[...], wv_ref[...], bg_ref[...], bv_ref[...]

        def conv(pad_ref, r0, w, b):
            dn, mid, up = (pad_ref[pl.ds(r0 + HALO + d, R), :] for d in (-1, 0, 1))
            return dn * w[0:1, :] + mid * w[1:2, :] + up * w[2:3, :] + b

        def step(c, carry):
            r0 = pl.multiple_of(c * R, R)
            g = conv(gpad, r0, wg, bg)
            val = conv(vpad, r0, wv, bv)
            a_ref[pl.ds(r0, R), :] = (g * _sigmoid(g) * val).astype(BF16)
            return carry

        lax.fori_loop(0, S // R, step, 0)

    return _pcall(body, name="conv_gate_fwd", grid=(N_STRIPS,),
                  in_specs=[_strip(S, 0), _strip(S, 1), _strip(3, 0), _strip(3, 1), _strip(1, 0), _strip(1, 1)],
                  out_specs=_strip(S, 0), out_shape=jax.ShapeDtypeStruct((S, D_FF), BF16),
                  scratch_shapes=[pltpu.VMEM((S + 2 * HALO, CONV_STRIP), F32)] * 2,
                  dims=("parallel",))(u, u, conv_w, conv_w, conv_b, conv_b)


def _conv_gate_bwd(u, conv_w, conv_b, da, S):
    R = min(CONV_ROWS, S)
    n = R + 2 * HALO

    def body(ug_ref, uv_ref, wg_ref, wv_ref, bg_ref, bv_ref, da_ref, dug_ref, duv_ref, dwg_ref, dwv_ref, dbg_ref, dbv_ref,
             gpad, vpad, dapad):
        _fill_padded(gpad, ug_ref, 2 * HALO, S)
        _fill_padded(vpad, uv_ref, 2 * HALO, S)
        _fill_padded(dapad, da_ref, HALO, S)
        wg, wv, bg, bv = wg_ref[...], wv_ref[...], bg_ref[...], bv_ref[...]

        def conv(pad_ref, r0, w, b):
            dn, mid, up = (pad_ref[pl.ds(r0 + HALO + d, n), :] for d in (-1, 0, 1))
            return dn * w[0:1, :] + mid * w[1:2, :] + up * w[2:3, :] + b, mid[HALO:HALO + R]

        def conv_bwd(duc, u_mid, w, r0, du_ref):
            dn, mid, up = _taps(duc, R)
            du_ref[pl.ds(r0, R), :] = (up * w[0:1, :] + mid * w[1:2, :] + dn * w[2:3, :]).astype(BF16)
            dw = jnp.concatenate([jnp.sum(up * u_mid, axis=0, keepdims=True), jnp.sum(mid * u_mid, axis=0, keepdims=True),
                                  jnp.sum(dn * u_mid, axis=0, keepdims=True)], axis=0)
            return dw, jnp.sum(mid, axis=0, keepdims=True)

        def step(c, carry):
            dw_g, db_g, dw_v, db_v = carry
            r0 = pl.multiple_of(c * R, R)
            g, ug_mid = conv(gpad, r0, wg, bg)
            val, uv_mid = conv(vpad, r0, wv, bv)
            da_ext = dapad[pl.ds(r0, n), :]
            sg = _sigmoid(g)
            ddw_v, ddb_v = conv_bwd(da_ext * (g * sg), uv_mid, wv, r0, duv_ref)
            ddw_g, ddb_g = conv_bwd(da_ext * val * (sg * (1.0 + g * (1.0 - sg))), ug_mid, wg, r0, dug_ref)
            return dw_g + ddw_g, db_g + ddb_g, dw_v + ddw_v, db_v + ddb_v

        z3, z1 = jnp.zeros((3, CONV_STRIP), F32), jnp.zeros((1, CONV_STRIP), F32)
        dwg_ref[...], dbg_ref[...], dwv_ref[...], dbv_ref[...] = lax.fori_loop(0, S // R, step, (z3, z1, z3, z1))

    half = lambda r, dt: (_strip(r, 0), jax.ShapeDtypeStruct((r, D_FF), dt))
    outs = [half(S, BF16), half(S, BF16), half(3, F32), half(3, F32), half(1, F32), half(1, F32)]
    return _pcall(
        body, name="conv_gate_bwd", grid=(N_STRIPS,),
        in_specs=[_strip(S, 0), _strip(S, 1), _strip(3, 0), _strip(3, 1), _strip(1, 0), _strip(1, 1), _strip(S, 0)],
        out_specs=[o[0] for o in outs], out_shape=[o[1] for o in outs],
        scratch_shapes=[pltpu.VMEM((S + 4 * HALO, CONV_STRIP), F32)] * 2 + [pltpu.VMEM((S + 2 * HALO, CONV_STRIP), F32)],
        dims=("parallel",))(u, u, conv_w, conv_w, conv_b, conv_b, da)


MESH = pl.DeviceIdType.MESH
ANY = pl.BlockSpec(memory_space=pl.ANY)


def _place():
    return lax.axis_index("x"), lax.axis_index("y"), lax.axis_index("c")


def _all_gather(shards):
    n_arr = len(shards)

    def body(*refs):
        ins, outs = refs[:n_arr], refs[n_arr:2 * n_arr]
        send_sems, recv_sems, local_sems = refs[2 * n_arr:]
        x, y, c = _place()
        me, sibling = (x, y, c), (x, y, 1 - c)
        chips = [(1 - x, y), (x, 1 - y), (1 - x, 1 - y)]

        def slot(a, p):
            return outs[a].at[4 * p[0] + 2 * p[1] + p[2]]

        def copy(a, k, block, to, src=None):
            return pltpu.make_async_remote_copy(
                src_ref=slot(a, block) if src is None else src, dst_ref=slot(a, block),
                send_sem=send_sems.at[a, k], recv_sem=recv_sems.at[a, k], device_id=to, device_id_type=MESH)

        mine = [pltpu.make_async_copy(ins[a], slot(a, me), local_sems.at[a]) for a in range(n_arr)]
        for cp in mine:
            cp.start()
        first = []
        for a in range(n_arr):
            first.append(copy(a, 0, me, sibling, src=ins[a]))
            first += [copy(a, 1 + j, me, (*chip, c), src=ins[a]) for j, chip in enumerate(chips)]
        for cp in first:
            cp.start()
        passed = []
        for j, chip in enumerate(chips):
            for a in range(n_arr):
                copy(a, 1 + j, (*chip, c), me).wait_recv()
                cp = copy(a, 4 + j, (*chip, c), sibling)
                cp.start()
                passed.append(cp)
        for a in range(n_arr):
            copy(a, 0, sibling, me).wait_recv()
            for j, chip in enumerate(chips):
                copy(a, 4 + j, (*chip, 1 - c), me).wait_recv()
        for cp in first + passed:
            cp.wait_send()
        for cp in mine:
            cp.wait()

    return pl.pallas_call(
        body, name="all_gather_weights",
        in_specs=[ANY] * n_arr, out_specs=[ANY] * n_arr,
        out_shape=[jax.ShapeDtypeStruct((N_DEV, *s.shape), s.dtype) for s in shards],
        scratch_shapes=[pltpu.SemaphoreType.DMA((n_arr, 7)), pltpu.SemaphoreType.DMA((n_arr, 7)), pltpu.SemaphoreType.DMA((n_arr,))],
    )(*shards)


def _xchg_out_shapes(stacked, replicated):
    return ([jax.ShapeDtypeStruct(s.shape, s.dtype) for s in stacked]
            + [jax.ShapeDtypeStruct((N_DEV, *r.shape), r.dtype) for r in replicated])


def _xchg_sems(n_arr):
    return [pltpu.SemaphoreType.DMA((n_arr, 7)), pltpu.SemaphoreType.DMA((n_arr, 7)), pltpu.SemaphoreType.DMA((n_arr,))]


def _xchg_copies(ins, outs, sems, n_st, with_recv):
    send_sems, recv_sems, local_sems = sems
    n_arr = len(ins)
    x, y, c = _place()
    me = 4 * x + 2 * y + c

    def src(a, idx):
        return ins[a].at[idx] if a < n_st else ins[a]

    mine = [pltpu.make_async_copy(src(a, me), outs[a].at[me], local_sems.at[a]) for a in range(n_arr)]
    pairs = []
    for k in range(1, N_DEV):
        px, py, pc = x ^ (k >> 2), y ^ ((k >> 1) & 1), c ^ (k & 1)
        peer = 4 * px + 2 * py + pc
        for a in range(n_arr):
            sems_k = dict(send_sem=send_sems.at[a, k - 1], recv_sem=recv_sems.at[a, k - 1], device_id_type=MESH)
            send = pltpu.make_async_remote_copy(src_ref=src(a, peer), dst_ref=outs[a].at[me], device_id=(px, py, pc), **sems_k)
            recv = None
            if with_recv:
                recv = pltpu.make_async_remote_copy(src_ref=src(a, peer), dst_ref=outs[a].at[peer], device_id=(x, y, c), **sems_k)
            pairs.append((send, recv))
    return mine, pairs


def _xchg_start(ins, outs, sems, n_st):
    mine, pairs = _xchg_copies(ins, outs, sems, n_st, False)
    for cp in mine:
        cp.start()
    for send, _ in pairs:
        send.start()


def _xchg_finish(ins, outs, sems, n_st):
    mine, pairs = _xchg_copies(ins, outs, sems, n_st, True)
    for _, recv in pairs:
        recv.wait_recv()
    for send, _ in pairs:
        send.wait_send()
    for cp in mine:
        cp.wait()


def _exchange(stacked, replicated, name):
    _, landed = _pcall(lambda: None, name=name, grid=(), in_specs=[], out_specs=[], out_shape=[], comm=(stacked, replicated))()
    return landed


def _adamw(parts, w, m, v, name):
    _, R, C = w.shape
    tr = R if R <= 512 else max(t for t in range(16, 513, 16) if R % t == 0)

    def body(p_ref, w_ref, m_ref, v_ref, g_ref, d_ref, nm_ref, nv_ref):
        g = p_ref[0].astype(F32)
        for s in range(1, N_DEV):
            g = g + p_ref[s].astype(F32)
        m2 = ADAM_B1 * m_ref[0] + (1.0 - ADAM_B1) * g
        v2 = ADAM_B2 * v_ref[0] + (1.0 - ADAM_B2) * (g * g)
        m_hat = m2 / (1.0 - ADAM_B1 ** ADAM_STEP)
        v_hat = v2 / (1.0 - ADAM_B2 ** ADAM_STEP)
        g_ref[0] = g
        d_ref[0] = -ADAM_LR * (m_hat / (jnp.sqrt(v_hat) + ADAM_EPS) + ADAM_WD * w_ref[0])
        nm_ref[0] = m2
        nv_ref[0] = v2

    blk = pl.BlockSpec((1, tr, C), lambda i: (0, i, 0))
    return _pcall(body, name=name, grid=(R // tr,),
                  in_specs=[pl.BlockSpec((N_DEV, tr, C), lambda i: (0, i, 0)), blk, blk, blk],
                  out_specs=[blk] * 4, out_shape=[jax.ShapeDtypeStruct((1, R, C), F32)] * 4,
                  dims=("parallel",))(parts, w, m, v)


def _t5_bucket(rel):
    nb = NUM_BUCKETS // 2
    max_exact = nb // 2
    base = (rel > 0).astype(jnp.int32) * nb
    n = jnp.abs(rel)
    nf = jnp.maximum(n, 1).astype(jnp.float32)
    large = max_exact + (jnp.log(nf / max_exact) / math.log(MAX_DISTANCE / max_exact) * (nb - max_exact)).astype(jnp.int32)
    large = jnp.minimum(large, nb - 1)
    return base + jnp.where(n < max_exact, n, large)


def _unstack_cols(g):
    return jnp.transpose(g, (1, 0, 2)).reshape(g.shape[1], N_DEV * g.shape[2])


def _stack_cols(w, n=N_DEV):
    R = w.shape[0]
    return jnp.transpose(w.reshape(R, n, w.shape[1] // n), (1, 0, 2))


def _stack_halves(g, v):
    return jnp.concatenate([_stack_cols(g, N_DEV // 2), _stack_cols(v, N_DEV // 2)], axis=0)


def kernel(x, positions, norm1_g, w_in, q_a_norm_g, w_q_b, kv_a_norm_g, w_kv_b, rel_bias, sinks, w_out, norm2_g, w_up, conv_w, conv_b, w_down, final_norm_g, loss_target, m_norm1_g, m_w_in, m_q_a_norm_g, m_w_q_b, m_kv_a_norm_g, m_w_kv_b, m_rel_bias, m_sinks, m_w_out, m_norm2_g, m_w_up, m_conv_w, m_conv_b, m_w_down, m_final_norm_g, v_norm1_g, v_w_in, v_q_a_norm_g, v_w_q_b, v_kv_a_norm_g, v_w_kv_b, v_rel_bias, v_sinks, v_w_out, v_norm2_g, v_w_up, v_conv_w, v_conv_b, v_w_down, v_final_norm_g):
    S = x.shape[1]
    x = x[0]
    target = loss_target[0]
    TM = 256

    tr = lambda w: jnp.swapaxes(w, 1, 2)
    g_in, g_qb, g_kvb = _all_gather([tr(w_in)[0].astype(BF16), tr(w_q_b)[0].astype(BF16), w_kv_b[0].astype(BF16)])
    late_weights = [w_out[0].astype(BF16), tr(w_up)[0].astype(BF16), conv_w[0]]
    wi = g_in.reshape(W_IN_COLS, D_MODEL)
    c0, c1, c2, c3, c4, c5 = (sum(W_IN_SIZES[:i + 1]) for i in range(6))
    w_in_pt = jnp.concatenate([wi[c4:c5], wi[c5:], wi[c1:c2], wi[:c0], wi[c2:c3], wi[c3:c4],
                               wi[c0:c0 + KV_LORA], wi[c0 + KV_LORA:c1], jnp.zeros((64, D_MODEL), BF16)], axis=0)
    wq = g_qb.reshape(H_A, QK_HEAD, Q_LORA)
    w_qb_pt = jnp.concatenate([wq[:, :QK_NOPE].reshape(H_A * QK_NOPE, Q_LORA), wq[:, QK_NOPE:].reshape(H_A * QK_ROPE, Q_LORA)], axis=0)
    w_kvb = _unstack_cols(g_kvb)

    half = QK_ROPE // 2
    inv_freq = ROPE_THETA ** (-jnp.arange(half, dtype=F32) / half)
    ang = positions.astype(F32)[:, None] * inv_freq[None, :]
    cos, sin = jnp.cos(ang), jnp.sin(ang)
    qa = jnp.arange(Q_BLOCK, dtype=jnp.int32)[:, None]
    kc = jnp.arange(SPAN, dtype=jnp.int32)[None, :]
    rel = (kc - WINDOW - qa).T
    in_band = (jnp.abs(rel) <= WINDOW).astype(F32).reshape(1, Q_BLOCK * SPAN)
    onehot_t = (_t5_bucket(rel).reshape(1, Q_BLOCK * SPAN) == jnp.arange(NUM_BUCKETS, dtype=jnp.int32)[:, None]).astype(F32)
    bias_t = _bias_table(rel_bias.T, onehot_t, in_band).reshape(H_B, SPAN, Q_BLOCK)
    sinks_b = jnp.broadcast_to(sinks.reshape(H_B, 1), (H_B, Q_BLOCK))

    (h1,) = _rowwise(lambda a, g: (_rms(a, g),), "norm1", S, TM, [_rows(x), _whole(norm1_g)], [("rows", D_MODEL, BF16)])
    proj = _matmul(h1, w_in_pt, "nt", BF16, "proj")

    def lat_fn(qlat, ckv, kr, gq, gkv, cs, sn):
        r1, r2 = _rope(kr[:, :half], kr[:, half:QK_ROPE], cs, sn)
        return _rms(qlat, gq), _rms(ckv, gkv), jnp.concatenate([r1, r2], axis=1)

    qn, ckvn, k_rope = _rowwise(lat_fn, "latents", S, TM,
                                [_rows(proj, 256, PROJ_QLAT), _rows(proj, 128, PROJ_CKV), _rows(proj, 128, PROJ_KROPE),
                                 _whole(q_a_norm_g), _whole(kv_a_norm_g), _rows(cos), _rows(sin)],
                                [("rows", Q_LORA, BF16), ("rows", KV_LORA, BF16), ("rows", QK_ROPE, BF16)])
    def q_heads_fn(q, cs, sn):
        q = q * MLA_PRESCALE
        outs = []
        for h in range(H_A):
            o = H_A * QK_NOPE + QK_ROPE * h
            r1, r2 = _rope(q[:, o:o + half], q[:, o + half:o + QK_ROPE], cs, sn)
            outs.append(jnp.concatenate([q[:, QK_NOPE * h:QK_NOPE * (h + 1)], r1, r2], axis=1)[None])
        return (jnp.concatenate(outs, axis=0),)

    (q_full,) = _matmul(qn, w_qb_pt, "nt", None, "q_up_heads", tm=512, tn=1536,
                        epi=(q_heads_fn, [_rows(cos), _rows(sin)], [("heads", H_A, QK_HEAD, BF16)]))

    def k_heads_fn(kvf, kr):
        return kvf, jnp.concatenate([jnp.concatenate([kvf[:, 256 * h:256 * h + QK_NOPE], kr], axis=1)[None] for h in range(H_A)], axis=0)

    kv, k_full = _matmul(ckvn, w_kvb, "nn", None, "kv_up_heads", tm=512, tn=2048,
                         epi=(k_heads_fn, [_rows(k_rope)], [("rows", H_A * (QK_NOPE + V_DIM), BF16), ("heads", H_A, QK_HEAD, BF16)]))
    (o_a, lse), (g_out, g_up, g_cw) = _mla_fwd(q_full, k_full, kv, S, comm=([], late_weights))
    w_out_f = g_out.reshape(D_MODEL, D_MODEL)
    w_up_t = g_up.reshape(2 * D_FF, D_MODEL)
    conv_w_f = _unstack_cols(g_cw)

    kp = jnp.pad(proj[:, 256 * PROJ_KB:256 * (PROJ_KB + 1)], ((WINDOW, WINDOW), (0, 0)))
    vp = jnp.pad(proj[:, 256 * PROJ_VB:256 * (PROJ_VB + 1)], ((WINDOW, WINDOW), (0, 0)))
    o_b, (g_down,) = _win_fwd(proj, kp, vp, bias_t, sinks_b, S, comm=([], [w_down[0].astype(BF16)]))
    w_down_f = g_down.reshape(D_FF, D_MODEL)

    (mixed,) = _rowwise(lambda ga, gb, oa, ob: (_sigmoid(ga) * oa + _sigmoid(gb) * ob,), "gate_mix", S, TM,
                        [_rows(proj, 1024, PROJ_GA), _rows(proj, 1024, PROJ_GB), _rows(o_a), _rows(o_b)], [("rows", D_MODEL, BF16)])
    x1, h2 = _matmul(mixed, w_out_f, "nn", None, "out_proj", residual=x, tm=512,
                     epi=(lambda a, g: (a, _rms(a, g)), [_whole(norm2_g)], [("rows", D_MODEL, F32), ("rows", D_MODEL, BF16)]))
    u = _matmul(h2, w_up_t, "nt", BF16, "ffn_up", tn=1408)
    act = _conv_gate_fwd(u, conv_w_f, conv_b, S)

    def final_fn(a, g, t):
        err = _rms(a, g) - t
        loss = 0.5 * jnp.sum(jnp.mean(err * err, axis=-1, keepdims=True), axis=0, keepdims=True)
        dx, dg = _rms_bwd(err * (1.0 / D_MODEL), a, g)
        return dx, dx, dg, jnp.broadcast_to(loss, (1, 128))

    gfin = final_norm_g.reshape(1, D_MODEL)
    dx2, dx2_b, d_gfin, loss_row = _matmul(
        act, w_down_f, "nn", None, "ffn_down_loss", residual=x1, tm=512,
        epi=(final_fn, [_whole(gfin), _rows(target)],
             [("rows", D_MODEL, F32), ("rows", D_MODEL, BF16), ("acc", 1, D_MODEL), ("acc", 1, 128)]))
    d_act = _matmul(dx2_b, w_down_f, "nt", BF16, "ffn_down_dx", tn=1408)
    d_w_down = _matmul(act, dx2_b, "tn", F32, "ffn_down_dw")
    du_g, du_v, dcw_g, dcw_v, dcb_g, dcb_v = _conv_gate_bwd(u, conv_w_f, conv_b, d_act, S)
    d_conv_b = jnp.concatenate([dcb_g, dcb_v], axis=1)

    def norm_bwd_fn(dh, a, g, dres):
        dx, dg = _rms_bwd(dh, a, g)
        dx = dx + dres
        return dx, dx, dg

    dx1, dx1_b, d_g2 = _matmul(du_g, w_up_t, "nn", None, "ffn_up_dx_norm2_bwd", tm=256, a2=du_v,
                               epi=(norm_bwd_fn, [_rows(x1), _whole(norm2_g), _rows(dx2)],
                                    [("rows", D_MODEL, F32), ("rows", D_MODEL, BF16), ("acc", 1, D_MODEL)]))
    d_w_up_g = _matmul(du_g, h2, "tn", F32, "ffn_up_dw_gate", tm=256)
    d_w_up_v = _matmul(du_v, h2, "tn", F32, "ffn_up_dw_value", tm=256)
    d_w_out = _matmul(mixed, dx1_b, "tn", F32, "out_proj_dw", tm=512)

    def gate_bwd_fn(dm, ga, gb, oa, ob):
        sa, sb = _sigmoid(ga), _sigmoid(gb)
        return jnp.concatenate([dm * oa * sa * (1.0 - sa), dm * ob * sb * (1.0 - sb)], axis=1), dm * sa, dm * sb

    d_proj, do_a, do_b = _matmul(
        dx1_b, w_out_f, "nt", None, "out_proj_dx_gate_bwd", tm=512,
        epi=(gate_bwd_fn, [_rows(proj, 1024, PROJ_GA), _rows(proj, 1024, PROJ_GB), _rows(o_a), _rows(o_b)],
             [("cols", 2 * D_MODEL, 0, PROJ_P, BF16), ("rows", D_MODEL, BF16), ("rows", D_MODEL, BF16)]))

    d_proj, dk_acc, dv_acc, d_bias, d_sinks_g = _win_bwd(proj, kp, vp, bias_t, sinks_b, do_b, d_proj, S)
    d_rel_bias = _bias_table_bwd(d_bias.reshape(H_B, Q_BLOCK * SPAN), onehot_t).T
    d_sinks = d_sinks_g.reshape(1, H_B)

    d_w_up_t = jnp.concatenate([d_w_up_g, d_w_up_v], axis=0).reshape(N_DEV, 2 * D_FF // N_DEV, D_MODEL)
    early = [d_w_out.reshape(N_DEV, D_MODEL // N_DEV, D_MODEL).astype(BF16), d_w_up_t.astype(BF16),
             d_w_down.reshape(N_DEV, D_FF // N_DEV, D_MODEL).astype(BF16), _stack_halves(dcw_g, dcw_v)]
    (dq_full, dk_full, dv_full), recv_early = _mla_bwd(q_full, k_full, kv, do_a, o_a, lse, S, comm=(early, []))

    def dq_post_fn(dq, cs, sn):
        nope = [dq[h, :, :QK_NOPE] for h in range(H_A)]
        rope = []
        for h in range(H_A):
            rope += list(_rope_bwd(dq[h, :, QK_NOPE:QK_NOPE + half], dq[h, :, QK_NOPE + half:], cs, sn))
        return (jnp.concatenate(nope + rope, axis=1),)

    (dq_p,) = _rowwise(dq_post_fn, "dq_post", S, TM, [_heads(dq_full), _rows(cos), _rows(sin)], [("rows", H_A * QK_HEAD, BF16)])

    def dkv_post_fn(dk, dv, cs, sn):
        dk = dk * math.log(2.0)
        dkv = jnp.concatenate([jnp.concatenate([dk[h, :, :QK_NOPE], dv[h]], axis=1) for h in range(H_A)], axis=1)
        dkr = dk[0, :, QK_NOPE:]
        for h in range(1, H_A):
            dkr = dkr + dk[h, :, QK_NOPE:]
        r1, r2 = _rope_bwd(dkr[:, :half], dkr[:, half:], cs, sn)
        return dkv, jnp.concatenate([r1, r2], axis=1)

    dkv, d_krope = _rowwise(dkv_post_fn, "dkv_post", S, TM, [_heads(dk_full), _heads(dv_full), _rows(cos), _rows(sin)],
                            [("rows", H_A * (QK_NOPE + V_DIM), BF16), ("rows", QK_ROPE, F32)])
    d_qn = _matmul(dq_p, w_qb_pt, "nn", F32, "q_up_dx")
    d_w_qb_pt = _matmul(dq_p, qn, "tn", F32, "q_up_dw", tm=512)
    d_ckvn = _matmul(dkv, w_kvb, "nt", F32, "kv_up_dx")
    d_w_kvb = _matmul(ckvn, dkv, "tn", F32, "kv_up_dw", tn=2048)

    def lat_bwd_fn(dqn, dckvn, dkr, qlat, ckv, gq, gkv, dkb, dvb):
        dql, dgq = _rms_bwd(dqn, qlat, gq)
        dck, dgkv = _rms_bwd(dckvn, ckv, gkv)
        tail = jnp.concatenate([dql, dkb, dvb, dck, dkr, jnp.zeros_like(dkr)], axis=1)
        return tail, dgq, dgkv

    shifted = lambda arr: (arr, lambda tm: pl.BlockSpec((tm, arr.shape[1]), lambda i, *_: (i + WINDOW // tm, 0)))
    TL = min(128, S)
    d_proj, d_gq, d_gkv = _rowwise(lat_bwd_fn, "latents_bwd", S, TL,
                                   [_rows(d_qn), _rows(d_ckvn), _rows(d_krope), _rows(proj, 256, PROJ_QLAT), _rows(proj, 128, PROJ_CKV),
                                    _whole(q_a_norm_g), _whole(kv_a_norm_g), shifted(dk_acc), shifted(dv_acc)],
                                   [("cols", 1024, 3, PROJ_P, BF16), ("acc", 1, Q_LORA), ("acc", 1, KV_LORA)], into=(d_proj, 0))
    dp = _matmul(d_proj, h1, "tn", F32, "proj_dw", tm=512)

    d_w_in_t = jnp.concatenate([dp[3072:3328], dp[3840:3968], dp[3968:4032], dp[2048:3072], dp[3328:3584],
                                dp[3584:3840], dp[0:1024], dp[1024:2048]], axis=0)
    d_w_qb_t = jnp.concatenate([d_w_qb_pt[:H_A * QK_NOPE].reshape(H_A, QK_NOPE, Q_LORA),
                                d_w_qb_pt[H_A * QK_NOPE:].reshape(H_A, QK_ROPE, Q_LORA)], axis=1)
    late = [d_w_in_t.reshape(N_DEV, W_IN_COLS // N_DEV, D_MODEL).astype(BF16), d_w_qb_t.astype(BF16), _stack_cols(d_w_kvb).astype(BF16)]
    def norm1_bwd_fn(dh, a, g, dres):
        dx, dg = _rms_bwd(dh, a, g)
        return dx + dres, dg

    (grad_x, d_g1), recv_late = _matmul(
        d_proj, w_in_pt, "nn", None, "proj_dx_norm1_bwd", tm=512, comm=(late, []),
        epi=(norm1_bwd_fn, [_rows(x), _whole(norm1_g), _rows(dx1)], [("rows", D_MODEL, F32), ("acc", 1, D_MODEL)]))
    recv = [*recv_late, *recv_early]

    small_parts = [d_g1, d_gq, d_gkv, d_rel_bias.reshape(1, NUM_BUCKETS * H_B), d_sinks, d_g2, d_conv_b, d_gfin, loss_row[:, :1]]
    small = jnp.concatenate(small_parts, axis=1)
    n_small = small.shape[1]
    pad = (-n_small) % 128
    small = jnp.pad(small, ((0, 0), (0, pad)))
    (recv_small,) = _exchange([], [small], "exchange_small_grads")

    def flat(a):
        return a.reshape(1, -1)

    small_w = [norm1_g, q_a_norm_g, kv_a_norm_g, rel_bias, sinks, norm2_g, conv_b, final_norm_g]
    small_m = [m_norm1_g, m_q_a_norm_g, m_kv_a_norm_g, m_rel_bias, m_sinks, m_norm2_g, m_conv_b, m_final_norm_g]
    small_v = [v_norm1_g, v_q_a_norm_g, v_kv_a_norm_g, v_rel_bias, v_sinks, v_norm2_g, v_conv_b, v_final_norm_g]
    cat = lambda parts: jnp.pad(jnp.concatenate([flat(a) for a in parts], axis=1), ((0, 0), (0, pad + 1)))[None]
    sm = _adamw(recv_small, cat(small_w), cat(small_m), cat(small_v), "adamw_small")
    big_names = ["w_in", "w_q_b", "w_kv_b", "w_out", "w_up", "w_down", "conv_w"]
    transposed = ("w_in", "w_q_b", "w_up")
    big_w = [tr(w_in), tr(w_q_b), w_kv_b, w_out, tr(w_up), w_down, conv_w]
    big_m = [tr(m_w_in), tr(m_w_q_b), m_w_kv_b, m_w_out, tr(m_w_up), m_w_down, m_conv_w]
    big_v = [tr(v_w_in), tr(v_w_q_b), v_w_kv_b, v_w_out, tr(v_w_up), v_w_down, v_conv_w]
    big = {n: _adamw(r, w, m, v, "adamw_" + n) for n, r, w, m, v in zip(big_names, recv, big_w, big_m, big_v)}

    loss = sm[0][0, 0, n_small - 1]
    order =["norm1_g", "w_in", "q_a_norm_g", "w_q_b", "kv_a_norm_g", "w_kv_b", "rel_bias", "sinks", "w_out", "norm2_g", "w_up",
             "conv_w", "conv_b", "w_down", "final_norm_g"]
    small_names = ["norm1_g", "q_a_norm_g", "kv_a_norm_g", "rel_bias", "sinks", "norm2_g", "conv_b", "final_norm_g"]
    offs, o = {}, 0
    for n, a in zip(small_names, small_w):
        offs[n] = (o, a.size, a.shape)
        o += a.size
    outs = [loss, grad_x[None]]
    for kind in range(4):
        for n in order:
            if n in big:
                outs.append(tr(big[n][kind]) if n in transposed else big[n][kind])
            else:
                o, size, shape = offs[n]
                outs.append(sm[kind][0, 0, o:o + size].reshape(shape))
    return tuple(outs)
```

```python
import math

import jax
import jax.numpy as jnp
from jax import lax
from jax.experimental import pallas as pl
from jax.experimental.pallas import tpu as pltpu

F32 = jnp.float32
BF16 = jnp.bfloat16

N_DEV = 8
D_MODEL = 1024
EPS = 1e-6
H_A, QK_NOPE, QK_ROPE, V_DIM, Q_LORA, KV_LORA = 8, 128, 64, 128, 256, 128
QK_HEAD = QK_NOPE + QK_ROPE
ROPE_THETA = 10000.0
H_B, KV_B, GROUP, HD_B, WINDOW, Q_BLOCK = 16, 4, 4, 64, 128, 128
SPAN = Q_BLOCK + 2 * WINDOW
NUM_BUCKETS, MAX_DISTANCE = 32, 128
D_FF = 2816
ADAM_LR, ADAM_B1, ADAM_B2, ADAM_EPS, ADAM_WD, ADAM_STEP = 0.001, 0.9, 0.999, 1e-08, 0.01, 10

W_IN_SIZES = (Q_LORA, KV_LORA + QK_ROPE, H_B * HD_B, KV_B * HD_B, KV_B * HD_B, D_MODEL, D_MODEL)
W_IN_COLS = sum(W_IN_SIZES)
PROJ_P = 4096
PROJ_GA, PROJ_GB, PROJ_QB, PROJ_QLAT, PROJ_KB, PROJ_VB, PROJ_CKV, PROJ_KROPE = 0, 1, 2, 12, 13, 14, 30, 31

VMEM_LIMIT = 56 * 1024 * 1024

NN = (((1,), (0,)), ((), ()))
NT = (((1,), (1,)), ((), ()))
TN = (((0,), (0,)), ((), ()))


def _pcall(body, *, name, grid, in_specs, out_specs, out_shape, scratch_shapes=(), dims=None, comm=None, aliases=None):
    if comm is None:
        params = pltpu.CompilerParams(dimension_semantics=dims, vmem_limit_bytes=VMEM_LIMIT)
        return pl.pallas_call(body, name=name, grid=grid, in_specs=in_specs, out_specs=out_specs, out_shape=out_shape,
                              scratch_shapes=list(scratch_shapes), input_output_aliases=aliases or {}, compiler_params=params)
    assert not aliases
    stacked, replicated = comm
    arrs = [*stacked, *replicated]
    n_st, n_arr = len(stacked), len(arrs)
    single = not isinstance(out_specs, (list, tuple))
    o_specs, o_shape = ([out_specs], [out_shape]) if single else (list(out_specs), list(out_shape))
    n_in, n_out = len(in_specs), len(o_specs)

    def wrapped(*refs):
        c_in = refs[n_in:n_in + n_arr]
        c_out = refs[n_in + n_arr + n_out:n_in + 2 * n_arr + n_out]
        sems = refs[len(refs) - 3:]
        own = (*refs[:n_in], *refs[n_in + n_arr:n_in + n_arr + n_out], *refs[n_in + 2 * n_arr + n_out:len(refs) - 3])
        if not grid:
            _xchg_start(c_in, c_out, sems, n_st)
            _xchg_finish(c_in, c_out, sems, n_st)
            return
        first = last = None
        for d, n in enumerate(grid):
            pid = pl.program_id(d)
            first = (pid == 0) if first is None else first & (pid == 0)
            last = (pid == n - 1) if last is None else last & (pid == n - 1)

        @pl.when(first)
        def _():
            _xchg_start(c_in, c_out, sems, n_st)

        body(*own)

        @pl.when(last)
        def _():
            _xchg_finish(c_in, c_out, sems, n_st)

    params = pltpu.CompilerParams(dimension_semantics=("arbitrary",) * len(grid), vmem_limit_bytes=VMEM_LIMIT)
    call = pl.pallas_call(wrapped, name=name, grid=grid, in_specs=[*in_specs, *[ANY] * n_arr], out_specs=[*o_specs, *[ANY] * n_arr],
                          out_shape=[*o_shape, *_xchg_out_shapes(stacked, replicated)],
                          scratch_shapes=[*scratch_shapes, *_xchg_sems(n_arr)], compiler_params=params)

    def run(*args):
        res = call(*args, *arrs)
        outs, landed = res[:n_out], res[n_out:]
        return (outs[0] if single else outs), landed

    return run


def _dot(a, b, dn):
    return lax.dot_general(a, b, dn, preferred_element_type=F32)


def _tile(n, target):
    best = None
    for t in range(128, min(n, target) + 1, 128):
        if n % t == 0:
            best = t
    return n if best is None else best


def _matmul(a, b, mode, out_dtype, name, residual=None, tm=1024, tn=1024, comm=None, a2=None, epi=None):
    if mode == "nn":
        (M, K), N = a.shape, b.shape[1]
    elif mode == "nt":
        (M, K), N = a.shape, b.shape[0]
    else:
        (K, M), N = a.shape, b.shape[1]
    tm, tn = _tile(M, tm), _tile(N, tn)
    a_spec = pl.BlockSpec((K, tm), lambda i, j: (0, i)) if mode == "tn" else pl.BlockSpec((tm, K), lambda i, j: (i, 0))
    b_spec = pl.BlockSpec((tn, b.shape[1]), lambda i, j: (j, 0)) if mode == "nt" else pl.BlockSpec((K, tn), lambda i, j: (0, j))
    o_spec = pl.BlockSpec((tm, tn), lambda i, j: (i, j))
    in_specs, args = [a_spec, b_spec], [a, b]
    if a2 is not None:
        assert (mode == "nt" and K + a2.shape[1] == b.shape[1]) or (mode == "nn" and K + a2.shape[1] == b.shape[0])
        if mode == "nn":
            b_spec = in_specs[1] = pl.BlockSpec((b.shape[0], tn), lambda i, j: (0, j))
        in_specs.append(pl.BlockSpec((tm, a2.shape[1]), lambda i, j: (i, 0)))
        args.append(a2)
    if residual is not None:
        in_specs.append(o_spec)
        args.append(residual)
    n_mm = len(args)
    scratch = [pltpu.VMEM((tm, K), a.dtype)] if mode == "tn" else []
    if epi is None:
        out_specs, out_shape, is_acc = o_spec, jax.ShapeDtypeStruct((M, N), out_dtype), None
    else:
        assert tn == N
        fn, epi_ins, epi_outs = epi
        in_specs += [mk(tm) for _, mk in epi_ins]
        args += [arr for arr, _ in epi_ins]
        out_specs, out_shape, is_acc = _row_out_specs(epi_outs, M, tm)

    def body(*refs):
        a_ref, b_ref = refs[0], refs[1]
        n_out = 1 if epi is None else len(is_acc)
        out_refs = refs[len(args):len(args) + n_out]
        if mode == "tn":
            at_ref = refs[len(args) + n_out]

            @pl.when(pl.program_id(1) == 0)
            def _():
                at_ref[...] = a_ref[...].T

            acc = _dot(at_ref[...], b_ref[...], NN)
        elif a2 is not None and mode == "nt":
            acc = _dot(a_ref[...], b_ref[:, :K], NT) + _dot(refs[2][...], b_ref[:, K:], NT)
        elif a2 is not None:
            acc = _dot(a_ref[...], b_ref[:K, :], NN) + _dot(refs[2][...], b_ref[K:, :], NN)
        else:
            acc = _dot(a_ref[...], b_ref[...], NT if mode == "nt" else NN)
        if residual is not None:
            acc = acc + refs[n_mm - 1][...]
        if epi is None:
            out_refs[0][...] = acc.astype(out_dtype)
        else:
            _store_rows(out_refs, fn(acc, *[_load_f32(r) for r in refs[n_mm:len(args)]]), is_acc)

    return _pcall(body, name=name, grid=(M // tm, N // tn), in_specs=in_specs, out_specs=out_specs,
                  out_shape=out_shape, scratch_shapes=scratch,
                  dims=("arbitrary" if epi is not None else "parallel", "arbitrary"), comm=comm)(*args)


def _rows(arr, width=None, col=0):
    width = arr.shape[1] if width is None else width
    return (arr, lambda tm: pl.BlockSpec((tm, width), lambda i, *_: (i, col)))


def _heads(arr):
    return (arr, lambda tm: pl.BlockSpec((arr.shape[0], tm, arr.shape[2]), lambda i, *_: (0, i, 0)))


def _whole(arr):
    nd = arr.ndim
    return (arr, lambda tm: pl.BlockSpec(arr.shape, lambda i, *_: (0,) * nd))


def _row_out_specs(outs, n_rows, tm):
    out_specs, out_shape, is_acc = [], [], []
    for o in outs:
        if o[0] == "rows":
            out_specs.append(pl.BlockSpec((tm, o[1]), lambda i, *_: (i, 0)))
            out_shape.append(jax.ShapeDtypeStruct((n_rows, o[1]), o[2]))
        elif o[0] == "cols":
            out_specs.append(pl.BlockSpec((tm, o[1]), lambda i, *_, c=o[2]: (i, c)))
            out_shape.append(jax.ShapeDtypeStruct((n_rows, o[3]), o[4]))
        elif o[0] == "heads":
            out_specs.append(pl.BlockSpec((o[1], tm, o[2]), lambda i, *_: (0, i, 0)))
            out_shape.append(jax.ShapeDtypeStruct((o[1], n_rows, o[2]), o[3]))
        else:
            out_specs.append(pl.BlockSpec((o[1], o[2]), lambda i, *_: (0, 0)))
            out_shape.append(jax.ShapeDtypeStruct((o[1], o[2]), F32))
        is_acc.append(o[0] == "acc")
    return out_specs, out_shape, is_acc


def _load_f32(r):
    v = r[...]
    return v.astype(F32) if v.dtype == BF16 else v


def _store_rows(out_refs, vals, is_acc):
    for r, v, acc in zip(out_refs, vals, is_acc):
        if acc:
            @pl.when(pl.program_id(0) == 0)
            def _():
                r[...] = jnp.zeros_like(r)

            r[...] += v
        else:
            r[...] = v.astype(r.dtype)


def _rowwise(fn, name, n_rows, tm, ins, outs, upcast=True, into=None):
    tm = min(tm, n_rows)
    assert n_rows % tm == 0
    in_specs = [mk(tm) for _, mk in ins]
    out_specs, out_shape, is_acc = _row_out_specs(outs, n_rows, tm)
    n_in = len(ins)
    args = [a for a, _ in ins]
    aliases = {}
    if into is not None:
        in_specs.append(ANY)
        args.append(into[0])
        aliases = {n_in: into[1]}

    def body(*refs):
        vals = fn(*[_load_f32(r) if upcast else r[...] for r in refs[:n_in]])
        _store_rows(refs[len(args):], vals, is_acc)

    return _pcall(body, name=name, grid=(n_rows // tm,), in_specs=in_specs, out_specs=out_specs,
                  out_shape=out_shape, dims=("arbitrary",), aliases=aliases)(*args)


def _rms(x, g):
    r = lax.rsqrt(jnp.mean(x * x, axis=-1, keepdims=True) + EPS)
    return x * r * g


def _rms_bwd(dy, x, g):
    r = lax.rsqrt(jnp.mean(x * x, axis=-1, keepdims=True) + EPS)
    xhat = x * r
    dxhat = dy * g
    dx = r * (dxhat - xhat * jnp.mean(dxhat * xhat, axis=-1, keepdims=True))
    return dx, jnp.sum(dy * xhat, axis=0, keepdims=True)


def _rope(x1, x2, cos, sin):
    return x1 * cos - x2 * sin, x2 * cos + x1 * sin


def _rope_bwd(d1, d2, cos, sin):
    return d1 * cos + d2 * sin, d2 * cos - d1 * sin


def _sigmoid(x):
    return 1.0 / (1.0 + jnp.exp(-x))


MLA_SCALE = 1.0 / math.sqrt(QK_HEAD)
MLA_PRESCALE = MLA_SCALE * math.log2(math.e)
MLA_TQ, MLA_KC = 1024, 1024


def _mla_fwd(q_full, k_full, kv, S, comm=None):
    tq, kc = min(MLA_TQ, S), min(MLA_KC, S)

    def body(q_ref, k_ref, v_ref, o_ref, lse_ref):
        q = q_ref[0]
        m = jnp.full((tq, 1), -1e30, F32)
        l = jnp.zeros((tq, 1), F32)
        acc = jnp.zeros((tq, V_DIM), F32)
        for c in range(S // kc):
            s = _dot(q, k_ref[0, c * kc:(c + 1) * kc, :], NT)
            m_new = jnp.maximum(m, jnp.max(s, axis=-1, keepdims=True))
            alpha = jnp.exp2(m - m_new)
            p = jnp.exp2(s - m_new)
            l = alpha * l + jnp.sum(p, axis=-1, keepdims=True)
            acc = alpha * acc + _dot(p.astype(BF16), v_ref[c * kc:(c + 1) * kc, :], NN)
            m = m_new
        o_ref[...] = (acc / l).astype(BF16)
        lse_ref[0] = m + jnp.log2(l)

    return _pcall(
        body, name="mla_fwd", grid=(H_A, S // tq),
        in_specs=[pl.BlockSpec((1, tq, QK_HEAD), lambda h, i: (h, i, 0)),
                  pl.BlockSpec((1, S, QK_HEAD), lambda h, i: (h, 0, 0)),
                  pl.BlockSpec((S, V_DIM), lambda h, i: (0, 2 * h + 1))],
        out_specs=[pl.BlockSpec((tq, V_DIM), lambda h, i: (i, h)),
                   pl.BlockSpec((1, tq, 1), lambda h, i: (h, i, 0))],
        out_shape=[jax.ShapeDtypeStruct((S, H_A * V_DIM), BF16), jax.ShapeDtypeStruct((H_A, S, 1), F32)],
        dims=("parallel", "parallel"), comm=comm)(q_full, k_full, kv)


def _mla_bwd(q_full, k_full, kv, do_a, o_a, lse, S, comm=None):
    tq, kc = min(MLA_TQ, S), min(MLA_KC, S)

    def body(q_ref, k_ref, v_ref, do_ref, o_ref, lse_ref, dq_ref, dk_ref, dv_ref):
        @pl.when(pl.program_id(1) == 0)
        def _():
            dk_ref[...] = jnp.zeros_like(dk_ref)
            dv_ref[...] = jnp.zeros_like(dv_ref)

        q = q_ref[0]
        do = do_ref[...]
        lse_q = lse_ref[0]
        delta = jnp.sum(do.astype(F32) * o_ref[...].astype(F32), axis=-1, keepdims=True)
        dq = jnp.zeros((tq, QK_HEAD), F32)
        for c in range(S // kc):
            k = k_ref[0, c * kc:(c + 1) * kc, :]
            v = v_ref[c * kc:(c + 1) * kc, :]
            p = jnp.exp2(_dot(q, k, NT) - lse_q)
            ds = (p * (_dot(do, v, NT) - delta)).astype(BF16)
            dq = dq + _dot(ds, k, NN)
            dk_ref[0, c * kc:(c + 1) * kc, :] += _dot(ds, q, TN)
            dv_ref[0, c * kc:(c + 1) * kc, :] += _dot(p.astype(BF16), do, TN)
        dq_ref[0] = dq * MLA_SCALE

    return _pcall(
        body, name="mla_bwd", grid=(H_A, S // tq),
        in_specs=[pl.BlockSpec((1, tq, QK_HEAD), lambda h, i: (h, i, 0)),
                  pl.BlockSpec((1, S, QK_HEAD), lambda h, i: (h, 0, 0)),
                  pl.BlockSpec((S, V_DIM), lambda h, i: (0, 2 * h + 1)),
                  pl.BlockSpec((tq, V_DIM), lambda h, i: (i, h)),
                  pl.BlockSpec((tq, V_DIM), lambda h, i: (i, h)),
                  pl.BlockSpec((1, tq, 1), lambda h, i: (h, i, 0))],
        out_specs=[pl.BlockSpec((1, tq, QK_HEAD), lambda h, i: (h, i, 0)),
                   pl.BlockSpec((1, S, QK_HEAD), lambda h, i: (h, 0, 0)),
                   pl.BlockSpec((1, S, V_DIM), lambda h, i: (h, 0, 0))],
        out_shape=[jax.ShapeDtypeStruct((H_A, S, QK_HEAD), F32), jax.ShapeDtypeStruct((H_A, S, QK_HEAD), F32),
                   jax.ShapeDtypeStruct((H_A, S, V_DIM), F32)],
        dims=("parallel", "arbitrary"), comm=comm)(q_full, k_full, kv, do_a, o_a, lse)


WIN_SCALE = 1.0 / math.sqrt(HD_B)


def _win_specs(S):
    last = S // Q_BLOCK - 1
    qspec = pl.BlockSpec((Q_BLOCK, H_B * HD_B), lambda n: (n, PROJ_QB))
    kspecs = [[pl.BlockSpec((Q_BLOCK, KV_B * HD_B), lambda n, d=d, c=c: (jnp.clip(n + d, 0, last), c)) for d in (-1, 0, 1)]
              for c in (PROJ_KB, PROJ_VB)]
    bias_spec = pl.BlockSpec((H_B, SPAN, Q_BLOCK), lambda n: (0, 0, 0))
    sink_spec = pl.BlockSpec((H_B, Q_BLOCK), lambda n: (0, 0))
    return qspec, kspecs, bias_spec, sink_spec


def _win_edge_ok(n, n_blk):
    row = lax.broadcasted_iota(jnp.int32, (SPAN, 1), 0)
    return jnp.logical_not(((n == 0) & (row < WINDOW)) | ((n == n_blk - 1) & (row >= SPAN - WINDOW)))


def _lanes4(pieces):
    return jnp.concatenate(pieces, axis=1)


def _win_probs(kg, q4t, bias_ref, sink_ref, g, edge_ok):
    bias4 = _lanes4([bias_ref[GROUP * g + j] for j in range(GROUP)])
    sink4 = _lanes4([sink_ref[GROUP * g + j:GROUP * g + j + 1, :] for j in range(GROUP)])
    s = jnp.where(edge_ok, _dot(kg, q4t, NN) + bias4, -1e30)
    m = jnp.maximum(jnp.max(s, axis=0, keepdims=True), sink4)
    p = jnp.exp(s - m)
    e_sink = jnp.exp(sink4 - m)
    inv_l = 1.0 / (jnp.sum(p, axis=0, keepdims=True) + e_sink)
    return p * inv_l, e_sink * inv_l


def _group_t(xt, g):
    return _lanes4([xt[HD_B * (GROUP * g + j):HD_B * (GROUP * g + j + 1), :] for j in range(GROUP)])


def _win_fwd(proj, bias_t, sinks_b, S, comm=None):
    n_blk = S // Q_BLOCK
    qspec, kspecs, bias_spec, sink_spec = _win_specs(S)

    def body(q_ref, k0, k1, k2, v0, v1, v2, bias_ref, sink_ref, o_ref):
        n = pl.program_id(0)
        edge_ok = _win_edge_ok(n, n_blk)
        k = jnp.concatenate([k0[...], k1[...], k2[...]], axis=0)
        vt = jnp.concatenate([v0[...], v1[...], v2[...]], axis=0).T
        qt = (q_ref[...].astype(F32) * WIN_SCALE).T.astype(BF16)
        parts = []
        for g in range(KV_B):
            p, _ = _win_probs(k[:, HD_B * g:HD_B * (g + 1)], _group_t(qt, g), bias_ref, sink_ref, g, edge_ok)
            o4t = _dot(vt[HD_B * g:HD_B * (g + 1), :], p.astype(BF16), NN)
            parts += [o4t[:, Q_BLOCK * j:Q_BLOCK * (j + 1)] for j in range(GROUP)]
        o_ref[...] = jnp.concatenate(parts, axis=0).T.astype(BF16)

    return _pcall(body, name="win_fwd", grid=(n_blk,),
                  in_specs=[qspec, *kspecs[0], *kspecs[1], bias_spec, sink_spec],
                  out_specs=pl.BlockSpec((Q_BLOCK, H_B * HD_B), lambda n: (n, 0)),
                  out_shape=jax.ShapeDtypeStruct((S, H_B * HD_B), BF16),
                  dims=("parallel",), comm=comm)(*[proj] * 7, bias_t, sinks_b)


def _win_bwd(proj, bias_t, sinks_b, do_b, d_proj, S):
    n_blk = S // Q_BLOCK
    qspec, kspecs, bias_spec, sink_spec = _win_specs(S)

    def body(q_ref, k0, k1, k2, v0, v1, v2, bias_ref, sink_ref, do_ref, _, dq_ref, dk_ref, dv_ref, dbias_ref, dsink_ref, dsink_acc):
        n = pl.program_id(0)

        @pl.when(n == 0)
        def _():
            dk_ref[...] = jnp.zeros_like(dk_ref)
            dv_ref[...] = jnp.zeros_like(dv_ref)
            dbias_ref[...] = jnp.zeros_like(dbias_ref)
            dsink_acc[...] = jnp.zeros_like(dsink_acc)

        edge_ok = _win_edge_ok(n, n_blk)
        k = jnp.concatenate([k0[...], k1[...], k2[...]], axis=0)
        v = jnp.concatenate([v0[...], v1[...], v2[...]], axis=0)
        kt = k.T
        qt = (q_ref[...].astype(F32) * WIN_SCALE).T.astype(BF16)
        dot_ = do_ref[...].astype(F32).T.astype(BF16)
        dq_parts, dks, dvs = [], [], []
        for g in range(KV_B):
            kg, vg = k[:, HD_B * g:HD_B * (g + 1)], v[:, HD_B * g:HD_B * (g + 1)]
            q4t, do4t = _group_t(qt, g), _group_t(dot_, g)
            p, p_sink = _win_probs(kg, q4t, bias_ref, sink_ref, g, edge_ok)
            dp = _dot(vg, do4t, NN)
            delta = jnp.sum(p * dp, axis=0, keepdims=True)
            ds = p * (dp - delta)
            for j in range(GROUP):
                dbias_ref[GROUP * g + j] += ds[:, Q_BLOCK * j:Q_BLOCK * (j + 1)]
            dsink_acc[g:g + 1, :] += -p_sink * delta
            dsb = ds.astype(BF16)
            dq4t = _dot(kt[HD_B * g:HD_B * (g + 1), :], dsb, NN) * WIN_SCALE
            dq_parts += [dq4t[:, Q_BLOCK * j:Q_BLOCK * (j + 1)] for j in range(GROUP)]
            dks.append(_dot(dsb, q4t, NT))
            dvs.append(_dot(p.astype(BF16), do4t, NT))
        dq_ref[...] = jnp.concatenate(dq_parts, axis=0).T.astype(BF16)
        rows = pl.ds(pl.multiple_of(n * Q_BLOCK, Q_BLOCK), SPAN)
        dk_ref[rows, :] += jnp.concatenate(dks, axis=1)
        dv_ref[rows, :] += jnp.concatenate(dvs, axis=1)

        @pl.when(n == n_blk - 1)
        def _():
            acc = dsink_acc[...]
            dsink_ref[...] = jnp.concatenate(
                [jnp.sum(acc[:, Q_BLOCK * j:Q_BLOCK * (j + 1)], axis=1, keepdims=True) for j in range(GROUP)], axis=1)

    whole = lambda shape: pl.BlockSpec(shape, lambda n: (0,) * len(shape))
    return _pcall(
        body, name="win_bwd", grid=(n_blk,),
        in_specs=[qspec, *kspecs[0], *kspecs[1], bias_spec, sink_spec, pl.BlockSpec((Q_BLOCK, H_B * HD_B), lambda n: (n, 0)), ANY],
        out_specs=[qspec, whole((S + 2 * WINDOW, KV_B * HD_B)),
                   whole((S + 2 * WINDOW, KV_B * HD_B)), whole((H_B, SPAN, Q_BLOCK)), whole((KV_B, GROUP))],
        out_shape=[jax.ShapeDtypeStruct((S, PROJ_P), BF16), jax.ShapeDtypeStruct((S + 2 * WINDOW, KV_B * HD_B), F32),
                   jax.ShapeDtypeStruct((S + 2 * WINDOW, KV_B * HD_B), F32), jax.ShapeDtypeStruct((H_B, SPAN, Q_BLOCK), F32),
                   jax.ShapeDtypeStruct((KV_B, GROUP), F32)],
        scratch_shapes=[pltpu.VMEM((KV_B, GROUP * Q_BLOCK), F32)],
        dims=("arbitrary",), aliases={10: 0})(*[proj] * 7, bias_t, sinks_b, do_b, d_proj)


def _bias_table(rel_bias_t, onehot_t, in_band):
    def body(rb_ref, oh_ref, band_ref, o_ref):
        t = lax.dot_general(rb_ref[...], oh_ref[...], NN, preferred_element_type=F32, precision=lax.Precision.HIGHEST)
        o_ref[...] = jnp.where(band_ref[...] > 0.5, t, -1e30)

    n = onehot_t.shape[1]
    tn = _tile(n, 8192)
    return _pcall(body, name="bias_table", grid=(n // tn,),
                  in_specs=[pl.BlockSpec((H_B, NUM_BUCKETS), lambda j: (0, 0)), pl.BlockSpec((NUM_BUCKETS, tn), lambda j: (0, j)),
                            pl.BlockSpec((1, tn), lambda j: (0, j))],
                  out_specs=pl.BlockSpec((H_B, tn), lambda j: (0, j)),
                  out_shape=jax.ShapeDtypeStruct((H_B, n), F32), dims=("parallel",))(rel_bias_t, onehot_t, in_band)


def _bias_table_bwd(dbias, onehot_t):
    n = onehot_t.shape[1]
    tk = _tile(n, 8192)

    def body(d_ref, oh_ref, o_ref):
        @pl.when(pl.program_id(0) == 0)
        def _():
            o_ref[...] = jnp.zeros_like(o_ref)

        o_ref[...] += lax.dot_general(d_ref[...], oh_ref[...], NT, preferred_element_type=F32, precision=lax.Precision.HIGHEST)

    return _pcall(body, name="bias_table_bwd", grid=(n // tk,),
                  in_specs=[pl.BlockSpec((H_B, tk), lambda j: (0, j)), pl.BlockSpec((NUM_BUCKETS, tk), lambda j: (0, j))],
                  out_specs=pl.BlockSpec((H_B, NUM_BUCKETS), lambda j: (0, 0)),
                  out_shape=jax.ShapeDtypeStruct((H_B, NUM_BUCKETS), F32), dims=("arbitrary",))(dbias, onehot_t)


CONV_STRIP = 128
N_STRIPS = D_FF // CONV_STRIP
CONV_ROWS = 128
HALO = 8


def _strip(rows, half):
    return pl.BlockSpec((rows, CONV_STRIP), lambda j: (0, j + half * N_STRIPS))


def _fill_padded(pad_ref, src_ref, halo, S):
    pad_ref[0:halo, :] = jnp.zeros((halo, CONV_STRIP), F32)
    pad_ref[halo + S:2 * halo + S, :] = jnp.zeros((halo, CONV_STRIP), F32)
    pad_ref[halo:halo + S, :] = src_ref[...].astype(F32)


def _taps(ext, n):
    m = ext.shape[0]
    return pltpu.roll(ext, 1, axis=0)[HALO:HALO + n], ext[HALO:HALO + n], pltpu.roll(ext, m - 1, axis=0)[HALO:HALO + n]


def _conv_gate_fwd(u, conv_w, conv_b, S):
    R = min(CONV_ROWS, S)

    def body(ug_ref, uv_ref, wg_ref, wv_ref, bg_ref, bv_ref, a_ref, gpad, vpad):
        _fill_padded(gpad, ug_ref, HALO, S)
        _fill_padded(vpad, uv_ref, HALO, S)
        wg, wv, bg, bv = wg_ref[...], wv_ref[...], bg_ref[...], bv_ref[...]

        def conv(pad_ref, r0, w, b):
            dn, mid, up = (pad_ref[pl.ds(r0 + HALO + d, R), :] for d in (-1, 0, 1))
            return dn * w[0:1, :] + mid * w[1:2, :] + up * w[2:3, :] + b

        def step(c, carry):
            r0 = pl.multiple_of(c * R, R)
            g = conv(gpad, r0, wg, bg)
            val = conv(vpad, r0, wv, bv)
            a_ref[pl.ds(r0, R), :] = (g * _sigmoid(g) * val).astype(BF16)
            return carry

        lax.fori_loop(0, S // R, step, 0)

    return _pcall(body, name="conv_gate_fwd", grid=(N_STRIPS,),
                  in_specs=[_strip(S, 0), _strip(S, 1), _strip(3, 0), _strip(3, 1), _strip(1, 0), _strip(1, 1)],
                  out_specs=_strip(S, 0), out_shape=jax.ShapeDtypeStruct((S, D_FF), BF16),
                  scratch_shapes=[pltpu.VMEM((S + 2 * HALO, CONV_STRIP), F32)] * 2,
                  dims=("parallel",))(u, u, conv_w, conv_w, conv_b, conv_b)


def _conv_gate_bwd(u, conv_w, conv_b, da, S):
    R = min(CONV_ROWS, S)
    n = R + 2 * HALO

    def body(ug_ref, uv_ref, wg_ref, wv_ref, bg_ref, bv_ref, da_ref, dug_ref, duv_ref, dwg_ref, dwv_ref, dbg_ref, dbv_ref,
             gpad, vpad, dapad):
        _fill_padded(gpad, ug_ref, 2 * HALO, S)
        _fill_padded(vpad, uv_ref, 2 * HALO, S)
        _fill_padded(dapad, da_ref, HALO, S)
        wg, wv, bg, bv = wg_ref[...], wv_ref[...], bg_ref[...], bv_ref[...]

        def conv(pad_ref, r0, w, b):
            dn, mid, up = (pad_ref[pl.ds(r0 + HALO + d, n), :] for d in (-1, 0, 1))
            return dn * w[0:1, :] + mid * w[1:2, :] + up * w[2:3, :] + b, mid[HALO:HALO + R]

        def conv_bwd(duc, u_mid, w, r0, du_ref):
            dn, mid, up = _taps(duc, R)
            du_ref[pl.ds(r0, R), :] = (up * w[0:1, :] + mid * w[1:2, :] + dn * w[2:3, :]).astype(BF16)
            dw = jnp.concatenate([jnp.sum(up * u_mid, axis=0, keepdims=True), jnp.sum(mid * u_mid, axis=0, keepdims=True),
                                  jnp.sum(dn * u_mid, axis=0, keepdims=True)], axis=0)
            return dw, jnp.sum(mid, axis=0, keepdims=True)

        def step(c, carry):
            dw_g, db_g, dw_v, db_v = carry
            r0 = pl.multiple_of(c * R, R)
            g, ug_mid = conv(gpad, r0, wg, bg)
            val, uv_mid = conv(vpad, r0, wv, bv)
            da_ext = dapad[pl.ds(r0, n), :]
            sg = _sigmoid(g)
            ddw_v, ddb_v = conv_bwd(da_ext * (g * sg), uv_mid, wv, r0, duv_ref)
            ddw_g, ddb_g = conv_bwd(da_ext * val * (sg * (1.0 + g * (1.0 - sg))), ug_mid, wg, r0, dug_ref)
            return dw_g + ddw_g, db_g + ddb_g, dw_v + ddw_v, db_v + ddb_v

        z3, z1 = jnp.zeros((3, CONV_STRIP), F32), jnp.zeros((1, CONV_STRIP), F32)
        dwg_ref[...], dbg_ref[...], dwv_ref[...], dbv_ref[...] = lax.fori_loop(0, S // R, step, (z3, z1, z3, z1))

    half = lambda r, dt: (_strip(r, 0), jax.ShapeDtypeStruct((r, D_FF), dt))
    outs = [half(S, BF16), half(S, BF16), half(3, F32), half(3, F32), half(1, F32), half(1, F32)]
    return _pcall(
        body, name="conv_gate_bwd", grid=(N_STRIPS,),
        in_specs=[_strip(S, 0), _strip(S, 1), _strip(3, 0), _strip(3, 1), _strip(1, 0), _strip(1, 1), _strip(S, 0)],
        out_specs=[o[0] for o in outs], out_shape=[o[1] for o in outs],
        scratch_shapes=[pltpu.VMEM((S + 4 * HALO, CONV_STRIP), F32)] * 2 + [pltpu.VMEM((S + 2 * HALO, CONV_STRIP), F32)],
        dims=("parallel",))(u, u, conv_w, conv_w, conv_b, conv_b, da)


MESH = pl.DeviceIdType.MESH
ANY = pl.BlockSpec(memory_space=pl.ANY)


def _place():
    return lax.axis_index("x"), lax.axis_index("y"), lax.axis_index("c")


def _all_gather(shards):
    n_arr = len(shards)

    def body(*refs):
        ins, outs = refs[:n_arr], refs[n_arr:2 * n_arr]
        send_sems, recv_sems, local_sems = refs[2 * n_arr:]
        x, y, c = _place()
        me, sibling = (x, y, c), (x, y, 1 - c)
        chips = [(1 - x, y), (x, 1 - y), (1 - x, 1 - y)]

        def slot(a, p):
            return outs[a].at[4 * p[0] + 2 * p[1] + p[2]]

        def copy(a, k, block, to, src=None):
            return pltpu.make_async_remote_copy(
                src_ref=slot(a, block) if src is None else src, dst_ref=slot(a, block),
                send_sem=send_sems.at[a, k], recv_sem=recv_sems.at[a, k], device_id=to, device_id_type=MESH)

        mine = [pltpu.make_async_copy(ins[a], slot(a, me), local_sems.at[a]) for a in range(n_arr)]
        for cp in mine:
            cp.start()
        first = []
        for a in range(n_arr):
            first.append(copy(a, 0, me, sibling, src=ins[a]))
            first += [copy(a, 1 + j, me, (*chip, c), src=ins[a]) for j, chip in enumerate(chips)]
        for cp in first:
            cp.start()
        passed = []
        for j, chip in enumerate(chips):
            for a in range(n_arr):
                copy(a, 1 + j, (*chip, c), me).wait_recv()
                cp = copy(a, 4 + j, (*chip, c), sibling)
                cp.start()
                passed.append(cp)
        for a in range(n_arr):
            copy(a, 0, sibling, me).wait_recv()
            for j, chip in enumerate(chips):
                copy(a, 4 + j, (*chip, 1 - c), me).wait_recv()
        for cp in first + passed:
            cp.wait_send()
        for cp in mine:
            cp.wait()

    return pl.pallas_call(
        body, name="all_gather_weights",
        in_specs=[ANY] * n_arr, out_specs=[ANY] * n_arr,
        out_shape=[jax.ShapeDtypeStruct((N_DEV, *s.shape), s.dtype) for s in shards],
        scratch_shapes=[pltpu.SemaphoreType.DMA((n_arr, 7)), pltpu.SemaphoreType.DMA((n_arr, 7)), pltpu.SemaphoreType.DMA((n_arr,))],
    )(*shards)


def _xchg_out_shapes(stacked, replicated):
    return ([jax.ShapeDtypeStruct(s.shape, s.dtype) for s in stacked]
            + [jax.ShapeDtypeStruct((N_DEV, *r.shape), r.dtype) for r in replicated])


def _xchg_sems(n_arr):
    return [pltpu.SemaphoreType.DMA((n_arr, 7)), pltpu.SemaphoreType.DMA((n_arr, 7)), pltpu.SemaphoreType.DMA((n_arr,))]


def _xchg_copies(ins, outs, sems, n_st, with_recv):
    send_sems, recv_sems, local_sems = sems
    n_arr = len(ins)
    x, y, c = _place()
    me = 4 * x + 2 * y + c

    def src(a, idx):
        return ins[a].at[idx] if a < n_st else ins[a]

    mine = [pltpu.make_async_copy(src(a, me), outs[a].at[me], local_sems.at[a]) for a in range(n_arr)]
    pairs = []
    for k in range(1, N_DEV):
        px, py, pc = x ^ (k >> 2), y ^ ((k >> 1) & 1), c ^ (k & 1)
        peer = 4 * px + 2 * py + pc
        for a in range(n_arr):
            sems_k = dict(send_sem=send_sems.at[a, k - 1], recv_sem=recv_sems.at[a, k - 1], device_id_type=MESH)
            send = pltpu.make_async_remote_copy(src_ref=src(a, peer), dst_ref=outs[a].at[me], device_id=(px, py, pc), **sems_k)
            recv = None
            if with_recv:
                recv = pltpu.make_async_remote_copy(src_ref=src(a, peer), dst_ref=outs[a].at[peer], device_id=(x, y, c), **sems_k)
            pairs.append((send, recv))
    return mine, pairs


def _xchg_start(ins, outs, sems, n_st):
    mine, pairs = _xchg_copies(ins, outs, sems, n_st, False)
    for cp in mine:
        cp.start()
    for send, _ in pairs:
        send.start()


def _xchg_finish(ins, outs, sems, n_st):
    mine, pairs = _xchg_copies(ins, outs, sems, n_st, True)
    for _, recv in pairs:
        recv.wait_recv()
    for send, _ in pairs:
        send.wait_send()
    for cp in mine:
        cp.wait()


def _exchange(stacked, replicated, name):
    _, landed = _pcall(lambda: None, name=name, grid=(), in_specs=[], out_specs=[], out_shape=[], comm=(stacked, replicated))()
    return landed


HBM = pl.BlockSpec(memory_space=pltpu.HBM)
SEMS = pl.BlockSpec(memory_space=pltpu.SEMAPHORE)
SIDE_EFFECT = pltpu.SideEffectType.DATAFLOW_SIDE_EFFECTING


N_SPLIT_SEMS = 2 * (N_DEV - 1)


def _split_copies(src, land, sems, with_recv):
    x, y, c = _place()
    me = 4 * x + 2 * y + c
    pairs = []
    for k in range(1, N_DEV):
        px, py, pc = x ^ (k >> 2), y ^ ((k >> 1) & 1), c ^ (k & 1)
        peer = 4 * px + 2 * py + pc
        sems_k = dict(send_sem=sems[k - 1], recv_sem=sems[N_DEV - 1 + k - 1], device_id_type=MESH)
        send = pltpu.make_async_remote_copy(src_ref=src.at[peer], dst_ref=land.at[me], device_id=(px, py, pc), **sems_k)
        recv = None
        if with_recv:
            recv = pltpu.make_async_remote_copy(src_ref=src.at[peer], dst_ref=land.at[peer], device_id=(x, y, c), **sems_k)
        pairs.append((send, recv))
    return pairs


def _exchange_start(stacked, name):
    def body(src, land, *rest):
        for send, _ in _split_copies(src, land, rest[:N_SPLIT_SEMS], False):
            send.start()
        rest[-1][...] = jnp.zeros_like(rest[-1])

    shape = pltpu.HBM(stacked.shape, stacked.dtype)
    res = pl.pallas_call(
        body, name=name, in_specs=[HBM, HBM],
        out_shape=(*[pltpu.SemaphoreType.DMA(())] * N_SPLIT_SEMS, shape, shape, jax.ShapeDtypeStruct((8, 128), F32)),
        out_specs=(*[SEMS] * N_SPLIT_SEMS, HBM, HBM, pl.BlockSpec(memory_space=pltpu.VMEM)),
        input_output_aliases={0: N_SPLIT_SEMS, 1: N_SPLIT_SEMS + 1},
        compiler_params=pltpu.CompilerParams(has_side_effects=SIDE_EFFECT),
    )(pltpu.with_memory_space_constraint(stacked, pltpu.HBM),
      pltpu.with_memory_space_constraint(lax.empty(stacked.shape, stacked.dtype), pltpu.HBM))
    return res[:N_SPLIT_SEMS], res[N_SPLIT_SEMS], res[N_SPLIT_SEMS + 1], res[-1]


def _exchange_wait(sems, src, land, after, name):
    def body(src_ref, land_ref, *rest):
        for send, recv in _split_copies(src_ref, land_ref, rest[:N_SPLIT_SEMS], True):
            send.wait_send()
            recv.wait_recv()

    shape = pltpu.HBM(src.shape, src.dtype)
    return pl.pallas_call(
        body, name=name, in_specs=[HBM, HBM, *[SEMS] * N_SPLIT_SEMS, ANY],
        out_shape=(shape, shape), out_specs=(HBM, HBM), input_output_aliases={0: 0, 1: 1},
        compiler_params=pltpu.CompilerParams(has_side_effects=SIDE_EFFECT))(src, land, *sems, after)[1]


def _adamw(parts, w, m, v, name):
    _, R, C = w.shape
    tr = R if R <= 512 else max(t for t in range(16, 513, 16) if R % t == 0)
    pr = tr if parts.shape[1] == R else -(-R // 16) * 16
    assert pr == tr or tr == R

    def body(p_ref, w_ref, m_ref, v_ref, g_ref, d_ref, nm_ref, nv_ref):
        g = p_ref[0].astype(F32)[:tr]
        for s in range(1, N_DEV):
            g = g + p_ref[s].astype(F32)[:tr]
        m2 = ADAM_B1 * m_ref[0] + (1.0 - ADAM_B1) * g
        v2 = ADAM_B2 * v_ref[0] + (1.0 - ADAM_B2) * (g * g)
        m_hat = m2 / (1.0 - ADAM_B1 ** ADAM_STEP)
        v_hat = v2 / (1.0 - ADAM_B2 ** ADAM_STEP)
        g_ref[0] = g
        d_ref[0] = -ADAM_LR * (m_hat / (jnp.sqrt(v_hat) + ADAM_EPS) + ADAM_WD * w_ref[0])
        nm_ref[0] = m2
        nv_ref[0] = v2

    blk = pl.BlockSpec((1, tr, C), lambda i: (0, i, 0))
    return _pcall(body, name=name, grid=(R // tr,),
                  in_specs=[pl.BlockSpec((N_DEV, pr, C), lambda i: (0, i, 0)), blk, blk, blk],
                  out_specs=[blk] * 4, out_shape=[jax.ShapeDtypeStruct((1, R, C), F32)] * 4,
                  dims=("parallel",))(parts, w, m, v)


def _t5_bucket(rel):
    nb = NUM_BUCKETS // 2
    max_exact = nb // 2
    base = (rel > 0).astype(jnp.int32) * nb
    n = jnp.abs(rel)
    nf = jnp.maximum(n, 1).astype(jnp.float32)
    large = max_exact + (jnp.log(nf / max_exact) / math.log(MAX_DISTANCE / max_exact) * (nb - max_exact)).astype(jnp.int32)
    large = jnp.minimum(large, nb - 1)
    return base + jnp.where(n < max_exact, n, large)


def _unstack_cols(g):
    return jnp.transpose(g, (1, 0, 2)).reshape(g.shape[1], N_DEV * g.shape[2])


def _stack_cols(w, n=N_DEV):
    R = w.shape[0]
    return jnp.transpose(w.reshape(R, n, w.shape[1] // n), (1, 0, 2))


def _stack_halves(g, v):
    return jnp.concatenate([_stack_cols(g, N_DEV // 2), _stack_cols(v, N_DEV // 2)], axis=0)


def kernel(x, positions, norm1_g, w_in, q_a_norm_g, w_q_b, kv_a_norm_g, w_kv_b, rel_bias, sinks, w_out, norm2_g, w_up, conv_w, conv_b, w_down, final_norm_g, loss_target, m_norm1_g, m_w_in, m_q_a_norm_g, m_w_q_b, m_kv_a_norm_g, m_w_kv_b, m_rel_bias, m_sinks, m_w_out, m_norm2_g, m_w_up, m_conv_w, m_conv_b, m_w_down, m_final_norm_g, v_norm1_g, v_w_in, v_q_a_norm_g, v_w_q_b, v_kv_a_norm_g, v_w_kv_b, v_rel_bias, v_sinks, v_w_out, v_norm2_g, v_w_up, v_conv_w, v_conv_b, v_w_down, v_final_norm_g):
    S = x.shape[1]
    x = x[0]
    target = loss_target[0]
    TM = 256

    tr = lambda w: jnp.swapaxes(w, 1, 2)
    g_in, g_qb, g_kvb = _all_gather([tr(w_in)[0].astype(BF16), tr(w_q_b)[0].astype(BF16), w_kv_b[0].astype(BF16)])
    late_weights = [w_out[0].astype(BF16), tr(w_up)[0].astype(BF16), conv_w[0]]
    wi = g_in.reshape(W_IN_COLS, D_MODEL)
    c0, c1, c2, c3, c4, c5 = (sum(W_IN_SIZES[:i + 1]) for i in range(6))
    w_in_pt = jnp.concatenate([wi[c4:c5], wi[c5:], wi[c1:c2], wi[:c0], wi[c2:c3], wi[c3:c4],
                               wi[c0:c0 + KV_LORA], wi[c0 + KV_LORA:c1], jnp.zeros((64, D_MODEL), BF16)], axis=0)
    wq = g_qb.reshape(H_A, QK_HEAD, Q_LORA)
    w_qb_pt = jnp.concatenate([wq[:, :QK_NOPE].reshape(H_A * QK_NOPE, Q_LORA), wq[:, QK_NOPE:].reshape(H_A * QK_ROPE, Q_LORA)], axis=0)
    w_kvb = _unstack_cols(g_kvb)

    half = QK_ROPE // 2
    inv_freq = ROPE_THETA ** (-jnp.arange(half, dtype=F32) / half)
    ang = positions.astype(F32)[:, None] * inv_freq[None, :]
    cos, sin = jnp.cos(ang), jnp.sin(ang)
    qa = jnp.arange(Q_BLOCK, dtype=jnp.int32)[:, None]
    kc = jnp.arange(SPAN, dtype=jnp.int32)[None, :]
    rel = (kc - WINDOW - qa).T
    in_band = (jnp.abs(rel) <= WINDOW).astype(F32).reshape(1, Q_BLOCK * SPAN)
    onehot_t = (_t5_bucket(rel).reshape(1, Q_BLOCK * SPAN) == jnp.arange(NUM_BUCKETS, dtype=jnp.int32)[:, None]).astype(F32)
    bias_t = _bias_table(rel_bias.T, onehot_t, in_band).reshape(H_B, SPAN, Q_BLOCK)
    sinks_b = jnp.broadcast_to(sinks.reshape(H_B, 1), (H_B, Q_BLOCK))

    (h1,) = _rowwise(lambda a, g: (_rms(a, g),), "norm1", S, TM, [_rows(x), _whole(norm1_g)], [("rows", D_MODEL, BF16)])
    proj = _matmul(h1, w_in_pt, "nt", BF16, "proj")

    def lat_fn(qlat, ckv, kr, gq, gkv, cs, sn):
        r1, r2 = _rope(kr[:, :half], kr[:, half:QK_ROPE], cs, sn)
        return _rms(qlat, gq), _rms(ckv, gkv), jnp.concatenate([r1, r2], axis=1)

    qn, ckvn, k_rope = _rowwise(lat_fn, "latents", S, TM,
                                [_rows(proj, 256, PROJ_QLAT), _rows(proj, 128, PROJ_CKV), _rows(proj, 128, PROJ_KROPE),
                                 _whole(q_a_norm_g), _whole(kv_a_norm_g), _rows(cos), _rows(sin)],
                                [("rows", Q_LORA, BF16), ("rows", KV_LORA, BF16), ("rows", QK_ROPE, BF16)])
    def q_heads_fn(q, cs, sn):
        q = q * MLA_PRESCALE
        outs = []
        for h in range(H_A):
            o = H_A * QK_NOPE + QK_ROPE * h
            r1, r2 = _rope(q[:, o:o + half], q[:, o + half:o + QK_ROPE], cs, sn)
            outs.append(jnp.concatenate([q[:, QK_NOPE * h:QK_NOPE * (h + 1)], r1, r2], axis=1)[None])
        return (jnp.concatenate(outs, axis=0),)

    (q_full,) = _matmul(qn, w_qb_pt, "nt", None, "q_up_heads", tm=512, tn=1536,
                        epi=(q_heads_fn, [_rows(cos), _rows(sin)], [("heads", H_A, QK_HEAD, BF16)]))

    def k_heads_fn(kvf, kr):
        return kvf, jnp.concatenate([jnp.concatenate([kvf[:, 256 * h:256 * h + QK_NOPE], kr], axis=1)[None] for h in range(H_A)], axis=0)

    kv, k_full = _matmul(ckvn, w_kvb, "nn", None, "kv_up_heads", tm=512, tn=2048,
                         epi=(k_heads_fn, [_rows(k_rope)], [("rows", H_A * (QK_NOPE + V_DIM), BF16), ("heads", H_A, QK_HEAD, BF16)]))
    (o_a, lse), (g_out, g_up, g_cw) = _mla_fwd(q_full, k_full, kv, S, comm=([], late_weights))
    w_out_f = g_out.reshape(D_MODEL, D_MODEL)
    w_up_t = g_up.reshape(2 * D_FF, D_MODEL)
    conv_w_f = _unstack_cols(g_cw)

    o_b, (g_down,) = _win_fwd(proj, bias_t, sinks_b, S, comm=([], [w_down[0].astype(BF16)]))
    w_down_f = g_down.reshape(D_FF, D_MODEL)

    (mixed,) = _rowwise(lambda ga, gb, oa, ob: (_sigmoid(ga) * oa + _sigmoid(gb) * ob,), "gate_mix", S, TM,
                        [_rows(proj, 1024, PROJ_GA), _rows(proj, 1024, PROJ_GB), _rows(o_a), _rows(o_b)], [("rows", D_MODEL, BF16)])
    x1, h2 = _matmul(mixed, w_out_f, "nn", None, "out_proj", residual=x, tm=512,
                     epi=(lambda a, g: (a, _rms(a, g)), [_whole(norm2_g)], [("rows", D_MODEL, F32), ("rows", D_MODEL, BF16)]))
    u = _matmul(h2, w_up_t, "nt", BF16, "ffn_up", tn=1408)
    act = _conv_gate_fwd(u, conv_w_f, conv_b, S)

    def final_fn(a, g, t):
        err = _rms(a, g) - t
        loss = 0.5 * jnp.sum(jnp.mean(err * err, axis=-1, keepdims=True), axis=0, keepdims=True)
        dx, dg = _rms_bwd(err * (1.0 / D_MODEL), a, g)
        return dx, dx, dg, jnp.broadcast_to(loss, (1, 128))

    gfin = final_norm_g.reshape(1, D_MODEL)
    dx2, dx2_b, d_gfin, loss_row = _matmul(
        act, w_down_f, "nn", None, "ffn_down_loss", residual=x1, tm=512,
        epi=(final_fn, [_whole(gfin), _rows(target)],
             [("rows", D_MODEL, F32), ("rows", D_MODEL, BF16), ("acc", 1, D_MODEL), ("acc", 1, 128)]))
    d_act = _matmul(dx2_b, w_down_f, "nt", BF16, "ffn_down_dx", tn=1408)
    d_w_down = _matmul(act, dx2_b, "tn", F32, "ffn_down_dw")
    du_g, du_v, dcw_g, dcw_v, dcb_g, dcb_v = _conv_gate_bwd(u, conv_w_f, conv_b, d_act, S)
    d_conv_b = jnp.concatenate([dcb_g, dcb_v], axis=1)

    def norm_bwd_fn(dh, a, g, dres):
        dx, dg = _rms_bwd(dh, a, g)
        dx = dx + dres
        return dx, dx, dg

    dx1, dx1_b, d_g2 = _matmul(du_g, w_up_t, "nn", None, "ffn_up_dx_norm2_bwd", tm=256, a2=du_v,
                               epi=(norm_bwd_fn, [_rows(x1), _whole(norm2_g), _rows(dx2)],
                                    [("rows", D_MODEL, F32), ("rows", D_MODEL, BF16), ("acc", 1, D_MODEL)]))
    d_w_up_g = _matmul(du_g, h2, "tn", F32, "ffn_up_dw_gate", tm=256)
    d_w_up_v = _matmul(du_v, h2, "tn", F32, "ffn_up_dw_value", tm=256)
    d_w_out = _matmul(mixed, dx1_b, "tn", F32, "out_proj_dw", tm=512)

    def gate_bwd_fn(dm, ga, gb, oa, ob):
        sa, sb = _sigmoid(ga), _sigmoid(gb)
        return jnp.concatenate([dm * oa * sa * (1.0 - sa), dm * ob * sb * (1.0 - sb)], axis=1), dm * sa, dm * sb

    d_proj, do_a, do_b = _matmul(
        dx1_b, w_out_f, "nt", None, "out_proj_dx_gate_bwd", tm=512,
        epi=(gate_bwd_fn, [_rows(proj, 1024, PROJ_GA), _rows(proj, 1024, PROJ_GB), _rows(o_a), _rows(o_b)],
             [("cols", 2 * D_MODEL, 0, PROJ_P, BF16), ("rows", D_MODEL, BF16), ("rows", D_MODEL, BF16)]))

    d_proj, dk_acc, dv_acc, d_bias, d_sinks_g = _win_bwd(proj, bias_t, sinks_b, do_b, d_proj, S)
    d_rel_bias = _bias_table_bwd(d_bias.reshape(H_B, Q_BLOCK * SPAN), onehot_t).T
    d_sinks = d_sinks_g.reshape(1, H_B)

    d_w_up_t = jnp.concatenate([d_w_up_g, d_w_up_v], axis=0).reshape(N_DEV, 2 * D_FF // N_DEV, D_MODEL)
    early = [d_w_out.reshape(N_DEV, D_MODEL // N_DEV, D_MODEL).astype(BF16), d_w_up_t.astype(BF16),
             d_w_down.reshape(N_DEV, D_FF // N_DEV, D_MODEL).astype(BF16), _stack_halves(dcw_g, dcw_v)]
    (dq_full, dk_full, dv_full), recv_early = _mla_bwd(q_full, k_full, kv, do_a, o_a, lse, S, comm=(early, []))

    def dq_post_fn(dq, cs, sn):
        nope = [dq[h, :, :QK_NOPE] for h in range(H_A)]
        rope = []
        for h in range(H_A):
            rope += list(_rope_bwd(dq[h, :, QK_NOPE:QK_NOPE + half], dq[h, :, QK_NOPE + half:], cs, sn))
        return (jnp.concatenate(nope + rope, axis=1),)

    (dq_p,) = _rowwise(dq_post_fn, "dq_post", S, TM, [_heads(dq_full), _rows(cos), _rows(sin)], [("rows", H_A * QK_HEAD, BF16)])

    def dkv_post_fn(dk, dv, cs, sn):
        dk = dk * math.log(2.0)
        dkv = jnp.concatenate([jnp.concatenate([dk[h, :, :QK_NOPE], dv[h]], axis=1) for h in range(H_A)], axis=1)
        dkr = dk[0, :, QK_NOPE:]
        for h in range(1, H_A):
            dkr = dkr + dk[h, :, QK_NOPE:]
        r1, r2 = _rope_bwd(dkr[:, :half], dkr[:, half:], cs, sn)
        return dkv, jnp.concatenate([r1, r2], axis=1)

    dkv, d_krope = _rowwise(dkv_post_fn, "dkv_post", S, TM, [_heads(dk_full), _heads(dv_full), _rows(cos), _rows(sin)],
                            [("rows", H_A * (QK_NOPE + V_DIM), BF16), ("rows", QK_ROPE, F32)])
    d_qn = _matmul(dq_p, w_qb_pt, "nn", F32, "q_up_dx")
    d_w_qb_pt = _matmul(dq_p, qn, "tn", F32, "q_up_dw", tm=512)
    d_ckvn = _matmul(dkv, w_kvb, "nt", F32, "kv_up_dx")
    d_w_kvb = _matmul(ckvn, dkv, "tn", F32, "kv_up_dw", tn=2048)

    def lat_bwd_fn(dqn, dckvn, dkr, qlat, ckv, gq, gkv, dkb, dvb):
        dql, dgq = _rms_bwd(dqn, qlat, gq)
        dck, dgkv = _rms_bwd(dckvn, ckv, gkv)
        tail = jnp.concatenate([dql, dkb, dvb, dck, dkr, jnp.zeros_like(dkr)], axis=1)
        return tail, dgq, dgkv

    shifted = lambda arr: (arr, lambda tm: pl.BlockSpec((tm, arr.shape[1]), lambda i, *_: (i + WINDOW // tm, 0)))
    TL = min(128, S)
    d_proj, d_gq, d_gkv = _rowwise(lat_bwd_fn, "latents_bwd", S, TL,
                                   [_rows(d_qn), _rows(d_ckvn), _rows(d_krope), _rows(proj, 256, PROJ_QLAT), _rows(proj, 128, PROJ_CKV),
                                    _whole(q_a_norm_g), _whole(kv_a_norm_g), shifted(dk_acc), shifted(dv_acc)],
                                   [("cols", 1024, 3, PROJ_P, BF16), ("acc", 1, Q_LORA), ("acc", 1, KV_LORA)], into=(d_proj, 0))
    dp = _matmul(d_proj, h1, "tn", F32, "proj_dw", tm=512)

    d_w_in_t = jnp.concatenate([dp[3072:3328], dp[3840:3968], dp[3968:4032], dp[2048:3072], dp[3328:3584],
                                dp[3584:3840], dp[0:1024], dp[1024:2048]], axis=0)
    d_w_qb_t = jnp.concatenate([d_w_qb_pt[:H_A * QK_NOPE].reshape(H_A, QK_NOPE, Q_LORA),
                                d_w_qb_pt[H_A * QK_NOPE:].reshape(H_A, QK_ROPE, Q_LORA)], axis=1)
    n_in, n_qb, n_kvb = W_IN_COLS // N_DEV, QK_HEAD * Q_LORA // D_MODEL, KV_LORA * 256 // D_MODEL
    n_in_pad = -(-n_in // 16) * 16
    late = jnp.concatenate([d_w_in_t.reshape(N_DEV, n_in, D_MODEL), jnp.zeros((N_DEV, n_in_pad - n_in, D_MODEL), F32),
                            d_w_qb_t.reshape(N_DEV, n_qb, D_MODEL), _stack_cols(d_w_kvb).reshape(N_DEV, n_kvb, D_MODEL)],
                           axis=1).astype(BF16)
    late_sems, late_src, late_land, started = _exchange_start(late, "late_grads_start")

    def norm1_bwd_fn(dh, a, g, dres):
        dx, dg = _rms_bwd(dh, a, g)
        return dx + dres, dg

    grad_x, d_g1 = _matmul(
        d_proj, w_in_pt, "nn", None, "proj_dx_norm1_bwd", tm=512,
        epi=(norm1_bwd_fn, [_rows(x), _whole(norm1_g + started[:1, :1]), _rows(dx1)], [("rows", D_MODEL, F32), ("acc", 1, D_MODEL)]))

    small_parts = [d_g1, d_gq, d_gkv, d_rel_bias.reshape(1, NUM_BUCKETS * H_B), d_sinks, d_g2, d_conv_b, d_gfin, loss_row[:, :1]]
    small = jnp.concatenate(small_parts, axis=1)
    n_small = small.shape[1]
    pad = (-n_small) % 128
    small = jnp.pad(small, ((0, 0), (0, pad)))
    (recv_small,) = _exchange([], [small], "exchange_small_grads")

    def flat(a):
        return a.reshape(1, -1)

    small_w = [norm1_g, q_a_norm_g, kv_a_norm_g, rel_bias, sinks, norm2_g, conv_b, final_norm_g]
    small_m = [m_norm1_g, m_q_a_norm_g, m_kv_a_norm_g, m_rel_bias, m_sinks, m_norm2_g, m_conv_b, m_final_norm_g]
    small_v = [v_norm1_g, v_q_a_norm_g, v_kv_a_norm_g, v_rel_bias, v_sinks, v_norm2_g, v_conv_b, v_final_norm_g]
    cat = lambda parts: jnp.pad(jnp.concatenate([flat(a) for a in parts], axis=1), ((0, 0), (0, pad + 1)))[None]
    sm = _adamw(recv_small, cat(small_w), cat(small_m), cat(small_v), "adamw_small")
    transposed = ("w_in", "w_q_b", "w_up")
    early_names = ["w_out", "w_up", "w_down", "conv_w"]
    early_wmv = [(w_out, m_w_out, v_w_out), (tr(w_up), tr(m_w_up), tr(v_w_up)), (w_down, m_w_down, v_w_down), (conv_w, m_conv_w, v_conv_w)]
    big = {n: _adamw(r, *wmv, "adamw_" + n) for n, r, wmv in zip(early_names, recv_early, early_wmv)}

    after = sm[0][0, :1, :128] + big["w_up"][0][0, :1, :128] + big["w_down"][0][0, :1, :128]
    landed = _exchange_wait(late_sems, late_src, late_land, after, "late_grads_wait")
    me = 4 * lax.axis_index("x") + 2 * lax.axis_index("y") + lax.axis_index("c")
    landed = lax.dynamic_update_slice_in_dim(landed, lax.dynamic_slice_in_dim(late, me, 1, axis=0), me, axis=0)
    recv_late = [landed, landed[:, n_in_pad:n_in_pad + n_qb].reshape(N_DEV, QK_HEAD, Q_LORA),
                 landed[:, n_in_pad + n_qb:].reshape(N_DEV, KV_LORA, 256)]
    late_names = ["w_in", "w_q_b", "w_kv_b"]
    late_wmv = [(tr(w_in), tr(m_w_in), tr(v_w_in)), (tr(w_q_b), tr(m_w_q_b), tr(v_w_q_b)), (w_kv_b, m_w_kv_b, v_w_kv_b)]
    big.update({n: _adamw(r, *wmv, "adamw_" + n) for n, r, wmv in zip(late_names, recv_late, late_wmv)})

    loss = sm[0][0, 0, n_small - 1]
    order =["norm1_g", "w_in", "q_a_norm_g", "w_q_b", "kv_a_norm_g", "w_kv_b", "rel_bias", "sinks", "w_out", "norm2_g", "w_up",
             "conv_w", "conv_b", "w_down", "final_norm_g"]
    small_names = ["norm1_g", "q_a_norm_g", "kv_a_norm_g", "rel_bias", "sinks", "norm2_g", "conv_b", "final_norm_g"]
    offs, o = {}, 0
    for n, a in zip(small_names, small_w):
        offs[n] = (o, a.size, a.shape)
        o += a.size
    outs = [loss, grad_x[None]]
    for kind in range(4):
        for n in order:
            if n in big:
                outs.append(tr(big[n][kind]) if n in transposed else big[n][kind])
            else:
                o, size, shape = offs[n]
                outs.append(sm[kind][0, 0, o:o + size].reshape(shape))
    return tuple(outs)
```

```python
import math

import jax
import jax.numpy as jnp
from jax import lax
from jax.experimental import pallas as pl
from jax.experimental.pallas import tpu as pltpu

F32 = jnp.float32
BF16 = jnp.bfloat16

N_DEV = 8
D_MODEL = 1024
EPS = 1e-6
H_A, QK_NOPE, QK_ROPE, V_DIM, Q_LORA, KV_LORA = 8, 128, 64, 128, 256, 128
QK_HEAD = QK_NOPE + QK_ROPE
ROPE_THETA = 10000.0
H_B, KV_B, GROUP, HD_B, WINDOW, Q_BLOCK = 16, 4, 4, 64, 128, 128
SPAN = Q_BLOCK + 2 * WINDOW
NUM_BUCKETS, MAX_DISTANCE = 32, 128
D_FF = 2816
ADAM_LR, ADAM_B1, ADAM_B2, ADAM_EPS, ADAM_WD, ADAM_STEP = 0.001, 0.9, 0.999, 1e-08, 0.01, 10

W_IN_SIZES = (Q_LORA, KV_LORA + QK_ROPE, H_B * HD_B, KV_B * HD_B, KV_B * HD_B, D_MODEL, D_MODEL)
W_IN_COLS = sum(W_IN_SIZES)
PROJ_P = 4096
PROJ_GA, PROJ_GB, PROJ_QB, PROJ_QLAT, PROJ_KB, PROJ_VB, PROJ_CKV, PROJ_KROPE = 0, 1, 2, 12, 13, 14, 30, 31

VMEM_LIMIT = 56 * 1024 * 1024

NN = (((1,), (0,)), ((), ()))
NT = (((1,), (1,)), ((), ()))
TN = (((0,), (0,)), ((), ()))


def _pcall(body, *, name, grid, in_specs, out_specs, out_shape, scratch_shapes=(), dims=None, comm=None, aliases=None):
    if comm is None:
        params = pltpu.CompilerParams(dimension_semantics=dims, vmem_limit_bytes=VMEM_LIMIT)
        return pl.pallas_call(body, name=name, grid=grid, in_specs=in_specs, out_specs=out_specs, out_shape=out_shape,
                              scratch_shapes=list(scratch_shapes), input_output_aliases=aliases or {}, compiler_params=params)
    assert not aliases
    stacked, replicated = comm
    arrs = [*stacked, *replicated]
    n_st, n_arr = len(stacked), len(arrs)
    single = not isinstance(out_specs, (list, tuple))
    o_specs, o_shape = ([out_specs], [out_shape]) if single else (list(out_specs), list(out_shape))
    n_in, n_out = len(in_specs), len(o_specs)

    def wrapped(*refs):
        c_in = refs[n_in:n_in + n_arr]
        c_out = refs[n_in + n_arr + n_out:n_in + 2 * n_arr + n_out]
        sems = refs[len(refs) - 3:]
        own = (*refs[:n_in], *refs[n_in + n_arr:n_in + n_arr + n_out], *refs[n_in + 2 * n_arr + n_out:len(refs) - 3])
        if not grid:
            _xchg_start(c_in, c_out, sems, n_st)
            _xchg_finish(c_in, c_out, sems, n_st)
            return
        first = last = None
        for d, n in enumerate(grid):
            pid = pl.program_id(d)
            first = (pid == 0) if first is None else first & (pid == 0)
            last = (pid == n - 1) if last is None else last & (pid == n - 1)

        @pl.when(first)
        def _():
            _xchg_start(c_in, c_out, sems, n_st)

        body(*own)

        @pl.when(last)
        def _():
            _xchg_finish(c_in, c_out, sems, n_st)

    params = pltpu.CompilerParams(dimension_semantics=("arbitrary",) * len(grid), vmem_limit_bytes=VMEM_LIMIT)
    call = pl.pallas_call(wrapped, name=name, grid=grid, in_specs=[*in_specs, *[ANY] * n_arr], out_specs=[*o_specs, *[ANY] * n_arr],
                          out_shape=[*o_shape, *_xchg_out_shapes(stacked, replicated)],
                          scratch_shapes=[*scratch_shapes, *_xchg_sems(n_arr)], compiler_params=params)

    def run(*args):
        res = call(*args, *arrs)
        outs, landed = res[:n_out], res[n_out:]
        return (outs[0] if single else outs), landed

    return run


def _dot(a, b, dn):
    return lax.dot_general(a, b, dn, preferred_element_type=F32)


def _tile(n, target):
    best = None
    for t in range(128, min(n, target) + 1, 128):
        if n % t == 0:
            best = t
    return n if best is None else best


def _matmul(a, b, mode, out_dtype, name, residual=None, tm=1024, tn=1024, comm=None, a2=None, epi=None):
    if mode == "nn":
        (M, K), N = a.shape, b.shape[1]
    elif mode == "nt":
        (M, K), N = a.shape, b.shape[0]
    else:
        (K, M), N = a.shape, b.shape[1]
    tm, tn = _tile(M, tm), _tile(N, tn)
    a_spec = pl.BlockSpec((K, tm), lambda i, j: (0, i)) if mode == "tn" else pl.BlockSpec((tm, K), lambda i, j: (i, 0))
    b_spec = pl.BlockSpec((tn, b.shape[1]), lambda i, j: (j, 0)) if mode == "nt" else pl.BlockSpec((K, tn), lambda i, j: (0, j))
    o_spec = pl.BlockSpec((tm, tn), lambda i, j: (i, j))
    in_specs, args = [a_spec, b_spec], [a, b]
    n1 = M // tm
    if a2 is not None and mode == "tn":
        assert M % tm == 0 and a2.shape[1] % tm == 0
        in_specs[0] = pl.BlockSpec((K, tm), lambda i, j: (0, jnp.minimum(i, n1 - 1)))
        in_specs.append(pl.BlockSpec((K, tm), lambda i, j: (0, jnp.maximum(i - n1, 0))))
        args.append(a2)
        M += a2.shape[1]
    elif a2 is not None:
        assert (mode == "nt" and K + a2.shape[1] == b.shape[1]) or (mode == "nn" and K + a2.shape[1] == b.shape[0])
        if mode == "nn":
            b_spec = in_specs[1] = pl.BlockSpec((b.shape[0], tn), lambda i, j: (0, j))
        in_specs.append(pl.BlockSpec((tm, a2.shape[1]), lambda i, j: (i, 0)))
        args.append(a2)
    if residual is not None:
        in_specs.append(o_spec)
        args.append(residual)
    n_mm = len(args)
    scratch = [pltpu.VMEM((tm, K), a.dtype)] if mode == "tn" else []
    if epi is None:
        out_specs, out_shape, is_acc = o_spec, jax.ShapeDtypeStruct((M, N), out_dtype), None
    else:
        assert tn == N
        fn, epi_ins, epi_outs = epi
        in_specs += [mk(tm) for _, mk in epi_ins]
        args += [arr for arr, _ in epi_ins]
        out_specs, out_shape, is_acc = _row_out_specs(epi_outs, M, tm)

    def body(*refs):
        a_ref, b_ref = refs[0], refs[1]
        n_out = 1 if epi is None else len(is_acc)
        out_refs = refs[len(args):len(args) + n_out]
        if mode == "tn":
            at_ref = refs[len(args) + n_out]

            first_col = pl.program_id(1) == 0
            from_a = first_col if a2 is None else first_col & (pl.program_id(0) < n1)

            @pl.when(from_a)
            def _():
                at_ref[...] = a_ref[...].T

            if a2 is not None:
                @pl.when(first_col & (pl.program_id(0) >= n1))
                def _():
                    at_ref[...] = refs[2][...].T

            acc = _dot(at_ref[...], b_ref[...], NN)
        elif a2 is not None and mode == "nt":
            acc = _dot(a_ref[...], b_ref[:, :K], NT) + _dot(refs[2][...], b_ref[:, K:], NT)
        elif a2 is not None:
            acc = _dot(a_ref[...], b_ref[:K, :], NN) + _dot(refs[2][...], b_ref[K:, :], NN)
        else:
            acc = _dot(a_ref[...], b_ref[...], NT if mode == "nt" else NN)
        if residual is not None:
            acc = acc + refs[n_mm - 1][...]
        if epi is None:
            out_refs[0][...] = acc.astype(out_dtype)
        else:
            _store_rows(out_refs, fn(acc, *[_load_f32(r) for r in refs[n_mm:len(args)]]), is_acc)

    return _pcall(body, name=name, grid=(M // tm, N // tn), in_specs=in_specs, out_specs=out_specs,
                  out_shape=out_shape, scratch_shapes=scratch,
                  dims=("arbitrary" if epi is not None else "parallel", "arbitrary"), comm=comm)(*args)


def _rows(arr, width=None, col=0):
    width = arr.shape[1] if width is None else width
    return (arr, lambda tm: pl.BlockSpec((tm, width), lambda i, *_: (i, col)))


def _heads(arr):
    return (arr, lambda tm: pl.BlockSpec((arr.shape[0], tm, arr.shape[2]), lambda i, *_: (0, i, 0)))


def _whole(arr):
    nd = arr.ndim
    return (arr, lambda tm: pl.BlockSpec(arr.shape, lambda i, *_: (0,) * nd))


def _row_out_specs(outs, n_rows, tm):
    out_specs, out_shape, is_acc = [], [], []
    for o in outs:
        if o[0] == "rows":
            out_specs.append(pl.BlockSpec((tm, o[1]), lambda i, *_: (i, 0)))
            out_shape.append(jax.ShapeDtypeStruct((n_rows, o[1]), o[2]))
        elif o[0] == "cols":
            out_specs.append(pl.BlockSpec((tm, o[1]), lambda i, *_, c=o[2]: (i, c)))
            out_shape.append(jax.ShapeDtypeStruct((n_rows, o[3]), o[4]))
        elif o[0] == "heads":
            out_specs.append(pl.BlockSpec((o[1], tm, o[2]), lambda i, *_: (0, i, 0)))
            out_shape.append(jax.ShapeDtypeStruct((o[1], n_rows, o[2]), o[3]))
        else:
            out_specs.append(pl.BlockSpec((o[1], o[2]), lambda i, *_: (0, 0)))
            out_shape.append(jax.ShapeDtypeStruct((o[1], o[2]), F32))
        is_acc.append(o[0] == "acc")
    return out_specs, out_shape, is_acc


def _load_f32(r):
    v = r[...]
    return v.astype(F32) if v.dtype == BF16 else v


def _store_rows(out_refs, vals, is_acc):
    for r, v, acc in zip(out_refs, vals, is_acc):
        if acc:
            @pl.when(pl.program_id(0) == 0)
            def _():
                r[...] = jnp.zeros_like(r)

            r[...] += v
        else:
            r[...] = v.astype(r.dtype)


def _rowwise(fn, name, n_rows, tm, ins, outs, upcast=True, into=None):
    tm = min(tm, n_rows)
    assert n_rows % tm == 0
    in_specs = [mk(tm) for _, mk in ins]
    out_specs, out_shape, is_acc = _row_out_specs(outs, n_rows, tm)
    n_in = len(ins)
    args = [a for a, _ in ins]
    aliases = {}
    if into is not None:
        in_specs.append(ANY)
        args.append(into[0])
        aliases = {n_in: into[1]}

    def body(*refs):
        vals = fn(*[_load_f32(r) if upcast else r[...] for r in refs[:n_in]])
        _store_rows(refs[len(args):], vals, is_acc)

    return _pcall(body, name=name, grid=(n_rows // tm,), in_specs=in_specs, out_specs=out_specs,
                  out_shape=out_shape, dims=("arbitrary",), aliases=aliases)(*args)


def _rms(x, g):
    r = lax.rsqrt(jnp.mean(x * x, axis=-1, keepdims=True) + EPS)
    return x * r * g


def _rms_bwd(dy, x, g):
    r = lax.rsqrt(jnp.mean(x * x, axis=-1, keepdims=True) + EPS)
    xhat = x * r
    dxhat = dy * g
    dx = r * (dxhat - xhat * jnp.mean(dxhat * xhat, axis=-1, keepdims=True))
    return dx, jnp.sum(dy * xhat, axis=0, keepdims=True)


def _rope(x1, x2, cos, sin):
    return x1 * cos - x2 * sin, x2 * cos + x1 * sin


def _rope_bwd(d1, d2, cos, sin):
    return d1 * cos + d2 * sin, d2 * cos - d1 * sin


def _sigmoid(x):
    return 1.0 / (1.0 + jnp.exp(-x))


MLA_SCALE = 1.0 / math.sqrt(QK_HEAD)
MLA_PRESCALE = MLA_SCALE * math.log2(math.e)
MLA_TQ, MLA_KC = 1024, 1024


def _mla_fwd(q_full, k_full, kv, S, comm=None):
    tq, kc = min(MLA_TQ, S), min(MLA_KC, S)

    def body(q_ref, k_ref, v_ref, o_ref, lse_ref):
        q = q_ref[0]
        m = jnp.full((tq, 1), -1e30, F32)
        l = jnp.zeros((tq, 1), F32)
        acc = jnp.zeros((tq, V_DIM), F32)
        for c in range(S // kc):
            s = _dot(q, k_ref[0, c * kc:(c + 1) * kc, :], NT)
            m_new = jnp.maximum(m, jnp.max(s, axis=-1, keepdims=True))
            alpha = jnp.exp2(m - m_new)
            p = jnp.exp2(s - m_new)
            l = alpha * l + jnp.sum(p, axis=-1, keepdims=True)
            acc = alpha * acc + _dot(p.astype(BF16), v_ref[c * kc:(c + 1) * kc, :], NN)
            m = m_new
        o_ref[...] = (acc / l).astype(BF16)
        lse_ref[0] = m + jnp.log2(l)

    return _pcall(
        body, name="mla_fwd", grid=(H_A, S // tq),
        in_specs=[pl.BlockSpec((1, tq, QK_HEAD), lambda h, i: (h, i, 0)),
                  pl.BlockSpec((1, S, QK_HEAD), lambda h, i: (h, 0, 0)),
                  pl.BlockSpec((S, V_DIM), lambda h, i: (0, 2 * h + 1))],
        out_specs=[pl.BlockSpec((tq, V_DIM), lambda h, i: (i, h)),
                   pl.BlockSpec((1, tq, 1), lambda h, i: (h, i, 0))],
        out_shape=[jax.ShapeDtypeStruct((S, H_A * V_DIM), BF16), jax.ShapeDtypeStruct((H_A, S, 1), F32)],
        dims=("parallel", "parallel"), comm=comm)(q_full, k_full, kv)


def _mla_bwd(q_full, k_full, kv, do_a, o_a, lse, S, comm=None):
    tq, kc = min(MLA_TQ, S), min(MLA_KC, S)

    def body(q_ref, k_ref, v_ref, do_ref, o_ref, lse_ref, dq_ref, dk_ref, dv_ref):
        @pl.when(pl.program_id(1) == 0)
        def _():
            dk_ref[...] = jnp.zeros_like(dk_ref)
            dv_ref[...] = jnp.zeros_like(dv_ref)

        q = q_ref[0]
        do = do_ref[...]
        lse_q = lse_ref[0]
        delta = jnp.sum(do.astype(F32) * o_ref[...].astype(F32), axis=-1, keepdims=True)
        dq = jnp.zeros((tq, QK_HEAD), F32)
        for c in range(S // kc):
            k = k_ref[0, c * kc:(c + 1) * kc, :]
            v = v_ref[c * kc:(c + 1) * kc, :]
            p = jnp.exp2(_dot(q, k, NT) - lse_q)
            ds = (p * (_dot(do, v, NT) - delta)).astype(BF16)
            dq = dq + _dot(ds, k, NN)
            dk_ref[0, c * kc:(c + 1) * kc, :] += _dot(ds, q, TN)
            dv_ref[0, c * kc:(c + 1) * kc, :] += _dot(p.astype(BF16), do, TN)
        dq_ref[0] = dq * MLA_SCALE

    return _pcall(
        body, name="mla_bwd", grid=(H_A, S // tq),
        in_specs=[pl.BlockSpec((1, tq, QK_HEAD), lambda h, i: (h, i, 0)),
                  pl.BlockSpec((1, S, QK_HEAD), lambda h, i: (h, 0, 0)),
                  pl.BlockSpec((S, V_DIM), lambda h, i: (0, 2 * h + 1)),
                  pl.BlockSpec((tq, V_DIM), lambda h, i: (i, h)),
                  pl.BlockSpec((tq, V_DIM), lambda h, i: (i, h)),
                  pl.BlockSpec((1, tq, 1), lambda h, i: (h, i, 0))],
        out_specs=[pl.BlockSpec((1, tq, QK_HEAD), lambda h, i: (h, i, 0)),
                   pl.BlockSpec((1, S, QK_HEAD), lambda h, i: (h, 0, 0)),
                   pl.BlockSpec((1, S, V_DIM), lambda h, i: (h, 0, 0))],
        out_shape=[jax.ShapeDtypeStruct((H_A, S, QK_HEAD), F32), jax.ShapeDtypeStruct((H_A, S, QK_HEAD), F32),
                   jax.ShapeDtypeStruct((H_A, S, V_DIM), F32)],
        dims=("parallel", "arbitrary"), comm=comm)(q_full, k_full, kv, do_a, o_a, lse)


WIN_SCALE = 1.0 / math.sqrt(HD_B)


def _win_specs(S):
    last = S // Q_BLOCK - 1
    qspec = pl.BlockSpec((Q_BLOCK, H_B * HD_B), lambda n: (n, PROJ_QB))
    kspecs = [[pl.BlockSpec((Q_BLOCK, KV_B * HD_B), lambda n, d=d, c=c: (jnp.clip(n + d, 0, last), c)) for d in (-1, 0, 1)]
              for c in (PROJ_KB, PROJ_VB)]
    bias_spec = pl.BlockSpec((H_B, SPAN, Q_BLOCK), lambda n: (0, 0, 0))
    sink_spec = pl.BlockSpec((H_B, Q_BLOCK), lambda n: (0, 0))
    return qspec, kspecs, bias_spec, sink_spec


def _win_edge_ok(n, n_blk):
    row = lax.broadcasted_iota(jnp.int32, (SPAN, 1), 0)
    return jnp.logical_not(((n == 0) & (row < WINDOW)) | ((n == n_blk - 1) & (row >= SPAN - WINDOW)))


def _lanes4(pieces):
    return jnp.concatenate(pieces, axis=1)


def _win_probs(kg, q4t, bias_ref, sink_ref, g, edge_ok):
    bias4 = _lanes4([bias_ref[GROUP * g + j] for j in range(GROUP)])
    sink4 = _lanes4([sink_ref[GROUP * g + j:GROUP * g + j + 1, :] for j in range(GROUP)])
    s = jnp.where(edge_ok, _dot(kg, q4t, NN) + bias4, -1e30)
    m = jnp.maximum(jnp.max(s, axis=0, keepdims=True), sink4)
    p = jnp.exp(s - m)
    e_sink = jnp.exp(sink4 - m)
    inv_l = 1.0 / (jnp.sum(p, axis=0, keepdims=True) + e_sink)
    return p * inv_l, e_sink * inv_l


def _group_t(xt, g):
    return _lanes4([xt[HD_B * (GROUP * g + j):HD_B * (GROUP * g + j + 1), :] for j in range(GROUP)])


def _win_fwd(proj, bias_t, sinks_b, S, comm=None):
    n_blk = S // Q_BLOCK
    qspec, kspecs, bias_spec, sink_spec = _win_specs(S)

    def body(q_ref, k0, k1, k2, v0, v1, v2, bias_ref, sink_ref, o_ref):
        n = pl.program_id(0)
        edge_ok = _win_edge_ok(n, n_blk)
        k = jnp.concatenate([k0[...], k1[...], k2[...]], axis=0)
        vt = jnp.concatenate([v0[...], v1[...], v2[...]], axis=0).T
        qt = (q_ref[...].astype(F32) * WIN_SCALE).T.astype(BF16)
        parts = []
        for g in range(KV_B):
            p, _ = _win_probs(k[:, HD_B * g:HD_B * (g + 1)], _group_t(qt, g), bias_ref, sink_ref, g, edge_ok)
            o4t = _dot(vt[HD_B * g:HD_B * (g + 1), :], p.astype(BF16), NN)
            parts += [o4t[:, Q_BLOCK * j:Q_BLOCK * (j + 1)] for j in range(GROUP)]
        o_ref[...] = jnp.concatenate(parts, axis=0).T.astype(BF16)

    return _pcall(body, name="win_fwd", grid=(n_blk,),
                  in_specs=[qspec, *kspecs[0], *kspecs[1], bias_spec, sink_spec],
                  out_specs=pl.BlockSpec((Q_BLOCK, H_B * HD_B), lambda n: (n, 0)),
                  out_shape=jax.ShapeDtypeStruct((S, H_B * HD_B), BF16),
                  dims=("parallel",), comm=comm)(*[proj] * 7, bias_t, sinks_b)


def _win_bwd(proj, bias_t, sinks_b, do_b, d_proj, S):
    n_blk = S // Q_BLOCK
    qspec, kspecs, bias_spec, sink_spec = _win_specs(S)

    def body(q_ref, k0, k1, k2, v0, v1, v2, bias_ref, sink_ref, do_ref, _, dq_ref, dk_ref, dv_ref, dbias_ref, dsink_ref, dsink_acc):
        n = pl.program_id(0)

        @pl.when(n == 0)
        def _():
            dk_ref[...] = jnp.zeros_like(dk_ref)
            dv_ref[...] = jnp.zeros_like(dv_ref)
            dbias_ref[...] = jnp.zeros_like(dbias_ref)
            dsink_acc[...] = jnp.zeros_like(dsink_acc)

        edge_ok = _win_edge_ok(n, n_blk)
        k = jnp.concatenate([k0[...], k1[...], k2[...]], axis=0)
        v = jnp.concatenate([v0[...], v1[...], v2[...]], axis=0)
        kt = k.T
        qt = (q_ref[...].astype(F32) * WIN_SCALE).T.astype(BF16)
        dot_ = do_ref[...].astype(F32).T.astype(BF16)
        dq_parts, dks, dvs = [], [], []
        for g in range(KV_B):
            kg, vg = k[:, HD_B * g:HD_B * (g + 1)], v[:, HD_B * g:HD_B * (g + 1)]
            q4t, do4t = _group_t(qt, g), _group_t(dot_, g)
            p, p_sink = _win_probs(kg, q4t, bias_ref, sink_ref, g, edge_ok)
            dp = _dot(vg, do4t, NN)
            delta = jnp.sum(p * dp, axis=0, keepdims=True)
            ds = p * (dp - delta)
            for j in range(GROUP):
                dbias_ref[GROUP * g + j] += ds[:, Q_BLOCK * j:Q_BLOCK * (j + 1)]
            dsink_acc[g:g + 1, :] += -p_sink * delta
            dsb = ds.astype(BF16)
            dq4t = _dot(kt[HD_B * g:HD_B * (g + 1), :], dsb, NN) * WIN_SCALE
            dq_parts += [dq4t[:, Q_BLOCK * j:Q_BLOCK * (j + 1)] for j in range(GROUP)]
            dks.append(_dot(dsb, q4t, NT))
            dvs.append(_dot(p.astype(BF16), do4t, NT))
        dq_ref[...] = jnp.concatenate(dq_parts, axis=0).T.astype(BF16)
        rows = pl.ds(pl.multiple_of(n * Q_BLOCK, Q_BLOCK), SPAN)
        dk_ref[rows, :] += jnp.concatenate(dks, axis=1)
        dv_ref[rows, :] += jnp.concatenate(dvs, axis=1)

        @pl.when(n == n_blk - 1)
        def _():
            acc = dsink_acc[...]
            dsink_ref[...] = jnp.concatenate(
                [jnp.sum(acc[:, Q_BLOCK * j:Q_BLOCK * (j + 1)], axis=1, keepdims=True) for j in range(GROUP)], axis=1)

    whole = lambda shape: pl.BlockSpec(shape, lambda n: (0,) * len(shape))
    return _pcall(
        body, name="win_bwd", grid=(n_blk,),
        in_specs=[qspec, *kspecs[0], *kspecs[1], bias_spec, sink_spec, pl.BlockSpec((Q_BLOCK, H_B * HD_B), lambda n: (n, 0)), ANY],
        out_specs=[qspec, whole((S + 2 * WINDOW, KV_B * HD_B)),
                   whole((S + 2 * WINDOW, KV_B * HD_B)), whole((H_B, SPAN, Q_BLOCK)), whole((KV_B, GROUP))],
        out_shape=[jax.ShapeDtypeStruct((S, PROJ_P), BF16), jax.ShapeDtypeStruct((S + 2 * WINDOW, KV_B * HD_B), F32),
                   jax.ShapeDtypeStruct((S + 2 * WINDOW, KV_B * HD_B), F32), jax.ShapeDtypeStruct((H_B, SPAN, Q_BLOCK), F32),
                   jax.ShapeDtypeStruct((KV_B, GROUP), F32)],
        scratch_shapes=[pltpu.VMEM((KV_B, GROUP * Q_BLOCK), F32)],
        dims=("arbitrary",), aliases={10: 0})(*[proj] * 7, bias_t, sinks_b, do_b, d_proj)


def _bias_table(rel_bias_t, onehot_t, in_band):
    def body(rb_ref, oh_ref, band_ref, o_ref):
        t = lax.dot_general(rb_ref[...], oh_ref[...], NN, preferred_element_type=F32, precision=lax.Precision.HIGHEST)
        o_ref[...] = jnp.where(band_ref[...] > 0.5, t, -1e30)

    n = onehot_t.shape[1]
    tn = _tile(n, 8192)
    return _pcall(body, name="bias_table", grid=(n // tn,),
                  in_specs=[pl.BlockSpec((H_B, NUM_BUCKETS), lambda j: (0, 0)), pl.BlockSpec((NUM_BUCKETS, tn), lambda j: (0, j)),
                            pl.BlockSpec((1, tn), lambda j: (0, j))],
                  out_specs=pl.BlockSpec((H_B, tn), lambda j: (0, j)),
                  out_shape=jax.ShapeDtypeStruct((H_B, n), F32), dims=("parallel",))(rel_bias_t, onehot_t, in_band)


def _bias_table_bwd(dbias, onehot_t):
    n = onehot_t.shape[1]
    tk = _tile(n, 8192)

    def body(d_ref, oh_ref, o_ref):
        @pl.when(pl.program_id(0) == 0)
        def _():
            o_ref[...] = jnp.zeros_like(o_ref)

        o_ref[...] += lax.dot_general(d_ref[...], oh_ref[...], NT, preferred_element_type=F32, precision=lax.Precision.HIGHEST)

    return _pcall(body, name="bias_table_bwd", grid=(n // tk,),
                  in_specs=[pl.BlockSpec((H_B, tk), lambda j: (0, j)), pl.BlockSpec((NUM_BUCKETS, tk), lambda j: (0, j))],
                  out_specs=pl.BlockSpec((H_B, NUM_BUCKETS), lambda j: (0, 0)),
                  out_shape=jax.ShapeDtypeStruct((H_B, NUM_BUCKETS), F32), dims=("arbitrary",))(dbias, onehot_t)


CONV_STRIP = 128
N_STRIPS = D_FF // CONV_STRIP
CONV_ROWS = 128
HALO = 8


def _strip(rows, half):
    return pl.BlockSpec((rows, CONV_STRIP), lambda j: (0, j + half * N_STRIPS))


def _fill_padded(pad_ref, src_ref, halo, S):
    pad_ref[0:halo, :] = jnp.zeros((halo, CONV_STRIP), F32)
    pad_ref[halo + S:2 * halo + S, :] = jnp.zeros((halo, CONV_STRIP), F32)
    pad_ref[halo:halo + S, :] = src_ref[...].astype(F32)


def _taps(ext, n):
    m = ext.shape[0]
    return pltpu.roll(ext, 1, axis=0)[HALO:HALO + n], ext[HALO:HALO + n], pltpu.roll(ext, m - 1, axis=0)[HALO:HALO + n]


def _conv_gate_fwd(u, conv_w, conv_b, S):
    R = min(CONV_ROWS, S)

    def body(ug_ref, uv_ref, wg_ref, wv_ref, bg_ref, bv_ref, a_ref, gpad, vpad):
        _fill_padded(gpad, ug_ref, HALO, S)
        _fill_padded(vpad, uv_ref, HALO, S)
        wg, wv, bg, bv = wg_ref[...], wv_ref[...], bg_ref[...], bv_ref[...]

        def conv(pad_ref, r0, w, b):
            dn, mid, up = (pad_ref[pl.ds(r0 + HALO + d, R), :] for d in (-1, 0, 1))
            return dn * w[0:1, :] + mid * w[1:2, :] + up * w[2:3, :] + b

        def step(c, carry):
            r0 = pl.multiple_of(c * R, R)
            g = conv(gpad, r0, wg, bg)
            val = conv(vpad, r0, wv, bv)
            a_ref[pl.ds(r0, R), :] = (g * _sigmoid(g) * val).astype(BF16)
            return carry

        lax.fori_loop(0, S // R, step, 0)

    return _pcall(body, name="conv_gate_fwd", grid=(N_STRIPS,),
                  in_specs=[_strip(S, 0), _strip(S, 1), _strip(3, 0), _strip(3, 1), _strip(1, 0), _strip(1, 1)],
                  out_specs=_strip(S, 0), out_shape=jax.ShapeDtypeStruct((S, D_FF), BF16),
                  scratch_shapes=[pltpu.VMEM((S + 2 * HALO, CONV_STRIP), F32)] * 2,
                  dims=("parallel",))(u, u, conv_w, conv_w, conv_b, conv_b)


def _conv_gate_bwd(u, conv_w, conv_b, da, S):
    R = min(CONV_ROWS, S)
    n = R + 2 * HALO

    def body(ug_ref, uv_ref, wg_ref, wv_ref, bg_ref, bv_ref, da_ref, dug_ref, duv_ref, dwg_ref, dwv_ref, dbg_ref, dbv_ref,
             gpad, vpad, dapad):
        _fill_padded(gpad, ug_ref, 2 * HALO, S)
        _fill_padded(vpad, uv_ref, 2 * HALO, S)
        _fill_padded(dapad, da_ref, HALO, S)
        wg, wv, bg, bv = wg_ref[...], wv_ref[...], bg_ref[...], bv_ref[...]

        def conv(pad_ref, r0, w, b):
            dn, mid, up = (pad_ref[pl.ds(r0 + HALO + d, n), :] for d in (-1, 0, 1))
            return dn * w[0:1, :] + mid * w[1:2, :] + up * w[2:3, :] + b, mid[HALO:HALO + R]

        def conv_bwd(duc, u_mid, w, r0, du_ref):
            dn, mid, up = _taps(duc, R)
            du_ref[pl.ds(r0, R), :] = (up * w[0:1, :] + mid * w[1:2, :] + dn * w[2:3, :]).astype(BF16)
            dw = jnp.concatenate([jnp.sum(up * u_mid, axis=0, keepdims=True), jnp.sum(mid * u_mid, axis=0, keepdims=True),
                                  jnp.sum(dn * u_mid, axis=0, keepdims=True)], axis=0)
            return dw, jnp.sum(mid, axis=0, keepdims=True)

        def step(c, carry):
            dw_g, db_g, dw_v, db_v = carry
            r0 = pl.multiple_of(c * R, R)
            g, ug_mid = conv(gpad, r0, wg, bg)
            val, uv_mid = conv(vpad, r0, wv, bv)
            da_ext = dapad[pl.ds(r0, n), :]
            sg = _sigmoid(g)
            ddw_v, ddb_v = conv_bwd(da_ext * (g * sg), uv_mid, wv, r0, duv_ref)
            ddw_g, ddb_g = conv_bwd(da_ext * val * (sg * (1.0 + g * (1.0 - sg))), ug_mid, wg, r0, dug_ref)
            return dw_g + ddw_g, db_g + ddb_g, dw_v + ddw_v, db_v + ddb_v

        z3, z1 = jnp.zeros((3, CONV_STRIP), F32), jnp.zeros((1, CONV_STRIP), F32)
        dwg_ref[...], dbg_ref[...], dwv_ref[...], dbv_ref[...] = lax.fori_loop(0, S // R, step, (z3, z1, z3, z1))

    half = lambda r, dt: (_strip(r, 0), jax.ShapeDtypeStruct((r, D_FF), dt))
    outs = [half(S, BF16), half(S, BF16), half(3, F32), half(3, F32), half(1, F32), half(1, F32)]
    return _pcall(
        body, name="conv_gate_bwd", grid=(N_STRIPS,),
        in_specs=[_strip(S, 0), _strip(S, 1), _strip(3, 0), _strip(3, 1), _strip(1, 0), _strip(1, 1), _strip(S, 0)],
        out_specs=[o[0] for o in outs], out_shape=[o[1] for o in outs],
        scratch_shapes=[pltpu.VMEM((S + 4 * HALO, CONV_STRIP), F32)] * 2 + [pltpu.VMEM((S + 2 * HALO, CONV_STRIP), F32)],
        dims=("parallel",))(u, u, conv_w, conv_w, conv_b, conv_b, da)


MESH = pl.DeviceIdType.MESH
ANY = pl.BlockSpec(memory_space=pl.ANY)


def _place():
    return lax.axis_index("x"), lax.axis_index("y"), lax.axis_index("c")


def _all_gather(shards):
    n_arr = len(shards)

    def body(*refs):
        ins, outs = refs[:n_arr], refs[n_arr:2 * n_arr]
        send_sems, recv_sems, local_sems = refs[2 * n_arr:]
        x, y, c = _place()
        me, sibling = (x, y, c), (x, y, 1 - c)
        chips = [(1 - x, y), (x, 1 - y), (1 - x, 1 - y)]

        def slot(a, p):
            return outs[a].at[4 * p[0] + 2 * p[1] + p[2]]

        def copy(a, k, block, to, src=None):
            return pltpu.make_async_remote_copy(
                src_ref=slot(a, block) if src is None else src, dst_ref=slot(a, block),
                send_sem=send_sems.at[a, k], recv_sem=recv_sems.at[a, k], device_id=to, device_id_type=MESH)

        mine = [pltpu.make_async_copy(ins[a], slot(a, me), local_sems.at[a]) for a in range(n_arr)]
        for cp in mine:
            cp.start()
        first = []
        for a in range(n_arr):
            first.append(copy(a, 0, me, sibling, src=ins[a]))
            first += [copy(a, 1 + j, me, (*chip, c), src=ins[a]) for j, chip in enumerate(chips)]
        for cp in first:
            cp.start()
        passed = []
        for j, chip in enumerate(chips):
            for a in range(n_arr):
                copy(a, 1 + j, (*chip, c), me).wait_recv()
                cp = copy(a, 4 + j, (*chip, c), sibling)
                cp.start()
                passed.append(cp)
        for a in range(n_arr):
            copy(a, 0, sibling, me).wait_recv()
            for j, chip in enumerate(chips):
                copy(a, 4 + j, (*chip, 1 - c), me).wait_recv()
        for cp in first + passed:
            cp.wait_send()
        for cp in mine:
            cp.wait()

    return pl.pallas_call(
        body, name="all_gather_weights",
        in_specs=[ANY] * n_arr, out_specs=[ANY] * n_arr,
        out_shape=[jax.ShapeDtypeStruct((N_DEV, *s.shape), s.dtype) for s in shards],
        scratch_shapes=[pltpu.SemaphoreType.DMA((n_arr, 7)), pltpu.SemaphoreType.DMA((n_arr, 7)), pltpu.SemaphoreType.DMA((n_arr,))],
    )(*shards)


def _xchg_out_shapes(stacked, replicated):
    return ([jax.ShapeDtypeStruct(s.shape, s.dtype) for s in stacked]
            + [jax.ShapeDtypeStruct((N_DEV, *r.shape), r.dtype) for r in replicated])


def _xchg_sems(n_arr):
    return [pltpu.SemaphoreType.DMA((n_arr, 7)), pltpu.SemaphoreType.DMA((n_arr, 7)), pltpu.SemaphoreType.DMA((n_arr,))]


def _xchg_copies(ins, outs, sems, n_st, with_recv):
    send_sems, recv_sems, local_sems = sems
    n_arr = len(ins)
    x, y, c = _place()
    me = 4 * x + 2 * y + c

    def src(a, idx):
        return ins[a].at[idx] if a < n_st else ins[a]

    mine = [pltpu.make_async_copy(src(a, me), outs[a].at[me], local_sems.at[a]) for a in range(n_arr)]
    pairs = []
    for k in range(1, N_DEV):
        px, py, pc = x ^ (k >> 2), y ^ ((k >> 1) & 1), c ^ (k & 1)
        peer = 4 * px + 2 * py + pc
        for a in range(n_arr):
            sems_k = dict(send_sem=send_sems.at[a, k - 1], recv_sem=recv_sems.at[a, k - 1], device_id_type=MESH)
            send = pltpu.make_async_remote_copy(src_ref=src(a, peer), dst_ref=outs[a].at[me], device_id=(px, py, pc), **sems_k)
            recv = None
            if with_recv:
                recv = pltpu.make_async_remote_copy(src_ref=src(a, peer), dst_ref=outs[a].at[peer], device_id=(x, y, c), **sems_k)
            pairs.append((send, recv))
    return mine, pairs


def _xchg_start(ins, outs, sems, n_st):
    mine, pairs = _xchg_copies(ins, outs, sems, n_st, False)
    for cp in mine:
        cp.start()
    for send, _ in pairs:
        send.start()


def _xchg_finish(ins, outs, sems, n_st):
    mine, pairs = _xchg_copies(ins, outs, sems, n_st, True)
    for _, recv in pairs:
        recv.wait_recv()
    for send, _ in pairs:
        send.wait_send()
    for cp in mine:
        cp.wait()


def _exchange(stacked, replicated, name):
    _, landed = _pcall(lambda: None, name=name, grid=(), in_specs=[], out_specs=[], out_shape=[], comm=(stacked, replicated))()
    return landed


HBM = pl.BlockSpec(memory_space=pltpu.HBM)
SEMS = pl.BlockSpec(memory_space=pltpu.SEMAPHORE)
SIDE_EFFECT = pltpu.SideEffectType.DATAFLOW_SIDE_EFFECTING


N_SPLIT_SEMS = 2 * (N_DEV - 1)


def _split_copies(src, land, sems, with_recv):
    x, y, c = _place()
    me = 4 * x + 2 * y + c
    pairs = []
    for k in range(1, N_DEV):
        px, py, pc = x ^ (k >> 2), y ^ ((k >> 1) & 1), c ^ (k & 1)
        peer = 4 * px + 2 * py + pc
        sems_k = dict(send_sem=sems[k - 1], recv_sem=sems[N_DEV - 1 + k - 1], device_id_type=MESH)
        send = pltpu.make_async_remote_copy(src_ref=src.at[peer], dst_ref=land.at[me], device_id=(px, py, pc), **sems_k)
        recv = None
        if with_recv:
            recv = pltpu.make_async_remote_copy(src_ref=src.at[peer], dst_ref=land.at[peer], device_id=(x, y, c), **sems_k)
        pairs.append((send, recv))
    return pairs


def _exchange_start(stacked, name):
    def body(src, land, *rest):
        for send, _ in _split_copies(src, land, rest[:N_SPLIT_SEMS], False):
            send.start()
        rest[-1][...] = jnp.zeros_like(rest[-1])

    shape = pltpu.HBM(stacked.shape, stacked.dtype)
    res = pl.pallas_call(
        body, name=name, in_specs=[HBM, HBM],
        out_shape=(*[pltpu.SemaphoreType.DMA(())] * N_SPLIT_SEMS, shape, shape, jax.ShapeDtypeStruct((8, 128), F32)),
        out_specs=(*[SEMS] * N_SPLIT_SEMS, HBM, HBM, pl.BlockSpec(memory_space=pltpu.VMEM)),
        input_output_aliases={0: N_SPLIT_SEMS, 1: N_SPLIT_SEMS + 1},
        compiler_params=pltpu.CompilerParams(has_side_effects=SIDE_EFFECT),
    )(pltpu.with_memory_space_constraint(stacked, pltpu.HBM),
      pltpu.with_memory_space_constraint(lax.empty(stacked.shape, stacked.dtype), pltpu.HBM))
    return res[:N_SPLIT_SEMS], res[N_SPLIT_SEMS], res[N_SPLIT_SEMS + 1], res[-1]


def _exchange_wait(sems, src, land, after, name):
    def body(src_ref, land_ref, *rest):
        for send, recv in _split_copies(src_ref, land_ref, rest[:N_SPLIT_SEMS], True):
            send.wait_send()
            recv.wait_recv()

    shape = pltpu.HBM(src.shape, src.dtype)
    return pl.pallas_call(
        body, name=name, in_specs=[HBM, HBM, *[SEMS] * N_SPLIT_SEMS, ANY],
        out_shape=(shape, shape), out_specs=(HBM, HBM), input_output_aliases={0: 0, 1: 1},
        compiler_params=pltpu.CompilerParams(has_side_effects=SIDE_EFFECT))(src, land, *sems, after)[1]


def _adamw(parts, w, m, v, name):
    _, R, C = w.shape
    tr = R if R <= 512 else max(t for t in range(16, 513, 16) if R % t == 0)
    pr = tr if parts.shape[1] == R else -(-R // 16) * 16
    assert pr == tr or tr == R

    def body(p_ref, w_ref, m_ref, v_ref, g_ref, d_ref, nm_ref, nv_ref):
        g = p_ref[0].astype(F32)[:tr]
        for s in range(1, N_DEV):
            g = g + p_ref[s].astype(F32)[:tr]
        m2 = ADAM_B1 * m_ref[0] + (1.0 - ADAM_B1) * g
        v2 = ADAM_B2 * v_ref[0] + (1.0 - ADAM_B2) * (g * g)
        m_hat = m2 / (1.0 - ADAM_B1 ** ADAM_STEP)
        v_hat = v2 / (1.0 - ADAM_B2 ** ADAM_STEP)
        g_ref[0] = g
        d_ref[0] = -ADAM_LR * (m_hat / (jnp.sqrt(v_hat) + ADAM_EPS) + ADAM_WD * w_ref[0])
        nm_ref[0] = m2
        nv_ref[0] = v2

    blk = pl.BlockSpec((1, tr, C), lambda i: (0, i, 0))
    return _pcall(body, name=name, grid=(R // tr,),
                  in_specs=[pl.BlockSpec((N_DEV, pr, C), lambda i: (0, i, 0)), blk, blk, blk],
                  out_specs=[blk] * 4, out_shape=[jax.ShapeDtypeStruct((1, R, C), F32)] * 4,
                  dims=("parallel",))(parts, w, m, v)


def _t5_bucket(rel):
    nb = NUM_BUCKETS // 2
    max_exact = nb // 2
    base = (rel > 0).astype(jnp.int32) * nb
    n = jnp.abs(rel)
    nf = jnp.maximum(n, 1).astype(jnp.float32)
    large = max_exact + (jnp.log(nf / max_exact) / math.log(MAX_DISTANCE / max_exact) * (nb - max_exact)).astype(jnp.int32)
    large = jnp.minimum(large, nb - 1)
    return base + jnp.where(n < max_exact, n, large)


def _unstack_cols(g):
    return jnp.transpose(g, (1, 0, 2)).reshape(g.shape[1], N_DEV * g.shape[2])


def _stack_cols(w, n=N_DEV):
    R = w.shape[0]
    return jnp.transpose(w.reshape(R, n, w.shape[1] // n), (1, 0, 2))


def _stack_halves(g, v):
    return jnp.concatenate([_stack_cols(g, N_DEV // 2), _stack_cols(v, N_DEV // 2)], axis=0)


def kernel(x, positions, norm1_g, w_in, q_a_norm_g, w_q_b, kv_a_norm_g, w_kv_b, rel_bias, sinks, w_out, norm2_g, w_up, conv_w, conv_b, w_down, final_norm_g, loss_target, m_norm1_g, m_w_in, m_q_a_norm_g, m_w_q_b, m_kv_a_norm_g, m_w_kv_b, m_rel_bias, m_sinks, m_w_out, m_norm2_g, m_w_up, m_conv_w, m_conv_b, m_w_down, m_final_norm_g, v_norm1_g, v_w_in, v_q_a_norm_g, v_w_q_b, v_kv_a_norm_g, v_w_kv_b, v_rel_bias, v_sinks, v_w_out, v_norm2_g, v_w_up, v_conv_w, v_conv_b, v_w_down, v_final_norm_g):
    S = x.shape[1]
    x = x[0]
    target = loss_target[0]
    TM = 256

    tr = lambda w: jnp.swapaxes(w, 1, 2)
    g_in, g_qb, g_kvb = _all_gather([tr(w_in)[0].astype(BF16), tr(w_q_b)[0].astype(BF16), w_kv_b[0].astype(BF16)])
    late_weights = [w_out[0].astype(BF16), tr(w_up)[0].astype(BF16), conv_w[0]]
    wi = g_in.reshape(W_IN_COLS, D_MODEL)
    c0, c1, c2, c3, c4, c5 = (sum(W_IN_SIZES[:i + 1]) for i in range(6))
    w_in_pt = jnp.concatenate([wi[c4:c5], wi[c5:], wi[c1:c2], wi[:c0], wi[c2:c3], wi[c3:c4],
                               wi[c0:c0 + KV_LORA], wi[c0 + KV_LORA:c1], jnp.zeros((64, D_MODEL), BF16)], axis=0)
    wq = g_qb.reshape(H_A, QK_HEAD, Q_LORA)
    w_qb_pt = jnp.concatenate([wq[:, :QK_NOPE].reshape(H_A * QK_NOPE, Q_LORA), wq[:, QK_NOPE:].reshape(H_A * QK_ROPE, Q_LORA)], axis=0)
    w_kvb = _unstack_cols(g_kvb)

    half = QK_ROPE // 2
    inv_freq = ROPE_THETA ** (-jnp.arange(half, dtype=F32) / half)
    ang = positions.astype(F32)[:, None] * inv_freq[None, :]
    cos, sin = jnp.cos(ang), jnp.sin(ang)
    qa = jnp.arange(Q_BLOCK, dtype=jnp.int32)[:, None]
    kc = jnp.arange(SPAN, dtype=jnp.int32)[None, :]
    rel = (kc - WINDOW - qa).T
    in_band = (jnp.abs(rel) <= WINDOW).astype(F32).reshape(1, Q_BLOCK * SPAN)
    onehot_t = (_t5_bucket(rel).reshape(1, Q_BLOCK * SPAN) == jnp.arange(NUM_BUCKETS, dtype=jnp.int32)[:, None]).astype(F32)
    bias_t = _bias_table(rel_bias.T, onehot_t, in_band).reshape(H_B, SPAN, Q_BLOCK)
    sinks_b = jnp.broadcast_to(sinks.reshape(H_B, 1), (H_B, Q_BLOCK))

    (h1,) = _rowwise(lambda a, g: (_rms(a, g),), "norm1", S, TM, [_rows(x), _whole(norm1_g)], [("rows", D_MODEL, BF16)])
    proj = _matmul(h1, w_in_pt, "nt", BF16, "proj")

    def lat_fn(qlat, ckv, kr, gq, gkv, cs, sn):
        r1, r2 = _rope(kr[:, :half], kr[:, half:QK_ROPE], cs, sn)
        return _rms(qlat, gq), _rms(ckv, gkv), jnp.concatenate([r1, r2], axis=1)

    qn, ckvn, k_rope = _rowwise(lat_fn, "latents", S, TM,
                                [_rows(proj, 256, PROJ_QLAT), _rows(proj, 128, PROJ_CKV), _rows(proj, 128, PROJ_KROPE),
                                 _whole(q_a_norm_g), _whole(kv_a_norm_g), _rows(cos), _rows(sin)],
                                [("rows", Q_LORA, BF16), ("rows", KV_LORA, BF16), ("rows", QK_ROPE, BF16)])
    def q_heads_fn(q, cs, sn):
        q = q * MLA_PRESCALE
        outs = []
        for h in range(H_A):
            o = H_A * QK_NOPE + QK_ROPE * h
            r1, r2 = _rope(q[:, o:o + half], q[:, o + half:o + QK_ROPE], cs, sn)
            outs.append(jnp.concatenate([q[:, QK_NOPE * h:QK_NOPE * (h + 1)], r1, r2], axis=1)[None])
        return (jnp.concatenate(outs, axis=0),)

    (q_full,) = _matmul(qn, w_qb_pt, "nt", None, "q_up_heads", tm=512, tn=1536,
                        epi=(q_heads_fn, [_rows(cos), _rows(sin)], [("heads", H_A, QK_HEAD, BF16)]))

    def k_heads_fn(kvf, kr):
        return kvf, jnp.concatenate([jnp.concatenate([kvf[:, 256 * h:256 * h + QK_NOPE], kr], axis=1)[None] for h in range(H_A)], axis=0)

    kv, k_full = _matmul(ckvn, w_kvb, "nn", None, "kv_up_heads", tm=512, tn=2048,
                         epi=(k_heads_fn, [_rows(k_rope)], [("rows", H_A * (QK_NOPE + V_DIM), BF16), ("heads", H_A, QK_HEAD, BF16)]))
    (o_a, lse), (g_out, g_up, g_cw) = _mla_fwd(q_full, k_full, kv, S, comm=([], late_weights))
    w_out_f = g_out.reshape(D_MODEL, D_MODEL)
    w_up_t = g_up.reshape(2 * D_FF, D_MODEL)
    conv_w_f = _unstack_cols(g_cw)

    o_b, (g_down,) = _win_fwd(proj, bias_t, sinks_b, S, comm=([], [w_down[0].astype(BF16)]))
    w_down_f = g_down.reshape(D_FF, D_MODEL)

    (mixed,) = _rowwise(lambda ga, gb, oa, ob: (_sigmoid(ga) * oa + _sigmoid(gb) * ob,), "gate_mix", S, TM,
                        [_rows(proj, 1024, PROJ_GA), _rows(proj, 1024, PROJ_GB), _rows(o_a), _rows(o_b)], [("rows", D_MODEL, BF16)])
    x1, h2 = _matmul(mixed, w_out_f, "nn", None, "out_proj", residual=x, tm=512,
                     epi=(lambda a, g: (a, _rms(a, g)), [_whole(norm2_g)], [("rows", D_MODEL, F32), ("rows", D_MODEL, BF16)]))
    u = _matmul(h2, w_up_t, "nt", BF16, "ffn_up", tn=1408)
    act = _conv_gate_fwd(u, conv_w_f, conv_b, S)

    def final_fn(a, g, t):
        err = _rms(a, g) - t
        loss = 0.5 * jnp.sum(jnp.mean(err * err, axis=-1, keepdims=True), axis=0, keepdims=True)
        dx, dg = _rms_bwd(err * (1.0 / D_MODEL), a, g)
        return dx, dx, dg, jnp.broadcast_to(loss, (1, 128))

    gfin = final_norm_g.reshape(1, D_MODEL)
    dx2, dx2_b, d_gfin, loss_row = _matmul(
        act, w_down_f, "nn", None, "ffn_down_loss", residual=x1, tm=512,
        epi=(final_fn, [_whole(gfin), _rows(target)],
             [("rows", D_MODEL, F32), ("rows", D_MODEL, BF16), ("acc", 1, D_MODEL), ("acc", 1, 128)]))
    d_act = _matmul(dx2_b, w_down_f, "nt", BF16, "ffn_down_dx", tn=1408)
    d_w_down = _matmul(act, dx2_b, "tn", BF16, "ffn_down_dw")
    du_g, du_v, dcw_g, dcw_v, dcb_g, dcb_v = _conv_gate_bwd(u, conv_w_f, conv_b, d_act, S)
    d_conv_b = jnp.concatenate([dcb_g, dcb_v], axis=1)

    def norm_bwd_fn(dh, a, g, dres):
        dx, dg = _rms_bwd(dh, a, g)
        dx = dx + dres
        return dx, dx, dg

    dx1, dx1_b, d_g2 = _matmul(du_g, w_up_t, "nn", None, "ffn_up_dx_norm2_bwd", tm=256, a2=du_v,
                               epi=(norm_bwd_fn, [_rows(x1), _whole(norm2_g), _rows(dx2)],
                                    [("rows", D_MODEL, F32), ("rows", D_MODEL, BF16), ("acc", 1, D_MODEL)]))
    d_w_up_t = _matmul(du_g, h2, "tn", BF16, "ffn_up_dw", tm=256, a2=du_v)
    d_w_out = _matmul(mixed, dx1_b, "tn", BF16, "out_proj_dw", tm=512)

    def gate_bwd_fn(dm, ga, gb, oa, ob):
        sa, sb = _sigmoid(ga), _sigmoid(gb)
        return jnp.concatenate([dm * oa * sa * (1.0 - sa), dm * ob * sb * (1.0 - sb)], axis=1), dm * sa, dm * sb

    d_proj, do_a, do_b = _matmul(
        dx1_b, w_out_f, "nt", None, "out_proj_dx_gate_bwd", tm=512,
        epi=(gate_bwd_fn, [_rows(proj, 1024, PROJ_GA), _rows(proj, 1024, PROJ_GB), _rows(o_a), _rows(o_b)],
             [("cols", 2 * D_MODEL, 0, PROJ_P, BF16), ("rows", D_MODEL, BF16), ("rows", D_MODEL, BF16)]))

    d_proj, dk_acc, dv_acc, d_bias, d_sinks_g = _win_bwd(proj, bias_t, sinks_b, do_b, d_proj, S)
    d_rel_bias = _bias_table_bwd(d_bias.reshape(H_B, Q_BLOCK * SPAN), onehot_t).T
    d_sinks = d_sinks_g.reshape(1, H_B)

    early = [d_w_out.reshape(N_DEV, D_MODEL // N_DEV, D_MODEL), d_w_up_t.reshape(N_DEV, 2 * D_FF // N_DEV, D_MODEL),
             d_w_down.reshape(N_DEV, D_FF // N_DEV, D_MODEL), _stack_halves(dcw_g, dcw_v)]
    (dq_full, dk_full, dv_full), recv_early = _mla_bwd(q_full, k_full, kv, do_a, o_a, lse, S, comm=(early, []))

    def dq_post_fn(dq, cs, sn):
        nope = [dq[h, :, :QK_NOPE] for h in range(H_A)]
        rope = []
        for h in range(H_A):
            rope += list(_rope_bwd(dq[h, :, QK_NOPE:QK_NOPE + half], dq[h, :, QK_NOPE + half:], cs, sn))
        return (jnp.concatenate(nope + rope, axis=1),)

    (dq_p,) = _rowwise(dq_post_fn, "dq_post", S, TM, [_heads(dq_full), _rows(cos), _rows(sin)], [("rows", H_A * QK_HEAD, BF16)])

    def dkv_post_fn(dk, dv, cs, sn):
        dk = dk * math.log(2.0)
        dkv = jnp.concatenate([jnp.concatenate([dk[h, :, :QK_NOPE], dv[h]], axis=1) for h in range(H_A)], axis=1)
        dkr = dk[0, :, QK_NOPE:]
        for h in range(1, H_A):
            dkr = dkr + dk[h, :, QK_NOPE:]
        r1, r2 = _rope_bwd(dkr[:, :half], dkr[:, half:], cs, sn)
        return dkv, jnp.concatenate([r1, r2], axis=1)

    dkv, d_krope = _rowwise(dkv_post_fn, "dkv_post", S, TM, [_heads(dk_full), _heads(dv_full), _rows(cos), _rows(sin)],
                            [("rows", H_A * (QK_NOPE + V_DIM), BF16), ("rows", QK_ROPE, F32)])
    d_qn = _matmul(dq_p, w_qb_pt, "nn", F32, "q_up_dx")
    d_w_qb_pt = _matmul(dq_p, qn, "tn", BF16, "q_up_dw", tm=512)
    d_ckvn = _matmul(dkv, w_kvb, "nt", F32, "kv_up_dx")
    d_w_kvb = _matmul(ckvn, dkv, "tn", BF16, "kv_up_dw", tn=2048)

    def lat_bwd_fn(dqn, dckvn, dkr, qlat, ckv, gq, gkv, dkb, dvb):
        dql, dgq = _rms_bwd(dqn, qlat, gq)
        dck, dgkv = _rms_bwd(dckvn, ckv, gkv)
        tail = jnp.concatenate([dql, dkb, dvb, dck, dkr, jnp.zeros_like(dkr)], axis=1)
        return tail, dgq, dgkv

    shifted = lambda arr: (arr, lambda tm: pl.BlockSpec((tm, arr.shape[1]), lambda i, *_: (i + WINDOW // tm, 0)))
    TL = min(128, S)
    d_proj, d_gq, d_gkv = _rowwise(lat_bwd_fn, "latents_bwd", S, TL,
                                   [_rows(d_qn), _rows(d_ckvn), _rows(d_krope), _rows(proj, 256, PROJ_QLAT), _rows(proj, 128, PROJ_CKV),
                                    _whole(q_a_norm_g), _whole(kv_a_norm_g), shifted(dk_acc), shifted(dv_acc)],
                                   [("cols", 1024, 3, PROJ_P, BF16), ("acc", 1, Q_LORA), ("acc", 1, KV_LORA)], into=(d_proj, 0))
    d_w_qb_t = jnp.concatenate([d_w_qb_pt[:H_A * QK_NOPE].reshape(H_A, QK_NOPE, Q_LORA),
                                d_w_qb_pt[H_A * QK_NOPE:].reshape(H_A, QK_ROPE, Q_LORA)], axis=1)
    dp, recv_mid = _matmul(d_proj, h1, "tn", BF16, "proj_dw", tm=512, comm=([d_w_qb_t, _stack_cols(d_w_kvb)], []))

    late = jnp.concatenate([dp[3072:3328], dp[3840:3968], dp[3968:4032], dp[2048:3072], dp[3328:3584],
                            dp[3584:3840], dp[0:1024], dp[1024:2048]], axis=0).reshape(N_DEV, W_IN_COLS // N_DEV, D_MODEL)
    late_sems, late_src, late_land, started = _exchange_start(late, "late_grads_start")

    def norm1_bwd_fn(dh, a, g, dres):
        dx, dg = _rms_bwd(dh, a, g)
        return dx + dres, dg

    grad_x, d_g1 = _matmul(
        d_proj, w_in_pt, "nn", None, "proj_dx_norm1_bwd", tm=512,
        epi=(norm1_bwd_fn, [_rows(x), _whole(norm1_g + started[:1, :1]), _rows(dx1)], [("rows", D_MODEL, F32), ("acc", 1, D_MODEL)]))

    transposed = ("w_in", "w_q_b", "w_up")
    ready_names = ["w_out", "w_up", "w_down", "conv_w", "w_q_b", "w_kv_b"]
    ready_wmv = [(w_out, m_w_out, v_w_out), (tr(w_up), tr(m_w_up), tr(v_w_up)), (w_down, m_w_down, v_w_down), (conv_w, m_conv_w, v_conv_w),
                 (tr(w_q_b), tr(m_w_q_b), tr(v_w_q_b)), (w_kv_b, m_w_kv_b, v_w_kv_b)]
    big = {n: _adamw(r, *wmv, "adamw_" + n) for n, r, wmv in zip(ready_names, [*recv_early, *recv_mid], ready_wmv)}

    small_parts = [d_g1, d_gq, d_gkv, d_rel_bias.reshape(1, NUM_BUCKETS * H_B), d_sinks, d_g2, d_conv_b, d_gfin, loss_row[:, :1]]
    small = jnp.concatenate(small_parts, axis=1)
    n_small = small.shape[1]
    pad = (-n_small) % 128
    small = jnp.pad(small, ((0, 0), (0, pad)))
    small, _ = lax.optimization_barrier((small, [big[n][0] for n in ready_names]))
    (recv_small,) = _exchange([], [small], "exchange_small_grads")

    def flat(a):
        return a.reshape(1, -1)

    small_w = [norm1_g, q_a_norm_g, kv_a_norm_g, rel_bias, sinks, norm2_g, conv_b, final_norm_g]
    small_m = [m_norm1_g, m_q_a_norm_g, m_kv_a_norm_g, m_rel_bias, m_sinks, m_norm2_g, m_conv_b, m_final_norm_g]
    small_v = [v_norm1_g, v_q_a_norm_g, v_kv_a_norm_g, v_rel_bias, v_sinks, v_norm2_g, v_conv_b, v_final_norm_g]
    cat = lambda parts: jnp.pad(jnp.concatenate([flat(a) for a in parts], axis=1), ((0, 0), (0, pad + 1)))[None]
    sm = _adamw(recv_small, cat(small_w), cat(small_m), cat(small_v), "adamw_small")

    landed = _exchange_wait(late_sems, late_src, late_land, sm[0], "late_grads_wait")
    me = 4 * lax.axis_index("x") + 2 * lax.axis_index("y") + lax.axis_index("c")
    landed = lax.dynamic_update_slice_in_dim(landed, lax.dynamic_slice_in_dim(late, me, 1, axis=0), me, axis=0)
    big["w_in"] = _adamw(landed, tr(w_in), tr(m_w_in), tr(v_w_in), "adamw_w_in")

    loss = sm[0][0, 0, n_small - 1]
    order =["norm1_g", "w_in", "q_a_norm_g", "w_q_b", "kv_a_norm_g", "w_kv_b", "rel_bias", "sinks", "w_out", "norm2_g", "w_up",
             "conv_w", "conv_b", "w_down", "final_norm_g"]
    small_names = ["norm1_g", "q_a_norm_g", "kv_a_norm_g", "rel_bias", "sinks", "norm2_g", "conv_b", "final_norm_g"]
    offs, o = {}, 0
    for n, a in zip(small_names, small_w):
        offs[n] = (o, a.size, a.shape)
        o += a.size
    outs = [loss, grad_x[None]]
    for kind in range(4):
        for n in order:
            if n in big:
                outs.append(tr(big[n][kind]) if n in transposed else big[n][kind])
            else:
                o, size, shape = offs[n]
                outs.append(sm[kind][0, 0, o:o + size].reshape(shape))
    return tuple(outs)
```

```python
import math

import jax
import jax.numpy as jnp
from jax import lax
from jax.experimental import pallas as pl
from jax.experimental.pallas import tpu as pltpu

F32 = jnp.float32
BF16 = jnp.bfloat16

N_DEV = 8
D_MODEL = 1024
EPS = 1e-6
H_A, QK_NOPE, QK_ROPE, V_DIM, Q_LORA, KV_LORA = 8, 128, 64, 128, 256, 128
QK_HEAD = QK_NOPE + QK_ROPE
ROPE_THETA = 10000.0
H_B, KV_B, GROUP, HD_B, WINDOW, Q_BLOCK = 16, 4, 4, 64, 128, 128
SPAN = Q_BLOCK + 2 * WINDOW
NUM_BUCKETS, MAX_DISTANCE = 32, 128
D_FF = 2816
ADAM_LR, ADAM_B1, ADAM_B2, ADAM_EPS, ADAM_WD, ADAM_STEP = 0.001, 0.9, 0.999, 1e-08, 0.01, 10

W_IN_SIZES = (Q_LORA, KV_LORA + QK_ROPE, H_B * HD_B, KV_B * HD_B, KV_B * HD_B, D_MODEL, D_MODEL)
W_IN_COLS = sum(W_IN_SIZES)
PROJ_P = 4096
PROJ_GA, PROJ_GB, PROJ_QB, PROJ_QLAT, PROJ_KB, PROJ_VB, PROJ_CKV, PROJ_KROPE = 0, 1, 2, 12, 13, 14, 30, 31

VMEM_LIMIT = 56 * 1024 * 1024

NN = (((1,), (0,)), ((), ()))
NT = (((1,), (1,)), ((), ()))
TN = (((0,), (0,)), ((), ()))


def _pcall(body, *, name, grid, in_specs, out_specs, out_shape, scratch_shapes=(), dims=None, comm=None, aliases=None):
    if comm is None:
        params = pltpu.CompilerParams(dimension_semantics=dims, vmem_limit_bytes=VMEM_LIMIT)
        return pl.pallas_call(body, name=name, grid=grid, in_specs=in_specs, out_specs=out_specs, out_shape=out_shape,
                              scratch_shapes=list(scratch_shapes), input_output_aliases=aliases or {}, compiler_params=params)
    assert not aliases
    stacked, replicated = comm
    arrs = [*stacked, *replicated]
    n_st, n_arr = len(stacked), len(arrs)
    single = not isinstance(out_specs, (list, tuple))
    o_specs, o_shape = ([out_specs], [out_shape]) if single else (list(out_specs), list(out_shape))
    n_in, n_out = len(in_specs), len(o_specs)

    def wrapped(*refs):
        c_in = refs[n_in:n_in + n_arr]
        c_out = refs[n_in + n_arr + n_out:n_in + 2 * n_arr + n_out]
        sems = refs[len(refs) - 3:]
        own = (*refs[:n_in], *refs[n_in + n_arr:n_in + n_arr + n_out], *refs[n_in + 2 * n_arr + n_out:len(refs) - 3])
        if not grid:
            _xchg_start(c_in, c_out, sems, n_st)
            _xchg_finish(c_in, c_out, sems, n_st)
            return
        first = last = None
        for d, n in enumerate(grid):
            pid = pl.program_id(d)
            first = (pid == 0) if first is None else first & (pid == 0)
            last = (pid == n - 1) if last is None else last & (pid == n - 1)

        @pl.when(first)
        def _():
            _xchg_start(c_in, c_out, sems, n_st)

        body(*own)

        @pl.when(last)
        def _():
            _xchg_finish(c_in, c_out, sems, n_st)

    params = pltpu.CompilerParams(dimension_semantics=("arbitrary",) * len(grid), vmem_limit_bytes=VMEM_LIMIT)
    call = pl.pallas_call(wrapped, name=name, grid=grid, in_specs=[*in_specs, *[ANY] * n_arr], out_specs=[*o_specs, *[ANY] * n_arr],
                          out_shape=[*o_shape, *_xchg_out_shapes(stacked, replicated)],
                          scratch_shapes=[*scratch_shapes, *_xchg_sems(n_arr)], compiler_params=params)

    def run(*args):
        res = call(*args, *arrs)
        outs, landed = res[:n_out], res[n_out:]
        return (outs[0] if single else outs), landed

    return run


def _dot(a, b, dn):
    return lax.dot_general(a, b, dn, preferred_element_type=F32)


def _tile(n, target):
    best = None
    for t in range(128, min(n, target) + 1, 128):
        if n % t == 0:
            best = t
    return n if best is None else best


def _matmul(a, b, mode, out_dtype, name, residual=None, tm=1024, tn=1024, comm=None, a2=None, epi=None):
    if mode == "nn":
        (M, K), N = a.shape, b.shape[1]
    elif mode == "nt":
        (M, K), N = a.shape, b.shape[0]
    else:
        (K, M), N = a.shape, b.shape[1]
    tm, tn = _tile(M, tm), _tile(N, tn)
    a_spec = pl.BlockSpec((K, tm), lambda i, j: (0, i)) if mode == "tn" else pl.BlockSpec((tm, K), lambda i, j: (i, 0))
    b_spec = pl.BlockSpec((tn, b.shape[1]), lambda i, j: (j, 0)) if mode == "nt" else pl.BlockSpec((K, tn), lambda i, j: (0, j))
    o_spec = pl.BlockSpec((tm, tn), lambda i, j: (i, j))
    in_specs, args = [a_spec, b_spec], [a, b]
    n1 = M // tm
    if a2 is not None and mode == "tn":
        assert M % tm == 0 and a2.shape[1] % tm == 0
        in_specs[0] = pl.BlockSpec((K, tm), lambda i, j: (0, jnp.minimum(i, n1 - 1)))
        in_specs.append(pl.BlockSpec((K, tm), lambda i, j: (0, jnp.maximum(i - n1, 0))))
        args.append(a2)
        M += a2.shape[1]
    elif a2 is not None:
        assert (mode == "nt" and K + a2.shape[1] == b.shape[1]) or (mode == "nn" and K + a2.shape[1] == b.shape[0])
        if mode == "nn":
            b_spec = in_specs[1] = pl.BlockSpec((b.shape[0], tn), lambda i, j: (0, j))
        in_specs.append(pl.BlockSpec((tm, a2.shape[1]), lambda i, j: (i, 0)))
        args.append(a2)
    if residual is not None:
        in_specs.append(o_spec)
        args.append(residual)
    n_mm = len(args)
    scratch = [pltpu.VMEM((tm, K), a.dtype)] if mode == "tn" else []
    if epi is None:
        out_specs, out_shape, is_acc = o_spec, jax.ShapeDtypeStruct((M, N), out_dtype), None
    else:
        assert tn == N
        fn, epi_ins, epi_outs = epi
        in_specs += [mk(tm) for _, mk in epi_ins]
        args += [arr for arr, _ in epi_ins]
        out_specs, out_shape, is_acc = _row_out_specs(epi_outs, M, tm)

    def body(*refs):
        a_ref, b_ref = refs[0], refs[1]
        n_out = 1 if epi is None else len(is_acc)
        out_refs = refs[len(args):len(args) + n_out]
        if mode == "tn":
            at_ref = refs[len(args) + n_out]

            first_col = pl.program_id(1) == 0
            from_a = first_col if a2 is None else first_col & (pl.program_id(0) < n1)

            @pl.when(from_a)
            def _():
                at_ref[...] = a_ref[...].T

            if a2 is not None:
                @pl.when(first_col & (pl.program_id(0) >= n1))
                def _():
                    at_ref[...] = refs[2][...].T

            acc = _dot(at_ref[...], b_ref[...], NN)
        elif a2 is not None and mode == "nt":
            acc = _dot(a_ref[...], b_ref[:, :K], NT) + _dot(refs[2][...], b_ref[:, K:], NT)
        elif a2 is not None:
            acc = _dot(a_ref[...], b_ref[:K, :], NN) + _dot(refs[2][...], b_ref[K:, :], NN)
        else:
            acc = _dot(a_ref[...], b_ref[...], NT if mode == "nt" else NN)
        if residual is not None:
            acc = acc + refs[n_mm - 1][...]
        if epi is None:
            out_refs[0][...] = acc.astype(out_dtype)
        else:
            _store_rows(out_refs, fn(acc, *[_load_f32(r) for r in refs[n_mm:len(args)]]), is_acc)

    return _pcall(body, name=name, grid=(M // tm, N // tn), in_specs=in_specs, out_specs=out_specs,
                  out_shape=out_shape, scratch_shapes=scratch,
                  dims=("arbitrary" if epi is not None else "parallel", "arbitrary"), comm=comm)(*args)


def _rows(arr, width=None, col=0):
    width = arr.shape[1] if width is None else width
    return (arr, lambda tm: pl.BlockSpec((tm, width), lambda i, *_: (i, col)))


def _heads(arr):
    return (arr, lambda tm: pl.BlockSpec((arr.shape[0], tm, arr.shape[2]), lambda i, *_: (0, i, 0)))


def _whole(arr):
    nd = arr.ndim
    return (arr, lambda tm: pl.BlockSpec(arr.shape, lambda i, *_: (0,) * nd))


def _row_out_specs(outs, n_rows, tm):
    out_specs, out_shape, is_acc = [], [], []
    for o in outs:
        if o[0] == "rows":
            out_specs.append(pl.BlockSpec((tm, o[1]), lambda i, *_: (i, 0)))
            out_shape.append(jax.ShapeDtypeStruct((n_rows, o[1]), o[2]))
        elif o[0] == "cols":
            out_specs.append(pl.BlockSpec((tm, o[1]), lambda i, *_, c=o[2]: (i, c)))
            out_shape.append(jax.ShapeDtypeStruct((n_rows, o[3]), o[4]))
        elif o[0] == "heads":
            out_specs.append(pl.BlockSpec((o[1], tm, o[2]), lambda i, *_: (0, i, 0)))
            out_shape.append(jax.ShapeDtypeStruct((o[1], n_rows, o[2]), o[3]))
        else:
            out_specs.append(pl.BlockSpec((o[1], o[2]), lambda i, *_: (0, 0)))
            out_shape.append(jax.ShapeDtypeStruct((o[1], o[2]), F32))
        is_acc.append(o[0] == "acc")
    return out_specs, out_shape, is_acc


def _load_f32(r):
    v = r[...]
    return v.astype(F32) if v.dtype == BF16 else v


def _store_rows(out_refs, vals, is_acc):
    for r, v, acc in zip(out_refs, vals, is_acc):
        if acc:
            @pl.when(pl.program_id(0) == 0)
            def _():
                r[...] = jnp.zeros_like(r)

            r[...] += v
        else:
            r[...] = v.astype(r.dtype)


def _rowwise(fn, name, n_rows, tm, ins, outs, upcast=True, into=None):
    tm = min(tm, n_rows)
    assert n_rows % tm == 0
    in_specs = [mk(tm) for _, mk in ins]
    out_specs, out_shape, is_acc = _row_out_specs(outs, n_rows, tm)
    n_in = len(ins)
    args = [a for a, _ in ins]
    aliases = {}
    if into is not None:
        in_specs.append(ANY)
        args.append(into[0])
        aliases = {n_in: into[1]}

    def body(*refs):
        vals = fn(*[_load_f32(r) if upcast else r[...] for r in refs[:n_in]])
        _store_rows(refs[len(args):], vals, is_acc)

    return _pcall(body, name=name, grid=(n_rows // tm,), in_specs=in_specs, out_specs=out_specs,
                  out_shape=out_shape, dims=("arbitrary",), aliases=aliases)(*args)


def _rms(x, g):
    r = lax.rsqrt(jnp.mean(x * x, axis=-1, keepdims=True) + EPS)
    return x * r * g


def _rms_bwd(dy, x, g):
    r = lax.rsqrt(jnp.mean(x * x, axis=-1, keepdims=True) + EPS)
    xhat = x * r
    dxhat = dy * g
    dx = r * (dxhat - xhat * jnp.mean(dxhat * xhat, axis=-1, keepdims=True))
    return dx, jnp.sum(dy * xhat, axis=0, keepdims=True)


def _rope(x1, x2, cos, sin):
    return x1 * cos - x2 * sin, x2 * cos + x1 * sin


def _rope_bwd(d1, d2, cos, sin):
    return d1 * cos + d2 * sin, d2 * cos - d1 * sin


def _sigmoid(x):
    return 1.0 / (1.0 + jnp.exp(-x))


MLA_SCALE = 1.0 / math.sqrt(QK_HEAD)
MLA_PRESCALE = MLA_SCALE * math.log2(math.e)
MLA_TQ, MLA_KC = 1024, 1024


def _mla_fwd(q_full, k_full, kv, S, comm=None):
    tq, kc = min(MLA_TQ, S), min(MLA_KC, S)

    def body(q_ref, k_ref, v_ref, o_ref, lse_ref):
        q = q_ref[0]
        m = jnp.full((tq, 1), -1e30, F32)
        l = jnp.zeros((tq, 1), F32)
        acc = jnp.zeros((tq, V_DIM), F32)
        for c in range(S // kc):
            s = _dot(q, k_ref[0, c * kc:(c + 1) * kc, :], NT)
            m_new = jnp.maximum(m, jnp.max(s, axis=-1, keepdims=True))
            alpha = jnp.exp2(m - m_new)
            p = jnp.exp2(s - m_new)
            l = alpha * l + jnp.sum(p, axis=-1, keepdims=True)
            acc = alpha * acc + _dot(p.astype(BF16), v_ref[c * kc:(c + 1) * kc, :], NN)
            m = m_new
        o_ref[...] = (acc / l).astype(BF16)
        lse_ref[0] = m + jnp.log2(l)

    return _pcall(
        body, name="mla_fwd", grid=(H_A, S // tq),
        in_specs=[pl.BlockSpec((1, tq, QK_HEAD), lambda h, i: (h, i, 0)),
                  pl.BlockSpec((1, S, QK_HEAD), lambda h, i: (h, 0, 0)),
                  pl.BlockSpec((S, V_DIM), lambda h, i: (0, 2 * h + 1))],
        out_specs=[pl.BlockSpec((tq, V_DIM), lambda h, i: (i, h)),
                   pl.BlockSpec((1, tq, 1), lambda h, i: (h, i, 0))],
        out_shape=[jax.ShapeDtypeStruct((S, H_A * V_DIM), BF16), jax.ShapeDtypeStruct((H_A, S, 1), F32)],
        dims=("parallel", "parallel"), comm=comm)(q_full, k_full, kv)


def _mla_bwd(q_full, k_full, kv, do_a, o_a, lse, S, comm=None):
    tq, kc = min(MLA_TQ, S), min(MLA_KC, S)

    def body(q_ref, k_ref, v_ref, do_ref, o_ref, lse_ref, dq_ref, dk_ref, dv_ref):
        @pl.when(pl.program_id(1) == 0)
        def _():
            dk_ref[...] = jnp.zeros_like(dk_ref)
            dv_ref[...] = jnp.zeros_like(dv_ref)

        q = q_ref[0]
        do = do_ref[...]
        lse_q = lse_ref[0]
        delta = jnp.sum(do.astype(F32) * o_ref[...].astype(F32), axis=-1, keepdims=True)
        dq = jnp.zeros((tq, QK_HEAD), F32)
        for c in range(S // kc):
            k = k_ref[0, c * kc:(c + 1) * kc, :]
            v = v_ref[c * kc:(c + 1) * kc, :]
            p = jnp.exp2(_dot(q, k, NT) - lse_q)
            ds = (p * (_dot(do, v, NT) - delta)).astype(BF16)
            dq = dq + _dot(ds, k, NN)
            dk_ref[0, c * kc:(c + 1) * kc, :] += _dot(ds, q, TN)
            dv_ref[0, c * kc:(c + 1) * kc, :] += _dot(p.astype(BF16), do, TN)
        dq_ref[0] = dq * MLA_SCALE

    return _pcall(
        body, name="mla_bwd", grid=(H_A, S // tq),
        in_specs=[pl.BlockSpec((1, tq, QK_HEAD), lambda h, i: (h, i, 0)),
                  pl.BlockSpec((1, S, QK_HEAD), lambda h, i: (h, 0, 0)),
                  pl.BlockSpec((S, V_DIM), lambda h, i: (0, 2 * h + 1)),
                  pl.BlockSpec((tq, V_DIM), lambda h, i: (i, h)),
                  pl.BlockSpec((tq, V_DIM), lambda h, i: (i, h)),
                  pl.BlockSpec((1, tq, 1), lambda h, i: (h, i, 0))],
        out_specs=[pl.BlockSpec((1, tq, QK_HEAD), lambda h, i: (h, i, 0)),
                   pl.BlockSpec((1, S, QK_HEAD), lambda h, i: (h, 0, 0)),
                   pl.BlockSpec((1, S, V_DIM), lambda h, i: (h, 0, 0))],
        out_shape=[jax.ShapeDtypeStruct((H_A, S, QK_HEAD), F32), jax.ShapeDtypeStruct((H_A, S, QK_HEAD), F32),
                   jax.ShapeDtypeStruct((H_A, S, V_DIM), F32)],
        dims=("parallel", "arbitrary"), comm=comm)(q_full, k_full, kv, do_a, o_a, lse)


WIN_SCALE = 1.0 / math.sqrt(HD_B)


WIN_PER_STEP = 4


def _win_specs(S):
    last, B = S // Q_BLOCK - 1, WIN_PER_STEP
    qspec = pl.BlockSpec((B * Q_BLOCK, H_B * HD_B), lambda i: (i, PROJ_QB))
    kspecs = [[pl.BlockSpec((Q_BLOCK, KV_B * HD_B), lambda i, d=d, c=c: (jnp.clip(B * i + d, 0, last), c)) for d in range(-1, B + 1)]
              for c in (PROJ_KB, PROJ_VB)]
    bias_spec = pl.BlockSpec((H_B, SPAN, Q_BLOCK), lambda i: (0, 0, 0))
    sink_spec = pl.BlockSpec((H_B, Q_BLOCK), lambda i: (0, 0))
    return qspec, kspecs, bias_spec, sink_spec


def _win_edge_ok(n, n_blk):
    row = lax.broadcasted_iota(jnp.int32, (SPAN, 1), 0)
    return jnp.logical_not(((n == 0) & (row < WINDOW)) | ((n == n_blk - 1) & (row >= SPAN - WINDOW)))


def _lanes4(pieces):
    return jnp.concatenate(pieces, axis=1)


def _win_probs(kg, q4t, bias_ref, sink_ref, g, edge_ok):
    bias4 = _lanes4([bias_ref[GROUP * g + j] for j in range(GROUP)])
    sink4 = _lanes4([sink_ref[GROUP * g + j:GROUP * g + j + 1, :] for j in range(GROUP)])
    s = jnp.where(edge_ok, _dot(kg, q4t, NN) + bias4, -1e30)
    m = jnp.maximum(jnp.max(s, axis=0, keepdims=True), sink4)
    p = jnp.exp(s - m)
    e_sink = jnp.exp(sink4 - m)
    inv_l = 1.0 / (jnp.sum(p, axis=0, keepdims=True) + e_sink)
    return p * inv_l, e_sink * inv_l


def _group_t(xt, g):
    return _lanes4([xt[HD_B * (GROUP * g + j):HD_B * (GROUP * g + j + 1), :] for j in range(GROUP)])


def _rows_of(ref, b):
    return ref[Q_BLOCK * b:Q_BLOCK * (b + 1), :]


def _win_fwd(proj, bias_t, sinks_b, S, comm=None):
    n_blk, B = S // Q_BLOCK, WIN_PER_STEP
    qspec, kspecs, bias_spec, sink_spec = _win_specs(S)

    def body(q_ref, *refs):
        k_refs, v_refs, (bias_ref, sink_ref, o_ref) = refs[:B + 2], refs[B + 2:2 * B + 4], refs[2 * B + 4:]
        for b in range(B):
            edge_ok = _win_edge_ok(B * pl.program_id(0) + b, n_blk)
            k = jnp.concatenate([r[...] for r in k_refs[b:b + 3]], axis=0)
            vt = jnp.concatenate([r[...] for r in v_refs[b:b + 3]], axis=0).T
            qt = (_rows_of(q_ref, b).astype(F32) * WIN_SCALE).T.astype(BF16)
            parts = []
            for g in range(KV_B):
                p, _ = _win_probs(k[:, HD_B * g:HD_B * (g + 1)], _group_t(qt, g), bias_ref, sink_ref, g, edge_ok)
                o4t = _dot(vt[HD_B * g:HD_B * (g + 1), :], p.astype(BF16), NN)
                parts += [o4t[:, Q_BLOCK * j:Q_BLOCK * (j + 1)] for j in range(GROUP)]
            o_ref[Q_BLOCK * b:Q_BLOCK * (b + 1), :] = jnp.concatenate(parts, axis=0).T.astype(BF16)

    return _pcall(body, name="win_fwd", grid=(n_blk // B,),
                  in_specs=[qspec, *kspecs[0], *kspecs[1], bias_spec, sink_spec],
                  out_specs=pl.BlockSpec((B * Q_BLOCK, H_B * HD_B), lambda i: (i, 0)),
                  out_shape=jax.ShapeDtypeStruct((S, H_B * HD_B), BF16),
                  dims=("parallel",), comm=comm)(*[proj] * (2 * B + 5), bias_t, sinks_b)


def _win_bwd(proj, bias_t, sinks_b, do_b, d_proj, S):
    n_blk, B = S // Q_BLOCK, WIN_PER_STEP
    qspec, kspecs, bias_spec, sink_spec = _win_specs(S)

    def body(q_ref, *refs):
        k_refs, v_refs = refs[:B + 2], refs[B + 2:2 * B + 4]
        bias_ref, sink_ref, do_ref, _, dq_ref, dk_ref, dv_ref, dbias_ref, dsink_ref, dsink_acc = refs[2 * B + 4:]
        i = pl.program_id(0)

        @pl.when(i == 0)
        def _():
            dk_ref[...] = jnp.zeros_like(dk_ref)
            dv_ref[...] = jnp.zeros_like(dv_ref)
            dbias_ref[...] = jnp.zeros_like(dbias_ref)
            dsink_acc[...] = jnp.zeros_like(dsink_acc)

        d_bias, d_sink, dk_blocks, dv_blocks = {}, {}, [], []
        for b in range(B):
            edge_ok = _win_edge_ok(B * i + b, n_blk)
            k = jnp.concatenate([r[...] for r in k_refs[b:b + 3]], axis=0)
            v = jnp.concatenate([r[...] for r in v_refs[b:b + 3]], axis=0)
            kt = k.T
            qt = (_rows_of(q_ref, b).astype(F32) * WIN_SCALE).T.astype(BF16)
            dot_ = _rows_of(do_ref, b).astype(F32).T.astype(BF16)
            dq_parts, dks, dvs = [], [], []
            for g in range(KV_B):
                kg, vg = k[:, HD_B * g:HD_B * (g + 1)], v[:, HD_B * g:HD_B * (g + 1)]
                q4t, do4t = _group_t(qt, g), _group_t(dot_, g)
                p, p_sink = _win_probs(kg, q4t, bias_ref, sink_ref, g, edge_ok)
                dp = _dot(vg, do4t, NN)
                delta = jnp.sum(p * dp, axis=0, keepdims=True)
                ds = p * (dp - delta)
                d_bias[g] = ds if b == 0 else d_bias[g] + ds
                d_sink[g] = -p_sink * delta if b == 0 else d_sink[g] - p_sink * delta
                dsb = ds.astype(BF16)
                dq4t = _dot(kt[HD_B * g:HD_B * (g + 1), :], dsb, NN) * WIN_SCALE
                dq_parts += [dq4t[:, Q_BLOCK * j:Q_BLOCK * (j + 1)] for j in range(GROUP)]
                dks.append(_dot(dsb, q4t, NT))
                dvs.append(_dot(p.astype(BF16), do4t, NT))
            dq_ref[Q_BLOCK * b:Q_BLOCK * (b + 1), :] = jnp.concatenate(dq_parts, axis=0).T.astype(BF16)
            dk_blocks.append(jnp.concatenate(dks, axis=1))
            dv_blocks.append(jnp.concatenate(dvs, axis=1))

        for g in range(KV_B):
            for j in range(GROUP):
                dbias_ref[GROUP * g + j] += d_bias[g][:, Q_BLOCK * j:Q_BLOCK * (j + 1)]
            dsink_acc[g:g + 1, :] += d_sink[g]

        def overlap(blocks):
            out = blocks[0]
            for blk in blocks[1:]:
                keep = out.shape[0] - 2 * Q_BLOCK
                out = jnp.concatenate([out[:keep], out[keep:] + blk[:2 * Q_BLOCK], blk[2 * Q_BLOCK:]], axis=0)
            return out

        rows = pl.ds(pl.multiple_of(i * (B * Q_BLOCK), B * Q_BLOCK), (B + 2) * Q_BLOCK)
        dk_ref[rows, :] += overlap(dk_blocks)
        dv_ref[rows, :] += overlap(dv_blocks)

        @pl.when(i == n_blk // B - 1)
        def _():
            acc = dsink_acc[...]
            dsink_ref[...] = jnp.concatenate(
                [jnp.sum(acc[:, Q_BLOCK * j:Q_BLOCK * (j + 1)], axis=1, keepdims=True) for j in range(GROUP)], axis=1)

    whole = lambda shape: pl.BlockSpec(shape, lambda i: (0,) * len(shape))
    return _pcall(
        body, name="win_bwd", grid=(n_blk // B,),
        in_specs=[qspec, *kspecs[0], *kspecs[1], bias_spec, sink_spec, pl.BlockSpec((B * Q_BLOCK, H_B * HD_B), lambda i: (i, 0)), ANY],
        out_specs=[qspec, whole((S + 2 * WINDOW, KV_B * HD_B)),
                   whole((S + 2 * WINDOW, KV_B * HD_B)), whole((H_B, SPAN, Q_BLOCK)), whole((KV_B, GROUP))],
        out_shape=[jax.ShapeDtypeStruct((S, PROJ_P), BF16), jax.ShapeDtypeStruct((S + 2 * WINDOW, KV_B * HD_B), F32),
                   jax.ShapeDtypeStruct((S + 2 * WINDOW, KV_B * HD_B), F32), jax.ShapeDtypeStruct((H_B, SPAN, Q_BLOCK), F32),
                   jax.ShapeDtypeStruct((KV_B, GROUP), F32)],
        scratch_shapes=[pltpu.VMEM((KV_B, GROUP * Q_BLOCK), F32)],
        dims=("arbitrary",), aliases={2 * B + 8: 0})(*[proj] * (2 * B + 5), bias_t, sinks_b, do_b, d_proj)


def _bias_table(rel_bias_t, onehot_t, in_band):
    def body(rb_ref, oh_ref, band_ref, o_ref):
        t = lax.dot_general(rb_ref[...], oh_ref[...], NN, preferred_element_type=F32, precision=lax.Precision.HIGHEST)
        o_ref[...] = jnp.where(band_ref[...] > 0.5, t, -1e30)

    n = onehot_t.shape[1]
    tn = _tile(n, 8192)
    return _pcall(body, name="bias_table", grid=(n // tn,),
                  in_specs=[pl.BlockSpec((H_B, NUM_BUCKETS), lambda j: (0, 0)), pl.BlockSpec((NUM_BUCKETS, tn), lambda j: (0, j)),
                            pl.BlockSpec((1, tn), lambda j: (0, j))],
                  out_specs=pl.BlockSpec((H_B, tn), lambda j: (0, j)),
                  out_shape=jax.ShapeDtypeStruct((H_B, n), F32), dims=("parallel",))(rel_bias_t, onehot_t, in_band)


def _bias_table_bwd(dbias, onehot_t):
    n = onehot_t.shape[1]
    tk = _tile(n, 8192)

    def body(d_ref, oh_ref, o_ref):
        @pl.when(pl.program_id(0) == 0)
        def _():
            o_ref[...] = jnp.zeros_like(o_ref)

        o_ref[...] += lax.dot_general(d_ref[...], oh_ref[...], NT, preferred_element_type=F32, precision=lax.Precision.HIGHEST)

    return _pcall(body, name="bias_table_bwd", grid=(n // tk,),
                  in_specs=[pl.BlockSpec((H_B, tk), lambda j: (0, j)), pl.BlockSpec((NUM_BUCKETS, tk), lambda j: (0, j))],
                  out_specs=pl.BlockSpec((H_B, NUM_BUCKETS), lambda j: (0, 0)),
                  out_shape=jax.ShapeDtypeStruct((H_B, NUM_BUCKETS), F32), dims=("arbitrary",))(dbias, onehot_t)


CONV_STRIP = 128
N_STRIPS = D_FF // CONV_STRIP
CONV_ROWS = 128
HALO = 8


def _strip(rows, half):
    return pl.BlockSpec((rows, CONV_STRIP), lambda j: (0, j + half * N_STRIPS))


def _fill_padded(pad_ref, src_ref, halo, S):
    pad_ref[0:halo, :] = jnp.zeros((halo, CONV_STRIP), F32)
    pad_ref[halo + S:2 * halo + S, :] = jnp.zeros((halo, CONV_STRIP), F32)
    pad_ref[halo:halo + S, :] = src_ref[...].astype(F32)


def _taps(ext, n):
    m = ext.shape[0]
    return pltpu.roll(ext, 1, axis=0)[HALO:HALO + n], ext[HALO:HALO + n], pltpu.roll(ext, m - 1, axis=0)[HALO:HALO + n]


def _conv_gate_fwd(u, conv_w, conv_b, S):
    R = min(CONV_ROWS, S)

    def body(ug_ref, uv_ref, wg_ref, wv_ref, bg_ref, bv_ref, a_ref, gpad, vpad):
        _fill_padded(gpad, ug_ref, HALO, S)
        _fill_padded(vpad, uv_ref, HALO, S)
        wg, wv, bg, bv = wg_ref[...], wv_ref[...], bg_ref[...], bv_ref[...]

        def conv(pad_ref, r0, w, b):
            dn, mid, up = (pad_ref[pl.ds(r0 + HALO + d, R), :] for d in (-1, 0, 1))
            return dn * w[0:1, :] + mid * w[1:2, :] + up * w[2:3, :] + b

        def step(c, carry):
            r0 = pl.multiple_of(c * R, R)
            g = conv(gpad, r0, wg, bg)
            val = conv(vpad, r0, wv, bv)
            a_ref[pl.ds(r0, R), :] = (g * _sigmoid(g) * val).astype(BF16)
            return carry

        lax.fori_loop(0, S // R, step, 0)

    return _pcall(body, name="conv_gate_fwd", grid=(N_STRIPS,),
                  in_specs=[_strip(S, 0), _strip(S, 1), _strip(3, 0), _strip(3, 1), _strip(1, 0), _strip(1, 1)],
                  out_specs=_strip(S, 0), out_shape=jax.ShapeDtypeStruct((S, D_FF), BF16),
                  scratch_shapes=[pltpu.VMEM((S + 2 * HALO, CONV_STRIP), F32)] * 2,
                  dims=("parallel",))(u, u, conv_w, conv_w, conv_b, conv_b)


def _conv_gate_bwd(u, conv_w, conv_b, da, S):
    R = min(CONV_ROWS, S)
    n = R + 2 * HALO

    def body(ug_ref, uv_ref, wg_ref, wv_ref, bg_ref, bv_ref, da_ref, dug_ref, duv_ref, dwg_ref, dwv_ref, dbg_ref, dbv_ref,
             gpad, vpad, dapad):
        _fill_padded(gpad, ug_ref, 2 * HALO, S)
        _fill_padded(vpad, uv_ref, 2 * HALO, S)
        _fill_padded(dapad, da_ref, HALO, S)
        wg, wv, bg, bv = wg_ref[...], wv_ref[...], bg_ref[...], bv_ref[...]

        def conv(pad_ref, r0, w, b):
            dn, mid, up = (pad_ref[pl.ds(r0 + HALO + d, n), :] for d in (-1, 0, 1))
            return dn * w[0:1, :] + mid * w[1:2, :] + up * w[2:3, :] + b, mid[HALO:HALO + R]

        def conv_bwd(duc, u_mid, w, r0, du_ref):
            dn, mid, up = _taps(duc, R)
            du_ref[pl.ds(r0, R), :] = (up * w[0:1, :] + mid * w[1:2, :] + dn * w[2:3, :]).astype(BF16)
            dw = jnp.concatenate([jnp.sum(up * u_mid, axis=0, keepdims=True), jnp.sum(mid * u_mid, axis=0, keepdims=True),
                                  jnp.sum(dn * u_mid, axis=0, keepdims=True)], axis=0)
            return dw, jnp.sum(mid, axis=0, keepdims=True)

        def step(c, carry):
            dw_g, db_g, dw_v, db_v = carry
            r0 = pl.multiple_of(c * R, R)
            g, ug_mid = conv(gpad, r0, wg, bg)
            val, uv_mid = conv(vpad, r0, wv, bv)
            da_ext = dapad[pl.ds(r0, n), :]
            sg = _sigmoid(g)
            ddw_v, ddb_v = conv_bwd(da_ext * (g * sg), uv_mid, wv, r0, duv_ref)
            ddw_g, ddb_g = conv_bwd(da_ext * val * (sg * (1.0 + g * (1.0 - sg))), ug_mid, wg, r0, dug_ref)
            return dw_g + ddw_g, db_g + ddb_g, dw_v + ddw_v, db_v + ddb_v

        z3, z1 = jnp.zeros((3, CONV_STRIP), F32), jnp.zeros((1, CONV_STRIP), F32)
        dwg_ref[...], dbg_ref[...], dwv_ref[...], dbv_ref[...] = lax.fori_loop(0, S // R, step, (z3, z1, z3, z1))

    half = lambda r, dt: (_strip(r, 0), jax.ShapeDtypeStruct((r, D_FF), dt))
    outs = [half(S, BF16), half(S, BF16), half(3, F32), half(3, F32), half(1, F32), half(1, F32)]
    return _pcall(
        body, name="conv_gate_bwd", grid=(N_STRIPS,),
        in_specs=[_strip(S, 0), _strip(S, 1), _strip(3, 0), _strip(3, 1), _strip(1, 0), _strip(1, 1), _strip(S, 0)],
        out_specs=[o[0] for o in outs], out_shape=[o[1] for o in outs],
        scratch_shapes=[pltpu.VMEM((S + 4 * HALO, CONV_STRIP), F32)] * 2 + [pltpu.VMEM((S + 2 * HALO, CONV_STRIP), F32)],
        dims=("parallel",))(u, u, conv_w, conv_w, conv_b, conv_b, da)


MESH = pl.DeviceIdType.MESH
ANY = pl.BlockSpec(memory_space=pl.ANY)


def _place():
    return lax.axis_index("x"), lax.axis_index("y"), lax.axis_index("c")


def _all_gather(shards):
    n_arr = len(shards)

    def body(*refs):
        ins, outs = refs[:n_arr], refs[n_arr:2 * n_arr]
        send_sems, recv_sems, local_sems = refs[2 * n_arr:]
        x, y, c = _place()
        me, sibling = (x, y, c), (x, y, 1 - c)
        chips = [(1 - x, y), (x, 1 - y), (1 - x, 1 - y)]

        def slot(a, p):
            return outs[a].at[4 * p[0] + 2 * p[1] + p[2]]

        def copy(a, k, block, to, src=None):
            return pltpu.make_async_remote_copy(
                src_ref=slot(a, block) if src is None else src, dst_ref=slot(a, block),
                send_sem=send_sems.at[a, k], recv_sem=recv_sems.at[a, k], device_id=to, device_id_type=MESH)

        mine = [pltpu.make_async_copy(ins[a], slot(a, me), local_sems.at[a]) for a in range(n_arr)]
        for cp in mine:
            cp.start()
        first = []
        for a in range(n_arr):
            first.append(copy(a, 0, me, sibling, src=ins[a]))
            first += [copy(a, 1 + j, me, (*chip, c), src=ins[a]) for j, chip in enumerate(chips)]
        for cp in first:
            cp.start()
        passed = []
        for j, chip in enumerate(chips):
            for a in range(n_arr):
                copy(a, 1 + j, (*chip, c), me).wait_recv()
                cp = copy(a, 4 + j, (*chip, c), sibling)
                cp.start()
                passed.append(cp)
        for a in range(n_arr):
            copy(a, 0, sibling, me).wait_recv()
            for j, chip in enumerate(chips):
                copy(a, 4 + j, (*chip, 1 - c), me).wait_recv()
        for cp in first + passed:
            cp.wait_send()
        for cp in mine:
            cp.wait()

    return pl.pallas_call(
        body, name="all_gather_weights",
        in_specs=[ANY] * n_arr, out_specs=[ANY] * n_arr,
        out_shape=[jax.ShapeDtypeStruct((N_DEV, *s.shape), s.dtype) for s in shards],
        scratch_shapes=[pltpu.SemaphoreType.DMA((n_arr, 7)), pltpu.SemaphoreType.DMA((n_arr, 7)), pltpu.SemaphoreType.DMA((n_arr,))],
    )(*shards)


def _xchg_out_shapes(stacked, replicated):
    return ([jax.ShapeDtypeStruct(s.shape, s.dtype) for s in stacked]
            + [jax.ShapeDtypeStruct((N_DEV, *r.shape), r.dtype) for r in replicated])


def _xchg_sems(n_arr):
    return [pltpu.SemaphoreType.DMA((n_arr, 7)), pltpu.SemaphoreType.DMA((n_arr, 7)), pltpu.SemaphoreType.DMA((n_arr,))]


def _xchg_copies(ins, outs, sems, n_st, with_recv):
    send_sems, recv_sems, local_sems = sems
    n_arr = len(ins)
    x, y, c = _place()
    me = 4 * x + 2 * y + c

    def src(a, idx):
        return ins[a].at[idx] if a < n_st else ins[a]

    mine = [pltpu.make_async_copy(src(a, me), outs[a].at[me], local_sems.at[a]) for a in range(n_arr)]
    pairs = []
    for k in range(1, N_DEV):
        px, py, pc = x ^ (k >> 2), y ^ ((k >> 1) & 1), c ^ (k & 1)
        peer = 4 * px + 2 * py + pc
        for a in range(n_arr):
            sems_k = dict(send_sem=send_sems.at[a, k - 1], recv_sem=recv_sems.at[a, k - 1], device_id_type=MESH)
            send = pltpu.make_async_remote_copy(src_ref=src(a, peer), dst_ref=outs[a].at[me], device_id=(px, py, pc), **sems_k)
            recv = None
            if with_recv:
                recv = pltpu.make_async_remote_copy(src_ref=src(a, peer), dst_ref=outs[a].at[peer], device_id=(x, y, c), **sems_k)
            pairs.append((send, recv))
    return mine, pairs


def _xchg_start(ins, outs, sems, n_st):
    mine, pairs = _xchg_copies(ins, outs, sems, n_st, False)
    for cp in mine:
        cp.start()
    for send, _ in pairs:
        send.start()


def _xchg_finish(ins, outs, sems, n_st):
    mine, pairs = _xchg_copies(ins, outs, sems, n_st, True)
    for _, recv in pairs:
        recv.wait_recv()
    for send, _ in pairs:
        send.wait_send()
    for cp in mine:
        cp.wait()


def _exchange(stacked, replicated, name):
    _, landed = _pcall(lambda: None, name=name, grid=(), in_specs=[], out_specs=[], out_shape=[], comm=(stacked, replicated))()
    return landed


HBM = pl.BlockSpec(memory_space=pltpu.HBM)
SEMS = pl.BlockSpec(memory_space=pltpu.SEMAPHORE)
SIDE_EFFECT = pltpu.SideEffectType.DATAFLOW_SIDE_EFFECTING


N_SPLIT_SEMS = 2 * (N_DEV - 1)


def _split_copies(src, land, sems, with_recv):
    x, y, c = _place()
    me = 4 * x + 2 * y + c
    pairs = []
    for k in range(1, N_DEV):
        px, py, pc = x ^ (k >> 2), y ^ ((k >> 1) & 1), c ^ (k & 1)
        peer = 4 * px + 2 * py + pc
        sems_k = dict(send_sem=sems[k - 1], recv_sem=sems[N_DEV - 1 + k - 1], device_id_type=MESH)
        send = pltpu.make_async_remote_copy(src_ref=src.at[peer], dst_ref=land.at[me], device_id=(px, py, pc), **sems_k)
        recv = None
        if with_recv:
            recv = pltpu.make_async_remote_copy(src_ref=src.at[peer], dst_ref=land.at[peer], device_id=(x, y, c), **sems_k)
        pairs.append((send, recv))
    return pairs


def _exchange_start(stacked, name):
    def body(src, land, *rest):
        for send, _ in _split_copies(src, land, rest[:N_SPLIT_SEMS], False):
            send.start()
        rest[-1][...] = jnp.zeros_like(rest[-1])

    shape = pltpu.HBM(stacked.shape, stacked.dtype)
    res = pl.pallas_call(
        body, name=name, in_specs=[HBM, HBM],
        out_shape=(*[pltpu.SemaphoreType.DMA(())] * N_SPLIT_SEMS, shape, shape, jax.ShapeDtypeStruct((8, 128), F32)),
        out_specs=(*[SEMS] * N_SPLIT_SEMS, HBM, HBM, pl.BlockSpec(memory_space=pltpu.VMEM)),
        input_output_aliases={0: N_SPLIT_SEMS, 1: N_SPLIT_SEMS + 1},
        compiler_params=pltpu.CompilerParams(has_side_effects=SIDE_EFFECT),
    )(pltpu.with_memory_space_constraint(stacked, pltpu.HBM),
      pltpu.with_memory_space_constraint(lax.empty(stacked.shape, stacked.dtype), pltpu.HBM))
    return res[:N_SPLIT_SEMS], res[N_SPLIT_SEMS], res[N_SPLIT_SEMS + 1], res[-1]


def _exchange_wait(sems, src, land, after, name):
    def body(src_ref, land_ref, *rest):
        for send, recv in _split_copies(src_ref, land_ref, rest[:N_SPLIT_SEMS], True):
            send.wait_send()
            recv.wait_recv()

    shape = pltpu.HBM(src.shape, src.dtype)
    return pl.pallas_call(
        body, name=name, in_specs=[HBM, HBM, *[SEMS] * N_SPLIT_SEMS, ANY],
        out_shape=(shape, shape), out_specs=(HBM, HBM), input_output_aliases={0: 0, 1: 1},
        compiler_params=pltpu.CompilerParams(has_side_effects=SIDE_EFFECT))(src, land, *sems, after)[1]


def _adamw(parts, w, m, v, name):
    _, R, C = w.shape
    tr = R if R <= 512 else max(t for t in range(16, 513, 16) if R % t == 0)
    pr = tr if parts.shape[1] == R else -(-R // 16) * 16
    assert pr == tr or tr == R

    def body(p_ref, w_ref, m_ref, v_ref, g_ref, d_ref, nm_ref, nv_ref):
        g = p_ref[0].astype(F32)[:tr]
        for s in range(1, N_DEV):
            g = g + p_ref[s].astype(F32)[:tr]
        m2 = ADAM_B1 * m_ref[0] + (1.0 - ADAM_B1) * g
        v2 = ADAM_B2 * v_ref[0] + (1.0 - ADAM_B2) * (g * g)
        m_hat = m2 / (1.0 - ADAM_B1 ** ADAM_STEP)
        v_hat = v2 / (1.0 - ADAM_B2 ** ADAM_STEP)
        g_ref[0] = g
        d_ref[0] = -ADAM_LR * (m_hat / (jnp.sqrt(v_hat) + ADAM_EPS) + ADAM_WD * w_ref[0])
        nm_ref[0] = m2
        nv_ref[0] = v2

    blk = pl.BlockSpec((1, tr, C), lambda i: (0, i, 0))
    return _pcall(body, name=name, grid=(R // tr,),
                  in_specs=[pl.BlockSpec((N_DEV, pr, C), lambda i: (0, i, 0)), blk, blk, blk],
                  out_specs=[blk] * 4, out_shape=[jax.ShapeDtypeStruct((1, R, C), F32)] * 4,
                  dims=("parallel",))(parts, w, m, v)


def _t5_bucket(rel):
    nb = NUM_BUCKETS // 2
    max_exact = nb // 2
    base = (rel > 0).astype(jnp.int32) * nb
    n = jnp.abs(rel)
    nf = jnp.maximum(n, 1).astype(jnp.float32)
    large = max_exact + (jnp.log(nf / max_exact) / math.log(MAX_DISTANCE / max_exact) * (nb - max_exact)).astype(jnp.int32)
    large = jnp.minimum(large, nb - 1)
    return base + jnp.where(n < max_exact, n, large)


def _unstack_cols(g):
    return jnp.transpose(g, (1, 0, 2)).reshape(g.shape[1], N_DEV * g.shape[2])


def _stack_cols(w, n=N_DEV):
    R = w.shape[0]
    return jnp.transpose(w.reshape(R, n, w.shape[1] // n), (1, 0, 2))


def _stack_halves(g, v):
    return jnp.concatenate([_stack_cols(g, N_DEV // 2), _stack_cols(v, N_DEV // 2)], axis=0)


def kernel(x, positions, norm1_g, w_in, q_a_norm_g, w_q_b, kv_a_norm_g, w_kv_b, rel_bias, sinks, w_out, norm2_g, w_up, conv_w, conv_b, w_down, final_norm_g, loss_target, m_norm1_g, m_w_in, m_q_a_norm_g, m_w_q_b, m_kv_a_norm_g, m_w_kv_b, m_rel_bias, m_sinks, m_w_out, m_norm2_g, m_w_up, m_conv_w, m_conv_b, m_w_down, m_final_norm_g, v_norm1_g, v_w_in, v_q_a_norm_g, v_w_q_b, v_kv_a_norm_g, v_w_kv_b, v_rel_bias, v_sinks, v_w_out, v_norm2_g, v_w_up, v_conv_w, v_conv_b, v_w_down, v_final_norm_g):
    S = x.shape[1]
    x = x[0]
    target = loss_target[0]
    TM = 256

    tr = lambda w: jnp.swapaxes(w, 1, 2)
    g_in, g_qb, g_kvb = _all_gather([tr(w_in)[0].astype(BF16), tr(w_q_b)[0].astype(BF16), w_kv_b[0].astype(BF16)])
    late_weights = [w_out[0].astype(BF16), tr(w_up)[0].astype(BF16), conv_w[0]]
    wi = g_in.reshape(W_IN_COLS, D_MODEL)
    c0, c1, c2, c3, c4, c5 = (sum(W_IN_SIZES[:i + 1]) for i in range(6))
    w_in_pt = jnp.concatenate([wi[c4:c5], wi[c5:], wi[c1:c2], wi[:c0], wi[c2:c3], wi[c3:c4],
                               wi[c0:c0 + KV_LORA], wi[c0 + KV_LORA:c1], jnp.zeros((64, D_MODEL), BF16)], axis=0)
    wq = g_qb.reshape(H_A, QK_HEAD, Q_LORA)
    w_qb_pt = jnp.concatenate([wq[:, :QK_NOPE].reshape(H_A * QK_NOPE, Q_LORA), wq[:, QK_NOPE:].reshape(H_A * QK_ROPE, Q_LORA)], axis=0)
    w_kvb = _unstack_cols(g_kvb)

    half = QK_ROPE // 2
    inv_freq = ROPE_THETA ** (-jnp.arange(half, dtype=F32) / half)
    ang = positions.astype(F32)[:, None] * inv_freq[None, :]
    cos, sin = jnp.cos(ang), jnp.sin(ang)
    qa = jnp.arange(Q_BLOCK, dtype=jnp.int32)[:, None]
    kc = jnp.arange(SPAN, dtype=jnp.int32)[None, :]
    rel = (kc - WINDOW - qa).T
    in_band = (jnp.abs(rel) <= WINDOW).astype(F32).reshape(1, Q_BLOCK * SPAN)
    onehot_t = (_t5_bucket(rel).reshape(1, Q_BLOCK * SPAN) == jnp.arange(NUM_BUCKETS, dtype=jnp.int32)[:, None]).astype(F32)
    bias_t = _bias_table(rel_bias.T, onehot_t, in_band).reshape(H_B, SPAN, Q_BLOCK)
    sinks_b = jnp.broadcast_to(sinks.reshape(H_B, 1), (H_B, Q_BLOCK))

    (h1,) = _rowwise(lambda a, g: (_rms(a, g),), "norm1", S, TM, [_rows(x), _whole(norm1_g)], [("rows", D_MODEL, BF16)])
    proj = _matmul(h1, w_in_pt, "nt", BF16, "proj")

    def lat_fn(qlat, ckv, kr, gq, gkv, cs, sn):
        r1, r2 = _rope(kr[:, :half], kr[:, half:QK_ROPE], cs, sn)
        return _rms(qlat, gq), _rms(ckv, gkv), jnp.concatenate([r1, r2], axis=1)

    qn, ckvn, k_rope = _rowwise(lat_fn, "latents", S, TM,
                                [_rows(proj, 256, PROJ_QLAT), _rows(proj, 128, PROJ_CKV), _rows(proj, 128, PROJ_KROPE),
                                 _whole(q_a_norm_g), _whole(kv_a_norm_g), _rows(cos), _rows(sin)],
                                [("rows", Q_LORA, BF16), ("rows", KV_LORA, BF16), ("rows", QK_ROPE, BF16)])
    def q_heads_fn(q, cs, sn):
        q = q * MLA_PRESCALE
        outs = []
        for h in range(H_A):
            o = H_A * QK_NOPE + QK_ROPE * h
            r1, r2 = _rope(q[:, o:o + half], q[:, o + half:o + QK_ROPE], cs, sn)
            outs.append(jnp.concatenate([q[:, QK_NOPE * h:QK_NOPE * (h + 1)], r1, r2], axis=1)[None])
        return (jnp.concatenate(outs, axis=0),)

    (q_full,) = _matmul(qn, w_qb_pt, "nt", None, "q_up_heads", tm=512, tn=1536,
                        epi=(q_heads_fn, [_rows(cos), _rows(sin)], [("heads", H_A, QK_HEAD, BF16)]))

    def k_heads_fn(kvf, kr):
        return kvf, jnp.concatenate([jnp.concatenate([kvf[:, 256 * h:256 * h + QK_NOPE], kr], axis=1)[None] for h in range(H_A)], axis=0)

    kv, k_full = _matmul(ckvn, w_kvb, "nn", None, "kv_up_heads", tm=512, tn=2048,
                         epi=(k_heads_fn, [_rows(k_rope)], [("rows", H_A * (QK_NOPE + V_DIM), BF16), ("heads", H_A, QK_HEAD, BF16)]))
    (o_a, lse), (g_out, g_up, g_cw) = _mla_fwd(q_full, k_full, kv, S, comm=([], late_weights))
    w_out_f = g_out.reshape(D_MODEL, D_MODEL)
    w_up_t = g_up.reshape(2 * D_FF, D_MODEL)
    conv_w_f = _unstack_cols(g_cw)

    o_b, (g_down,) = _win_fwd(proj, bias_t, sinks_b, S, comm=([], [w_down[0].astype(BF16)]))
    w_down_f = g_down.reshape(D_FF, D_MODEL)

    (mixed,) = _rowwise(lambda ga, gb, oa, ob: (_sigmoid(ga) * oa + _sigmoid(gb) * ob,), "gate_mix", S, TM,
                        [_rows(proj, 1024, PROJ_GA), _rows(proj, 1024, PROJ_GB), _rows(o_a), _rows(o_b)], [("rows", D_MODEL, BF16)])
    x1, h2 = _matmul(mixed, w_out_f, "nn", None, "out_proj", residual=x, tm=512,
                     epi=(lambda a, g: (a, _rms(a, g)), [_whole(norm2_g)], [("rows", D_MODEL, F32), ("rows", D_MODEL, BF16)]))
    u = _matmul(h2, w_up_t, "nt", BF16, "ffn_up", tn=1408)
    act = _conv_gate_fwd(u, conv_w_f, conv_b, S)

    def final_fn(a, g, t):
        err = _rms(a, g) - t
        loss = 0.5 * jnp.sum(jnp.mean(err * err, axis=-1, keepdims=True), axis=0, keepdims=True)
        dx, dg = _rms_bwd(err * (1.0 / D_MODEL), a, g)
        return dx, dx, dg, jnp.broadcast_to(loss, (1, 128))

    gfin = final_norm_g.reshape(1, D_MODEL)
    dx2, dx2_b, d_gfin, loss_row = _matmul(
        act, w_down_f, "nn", None, "ffn_down_loss", residual=x1, tm=512,
        epi=(final_fn, [_whole(gfin), _rows(target)],
             [("rows", D_MODEL, F32), ("rows", D_MODEL, BF16), ("acc", 1, D_MODEL), ("acc", 1, 128)]))
    d_act = _matmul(dx2_b, w_down_f, "nt", BF16, "ffn_down_dx", tn=1408)
    d_w_down = _matmul(act, dx2_b, "tn", BF16, "ffn_down_dw")
    du_g, du_v, dcw_g, dcw_v, dcb_g, dcb_v = _conv_gate_bwd(u, conv_w_f, conv_b, d_act, S)
    d_conv_b = jnp.concatenate([dcb_g, dcb_v], axis=1)

    def norm_bwd_fn(dh, a, g, dres):
        dx, dg = _rms_bwd(dh, a, g)
        dx = dx + dres
        return dx, dx, dg

    dx1, dx1_b, d_g2 = _matmul(du_g, w_up_t, "nn", None, "ffn_up_dx_norm2_bwd", tm=256, a2=du_v,
                               epi=(norm_bwd_fn, [_rows(x1), _whole(norm2_g), _rows(dx2)],
                                    [("rows", D_MODEL, F32), ("rows", D_MODEL, BF16), ("acc", 1, D_MODEL)]))
    d_w_up_t = _matmul(du_g, h2, "tn", BF16, "ffn_up_dw", tm=256, a2=du_v)
    d_w_out = _matmul(mixed, dx1_b, "tn", BF16, "out_proj_dw", tm=512)

    def gate_bwd_fn(dm, ga, gb, oa, ob):
        sa, sb = _sigmoid(ga), _sigmoid(gb)
        return jnp.concatenate([dm * oa * sa * (1.0 - sa), dm * ob * sb * (1.0 - sb)], axis=1), dm * sa, dm * sb

    d_proj, do_a, do_b = _matmul(
        dx1_b, w_out_f, "nt", None, "out_proj_dx_gate_bwd", tm=512,
        epi=(gate_bwd_fn, [_rows(proj, 1024, PROJ_GA), _rows(proj, 1024, PROJ_GB), _rows(o_a), _rows(o_b)],
             [("cols", 2 * D_MODEL, 0, PROJ_P, BF16), ("rows", D_MODEL, BF16), ("rows", D_MODEL, BF16)]))

    d_proj, dk_acc, dv_acc, d_bias, d_sinks_g = _win_bwd(proj, bias_t, sinks_b, do_b, d_proj, S)
    d_rel_bias = _bias_table_bwd(d_bias.reshape(H_B, Q_BLOCK * SPAN), onehot_t).T
    d_sinks = d_sinks_g.reshape(1, H_B)

    early = [d_w_out.reshape(N_DEV, D_MODEL // N_DEV, D_MODEL), d_w_up_t.reshape(N_DEV, 2 * D_FF // N_DEV, D_MODEL),
             d_w_down.reshape(N_DEV, D_FF // N_DEV, D_MODEL), _stack_halves(dcw_g, dcw_v)]
    (dq_full, dk_full, dv_full), recv_early = _mla_bwd(q_full, k_full, kv, do_a, o_a, lse, S, comm=(early, []))

    def dq_post_fn(dq, cs, sn):
        nope = [dq[h, :, :QK_NOPE] for h in range(H_A)]
        rope = []
        for h in range(H_A):
            rope += list(_rope_bwd(dq[h, :, QK_NOPE:QK_NOPE + half], dq[h, :, QK_NOPE + half:], cs, sn))
        return (jnp.concatenate(nope + rope, axis=1),)

    (dq_p,) = _rowwise(dq_post_fn, "dq_post", S, TM, [_heads(dq_full), _rows(cos), _rows(sin)], [("rows", H_A * QK_HEAD, BF16)])

    def dkv_post_fn(dk, dv, cs, sn):
        dk = dk * math.log(2.0)
        dkv = jnp.concatenate([jnp.concatenate([dk[h, :, :QK_NOPE], dv[h]], axis=1) for h in range(H_A)], axis=1)
        dkr = dk[0, :, QK_NOPE:]
        for h in range(1, H_A):
            dkr = dkr + dk[h, :, QK_NOPE:]
        r1, r2 = _rope_bwd(dkr[:, :half], dkr[:, half:], cs, sn)
        return dkv, jnp.concatenate([r1, r2], axis=1)

    dkv, d_krope = _rowwise(dkv_post_fn, "dkv_post", S, TM, [_heads(dk_full), _heads(dv_full), _rows(cos), _rows(sin)],
                            [("rows", H_A * (QK_NOPE + V_DIM), BF16), ("rows", QK_ROPE, F32)])
    d_qn = _matmul(dq_p, w_qb_pt, "nn", F32, "q_up_dx")
    d_w_qb_pt = _matmul(dq_p, qn, "tn", BF16, "q_up_dw", tm=512)
    d_ckvn = _matmul(dkv, w_kvb, "nt", F32, "kv_up_dx")
    d_w_kvb = _matmul(ckvn, dkv, "tn", BF16, "kv_up_dw", tn=2048)

    def lat_bwd_fn(dqn, dckvn, dkr, qlat, ckv, gq, gkv, dkb, dvb):
        dql, dgq = _rms_bwd(dqn, qlat, gq)
        dck, dgkv = _rms_bwd(dckvn, ckv, gkv)
        tail = jnp.concatenate([dql, dkb, dvb, dck, dkr, jnp.zeros_like(dkr)], axis=1)
        return tail, dgq, dgkv

    shifted = lambda arr: (arr, lambda tm: pl.BlockSpec((tm, arr.shape[1]), lambda i, *_: (i + WINDOW // tm, 0)))
    TL = min(128, S)
    d_proj, d_gq, d_gkv = _rowwise(lat_bwd_fn, "latents_bwd", S, TL,
                                   [_rows(d_qn), _rows(d_ckvn), _rows(d_krope), _rows(proj, 256, PROJ_QLAT), _rows(proj, 128, PROJ_CKV),
                                    _whole(q_a_norm_g), _whole(kv_a_norm_g), shifted(dk_acc), shifted(dv_acc)],
                                   [("cols", 1024, 3, PROJ_P, BF16), ("acc", 1, Q_LORA), ("acc", 1, KV_LORA)], into=(d_proj, 0))
    d_w_qb_t = jnp.concatenate([d_w_qb_pt[:H_A * QK_NOPE].reshape(H_A, QK_NOPE, Q_LORA),
                                d_w_qb_pt[H_A * QK_NOPE:].reshape(H_A, QK_ROPE, Q_LORA)], axis=1)
    dp, recv_mid = _matmul(d_proj, h1, "tn", BF16, "proj_dw", tm=512, comm=([d_w_qb_t, _stack_cols(d_w_kvb)], []))

    late = jnp.concatenate([dp[3072:3328], dp[3840:3968], dp[3968:4032], dp[2048:3072], dp[3328:3584],
                            dp[3584:3840], dp[0:1024], dp[1024:2048]], axis=0).reshape(N_DEV, W_IN_COLS // N_DEV, D_MODEL)
    late_sems, late_src, late_land, started = _exchange_start(late, "late_grads_start")

    def norm1_bwd_fn(dh, a, g, dres):
        dx, dg = _rms_bwd(dh, a, g)
        return dx + dres, dg

    grad_x, d_g1 = _matmul(
        d_proj, w_in_pt, "nn", None, "proj_dx_norm1_bwd", tm=512,
        epi=(norm1_bwd_fn, [_rows(x), _whole(norm1_g + started[:1, :1]), _rows(dx1)], [("rows", D_MODEL, F32), ("acc", 1, D_MODEL)]))

    transposed = ("w_in", "w_q_b", "w_up")
    ready_names = ["w_out", "w_up", "w_down", "conv_w", "w_q_b", "w_kv_b"]
    ready_wmv = [(w_out, m_w_out, v_w_out), (tr(w_up), tr(m_w_up), tr(v_w_up)), (w_down, m_w_down, v_w_down), (conv_w, m_conv_w, v_conv_w),
                 (tr(w_q_b), tr(m_w_q_b), tr(v_w_q_b)), (w_kv_b, m_w_kv_b, v_w_kv_b)]
    big = {n: _adamw(r, *wmv, "adamw_" + n) for n, r, wmv in zip(ready_names, [*recv_early, *recv_mid], ready_wmv)}

    small_parts = [d_g1, d_gq, d_gkv, d_rel_bias.reshape(1, NUM_BUCKETS * H_B), d_sinks, d_g2, d_conv_b, d_gfin, loss_row[:, :1]]
    small = jnp.concatenate(small_parts, axis=1)
    n_small = small.shape[1]
    pad = (-n_small) % 128
    small = jnp.pad(small, ((0, 0), (0, pad)))
    small, _ = lax.optimization_barrier((small, [big[n][0] for n in ready_names]))
    (recv_small,) = _exchange([], [small], "exchange_small_grads")

    def flat(a):
        return a.reshape(1, -1)

    small_w = [norm1_g, q_a_norm_g, kv_a_norm_g, rel_bias, sinks, norm2_g, conv_b, final_norm_g]
    small_m = [m_norm1_g, m_q_a_norm_g, m_kv_a_norm_g, m_rel_bias, m_sinks, m_norm2_g, m_conv_b, m_final_norm_g]
    small_v = [v_norm1_g, v_q_a_norm_g, v_kv_a_norm_g, v_rel_bias, v_sinks, v_norm2_g, v_conv_b, v_final_norm_g]
    cat = lambda parts: jnp.pad(jnp.concatenate([flat(a) for a in parts], axis=1), ((0, 0), (0, pad + 1)))[None]
    sm = _adamw(recv_small, cat(small_w), cat(small_m), cat(small_v), "adamw_small")

    landed = _exchange_wait(late_sems, late_src, late_land, sm[0], "late_grads_wait")
    me = 4 * lax.axis_index("x") + 2 * lax.axis_index("y") + lax.axis_index("c")
    landed = lax.dynamic_update_slice_in_dim(landed, lax.dynamic_slice_in_dim(late, me, 1, axis=0), me, axis=0)
    big["w_in"] = _adamw(landed, tr(w_in), tr(m_w_in), tr(v_w_in), "adamw_w_in")

    loss = sm[0][0, 0, n_small - 1]
    order =["norm1_g", "w_in", "q_a_norm_g", "w_q_b", "kv_a_norm_g", "w_kv_b", "rel_bias", "sinks", "w_out", "norm2_g", "w_up",
             "conv_w", "conv_b", "w_down", "final_norm_g"]
    small_names = ["norm1_g", "q_a_norm_g", "kv_a_norm_g", "rel_bias", "sinks", "norm2_g", "conv_b", "final_norm_g"]
    offs, o = {}, 0
    for n, a in zip(small_names, small_w):
        offs[n] = (o, a.size, a.shape)
        o += a.size
    outs = [loss, grad_x[None]]
    for kind in range(4):
        for n in order:
            if n in big:
                outs.append(tr(big[n][kind]) if n in transposed else big[n][kind])
            else:
                o, size, shape = offs[n]
                outs.append(sm[kind][0, 0, o:o + size].reshape(shape))
    return tuple(outs)
```

```python
import math

import jax
import jax.numpy as jnp
from jax import lax
from jax.experimental import pallas as pl
from jax.experimental.pallas import tpu as pltpu

F32 = jnp.float32
BF16 = jnp.bfloat16

N_DEV = 8
D_MODEL = 1024
EPS = 1e-6
H_A, QK_NOPE, QK_ROPE, V_DIM, Q_LORA, KV_LORA = 8, 128, 64, 128, 256, 128
QK_HEAD = QK_NOPE + QK_ROPE
ROPE_THETA = 10000.0
H_B, KV_B, GROUP, HD_B, WINDOW, Q_BLOCK = 16, 4, 4, 64, 128, 128
SPAN = Q_BLOCK + 2 * WINDOW
NUM_BUCKETS, MAX_DISTANCE = 32, 128
D_FF = 2816
ADAM_LR, ADAM_B1, ADAM_B2, ADAM_EPS, ADAM_WD, ADAM_STEP = 0.001, 0.9, 0.999, 1e-08, 0.01, 10

W_IN_SIZES = (Q_LORA, KV_LORA + QK_ROPE, H_B * HD_B, KV_B * HD_B, KV_B * HD_B, D_MODEL, D_MODEL)
W_IN_COLS = sum(W_IN_SIZES)
PROJ_P = 4096
PROJ_GA, PROJ_GB, PROJ_QB, PROJ_QLAT, PROJ_KB, PROJ_VB, PROJ_CKV, PROJ_KROPE = 0, 1, 2, 12, 13, 14, 30, 31

VMEM_LIMIT = 56 * 1024 * 1024

NN = (((1,), (0,)), ((), ()))
NT = (((1,), (1,)), ((), ()))
TN = (((0,), (0,)), ((), ()))


def _pcall(body, *, name, grid, in_specs, out_specs, out_shape, scratch_shapes=(), dims=None, comm=None, aliases=None):
    if comm is None:
        params = pltpu.CompilerParams(dimension_semantics=dims, vmem_limit_bytes=VMEM_LIMIT)
        return pl.pallas_call(body, name=name, grid=grid, in_specs=in_specs, out_specs=out_specs, out_shape=out_shape,
                              scratch_shapes=list(scratch_shapes), input_output_aliases=aliases or {}, compiler_params=params)
    assert not aliases
    stacked, replicated = comm
    arrs = [*stacked, *replicated]
    n_st, n_arr = len(stacked), len(arrs)
    single = not isinstance(out_specs, (list, tuple))
    o_specs, o_shape = ([out_specs], [out_shape]) if single else (list(out_specs), list(out_shape))
    n_in, n_out = len(in_specs), len(o_specs)

    def wrapped(*refs):
        c_in = refs[n_in:n_in + n_arr]
        c_out = refs[n_in + n_arr + n_out:n_in + 2 * n_arr + n_out]
        sems = refs[len(refs) - 3:]
        own = (*refs[:n_in], *refs[n_in + n_arr:n_in + n_arr + n_out], *refs[n_in + 2 * n_arr + n_out:len(refs) - 3])
        if not grid:
            _xchg_start(c_in, c_out, sems, n_st)
            _xchg_finish(c_in, c_out, sems, n_st)
            return
        first = last = None
        for d, n in enumerate(grid):
            pid = pl.program_id(d)
            first = (pid == 0) if first is None else first & (pid == 0)
            last = (pid == n - 1) if last is None else last & (pid == n - 1)

        @pl.when(first)
        def _():
            _xchg_start(c_in, c_out, sems, n_st)

        body(*own)

        @pl.when(last)
        def _():
            _xchg_finish(c_in, c_out, sems, n_st)

    params = pltpu.CompilerParams(dimension_semantics=("arbitrary",) * len(grid), vmem_limit_bytes=VMEM_LIMIT)
    call = pl.pallas_call(wrapped, name=name, grid=grid, in_specs=[*in_specs, *[ANY] * n_arr], out_specs=[*o_specs, *[ANY] * n_arr],
                          out_shape=[*o_shape, *_xchg_out_shapes(stacked, replicated)],
                          scratch_shapes=[*scratch_shapes, *_xchg_sems(n_arr)], compiler_params=params)

    def run(*args):
        res = call(*args, *arrs)
        outs, landed = res[:n_out], res[n_out:]
        return (outs[0] if single else outs), landed

    return run


def _dot(a, b, dn):
    return lax.dot_general(a, b, dn, preferred_element_type=F32)


def _tile(n, target):
    best = None
    for t in range(128, min(n, target) + 1, 128):
        if n % t == 0:
            best = t
    return n if best is None else best


def _matmul(a, b, mode, out_dtype, name, residual=None, tm=1024, tn=1024, comm=None, a2=None, epi=None):
    if mode == "nn":
        (M, K), N = a.shape, b.shape[1]
    elif mode == "nt":
        (M, K), N = a.shape, b.shape[0]
    else:
        (K, M), N = a.shape, b.shape[1]
    tm, tn = _tile(M, tm), _tile(N, tn)
    a_spec = pl.BlockSpec((K, tm), lambda i, j: (0, i)) if mode == "tn" else pl.BlockSpec((tm, K), lambda i, j: (i, 0))
    b_spec = pl.BlockSpec((tn, b.shape[1]), lambda i, j: (j, 0)) if mode == "nt" else pl.BlockSpec((K, tn), lambda i, j: (0, j))
    o_spec = pl.BlockSpec((tm, tn), lambda i, j: (i, j))
    in_specs, args = [a_spec, b_spec], [a, b]
    n1 = M // tm
    if a2 is not None and mode == "tn":
        assert M % tm == 0 and a2.shape[1] % tm == 0
        in_specs[0] = pl.BlockSpec((K, tm), lambda i, j: (0, jnp.minimum(i, n1 - 1)))
        in_specs.append(pl.BlockSpec((K, tm), lambda i, j: (0, jnp.maximum(i - n1, 0))))
        args.append(a2)
        M += a2.shape[1]
    elif a2 is not None:
        assert (mode == "nt" and K + a2.shape[1] == b.shape[1]) or (mode == "nn" and K + a2.shape[1] == b.shape[0])
        if mode == "nn":
            b_spec = in_specs[1] = pl.BlockSpec((b.shape[0], tn), lambda i, j: (0, j))
        in_specs.append(pl.BlockSpec((tm, a2.shape[1]), lambda i, j: (i, 0)))
        args.append(a2)
    if residual is not None:
        in_specs.append(o_spec)
        args.append(residual)
    n_mm = len(args)
    scratch = [pltpu.VMEM((tm, K), a.dtype)] if mode == "tn" else []
    if epi is None:
        out_specs, out_shape, is_acc = o_spec, jax.ShapeDtypeStruct((M, N), out_dtype), None
    else:
        assert tn == N
        fn, epi_ins, epi_outs = epi
        in_specs += [mk(tm) for _, mk in epi_ins]
        args += [arr for arr, _ in epi_ins]
        out_specs, out_shape, is_acc = _row_out_specs(epi_outs, M, tm)

    def body(*refs):
        a_ref, b_ref = refs[0], refs[1]
        n_out = 1 if epi is None else len(is_acc)
        out_refs = refs[len(args):len(args) + n_out]
        if mode == "tn":
            at_ref = refs[len(args) + n_out]

            first_col = pl.program_id(1) == 0
            from_a = first_col if a2 is None else first_col & (pl.program_id(0) < n1)

            @pl.when(from_a)
            def _():
                at_ref[...] = a_ref[...].T

            if a2 is not None:
                @pl.when(first_col & (pl.program_id(0) >= n1))
                def _():
                    at_ref[...] = refs[2][...].T

            acc = _dot(at_ref[...], b_ref[...], NN)
        elif a2 is not None and mode == "nt":
            acc = _dot(a_ref[...], b_ref[:, :K], NT) + _dot(refs[2][...], b_ref[:, K:], NT)
        elif a2 is not None:
            acc = _dot(a_ref[...], b_ref[:K, :], NN) + _dot(refs[2][...], b_ref[K:, :], NN)
        else:
            acc = _dot(a_ref[...], b_ref[...], NT if mode == "nt" else NN)
        if residual is not None:
            acc = acc + refs[n_mm - 1][...]
        if epi is None:
            out_refs[0][...] = acc.astype(out_dtype)
        else:
            _store_rows(out_refs, fn(acc, *[_load_f32(r) for r in refs[n_mm:len(args)]]), is_acc)

    return _pcall(body, name=name, grid=(M // tm, N // tn), in_specs=in_specs, out_specs=out_specs,
                  out_shape=out_shape, scratch_shapes=scratch,
                  dims=("arbitrary" if epi is not None else "parallel", "arbitrary"), comm=comm)(*args)


def _rows(arr, width=None, col=0):
    width = arr.shape[1] if width is None else width
    return (arr, lambda tm: pl.BlockSpec((tm, width), lambda i, *_: (i, col)))


def _heads(arr):
    return (arr, lambda tm: pl.BlockSpec((arr.shape[0], tm, arr.shape[2]), lambda i, *_: (0, i, 0)))


def _whole(arr):
    nd = arr.ndim
    return (arr, lambda tm: pl.BlockSpec(arr.shape, lambda i, *_: (0,) * nd))


def _row_out_specs(outs, n_rows, tm):
    out_specs, out_shape, is_acc = [], [], []
    for o in outs:
        if o[0] == "rows":
            out_specs.append(pl.BlockSpec((tm, o[1]), lambda i, *_: (i, 0)))
            out_shape.append(jax.ShapeDtypeStruct((n_rows, o[1]), o[2]))
        elif o[0] == "cols":
            out_specs.append(pl.BlockSpec((tm, o[1]), lambda i, *_, c=o[2]: (i, c)))
            out_shape.append(jax.ShapeDtypeStruct((n_rows, o[3]), o[4]))
        elif o[0] == "heads":
            out_specs.append(pl.BlockSpec((o[1], tm, o[2]), lambda i, *_: (0, i, 0)))
            out_shape.append(jax.ShapeDtypeStruct((o[1], n_rows, o[2]), o[3]))
        else:
            out_specs.append(pl.BlockSpec((o[1], o[2]), lambda i, *_: (0, 0)))
            out_shape.append(jax.ShapeDtypeStruct((o[1], o[2]), F32))
        is_acc.append(o[0] == "acc")
    return out_specs, out_shape, is_acc


def _load_f32(r):
    v = r[...]
    return v.astype(F32) if v.dtype == BF16 else v


def _store_rows(out_refs, vals, is_acc):
    for r, v, acc in zip(out_refs, vals, is_acc):
        if acc:
            @pl.when(pl.program_id(0) == 0)
            def _():
                r[...] = jnp.zeros_like(r)

            r[...] += v
        else:
            r[...] = v.astype(r.dtype)


def _rowwise(fn, name, n_rows, tm, ins, outs, upcast=True, into=None):
    tm = min(tm, n_rows)
    assert n_rows % tm == 0
    in_specs = [mk(tm) for _, mk in ins]
    out_specs, out_shape, is_acc = _row_out_specs(outs, n_rows, tm)
    n_in = len(ins)
    args = [a for a, _ in ins]
    aliases = {}
    if into is not None:
        in_specs.append(ANY)
        args.append(into[0])
        aliases = {n_in: into[1]}

    def body(*refs):
        vals = fn(*[_load_f32(r) if upcast else r[...] for r in refs[:n_in]])
        _store_rows(refs[len(args):], vals, is_acc)

    return _pcall(body, name=name, grid=(n_rows // tm,), in_specs=in_specs, out_specs=out_specs,
                  out_shape=out_shape, dims=("arbitrary",), aliases=aliases)(*args)


def _rms(x, g):
    r = lax.rsqrt(jnp.mean(x * x, axis=-1, keepdims=True) + EPS)
    return x * r * g


def _rms_bwd(dy, x, g):
    r = lax.rsqrt(jnp.mean(x * x, axis=-1, keepdims=True) + EPS)
    xhat = x * r
    dxhat = dy * g
    dx = r * (dxhat - xhat * jnp.mean(dxhat * xhat, axis=-1, keepdims=True))
    return dx, jnp.sum(dy * xhat, axis=0, keepdims=True)


def _rope(x1, x2, cos, sin):
    return x1 * cos - x2 * sin, x2 * cos + x1 * sin


def _rope_bwd(d1, d2, cos, sin):
    return d1 * cos + d2 * sin, d2 * cos - d1 * sin


def _sigmoid(x):
    return 1.0 / (1.0 + jnp.exp(-x))


MLA_SCALE = 1.0 / math.sqrt(QK_HEAD)
MLA_PRESCALE = MLA_SCALE * math.log2(math.e)
MLA_TQ, MLA_KC = 1024, 1024


def _mla_fwd(q_full, k_full, kv, S, comm=None):
    tq, kc = min(MLA_TQ, S), min(MLA_KC, S)

    def body(q_ref, k_ref, v_ref, o_ref, lse_ref):
        q = q_ref[0]
        m = jnp.full((tq, 1), -1e30, F32)
        l = jnp.zeros((tq, 1), F32)
        acc = jnp.zeros((tq, V_DIM), F32)
        for c in range(S // kc):
            s = _dot(q, k_ref[0, c * kc:(c + 1) * kc, :], NT)
            m_new = jnp.maximum(m, jnp.max(s, axis=-1, keepdims=True))
            alpha = jnp.exp2(m - m_new)
            p = jnp.exp2(s - m_new)
            l = alpha * l + jnp.sum(p, axis=-1, keepdims=True)
            acc = alpha * acc + _dot(p.astype(BF16), v_ref[c * kc:(c + 1) * kc, :], NN)
            m = m_new
        o_ref[...] = (acc / l).astype(BF16)
        lse_ref[0] = m + jnp.log2(l)

    return _pcall(
        body, name="mla_fwd", grid=(H_A, S // tq),
        in_specs=[pl.BlockSpec((1, tq, QK_HEAD), lambda h, i: (h, i, 0)),
                  pl.BlockSpec((1, S, QK_HEAD), lambda h, i: (h, 0, 0)),
                  pl.BlockSpec((S, V_DIM), lambda h, i: (0, 2 * h + 1))],
        out_specs=[pl.BlockSpec((tq, V_DIM), lambda h, i: (i, h)),
                   pl.BlockSpec((1, tq, 1), lambda h, i: (h, i, 0))],
        out_shape=[jax.ShapeDtypeStruct((S, H_A * V_DIM), BF16), jax.ShapeDtypeStruct((H_A, S, 1), F32)],
        dims=("parallel", "parallel"), comm=comm)(q_full, k_full, kv)


def _mla_bwd(q_full, k_full, kv, do_a, o_a, lse, S, comm=None):
    tq, kc = min(MLA_TQ, S), min(MLA_KC, S)

    def body(q_ref, k_ref, v_ref, do_ref, o_ref, lse_ref, dq_ref, dk_ref, dv_ref):
        @pl.when(pl.program_id(1) == 0)
        def _():
            dk_ref[...] = jnp.zeros_like(dk_ref)
            dv_ref[...] = jnp.zeros_like(dv_ref)

        q = q_ref[0]
        do = do_ref[...]
        lse_q = lse_ref[0]
        delta = jnp.sum(do.astype(F32) * o_ref[...].astype(F32), axis=-1, keepdims=True)
        dq = jnp.zeros((tq, QK_HEAD), F32)
        for c in range(S // kc):
            k = k_ref[0, c * kc:(c + 1) * kc, :]
            v = v_ref[c * kc:(c + 1) * kc, :]
            p = jnp.exp2(_dot(q, k, NT) - lse_q)
            ds = (p * (_dot(do, v, NT) - delta)).astype(BF16)
            dq = dq + _dot(ds, k, NN)
            dk_ref[0, c * kc:(c + 1) * kc, :] += _dot(ds, q, TN)
            dv_ref[0, c * kc:(c + 1) * kc, :] += _dot(p.astype(BF16), do, TN)
        dq_ref[0] = dq * MLA_SCALE

    return _pcall(
        body, name="mla_bwd", grid=(H_A, S // tq),
        in_specs=[pl.BlockSpec((1, tq, QK_HEAD), lambda h, i: (h, i, 0)),
                  pl.BlockSpec((1, S, QK_HEAD), lambda h, i: (h, 0, 0)),
                  pl.BlockSpec((S, V_DIM), lambda h, i: (0, 2 * h + 1)),
                  pl.BlockSpec((tq, V_DIM), lambda h, i: (i, h)),
                  pl.BlockSpec((tq, V_DIM), lambda h, i: (i, h)),
                  pl.BlockSpec((1, tq, 1), lambda h, i: (h, i, 0))],
        out_specs=[pl.BlockSpec((1, tq, QK_HEAD), lambda h, i: (h, i, 0)),
                   pl.BlockSpec((1, S, QK_HEAD), lambda h, i: (h, 0, 0)),
                   pl.BlockSpec((1, S, V_DIM), lambda h, i: (h, 0, 0))],
        out_shape=[jax.ShapeDtypeStruct((H_A, S, QK_HEAD), F32), jax.ShapeDtypeStruct((H_A, S, QK_HEAD), F32),
                   jax.ShapeDtypeStruct((H_A, S, V_DIM), F32)],
        dims=("parallel", "arbitrary"), comm=comm)(q_full, k_full, kv, do_a, o_a, lse)


WIN_SCALE = 1.0 / math.sqrt(HD_B)


WIN_PER_STEP = 4


def _win_specs(S):
    last, B = S // Q_BLOCK - 1, WIN_PER_STEP
    qspec = pl.BlockSpec((B * Q_BLOCK, H_B * HD_B), lambda i: (i, PROJ_QB))
    kspecs = [[pl.BlockSpec((Q_BLOCK, KV_B * HD_B), lambda i, d=d, c=c: (jnp.clip(B * i + d, 0, last), c)) for d in range(-1, B + 1)]
              for c in (PROJ_KB, PROJ_VB)]
    bias_spec = pl.BlockSpec((H_B, SPAN, Q_BLOCK), lambda i: (0, 0, 0))
    sink_spec = pl.BlockSpec((H_B, Q_BLOCK), lambda i: (0, 0))
    return qspec, kspecs, bias_spec, sink_spec


def _win_edge_ok(n, n_blk):
    row = lax.broadcasted_iota(jnp.int32, (SPAN, 1), 0)
    return jnp.logical_not(((n == 0) & (row < WINDOW)) | ((n == n_blk - 1) & (row >= SPAN - WINDOW)))


def _lanes4(pieces):
    return jnp.concatenate(pieces, axis=1)


def _win_probs(kg, q4t, bias_ref, sink_ref, g, edge_ok):
    bias4 = _lanes4([bias_ref[GROUP * g + j] for j in range(GROUP)])
    sink4 = _lanes4([sink_ref[GROUP * g + j:GROUP * g + j + 1, :] for j in range(GROUP)])
    s = jnp.where(edge_ok, _dot(kg, q4t, NN) + bias4, -1e30)
    m = jnp.maximum(jnp.max(s, axis=0, keepdims=True), sink4)
    p = jnp.exp(s - m)
    e_sink = jnp.exp(sink4 - m)
    inv_l = 1.0 / (jnp.sum(p, axis=0, keepdims=True) + e_sink)
    return p * inv_l, e_sink * inv_l


def _group_t(xt, g):
    return _lanes4([xt[HD_B * (GROUP * g + j):HD_B * (GROUP * g + j + 1), :] for j in range(GROUP)])


def _rows_of(ref, b):
    return ref[Q_BLOCK * b:Q_BLOCK * (b + 1), :]


def _win_fwd(proj, bias_t, sinks_b, S, comm=None):
    n_blk, B = S // Q_BLOCK, WIN_PER_STEP
    qspec, kspecs, bias_spec, sink_spec = _win_specs(S)

    def body(q_ref, *refs):
        k_refs, v_refs, (bias_ref, sink_ref, o_ref) = refs[:B + 2], refs[B + 2:2 * B + 4], refs[2 * B + 4:]
        for b in range(B):
            edge_ok = _win_edge_ok(B * pl.program_id(0) + b, n_blk)
            k = jnp.concatenate([r[...] for r in k_refs[b:b + 3]], axis=0)
            vt = jnp.concatenate([r[...] for r in v_refs[b:b + 3]], axis=0).T
            qt = (_rows_of(q_ref, b).astype(F32) * WIN_SCALE).T.astype(BF16)
            parts = []
            for g in range(KV_B):
                p, _ = _win_probs(k[:, HD_B * g:HD_B * (g + 1)], _group_t(qt, g), bias_ref, sink_ref, g, edge_ok)
                o4t = _dot(vt[HD_B * g:HD_B * (g + 1), :], p.astype(BF16), NN)
                parts += [o4t[:, Q_BLOCK * j:Q_BLOCK * (j + 1)] for j in range(GROUP)]
            o_ref[Q_BLOCK * b:Q_BLOCK * (b + 1), :] = jnp.concatenate(parts, axis=0).T.astype(BF16)

    return _pcall(body, name="win_fwd", grid=(n_blk // B,),
                  in_specs=[qspec, *kspecs[0], *kspecs[1], bias_spec, sink_spec],
                  out_specs=pl.BlockSpec((B * Q_BLOCK, H_B * HD_B), lambda i: (i, 0)),
                  out_shape=jax.ShapeDtypeStruct((S, H_B * HD_B), BF16),
                  dims=("parallel",), comm=comm)(*[proj] * (2 * B + 5), bias_t, sinks_b)


def _win_bwd(proj, bias_t, sinks_b, do_b, d_proj, S):
    n_blk, B = S // Q_BLOCK, WIN_PER_STEP
    qspec, kspecs, bias_spec, sink_spec = _win_specs(S)

    def body(q_ref, *refs):
        k_refs, v_refs = refs[:B + 2], refs[B + 2:2 * B + 4]
        bias_ref, sink_ref, do_ref, _, dq_ref, dk_ref, dv_ref, dbias_ref, dsink_ref, dsink_acc = refs[2 * B + 4:]
        i = pl.program_id(0)

        @pl.when(i == 0)
        def _():
            dk_ref[...] = jnp.zeros_like(dk_ref)
            dv_ref[...] = jnp.zeros_like(dv_ref)
            dbias_ref[...] = jnp.zeros_like(dbias_ref)
            dsink_acc[...] = jnp.zeros_like(dsink_acc)

        d_bias, d_sink, dk_blocks, dv_blocks = {}, {}, [], []
        for b in range(B):
            edge_ok = _win_edge_ok(B * i + b, n_blk)
            k = jnp.concatenate([r[...] for r in k_refs[b:b + 3]], axis=0)
            v = jnp.concatenate([r[...] for r in v_refs[b:b + 3]], axis=0)
            kt = k.T
            qt = (_rows_of(q_ref, b).astype(F32) * WIN_SCALE).T.astype(BF16)
            dot_ = _rows_of(do_ref, b).astype(F32).T.astype(BF16)
            dq_parts, dks, dvs = [], [], []
            for g in range(KV_B):
                kg, vg = k[:, HD_B * g:HD_B * (g + 1)], v[:, HD_B * g:HD_B * (g + 1)]
                q4t, do4t = _group_t(qt, g), _group_t(dot_, g)
                p, p_sink = _win_probs(kg, q4t, bias_ref, sink_ref, g, edge_ok)
                dp = _dot(vg, do4t, NN)
                delta = jnp.sum(p * dp, axis=0, keepdims=True)
                ds = p * (dp - delta)
                d_bias[g] = ds if b == 0 else d_bias[g] + ds
                d_sink[g] = -p_sink * delta if b == 0 else d_sink[g] - p_sink * delta
                dsb = ds.astype(BF16)
                dq4t = _dot(kt[HD_B * g:HD_B * (g + 1), :], dsb, NN) * WIN_SCALE
                dq_parts += [dq4t[:, Q_BLOCK * j:Q_BLOCK * (j + 1)] for j in range(GROUP)]
                dks.append(_dot(dsb, q4t, NT))
                dvs.append(_dot(p.astype(BF16), do4t, NT))
            dq_ref[Q_BLOCK * b:Q_BLOCK * (b + 1), :] = jnp.concatenate(dq_parts, axis=0).T.astype(BF16)
            dk_blocks.append(jnp.concatenate(dks, axis=1))
            dv_blocks.append(jnp.concatenate(dvs, axis=1))

        for g in range(KV_B):
            for j in range(GROUP):
                dbias_ref[GROUP * g + j] += d_bias[g][:, Q_BLOCK * j:Q_BLOCK * (j + 1)]
            dsink_acc[g:g + 1, :] += d_sink[g]

        def overlap(blocks):
            out = blocks[0]
            for blk in blocks[1:]:
                keep = out.shape[0] - 2 * Q_BLOCK
                out = jnp.concatenate([out[:keep], out[keep:] + blk[:2 * Q_BLOCK], blk[2 * Q_BLOCK:]], axis=0)
            return out

        rows = pl.ds(pl.multiple_of(i * (B * Q_BLOCK), B * Q_BLOCK), (B + 2) * Q_BLOCK)
        dk_ref[rows, :] += overlap(dk_blocks)
        dv_ref[rows, :] += overlap(dv_blocks)

        @pl.when(i == n_blk // B - 1)
        def _():
            acc = dsink_acc[...]
            dsink_ref[...] = jnp.concatenate(
                [jnp.sum(acc[:, Q_BLOCK * j:Q_BLOCK * (j + 1)], axis=1, keepdims=True) for j in range(GROUP)], axis=1)

    whole = lambda shape: pl.BlockSpec(shape, lambda i: (0,) * len(shape))
    return _pcall(
        body, name="win_bwd", grid=(n_blk // B,),
        in_specs=[qspec, *kspecs[0], *kspecs[1], bias_spec, sink_spec, pl.BlockSpec((B * Q_BLOCK, H_B * HD_B), lambda i: (i, 0)), ANY],
        out_specs=[qspec, whole((S + 2 * WINDOW, KV_B * HD_B)),
                   whole((S + 2 * WINDOW, KV_B * HD_B)), whole((H_B, SPAN, Q_BLOCK)), whole((KV_B, GROUP))],
        out_shape=[jax.ShapeDtypeStruct((S, PROJ_P), BF16), jax.ShapeDtypeStruct((S + 2 * WINDOW, KV_B * HD_B), F32),
                   jax.ShapeDtypeStruct((S + 2 * WINDOW, KV_B * HD_B), F32), jax.ShapeDtypeStruct((H_B, SPAN, Q_BLOCK), F32),
                   jax.ShapeDtypeStruct((KV_B, GROUP), F32)],
        scratch_shapes=[pltpu.VMEM((KV_B, GROUP * Q_BLOCK), F32)],
        dims=("arbitrary",), aliases={2 * B + 8: 0})(*[proj] * (2 * B + 5), bias_t, sinks_b, do_b, d_proj)


def _bias_table(rel_bias_t, onehot_t, in_band):
    def body(rb_ref, oh_ref, band_ref, o_ref):
        t = lax.dot_general(rb_ref[...], oh_ref[...], NN, preferred_element_type=F32, precision=lax.Precision.HIGHEST)
        o_ref[...] = jnp.where(band_ref[...] > 0.5, t, -1e30)

    n = onehot_t.shape[1]
    tn = _tile(n, 8192)
    return _pcall(body, name="bias_table", grid=(n // tn,),
                  in_specs=[pl.BlockSpec((H_B, NUM_BUCKETS), lambda j: (0, 0)), pl.BlockSpec((NUM_BUCKETS, tn), lambda j: (0, j)),
                            pl.BlockSpec((1, tn), lambda j: (0, j))],
                  out_specs=pl.BlockSpec((H_B, tn), lambda j: (0, j)),
                  out_shape=jax.ShapeDtypeStruct((H_B, n), F32), dims=("parallel",))(rel_bias_t, onehot_t, in_band)


def _bias_table_bwd(dbias, onehot_t):
    n = onehot_t.shape[1]
    tk = _tile(n, 8192)

    def body(d_ref, oh_ref, o_ref):
        @pl.when(pl.program_id(0) == 0)
        def _():
            o_ref[...] = jnp.zeros_like(o_ref)

        o_ref[...] += lax.dot_general(d_ref[...], oh_ref[...], NT, preferred_element_type=F32, precision=lax.Precision.HIGHEST)

    return _pcall(body, name="bias_table_bwd", grid=(n // tk,),
                  in_specs=[pl.BlockSpec((H_B, tk), lambda j: (0, j)), pl.BlockSpec((NUM_BUCKETS, tk), lambda j: (0, j))],
                  out_specs=pl.BlockSpec((H_B, NUM_BUCKETS), lambda j: (0, 0)),
                  out_shape=jax.ShapeDtypeStruct((H_B, NUM_BUCKETS), F32), dims=("arbitrary",))(dbias, onehot_t)


CONV_STRIP = 128
N_STRIPS = D_FF // CONV_STRIP
CONV_ROWS = 128
HALO = 8


def _strip(rows, half):
    return pl.BlockSpec((rows, CONV_STRIP), lambda j: (0, j + half * N_STRIPS))


def _fill_padded(pad_ref, src_ref, halo, S):
    pad_ref[0:halo, :] = jnp.zeros((halo, CONV_STRIP), F32)
    pad_ref[halo + S:2 * halo + S, :] = jnp.zeros((halo, CONV_STRIP), F32)
    pad_ref[halo:halo + S, :] = src_ref[...].astype(F32)


def _conv_gate_fwd(u, conv_w, conv_b, S):
    R = min(CONV_ROWS, S)

    def body(ug_ref, uv_ref, wg_ref, wv_ref, bg_ref, bv_ref, a_ref, gpad, vpad):
        _fill_padded(gpad, ug_ref, HALO, S)
        _fill_padded(vpad, uv_ref, HALO, S)
        wg, wv, bg, bv = wg_ref[...], wv_ref[...], bg_ref[...], bv_ref[...]

        def conv(pad_ref, r0, w, b):
            dn, mid, up = (pad_ref[pl.ds(r0 + HALO + d, R), :] for d in (-1, 0, 1))
            return dn * w[0:1, :] + mid * w[1:2, :] + up * w[2:3, :] + b

        def step(c, carry):
            r0 = pl.multiple_of(c * R, R)
            g = conv(gpad, r0, wg, bg)
            val = conv(vpad, r0, wv, bv)
            a_ref[pl.ds(r0, R), :] = (g * _sigmoid(g) * val).astype(BF16)
            return carry

        lax.fori_loop(0, S // R, step, 0)

    return _pcall(body, name="conv_gate_fwd", grid=(N_STRIPS,),
                  in_specs=[_strip(S, 0), _strip(S, 1), _strip(3, 0), _strip(3, 1), _strip(1, 0), _strip(1, 1)],
                  out_specs=_strip(S, 0), out_shape=jax.ShapeDtypeStruct((S, D_FF), BF16),
                  scratch_shapes=[pltpu.VMEM((S + 2 * HALO, CONV_STRIP), F32)] * 2,
                  dims=("parallel",))(u, u, conv_w, conv_w, conv_b, conv_b)


def _conv_gate_bwd(u, conv_w, conv_b, da, S):
    R = min(CONV_ROWS, S)
    n = R + 2 * HALO

    def body(ug_ref, uv_ref, wg_ref, wv_ref, bg_ref, bv_ref, da_ref, dug_ref, duv_ref, dwg_ref, dwv_ref, dbg_ref, dbv_ref,
             gpad, vpad, dapad, tap_scr):
        _fill_padded(gpad, ug_ref, 2 * HALO, S)
        _fill_padded(vpad, uv_ref, 2 * HALO, S)
        _fill_padded(dapad, da_ref, HALO, S)
        wg, wv, bg, bv = wg_ref[...], wv_ref[...], bg_ref[...], bv_ref[...]

        def conv(pad_ref, r0, w, b):
            dn, mid, up = (pad_ref[pl.ds(r0 + HALO + d, n), :] for d in (-1, 0, 1))
            return dn * w[0:1, :] + mid * w[1:2, :] + up * w[2:3, :] + b, mid[HALO:HALO + R]

        def conv_bwd(duc, u_mid, w, r0, du_ref):
            tap_scr[...] = duc
            dn, mid, up = (tap_scr[HALO + d:HALO + d + R, :] for d in (-1, 0, 1))
            du_ref[pl.ds(r0, R), :] = (up * w[0:1, :] + mid * w[1:2, :] + dn * w[2:3, :]).astype(BF16)
            dw = jnp.concatenate([jnp.sum(up * u_mid, axis=0, keepdims=True), jnp.sum(mid * u_mid, axis=0, keepdims=True),
                                  jnp.sum(dn * u_mid, axis=0, keepdims=True)], axis=0)
            return dw, jnp.sum(mid, axis=0, keepdims=True)

        def step(c, carry):
            dw_g, db_g, dw_v, db_v = carry
            r0 = pl.multiple_of(c * R, R)
            g, ug_mid = conv(gpad, r0, wg, bg)
            val, uv_mid = conv(vpad, r0, wv, bv)
            da_ext = dapad[pl.ds(r0, n), :]
            sg = _sigmoid(g)
            ddw_v, ddb_v = conv_bwd(da_ext * (g * sg), uv_mid, wv, r0, duv_ref)
            ddw_g, ddb_g = conv_bwd(da_ext * val * (sg * (1.0 + g * (1.0 - sg))), ug_mid, wg, r0, dug_ref)
            return dw_g + ddw_g, db_g + ddb_g, dw_v + ddw_v, db_v + ddb_v

        z3, z1 = jnp.zeros((3, CONV_STRIP), F32), jnp.zeros((1, CONV_STRIP), F32)
        dwg_ref[...], dbg_ref[...], dwv_ref[...], dbv_ref[...] = lax.fori_loop(0, S // R, step, (z3, z1, z3, z1))

    half = lambda r, dt: (_strip(r, 0), jax.ShapeDtypeStruct((r, D_FF), dt))
    outs = [half(S, BF16), half(S, BF16), half(3, F32), half(3, F32), half(1, F32), half(1, F32)]
    return _pcall(
        body, name="conv_gate_bwd", grid=(N_STRIPS,),
        in_specs=[_strip(S, 0), _strip(S, 1), _strip(3, 0), _strip(3, 1), _strip(1, 0), _strip(1, 1), _strip(S, 0)],
        out_specs=[o[0] for o in outs], out_shape=[o[1] for o in outs],
        scratch_shapes=[pltpu.VMEM((S + 4 * HALO, CONV_STRIP), F32)] * 2 + [pltpu.VMEM((S + 2 * HALO, CONV_STRIP), F32),
                                                                          pltpu.VMEM((n, CONV_STRIP), F32)],
        dims=("parallel",))(u, u, conv_w, conv_w, conv_b, conv_b, da)


MESH = pl.DeviceIdType.MESH
ANY = pl.BlockSpec(memory_space=pl.ANY)


def _place():
    return lax.axis_index("x"), lax.axis_index("y"), lax.axis_index("c")


def _all_gather(shards):
    n_arr = len(shards)

    def body(*refs):
        ins, outs = refs[:n_arr], refs[n_arr:2 * n_arr]
        send_sems, recv_sems, local_sems = refs[2 * n_arr:]
        x, y, c = _place()
        me, sibling = (x, y, c), (x, y, 1 - c)
        chips = [(1 - x, y), (x, 1 - y), (1 - x, 1 - y)]

        def slot(a, p):
            return outs[a].at[4 * p[0] + 2 * p[1] + p[2]]

        def copy(a, k, block, to, src=None):
            return pltpu.make_async_remote_copy(
                src_ref=slot(a, block) if src is None else src, dst_ref=slot(a, block),
                send_sem=send_sems.at[a, k], recv_sem=recv_sems.at[a, k], device_id=to, device_id_type=MESH)

        mine = [pltpu.make_async_copy(ins[a], slot(a, me), local_sems.at[a]) for a in range(n_arr)]
        for cp in mine:
            cp.start()
        first = []
        for a in range(n_arr):
            first.append(copy(a, 0, me, sibling, src=ins[a]))
            first += [copy(a, 1 + j, me, (*chip, c), src=ins[a]) for j, chip in enumerate(chips)]
        for cp in first:
            cp.start()
        passed = []
        for j, chip in enumerate(chips):
            for a in range(n_arr):
                copy(a, 1 + j, (*chip, c), me).wait_recv()
                cp = copy(a, 4 + j, (*chip, c), sibling)
                cp.start()
                passed.append(cp)
        for a in range(n_arr):
            copy(a, 0, sibling, me).wait_recv()
            for j, chip in enumerate(chips):
                copy(a, 4 + j, (*chip, 1 - c), me).wait_recv()
        for cp in first + passed:
            cp.wait_send()
        for cp in mine:
            cp.wait()

    return pl.pallas_call(
        body, name="all_gather_weights",
        in_specs=[ANY] * n_arr, out_specs=[ANY] * n_arr,
        out_shape=[jax.ShapeDtypeStruct((N_DEV, *s.shape), s.dtype) for s in shards],
        scratch_shapes=[pltpu.SemaphoreType.DMA((n_arr, 7)), pltpu.SemaphoreType.DMA((n_arr, 7)), pltpu.SemaphoreType.DMA((n_arr,))],
    )(*shards)


def _xchg_out_shapes(stacked, replicated):
    return ([jax.ShapeDtypeStruct(s.shape, s.dtype) for s in stacked]
            + [jax.ShapeDtypeStruct((N_DEV, *r.shape), r.dtype) for r in replicated])


def _xchg_sems(n_arr):
    return [pltpu.SemaphoreType.DMA((n_arr, 7)), pltpu.SemaphoreType.DMA((n_arr, 7)), pltpu.SemaphoreType.DMA((n_arr,))]


def _xchg_copies(ins, outs, sems, n_st, with_recv):
    send_sems, recv_sems, local_sems = sems
    n_arr = len(ins)
    x, y, c = _place()
    me = 4 * x + 2 * y + c

    def src(a, idx):
        return ins[a].at[idx] if a < n_st else ins[a]

    mine = [pltpu.make_async_copy(src(a, me), outs[a].at[me], local_sems.at[a]) for a in range(n_arr)]
    pairs = []
    for k in range(1, N_DEV):
        px, py, pc = x ^ (k >> 2), y ^ ((k >> 1) & 1), c ^ (k & 1)
        peer = 4 * px + 2 * py + pc
        for a in range(n_arr):
            sems_k = dict(send_sem=send_sems.at[a, k - 1], recv_sem=recv_sems.at[a, k - 1], device_id_type=MESH)
            send = pltpu.make_async_remote_copy(src_ref=src(a, peer), dst_ref=outs[a].at[me], device_id=(px, py, pc), **sems_k)
            recv = None
            if with_recv:
                recv = pltpu.make_async_remote_copy(src_ref=src(a, peer), dst_ref=outs[a].at[peer], device_id=(x, y, c), **sems_k)
            pairs.append((send, recv))
    return mine, pairs


def _xchg_start(ins, outs, sems, n_st):
    mine, pairs = _xchg_copies(ins, outs, sems, n_st, False)
    for cp in mine:
        cp.start()
    for send, _ in pairs:
        send.start()


def _xchg_finish(ins, outs, sems, n_st):
    mine, pairs = _xchg_copies(ins, outs, sems, n_st, True)
    for _, recv in pairs:
        recv.wait_recv()
    for send, _ in pairs:
        send.wait_send()
    for cp in mine:
        cp.wait()


def _exchange(stacked, replicated, name):
    _, landed = _pcall(lambda: None, name=name, grid=(), in_specs=[], out_specs=[], out_shape=[], comm=(stacked, replicated))()
    return landed


HBM = pl.BlockSpec(memory_space=pltpu.HBM)
SEMS = pl.BlockSpec(memory_space=pltpu.SEMAPHORE)
SIDE_EFFECT = pltpu.SideEffectType.DATAFLOW_SIDE_EFFECTING


N_SPLIT_SEMS = 2 * (N_DEV - 1)


def _split_copies(src, land, sems, with_recv):
    x, y, c = _place()
    me = 4 * x + 2 * y + c
    pairs = []
    for k in range(1, N_DEV):
        px, py, pc = x ^ (k >> 2), y ^ ((k >> 1) & 1), c ^ (k & 1)
        peer = 4 * px + 2 * py + pc
        sems_k = dict(send_sem=sems[k - 1], recv_sem=sems[N_DEV - 1 + k - 1], device_id_type=MESH)
        send = pltpu.make_async_remote_copy(src_ref=src.at[peer], dst_ref=land.at[me], device_id=(px, py, pc), **sems_k)
        recv = None
        if with_recv:
            recv = pltpu.make_async_remote_copy(src_ref=src.at[peer], dst_ref=land.at[peer], device_id=(x, y, c), **sems_k)
        pairs.append((send, recv))
    return pairs


def _exchange_start(stacked, name):
    def body(src, land, *rest):
        for send, _ in _split_copies(src, land, rest[:N_SPLIT_SEMS], False):
            send.start()
        rest[-1][...] = jnp.zeros_like(rest[-1])

    shape = pltpu.HBM(stacked.shape, stacked.dtype)
    res = pl.pallas_call(
        body, name=name, in_specs=[HBM, HBM],
        out_shape=(*[pltpu.SemaphoreType.DMA(())] * N_SPLIT_SEMS, shape, shape, jax.ShapeDtypeStruct((8, 128), F32)),
        out_specs=(*[SEMS] * N_SPLIT_SEMS, HBM, HBM, pl.BlockSpec(memory_space=pltpu.VMEM)),
        input_output_aliases={0: N_SPLIT_SEMS, 1: N_SPLIT_SEMS + 1},
        compiler_params=pltpu.CompilerParams(has_side_effects=SIDE_EFFECT),
    )(pltpu.with_memory_space_constraint(stacked, pltpu.HBM),
      pltpu.with_memory_space_constraint(lax.empty(stacked.shape, stacked.dtype), pltpu.HBM))
    return res[:N_SPLIT_SEMS], res[N_SPLIT_SEMS], res[N_SPLIT_SEMS + 1], res[-1]


def _exchange_wait(sems, src, land, after, name):
    def body(src_ref, land_ref, *rest):
        for send, recv in _split_copies(src_ref, land_ref, rest[:N_SPLIT_SEMS], True):
            send.wait_send()
            recv.wait_recv()

    shape = pltpu.HBM(src.shape, src.dtype)
    return pl.pallas_call(
        body, name=name, in_specs=[HBM, HBM, *[SEMS] * N_SPLIT_SEMS, ANY],
        out_shape=(shape, shape), out_specs=(HBM, HBM), input_output_aliases={0: 0, 1: 1},
        compiler_params=pltpu.CompilerParams(has_side_effects=SIDE_EFFECT))(src, land, *sems, after)[1]


def _adamw(parts, w, m, v, name):
    _, R, C = w.shape
    tr = R if R <= 512 else max(t for t in range(16, 513, 16) if R % t == 0)
    pr = tr if parts.shape[1] == R else -(-R // 16) * 16
    assert pr == tr or tr == R

    def body(p_ref, w_ref, m_ref, v_ref, g_ref, d_ref, nm_ref, nv_ref):
        g = p_ref[0].astype(F32)[:tr]
        for s in range(1, N_DEV):
            g = g + p_ref[s].astype(F32)[:tr]
        m2 = ADAM_B1 * m_ref[0] + (1.0 - ADAM_B1) * g
        v2 = ADAM_B2 * v_ref[0] + (1.0 - ADAM_B2) * (g * g)
        m_hat = m2 / (1.0 - ADAM_B1 ** ADAM_STEP)
        v_hat = v2 / (1.0 - ADAM_B2 ** ADAM_STEP)
        g_ref[0] = g
        d_ref[0] = -ADAM_LR * (m_hat / (jnp.sqrt(v_hat) + ADAM_EPS) + ADAM_WD * w_ref[0])
        nm_ref[0] = m2
        nv_ref[0] = v2

    blk = pl.BlockSpec((1, tr, C), lambda i: (0, i, 0))
    return _pcall(body, name=name, grid=(R // tr,),
                  in_specs=[pl.BlockSpec((N_DEV, pr, C), lambda i: (0, i, 0)), blk, blk, blk],
                  out_specs=[blk] * 4, out_shape=[jax.ShapeDtypeStruct((1, R, C), F32)] * 4,
                  dims=("parallel",))(parts, w, m, v)


def _t5_bucket(rel):
    nb = NUM_BUCKETS // 2
    max_exact = nb // 2
    base = (rel > 0).astype(jnp.int32) * nb
    n = jnp.abs(rel)
    nf = jnp.maximum(n, 1).astype(jnp.float32)
    large = max_exact + (jnp.log(nf / max_exact) / math.log(MAX_DISTANCE / max_exact) * (nb - max_exact)).astype(jnp.int32)
    large = jnp.minimum(large, nb - 1)
    return base + jnp.where(n < max_exact, n, large)


def _unstack_cols(g):
    return jnp.transpose(g, (1, 0, 2)).reshape(g.shape[1], N_DEV * g.shape[2])


def _stack_cols(w, n=N_DEV):
    R = w.shape[0]
    return jnp.transpose(w.reshape(R, n, w.shape[1] // n), (1, 0, 2))


def _stack_halves(g, v):
    return jnp.concatenate([_stack_cols(g, N_DEV // 2), _stack_cols(v, N_DEV // 2)], axis=0)


def kernel(x, positions, norm1_g, w_in, q_a_norm_g, w_q_b, kv_a_norm_g, w_kv_b, rel_bias, sinks, w_out, norm2_g, w_up, conv_w, conv_b, w_down, final_norm_g, loss_target, m_norm1_g, m_w_in, m_q_a_norm_g, m_w_q_b, m_kv_a_norm_g, m_w_kv_b, m_rel_bias, m_sinks, m_w_out, m_norm2_g, m_w_up, m_conv_w, m_conv_b, m_w_down, m_final_norm_g, v_norm1_g, v_w_in, v_q_a_norm_g, v_w_q_b, v_kv_a_norm_g, v_w_kv_b, v_rel_bias, v_sinks, v_w_out, v_norm2_g, v_w_up, v_conv_w, v_conv_b, v_w_down, v_final_norm_g):
    S = x.shape[1]
    x = x[0]
    target = loss_target[0]
    TM = 256

    tr = lambda w: jnp.swapaxes(w, 1, 2)
    g_in, g_qb, g_kvb = _all_gather([tr(w_in)[0].astype(BF16), tr(w_q_b)[0].astype(BF16), w_kv_b[0].astype(BF16)])
    late_weights = [w_out[0].astype(BF16), tr(w_up)[0].astype(BF16), conv_w[0]]
    wi = g_in.reshape(W_IN_COLS, D_MODEL)
    c0, c1, c2, c3, c4, c5 = (sum(W_IN_SIZES[:i + 1]) for i in range(6))
    w_in_pt = jnp.concatenate([wi[c4:c5], wi[c5:], wi[c1:c2], wi[:c0], wi[c2:c3], wi[c3:c4],
                               wi[c0:c0 + KV_LORA], wi[c0 + KV_LORA:c1], jnp.zeros((64, D_MODEL), BF16)], axis=0)
    wq = g_qb.reshape(H_A, QK_HEAD, Q_LORA)
    w_qb_pt = jnp.concatenate([wq[:, :QK_NOPE].reshape(H_A * QK_NOPE, Q_LORA), wq[:, QK_NOPE:].reshape(H_A * QK_ROPE, Q_LORA)], axis=0)
    w_kvb = _unstack_cols(g_kvb)

    half = QK_ROPE // 2
    inv_freq = ROPE_THETA ** (-jnp.arange(half, dtype=F32) / half)
    ang = positions.astype(F32)[:, None] * inv_freq[None, :]
    cos, sin = jnp.cos(ang), jnp.sin(ang)
    qa = jnp.arange(Q_BLOCK, dtype=jnp.int32)[:, None]
    kc = jnp.arange(SPAN, dtype=jnp.int32)[None, :]
    rel = (kc - WINDOW - qa).T
    in_band = (jnp.abs(rel) <= WINDOW).astype(F32).reshape(1, Q_BLOCK * SPAN)
    onehot_t = (_t5_bucket(rel).reshape(1, Q_BLOCK * SPAN) == jnp.arange(NUM_BUCKETS, dtype=jnp.int32)[:, None]).astype(F32)
    bias_t = _bias_table(rel_bias.T, onehot_t, in_band).reshape(H_B, SPAN, Q_BLOCK)
    sinks_b = jnp.broadcast_to(sinks.reshape(H_B, 1), (H_B, Q_BLOCK))

    (h1,) = _rowwise(lambda a, g: (_rms(a, g),), "norm1", S, TM, [_rows(x), _whole(norm1_g)], [("rows", D_MODEL, BF16)])
    proj = _matmul(h1, w_in_pt, "nt", BF16, "proj")

    def lat_fn(qlat, ckv, kr, gq, gkv, cs, sn):
        r1, r2 = _rope(kr[:, :half], kr[:, half:QK_ROPE], cs, sn)
        return _rms(qlat, gq), _rms(ckv, gkv), jnp.concatenate([r1, r2], axis=1)

    qn, ckvn, k_rope = _rowwise(lat_fn, "latents", S, TM,
                                [_rows(proj, 256, PROJ_QLAT), _rows(proj, 128, PROJ_CKV), _rows(proj, 128, PROJ_KROPE),
                                 _whole(q_a_norm_g), _whole(kv_a_norm_g), _rows(cos), _rows(sin)],
                                [("rows", Q_LORA, BF16), ("rows", KV_LORA, BF16), ("rows", QK_ROPE, BF16)])
    def q_heads_fn(q, cs, sn):
        q = q * MLA_PRESCALE
        outs = []
        for h in range(H_A):
            o = H_A * QK_NOPE + QK_ROPE * h
            r1, r2 = _rope(q[:, o:o + half], q[:, o + half:o + QK_ROPE], cs, sn)
            outs.append(jnp.concatenate([q[:, QK_NOPE * h:QK_NOPE * (h + 1)], r1, r2], axis=1)[None])
        return (jnp.concatenate(outs, axis=0),)

    (q_full,) = _matmul(qn, w_qb_pt, "nt", None, "q_up_heads", tm=512, tn=1536,
                        epi=(q_heads_fn, [_rows(cos), _rows(sin)], [("heads", H_A, QK_HEAD, BF16)]))

    def k_heads_fn(kvf, kr):
        return kvf, jnp.concatenate([jnp.concatenate([kvf[:, 256 * h:256 * h + QK_NOPE], kr], axis=1)[None] for h in range(H_A)], axis=0)

    kv, k_full = _matmul(ckvn, w_kvb, "nn", None, "kv_up_heads", tm=512, tn=2048,
                         epi=(k_heads_fn, [_rows(k_rope)], [("rows", H_A * (QK_NOPE + V_DIM), BF16), ("heads", H_A, QK_HEAD, BF16)]))
    (o_a, lse), (g_out, g_up, g_cw) = _mla_fwd(q_full, k_full, kv, S, comm=([], late_weights))
    w_out_f = g_out.reshape(D_MODEL, D_MODEL)
    w_up_t = g_up.reshape(2 * D_FF, D_MODEL)
    conv_w_f = _unstack_cols(g_cw)

    o_b, (g_down,) = _win_fwd(proj, bias_t, sinks_b, S, comm=([], [w_down[0].astype(BF16)]))
    w_down_f = g_down.reshape(D_FF, D_MODEL)

    (mixed,) = _rowwise(lambda ga, gb, oa, ob: (_sigmoid(ga) * oa + _sigmoid(gb) * ob,), "gate_mix", S, TM,
                        [_rows(proj, 1024, PROJ_GA), _rows(proj, 1024, PROJ_GB), _rows(o_a), _rows(o_b)], [("rows", D_MODEL, BF16)])
    x1, h2 = _matmul(mixed, w_out_f, "nn", None, "out_proj", residual=x, tm=512,
                     epi=(lambda a, g: (a, _rms(a, g)), [_whole(norm2_g)], [("rows", D_MODEL, F32), ("rows", D_MODEL, BF16)]))
    u = _matmul(h2, w_up_t, "nt", BF16, "ffn_up", tn=1408)
    act = _conv_gate_fwd(u, conv_w_f, conv_b, S)

    def final_fn(a, g, t):
        err = _rms(a, g) - t
        loss = 0.5 * jnp.sum(jnp.mean(err * err, axis=-1, keepdims=True), axis=0, keepdims=True)
        dx, dg = _rms_bwd(err * (1.0 / D_MODEL), a, g)
        return dx, dx, dg, jnp.broadcast_to(loss, (1, 128))

    gfin = final_norm_g.reshape(1, D_MODEL)
    dx2, dx2_b, d_gfin, loss_row = _matmul(
        act, w_down_f, "nn", None, "ffn_down_loss", residual=x1, tm=512,
        epi=(final_fn, [_whole(gfin), _rows(target)],
             [("rows", D_MODEL, F32), ("rows", D_MODEL, BF16), ("acc", 1, D_MODEL), ("acc", 1, 128)]))
    d_act = _matmul(dx2_b, w_down_f, "nt", BF16, "ffn_down_dx", tn=1408)
    d_w_down = _matmul(act, dx2_b, "tn", BF16, "ffn_down_dw")
    du_g, du_v, dcw_g, dcw_v, dcb_g, dcb_v = _conv_gate_bwd(u, conv_w_f, conv_b, d_act, S)
    d_conv_b = jnp.concatenate([dcb_g, dcb_v], axis=1)

    def norm_bwd_fn(dh, a, g, dres):
        dx, dg = _rms_bwd(dh, a, g)
        dx = dx + dres
        return dx, dx, dg

    dx1, dx1_b, d_g2 = _matmul(du_g, w_up_t, "nn", None, "ffn_up_dx_norm2_bwd", tm=256, a2=du_v,
                               epi=(norm_bwd_fn, [_rows(x1), _whole(norm2_g), _rows(dx2)],
                                    [("rows", D_MODEL, F32), ("rows", D_MODEL, BF16), ("acc", 1, D_MODEL)]))
    d_w_up_t = _matmul(du_g, h2, "tn", BF16, "ffn_up_dw", tm=256, a2=du_v)
    d_w_out = _matmul(mixed, dx1_b, "tn", BF16, "out_proj_dw", tm=512)

    def gate_bwd_fn(dm, ga, gb, oa, ob):
        sa, sb = _sigmoid(ga), _sigmoid(gb)
        return jnp.concatenate([dm * oa * sa * (1.0 - sa), dm * ob * sb * (1.0 - sb)], axis=1), dm * sa, dm * sb

    d_proj, do_a, do_b = _matmul(
        dx1_b, w_out_f, "nt", None, "out_proj_dx_gate_bwd", tm=512,
        epi=(gate_bwd_fn, [_rows(proj, 1024, PROJ_GA), _rows(proj, 1024, PROJ_GB), _rows(o_a), _rows(o_b)],
             [("cols", 2 * D_MODEL, 0, PROJ_P, BF16), ("rows", D_MODEL, BF16), ("rows", D_MODEL, BF16)]))

    d_proj, dk_acc, dv_acc, d_bias, d_sinks_g = _win_bwd(proj, bias_t, sinks_b, do_b, d_proj, S)
    d_sinks = d_sinks_g.reshape(1, H_B)

    early = [d_w_out.reshape(N_DEV, D_MODEL // N_DEV, D_MODEL), d_w_up_t.reshape(N_DEV, 2 * D_FF // N_DEV, D_MODEL),
             d_w_down.reshape(N_DEV, D_FF // N_DEV, D_MODEL), _stack_halves(dcw_g, dcw_v)]
    (dq_full, dk_full, dv_full), recv_early = _mla_bwd(q_full, k_full, kv, do_a, o_a, lse, S, comm=(early, []))

    def dq_post_fn(dq, cs, sn):
        nope = [dq[h, :, :QK_NOPE] for h in range(H_A)]
        rope = []
        for h in range(H_A):
            rope += list(_rope_bwd(dq[h, :, QK_NOPE:QK_NOPE + half], dq[h, :, QK_NOPE + half:], cs, sn))
        return (jnp.concatenate(nope + rope, axis=1),)

    (dq_p,) = _rowwise(dq_post_fn, "dq_post", S, TM, [_heads(dq_full), _rows(cos), _rows(sin)], [("rows", H_A * QK_HEAD, BF16)])

    def dkv_post_fn(dk, dv, cs, sn):
        dk = dk * math.log(2.0)
        dkv = jnp.concatenate([jnp.concatenate([dk[h, :, :QK_NOPE], dv[h]], axis=1) for h in range(H_A)], axis=1)
        dkr = dk[0, :, QK_NOPE:]
        for h in range(1, H_A):
            dkr = dkr + dk[h, :, QK_NOPE:]
        r1, r2 = _rope_bwd(dkr[:, :half], dkr[:, half:], cs, sn)
        return dkv, jnp.concatenate([r1, r2], axis=1)

    dkv, d_krope = _rowwise(dkv_post_fn, "dkv_post", S, TM, [_heads(dk_full), _heads(dv_full), _rows(cos), _rows(sin)],
                            [("rows", H_A * (QK_NOPE + V_DIM), BF16), ("rows", QK_ROPE, F32)])
    d_qn = _matmul(dq_p, w_qb_pt, "nn", F32, "q_up_dx")
    d_w_qb_pt = _matmul(dq_p, qn, "tn", BF16, "q_up_dw", tm=512)
    d_ckvn = _matmul(dkv, w_kvb, "nt", F32, "kv_up_dx")
    d_w_kvb = _matmul(ckvn, dkv, "tn", BF16, "kv_up_dw", tn=2048)

    def lat_bwd_fn(dqn, dckvn, dkr, qlat, ckv, gq, gkv, dkb, dvb):
        dql, dgq = _rms_bwd(dqn, qlat, gq)
        dck, dgkv = _rms_bwd(dckvn, ckv, gkv)
        tail = jnp.concatenate([dql, dkb, dvb, dck, dkr, jnp.zeros_like(dkr)], axis=1)
        return tail, dgq, dgkv

    shifted = lambda arr: (arr, lambda tm: pl.BlockSpec((tm, arr.shape[1]), lambda i, *_: (i + WINDOW // tm, 0)))
    TL = min(128, S)
    d_proj, d_gq, d_gkv = _rowwise(lat_bwd_fn, "latents_bwd", S, TL,
                                   [_rows(d_qn), _rows(d_ckvn), _rows(d_krope), _rows(proj, 256, PROJ_QLAT), _rows(proj, 128, PROJ_CKV),
                                    _whole(q_a_norm_g), _whole(kv_a_norm_g), shifted(dk_acc), shifted(dv_acc)],
                                   [("cols", 1024, 3, PROJ_P, BF16), ("acc", 1, Q_LORA), ("acc", 1, KV_LORA)], into=(d_proj, 0))
    d_w_qb_t = jnp.concatenate([d_w_qb_pt[:H_A * QK_NOPE].reshape(H_A, QK_NOPE, Q_LORA),
                                d_w_qb_pt[H_A * QK_NOPE:].reshape(H_A, QK_ROPE, Q_LORA)], axis=1)
    dp, recv_mid = _matmul(d_proj, h1, "tn", BF16, "proj_dw", tm=512, comm=([d_w_qb_t, _stack_cols(d_w_kvb)], []))

    late = jnp.concatenate([dp[3072:3328], dp[3840:3968], dp[3968:4032], dp[2048:3072], dp[3328:3584],
                            dp[3584:3840], dp[0:1024], dp[1024:2048]], axis=0).reshape(N_DEV, W_IN_COLS // N_DEV, D_MODEL)
    late_sems, late_src, late_land, started = _exchange_start(late, "late_grads_start")

    def norm1_bwd_fn(dh, a, g, dres):
        dx, dg = _rms_bwd(dh, a, g)
        return dx + dres, dg

    grad_x, d_g1 = _matmul(
        d_proj, w_in_pt, "nn", None, "proj_dx_norm1_bwd", tm=512,
        epi=(norm1_bwd_fn, [_rows(x), _whole(norm1_g + started[:1, :1]), _rows(dx1)], [("rows", D_MODEL, F32), ("acc", 1, D_MODEL)]))

    transposed = ("w_in", "w_q_b", "w_up")
    ready_names = ["w_out", "w_up", "w_down", "conv_w", "w_q_b", "w_kv_b"]
    ready_wmv = [(w_out, m_w_out, v_w_out), (tr(w_up), tr(m_w_up), tr(v_w_up)), (w_down, m_w_down, v_w_down), (conv_w, m_conv_w, v_conv_w),
                 (tr(w_q_b), tr(m_w_q_b), tr(v_w_q_b)), (w_kv_b, m_w_kv_b, v_w_kv_b)]
    big = {n: _adamw(r, *wmv, "adamw_" + n) for n, r, wmv in zip(ready_names, [*recv_early, *recv_mid], ready_wmv)}

    d_bias, _ = lax.optimization_barrier((d_bias, started))
    d_rel_bias = _bias_table_bwd(d_bias.reshape(H_B, Q_BLOCK * SPAN), onehot_t).T

    after = lax.optimization_barrier([big[n][0] for n in ready_names] + [d_rel_bias])
    landed = _exchange_wait(late_sems, late_src, late_land, after[-1], "late_grads_wait")
    me = 4 * lax.axis_index("x") + 2 * lax.axis_index("y") + lax.axis_index("c")
    landed = lax.dynamic_update_slice_in_dim(landed, lax.dynamic_slice_in_dim(late, me, 1, axis=0), me, axis=0)
    big["w_in"] = _adamw(landed, tr(w_in), tr(m_w_in), tr(v_w_in), "adamw_w_in")

    small_parts = [d_g1, d_gq, d_gkv, d_rel_bias.reshape(1, NUM_BUCKETS * H_B), d_sinks, d_g2, d_conv_b, d_gfin, loss_row[:, :1]]
    small = jnp.concatenate(small_parts, axis=1)
    n_small = small.shape[1]
    pad = (-n_small) % 128
    small = jnp.pad(small, ((0, 0), (0, pad)))
    small, _ = lax.optimization_barrier((small, [landed, *after]))
    (recv_small,) = _exchange([], [small], "exchange_small_grads")

    def flat(a):
        return a.reshape(1, -1)

    small_w = [norm1_g, q_a_norm_g, kv_a_norm_g, rel_bias, sinks, norm2_g, conv_b, final_norm_g]
    small_m = [m_norm1_g, m_q_a_norm_g, m_kv_a_norm_g, m_rel_bias, m_sinks, m_norm2_g, m_conv_b, m_final_norm_g]
    small_v = [v_norm1_g, v_q_a_norm_g, v_kv_a_norm_g, v_rel_bias, v_sinks, v_norm2_g, v_conv_b, v_final_norm_g]
    cat = lambda parts: jnp.pad(jnp.concatenate([flat(a) for a in parts], axis=1), ((0, 0), (0, pad + 1)))[None]
    sm = _adamw(recv_small, cat(small_w), cat(small_m), cat(small_v), "adamw_small")

    loss = sm[0][0, 0, n_small - 1]
    order =["norm1_g", "w_in", "q_a_norm_g", "w_q_b", "kv_a_norm_g", "w_kv_b", "rel_bias", "sinks", "w_out", "norm2_g", "w_up",
             "conv_w", "conv_b", "w_down", "final_norm_g"]
    small_names = ["norm1_g", "q_a_norm_g", "kv_a_norm_g", "rel_bias", "sinks", "norm2_g", "conv_b", "final_norm_g"]
    offs, o = {}, 0
    for n, a in zip(small_names, small_w):
        offs[n] = (o, a.size, a.shape)
        o += a.size
    outs = [loss, grad_x[None]]
    for kind in range(4):
        for n in order:
            if n in big:
                outs.append(tr(big[n][kind]) if n in transposed else big[n][kind])
            else:
                o, size, shape = offs[n]
                outs.append(sm[kind][0, 0, o:o + size].reshape(shape))
    return tuple(outs)
```

```python
import math

import jax
import jax.numpy as jnp
from jax import lax
from jax.experimental import pallas as pl
from jax.experimental.pallas import tpu as pltpu

F32 = jnp.float32
BF16 = jnp.bfloat16

N_DEV = 8
D_MODEL = 1024
EPS = 1e-6
H_A, QK_NOPE, QK_ROPE, V_DIM, Q_LORA, KV_LORA = 8, 128, 64, 128, 256, 128
QK_HEAD = QK_NOPE + QK_ROPE
ROPE_THETA = 10000.0
H_B, KV_B, GROUP, HD_B, WINDOW, Q_BLOCK = 16, 4, 4, 64, 128, 128
SPAN = Q_BLOCK + 2 * WINDOW
NUM_BUCKETS, MAX_DISTANCE = 32, 128
D_FF = 2816
ADAM_LR, ADAM_B1, ADAM_B2, ADAM_EPS, ADAM_WD, ADAM_STEP = 0.001, 0.9, 0.999, 1e-08, 0.01, 10

W_IN_SIZES = (Q_LORA, KV_LORA + QK_ROPE, H_B * HD_B, KV_B * HD_B, KV_B * HD_B, D_MODEL, D_MODEL)
W_IN_COLS = sum(W_IN_SIZES)
PROJ_P = 4096
PROJ_GA, PROJ_GB, PROJ_QB, PROJ_QLAT, PROJ_KB, PROJ_VB, PROJ_CKV, PROJ_KROPE = 0, 1, 2, 12, 13, 14, 30, 31

VMEM_LIMIT = 56 * 1024 * 1024

NN = (((1,), (0,)), ((), ()))
NT = (((1,), (1,)), ((), ()))
TN = (((0,), (0,)), ((), ()))


def _pcall(body, *, name, grid, in_specs, out_specs, out_shape, scratch_shapes=(), dims=None, comm=None, aliases=None):
    if comm is None:
        params = pltpu.CompilerParams(dimension_semantics=dims, vmem_limit_bytes=VMEM_LIMIT)
        return pl.pallas_call(body, name=name, grid=grid, in_specs=in_specs, out_specs=out_specs, out_shape=out_shape,
                              scratch_shapes=list(scratch_shapes), input_output_aliases=aliases or {}, compiler_params=params)
    assert not aliases
    stacked, replicated = comm
    arrs = [*stacked, *replicated]
    n_st, n_arr = len(stacked), len(arrs)
    single = not isinstance(out_specs, (list, tuple))
    o_specs, o_shape = ([out_specs], [out_shape]) if single else (list(out_specs), list(out_shape))
    n_in, n_out = len(in_specs), len(o_specs)

    def wrapped(*refs):
        c_in = refs[n_in:n_in + n_arr]
        c_out = refs[n_in + n_arr + n_out:n_in + 2 * n_arr + n_out]
        sems = refs[len(refs) - 3:]
        own = (*refs[:n_in], *refs[n_in + n_arr:n_in + n_arr + n_out], *refs[n_in + 2 * n_arr + n_out:len(refs) - 3])
        if not grid:
            _xchg_start(c_in, c_out, sems, n_st)
            _xchg_finish(c_in, c_out, sems, n_st)
            return
        first = last = None
        for d, n in enumerate(grid):
            pid = pl.program_id(d)
            first = (pid == 0) if first is None else first & (pid == 0)
            last = (pid == n - 1) if last is None else last & (pid == n - 1)

        @pl.when(first)
        def _():
            _xchg_start(c_in, c_out, sems, n_st)

        body(*own)

        @pl.when(last)
        def _():
            _xchg_finish(c_in, c_out, sems, n_st)

    params = pltpu.CompilerParams(dimension_semantics=("arbitrary",) * len(grid), vmem_limit_bytes=VMEM_LIMIT)
    call = pl.pallas_call(wrapped, name=name, grid=grid, in_specs=[*in_specs, *[ANY] * n_arr], out_specs=[*o_specs, *[ANY] * n_arr],
                          out_shape=[*o_shape, *_xchg_out_shapes(stacked, replicated)],
                          scratch_shapes=[*scratch_shapes, *_xchg_sems(n_arr)], compiler_params=params)

    def run(*args):
        res = call(*args, *arrs)
        outs, landed = res[:n_out], res[n_out:]
        return (outs[0] if single else outs), landed

    return run


def _dot(a, b, dn):
    return lax.dot_general(a, b, dn, preferred_element_type=F32)


def _tile(n, target):
    best = None
    for t in range(128, min(n, target) + 1, 128):
        if n % t == 0:
            best = t
    return n if best is None else best


def _matmul(a, b, mode, out_dtype, name, residual=None, tm=1024, tn=1024, comm=None, a2=None, epi=None):
    if mode == "nn":
        (M, K), N = a.shape, b.shape[1]
    elif mode == "nt":
        (M, K), N = a.shape, b.shape[0]
    else:
        (K, M), N = a.shape, b.shape[1]
    tm, tn = _tile(M, tm), _tile(N, tn)
    a_spec = pl.BlockSpec((K, tm), lambda i, j: (0, i)) if mode == "tn" else pl.BlockSpec((tm, K), lambda i, j: (i, 0))
    b_spec = pl.BlockSpec((tn, b.shape[1]), lambda i, j: (j, 0)) if mode == "nt" else pl.BlockSpec((K, tn), lambda i, j: (0, j))
    o_spec = pl.BlockSpec((tm, tn), lambda i, j: (i, j))
    in_specs, args = [a_spec, b_spec], [a, b]
    n1 = M // tm
    if a2 is not None and mode == "tn":
        assert M % tm == 0 and a2.shape[1] % tm == 0
        in_specs[0] = pl.BlockSpec((K, tm), lambda i, j: (0, jnp.minimum(i, n1 - 1)))
        in_specs.append(pl.BlockSpec((K, tm), lambda i, j: (0, jnp.maximum(i - n1, 0))))
        args.append(a2)
        M += a2.shape[1]
    elif a2 is not None:
        assert (mode == "nt" and K + a2.shape[1] == b.shape[1]) or (mode == "nn" and K + a2.shape[1] == b.shape[0])
        if mode == "nn":
            b_spec = in_specs[1] = pl.BlockSpec((b.shape[0], tn), lambda i, j: (0, j))
        in_specs.append(pl.BlockSpec((tm, a2.shape[1]), lambda i, j: (i, 0)))
        args.append(a2)
    if residual is not None:
        in_specs.append(o_spec)
        args.append(residual)
    n_mm = len(args)
    scratch = [pltpu.VMEM((tm, K), a.dtype)] if mode == "tn" else []
    if epi is None:
        out_specs, out_shape, is_acc = o_spec, jax.ShapeDtypeStruct((M, N), out_dtype), None
    else:
        assert tn == N
        fn, epi_ins, epi_outs = epi
        in_specs += [mk(tm) for _, mk in epi_ins]
        args += [arr for arr, _ in epi_ins]
        out_specs, out_shape, is_acc = _row_out_specs(epi_outs, M, tm)

    def body(*refs):
        a_ref, b_ref = refs[0], refs[1]
        n_out = 1 if epi is None else len(is_acc)
        out_refs = refs[len(args):len(args) + n_out]
        if mode == "tn":
            at_ref = refs[len(args) + n_out]

            first_col = pl.program_id(1) == 0
            from_a = first_col if a2 is None else first_col & (pl.program_id(0) < n1)

            @pl.when(from_a)
            def _():
                at_ref[...] = a_ref[...].T

            if a2 is not None:
                @pl.when(first_col & (pl.program_id(0) >= n1))
                def _():
                    at_ref[...] = refs[2][...].T

            acc = _dot(at_ref[...], b_ref[...], NN)
        elif a2 is not None and mode == "nt":
            acc = _dot(a_ref[...], b_ref[:, :K], NT) + _dot(refs[2][...], b_ref[:, K:], NT)
        elif a2 is not None:
            acc = _dot(a_ref[...], b_ref[:K, :], NN) + _dot(refs[2][...], b_ref[K:, :], NN)
        else:
            acc = _dot(a_ref[...], b_ref[...], NT if mode == "nt" else NN)
        if residual is not None:
            acc = acc + refs[n_mm - 1][...]
        if epi is None:
            out_refs[0][...] = acc.astype(out_dtype)
        else:
            _store_rows(out_refs, fn(acc, *[_load_f32(r) for r in refs[n_mm:len(args)]]), is_acc)

    return _pcall(body, name=name, grid=(M // tm, N // tn), in_specs=in_specs, out_specs=out_specs,
                  out_shape=out_shape, scratch_shapes=scratch,
                  dims=("arbitrary" if epi is not None else "parallel", "arbitrary"), comm=comm)(*args)


def _rows(arr, width=None, col=0):
    width = arr.shape[1] if width is None else width
    return (arr, lambda tm: pl.BlockSpec((tm, width), lambda i, *_: (i, col)))


def _heads(arr):
    return (arr, lambda tm: pl.BlockSpec((arr.shape[0], tm, arr.shape[2]), lambda i, *_: (0, i, 0)))


def _whole(arr):
    nd = arr.ndim
    return (arr, lambda tm: pl.BlockSpec(arr.shape, lambda i, *_: (0,) * nd))


def _row_out_specs(outs, n_rows, tm):
    out_specs, out_shape, is_acc = [], [], []
    for o in outs:
        if o[0] == "rows":
            out_specs.append(pl.BlockSpec((tm, o[1]), lambda i, *_: (i, 0)))
            out_shape.append(jax.ShapeDtypeStruct((n_rows, o[1]), o[2]))
        elif o[0] == "cols":
            out_specs.append(pl.BlockSpec((tm, o[1]), lambda i, *_, c=o[2]: (i, c)))
            out_shape.append(jax.ShapeDtypeStruct((n_rows, o[3]), o[4]))
        elif o[0] == "heads":
            out_specs.append(pl.BlockSpec((o[1], tm, o[2]), lambda i, *_: (0, i, 0)))
            out_shape.append(jax.ShapeDtypeStruct((o[1], n_rows, o[2]), o[3]))
        else:
            out_specs.append(pl.BlockSpec((o[1], o[2]), lambda i, *_: (0, 0)))
            out_shape.append(jax.ShapeDtypeStruct((o[1], o[2]), F32))
        is_acc.append(o[0] == "acc")
    return out_specs, out_shape, is_acc


def _load_f32(r):
    v = r[...]
    return v.astype(F32) if v.dtype == BF16 else v


def _store_rows(out_refs, vals, is_acc):
    for r, v, acc in zip(out_refs, vals, is_acc):
        if acc:
            @pl.when(pl.program_id(0) == 0)
            def _():
                r[...] = jnp.zeros_like(r)

            r[...] += v
        else:
            r[...] = v.astype(r.dtype)


def _rowwise(fn, name, n_rows, tm, ins, outs, upcast=True, into=None):
    tm = min(tm, n_rows)
    assert n_rows % tm == 0
    in_specs = [mk(tm) for _, mk in ins]
    out_specs, out_shape, is_acc = _row_out_specs(outs, n_rows, tm)
    n_in = len(ins)
    args = [a for a, _ in ins]
    aliases = {}
    if into is not None:
        in_specs.append(ANY)
        args.append(into[0])
        aliases = {n_in: into[1]}

    def body(*refs):
        vals = fn(*[_load_f32(r) if upcast else r[...] for r in refs[:n_in]])
        _store_rows(refs[len(args):], vals, is_acc)

    return _pcall(body, name=name, grid=(n_rows // tm,), in_specs=in_specs, out_specs=out_specs,
                  out_shape=out_shape, dims=("arbitrary",), aliases=aliases)(*args)


def _rms(x, g):
    r = lax.rsqrt(jnp.mean(x * x, axis=-1, keepdims=True) + EPS)
    return x * r * g


def _rms_bwd(dy, x, g):
    r = lax.rsqrt(jnp.mean(x * x, axis=-1, keepdims=True) + EPS)
    xhat = x * r
    dxhat = dy * g
    dx = r * (dxhat - xhat * jnp.mean(dxhat * xhat, axis=-1, keepdims=True))
    return dx, jnp.sum(dy * xhat, axis=0, keepdims=True)


def _rope(x1, x2, cos, sin):
    return x1 * cos - x2 * sin, x2 * cos + x1 * sin


def _rope_bwd(d1, d2, cos, sin):
    return d1 * cos + d2 * sin, d2 * cos - d1 * sin


def _sigmoid(x):
    return 1.0 / (1.0 + jnp.exp(-x))


MLA_SCALE = 1.0 / math.sqrt(QK_HEAD)
MLA_PRESCALE = MLA_SCALE * math.log2(math.e)
MLA_TQ, MLA_KC = 1024, 1024


def _mla_fwd(q_full, k_full, kv, S, comm=None):
    tq, kc = min(MLA_TQ, S), min(MLA_KC, S)

    def body(q_ref, k_ref, v_ref, o_ref, lse_ref):
        q = q_ref[0]
        m = jnp.full((tq, 1), -1e30, F32)
        l = jnp.zeros((tq, 1), F32)
        acc = jnp.zeros((tq, V_DIM), F32)
        for c in range(S // kc):
            s = _dot(q, k_ref[0, c * kc:(c + 1) * kc, :], NT)
            m_new = jnp.maximum(m, jnp.max(s, axis=-1, keepdims=True))
            alpha = jnp.exp2(m - m_new)
            p = jnp.exp2(s - m_new)
            l = alpha * l + jnp.sum(p, axis=-1, keepdims=True)
            acc = alpha * acc + _dot(p.astype(BF16), v_ref[c * kc:(c + 1) * kc, :], NN)
            m = m_new
        o_ref[...] = (acc / l).astype(BF16)
        lse_ref[0] = m + jnp.log2(l)

    return _pcall(
        body, name="mla_fwd", grid=(H_A, S // tq),
        in_specs=[pl.BlockSpec((1, tq, QK_HEAD), lambda h, i: (h, i, 0)),
                  pl.BlockSpec((1, S, QK_HEAD), lambda h, i: (h, 0, 0)),
                  pl.BlockSpec((S, V_DIM), lambda h, i: (0, 2 * h + 1))],
        out_specs=[pl.BlockSpec((tq, V_DIM), lambda h, i: (i, h)),
                   pl.BlockSpec((1, tq, 1), lambda h, i: (h, i, 0))],
        out_shape=[jax.ShapeDtypeStruct((S, H_A * V_DIM), BF16), jax.ShapeDtypeStruct((H_A, S, 1), F32)],
        dims=("parallel", "parallel"), comm=comm)(q_full, k_full, kv)


def _mla_bwd(q_full, k_full, kv, do_a, o_a, lse, S, comm=None):
    tq, kc = min(MLA_TQ, S), min(MLA_KC, S)

    def body(q_ref, k_ref, v_ref, do_ref, o_ref, lse_ref, dq_ref, dk_ref, dv_ref):
        @pl.when(pl.program_id(1) == 0)
        def _():
            dk_ref[...] = jnp.zeros_like(dk_ref)
            dv_ref[...] = jnp.zeros_like(dv_ref)

        q = q_ref[0]
        do = do_ref[...]
        lse_q = lse_ref[0]
        delta = jnp.sum(do.astype(F32) * o_ref[...].astype(F32), axis=-1, keepdims=True)
        dq = jnp.zeros((tq, QK_HEAD), F32)
        for c in range(S // kc):
            k = k_ref[0, c * kc:(c + 1) * kc, :]
            v = v_ref[c * kc:(c + 1) * kc, :]
            p = jnp.exp2(_dot(q, k, NT) - lse_q)
            ds = (p * (_dot(do, v, NT) - delta)).astype(BF16)
            dq = dq + _dot(ds, k, NN)
            dk_ref[0, c * kc:(c + 1) * kc, :] += _dot(ds, q, TN)
            dv_ref[0, c * kc:(c + 1) * kc, :] += _dot(p.astype(BF16), do, TN)
        dq_ref[0] = dq * MLA_SCALE

    return _pcall(
        body, name="mla_bwd", grid=(H_A, S // tq),
        in_specs=[pl.BlockSpec((1, tq, QK_HEAD), lambda h, i: (h, i, 0)),
                  pl.BlockSpec((1, S, QK_HEAD), lambda h, i: (h, 0, 0)),
                  pl.BlockSpec((S, V_DIM), lambda h, i: (0, 2 * h + 1)),
                  pl.BlockSpec((tq, V_DIM), lambda h, i: (i, h)),
                  pl.BlockSpec((tq, V_DIM), lambda h, i: (i, h)),
                  pl.BlockSpec((1, tq, 1), lambda h, i: (h, i, 0))],
        out_specs=[pl.BlockSpec((1, tq, QK_HEAD), lambda h, i: (h, i, 0)),
                   pl.BlockSpec((1, S, QK_HEAD), lambda h, i: (h, 0, 0)),
                   pl.BlockSpec((1, S, V_DIM), lambda h, i: (h, 0, 0))],
        out_shape=[jax.ShapeDtypeStruct((H_A, S, QK_HEAD), F32), jax.ShapeDtypeStruct((H_A, S, QK_HEAD), F32),
                   jax.ShapeDtypeStruct((H_A, S, V_DIM), F32)],
        dims=("parallel", "arbitrary"), comm=comm)(q_full, k_full, kv, do_a, o_a, lse)


WIN_SCALE = 1.0 / math.sqrt(HD_B)


WIN_PER_STEP = 4


def _win_specs(S):
    last, B = S // Q_BLOCK - 1, WIN_PER_STEP
    qspec = pl.BlockSpec((B * Q_BLOCK, H_B * HD_B), lambda i: (i, PROJ_QB))
    kspecs = [[pl.BlockSpec((Q_BLOCK, KV_B * HD_B), lambda i, d=d, c=c: (jnp.clip(B * i + d, 0, last), c)) for d in range(-1, B + 1)]
              for c in (PROJ_KB, PROJ_VB)]
    bias_spec = pl.BlockSpec((H_B, SPAN, Q_BLOCK), lambda i: (0, 0, 0))
    sink_spec = pl.BlockSpec((H_B, Q_BLOCK), lambda i: (0, 0))
    return qspec, kspecs, bias_spec, sink_spec


def _win_edge_ok(n, n_blk):
    row = lax.broadcasted_iota(jnp.int32, (SPAN, 1), 0)
    return jnp.logical_not(((n == 0) & (row < WINDOW)) | ((n == n_blk - 1) & (row >= SPAN - WINDOW)))


def _lanes4(pieces):
    return jnp.concatenate(pieces, axis=1)


def _win_probs(kg, q4t, bias_ref, sink_ref, g, edge_ok):
    bias4 = _lanes4([bias_ref[GROUP * g + j] for j in range(GROUP)])
    sink4 = _lanes4([sink_ref[GROUP * g + j:GROUP * g + j + 1, :] for j in range(GROUP)])
    s = jnp.where(edge_ok, _dot(kg, q4t, NN) + bias4, -1e30)
    m = jnp.maximum(jnp.max(s, axis=0, keepdims=True), sink4)
    p = jnp.exp(s - m)
    e_sink = jnp.exp(sink4 - m)
    inv_l = 1.0 / (jnp.sum(p, axis=0, keepdims=True) + e_sink)
    return p * inv_l, e_sink * inv_l


def _group_t(xt, g):
    return _lanes4([xt[HD_B * (GROUP * g + j):HD_B * (GROUP * g + j + 1), :] for j in range(GROUP)])


def _rows_of(ref, b):
    return ref[Q_BLOCK * b:Q_BLOCK * (b + 1), :]


def _win_fwd(proj, bias_t, sinks_b, S, comm=None):
    n_blk, B = S // Q_BLOCK, WIN_PER_STEP
    qspec, kspecs, bias_spec, sink_spec = _win_specs(S)

    def body(q_ref, *refs):
        k_refs, v_refs, (bias_ref, sink_ref, o_ref) = refs[:B + 2], refs[B + 2:2 * B + 4], refs[2 * B + 4:]
        for b in range(B):
            edge_ok = _win_edge_ok(B * pl.program_id(0) + b, n_blk)
            k = jnp.concatenate([r[...] for r in k_refs[b:b + 3]], axis=0)
            vt = jnp.concatenate([r[...] for r in v_refs[b:b + 3]], axis=0).T
            qt = (_rows_of(q_ref, b).astype(F32) * WIN_SCALE).T.astype(BF16)
            parts = []
            for g in range(KV_B):
                p, _ = _win_probs(k[:, HD_B * g:HD_B * (g + 1)], _group_t(qt, g), bias_ref, sink_ref, g, edge_ok)
                o4t = _dot(vt[HD_B * g:HD_B * (g + 1), :], p.astype(BF16), NN)
                parts += [o4t[:, Q_BLOCK * j:Q_BLOCK * (j + 1)] for j in range(GROUP)]
            o_ref[Q_BLOCK * b:Q_BLOCK * (b + 1), :] = jnp.concatenate(parts, axis=0).T.astype(BF16)

    return _pcall(body, name="win_fwd", grid=(n_blk // B,),
                  in_specs=[qspec, *kspecs[0], *kspecs[1], bias_spec, sink_spec],
                  out_specs=pl.BlockSpec((B * Q_BLOCK, H_B * HD_B), lambda i: (i, 0)),
                  out_shape=jax.ShapeDtypeStruct((S, H_B * HD_B), BF16),
                  dims=("parallel",), comm=comm)(*[proj] * (2 * B + 5), bias_t, sinks_b)


def _win_bwd(proj, bias_t, sinks_b, do_b, d_proj, S):
    n_blk, B = S // Q_BLOCK, WIN_PER_STEP
    qspec, kspecs, bias_spec, sink_spec = _win_specs(S)

    def body(q_ref, *refs):
        k_refs, v_refs = refs[:B + 2], refs[B + 2:2 * B + 4]
        bias_ref, sink_ref, do_ref, _, dq_ref, dk_ref, dv_ref, dbias_ref, dsink_ref, dsink_acc = refs[2 * B + 4:]
        i = pl.program_id(0)

        @pl.when(i == 0)
        def _():
            dk_ref[...] = jnp.zeros_like(dk_ref)
            dv_ref[...] = jnp.zeros_like(dv_ref)
            dbias_ref[...] = jnp.zeros_like(dbias_ref)
            dsink_acc[...] = jnp.zeros_like(dsink_acc)

        d_bias, d_sink, dk_blocks, dv_blocks = {}, {}, [], []
        for b in range(B):
            edge_ok = _win_edge_ok(B * i + b, n_blk)
            k = jnp.concatenate([r[...] for r in k_refs[b:b + 3]], axis=0)
            v = jnp.concatenate([r[...] for r in v_refs[b:b + 3]], axis=0)
            kt = k.T
            qt = (_rows_of(q_ref, b).astype(F32) * WIN_SCALE).T.astype(BF16)
            dot_ = _rows_of(do_ref, b).astype(F32).T.astype(BF16)
            dq_parts, dks, dvs = [], [], []
            for g in range(KV_B):
                kg, vg = k[:, HD_B * g:HD_B * (g + 1)], v[:, HD_B * g:HD_B * (g + 1)]
                q4t, do4t = _group_t(qt, g), _group_t(dot_, g)
                p, p_sink = _win_probs(kg, q4t, bias_ref, sink_ref, g, edge_ok)
                dp = _dot(vg, do4t, NN)
                delta = jnp.sum(p * dp, axis=0, keepdims=True)
                ds = p * (dp - delta)
                d_bias[g] = ds if b == 0 else d_bias[g] + ds
                d_sink[g] = -p_sink * delta if b == 0 else d_sink[g] - p_sink * delta
                dsb = ds.astype(BF16)
                dq4t = _dot(kt[HD_B * g:HD_B * (g + 1), :], dsb, NN) * WIN_SCALE
                dq_parts += [dq4t[:, Q_BLOCK * j:Q_BLOCK * (j + 1)] for j in range(GROUP)]
                dks.append(_dot(dsb, q4t, NT))
                dvs.append(_dot(p.astype(BF16), do4t, NT))
            dq_ref[Q_BLOCK * b:Q_BLOCK * (b + 1), :] = jnp.concatenate(dq_parts, axis=0).T.astype(BF16)
            dk_blocks.append(jnp.concatenate(dks, axis=1))
            dv_blocks.append(jnp.concatenate(dvs, axis=1))

        for g in range(KV_B):
            for j in range(GROUP):
                dbias_ref[GROUP * g + j] += d_bias[g][:, Q_BLOCK * j:Q_BLOCK * (j + 1)]
            dsink_acc[g:g + 1, :] += d_sink[g]

        def overlap(blocks):
            out = blocks[0]
            for blk in blocks[1:]:
                keep = out.shape[0] - 2 * Q_BLOCK
                out = jnp.concatenate([out[:keep], out[keep:] + blk[:2 * Q_BLOCK], blk[2 * Q_BLOCK:]], axis=0)
            return out

        rows = pl.ds(pl.multiple_of(i * (B * Q_BLOCK), B * Q_BLOCK), (B + 2) * Q_BLOCK)
        dk_ref[rows, :] += overlap(dk_blocks)
        dv_ref[rows, :] += overlap(dv_blocks)

        @pl.when(i == n_blk // B - 1)
        def _():
            acc = dsink_acc[...]
            dsink_ref[...] = jnp.concatenate(
                [jnp.sum(acc[:, Q_BLOCK * j:Q_BLOCK * (j + 1)], axis=1, keepdims=True) for j in range(GROUP)], axis=1)

    whole = lambda shape: pl.BlockSpec(shape, lambda i: (0,) * len(shape))
    return _pcall(
        body, name="win_bwd", grid=(n_blk // B,),
        in_specs=[qspec, *kspecs[0], *kspecs[1], bias_spec, sink_spec, pl.BlockSpec((B * Q_BLOCK, H_B * HD_B), lambda i: (i, 0)), ANY],
        out_specs=[qspec, whole((S + 2 * WINDOW, KV_B * HD_B)),
                   whole((S + 2 * WINDOW, KV_B * HD_B)), whole((H_B, SPAN, Q_BLOCK)), whole((KV_B, GROUP))],
        out_shape=[jax.ShapeDtypeStruct((S, PROJ_P), BF16), jax.ShapeDtypeStruct((S + 2 * WINDOW, KV_B * HD_B), F32),
                   jax.ShapeDtypeStruct((S + 2 * WINDOW, KV_B * HD_B), F32), jax.ShapeDtypeStruct((H_B, SPAN, Q_BLOCK), F32),
                   jax.ShapeDtypeStruct((KV_B, GROUP), F32)],
        scratch_shapes=[pltpu.VMEM((KV_B, GROUP * Q_BLOCK), F32)],
        dims=("arbitrary",), aliases={2 * B + 8: 0})(*[proj] * (2 * B + 5), bias_t, sinks_b, do_b, d_proj)


def _bias_table(rel_bias_t, onehot_t, in_band):
    def body(rb_ref, oh_ref, band_ref, o_ref):
        t = lax.dot_general(rb_ref[...], oh_ref[...], NN, preferred_element_type=F32, precision=lax.Precision.HIGHEST)
        o_ref[...] = jnp.where(band_ref[...] > 0.5, t, -1e30)

    n = onehot_t.shape[1]
    tn = _tile(n, 8192)
    return _pcall(body, name="bias_table", grid=(n // tn,),
                  in_specs=[pl.BlockSpec((H_B, NUM_BUCKETS), lambda j: (0, 0)), pl.BlockSpec((NUM_BUCKETS, tn), lambda j: (0, j)),
                            pl.BlockSpec((1, tn), lambda j: (0, j))],
                  out_specs=pl.BlockSpec((H_B, tn), lambda j: (0, j)),
                  out_shape=jax.ShapeDtypeStruct((H_B, n), F32), dims=("parallel",))(rel_bias_t, onehot_t, in_band)


def _bias_table_bwd(dbias, onehot_t):
    n = onehot_t.shape[1]
    tk = _tile(n, 8192)

    def body(d_ref, oh_ref, o_ref):
        @pl.when(pl.program_id(0) == 0)
        def _():
            o_ref[...] = jnp.zeros_like(o_ref)

        o_ref[...] += lax.dot_general(d_ref[...], oh_ref[...], NT, preferred_element_type=F32, precision=lax.Precision.HIGHEST)

    return _pcall(body, name="bias_table_bwd", grid=(n // tk,),
                  in_specs=[pl.BlockSpec((H_B, tk), lambda j: (0, j)), pl.BlockSpec((NUM_BUCKETS, tk), lambda j: (0, j))],
                  out_specs=pl.BlockSpec((H_B, NUM_BUCKETS), lambda j: (0, 0)),
                  out_shape=jax.ShapeDtypeStruct((H_B, NUM_BUCKETS), F32), dims=("arbitrary",))(dbias, onehot_t)


CONV_STRIP = 128
N_STRIPS = D_FF // CONV_STRIP
CONV_ROWS = 128
HALO = 8


def _strip(rows, half):
    return pl.BlockSpec((rows, CONV_STRIP), lambda j: (0, j + half * N_STRIPS))


def _fill_padded(pad_ref, src_ref, halo, S):
    pad_ref[0:halo, :] = jnp.zeros((halo, CONV_STRIP), F32)
    pad_ref[halo + S:2 * halo + S, :] = jnp.zeros((halo, CONV_STRIP), F32)
    pad_ref[halo:halo + S, :] = src_ref[...].astype(F32)


def _conv_gate_fwd(u, conv_w, conv_b, S):
    R = min(CONV_ROWS, S)

    def body(ug_ref, uv_ref, wg_ref, wv_ref, bg_ref, bv_ref, a_ref, gpad, vpad):
        _fill_padded(gpad, ug_ref, HALO, S)
        _fill_padded(vpad, uv_ref, HALO, S)
        wg, wv, bg, bv = wg_ref[...], wv_ref[...], bg_ref[...], bv_ref[...]

        def conv(pad_ref, r0, w, b):
            dn, mid, up = (pad_ref[pl.ds(r0 + HALO + d, R), :] for d in (-1, 0, 1))
            return dn * w[0:1, :] + mid * w[1:2, :] + up * w[2:3, :] + b

        def step(c, carry):
            r0 = pl.multiple_of(c * R, R)
            g = conv(gpad, r0, wg, bg)
            val = conv(vpad, r0, wv, bv)
            a_ref[pl.ds(r0, R), :] = (g * _sigmoid(g) * val).astype(BF16)
            return carry

        lax.fori_loop(0, S // R, step, 0)

    return _pcall(body, name="conv_gate_fwd", grid=(N_STRIPS,),
                  in_specs=[_strip(S, 0), _strip(S, 1), _strip(3, 0), _strip(3, 1), _strip(1, 0), _strip(1, 1)],
                  out_specs=_strip(S, 0), out_shape=jax.ShapeDtypeStruct((S, D_FF), BF16),
                  scratch_shapes=[pltpu.VMEM((S + 2 * HALO, CONV_STRIP), F32)] * 2,
                  dims=("parallel",))(u, u, conv_w, conv_w, conv_b, conv_b)


def _conv_gate_bwd(u, conv_w, conv_b, da, S):
    R = min(CONV_ROWS, S)
    n = R + 2 * HALO

    def body(ug_ref, uv_ref, wg_ref, wv_ref, bg_ref, bv_ref, da_ref, dug_ref, duv_ref, dwg_ref, dwv_ref, dbg_ref, dbv_ref,
             gpad, vpad, dapad):
        _fill_padded(gpad, ug_ref, 2 * HALO, S)
        _fill_padded(vpad, uv_ref, 2 * HALO, S)
        _fill_padded(dapad, da_ref, HALO, S)
        wg, wv, bg, bv = wg_ref[...], wv_ref[...], bg_ref[...], bv_ref[...]

        def conv(pad_ref, r0, w, b):
            dn, mid, up = (pad_ref[pl.ds(r0 + HALO + d, n), :] for d in (-1, 0, 1))
            return dn * w[0:1, :] + mid * w[1:2, :] + up * w[2:3, :] + b, mid[HALO:HALO + R]

        def conv_bwd(duc, u_mid, w, r0, du_ref):
            dn, mid, up = pltpu.roll(duc, 1, axis=0)[HALO:HALO + R], duc[HALO:HALO + R], pltpu.roll(duc, n - 1, axis=0)[HALO:HALO + R]
            du_ref[pl.ds(r0, R), :] = (up * w[0:1, :] + mid * w[1:2, :] + dn * w[2:3, :]).astype(BF16)
            dw = jnp.concatenate([jnp.sum(up * u_mid, axis=0, keepdims=True), jnp.sum(mid * u_mid, axis=0, keepdims=True),
                                  jnp.sum(dn * u_mid, axis=0, keepdims=True)], axis=0)
            return dw, jnp.sum(mid, axis=0, keepdims=True)

        def step(c, carry):
            dw_g, db_g, dw_v, db_v = carry
            r0 = pl.multiple_of(c * R, R)
            g, ug_mid = conv(gpad, r0, wg, bg)
            val, uv_mid = conv(vpad, r0, wv, bv)
            da_ext = dapad[pl.ds(r0, n), :]
            sg = _sigmoid(g)
            ddw_v, ddb_v = conv_bwd(da_ext * (g * sg), uv_mid, wv, r0, duv_ref)
            ddw_g, ddb_g = conv_bwd(da_ext * val * (sg * (1.0 + g * (1.0 - sg))), ug_mid, wg, r0, dug_ref)
            return dw_g + ddw_g, db_g + ddb_g, dw_v + ddw_v, db_v + ddb_v

        z3, z1 = jnp.zeros((3, CONV_STRIP), F32), jnp.zeros((1, CONV_STRIP), F32)
        dwg_ref[...], dbg_ref[...], dwv_ref[...], dbv_ref[...] = lax.fori_loop(0, S // R, step, (z3, z1, z3, z1))

    half = lambda r, dt: (_strip(r, 0), jax.ShapeDtypeStruct((r, D_FF), dt))
    outs = [half(S, BF16), half(S, BF16), half(3, F32), half(3, F32), half(1, F32), half(1, F32)]
    return _pcall(
        body, name="conv_gate_bwd", grid=(N_STRIPS,),
        in_specs=[_strip(S, 0), _strip(S, 1), _strip(3, 0), _strip(3, 1), _strip(1, 0), _strip(1, 1), _strip(S, 0)],
        out_specs=[o[0] for o in outs], out_shape=[o[1] for o in outs],
        scratch_shapes=[pltpu.VMEM((S + 4 * HALO, CONV_STRIP), F32)] * 2 + [pltpu.VMEM((S + 2 * HALO, CONV_STRIP), F32)],
        dims=("parallel",))(u, u, conv_w, conv_w, conv_b, conv_b, da)


MESH = pl.DeviceIdType.MESH
ANY = pl.BlockSpec(memory_space=pl.ANY)


def _place():
    return lax.axis_index("x"), lax.axis_index("y"), lax.axis_index("c")


def _all_gather(shards):
    n_arr = len(shards)

    def body(*refs):
        ins, outs = refs[:n_arr], refs[n_arr:2 * n_arr]
        send_sems, recv_sems, local_sems = refs[2 * n_arr:]
        x, y, c = _place()
        me, sibling = (x, y, c), (x, y, 1 - c)
        chips = [(1 - x, y), (x, 1 - y), (1 - x, 1 - y)]

        def slot(a, p):
            return outs[a].at[4 * p[0] + 2 * p[1] + p[2]]

        def copy(a, k, block, to, src=None):
            return pltpu.make_async_remote_copy(
                src_ref=slot(a, block) if src is None else src, dst_ref=slot(a, block),
                send_sem=send_sems.at[a, k], recv_sem=recv_sems.at[a, k], device_id=to, device_id_type=MESH)

        mine = [pltpu.make_async_copy(ins[a], slot(a, me), local_sems.at[a]) for a in range(n_arr)]
        for cp in mine:
            cp.start()
        first = []
        for a in range(n_arr):
            first.append(copy(a, 0, me, sibling, src=ins[a]))
            first += [copy(a, 1 + j, me, (*chip, c), src=ins[a]) for j, chip in enumerate(chips)]
        for cp in first:
            cp.start()
        passed = []
        for j, chip in enumerate(chips):
            for a in range(n_arr):
                copy(a, 1 + j, (*chip, c), me).wait_recv()
                cp = copy(a, 4 + j, (*chip, c), sibling)
                cp.start()
                passed.append(cp)
        for a in range(n_arr):
            copy(a, 0, sibling, me).wait_recv()
            for j, chip in enumerate(chips):
                copy(a, 4 + j, (*chip, 1 - c), me).wait_recv()
        for cp in first + passed:
            cp.wait_send()
        for cp in mine:
            cp.wait()

    return pl.pallas_call(
        body, name="all_gather_weights",
        in_specs=[ANY] * n_arr, out_specs=[ANY] * n_arr,
        out_shape=[jax.ShapeDtypeStruct((N_DEV, *s.shape), s.dtype) for s in shards],
        scratch_shapes=[pltpu.SemaphoreType.DMA((n_arr, 7)), pltpu.SemaphoreType.DMA((n_arr, 7)), pltpu.SemaphoreType.DMA((n_arr,))],
    )(*shards)


def _xchg_out_shapes(stacked, replicated):
    return ([jax.ShapeDtypeStruct(s.shape, s.dtype) for s in stacked]
            + [jax.ShapeDtypeStruct((N_DEV, *r.shape), r.dtype) for r in replicated])


def _xchg_sems(n_arr):
    return [pltpu.SemaphoreType.DMA((n_arr, 7)), pltpu.SemaphoreType.DMA((n_arr, 7)), pltpu.SemaphoreType.DMA((n_arr,))]


def _xchg_copies(ins, outs, sems, n_st, with_recv):
    send_sems, recv_sems, local_sems = sems
    n_arr = len(ins)
    x, y, c = _place()
    me = 4 * x + 2 * y + c

    def src(a, idx):
        return ins[a].at[idx] if a < n_st else ins[a]

    mine = [pltpu.make_async_copy(src(a, me), outs[a].at[me], local_sems.at[a]) for a in range(n_arr)]
    pairs = []
    for k in range(1, N_DEV):
        px, py, pc = x ^ (k >> 2), y ^ ((k >> 1) & 1), c ^ (k & 1)
        peer = 4 * px + 2 * py + pc
        for a in range(n_arr):
            sems_k = dict(send_sem=send_sems.at[a, k - 1], recv_sem=recv_sems.at[a, k - 1], device_id_type=MESH)
            send = pltpu.make_async_remote_copy(src_ref=src(a, peer), dst_ref=outs[a].at[me], device_id=(px, py, pc), **sems_k)
            recv = None
            if with_recv:
                recv = pltpu.make_async_remote_copy(src_ref=src(a, peer), dst_ref=outs[a].at[peer], device_id=(x, y, c), **sems_k)
            pairs.append((send, recv))
    return mine, pairs


def _xchg_start(ins, outs, sems, n_st):
    mine, pairs = _xchg_copies(ins, outs, sems, n_st, False)
    for cp in mine:
        cp.start()
    for send, _ in pairs:
        send.start()


def _xchg_finish(ins, outs, sems, n_st):
    mine, pairs = _xchg_copies(ins, outs, sems, n_st, True)
    for _, recv in pairs:
        recv.wait_recv()
    for send, _ in pairs:
        send.wait_send()
    for cp in mine:
        cp.wait()


def _exchange(stacked, replicated, name):
    _, landed = _pcall(lambda: None, name=name, grid=(), in_specs=[], out_specs=[], out_shape=[], comm=(stacked, replicated))()
    return landed


HBM = pl.BlockSpec(memory_space=pltpu.HBM)
SEMS = pl.BlockSpec(memory_space=pltpu.SEMAPHORE)
SIDE_EFFECT = pltpu.SideEffectType.DATAFLOW_SIDE_EFFECTING


N_SPLIT_SEMS = 2 * (N_DEV - 1)


def _split_copies(src, land, sems, with_recv):
    x, y, c = _place()
    me = 4 * x + 2 * y + c
    pairs = []
    for k in range(1, N_DEV):
        px, py, pc = x ^ (k >> 2), y ^ ((k >> 1) & 1), c ^ (k & 1)
        peer = 4 * px + 2 * py + pc
        sems_k = dict(send_sem=sems[k - 1], recv_sem=sems[N_DEV - 1 + k - 1], device_id_type=MESH)
        send = pltpu.make_async_remote_copy(src_ref=src.at[peer], dst_ref=land.at[me], device_id=(px, py, pc), **sems_k)
        recv = None
        if with_recv:
            recv = pltpu.make_async_remote_copy(src_ref=src.at[peer], dst_ref=land.at[peer], device_id=(x, y, c), **sems_k)
        pairs.append((send, recv))
    return pairs


def _exchange_start(stacked, name):
    def body(src, land, *rest):
        for send, _ in _split_copies(src, land, rest[:N_SPLIT_SEMS], False):
            send.start()
        rest[-1][...] = jnp.zeros_like(rest[-1])

    shape = pltpu.HBM(stacked.shape, stacked.dtype)
    res = pl.pallas_call(
        body, name=name, in_specs=[HBM, HBM],
        out_shape=(*[pltpu.SemaphoreType.DMA(())] * N_SPLIT_SEMS, shape, shape, jax.ShapeDtypeStruct((8, 128), F32)),
        out_specs=(*[SEMS] * N_SPLIT_SEMS, HBM, HBM, pl.BlockSpec(memory_space=pltpu.VMEM)),
        input_output_aliases={0: N_SPLIT_SEMS, 1: N_SPLIT_SEMS + 1},
        compiler_params=pltpu.CompilerParams(has_side_effects=SIDE_EFFECT),
    )(pltpu.with_memory_space_constraint(stacked, pltpu.HBM),
      pltpu.with_memory_space_constraint(lax.empty(stacked.shape, stacked.dtype), pltpu.HBM))
    return res[:N_SPLIT_SEMS], res[N_SPLIT_SEMS], res[N_SPLIT_SEMS + 1], res[-1]


def _exchange_wait(sems, src, land, after, name):
    def body(src_ref, land_ref, *rest):
        for send, recv in _split_copies(src_ref, land_ref, rest[:N_SPLIT_SEMS], True):
            send.wait_send()
            recv.wait_recv()

    shape = pltpu.HBM(src.shape, src.dtype)
    return pl.pallas_call(
        body, name=name, in_specs=[HBM, HBM, *[SEMS] * N_SPLIT_SEMS, ANY],
        out_shape=(shape, shape), out_specs=(HBM, HBM), input_output_aliases={0: 0, 1: 1},
        compiler_params=pltpu.CompilerParams(has_side_effects=SIDE_EFFECT))(src, land, *sems, after)[1]


def _adamw(parts, w, m, v, name):
    _, R, C = w.shape
    tr = R if R <= 512 else max(t for t in range(16, 513, 16) if R % t == 0)
    pr = tr if parts.shape[1] == R else -(-R // 16) * 16
    assert pr == tr or tr == R

    def body(p_ref, w_ref, m_ref, v_ref, g_ref, d_ref, nm_ref, nv_ref):
        g = p_ref[0].astype(F32)[:tr]
        for s in range(1, N_DEV):
            g = g + p_ref[s].astype(F32)[:tr]
        m2 = ADAM_B1 * m_ref[0] + (1.0 - ADAM_B1) * g
        v2 = ADAM_B2 * v_ref[0] + (1.0 - ADAM_B2) * (g * g)
        m_hat = m2 / (1.0 - ADAM_B1 ** ADAM_STEP)
        v_hat = v2 / (1.0 - ADAM_B2 ** ADAM_STEP)
        g_ref[0] = g
        d_ref[0] = -ADAM_LR * (m_hat / (jnp.sqrt(v_hat) + ADAM_EPS) + ADAM_WD * w_ref[0])
        nm_ref[0] = m2
        nv_ref[0] = v2

    blk = pl.BlockSpec((1, tr, C), lambda i: (0, i, 0))
    return _pcall(body, name=name, grid=(R // tr,),
                  in_specs=[pl.BlockSpec((N_DEV, pr, C), lambda i: (0, i, 0)), blk, blk, blk],
                  out_specs=[blk] * 4, out_shape=[jax.ShapeDtypeStruct((1, R, C), F32)] * 4,
                  dims=("parallel",))(parts, w, m, v)


def _t5_bucket(rel):
    nb = NUM_BUCKETS // 2
    max_exact = nb // 2
    base = (rel > 0).astype(jnp.int32) * nb
    n = jnp.abs(rel)
    nf = jnp.maximum(n, 1).astype(jnp.float32)
    large = max_exact + (jnp.log(nf / max_exact) / math.log(MAX_DISTANCE / max_exact) * (nb - max_exact)).astype(jnp.int32)
    large = jnp.minimum(large, nb - 1)
    return base + jnp.where(n < max_exact, n, large)


def _unstack_cols(g):
    return jnp.transpose(g, (1, 0, 2)).reshape(g.shape[1], N_DEV * g.shape[2])


def _stack_cols(w, n=N_DEV):
    R = w.shape[0]
    return jnp.transpose(w.reshape(R, n, w.shape[1] // n), (1, 0, 2))


def _stack_halves(g, v):
    return jnp.concatenate([_stack_cols(g, N_DEV // 2), _stack_cols(v, N_DEV // 2)], axis=0)


def kernel(x, positions, norm1_g, w_in, q_a_norm_g, w_q_b, kv_a_norm_g, w_kv_b, rel_bias, sinks, w_out, norm2_g, w_up, conv_w, conv_b, w_down, final_norm_g, loss_target, m_norm1_g, m_w_in, m_q_a_norm_g, m_w_q_b, m_kv_a_norm_g, m_w_kv_b, m_rel_bias, m_sinks, m_w_out, m_norm2_g, m_w_up, m_conv_w, m_conv_b, m_w_down, m_final_norm_g, v_norm1_g, v_w_in, v_q_a_norm_g, v_w_q_b, v_kv_a_norm_g, v_w_kv_b, v_rel_bias, v_sinks, v_w_out, v_norm2_g, v_w_up, v_conv_w, v_conv_b, v_w_down, v_final_norm_g):
    S = x.shape[1]
    x = x[0]
    target = loss_target[0]
    TM = 256

    tr = lambda w: jnp.swapaxes(w, 1, 2)
    g_in, g_qb, g_kvb = _all_gather([tr(w_in)[0].astype(BF16), tr(w_q_b)[0].astype(BF16), w_kv_b[0].astype(BF16)])
    late_weights = [w_out[0].astype(BF16), tr(w_up)[0].astype(BF16), conv_w[0]]
    wi = g_in.reshape(W_IN_COLS, D_MODEL)
    c0, c1, c2, c3, c4, c5 = (sum(W_IN_SIZES[:i + 1]) for i in range(6))
    w_in_pt = jnp.concatenate([wi[c4:c5], wi[c5:], wi[c1:c2], wi[:c0], wi[c2:c3], wi[c3:c4],
                               wi[c0:c0 + KV_LORA], wi[c0 + KV_LORA:c1], jnp.zeros((64, D_MODEL), BF16)], axis=0)
    wq = g_qb.reshape(H_A, QK_HEAD, Q_LORA)
    w_qb_pt = jnp.concatenate([wq[:, :QK_NOPE].reshape(H_A * QK_NOPE, Q_LORA), wq[:, QK_NOPE:].reshape(H_A * QK_ROPE, Q_LORA)], axis=0)
    w_kvb = _unstack_cols(g_kvb)

    half = QK_ROPE // 2
    inv_freq = ROPE_THETA ** (-jnp.arange(half, dtype=F32) / half)
    ang = positions.astype(F32)[:, None] * inv_freq[None, :]
    cos, sin = jnp.cos(ang), jnp.sin(ang)
    qa = jnp.arange(Q_BLOCK, dtype=jnp.int32)[:, None]
    kc = jnp.arange(SPAN, dtype=jnp.int32)[None, :]
    rel = (kc - WINDOW - qa).T
    in_band = (jnp.abs(rel) <= WINDOW).astype(F32).reshape(1, Q_BLOCK * SPAN)
    onehot_t = (_t5_bucket(rel).reshape(1, Q_BLOCK * SPAN) == jnp.arange(NUM_BUCKETS, dtype=jnp.int32)[:, None]).astype(F32)
    bias_t = _bias_table(rel_bias.T, onehot_t, in_band).reshape(H_B, SPAN, Q_BLOCK)
    sinks_b = jnp.broadcast_to(sinks.reshape(H_B, 1), (H_B, Q_BLOCK))

    (h1,) = _rowwise(lambda a, g: (_rms(a, g),), "norm1", S, TM, [_rows(x), _whole(norm1_g)], [("rows", D_MODEL, BF16)])
    proj = _matmul(h1, w_in_pt, "nt", BF16, "proj")

    def lat_fn(qlat, ckv, kr, gq, gkv, cs, sn):
        r1, r2 = _rope(kr[:, :half], kr[:, half:QK_ROPE], cs, sn)
        return _rms(qlat, gq), _rms(ckv, gkv), jnp.concatenate([r1, r2], axis=1)

    qn, ckvn, k_rope = _rowwise(lat_fn, "latents", S, TM,
                                [_rows(proj, 256, PROJ_QLAT), _rows(proj, 128, PROJ_CKV), _rows(proj, 128, PROJ_KROPE),
                                 _whole(q_a_norm_g), _whole(kv_a_norm_g), _rows(cos), _rows(sin)],
                                [("rows", Q_LORA, BF16), ("rows", KV_LORA, BF16), ("rows", QK_ROPE, BF16)])
    def q_heads_fn(q, cs, sn):
        q = q * MLA_PRESCALE
        outs = []
        for h in range(H_A):
            o = H_A * QK_NOPE + QK_ROPE * h
            r1, r2 = _rope(q[:, o:o + half], q[:, o + half:o + QK_ROPE], cs, sn)
            outs.append(jnp.concatenate([q[:, QK_NOPE * h:QK_NOPE * (h + 1)], r1, r2], axis=1)[None])
        return (jnp.concatenate(outs, axis=0),)

    (q_full,) = _matmul(qn, w_qb_pt, "nt", None, "q_up_heads", tm=512, tn=1536,
                        epi=(q_heads_fn, [_rows(cos), _rows(sin)], [("heads", H_A, QK_HEAD, BF16)]))

    def k_heads_fn(kvf, kr):
        return kvf, jnp.concatenate([jnp.concatenate([kvf[:, 256 * h:256 * h + QK_NOPE], kr], axis=1)[None] for h in range(H_A)], axis=0)

    kv, k_full = _matmul(ckvn, w_kvb, "nn", None, "kv_up_heads", tm=512, tn=2048,
                         epi=(k_heads_fn, [_rows(k_rope)], [("rows", H_A * (QK_NOPE + V_DIM), BF16), ("heads", H_A, QK_HEAD, BF16)]))
    (o_a, lse), (g_out, g_up, g_cw) = _mla_fwd(q_full, k_full, kv, S, comm=([], late_weights))
    w_out_f = g_out.reshape(D_MODEL, D_MODEL)
    w_up_t = g_up.reshape(2 * D_FF, D_MODEL)
    conv_w_f = _unstack_cols(g_cw)

    o_b, (g_down,) = _win_fwd(proj, bias_t, sinks_b, S, comm=([], [w_down[0].astype(BF16)]))
    w_down_f = g_down.reshape(D_FF, D_MODEL)

    (mixed,) = _rowwise(lambda ga, gb, oa, ob: (_sigmoid(ga) * oa + _sigmoid(gb) * ob,), "gate_mix", S, TM,
                        [_rows(proj, 1024, PROJ_GA), _rows(proj, 1024, PROJ_GB), _rows(o_a), _rows(o_b)], [("rows", D_MODEL, BF16)])
    x1, h2 = _matmul(mixed, w_out_f, "nn", None, "out_proj", residual=x, tm=512,
                     epi=(lambda a, g: (a, _rms(a, g)), [_whole(norm2_g)], [("rows", D_MODEL, F32), ("rows", D_MODEL, BF16)]))
    u = _matmul(h2, w_up_t, "nt", BF16, "ffn_up", tn=1408)
    act = _conv_gate_fwd(u, conv_w_f, conv_b, S)

    def final_fn(a, g, t):
        err = _rms(a, g) - t
        loss = 0.5 * jnp.sum(jnp.mean(err * err, axis=-1, keepdims=True), axis=0, keepdims=True)
        dx, dg = _rms_bwd(err * (1.0 / D_MODEL), a, g)
        return dx, dx, dg, jnp.broadcast_to(loss, (1, 128))

    gfin = final_norm_g.reshape(1, D_MODEL)
    dx2, dx2_b, d_gfin, loss_row = _matmul(
        act, w_down_f, "nn", None, "ffn_down_loss", residual=x1, tm=512,
        epi=(final_fn, [_whole(gfin), _rows(target)],
             [("rows", D_MODEL, F32), ("rows", D_MODEL, BF16), ("acc", 1, D_MODEL), ("acc", 1, 128)]))
    d_act = _matmul(dx2_b, w_down_f, "nt", BF16, "ffn_down_dx", tn=1408)
    d_w_down = _matmul(act, dx2_b, "tn", BF16, "ffn_down_dw")
    du_g, du_v, dcw_g, dcw_v, dcb_g, dcb_v = _conv_gate_bwd(u, conv_w_f, conv_b, d_act, S)
    d_conv_b = jnp.concatenate([dcb_g, dcb_v], axis=1)

    def norm_bwd_fn(dh, a, g, dres):
        dx, dg = _rms_bwd(dh, a, g)
        dx = dx + dres
        return dx, dx, dg

    dx1, dx1_b, d_g2 = _matmul(du_g, w_up_t, "nn", None, "ffn_up_dx_norm2_bwd", tm=256, a2=du_v,
                               epi=(norm_bwd_fn, [_rows(x1), _whole(norm2_g), _rows(dx2)],
                                    [("rows", D_MODEL, F32), ("rows", D_MODEL, BF16), ("acc", 1, D_MODEL)]))
    d_w_up_t = _matmul(du_g, h2, "tn", BF16, "ffn_up_dw", tm=256, a2=du_v)
    d_w_out = _matmul(mixed, dx1_b, "tn", BF16, "out_proj_dw", tm=512)

    def gate_bwd_fn(dm, ga, gb, oa, ob):
        sa, sb = _sigmoid(ga), _sigmoid(gb)
        return jnp.concatenate([dm * oa * sa * (1.0 - sa), dm * ob * sb * (1.0 - sb)], axis=1), dm * sa, dm * sb

    d_proj, do_a, do_b = _matmul(
        dx1_b, w_out_f, "nt", None, "out_proj_dx_gate_bwd", tm=512,
        epi=(gate_bwd_fn, [_rows(proj, 1024, PROJ_GA), _rows(proj, 1024, PROJ_GB), _rows(o_a), _rows(o_b)],
             [("cols", 2 * D_MODEL, 0, PROJ_P, BF16), ("rows", D_MODEL, BF16), ("rows", D_MODEL, BF16)]))

    d_proj, dk_acc, dv_acc, d_bias, d_sinks_g = _win_bwd(proj, bias_t, sinks_b, do_b, d_proj, S)
    d_sinks = d_sinks_g.reshape(1, H_B)

    early = [d_w_out.reshape(N_DEV, D_MODEL // N_DEV, D_MODEL), d_w_up_t.reshape(N_DEV, 2 * D_FF // N_DEV, D_MODEL),
             d_w_down.reshape(N_DEV, D_FF // N_DEV, D_MODEL), _stack_halves(dcw_g, dcw_v)]
    (dq_full, dk_full, dv_full), recv_early = _mla_bwd(q_full, k_full, kv, do_a, o_a, lse, S, comm=(early, []))

    def dq_post_fn(dq, cs, sn):
        nope = [dq[h, :, :QK_NOPE] for h in range(H_A)]
        rope = []
        for h in range(H_A):
            rope += list(_rope_bwd(dq[h, :, QK_NOPE:QK_NOPE + half], dq[h, :, QK_NOPE + half:], cs, sn))
        return (jnp.concatenate(nope + rope, axis=1),)

    (dq_p,) = _rowwise(dq_post_fn, "dq_post", S, TM, [_heads(dq_full), _rows(cos), _rows(sin)], [("rows", H_A * QK_HEAD, BF16)])

    def dkv_post_fn(dk, dv, cs, sn):
        dk = dk * math.log(2.0)
        dkv = jnp.concatenate([jnp.concatenate([dk[h, :, :QK_NOPE], dv[h]], axis=1) for h in range(H_A)], axis=1)
        dkr = dk[0, :, QK_NOPE:]
        for h in range(1, H_A):
            dkr = dkr + dk[h, :, QK_NOPE:]
        r1, r2 = _rope_bwd(dkr[:, :half], dkr[:, half:], cs, sn)
        return dkv, jnp.concatenate([r1, r2], axis=1)

    dkv, d_krope = _rowwise(dkv_post_fn, "dkv_post", S, TM, [_heads(dk_full), _heads(dv_full), _rows(cos), _rows(sin)],
                            [("rows", H_A * (QK_NOPE + V_DIM), BF16), ("rows", QK_ROPE, F32)])
    d_qn = _matmul(dq_p, w_qb_pt, "nn", F32, "q_up_dx")
    d_w_qb_pt = _matmul(dq_p, qn, "tn", BF16, "q_up_dw", tm=512)
    d_ckvn = _matmul(dkv, w_kvb, "nt", F32, "kv_up_dx")
    d_w_kvb = _matmul(ckvn, dkv, "tn", BF16, "kv_up_dw", tn=2048)

    def lat_bwd_fn(dqn, dckvn, dkr, qlat, ckv, gq, gkv, dkb, dvb):
        dql, dgq = _rms_bwd(dqn, qlat, gq)
        dck, dgkv = _rms_bwd(dckvn, ckv, gkv)
        tail = jnp.concatenate([dql, dkb, dvb, dck, dkr, jnp.zeros_like(dkr)], axis=1)
        return tail, dgq, dgkv

    shifted = lambda arr: (arr, lambda tm: pl.BlockSpec((tm, arr.shape[1]), lambda i, *_: (i + WINDOW // tm, 0)))
    TL = min(128, S)
    d_proj, d_gq, d_gkv = _rowwise(lat_bwd_fn, "latents_bwd", S, TL,
                                   [_rows(d_qn), _rows(d_ckvn), _rows(d_krope), _rows(proj, 256, PROJ_QLAT), _rows(proj, 128, PROJ_CKV),
                                    _whole(q_a_norm_g), _whole(kv_a_norm_g), shifted(dk_acc), shifted(dv_acc)],
                                   [("cols", 1024, 3, PROJ_P, BF16), ("acc", 1, Q_LORA), ("acc", 1, KV_LORA)], into=(d_proj, 0))
    d_w_qb_t = jnp.concatenate([d_w_qb_pt[:H_A * QK_NOPE].reshape(H_A, QK_NOPE, Q_LORA),
                                d_w_qb_pt[H_A * QK_NOPE:].reshape(H_A, QK_ROPE, Q_LORA)], axis=1)
    dp, recv_mid = _matmul(d_proj, h1, "tn", BF16, "proj_dw", tm=512, comm=([d_w_qb_t, _stack_cols(d_w_kvb)], []))

    late = jnp.concatenate([dp[3072:3328], dp[3840:3968], dp[3968:4032], dp[2048:3072], dp[3328:3584],
                            dp[3584:3840], dp[0:1024], dp[1024:2048]], axis=0).reshape(N_DEV, W_IN_COLS // N_DEV, D_MODEL)
    late_sems, late_src, late_land, started = _exchange_start(late, "late_grads_start")

    def norm1_bwd_fn(dh, a, g, dres):
        dx, dg = _rms_bwd(dh, a, g)
        return dx + dres, dg

    grad_x, d_g1 = _matmul(
        d_proj, w_in_pt, "nn", None, "proj_dx_norm1_bwd", tm=512,
        epi=(norm1_bwd_fn, [_rows(x), _whole(norm1_g + started[:1, :1]), _rows(dx1)], [("rows", D_MODEL, F32), ("acc", 1, D_MODEL)]))

    transposed = ("w_in", "w_q_b", "w_up")
    ready_names = ["w_out", "w_up", "w_down", "conv_w", "w_q_b", "w_kv_b"]
    ready_wmv = [(w_out, m_w_out, v_w_out), (tr(w_up), tr(m_w_up), tr(v_w_up)), (w_down, m_w_down, v_w_down), (conv_w, m_conv_w, v_conv_w),
                 (tr(w_q_b), tr(m_w_q_b), tr(v_w_q_b)), (w_kv_b, m_w_kv_b, v_w_kv_b)]
    big = {n: _adamw(r, *wmv, "adamw_" + n) for n, r, wmv in zip(ready_names, [*recv_early, *recv_mid], ready_wmv)}

    d_bias, _ = lax.optimization_barrier((d_bias, started))
    d_rel_bias = _bias_table_bwd(d_bias.reshape(H_B, Q_BLOCK * SPAN), onehot_t).T

    after = lax.optimization_barrier([big[n][0] for n in ready_names] + [d_rel_bias])
    landed = _exchange_wait(late_sems, late_src, late_land, after[-1], "late_grads_wait")
    me = 4 * lax.axis_index("x") + 2 * lax.axis_index("y") + lax.axis_index("c")
    landed = lax.dynamic_update_slice_in_dim(landed, lax.dynamic_slice_in_dim(late, me, 1, axis=0), me, axis=0)
    big["w_in"] = _adamw(landed, tr(w_in), tr(m_w_in), tr(v_w_in), "adamw_w_in")

    small_parts = [d_g1, d_gq, d_gkv, d_rel_bias.reshape(1, NUM_BUCKETS * H_B), d_sinks, d_g2, d_conv_b, d_gfin, loss_row[:, :1]]
    small = jnp.concatenate(small_parts, axis=1)
    n_small = small.shape[1]
    pad = (-n_small) % 128
    small = jnp.pad(small, ((0, 0), (0, pad)))
    small, _ = lax.optimization_barrier((small, [landed, *after]))
    (recv_small,) = _exchange([], [small], "exchange_small_grads")

    def flat(a):
        return a.reshape(1, -1)

    small_w = [norm1_g, q_a_norm_g, kv_a_norm_g, rel_bias, sinks, norm2_g, conv_b, final_norm_g]
    small_m = [m_norm1_g, m_q_a_norm_g, m_kv_a_norm_g, m_rel_bias, m_sinks, m_norm2_g, m_conv_b, m_final_norm_g]
    small_v = [v_norm1_g, v_q_a_norm_g, v_kv_a_norm_g, v_rel_bias, v_sinks, v_norm2_g, v_conv_b, v_final_norm_g]
    cat = lambda parts: jnp.pad(jnp.concatenate([flat(a) for a in parts], axis=1), ((0, 0), (0, pad + 1)))[None]
    sm = _adamw(recv_small, cat(small_w), cat(small_m), cat(small_v), "adamw_small")

    loss = sm[0][0, 0, n_small - 1]
    order =["norm1_g", "w_in", "q_a_norm_g", "w_q_b", "kv_a_norm_g", "w_kv_b", "rel_bias", "sinks", "w_out", "norm2_g", "w_up",
             "conv_w", "conv_b", "w_down", "final_norm_g"]
    small_names = ["norm1_g", "q_a_norm_g", "kv_a_norm_g", "rel_bias", "sinks", "norm2_g", "conv_b", "final_norm_g"]
    offs, o = {}, 0
    for n, a in zip(small_names, small_w):
        offs[n] = (o, a.size, a.shape)
        o += a.size
    outs = [loss, grad_x[None]]
    for kind in range(4):
        for n in order:
            if n in big:
                outs.append(tr(big[n][kind]) if n in transposed else big[n][kind])
            else:
                o, size, shape = offs[n]
                outs.append(sm[kind][0, 0, o:o + size].reshape(shape))
    return tuple(outs)
```

```python
import math

import jax
import jax.numpy as jnp
from jax import lax
from jax.experimental import pallas as pl
from jax.experimental.pallas import tpu as pltpu

F32 = jnp.float32
BF16 = jnp.bfloat16

N_DEV = 8
D_MODEL = 1024
EPS = 1e-6
H_A, QK_NOPE, QK_ROPE, V_DIM, Q_LORA, KV_LORA = 8, 128, 64, 128, 256, 128
QK_HEAD = QK_NOPE + QK_ROPE
ROPE_THETA = 10000.0
H_B, KV_B, GROUP, HD_B, WINDOW, Q_BLOCK = 16, 4, 4, 64, 128, 128
SPAN = Q_BLOCK + 2 * WINDOW
NUM_BUCKETS, MAX_DISTANCE = 32, 128
D_FF = 2816
ADAM_LR, ADAM_B1, ADAM_B2, ADAM_EPS, ADAM_WD, ADAM_STEP = 0.001, 0.9, 0.999, 1e-08, 0.01, 10

W_IN_SIZES = (Q_LORA, KV_LORA + QK_ROPE, H_B * HD_B, KV_B * HD_B, KV_B * HD_B, D_MODEL, D_MODEL)
W_IN_COLS = sum(W_IN_SIZES)
PROJ_P = 4096
PROJ_GA, PROJ_GB, PROJ_QB, PROJ_QLAT, PROJ_KB, PROJ_VB, PROJ_CKV, PROJ_KROPE = 0, 1, 2, 12, 13, 14, 30, 31

VMEM_LIMIT = 56 * 1024 * 1024

NN = (((1,), (0,)), ((), ()))
NT = (((1,), (1,)), ((), ()))
TN = (((0,), (0,)), ((), ()))


def _pcall(body, *, name, grid, in_specs, out_specs, out_shape, scratch_shapes=(), dims=None, comm=None, aliases=None, two_level=False):
    if comm is None:
        params = pltpu.CompilerParams(dimension_semantics=dims, vmem_limit_bytes=VMEM_LIMIT)
        return pl.pallas_call(body, name=name, grid=grid, in_specs=in_specs, out_specs=out_specs, out_shape=out_shape,
                              scratch_shapes=list(scratch_shapes), input_output_aliases=aliases or {}, compiler_params=params)
    assert not aliases
    stacked, replicated = comm
    arrs = [*stacked, *replicated]
    n_st, n_arr = len(stacked), len(arrs)
    single = not isinstance(out_specs, (list, tuple))
    o_specs, o_shape = ([out_specs], [out_shape]) if single else (list(out_specs), list(out_shape))
    n_in, n_out = len(in_specs), len(o_specs)

    def wrapped(*refs):
        c_in = refs[n_in:n_in + n_arr]
        c_out = refs[n_in + n_arr + n_out:n_in + 2 * n_arr + n_out]
        sems = refs[len(refs) - 3:]
        own = (*refs[:n_in], *refs[n_in + n_arr:n_in + n_arr + n_out], *refs[n_in + 2 * n_arr + n_out:len(refs) - 3])
        if two_level:
            assert n_st == 0
            start, finish = (lambda: _gather2(c_in, c_out, sems, False)), (lambda: _gather2(c_in, c_out, sems, True))
        else:
            start, finish = (lambda: _xchg_start(c_in, c_out, sems, n_st)), (lambda: _xchg_finish(c_in, c_out, sems, n_st))
        if not grid:
            start()
            finish()
            return
        first = last = None
        for d, n in enumerate(grid):
            pid = pl.program_id(d)
            first = (pid == 0) if first is None else first & (pid == 0)
            last = (pid == n - 1) if last is None else last & (pid == n - 1)

        pl.when(first)(start)
        body(*own)
        pl.when(last)(finish)

    params = pltpu.CompilerParams(dimension_semantics=("arbitrary",) * len(grid), vmem_limit_bytes=VMEM_LIMIT)
    call = pl.pallas_call(wrapped, name=name, grid=grid, in_specs=[*in_specs, *[ANY] * n_arr], out_specs=[*o_specs, *[ANY] * n_arr],
                          out_shape=[*o_shape, *_xchg_out_shapes(stacked, replicated)],
                          scratch_shapes=[*scratch_shapes, *_xchg_sems(n_arr)], compiler_params=params)

    def run(*args):
        res = call(*args, *arrs)
        outs, landed = res[:n_out], res[n_out:]
        return (outs[0] if single else outs), landed

    return run


def _dot(a, b, dn):
    return lax.dot_general(a, b, dn, preferred_element_type=F32)


def _tile(n, target):
    best = None
    for t in range(128, min(n, target) + 1, 128):
        if n % t == 0:
            best = t
    return n if best is None else best


def _matmul(a, b, mode, out_dtype, name, residual=None, tm=1024, tn=1024, comm=None, a2=None, epi=None):
    if mode == "nn":
        (M, K), N = a.shape, b.shape[1]
    elif mode == "nt":
        (M, K), N = a.shape, b.shape[0]
    else:
        (K, M), N = a.shape, b.shape[1]
    tm, tn = _tile(M, tm), _tile(N, tn)
    a_spec = pl.BlockSpec((K, tm), lambda i, j: (0, i)) if mode == "tn" else pl.BlockSpec((tm, K), lambda i, j: (i, 0))
    b_spec = pl.BlockSpec((tn, b.shape[1]), lambda i, j: (j, 0)) if mode == "nt" else pl.BlockSpec((K, tn), lambda i, j: (0, j))
    o_spec = pl.BlockSpec((tm, tn), lambda i, j: (i, j))
    in_specs, args = [a_spec, b_spec], [a, b]
    n1 = M // tm
    if a2 is not None and mode == "tn":
        assert M % tm == 0 and a2.shape[1] % tm == 0
        in_specs[0] = pl.BlockSpec((K, tm), lambda i, j: (0, jnp.minimum(i, n1 - 1)))
        in_specs.append(pl.BlockSpec((K, tm), lambda i, j: (0, jnp.maximum(i - n1, 0))))
        args.append(a2)
        M += a2.shape[1]
    elif a2 is not None:
        assert (mode == "nt" and K + a2.shape[1] == b.shape[1]) or (mode == "nn" and K + a2.shape[1] == b.shape[0])
        if mode == "nn":
            b_spec = in_specs[1] = pl.BlockSpec((b.shape[0], tn), lambda i, j: (0, j))
        in_specs.append(pl.BlockSpec((tm, a2.shape[1]), lambda i, j: (i, 0)))
        args.append(a2)
    if residual is not None:
        in_specs.append(o_spec)
        args.append(residual)
    n_mm = len(args)
    scratch = [pltpu.VMEM((tm, K), a.dtype)] if mode == "tn" else []
    if epi is None:
        out_specs, out_shape, is_acc = o_spec, jax.ShapeDtypeStruct((M, N), out_dtype), None
    else:
        assert tn == N
        fn, epi_ins, epi_outs = epi
        in_specs += [mk(tm) for _, mk in epi_ins]
        args += [arr for arr, _ in epi_ins]
        out_specs, out_shape, is_acc = _row_out_specs(epi_outs, M, tm)

    def body(*refs):
        a_ref, b_ref = refs[0], refs[1]
        n_out = 1 if epi is None else len(is_acc)
        out_refs = refs[len(args):len(args) + n_out]
        if mode == "tn":
            at_ref = refs[len(args) + n_out]

            first_col = pl.program_id(1) == 0
            from_a = first_col if a2 is None else first_col & (pl.program_id(0) < n1)

            @pl.when(from_a)
            def _():
                at_ref[...] = a_ref[...].T

            if a2 is not None:
                @pl.when(first_col & (pl.program_id(0) >= n1))
                def _():
                    at_ref[...] = refs[2][...].T

            acc = _dot(at_ref[...], b_ref[...], NN)
        elif a2 is not None and mode == "nt":
            acc = _dot(a_ref[...], b_ref[:, :K], NT) + _dot(refs[2][...], b_ref[:, K:], NT)
        elif a2 is not None:
            acc = _dot(a_ref[...], b_ref[:K, :], NN) + _dot(refs[2][...], b_ref[K:, :], NN)
        else:
            acc = _dot(a_ref[...], b_ref[...], NT if mode == "nt" else NN)
        if residual is not None:
            acc = acc + refs[n_mm - 1][...]
        if epi is None:
            out_refs[0][...] = acc.astype(out_dtype)
        else:
            _store_rows(out_refs, fn(acc, *[_load_f32(r) for r in refs[n_mm:len(args)]]), is_acc)

    return _pcall(body, name=name, grid=(M // tm, N // tn), in_specs=in_specs, out_specs=out_specs,
                  out_shape=out_shape, scratch_shapes=scratch,
                  dims=("arbitrary" if epi is not None else "parallel", "arbitrary"), comm=comm)(*args)


def _rows(arr, width=None, col=0):
    width = arr.shape[1] if width is None else width
    return (arr, lambda tm: pl.BlockSpec((tm, width), lambda i, *_: (i, col)))


def _heads(arr):
    return (arr, lambda tm: pl.BlockSpec((arr.shape[0], tm, arr.shape[2]), lambda i, *_: (0, i, 0)))


def _whole(arr):
    nd = arr.ndim
    return (arr, lambda tm: pl.BlockSpec(arr.shape, lambda i, *_: (0,) * nd))


def _row_out_specs(outs, n_rows, tm):
    out_specs, out_shape, is_acc = [], [], []
    for o in outs:
        if o[0] == "rows":
            out_specs.append(pl.BlockSpec((tm, o[1]), lambda i, *_: (i, 0)))
            out_shape.append(jax.ShapeDtypeStruct((n_rows, o[1]), o[2]))
        elif o[0] == "cols":
            out_specs.append(pl.BlockSpec((tm, o[1]), lambda i, *_, c=o[2]: (i, c)))
            out_shape.append(jax.ShapeDtypeStruct((n_rows, o[3]), o[4]))
        elif o[0] == "heads":
            out_specs.append(pl.BlockSpec((o[1], tm, o[2]), lambda i, *_: (0, i, 0)))
            out_shape.append(jax.ShapeDtypeStruct((o[1], n_rows, o[2]), o[3]))
        else:
            out_specs.append(pl.BlockSpec((o[1], o[2]), lambda i, *_: (0, 0)))
            out_shape.append(jax.ShapeDtypeStruct((o[1], o[2]), F32))
        is_acc.append(o[0] == "acc")
    return out_specs, out_shape, is_acc


def _load_f32(r):
    v = r[...]
    return v.astype(F32) if v.dtype == BF16 else v


def _store_rows(out_refs, vals, is_acc):
    for r, v, acc in zip(out_refs, vals, is_acc):
        if acc:
            @pl.when(pl.program_id(0) == 0)
            def _():
                r[...] = jnp.zeros_like(r)

            r[...] += v
        else:
            r[...] = v.astype(r.dtype)


def _rowwise(fn, name, n_rows, tm, ins, outs, upcast=True, into=None, gather=None):
    tm = min(tm, n_rows)
    assert n_rows % tm == 0
    in_specs = [mk(tm) for _, mk in ins]
    out_specs, out_shape, is_acc = _row_out_specs(outs, n_rows, tm)
    n_in = len(ins)
    args = [a for a, _ in ins]
    aliases = {}
    if into is not None:
        in_specs.append(ANY)
        args.append(into[0])
        aliases = {n_in: into[1]}

    def body(*refs):
        vals = fn(*[_load_f32(r) if upcast else r[...] for r in refs[:n_in]])
        _store_rows(refs[len(args):], vals, is_acc)

    return _pcall(body, name=name, grid=(n_rows // tm,), in_specs=in_specs, out_specs=out_specs,
                  out_shape=out_shape, dims=("arbitrary",), aliases=aliases,
                  comm=None if gather is None else ([], gather), two_level=True)(*args)


def _rms(x, g):
    r = lax.rsqrt(jnp.mean(x * x, axis=-1, keepdims=True) + EPS)
    return x * r * g


def _rms_bwd(dy, x, g):
    r = lax.rsqrt(jnp.mean(x * x, axis=-1, keepdims=True) + EPS)
    xhat = x * r
    dxhat = dy * g
    dx = r * (dxhat - xhat * jnp.mean(dxhat * xhat, axis=-1, keepdims=True))
    return dx, jnp.sum(dy * xhat, axis=0, keepdims=True)


def _rope(x1, x2, cos, sin):
    return x1 * cos - x2 * sin, x2 * cos + x1 * sin


def _rope_bwd(d1, d2, cos, sin):
    return d1 * cos + d2 * sin, d2 * cos - d1 * sin


def _sigmoid(x):
    return 1.0 / (1.0 + jnp.exp(-x))


MLA_SCALE = 1.0 / math.sqrt(QK_HEAD)
MLA_PRESCALE = MLA_SCALE * math.log2(math.e)
MLA_TQ, MLA_KC = 1024, 1024


def _mla_fwd(q_full, k_full, kv, S, comm=None):
    tq, kc = min(MLA_TQ, S), min(MLA_KC, S)

    def body(q_ref, k_ref, v_ref, o_ref, lse_ref):
        q = q_ref[0]
        m = jnp.full((tq, 1), -1e30, F32)
        l = jnp.zeros((tq, 1), F32)
        acc = jnp.zeros((tq, V_DIM), F32)
        for c in range(S // kc):
            s = _dot(q, k_ref[0, c * kc:(c + 1) * kc, :], NT)
            m_new = jnp.maximum(m, jnp.max(s, axis=-1, keepdims=True))
            alpha = jnp.exp2(m - m_new)
            p = jnp.exp2(s - m_new)
            l = alpha * l + jnp.sum(p, axis=-1, keepdims=True)
            acc = alpha * acc + _dot(p.astype(BF16), v_ref[c * kc:(c + 1) * kc, :], NN)
            m = m_new
        o_ref[...] = (acc / l).astype(BF16)
        lse_ref[0] = m + jnp.log2(l)

    return _pcall(
        body, name="mla_fwd", grid=(H_A, S // tq),
        in_specs=[pl.BlockSpec((1, tq, QK_HEAD), lambda h, i: (h, i, 0)),
                  pl.BlockSpec((1, S, QK_HEAD), lambda h, i: (h, 0, 0)),
                  pl.BlockSpec((S, V_DIM), lambda h, i: (0, 2 * h + 1))],
        out_specs=[pl.BlockSpec((tq, V_DIM), lambda h, i: (i, h)),
                   pl.BlockSpec((1, tq, 1), lambda h, i: (h, i, 0))],
        out_shape=[jax.ShapeDtypeStruct((S, H_A * V_DIM), BF16), jax.ShapeDtypeStruct((H_A, S, 1), F32)],
        dims=("parallel", "parallel"), comm=comm)(q_full, k_full, kv)


def _mla_bwd(q_full, k_full, kv, do_a, o_a, lse, S, comm=None):
    tq, kc = min(MLA_TQ, S), min(MLA_KC, S)

    def body(q_ref, k_ref, v_ref, do_ref, o_ref, lse_ref, dq_ref, dk_ref, dv_ref):
        @pl.when(pl.program_id(1) == 0)
        def _():
            dk_ref[...] = jnp.zeros_like(dk_ref)
            dv_ref[...] = jnp.zeros_like(dv_ref)

        q = q_ref[0]
        do = do_ref[...]
        lse_q = lse_ref[0]
        delta = jnp.sum(do.astype(F32) * o_ref[...].astype(F32), axis=-1, keepdims=True)
        dq = jnp.zeros((tq, QK_HEAD), F32)
        for c in range(S // kc):
            k = k_ref[0, c * kc:(c + 1) * kc, :]
            v = v_ref[c * kc:(c + 1) * kc, :]
            p = jnp.exp2(_dot(q, k, NT) - lse_q)
            ds = (p * (_dot(do, v, NT) - delta)).astype(BF16)
            dq = dq + _dot(ds, k, NN)
            dk_ref[0, c * kc:(c + 1) * kc, :] += _dot(ds, q, TN)
            dv_ref[0, c * kc:(c + 1) * kc, :] += _dot(p.astype(BF16), do, TN)
        dq_ref[0] = dq * MLA_SCALE

    return _pcall(
        body, name="mla_bwd", grid=(H_A, S // tq),
        in_specs=[pl.BlockSpec((1, tq, QK_HEAD), lambda h, i: (h, i, 0)),
                  pl.BlockSpec((1, S, QK_HEAD), lambda h, i: (h, 0, 0)),
                  pl.BlockSpec((S, V_DIM), lambda h, i: (0, 2 * h + 1)),
                  pl.BlockSpec((tq, V_DIM), lambda h, i: (i, h)),
                  pl.BlockSpec((tq, V_DIM), lambda h, i: (i, h)),
                  pl.BlockSpec((1, tq, 1), lambda h, i: (h, i, 0))],
        out_specs=[pl.BlockSpec((1, tq, QK_HEAD), lambda h, i: (h, i, 0)),
                   pl.BlockSpec((1, S, QK_HEAD), lambda h, i: (h, 0, 0)),
                   pl.BlockSpec((1, S, V_DIM), lambda h, i: (h, 0, 0))],
        out_shape=[jax.ShapeDtypeStruct((H_A, S, QK_HEAD), F32), jax.ShapeDtypeStruct((H_A, S, QK_HEAD), F32),
                   jax.ShapeDtypeStruct((H_A, S, V_DIM), F32)],
        dims=("parallel", "arbitrary"), comm=comm)(q_full, k_full, kv, do_a, o_a, lse)


WIN_SCALE = 1.0 / math.sqrt(HD_B)


WIN_PER_STEP = 4


def _win_specs(S):
    last, B = S // Q_BLOCK - 1, WIN_PER_STEP
    qspec = pl.BlockSpec((B * Q_BLOCK, H_B * HD_B), lambda i: (i, PROJ_QB))
    kspecs = [[pl.BlockSpec((Q_BLOCK, KV_B * HD_B), lambda i, d=d, c=c: (jnp.clip(B * i + d, 0, last), c)) for d in range(-1, B + 1)]
              for c in (PROJ_KB, PROJ_VB)]
    bias_spec = pl.BlockSpec((H_B, SPAN, Q_BLOCK), lambda i: (0, 0, 0))
    sink_spec = pl.BlockSpec((H_B, Q_BLOCK), lambda i: (0, 0))
    return qspec, kspecs, bias_spec, sink_spec


def _win_edge_ok(n, n_blk):
    row = lax.broadcasted_iota(jnp.int32, (SPAN, 1), 0)
    return jnp.logical_not(((n == 0) & (row < WINDOW)) | ((n == n_blk - 1) & (row >= SPAN - WINDOW)))


def _lanes4(pieces):
    return jnp.concatenate(pieces, axis=1)


def _win_probs(kg, q4t, bias_ref, sink_ref, g, edge_ok):
    bias4 = _lanes4([bias_ref[GROUP * g + j] for j in range(GROUP)])
    sink4 = _lanes4([sink_ref[GROUP * g + j:GROUP * g + j + 1, :] for j in range(GROUP)])
    s = jnp.where(edge_ok, _dot(kg, q4t, NN) + bias4, -1e30)
    m = jnp.maximum(jnp.max(s, axis=0, keepdims=True), sink4)
    p = jnp.exp(s - m)
    e_sink = jnp.exp(sink4 - m)
    inv_l = 1.0 / (jnp.sum(p, axis=0, keepdims=True) + e_sink)
    return p * inv_l, e_sink * inv_l


def _group_t(xt, g):
    return _lanes4([xt[HD_B * (GROUP * g + j):HD_B * (GROUP * g + j + 1), :] for j in range(GROUP)])


def _rows_of(ref, b):
    return ref[Q_BLOCK * b:Q_BLOCK * (b + 1), :]


def _win_fwd(proj, bias_t, sinks_b, S, comm=None):
    n_blk, B = S // Q_BLOCK, WIN_PER_STEP
    qspec, kspecs, bias_spec, sink_spec = _win_specs(S)

    def body(q_ref, *refs):
        k_refs, v_refs, (bias_ref, sink_ref, o_ref) = refs[:B + 2], refs[B + 2:2 * B + 4], refs[2 * B + 4:]
        for b in range(B):
            edge_ok = _win_edge_ok(B * pl.program_id(0) + b, n_blk)
            k = jnp.concatenate([r[...] for r in k_refs[b:b + 3]], axis=0)
            vt = jnp.concatenate([r[...] for r in v_refs[b:b + 3]], axis=0).T
            qt = (_rows_of(q_ref, b).astype(F32) * WIN_SCALE).T.astype(BF16)
            parts = []
            for g in range(KV_B):
                p, _ = _win_probs(k[:, HD_B * g:HD_B * (g + 1)], _group_t(qt, g), bias_ref, sink_ref, g, edge_ok)
                o4t = _dot(vt[HD_B * g:HD_B * (g + 1), :], p.astype(BF16), NN)
                parts += [o4t[:, Q_BLOCK * j:Q_BLOCK * (j + 1)] for j in range(GROUP)]
            o_ref[Q_BLOCK * b:Q_BLOCK * (b + 1), :] = jnp.concatenate(parts, axis=0).T.astype(BF16)

    return _pcall(body, name="win_fwd", grid=(n_blk // B,),
                  in_specs=[qspec, *kspecs[0], *kspecs[1], bias_spec, sink_spec],
                  out_specs=pl.BlockSpec((B * Q_BLOCK, H_B * HD_B), lambda i: (i, 0)),
                  out_shape=jax.ShapeDtypeStruct((S, H_B * HD_B), BF16),
                  dims=("parallel",), comm=comm)(*[proj] * (2 * B + 5), bias_t, sinks_b)


def _win_bwd(proj, bias_t, sinks_b, do_b, d_proj, S):
    n_blk, B = S // Q_BLOCK, WIN_PER_STEP
    qspec, kspecs, bias_spec, sink_spec = _win_specs(S)

    def body(q_ref, *refs):
        k_refs, v_refs = refs[:B + 2], refs[B + 2:2 * B + 4]
        bias_ref, sink_ref, do_ref, _, dq_ref, dk_ref, dv_ref, dbias_ref, dsink_ref, dsink_acc = refs[2 * B + 4:]
        i = pl.program_id(0)

        @pl.when(i == 0)
        def _():
            dk_ref[...] = jnp.zeros_like(dk_ref)
            dv_ref[...] = jnp.zeros_like(dv_ref)
            dbias_ref[...] = jnp.zeros_like(dbias_ref)
            dsink_acc[...] = jnp.zeros_like(dsink_acc)

        d_bias, d_sink, dk_blocks, dv_blocks = {}, {}, [], []
        for b in range(B):
            edge_ok = _win_edge_ok(B * i + b, n_blk)
            k = jnp.concatenate([r[...] for r in k_refs[b:b + 3]], axis=0)
            v = jnp.concatenate([r[...] for r in v_refs[b:b + 3]], axis=0)
            kt = k.T
            qt = (_rows_of(q_ref, b).astype(F32) * WIN_SCALE).T.astype(BF16)
            dot_ = _rows_of(do_ref, b).astype(F32).T.astype(BF16)
            dq_parts, dks, dvs = [], [], []
            for g in range(KV_B):
                kg, vg = k[:, HD_B * g:HD_B * (g + 1)], v[:, HD_B * g:HD_B * (g + 1)]
                q4t, do4t = _group_t(qt, g), _group_t(dot_, g)
                p, p_sink = _win_probs(kg, q4t, bias_ref, sink_ref, g, edge_ok)
                dp = _dot(vg, do4t, NN)
                delta = jnp.sum(p * dp, axis=0, keepdims=True)
                ds = p * (dp - delta)
                d_bias[g] = ds if b == 0 else d_bias[g] + ds
                d_sink[g] = -p_sink * delta if b == 0 else d_sink[g] - p_sink * delta
                dsb = ds.astype(BF16)
                dq4t = _dot(kt[HD_B * g:HD_B * (g + 1), :], dsb, NN) * WIN_SCALE
                dq_parts += [dq4t[:, Q_BLOCK * j:Q_BLOCK * (j + 1)] for j in range(GROUP)]
                dks.append(_dot(dsb, q4t, NT))
                dvs.append(_dot(p.astype(BF16), do4t, NT))
            dq_ref[Q_BLOCK * b:Q_BLOCK * (b + 1), :] = jnp.concatenate(dq_parts, axis=0).T.astype(BF16)
            dk_blocks.append(jnp.concatenate(dks, axis=1))
            dv_blocks.append(jnp.concatenate(dvs, axis=1))

        for g in range(KV_B):
            for j in range(GROUP):
                dbias_ref[GROUP * g + j] += d_bias[g][:, Q_BLOCK * j:Q_BLOCK * (j + 1)]
            dsink_acc[g:g + 1, :] += d_sink[g]

        def overlap(blocks):
            out = blocks[0]
            for blk in blocks[1:]:
                keep = out.shape[0] - 2 * Q_BLOCK
                out = jnp.concatenate([out[:keep], out[keep:] + blk[:2 * Q_BLOCK], blk[2 * Q_BLOCK:]], axis=0)
            return out

        rows = pl.ds(pl.multiple_of(i * (B * Q_BLOCK), B * Q_BLOCK), (B + 2) * Q_BLOCK)
        dk_ref[rows, :] += overlap(dk_blocks)
        dv_ref[rows, :] += overlap(dv_blocks)

        @pl.when(i == n_blk // B - 1)
        def _():
            acc = dsink_acc[...]
            dsink_ref[...] = jnp.concatenate(
                [jnp.sum(acc[:, Q_BLOCK * j:Q_BLOCK * (j + 1)], axis=1, keepdims=True) for j in range(GROUP)], axis=1)

    whole = lambda shape: pl.BlockSpec(shape, lambda i: (0,) * len(shape))
    return _pcall(
        body, name="win_bwd", grid=(n_blk // B,),
        in_specs=[qspec, *kspecs[0], *kspecs[1], bias_spec, sink_spec, pl.BlockSpec((B * Q_BLOCK, H_B * HD_B), lambda i: (i, 0)), ANY],
        out_specs=[qspec, whole((S + 2 * WINDOW, KV_B * HD_B)),
                   whole((S + 2 * WINDOW, KV_B * HD_B)), whole((H_B, SPAN, Q_BLOCK)), whole((KV_B, GROUP))],
        out_shape=[jax.ShapeDtypeStruct((S, PROJ_P), BF16), jax.ShapeDtypeStruct((S + 2 * WINDOW, KV_B * HD_B), F32),
                   jax.ShapeDtypeStruct((S + 2 * WINDOW, KV_B * HD_B), F32), jax.ShapeDtypeStruct((H_B, SPAN, Q_BLOCK), F32),
                   jax.ShapeDtypeStruct((KV_B, GROUP), F32)],
        scratch_shapes=[pltpu.VMEM((KV_B, GROUP * Q_BLOCK), F32)],
        dims=("arbitrary",), aliases={2 * B + 8: 0})(*[proj] * (2 * B + 5), bias_t, sinks_b, do_b, d_proj)


def _bias_table(rel_bias_t, onehot_t, in_band):
    def body(rb_ref, oh_ref, band_ref, o_ref):
        t = lax.dot_general(rb_ref[...], oh_ref[...], NN, preferred_element_type=F32, precision=lax.Precision.HIGHEST)
        o_ref[...] = jnp.where(band_ref[...] > 0.5, t, -1e30)

    n = onehot_t.shape[1]
    tn = _tile(n, 8192)
    return _pcall(body, name="bias_table", grid=(n // tn,),
                  in_specs=[pl.BlockSpec((H_B, NUM_BUCKETS), lambda j: (0, 0)), pl.BlockSpec((NUM_BUCKETS, tn), lambda j: (0, j)),
                            pl.BlockSpec((1, tn), lambda j: (0, j))],
                  out_specs=pl.BlockSpec((H_B, tn), lambda j: (0, j)),
                  out_shape=jax.ShapeDtypeStruct((H_B, n), F32), dims=("parallel",))(rel_bias_t, onehot_t, in_band)


def _bias_table_bwd(dbias, onehot_t):
    n = onehot_t.shape[1]
    tk = _tile(n, 8192)

    def body(d_ref, oh_ref, o_ref):
        @pl.when(pl.program_id(0) == 0)
        def _():
            o_ref[...] = jnp.zeros_like(o_ref)

        o_ref[...] += lax.dot_general(d_ref[...], oh_ref[...], NT, preferred_element_type=F32, precision=lax.Precision.HIGHEST)

    return _pcall(body, name="bias_table_bwd", grid=(n // tk,),
                  in_specs=[pl.BlockSpec((H_B, tk), lambda j: (0, j)), pl.BlockSpec((NUM_BUCKETS, tk), lambda j: (0, j))],
                  out_specs=pl.BlockSpec((H_B, NUM_BUCKETS), lambda j: (0, 0)),
                  out_shape=jax.ShapeDtypeStruct((H_B, NUM_BUCKETS), F32), dims=("arbitrary",))(dbias, onehot_t)


CONV_STRIP = 128
N_STRIPS = D_FF // CONV_STRIP
CONV_ROWS = 128
HALO = 8


def _strip(rows, half):
    return pl.BlockSpec((rows, CONV_STRIP), lambda j: (0, j + half * N_STRIPS))


def _fill_padded(pad_ref, src_ref, halo, S):
    pad_ref[0:halo, :] = jnp.zeros((halo, CONV_STRIP), F32)
    pad_ref[halo + S:2 * halo + S, :] = jnp.zeros((halo, CONV_STRIP), F32)
    pad_ref[halo:halo + S, :] = src_ref[...].astype(F32)


def _conv_gate_fwd(u, conv_w, conv_b, S):
    R = min(CONV_ROWS, S)

    def body(ug_ref, uv_ref, wg_ref, wv_ref, bg_ref, bv_ref, a_ref, gpad, vpad):
        _fill_padded(gpad, ug_ref, HALO, S)
        _fill_padded(vpad, uv_ref, HALO, S)
        wg, wv, bg, bv = wg_ref[...], wv_ref[...], bg_ref[...], bv_ref[...]

        def conv(pad_ref, r0, w, b):
            dn, mid, up = (pad_ref[pl.ds(r0 + HALO + d, R), :] for d in (-1, 0, 1))
            return dn * w[0:1, :] + mid * w[1:2, :] + up * w[2:3, :] + b

        def step(c, carry):
            r0 = pl.multiple_of(c * R, R)
            g = conv(gpad, r0, wg, bg)
            val = conv(vpad, r0, wv, bv)
            a_ref[pl.ds(r0, R), :] = (g * _sigmoid(g) * val).astype(BF16)
            return carry

        lax.fori_loop(0, S // R, step, 0)

    return _pcall(body, name="conv_gate_fwd", grid=(N_STRIPS,),
                  in_specs=[_strip(S, 0), _strip(S, 1), _strip(3, 0), _strip(3, 1), _strip(1, 0), _strip(1, 1)],
                  out_specs=_strip(S, 0), out_shape=jax.ShapeDtypeStruct((S, D_FF), BF16),
                  scratch_shapes=[pltpu.VMEM((S + 2 * HALO, CONV_STRIP), F32)] * 2,
                  dims=("parallel",))(u, u, conv_w, conv_w, conv_b, conv_b)


def _conv_gate_bwd(u, conv_w, conv_b, da, S):
    R = min(CONV_ROWS, S)
    n = R + 2 * HALO

    def body(ug_ref, uv_ref, wg_ref, wv_ref, bg_ref, bv_ref, da_ref, dug_ref, duv_ref, dwg_ref, dwv_ref, dbg_ref, dbv_ref,
             gpad, vpad, dapad):
        _fill_padded(gpad, ug_ref, 2 * HALO, S)
        _fill_padded(vpad, uv_ref, 2 * HALO, S)
        _fill_padded(dapad, da_ref, HALO, S)
        wg, wv, bg, bv = wg_ref[...], wv_ref[...], bg_ref[...], bv_ref[...]

        def conv(pad_ref, r0, w, b):
            dn, mid, up = (pad_ref[pl.ds(r0 + HALO + d, n), :] for d in (-1, 0, 1))
            return dn * w[0:1, :] + mid * w[1:2, :] + up * w[2:3, :] + b, mid[HALO:HALO + R]

        def conv_bwd(duc, u_mid, w, r0, du_ref):
            dn, mid, up = pltpu.roll(duc, 1, axis=0)[HALO:HALO + R], duc[HALO:HALO + R], pltpu.roll(duc, n - 1, axis=0)[HALO:HALO + R]
            du_ref[pl.ds(r0, R), :] = (up * w[0:1, :] + mid * w[1:2, :] + dn * w[2:3, :]).astype(BF16)
            dw = jnp.concatenate([jnp.sum(up * u_mid, axis=0, keepdims=True), jnp.sum(mid * u_mid, axis=0, keepdims=True),
                                  jnp.sum(dn * u_mid, axis=0, keepdims=True)], axis=0)
            return dw, jnp.sum(mid, axis=0, keepdims=True)

        def step(c, carry):
            dw_g, db_g, dw_v, db_v = carry
            r0 = pl.multiple_of(c * R, R)
            g, ug_mid = conv(gpad, r0, wg, bg)
            val, uv_mid = conv(vpad, r0, wv, bv)
            da_ext = dapad[pl.ds(r0, n), :]
            sg = _sigmoid(g)
            ddw_v, ddb_v = conv_bwd(da_ext * (g * sg), uv_mid, wv, r0, duv_ref)
            ddw_g, ddb_g = conv_bwd(da_ext * val * (sg * (1.0 + g * (1.0 - sg))), ug_mid, wg, r0, dug_ref)
            return dw_g + ddw_g, db_g + ddb_g, dw_v + ddw_v, db_v + ddb_v

        z3, z1 = jnp.zeros((3, CONV_STRIP), F32), jnp.zeros((1, CONV_STRIP), F32)
        dwg_ref[...], dbg_ref[...], dwv_ref[...], dbv_ref[...] = lax.fori_loop(0, S // R, step, (z3, z1, z3, z1))

    half = lambda r, dt: (_strip(r, 0), jax.ShapeDtypeStruct((r, D_FF), dt))
    outs = [half(S, BF16), half(S, BF16), half(3, F32), half(3, F32), half(1, F32), half(1, F32)]
    return _pcall(
        body, name="conv_gate_bwd", grid=(N_STRIPS,),
        in_specs=[_strip(S, 0), _strip(S, 1), _strip(3, 0), _strip(3, 1), _strip(1, 0), _strip(1, 1), _strip(S, 0)],
        out_specs=[o[0] for o in outs], out_shape=[o[1] for o in outs],
        scratch_shapes=[pltpu.VMEM((S + 4 * HALO, CONV_STRIP), F32)] * 2 + [pltpu.VMEM((S + 2 * HALO, CONV_STRIP), F32)],
        dims=("parallel",))(u, u, conv_w, conv_w, conv_b, conv_b, da)


MESH = pl.DeviceIdType.MESH
ANY = pl.BlockSpec(memory_space=pl.ANY)


def _place():
    return lax.axis_index("x"), lax.axis_index("y"), lax.axis_index("c")


def _gather2(ins, outs, sems, finish):
    send_sems, recv_sems, local_sems = sems
    n_arr = len(ins)
    x, y, c = _place()
    me, sibling = (x, y, c), (x, y, 1 - c)
    chips = [(1 - x, y), (x, 1 - y), (1 - x, 1 - y)]

    def slot(a, p):
        return outs[a].at[4 * p[0] + 2 * p[1] + p[2]]

    def copy(a, k, block, to, src=None):
        return pltpu.make_async_remote_copy(
            src_ref=slot(a, block) if src is None else src, dst_ref=slot(a, block),
            send_sem=send_sems.at[a, k], recv_sem=recv_sems.at[a, k], device_id=to, device_id_type=MESH)

    mine = [pltpu.make_async_copy(ins[a], slot(a, me), local_sems.at[a]) for a in range(n_arr)]
    first = []
    for a in range(n_arr):
        first.append(copy(a, 0, me, sibling, src=ins[a]))
        first += [copy(a, 1 + j, me, (*chip, c), src=ins[a]) for j, chip in enumerate(chips)]
    if not finish:
        for cp in mine + first:
            cp.start()
        return
    passed = []
    for j, chip in enumerate(chips):
        for a in range(n_arr):
            copy(a, 1 + j, (*chip, c), me).wait_recv()
            cp = copy(a, 4 + j, (*chip, c), sibling)
            cp.start()
            passed.append(cp)
    for a in range(n_arr):
        copy(a, 0, sibling, me).wait_recv()
        for j, chip in enumerate(chips):
            copy(a, 4 + j, (*chip, 1 - c), me).wait_recv()
    for cp in first + passed:
        cp.wait_send()
    for cp in mine:
        cp.wait()


def _xchg_out_shapes(stacked, replicated):
    return ([jax.ShapeDtypeStruct(s.shape, s.dtype) for s in stacked]
            + [jax.ShapeDtypeStruct((N_DEV, *r.shape), r.dtype) for r in replicated])


def _xchg_sems(n_arr):
    return [pltpu.SemaphoreType.DMA((n_arr, 7)), pltpu.SemaphoreType.DMA((n_arr, 7)), pltpu.SemaphoreType.DMA((n_arr,))]


def _xchg_copies(ins, outs, sems, n_st, with_recv):
    send_sems, recv_sems, local_sems = sems
    n_arr = len(ins)
    x, y, c = _place()
    me = 4 * x + 2 * y + c

    def src(a, idx):
        return ins[a].at[idx] if a < n_st else ins[a]

    mine = [pltpu.make_async_copy(src(a, me), outs[a].at[me], local_sems.at[a]) for a in range(n_arr)]
    pairs = []
    for k in range(1, N_DEV):
        px, py, pc = x ^ (k >> 2), y ^ ((k >> 1) & 1), c ^ (k & 1)
        peer = 4 * px + 2 * py + pc
        for a in range(n_arr):
            sems_k = dict(send_sem=send_sems.at[a, k - 1], recv_sem=recv_sems.at[a, k - 1], device_id_type=MESH)
            send = pltpu.make_async_remote_copy(src_ref=src(a, peer), dst_ref=outs[a].at[me], device_id=(px, py, pc), **sems_k)
            recv = None
            if with_recv:
                recv = pltpu.make_async_remote_copy(src_ref=src(a, peer), dst_ref=outs[a].at[peer], device_id=(x, y, c), **sems_k)
            pairs.append((send, recv))
    return mine, pairs


def _xchg_start(ins, outs, sems, n_st):
    mine, pairs = _xchg_copies(ins, outs, sems, n_st, False)
    for cp in mine:
        cp.start()
    for send, _ in pairs:
        send.start()


def _xchg_finish(ins, outs, sems, n_st):
    mine, pairs = _xchg_copies(ins, outs, sems, n_st, True)
    for _, recv in pairs:
        recv.wait_recv()
    for send, _ in pairs:
        send.wait_send()
    for cp in mine:
        cp.wait()


def _exchange(stacked, replicated, name):
    _, landed = _pcall(lambda: None, name=name, grid=(), in_specs=[], out_specs=[], out_shape=[], comm=(stacked, replicated))()
    return landed


HBM = pl.BlockSpec(memory_space=pltpu.HBM)
SEMS = pl.BlockSpec(memory_space=pltpu.SEMAPHORE)
SIDE_EFFECT = pltpu.SideEffectType.DATAFLOW_SIDE_EFFECTING


N_SPLIT_SEMS = 2 * (N_DEV - 1)


def _split_copies(src, land, sems, with_recv):
    x, y, c = _place()
    me = 4 * x + 2 * y + c
    pairs = []
    for k in range(1, N_DEV):
        px, py, pc = x ^ (k >> 2), y ^ ((k >> 1) & 1), c ^ (k & 1)
        peer = 4 * px + 2 * py + pc
        sems_k = dict(send_sem=sems[k - 1], recv_sem=sems[N_DEV - 1 + k - 1], device_id_type=MESH)
        send = pltpu.make_async_remote_copy(src_ref=src.at[peer], dst_ref=land.at[me], device_id=(px, py, pc), **sems_k)
        recv = None
        if with_recv:
            recv = pltpu.make_async_remote_copy(src_ref=src.at[peer], dst_ref=land.at[peer], device_id=(x, y, c), **sems_k)
        pairs.append((send, recv))
    return pairs


def _exchange_start(stacked, name):
    def body(src, land, *rest):
        for send, _ in _split_copies(src, land, rest[:N_SPLIT_SEMS], False):
            send.start()
        rest[-1][...] = jnp.zeros_like(rest[-1])

    shape = pltpu.HBM(stacked.shape, stacked.dtype)
    res = pl.pallas_call(
        body, name=name, in_specs=[HBM, HBM],
        out_shape=(*[pltpu.SemaphoreType.DMA(())] * N_SPLIT_SEMS, shape, shape, jax.ShapeDtypeStruct((8, 128), F32)),
        out_specs=(*[SEMS] * N_SPLIT_SEMS, HBM, HBM, pl.BlockSpec(memory_space=pltpu.VMEM)),
        input_output_aliases={0: N_SPLIT_SEMS, 1: N_SPLIT_SEMS + 1},
        compiler_params=pltpu.CompilerParams(has_side_effects=SIDE_EFFECT),
    )(pltpu.with_memory_space_constraint(stacked, pltpu.HBM),
      pltpu.with_memory_space_constraint(lax.empty(stacked.shape, stacked.dtype), pltpu.HBM))
    return res[:N_SPLIT_SEMS], res[N_SPLIT_SEMS], res[N_SPLIT_SEMS + 1], res[-1]


def _exchange_wait(sems, src, land, after, name):
    def body(src_ref, land_ref, *rest):
        for send, recv in _split_copies(src_ref, land_ref, rest[:N_SPLIT_SEMS], True):
            send.wait_send()
            recv.wait_recv()

    shape = pltpu.HBM(src.shape, src.dtype)
    return pl.pallas_call(
        body, name=name, in_specs=[HBM, HBM, *[SEMS] * N_SPLIT_SEMS, ANY],
        out_shape=(shape, shape), out_specs=(HBM, HBM), input_output_aliases={0: 0, 1: 1},
        compiler_params=pltpu.CompilerParams(has_side_effects=SIDE_EFFECT))(src, land, *sems, after)[1]


def _adamw(parts, w, m, v, name):
    _, R, C = w.shape
    tr = R if R <= 512 else max(t for t in range(16, 513, 16) if R % t == 0)
    pr = tr if parts.shape[1] == R else -(-R // 16) * 16
    assert pr == tr or tr == R

    def body(p_ref, w_ref, m_ref, v_ref, g_ref, d_ref, nm_ref, nv_ref):
        g = p_ref[0].astype(F32)[:tr]
        for s in range(1, N_DEV):
            g = g + p_ref[s].astype(F32)[:tr]
        m2 = ADAM_B1 * m_ref[0] + (1.0 - ADAM_B1) * g
        v2 = ADAM_B2 * v_ref[0] + (1.0 - ADAM_B2) * (g * g)
        m_hat = m2 / (1.0 - ADAM_B1 ** ADAM_STEP)
        v_hat = v2 / (1.0 - ADAM_B2 ** ADAM_STEP)
        g_ref[0] = g
        d_ref[0] = -ADAM_LR * (m_hat / (jnp.sqrt(v_hat) + ADAM_EPS) + ADAM_WD * w_ref[0])
        nm_ref[0] = m2
        nv_ref[0] = v2

    blk = pl.BlockSpec((1, tr, C), lambda i: (0, i, 0))
    return _pcall(body, name=name, grid=(R // tr,),
                  in_specs=[pl.BlockSpec((N_DEV, pr, C), lambda i: (0, i, 0)), blk, blk, blk],
                  out_specs=[blk] * 4, out_shape=[jax.ShapeDtypeStruct((1, R, C), F32)] * 4,
                  dims=("parallel",))(parts, w, m, v)


def _t5_bucket(rel):
    nb = NUM_BUCKETS // 2
    max_exact = nb // 2
    base = (rel > 0).astype(jnp.int32) * nb
    n = jnp.abs(rel)
    nf = jnp.maximum(n, 1).astype(jnp.float32)
    large = max_exact + (jnp.log(nf / max_exact) / math.log(MAX_DISTANCE / max_exact) * (nb - max_exact)).astype(jnp.int32)
    large = jnp.minimum(large, nb - 1)
    return base + jnp.where(n < max_exact, n, large)


def _unstack_cols(g):
    return jnp.transpose(g, (1, 0, 2)).reshape(g.shape[1], N_DEV * g.shape[2])


def _stack_cols(w, n=N_DEV):
    R = w.shape[0]
    return jnp.transpose(w.reshape(R, n, w.shape[1] // n), (1, 0, 2))


def _stack_halves(g, v):
    return jnp.concatenate([_stack_cols(g, N_DEV // 2), _stack_cols(v, N_DEV // 2)], axis=0)


def kernel(x, positions, norm1_g, w_in, q_a_norm_g, w_q_b, kv_a_norm_g, w_kv_b, rel_bias, sinks, w_out, norm2_g, w_up, conv_w, conv_b, w_down, final_norm_g, loss_target, m_norm1_g, m_w_in, m_q_a_norm_g, m_w_q_b, m_kv_a_norm_g, m_w_kv_b, m_rel_bias, m_sinks, m_w_out, m_norm2_g, m_w_up, m_conv_w, m_conv_b, m_w_down, m_final_norm_g, v_norm1_g, v_w_in, v_q_a_norm_g, v_w_q_b, v_kv_a_norm_g, v_w_kv_b, v_rel_bias, v_sinks, v_w_out, v_norm2_g, v_w_up, v_conv_w, v_conv_b, v_w_down, v_final_norm_g):
    S = x.shape[1]
    x = x[0]
    target = loss_target[0]
    TM = 256

    tr = lambda w: jnp.swapaxes(w, 1, 2)
    (h1,), (g_in, g_qb, g_kvb) = _rowwise(lambda a, g: (_rms(a, g),), "norm1_gather", S, TM, [_rows(x), _whole(norm1_g)], [("rows", D_MODEL, BF16)],
                                          gather=[tr(w_in)[0].astype(BF16), tr(w_q_b)[0].astype(BF16), w_kv_b[0].astype(BF16)])
    late_weights = [w_out[0].astype(BF16), tr(w_up)[0].astype(BF16), conv_w[0]]
    wi = g_in.reshape(W_IN_COLS, D_MODEL)
    c0, c1, c2, c3, c4, c5 = (sum(W_IN_SIZES[:i + 1]) for i in range(6))
    w_in_pt = jnp.concatenate([wi[c4:c5], wi[c5:], wi[c1:c2], wi[:c0], wi[c2:c3], wi[c3:c4],
                               wi[c0:c0 + KV_LORA], wi[c0 + KV_LORA:c1], jnp.zeros((64, D_MODEL), BF16)], axis=0)
    wq = g_qb.reshape(H_A, QK_HEAD, Q_LORA)
    w_qb_pt = jnp.concatenate([wq[:, :QK_NOPE].reshape(H_A * QK_NOPE, Q_LORA), wq[:, QK_NOPE:].reshape(H_A * QK_ROPE, Q_LORA)], axis=0)
    w_kvb = _unstack_cols(g_kvb)

    half = QK_ROPE // 2
    inv_freq = ROPE_THETA ** (-jnp.arange(half, dtype=F32) / half)
    ang = positions.astype(F32)[:, None] * inv_freq[None, :]
    cos, sin = jnp.cos(ang), jnp.sin(ang)
    qa = jnp.arange(Q_BLOCK, dtype=jnp.int32)[:, None]
    kc = jnp.arange(SPAN, dtype=jnp.int32)[None, :]
    rel = (kc - WINDOW - qa).T
    in_band = (jnp.abs(rel) <= WINDOW).astype(F32).reshape(1, Q_BLOCK * SPAN)
    onehot_t = (_t5_bucket(rel).reshape(1, Q_BLOCK * SPAN) == jnp.arange(NUM_BUCKETS, dtype=jnp.int32)[:, None]).astype(F32)
    bias_t = _bias_table(rel_bias.T, onehot_t, in_band).reshape(H_B, SPAN, Q_BLOCK)
    sinks_b = jnp.broadcast_to(sinks.reshape(H_B, 1), (H_B, Q_BLOCK))

    proj =_matmul(h1, w_in_pt, "nt", BF16, "proj")

    def lat_fn(qlat, ckv, kr, gq, gkv, cs, sn):
        r1, r2 = _rope(kr[:, :half], kr[:, half:QK_ROPE], cs, sn)
        return _rms(qlat, gq), _rms(ckv, gkv), jnp.concatenate([r1, r2], axis=1)

    qn, ckvn, k_rope = _rowwise(lat_fn, "latents", S, TM,
                                [_rows(proj, 256, PROJ_QLAT), _rows(proj, 128, PROJ_CKV), _rows(proj, 128, PROJ_KROPE),
                                 _whole(q_a_norm_g), _whole(kv_a_norm_g), _rows(cos), _rows(sin)],
                                [("rows", Q_LORA, BF16), ("rows", KV_LORA, BF16), ("rows", QK_ROPE, BF16)])
    def q_heads_fn(q, cs, sn):
        q = q * MLA_PRESCALE
        outs = []
        for h in range(H_A):
            o = H_A * QK_NOPE + QK_ROPE * h
            r1, r2 = _rope(q[:, o:o + half], q[:, o + half:o + QK_ROPE], cs, sn)
            outs.append(jnp.concatenate([q[:, QK_NOPE * h:QK_NOPE * (h + 1)], r1, r2], axis=1)[None])
        return (jnp.concatenate(outs, axis=0),)

    (q_full,) = _matmul(qn, w_qb_pt, "nt", None, "q_up_heads", tm=512, tn=1536,
                        epi=(q_heads_fn, [_rows(cos), _rows(sin)], [("heads", H_A, QK_HEAD, BF16)]))

    def k_heads_fn(kvf, kr):
        return kvf, jnp.concatenate([jnp.concatenate([kvf[:, 256 * h:256 * h + QK_NOPE], kr], axis=1)[None] for h in range(H_A)], axis=0)

    kv, k_full = _matmul(ckvn, w_kvb, "nn", None, "kv_up_heads", tm=512, tn=2048,
                         epi=(k_heads_fn, [_rows(k_rope)], [("rows", H_A * (QK_NOPE + V_DIM), BF16), ("heads", H_A, QK_HEAD, BF16)]))
    (o_a, lse), (g_out, g_up, g_cw) = _mla_fwd(q_full, k_full, kv, S, comm=([], late_weights))
    w_out_f = g_out.reshape(D_MODEL, D_MODEL)
    w_up_t = g_up.reshape(2 * D_FF, D_MODEL)
    conv_w_f = _unstack_cols(g_cw)

    o_b, (g_down,) = _win_fwd(proj, bias_t, sinks_b, S, comm=([], [w_down[0].astype(BF16)]))
    w_down_f = g_down.reshape(D_FF, D_MODEL)

    (mixed,) = _rowwise(lambda ga, gb, oa, ob: (_sigmoid(ga) * oa + _sigmoid(gb) * ob,), "gate_mix", S, TM,
                        [_rows(proj, 1024, PROJ_GA), _rows(proj, 1024, PROJ_GB), _rows(o_a), _rows(o_b)], [("rows", D_MODEL, BF16)])
    x1, h2 = _matmul(mixed, w_out_f, "nn", None, "out_proj", residual=x, tm=512,
                     epi=(lambda a, g: (a, _rms(a, g)), [_whole(norm2_g)], [("rows", D_MODEL, F32), ("rows", D_MODEL, BF16)]))
    u = _matmul(h2, w_up_t, "nt", BF16, "ffn_up", tn=1408)
    act = _conv_gate_fwd(u, conv_w_f, conv_b, S)

    def final_fn(a, g, t):
        err = _rms(a, g) - t
        loss = 0.5 * jnp.sum(jnp.mean(err * err, axis=-1, keepdims=True), axis=0, keepdims=True)
        dx, dg = _rms_bwd(err * (1.0 / D_MODEL), a, g)
        return dx, dx, dg, jnp.broadcast_to(loss, (1, 128))

    gfin = final_norm_g.reshape(1, D_MODEL)
    dx2, dx2_b, d_gfin, loss_row = _matmul(
        act, w_down_f, "nn", None, "ffn_down_loss", residual=x1, tm=512,
        epi=(final_fn, [_whole(gfin), _rows(target)],
             [("rows", D_MODEL, F32), ("rows", D_MODEL, BF16), ("acc", 1, D_MODEL), ("acc", 1, 128)]))
    d_act = _matmul(dx2_b, w_down_f, "nt", BF16, "ffn_down_dx", tn=1408)
    d_w_down = _matmul(act, dx2_b, "tn", BF16, "ffn_down_dw")
    du_g, du_v, dcw_g, dcw_v, dcb_g, dcb_v = _conv_gate_bwd(u, conv_w_f, conv_b, d_act, S)
    d_conv_b = jnp.concatenate([dcb_g, dcb_v], axis=1)

    def norm_bwd_fn(dh, a, g, dres):
        dx, dg = _rms_bwd(dh, a, g)
        dx = dx + dres
        return dx, dx, dg

    dx1, dx1_b, d_g2 = _matmul(du_g, w_up_t, "nn", None, "ffn_up_dx_norm2_bwd", tm=256, a2=du_v,
                               epi=(norm_bwd_fn, [_rows(x1), _whole(norm2_g), _rows(dx2)],
                                    [("rows", D_MODEL, F32), ("rows", D_MODEL, BF16), ("acc", 1, D_MODEL)]))
    d_w_up_t = _matmul(du_g, h2, "tn", BF16, "ffn_up_dw", tm=256, a2=du_v)
    d_w_out = _matmul(mixed, dx1_b, "tn", BF16, "out_proj_dw", tm=512)

    def gate_bwd_fn(dm, ga, gb, oa, ob):
        sa, sb = _sigmoid(ga), _sigmoid(gb)
        return jnp.concatenate([dm * oa * sa * (1.0 - sa), dm * ob * sb * (1.0 - sb)], axis=1), dm * sa, dm * sb

    d_proj, do_a, do_b = _matmul(
        dx1_b, w_out_f, "nt", None, "out_proj_dx_gate_bwd", tm=512,
        epi=(gate_bwd_fn, [_rows(proj, 1024, PROJ_GA), _rows(proj, 1024, PROJ_GB), _rows(o_a), _rows(o_b)],
             [("cols", 2 * D_MODEL, 0, PROJ_P, BF16), ("rows", D_MODEL, BF16), ("rows", D_MODEL, BF16)]))

    d_proj, dk_acc, dv_acc, d_bias, d_sinks_g = _win_bwd(proj, bias_t, sinks_b, do_b, d_proj, S)
    d_sinks = d_sinks_g.reshape(1, H_B)

    early = [d_w_out.reshape(N_DEV, D_MODEL // N_DEV, D_MODEL), d_w_up_t.reshape(N_DEV, 2 * D_FF // N_DEV, D_MODEL),
             d_w_down.reshape(N_DEV, D_FF // N_DEV, D_MODEL), _stack_halves(dcw_g, dcw_v)]
    (dq_full, dk_full, dv_full), recv_early = _mla_bwd(q_full, k_full, kv, do_a, o_a, lse, S, comm=(early, []))

    def dq_post_fn(dq, cs, sn):
        nope = [dq[h, :, :QK_NOPE] for h in range(H_A)]
        rope = []
        for h in range(H_A):
            rope += list(_rope_bwd(dq[h, :, QK_NOPE:QK_NOPE + half], dq[h, :, QK_NOPE + half:], cs, sn))
        return (jnp.concatenate(nope + rope, axis=1),)

    (dq_p,) = _rowwise(dq_post_fn, "dq_post", S, TM, [_heads(dq_full), _rows(cos), _rows(sin)], [("rows", H_A * QK_HEAD, BF16)])

    def dkv_post_fn(dk, dv, cs, sn):
        dk = dk * math.log(2.0)
        dkv = jnp.concatenate([jnp.concatenate([dk[h, :, :QK_NOPE], dv[h]], axis=1) for h in range(H_A)], axis=1)
        dkr = dk[0, :, QK_NOPE:]
        for h in range(1, H_A):
            dkr = dkr + dk[h, :, QK_NOPE:]
        r1, r2 = _rope_bwd(dkr[:, :half], dkr[:, half:], cs, sn)
        return dkv, jnp.concatenate([r1, r2], axis=1)

    dkv, d_krope = _rowwise(dkv_post_fn, "dkv_post", S, TM, [_heads(dk_full), _heads(dv_full), _rows(cos), _rows(sin)],
                            [("rows", H_A * (QK_NOPE + V_DIM), BF16), ("rows", QK_ROPE, F32)])
    d_qn = _matmul(dq_p, w_qb_pt, "nn", F32, "q_up_dx")
    d_w_qb_pt = _matmul(dq_p, qn, "tn", BF16, "q_up_dw", tm=512)
    d_ckvn = _matmul(dkv, w_kvb, "nt", F32, "kv_up_dx")
    d_w_kvb = _matmul(ckvn, dkv, "tn", BF16, "kv_up_dw", tn=2048)

    def lat_bwd_fn(dqn, dckvn, dkr, qlat, ckv, gq, gkv, dkb, dvb):
        dql, dgq = _rms_bwd(dqn, qlat, gq)
        dck, dgkv = _rms_bwd(dckvn, ckv, gkv)
        tail = jnp.concatenate([dql, dkb, dvb, dck, dkr, jnp.zeros_like(dkr)], axis=1)
        return tail, dgq, dgkv

    shifted = lambda arr: (arr, lambda tm: pl.BlockSpec((tm, arr.shape[1]), lambda i, *_: (i + WINDOW // tm, 0)))
    TL = min(128, S)
    d_proj, d_gq, d_gkv = _rowwise(lat_bwd_fn, "latents_bwd", S, TL,
                                   [_rows(d_qn), _rows(d_ckvn), _rows(d_krope), _rows(proj, 256, PROJ_QLAT), _rows(proj, 128, PROJ_CKV),
                                    _whole(q_a_norm_g), _whole(kv_a_norm_g), shifted(dk_acc), shifted(dv_acc)],
                                   [("cols", 1024, 3, PROJ_P, BF16), ("acc", 1, Q_LORA), ("acc", 1, KV_LORA)], into=(d_proj, 0))
    d_w_qb_t = jnp.concatenate([d_w_qb_pt[:H_A * QK_NOPE].reshape(H_A, QK_NOPE, Q_LORA),
                                d_w_qb_pt[H_A * QK_NOPE:].reshape(H_A, QK_ROPE, Q_LORA)], axis=1)
    dp, recv_mid = _matmul(d_proj, h1, "tn", BF16, "proj_dw", tm=512, comm=([d_w_qb_t, _stack_cols(d_w_kvb)], []))

    late = jnp.concatenate([dp[3072:3328], dp[3840:3968], dp[3968:4032], dp[2048:3072], dp[3328:3584],
                            dp[3584:3840], dp[0:1024], dp[1024:2048]], axis=0).reshape(N_DEV, W_IN_COLS // N_DEV, D_MODEL)
    late_sems, late_src, late_land, started = _exchange_start(late, "late_grads_start")

    def norm1_bwd_fn(dh, a, g, dres):
        dx, dg = _rms_bwd(dh, a, g)
        return dx + dres, dg

    grad_x, d_g1 = _matmul(
        d_proj, w_in_pt, "nn", None, "proj_dx_norm1_bwd", tm=512,
        epi=(norm1_bwd_fn, [_rows(x), _whole(norm1_g + started[:1, :1]), _rows(dx1)], [("rows", D_MODEL, F32), ("acc", 1, D_MODEL)]))

    transposed = ("w_in", "w_q_b", "w_up")
    ready_names = ["w_out", "w_up", "w_down", "conv_w", "w_q_b", "w_kv_b"]
    ready_wmv = [(w_out, m_w_out, v_w_out), (tr(w_up), tr(m_w_up), tr(v_w_up)), (w_down, m_w_down, v_w_down), (conv_w, m_conv_w, v_conv_w),
                 (tr(w_q_b), tr(m_w_q_b), tr(v_w_q_b)), (w_kv_b, m_w_kv_b, v_w_kv_b)]
    big = {n: _adamw(r, *wmv, "adamw_" + n) for n, r, wmv in zip(ready_names, [*recv_early, *recv_mid], ready_wmv)}

    d_bias, _ = lax.optimization_barrier((d_bias, started))
    d_rel_bias = _bias_table_bwd(d_bias.reshape(H_B, Q_BLOCK * SPAN), onehot_t).T

    after = lax.optimization_barrier([big[n][0] for n in ready_names] + [d_rel_bias])
    landed = _exchange_wait(late_sems, late_src, late_land, after[-1], "late_grads_wait")
    me = 4 * lax.axis_index("x") + 2 * lax.axis_index("y") + lax.axis_index("c")
    landed = lax.dynamic_update_slice_in_dim(landed, lax.dynamic_slice_in_dim(late, me, 1, axis=0), me, axis=0)
    big["w_in"] = _adamw(landed, tr(w_in), tr(m_w_in), tr(v_w_in), "adamw_w_in")

    small_parts = [d_g1, d_gq, d_gkv, d_rel_bias.reshape(1, NUM_BUCKETS * H_B), d_sinks, d_g2, d_conv_b, d_gfin, loss_row[:, :1]]
    small = jnp.concatenate(small_parts, axis=1)
    n_small = small.shape[1]
    pad = (-n_small) % 128
    small = jnp.pad(small, ((0, 0), (0, pad)))
    small, _ = lax.optimization_barrier((small, [landed, *after]))
    (recv_small,) = _exchange([], [small], "exchange_small_grads")

    def flat(a):
        return a.reshape(1, -1)

    small_w = [norm1_g, q_a_norm_g, kv_a_norm_g, rel_bias, sinks, norm2_g, conv_b, final_norm_g]
    small_m = [m_norm1_g, m_q_a_norm_g, m_kv_a_norm_g, m_rel_bias, m_sinks, m_norm2_g, m_conv_b, m_final_norm_g]
    small_v = [v_norm1_g, v_q_a_norm_g, v_kv_a_norm_g, v_rel_bias, v_sinks, v_norm2_g, v_conv_b, v_final_norm_g]
    cat = lambda parts: jnp.pad(jnp.concatenate([flat(a) for a in parts], axis=1), ((0, 0), (0, pad + 1)))[None]
    sm = _adamw(recv_small, cat(small_w), cat(small_m), cat(small_v), "adamw_small")

    loss = sm[0][0, 0, n_small - 1]
    order =["norm1_g", "w_in", "q_a_norm_g", "w_q_b", "kv_a_norm_g", "w_kv_b", "rel_bias", "sinks", "w_out", "norm2_g", "w_up",
             "conv_w", "conv_b", "w_down", "final_norm_g"]
    small_names = ["norm1_g", "q_a_norm_g", "kv_a_norm_g", "rel_bias", "sinks", "norm2_g", "conv_b", "final_norm_g"]
    offs, o = {}, 0
    for n, a in zip(small_names, small_w):
        offs[n] = (o, a.size, a.shape)
        o += a.size
    outs = [loss, grad_x[None]]
    for kind in range(4):
        for n in order:
            if n in big:
                outs.append(tr(big[n][kind]) if n in transposed else big[n][kind])
            else:
                o, size, shape = offs[n]
                outs.append(sm[kind][0, 0, o:o + size].reshape(shape))
    return tuple(outs)
```

```python
import math

import jax
import jax.numpy as jnp
from jax import lax
from jax.experimental import pallas as pl
from jax.experimental.pallas import tpu as pltpu

F32 = jnp.float32
BF16 = jnp.bfloat16

N_DEV = 8
D_MODEL = 1024
EPS = 1e-6
H_A, QK_NOPE, QK_ROPE, V_DIM, Q_LORA, KV_LORA = 8, 128, 64, 128, 256, 128
QK_HEAD = QK_NOPE + QK_ROPE
ROPE_THETA = 10000.0
H_B, KV_B, GROUP, HD_B, WINDOW, Q_BLOCK = 16, 4, 4, 64, 128, 128
SPAN = Q_BLOCK + 2 * WINDOW
NUM_BUCKETS, MAX_DISTANCE = 32, 128
D_FF = 2816
ADAM_LR, ADAM_B1, ADAM_B2, ADAM_EPS, ADAM_WD, ADAM_STEP = 0.001, 0.9, 0.999, 1e-08, 0.01, 10

W_IN_SIZES = (Q_LORA, KV_LORA + QK_ROPE, H_B * HD_B, KV_B * HD_B, KV_B * HD_B, D_MODEL, D_MODEL)
W_IN_COLS = sum(W_IN_SIZES)
PROJ_P = 4096
PROJ_GA, PROJ_GB, PROJ_QB, PROJ_QLAT, PROJ_KB, PROJ_VB, PROJ_CKV, PROJ_KROPE = 0, 1, 2, 12, 13, 14, 30, 31

VMEM_LIMIT = 56 * 1024 * 1024

NN = (((1,), (0,)), ((), ()))
NT = (((1,), (1,)), ((), ()))
TN = (((0,), (0,)), ((), ()))


def _pcall(body, *, name, grid, in_specs, out_specs, out_shape, scratch_shapes=(), dims=None, comm=None, aliases=None, two_level=False):
    if comm is None:
        params = pltpu.CompilerParams(dimension_semantics=dims, vmem_limit_bytes=VMEM_LIMIT)
        return pl.pallas_call(body, name=name, grid=grid, in_specs=in_specs, out_specs=out_specs, out_shape=out_shape,
                              scratch_shapes=list(scratch_shapes), input_output_aliases=aliases or {}, compiler_params=params)
    assert not aliases
    stacked, replicated = comm
    arrs = [*stacked, *replicated]
    n_st, n_arr = len(stacked), len(arrs)
    single = not isinstance(out_specs, (list, tuple))
    o_specs, o_shape = ([out_specs], [out_shape]) if single else (list(out_specs), list(out_shape))
    n_in, n_out = len(in_specs), len(o_specs)

    def wrapped(*refs):
        c_in = refs[n_in:n_in + n_arr]
        c_out = refs[n_in + n_arr + n_out:n_in + 2 * n_arr + n_out]
        sems = refs[len(refs) - 3:]
        own = (*refs[:n_in], *refs[n_in + n_arr:n_in + n_arr + n_out], *refs[n_in + 2 * n_arr + n_out:len(refs) - 3])
        if two_level:
            assert n_st == 0
            start, finish = (lambda: _gather2(c_in, c_out, sems, False)), (lambda: _gather2(c_in, c_out, sems, True))
        else:
            start, finish = (lambda: _xchg_start(c_in, c_out, sems, n_st)), (lambda: _xchg_finish(c_in, c_out, sems, n_st))
        if not grid:
            start()
            finish()
            return
        first = last = None
        for d, n in enumerate(grid):
            pid = pl.program_id(d)
            first = (pid == 0) if first is None else first & (pid == 0)
            last = (pid == n - 1) if last is None else last & (pid == n - 1)

        pl.when(first)(start)
        body(*own)
        pl.when(last)(finish)

    params = pltpu.CompilerParams(dimension_semantics=("arbitrary",) * len(grid), vmem_limit_bytes=VMEM_LIMIT)
    call = pl.pallas_call(wrapped, name=name, grid=grid, in_specs=[*in_specs, *[ANY] * n_arr], out_specs=[*o_specs, *[ANY] * n_arr],
                          out_shape=[*o_shape, *_xchg_out_shapes(stacked, replicated)],
                          scratch_shapes=[*scratch_shapes, *_xchg_sems(n_arr)], compiler_params=params)

    def run(*args):
        res = call(*args, *arrs)
        outs, landed = res[:n_out], res[n_out:]
        return (outs[0] if single else outs), landed

    return run


def _dot(a, b, dn):
    return lax.dot_general(a, b, dn, preferred_element_type=F32)


def _tile(n, target):
    best = None
    for t in range(128, min(n, target) + 1, 128):
        if n % t == 0:
            best = t
    return n if best is None else best


def _matmul(a, b, mode, out_dtype, name, residual=None, tm=1024, tn=1024, comm=None, a2=None, epi=None):
    if mode == "nn":
        (M, K), N = a.shape, b.shape[1]
    elif mode == "nt":
        (M, K), N = a.shape, b.shape[0]
    else:
        (K, M), N = a.shape, b.shape[1]
    tm, tn = _tile(M, tm), _tile(N, tn)
    a_spec = pl.BlockSpec((K, tm), lambda i, j: (0, i)) if mode == "tn" else pl.BlockSpec((tm, K), lambda i, j: (i, 0))
    b_spec = pl.BlockSpec((tn, b.shape[1]), lambda i, j: (j, 0)) if mode == "nt" else pl.BlockSpec((K, tn), lambda i, j: (0, j))
    o_spec = pl.BlockSpec((tm, tn), lambda i, j: (i, j))
    in_specs, args = [a_spec, b_spec], [a, b]
    n1 = M // tm
    if a2 is not None and mode == "tn":
        assert M % tm == 0 and a2.shape[1] % tm == 0
        in_specs[0] = pl.BlockSpec((K, tm), lambda i, j: (0, jnp.minimum(i, n1 - 1)))
        in_specs.append(pl.BlockSpec((K, tm), lambda i, j: (0, jnp.maximum(i - n1, 0))))
        args.append(a2)
        M += a2.shape[1]
    elif a2 is not None:
        assert (mode == "nt" and K + a2.shape[1] == b.shape[1]) or (mode == "nn" and K + a2.shape[1] == b.shape[0])
        if mode == "nn":
            b_spec = in_specs[1] = pl.BlockSpec((b.shape[0], tn), lambda i, j: (0, j))
        in_specs.append(pl.BlockSpec((tm, a2.shape[1]), lambda i, j: (i, 0)))
        args.append(a2)
    if residual is not None:
        in_specs.append(o_spec)
        args.append(residual)
    n_mm = len(args)
    scratch = [pltpu.VMEM((tm, K), a.dtype)] if mode == "tn" else []
    if epi is None:
        out_specs, out_shape, is_acc = o_spec, jax.ShapeDtypeStruct((M, N), out_dtype), None
    else:
        assert tn == N
        fn, epi_ins, epi_outs = epi
        in_specs += [mk(tm) for _, mk in epi_ins]
        args += [arr for arr, _ in epi_ins]
        out_specs, out_shape, is_acc = _row_out_specs(epi_outs, M, tm)

    def body(*refs):
        a_ref, b_ref = refs[0], refs[1]
        n_out = 1 if epi is None else len(is_acc)
        out_refs = refs[len(args):len(args) + n_out]
        if mode == "tn":
            at_ref = refs[len(args) + n_out]

            first_col = pl.program_id(1) == 0
            from_a = first_col if a2 is None else first_col & (pl.program_id(0) < n1)

            @pl.when(from_a)
            def _():
                at_ref[...] = a_ref[...].T

            if a2 is not None:
                @pl.when(first_col & (pl.program_id(0) >= n1))
                def _():
                    at_ref[...] = refs[2][...].T

            acc = _dot(at_ref[...], b_ref[...], NN)
        elif a2 is not None and mode == "nt":
            acc = _dot(a_ref[...], b_ref[:, :K], NT) + _dot(refs[2][...], b_ref[:, K:], NT)
        elif a2 is not None:
            acc = _dot(a_ref[...], b_ref[:K, :], NN) + _dot(refs[2][...], b_ref[K:, :], NN)
        else:
            acc = _dot(a_ref[...], b_ref[...], NT if mode == "nt" else NN)
        if residual is not None:
            acc = acc + refs[n_mm - 1][...]
        if epi is None:
            out_refs[0][...] = acc.astype(out_dtype)
        else:
            _store_rows(out_refs, fn(acc, *[_load_f32(r) for r in refs[n_mm:len(args)]]), is_acc)

    return _pcall(body, name=name, grid=(M // tm, N // tn), in_specs=in_specs, out_specs=out_specs,
                  out_shape=out_shape, scratch_shapes=scratch,
                  dims=("arbitrary" if epi is not None else "parallel", "arbitrary"), comm=comm)(*args)


def _rows(arr, width=None, col=0):
    width = arr.shape[1] if width is None else width
    return (arr, lambda tm: pl.BlockSpec((tm, width), lambda i, *_: (i, col)))


def _heads(arr):
    return (arr, lambda tm: pl.BlockSpec((arr.shape[0], tm, arr.shape[2]), lambda i, *_: (0, i, 0)))


def _whole(arr):
    nd = arr.ndim
    return (arr, lambda tm: pl.BlockSpec(arr.shape, lambda i, *_: (0,) * nd))


def _row_out_specs(outs, n_rows, tm):
    out_specs, out_shape, is_acc = [], [], []
    for o in outs:
        if o[0] == "rows":
            out_specs.append(pl.BlockSpec((tm, o[1]), lambda i, *_: (i, 0)))
            out_shape.append(jax.ShapeDtypeStruct((n_rows, o[1]), o[2]))
        elif o[0] == "cols":
            out_specs.append(pl.BlockSpec((tm, o[1]), lambda i, *_, c=o[2]: (i, c)))
            out_shape.append(jax.ShapeDtypeStruct((n_rows, o[3]), o[4]))
        elif o[0] == "heads":
            out_specs.append(pl.BlockSpec((o[1], tm, o[2]), lambda i, *_: (0, i, 0)))
            out_shape.append(jax.ShapeDtypeStruct((o[1], n_rows, o[2]), o[3]))
        else:
            out_specs.append(pl.BlockSpec((o[1], o[2]), lambda i, *_: (0, 0)))
            out_shape.append(jax.ShapeDtypeStruct((o[1], o[2]), F32))
        is_acc.append(o[0] == "acc")
    return out_specs, out_shape, is_acc


def _load_f32(r):
    v = r[...]
    return v.astype(F32) if v.dtype == BF16 else v


def _store_rows(out_refs, vals, is_acc):
    for r, v, acc in zip(out_refs, vals, is_acc):
        if acc:
            @pl.when(pl.program_id(0) == 0)
            def _():
                r[...] = jnp.zeros_like(r)

            r[...] += v
        else:
            r[...] = v.astype(r.dtype)


def _rowwise(fn, name, n_rows, tm, ins, outs, upcast=True, into=None, gather=None):
    tm = min(tm, n_rows)
    assert n_rows % tm == 0
    in_specs = [mk(tm) for _, mk in ins]
    out_specs, out_shape, is_acc = _row_out_specs(outs, n_rows, tm)
    n_in = len(ins)
    args = [a for a, _ in ins]
    aliases = {}
    if into is not None:
        in_specs.append(ANY)
        args.append(into[0])
        aliases = {n_in: into[1]}

    def body(*refs):
        vals = fn(*[_load_f32(r) if upcast else r[...] for r in refs[:n_in]])
        _store_rows(refs[len(args):], vals, is_acc)

    return _pcall(body, name=name, grid=(n_rows // tm,), in_specs=in_specs, out_specs=out_specs,
                  out_shape=out_shape, dims=("arbitrary",), aliases=aliases,
                  comm=None if gather is None else ([], gather), two_level=True)(*args)


def _rms(x, g):
    r = lax.rsqrt(jnp.mean(x * x, axis=-1, keepdims=True) + EPS)
    return x * r * g


def _rms_bwd(dy, x, g):
    r = lax.rsqrt(jnp.mean(x * x, axis=-1, keepdims=True) + EPS)
    xhat = x * r
    dxhat = dy * g
    dx = r * (dxhat - xhat * jnp.mean(dxhat * xhat, axis=-1, keepdims=True))
    return dx, jnp.sum(dy * xhat, axis=0, keepdims=True)


def _rope(x1, x2, cos, sin):
    return x1 * cos - x2 * sin, x2 * cos + x1 * sin


def _rope_bwd(d1, d2, cos, sin):
    return d1 * cos + d2 * sin, d2 * cos - d1 * sin


def _sigmoid(x):
    return 1.0 / (1.0 + jnp.exp(-x))


MLA_SCALE = 1.0 / math.sqrt(QK_HEAD)
MLA_PRESCALE = MLA_SCALE * math.log2(math.e)
MLA_TQ, MLA_KC = 1024, 1024


def _mla_fwd(q_full, k_full, kv, S, comm=None):
    tq, kc = min(MLA_TQ, S), min(MLA_KC, S)

    def body(q_ref, k_ref, v_ref, o_ref, lse_ref):
        q = q_ref[0]
        m = jnp.full((tq, 1), -1e30, F32)
        l = jnp.zeros((tq, 1), F32)
        acc = jnp.zeros((tq, V_DIM), F32)
        for c in range(S // kc):
            s = _dot(q, k_ref[0, c * kc:(c + 1) * kc, :], NT)
            m_new = jnp.maximum(m, jnp.max(s, axis=-1, keepdims=True))
            alpha = jnp.exp2(m - m_new)
            p = jnp.exp2(s - m_new)
            l = alpha * l + jnp.sum(p, axis=-1, keepdims=True)
            acc = alpha * acc + _dot(p.astype(BF16), v_ref[c * kc:(c + 1) * kc, :], NN)
            m = m_new
        o_ref[...] = (acc / l).astype(BF16)
        lse_ref[0] = m + jnp.log2(l)

    return _pcall(
        body, name="mla_fwd", grid=(H_A, S // tq),
        in_specs=[pl.BlockSpec((1, tq, QK_HEAD), lambda h, i: (h, i, 0)),
                  pl.BlockSpec((1, S, QK_HEAD), lambda h, i: (h, 0, 0)),
                  pl.BlockSpec((S, V_DIM), lambda h, i: (0, 2 * h + 1))],
        out_specs=[pl.BlockSpec((tq, V_DIM), lambda h, i: (i, h)),
                   pl.BlockSpec((1, tq, 1), lambda h, i: (h, i, 0))],
        out_shape=[jax.ShapeDtypeStruct((S, H_A * V_DIM), BF16), jax.ShapeDtypeStruct((H_A, S, 1), F32)],
        dims=("parallel", "parallel"), comm=comm)(q_full, k_full, kv)


def _mla_bwd(q_full, k_full, kv, do_a, o_a, lse, S, comm=None):
    tq, kc = min(MLA_TQ, S), min(MLA_KC, S)

    def body(q_ref, k_ref, v_ref, do_ref, o_ref, lse_ref, dq_ref, dk_out, dv_out, dk_ref, dv_ref):
        @pl.when(pl.program_id(1) == 0)
        def _():
            dk_ref[...] = jnp.zeros_like(dk_ref)
            dv_ref[...] = jnp.zeros_like(dv_ref)

        q = q_ref[0]
        do = do_ref[...]
        lse_q = lse_ref[0]
        delta = jnp.sum(do.astype(F32) * o_ref[...].astype(F32), axis=-1, keepdims=True)
        dq = jnp.zeros((tq, QK_HEAD), F32)
        for c in range(S // kc):
            k = k_ref[0, c * kc:(c + 1) * kc, :]
            v = v_ref[c * kc:(c + 1) * kc, :]
            p = jnp.exp2(_dot(q, k, NT) - lse_q)
            ds = (p * (_dot(do, v, NT) - delta)).astype(BF16)
            dq = dq + _dot(ds, k, NN)
            dk_ref[0, c * kc:(c + 1) * kc, :] += _dot(ds, q, TN)
            dv_ref[0, c * kc:(c + 1) * kc, :] += _dot(p.astype(BF16), do, TN)
        dq_ref[0] = (dq * MLA_SCALE).astype(BF16)

        @pl.when(pl.program_id(1) == S // tq - 1)
        def _():
            dk_out[...] = (dk_ref[...] * math.log(2.0)).astype(BF16)
            dv_out[...] = dv_ref[...].astype(BF16)

    return _pcall(
        body, name="mla_bwd", grid=(H_A, S // tq),
        in_specs=[pl.BlockSpec((1, tq, QK_HEAD), lambda h, i: (h, i, 0)),
                  pl.BlockSpec((1, S, QK_HEAD), lambda h, i: (h, 0, 0)),
                  pl.BlockSpec((S, V_DIM), lambda h, i: (0, 2 * h + 1)),
                  pl.BlockSpec((tq, V_DIM), lambda h, i: (i, h)),
                  pl.BlockSpec((tq, V_DIM), lambda h, i: (i, h)),
                  pl.BlockSpec((1, tq, 1), lambda h, i: (h, i, 0))],
        out_specs=[pl.BlockSpec((1, tq, QK_HEAD), lambda h, i: (h, i, 0)),
                   pl.BlockSpec((1, S, QK_HEAD), lambda h, i: (h, 0, 0)),
                   pl.BlockSpec((1, S, V_DIM), lambda h, i: (h, 0, 0))],
        out_shape=[jax.ShapeDtypeStruct((H_A, S, QK_HEAD), BF16), jax.ShapeDtypeStruct((H_A, S, QK_HEAD), BF16),
                   jax.ShapeDtypeStruct((H_A, S, V_DIM), BF16)],
        scratch_shapes=[pltpu.VMEM((1, S, QK_HEAD), F32), pltpu.VMEM((1, S, V_DIM), F32)],
        dims=("parallel", "arbitrary"), comm=comm)(q_full, k_full, kv, do_a, o_a, lse)


WIN_SCALE = 1.0 / math.sqrt(HD_B)


WIN_PER_STEP = 4


def _win_specs(S):
    last, B = S // Q_BLOCK - 1, WIN_PER_STEP
    qspec = pl.BlockSpec((B * Q_BLOCK, H_B * HD_B), lambda i: (i, PROJ_QB))
    kspecs = [[pl.BlockSpec((Q_BLOCK, KV_B * HD_B), lambda i, d=d, c=c: (jnp.clip(B * i + d, 0, last), c)) for d in range(-1, B + 1)]
              for c in (PROJ_KB, PROJ_VB)]
    bias_spec = pl.BlockSpec((H_B, SPAN, Q_BLOCK), lambda i: (0, 0, 0))
    sink_spec = pl.BlockSpec((H_B, Q_BLOCK), lambda i: (0, 0))
    return qspec, kspecs, bias_spec, sink_spec


def _win_edge_ok(n, n_blk):
    row = lax.broadcasted_iota(jnp.int32, (SPAN, 1), 0)
    return jnp.logical_not(((n == 0) & (row < WINDOW)) | ((n == n_blk - 1) & (row >= SPAN - WINDOW)))


def _lanes4(pieces):
    return jnp.concatenate(pieces, axis=1)


def _win_probs(kg, q4t, bias_ref, sink_ref, g, edge_ok):
    bias4 = _lanes4([bias_ref[GROUP * g + j] for j in range(GROUP)])
    sink4 = _lanes4([sink_ref[GROUP * g + j:GROUP * g + j + 1, :] for j in range(GROUP)])
    s = jnp.where(edge_ok, _dot(kg, q4t, NN) + bias4, -1e30)
    m = jnp.maximum(jnp.max(s, axis=0, keepdims=True), sink4)
    p = jnp.exp(s - m)
    e_sink = jnp.exp(sink4 - m)
    inv_l = 1.0 / (jnp.sum(p, axis=0, keepdims=True) + e_sink)
    return p * inv_l, e_sink * inv_l


def _group_t(xt, g):
    return _lanes4([xt[HD_B * (GROUP * g + j):HD_B * (GROUP * g + j + 1), :] for j in range(GROUP)])


def _rows_of(ref, b):
    return ref[Q_BLOCK * b:Q_BLOCK * (b + 1), :]


def _win_fwd(proj, o_a, bias_t, sinks_b, S, comm=None):
    n_blk, B = S // Q_BLOCK, WIN_PER_STEP
    qspec, kspecs, bias_spec, sink_spec = _win_specs(S)
    rows = lambda col: pl.BlockSpec((B * Q_BLOCK, H_B * HD_B), lambda i: (i, col))

    def body(q_ref, *refs):
        k_refs, v_refs = refs[:B + 2], refs[B + 2:2 * B + 4]
        bias_ref, sink_ref, ga_ref, gb_ref, oa_ref, o_ref, mixed_ref = refs[2 * B + 4:]
        for b in range(B):
            edge_ok = _win_edge_ok(B * pl.program_id(0) + b, n_blk)
            k = jnp.concatenate([r[...] for r in k_refs[b:b + 3]], axis=0)
            vt = jnp.concatenate([r[...] for r in v_refs[b:b + 3]], axis=0).T
            qt = (_rows_of(q_ref, b).astype(F32) * WIN_SCALE).T.astype(BF16)
            parts = []
            for g in range(KV_B):
                p, _ = _win_probs(k[:, HD_B * g:HD_B * (g + 1)], _group_t(qt, g), bias_ref, sink_ref, g, edge_ok)
                o4t = _dot(vt[HD_B * g:HD_B * (g + 1), :], p.astype(BF16), NN)
                parts += [o4t[:, Q_BLOCK * j:Q_BLOCK * (j + 1)] for j in range(GROUP)]
            ob = jnp.concatenate(parts, axis=0).T
            o_ref[Q_BLOCK * b:Q_BLOCK * (b + 1), :] = ob.astype(BF16)
            ga, gb, oa = (_rows_of(r, b).astype(F32) for r in (ga_ref, gb_ref, oa_ref))
            mixed_ref[Q_BLOCK * b:Q_BLOCK * (b + 1), :] = (_sigmoid(ga) * oa + _sigmoid(gb) * ob).astype(BF16)

    return _pcall(body, name="win_fwd_mix", grid=(n_blk // B,),
                  in_specs=[qspec, *kspecs[0], *kspecs[1], bias_spec, sink_spec, rows(PROJ_GA), rows(PROJ_GB), rows(0)],
                  out_specs=[rows(0), rows(0)],
                  out_shape=[jax.ShapeDtypeStruct((S, H_B * HD_B), BF16)] * 2,
                  dims=("parallel",), comm=comm)(*[proj] * (2 * B + 5), bias_t, sinks_b, proj, proj, o_a)


def _win_bwd(proj, bias_t, sinks_b, do_b, d_proj, S):
    n_blk, B = S // Q_BLOCK, WIN_PER_STEP
    qspec, kspecs, bias_spec, sink_spec = _win_specs(S)

    def body(q_ref, *refs):
        k_refs, v_refs = refs[:B + 2], refs[B + 2:2 * B + 4]
        bias_ref, sink_ref, do_ref, _, dq_ref, dk_ref, dv_ref, dbias_ref, dsink_ref, dsink_acc = refs[2 * B + 4:]
        i = pl.program_id(0)

        @pl.when(i == 0)
        def _():
            dk_ref[...] = jnp.zeros_like(dk_ref)
            dv_ref[...] = jnp.zeros_like(dv_ref)
            dbias_ref[...] = jnp.zeros_like(dbias_ref)
            dsink_acc[...] = jnp.zeros_like(dsink_acc)

        d_bias, d_sink, dk_blocks, dv_blocks = {}, {}, [], []
        for b in range(B):
            edge_ok = _win_edge_ok(B * i + b, n_blk)
            k = jnp.concatenate([r[...] for r in k_refs[b:b + 3]], axis=0)
            v = jnp.concatenate([r[...] for r in v_refs[b:b + 3]], axis=0)
            kt = k.T
            qt = (_rows_of(q_ref, b).astype(F32) * WIN_SCALE).T.astype(BF16)
            dot_ = _rows_of(do_ref, b).astype(F32).T.astype(BF16)
            dq_parts, dks, dvs = [], [], []
            for g in range(KV_B):
                kg, vg = k[:, HD_B * g:HD_B * (g + 1)], v[:, HD_B * g:HD_B * (g + 1)]
                q4t, do4t = _group_t(qt, g), _group_t(dot_, g)
                p, p_sink = _win_probs(kg, q4t, bias_ref, sink_ref, g, edge_ok)
                dp = _dot(vg, do4t, NN)
                delta = jnp.sum(p * dp, axis=0, keepdims=True)
                ds = p * (dp - delta)
                d_bias[g] = ds if b == 0 else d_bias[g] + ds
                d_sink[g] = -p_sink * delta if b == 0 else d_sink[g] - p_sink * delta
                dsb = ds.astype(BF16)
                dq4t = _dot(kt[HD_B * g:HD_B * (g + 1), :], dsb, NN) * WIN_SCALE
                dq_parts += [dq4t[:, Q_BLOCK * j:Q_BLOCK * (j + 1)] for j in range(GROUP)]
                dks.append(_dot(dsb, q4t, NT))
                dvs.append(_dot(p.astype(BF16), do4t, NT))
            dq_ref[Q_BLOCK * b:Q_BLOCK * (b + 1), :] = jnp.concatenate(dq_parts, axis=0).T.astype(BF16)
            dk_blocks.append(jnp.concatenate(dks, axis=1))
            dv_blocks.append(jnp.concatenate(dvs, axis=1))

        for g in range(KV_B):
            for j in range(GROUP):
                dbias_ref[GROUP * g + j] += d_bias[g][:, Q_BLOCK * j:Q_BLOCK * (j + 1)]
            dsink_acc[g:g + 1, :] += d_sink[g]

        def overlap(blocks):
            out = blocks[0]
            for blk in blocks[1:]:
                keep = out.shape[0] - 2 * Q_BLOCK
                out = jnp.concatenate([out[:keep], out[keep:] + blk[:2 * Q_BLOCK], blk[2 * Q_BLOCK:]], axis=0)
            return out

        rows = pl.ds(pl.multiple_of(i * (B * Q_BLOCK), B * Q_BLOCK), (B + 2) * Q_BLOCK)
        dk_ref[rows, :] += overlap(dk_blocks)
        dv_ref[rows, :] += overlap(dv_blocks)

        @pl.when(i == n_blk // B - 1)
        def _():
            acc = dsink_acc[...]
            dsink_ref[...] = jnp.concatenate(
                [jnp.sum(acc[:, Q_BLOCK * j:Q_BLOCK * (j + 1)], axis=1, keepdims=True) for j in range(GROUP)], axis=1)

    whole = lambda shape: pl.BlockSpec(shape, lambda i: (0,) * len(shape))
    return _pcall(
        body, name="win_bwd", grid=(n_blk // B,),
        in_specs=[qspec, *kspecs[0], *kspecs[1], bias_spec, sink_spec, pl.BlockSpec((B * Q_BLOCK, H_B * HD_B), lambda i: (i, 0)), ANY],
        out_specs=[qspec, whole((S + 2 * WINDOW, KV_B * HD_B)),
                   whole((S + 2 * WINDOW, KV_B * HD_B)), whole((H_B, SPAN, Q_BLOCK)), whole((KV_B, GROUP))],
        out_shape=[jax.ShapeDtypeStruct((S, PROJ_P), BF16), jax.ShapeDtypeStruct((S + 2 * WINDOW, KV_B * HD_B), F32),
                   jax.ShapeDtypeStruct((S + 2 * WINDOW, KV_B * HD_B), F32), jax.ShapeDtypeStruct((H_B, SPAN, Q_BLOCK), F32),
                   jax.ShapeDtypeStruct((KV_B, GROUP), F32)],
        scratch_shapes=[pltpu.VMEM((KV_B, GROUP * Q_BLOCK), F32)],
        dims=("arbitrary",), aliases={2 * B + 8: 0})(*[proj] * (2 * B + 5), bias_t, sinks_b, do_b, d_proj)


def _bias_table(rel_bias_t, onehot_t, in_band):
    def body(rb_ref, oh_ref, band_ref, o_ref):
        t = lax.dot_general(rb_ref[...], oh_ref[...], NN, preferred_element_type=F32, precision=lax.Precision.HIGHEST)
        o_ref[...] = jnp.where(band_ref[...] > 0.5, t, -1e30)

    n = onehot_t.shape[1]
    tn = _tile(n, 8192)
    return _pcall(body, name="bias_table", grid=(n // tn,),
                  in_specs=[pl.BlockSpec((H_B, NUM_BUCKETS), lambda j: (0, 0)), pl.BlockSpec((NUM_BUCKETS, tn), lambda j: (0, j)),
                            pl.BlockSpec((1, tn), lambda j: (0, j))],
                  out_specs=pl.BlockSpec((H_B, tn), lambda j: (0, j)),
                  out_shape=jax.ShapeDtypeStruct((H_B, n), F32), dims=("parallel",))(rel_bias_t, onehot_t, in_band)


def _bias_table_bwd(dbias, onehot_t):
    n = onehot_t.shape[1]
    tk = _tile(n, 8192)

    def body(d_ref, oh_ref, o_ref):
        @pl.when(pl.program_id(0) == 0)
        def _():
            o_ref[...] = jnp.zeros_like(o_ref)

        o_ref[...] += lax.dot_general(d_ref[...], oh_ref[...], NT, preferred_element_type=F32, precision=lax.Precision.HIGHEST)

    return _pcall(body, name="bias_table_bwd", grid=(n // tk,),
                  in_specs=[pl.BlockSpec((H_B, tk), lambda j: (0, j)), pl.BlockSpec((NUM_BUCKETS, tk), lambda j: (0, j))],
                  out_specs=pl.BlockSpec((H_B, NUM_BUCKETS), lambda j: (0, 0)),
                  out_shape=jax.ShapeDtypeStruct((H_B, NUM_BUCKETS), F32), dims=("arbitrary",))(dbias, onehot_t)


CONV_STRIP = 128
N_STRIPS = D_FF // CONV_STRIP
CONV_ROWS = 128
HALO = 8


def _strip(rows, half):
    return pl.BlockSpec((rows, CONV_STRIP), lambda j: (0, j + half * N_STRIPS))


def _fill_padded(pad_ref, src_ref, halo, S):
    pad_ref[0:halo, :] = jnp.zeros((halo, CONV_STRIP), F32)
    pad_ref[halo + S:2 * halo + S, :] = jnp.zeros((halo, CONV_STRIP), F32)
    pad_ref[halo:halo + S, :] = src_ref[...].astype(F32)


def _conv_gate_fwd(u, conv_w, conv_b, S):
    R = min(CONV_ROWS, S)

    def body(ug_ref, uv_ref, wg_ref, wv_ref, bg_ref, bv_ref, a_ref, gpad, vpad):
        _fill_padded(gpad, ug_ref, HALO, S)
        _fill_padded(vpad, uv_ref, HALO, S)
        wg, wv, bg, bv = wg_ref[...], wv_ref[...], bg_ref[...], bv_ref[...]

        def conv(pad_ref, r0, w, b):
            dn, mid, up = (pad_ref[pl.ds(r0 + HALO + d, R), :] for d in (-1, 0, 1))
            return dn * w[0:1, :] + mid * w[1:2, :] + up * w[2:3, :] + b

        def step(c, carry):
            r0 = pl.multiple_of(c * R, R)
            g = conv(gpad, r0, wg, bg)
            val = conv(vpad, r0, wv, bv)
            a_ref[pl.ds(r0, R), :] = (g * _sigmoid(g) * val).astype(BF16)
            return carry

        lax.fori_loop(0, S // R, step, 0)

    return _pcall(body, name="conv_gate_fwd", grid=(N_STRIPS,),
                  in_specs=[_strip(S, 0), _strip(S, 1), _strip(3, 0), _strip(3, 1), _strip(1, 0), _strip(1, 1)],
                  out_specs=_strip(S, 0), out_shape=jax.ShapeDtypeStruct((S, D_FF), BF16),
                  scratch_shapes=[pltpu.VMEM((S + 2 * HALO, CONV_STRIP), F32)] * 2,
                  dims=("parallel",))(u, u, conv_w, conv_w, conv_b, conv_b)


def _conv_gate_bwd(u, conv_w, conv_b, da, S):
    R = min(CONV_ROWS, S)
    n = R + 2 * HALO

    def body(ug_ref, uv_ref, wg_ref, wv_ref, bg_ref, bv_ref, da_ref, dug_ref, duv_ref, dwg_ref, dwv_ref, dbg_ref, dbv_ref,
             gpad, vpad, dapad):
        _fill_padded(gpad, ug_ref, 2 * HALO, S)
        _fill_padded(vpad, uv_ref, 2 * HALO, S)
        _fill_padded(dapad, da_ref, HALO, S)
        wg, wv, bg, bv = wg_ref[...], wv_ref[...], bg_ref[...], bv_ref[...]

        def conv(pad_ref, r0, w, b):
            dn, mid, up = (pad_ref[pl.ds(r0 + HALO + d, n), :] for d in (-1, 0, 1))
            return dn * w[0:1, :] + mid * w[1:2, :] + up * w[2:3, :] + b, mid[HALO:HALO + R]

        def conv_bwd(duc, u_mid, w, r0, du_ref):
            dn, mid, up = pltpu.roll(duc, 1, axis=0)[HALO:HALO + R], duc[HALO:HALO + R], pltpu.roll(duc, n - 1, axis=0)[HALO:HALO + R]
            du_ref[pl.ds(r0, R), :] = (up * w[0:1, :] + mid * w[1:2, :] + dn * w[2:3, :]).astype(BF16)
            dw = jnp.concatenate([jnp.sum(up * u_mid, axis=0, keepdims=True), jnp.sum(mid * u_mid, axis=0, keepdims=True),
                                  jnp.sum(dn * u_mid, axis=0, keepdims=True)], axis=0)
            return dw, jnp.sum(mid, axis=0, keepdims=True)

        def step(c, carry):
            dw_g, db_g, dw_v, db_v = carry
            r0 = pl.multiple_of(c * R, R)
            g, ug_mid = conv(gpad, r0, wg, bg)
            val, uv_mid = conv(vpad, r0, wv, bv)
            da_ext = dapad[pl.ds(r0, n), :]
            sg = _sigmoid(g)
            ddw_v, ddb_v = conv_bwd(da_ext * (g * sg), uv_mid, wv, r0, duv_ref)
            ddw_g, ddb_g = conv_bwd(da_ext * val * (sg * (1.0 + g * (1.0 - sg))), ug_mid, wg, r0, dug_ref)
            return dw_g + ddw_g, db_g + ddb_g, dw_v + ddw_v, db_v + ddb_v

        z3, z1 = jnp.zeros((3, CONV_STRIP), F32), jnp.zeros((1, CONV_STRIP), F32)
        dwg_ref[...], dbg_ref[...], dwv_ref[...], dbv_ref[...] = lax.fori_loop(0, S // R, step, (z3, z1, z3, z1))

    half = lambda r, dt: (_strip(r, 0), jax.ShapeDtypeStruct((r, D_FF), dt))
    outs = [half(S, BF16), half(S, BF16), half(3, F32), half(3, F32), half(1, F32), half(1, F32)]
    return _pcall(
        body, name="conv_gate_bwd", grid=(N_STRIPS,),
        in_specs=[_strip(S, 0), _strip(S, 1), _strip(3, 0), _strip(3, 1), _strip(1, 0), _strip(1, 1), _strip(S, 0)],
        out_specs=[o[0] for o in outs], out_shape=[o[1] for o in outs],
        scratch_shapes=[pltpu.VMEM((S + 4 * HALO, CONV_STRIP), F32)] * 2 + [pltpu.VMEM((S + 2 * HALO, CONV_STRIP), F32)],
        dims=("parallel",))(u, u, conv_w, conv_w, conv_b, conv_b, da)


MESH = pl.DeviceIdType.MESH
ANY = pl.BlockSpec(memory_space=pl.ANY)


def _place():
    return lax.axis_index("x"), lax.axis_index("y"), lax.axis_index("c")


def _gather2(ins, outs, sems, finish):
    send_sems, recv_sems, local_sems = sems
    n_arr = len(ins)
    x, y, c = _place()
    me, sibling = (x, y, c), (x, y, 1 - c)
    chips = [(1 - x, y), (x, 1 - y), (1 - x, 1 - y)]

    def slot(a, p):
        return outs[a].at[4 * p[0] + 2 * p[1] + p[2]]

    def copy(a, k, block, to, src=None):
        return pltpu.make_async_remote_copy(
            src_ref=slot(a, block) if src is None else src, dst_ref=slot(a, block),
            send_sem=send_sems.at[a, k], recv_sem=recv_sems.at[a, k], device_id=to, device_id_type=MESH)

    mine = [pltpu.make_async_copy(ins[a], slot(a, me), local_sems.at[a]) for a in range(n_arr)]
    first = []
    for a in range(n_arr):
        first.append(copy(a, 0, me, sibling, src=ins[a]))
        first += [copy(a, 1 + j, me, (*chip, c), src=ins[a]) for j, chip in enumerate(chips)]
    if not finish:
        for cp in mine + first:
            cp.start()
        return
    passed = []
    for j, chip in enumerate(chips):
        for a in range(n_arr):
            copy(a, 1 + j, (*chip, c), me).wait_recv()
            cp = copy(a, 4 + j, (*chip, c), sibling)
            cp.start()
            passed.append(cp)
    for a in range(n_arr):
        copy(a, 0, sibling, me).wait_recv()
        for j, chip in enumerate(chips):
            copy(a, 4 + j, (*chip, 1 - c), me).wait_recv()
    for cp in first + passed:
        cp.wait_send()
    for cp in mine:
        cp.wait()


def _xchg_out_shapes(stacked, replicated):
    return ([jax.ShapeDtypeStruct(s.shape, s.dtype) for s in stacked]
            + [jax.ShapeDtypeStruct((N_DEV, *r.shape), r.dtype) for r in replicated])


def _xchg_sems(n_arr):
    return [pltpu.SemaphoreType.DMA((n_arr, 7)), pltpu.SemaphoreType.DMA((n_arr, 7)), pltpu.SemaphoreType.DMA((n_arr,))]


def _xchg_copies(ins, outs, sems, n_st, with_recv):
    send_sems, recv_sems, local_sems = sems
    n_arr = len(ins)
    x, y, c = _place()
    me = 4 * x + 2 * y + c

    def src(a, idx):
        return ins[a].at[idx] if a < n_st else ins[a]

    mine = [pltpu.make_async_copy(src(a, me), outs[a].at[me], local_sems.at[a]) for a in range(n_arr)]
    pairs = []
    for k in range(1, N_DEV):
        px, py, pc = x ^ (k >> 2), y ^ ((k >> 1) & 1), c ^ (k & 1)
        peer = 4 * px + 2 * py + pc
        for a in range(n_arr):
            sems_k = dict(send_sem=send_sems.at[a, k - 1], recv_sem=recv_sems.at[a, k - 1], device_id_type=MESH)
            send = pltpu.make_async_remote_copy(src_ref=src(a, peer), dst_ref=outs[a].at[me], device_id=(px, py, pc), **sems_k)
            recv = None
            if with_recv:
                recv = pltpu.make_async_remote_copy(src_ref=src(a, peer), dst_ref=outs[a].at[peer], device_id=(x, y, c), **sems_k)
            pairs.append((send, recv))
    return mine, pairs


def _xchg_start(ins, outs, sems, n_st):
    mine, pairs = _xchg_copies(ins, outs, sems, n_st, False)
    for cp in mine:
        cp.start()
    for send, _ in pairs:
        send.start()


def _xchg_finish(ins, outs, sems, n_st):
    mine, pairs = _xchg_copies(ins, outs, sems, n_st, True)
    for _, recv in pairs:
        recv.wait_recv()
    for send, _ in pairs:
        send.wait_send()
    for cp in mine:
        cp.wait()


def _exchange(stacked, replicated, name):
    _, landed = _pcall(lambda: None, name=name, grid=(), in_specs=[], out_specs=[], out_shape=[], comm=(stacked, replicated))()
    return landed


HBM = pl.BlockSpec(memory_space=pltpu.HBM)
SEMS = pl.BlockSpec(memory_space=pltpu.SEMAPHORE)
SIDE_EFFECT = pltpu.SideEffectType.DATAFLOW_SIDE_EFFECTING


N_SPLIT_SEMS = 2 * (N_DEV - 1)


def _split_copies(src, land, sems, with_recv):
    x, y, c = _place()
    me = 4 * x + 2 * y + c
    pairs = []
    for k in range(1, N_DEV):
        px, py, pc = x ^ (k >> 2), y ^ ((k >> 1) & 1), c ^ (k & 1)
        peer = 4 * px + 2 * py + pc
        sems_k = dict(send_sem=sems[k - 1], recv_sem=sems[N_DEV - 1 + k - 1], device_id_type=MESH)
        send = pltpu.make_async_remote_copy(src_ref=src.at[peer], dst_ref=land.at[me], device_id=(px, py, pc), **sems_k)
        recv = None
        if with_recv:
            recv = pltpu.make_async_remote_copy(src_ref=src.at[peer], dst_ref=land.at[peer], device_id=(x, y, c), **sems_k)
        pairs.append((send, recv))
    return pairs


def _exchange_start(stacked, name):
    def body(src, land, *rest):
        for send, _ in _split_copies(src, land, rest[:N_SPLIT_SEMS], False):
            send.start()
        rest[-1][...] = jnp.zeros_like(rest[-1])

    shape = pltpu.HBM(stacked.shape, stacked.dtype)
    res = pl.pallas_call(
        body, name=name, in_specs=[HBM, HBM],
        out_shape=(*[pltpu.SemaphoreType.DMA(())] * N_SPLIT_SEMS, shape, shape, jax.ShapeDtypeStruct((8, 128), F32)),
        out_specs=(*[SEMS] * N_SPLIT_SEMS, HBM, HBM, pl.BlockSpec(memory_space=pltpu.VMEM)),
        input_output_aliases={0: N_SPLIT_SEMS, 1: N_SPLIT_SEMS + 1},
        compiler_params=pltpu.CompilerParams(has_side_effects=SIDE_EFFECT),
    )(pltpu.with_memory_space_constraint(stacked, pltpu.HBM),
      pltpu.with_memory_space_constraint(lax.empty(stacked.shape, stacked.dtype), pltpu.HBM))
    return res[:N_SPLIT_SEMS], res[N_SPLIT_SEMS], res[N_SPLIT_SEMS + 1], res[-1]


def _exchange_wait(sems, src, land, after, name):
    def body(src_ref, land_ref, *rest):
        for send, recv in _split_copies(src_ref, land_ref, rest[:N_SPLIT_SEMS], True):
            send.wait_send()
            recv.wait_recv()

    shape = pltpu.HBM(src.shape, src.dtype)
    return pl.pallas_call(
        body, name=name, in_specs=[HBM, HBM, *[SEMS] * N_SPLIT_SEMS, ANY],
        out_shape=(shape, shape), out_specs=(HBM, HBM), input_output_aliases={0: 0, 1: 1},
        compiler_params=pltpu.CompilerParams(has_side_effects=SIDE_EFFECT))(src, land, *sems, after)[1]


def _adamw(parts, w, m, v, name):
    _, R, C = w.shape
    tr = R if R <= 512 else max(t for t in range(16, 513, 16) if R % t == 0)
    pr = tr if parts.shape[1] == R else -(-R // 16) * 16
    assert pr == tr or tr == R

    def body(p_ref, w_ref, m_ref, v_ref, g_ref, d_ref, nm_ref, nv_ref):
        g = p_ref[0].astype(F32)[:tr]
        for s in range(1, N_DEV):
            g = g + p_ref[s].astype(F32)[:tr]
        m2 = ADAM_B1 * m_ref[0] + (1.0 - ADAM_B1) * g
        v2 = ADAM_B2 * v_ref[0] + (1.0 - ADAM_B2) * (g * g)
        m_hat = m2 / (1.0 - ADAM_B1 ** ADAM_STEP)
        v_hat = v2 / (1.0 - ADAM_B2 ** ADAM_STEP)
        g_ref[0] = g
        d_ref[0] = -ADAM_LR * (m_hat / (jnp.sqrt(v_hat) + ADAM_EPS) + ADAM_WD * w_ref[0])
        nm_ref[0] = m2
        nv_ref[0] = v2

    blk = pl.BlockSpec((1, tr, C), lambda i: (0, i, 0))
    return _pcall(body, name=name, grid=(R // tr,),
                  in_specs=[pl.BlockSpec((N_DEV, pr, C), lambda i: (0, i, 0)), blk, blk, blk],
                  out_specs=[blk] * 4, out_shape=[jax.ShapeDtypeStruct((1, R, C), F32)] * 4,
                  dims=("parallel",))(parts, w, m, v)


def _t5_bucket(rel):
    nb = NUM_BUCKETS // 2
    max_exact = nb // 2
    base = (rel > 0).astype(jnp.int32) * nb
    n = jnp.abs(rel)
    nf = jnp.maximum(n, 1).astype(jnp.float32)
    large = max_exact + (jnp.log(nf / max_exact) / math.log(MAX_DISTANCE / max_exact) * (nb - max_exact)).astype(jnp.int32)
    large = jnp.minimum(large, nb - 1)
    return base + jnp.where(n < max_exact, n, large)


def _unstack_cols(g):
    return jnp.transpose(g, (1, 0, 2)).reshape(g.shape[1], N_DEV * g.shape[2])


def _stack_cols(w, n=N_DEV):
    R = w.shape[0]
    return jnp.transpose(w.reshape(R, n, w.shape[1] // n), (1, 0, 2))


def _stack_halves(g, v):
    return jnp.concatenate([_stack_cols(g, N_DEV // 2), _stack_cols(v, N_DEV // 2)], axis=0)


def kernel(x, positions, norm1_g, w_in, q_a_norm_g, w_q_b, kv_a_norm_g, w_kv_b, rel_bias, sinks, w_out, norm2_g, w_up, conv_w, conv_b, w_down, final_norm_g, loss_target, m_norm1_g, m_w_in, m_q_a_norm_g, m_w_q_b, m_kv_a_norm_g, m_w_kv_b, m_rel_bias, m_sinks, m_w_out, m_norm2_g, m_w_up, m_conv_w, m_conv_b, m_w_down, m_final_norm_g, v_norm1_g, v_w_in, v_q_a_norm_g, v_w_q_b, v_kv_a_norm_g, v_w_kv_b, v_rel_bias, v_sinks, v_w_out, v_norm2_g, v_w_up, v_conv_w, v_conv_b, v_w_down, v_final_norm_g):
    S = x.shape[1]
    x = x[0]
    target = loss_target[0]
    TM = 256

    tr = lambda w: jnp.swapaxes(w, 1, 2)
    (h1,), (g_in, g_qb, g_kvb) = _rowwise(lambda a, g: (_rms(a, g),), "norm1_gather", S, TM, [_rows(x), _whole(norm1_g)], [("rows", D_MODEL, BF16)],
                                          gather=[tr(w_in)[0].astype(BF16), tr(w_q_b)[0].astype(BF16), w_kv_b[0].astype(BF16)])
    late_weights = [w_out[0].astype(BF16), tr(w_up)[0].astype(BF16), conv_w[0]]
    wi = g_in.reshape(W_IN_COLS, D_MODEL)
    c0, c1, c2, c3, c4, c5 = (sum(W_IN_SIZES[:i + 1]) for i in range(6))
    w_in_pt = jnp.concatenate([wi[c4:c5], wi[c5:], wi[c1:c2], wi[:c0], wi[c2:c3], wi[c3:c4],
                               wi[c0:c0 + KV_LORA], wi[c0 + KV_LORA:c1], jnp.zeros((64, D_MODEL), BF16)], axis=0)
    wq = g_qb.reshape(H_A, QK_HEAD, Q_LORA)
    w_qb_pt = jnp.concatenate([wq[:, :QK_NOPE].reshape(H_A * QK_NOPE, Q_LORA), wq[:, QK_NOPE:].reshape(H_A * QK_ROPE, Q_LORA)], axis=0)
    w_kvb = _unstack_cols(g_kvb)

    half = QK_ROPE // 2
    inv_freq = ROPE_THETA ** (-jnp.arange(half, dtype=F32) / half)
    ang = positions.astype(F32)[:, None] * inv_freq[None, :]
    cos, sin = jnp.cos(ang), jnp.sin(ang)
    qa = jnp.arange(Q_BLOCK, dtype=jnp.int32)[:, None]
    kc = jnp.arange(SPAN, dtype=jnp.int32)[None, :]
    rel = (kc - WINDOW - qa).T
    in_band = (jnp.abs(rel) <= WINDOW).astype(F32).reshape(1, Q_BLOCK * SPAN)
    onehot_t = (_t5_bucket(rel).reshape(1, Q_BLOCK * SPAN) == jnp.arange(NUM_BUCKETS, dtype=jnp.int32)[:, None]).astype(F32)
    bias_t = _bias_table(rel_bias.T, onehot_t, in_band).reshape(H_B, SPAN, Q_BLOCK)
    sinks_b = jnp.broadcast_to(sinks.reshape(H_B, 1), (H_B, Q_BLOCK))

    proj =_matmul(h1, w_in_pt, "nt", BF16, "proj")

    def lat_fn(qlat, ckv, kr, gq, gkv, cs, sn):
        r1, r2 = _rope(kr[:, :half], kr[:, half:QK_ROPE], cs, sn)
        return _rms(qlat, gq), _rms(ckv, gkv), jnp.concatenate([r1, r2], axis=1)

    qn, ckvn, k_rope = _rowwise(lat_fn, "latents", S, TM,
                                [_rows(proj, 256, PROJ_QLAT), _rows(proj, 128, PROJ_CKV), _rows(proj, 128, PROJ_KROPE),
                                 _whole(q_a_norm_g), _whole(kv_a_norm_g), _rows(cos), _rows(sin)],
                                [("rows", Q_LORA, BF16), ("rows", KV_LORA, BF16), ("rows", QK_ROPE, BF16)])
    def q_heads_fn(q, cs, sn):
        q = q * MLA_PRESCALE
        outs = []
        for h in range(H_A):
            o = H_A * QK_NOPE + QK_ROPE * h
            r1, r2 = _rope(q[:, o:o + half], q[:, o + half:o + QK_ROPE], cs, sn)
            outs.append(jnp.concatenate([q[:, QK_NOPE * h:QK_NOPE * (h + 1)], r1, r2], axis=1)[None])
        return (jnp.concatenate(outs, axis=0),)

    (q_full,) = _matmul(qn, w_qb_pt, "nt", None, "q_up_heads", tm=512, tn=1536,
                        epi=(q_heads_fn, [_rows(cos), _rows(sin)], [("heads", H_A, QK_HEAD, BF16)]))

    def k_heads_fn(kvf, kr):
        return kvf, jnp.concatenate([jnp.concatenate([kvf[:, 256 * h:256 * h + QK_NOPE], kr], axis=1)[None] for h in range(H_A)], axis=0)

    kv, k_full = _matmul(ckvn, w_kvb, "nn", None, "kv_up_heads", tm=512, tn=2048,
                         epi=(k_heads_fn, [_rows(k_rope)], [("rows", H_A * (QK_NOPE + V_DIM), BF16), ("heads", H_A, QK_HEAD, BF16)]))
    (o_a, lse), (g_out, g_up, g_cw) = _mla_fwd(q_full, k_full, kv, S, comm=([], late_weights))
    w_out_f = g_out.reshape(D_MODEL, D_MODEL)
    w_up_t = g_up.reshape(2 * D_FF, D_MODEL)
    conv_w_f = _unstack_cols(g_cw)

    (o_b, mixed), (g_down,) = _win_fwd(proj, o_a, bias_t, sinks_b, S, comm=([], [w_down[0].astype(BF16)]))
    w_down_f = g_down.reshape(D_FF, D_MODEL)

    x1, h2 = _matmul(mixed, w_out_f, "nn", None, "out_proj", residual=x, tm=512,
                     epi=(lambda a, g: (a, _rms(a, g)), [_whole(norm2_g)], [("rows", D_MODEL, F32), ("rows", D_MODEL, BF16)]))
    u = _matmul(h2, w_up_t, "nt", BF16, "ffn_up", tn=1408)
    act = _conv_gate_fwd(u, conv_w_f, conv_b, S)

    def final_fn(a, g, t):
        err = _rms(a, g) - t
        loss = 0.5 * jnp.sum(jnp.mean(err * err, axis=-1, keepdims=True), axis=0, keepdims=True)
        dx, dg = _rms_bwd(err * (1.0 / D_MODEL), a, g)
        return dx, dx, dg, jnp.broadcast_to(loss, (1, 128))

    gfin = final_norm_g.reshape(1, D_MODEL)
    dx2, dx2_b, d_gfin, loss_row = _matmul(
        act, w_down_f, "nn", None, "ffn_down_loss", residual=x1, tm=512,
        epi=(final_fn, [_whole(gfin), _rows(target)],
             [("rows", D_MODEL, F32), ("rows", D_MODEL, BF16), ("acc", 1, D_MODEL), ("acc", 1, 128)]))
    d_act = _matmul(dx2_b, w_down_f, "nt", BF16, "ffn_down_dx", tn=1408)
    d_w_down = _matmul(act, dx2_b, "tn", BF16, "ffn_down_dw")
    du_g, du_v, dcw_g, dcw_v, dcb_g, dcb_v = _conv_gate_bwd(u, conv_w_f, conv_b, d_act, S)
    d_conv_b = jnp.concatenate([dcb_g, dcb_v], axis=1)

    def norm_bwd_fn(dh, a, g, dres):
        dx, dg = _rms_bwd(dh, a, g)
        dx = dx + dres
        return dx, dx, dg

    dx1, dx1_b, d_g2 = _matmul(du_g, w_up_t, "nn", None, "ffn_up_dx_norm2_bwd", tm=256, a2=du_v,
                               epi=(norm_bwd_fn, [_rows(x1), _whole(norm2_g), _rows(dx2)],
                                    [("rows", D_MODEL, F32), ("rows", D_MODEL, BF16), ("acc", 1, D_MODEL)]))
    d_w_up_t = _matmul(du_g, h2, "tn", BF16, "ffn_up_dw", tm=256, a2=du_v)
    d_w_out = _matmul(mixed, dx1_b, "tn", BF16, "out_proj_dw", tm=512)

    def gate_bwd_fn(dm, ga, gb, oa, ob):
        sa, sb = _sigmoid(ga), _sigmoid(gb)
        return jnp.concatenate([dm * oa * sa * (1.0 - sa), dm * ob * sb * (1.0 - sb)], axis=1), dm * sa, dm * sb

    d_proj, do_a, do_b = _matmul(
        dx1_b, w_out_f, "nt", None, "out_proj_dx_gate_bwd", tm=512,
        epi=(gate_bwd_fn, [_rows(proj, 1024, PROJ_GA), _rows(proj, 1024, PROJ_GB), _rows(o_a), _rows(o_b)],
             [("cols", 2 * D_MODEL, 0, PROJ_P, BF16), ("rows", D_MODEL, BF16), ("rows", D_MODEL, BF16)]))

    d_proj, dk_acc, dv_acc, d_bias, d_sinks_g = _win_bwd(proj, bias_t, sinks_b, do_b, d_proj, S)
    d_sinks = d_sinks_g.reshape(1, H_B)

    early = [d_w_out.reshape(N_DEV, D_MODEL // N_DEV, D_MODEL), d_w_up_t.reshape(N_DEV, 2 * D_FF // N_DEV, D_MODEL),
             d_w_down.reshape(N_DEV, D_FF // N_DEV, D_MODEL), _stack_halves(dcw_g, dcw_v)]
    (dq_full, dk_full, dv_full), recv_early = _mla_bwd(q_full, k_full, kv, do_a, o_a, lse, S, comm=(early, []))

    def dq_post_fn(dq, cs, sn):
        nope = [dq[h, :, :QK_NOPE] for h in range(H_A)]
        rope = []
        for h in range(H_A):
            rope += list(_rope_bwd(dq[h, :, QK_NOPE:QK_NOPE + half], dq[h, :, QK_NOPE + half:], cs, sn))
        return (jnp.concatenate(nope + rope, axis=1),)

    (dq_p,) = _rowwise(dq_post_fn, "dq_post", S, TM, [_heads(dq_full), _rows(cos), _rows(sin)], [("rows", H_A * QK_HEAD, BF16)])

    def dkv_post_fn(dk, dv, cs, sn):
        dkv = jnp.concatenate([jnp.concatenate([dk[h, :, :QK_NOPE], dv[h]], axis=1) for h in range(H_A)], axis=1)
        dkr = dk[0, :, QK_NOPE:]
        for h in range(1, H_A):
            dkr = dkr + dk[h, :, QK_NOPE:]
        r1, r2 = _rope_bwd(dkr[:, :half], dkr[:, half:], cs, sn)
        return dkv, jnp.concatenate([r1, r2], axis=1)

    dkv, d_krope = _rowwise(dkv_post_fn, "dkv_post", S, TM, [_heads(dk_full), _heads(dv_full), _rows(cos), _rows(sin)],
                            [("rows", H_A * (QK_NOPE + V_DIM), BF16), ("rows", QK_ROPE, F32)])
    d_qn = _matmul(dq_p, w_qb_pt, "nn", F32, "q_up_dx")
    d_w_qb_pt = _matmul(dq_p, qn, "tn", BF16, "q_up_dw", tm=512)
    d_ckvn = _matmul(dkv, w_kvb, "nt", F32, "kv_up_dx")
    d_w_kvb = _matmul(ckvn, dkv, "tn", BF16, "kv_up_dw", tn=2048)

    def lat_bwd_fn(dqn, dckvn, dkr, qlat, ckv, gq, gkv, dkb, dvb):
        dql, dgq = _rms_bwd(dqn, qlat, gq)
        dck, dgkv = _rms_bwd(dckvn, ckv, gkv)
        tail = jnp.concatenate([dql, dkb, dvb, dck, dkr, jnp.zeros_like(dkr)], axis=1)
        return tail, dgq, dgkv

    shifted = lambda arr: (arr, lambda tm: pl.BlockSpec((tm, arr.shape[1]), lambda i, *_: (i + WINDOW // tm, 0)))
    TL = min(128, S)
    d_proj, d_gq, d_gkv = _rowwise(lat_bwd_fn, "latents_bwd", S, TL,
                                   [_rows(d_qn), _rows(d_ckvn), _rows(d_krope), _rows(proj, 256, PROJ_QLAT), _rows(proj, 128, PROJ_CKV),
                                    _whole(q_a_norm_g), _whole(kv_a_norm_g), shifted(dk_acc), shifted(dv_acc)],
                                   [("cols", 1024, 3, PROJ_P, BF16), ("acc", 1, Q_LORA), ("acc", 1, KV_LORA)], into=(d_proj, 0))
    d_w_qb_t = jnp.concatenate([d_w_qb_pt[:H_A * QK_NOPE].reshape(H_A, QK_NOPE, Q_LORA),
                                d_w_qb_pt[H_A * QK_NOPE:].reshape(H_A, QK_ROPE, Q_LORA)], axis=1)
    dp, recv_mid = _matmul(d_proj, h1, "tn", BF16, "proj_dw", tm=512, comm=([d_w_qb_t, _stack_cols(d_w_kvb)], []))

    late = jnp.concatenate([dp[3072:3328], dp[3840:3968], dp[3968:4032], dp[2048:3072], dp[3328:3584],
                            dp[3584:3840], dp[0:1024], dp[1024:2048]], axis=0).reshape(N_DEV, W_IN_COLS // N_DEV, D_MODEL)
    late_sems, late_src, late_land, started = _exchange_start(late, "late_grads_start")

    def norm1_bwd_fn(dh, a, g, dres):
        dx, dg = _rms_bwd(dh, a, g)
        return dx + dres, dg

    grad_x, d_g1 = _matmul(
        d_proj, w_in_pt, "nn", None, "proj_dx_norm1_bwd", tm=512,
        epi=(norm1_bwd_fn, [_rows(x), _whole(norm1_g + started[:1, :1]), _rows(dx1)], [("rows", D_MODEL, F32), ("acc", 1, D_MODEL)]))

    transposed = ("w_in", "w_q_b", "w_up")
    ready_names = ["w_out", "w_up", "w_down", "conv_w", "w_q_b", "w_kv_b"]
    ready_wmv = [(w_out, m_w_out, v_w_out), (tr(w_up), tr(m_w_up), tr(v_w_up)), (w_down, m_w_down, v_w_down), (conv_w, m_conv_w, v_conv_w),
                 (tr(w_q_b), tr(m_w_q_b), tr(v_w_q_b)), (w_kv_b, m_w_kv_b, v_w_kv_b)]
    big = {n: _adamw(r, *wmv, "adamw_" + n) for n, r, wmv in zip(ready_names, [*recv_early, *recv_mid], ready_wmv)}

    d_bias, _ = lax.optimization_barrier((d_bias, started))
    d_rel_bias = _bias_table_bwd(d_bias.reshape(H_B, Q_BLOCK * SPAN), onehot_t).T

    after = lax.optimization_barrier([big[n][0] for n in ready_names] + [d_rel_bias])
    landed = _exchange_wait(late_sems, late_src, late_land, after[-1], "late_grads_wait")
    me = 4 * lax.axis_index("x") + 2 * lax.axis_index("y") + lax.axis_index("c")
    landed = lax.dynamic_update_slice_in_dim(landed, lax.dynamic_slice_in_dim(late, me, 1, axis=0), me, axis=0)
    big["w_in"] = _adamw(landed, tr(w_in), tr(m_w_in), tr(v_w_in), "adamw_w_in")

    small_parts = [d_g1, d_gq, d_gkv, d_rel_bias.reshape(1, NUM_BUCKETS * H_B), d_sinks, d_g2, d_conv_b, d_gfin, loss_row[:, :1]]
    small = jnp.concatenate(small_parts, axis=1)
    n_small = small.shape[1]
    pad = (-n_small) % 128
    small = jnp.pad(small, ((0, 0), (0, pad)))
    small, _ = lax.optimization_barrier((small, [landed, *after]))
    (recv_small,) = _exchange([], [small], "exchange_small_grads")

    def flat(a):
        return a.reshape(1, -1)

    small_w = [norm1_g, q_a_norm_g, kv_a_norm_g, rel_bias, sinks, norm2_g, conv_b, final_norm_g]
    small_m = [m_norm1_g, m_q_a_norm_g, m_kv_a_norm_g, m_rel_bias, m_sinks, m_norm2_g, m_conv_b, m_final_norm_g]
    small_v = [v_norm1_g, v_q_a_norm_g, v_kv_a_norm_g, v_rel_bias, v_sinks, v_norm2_g, v_conv_b, v_final_norm_g]
    cat = lambda parts: jnp.pad(jnp.concatenate([flat(a) for a in parts], axis=1), ((0, 0), (0, pad + 1)))[None]
    sm = _adamw(recv_small, cat(small_w), cat(small_m), cat(small_v), "adamw_small")

    loss = sm[0][0, 0, n_small - 1]
    order =["norm1_g", "w_in", "q_a_norm_g", "w_q_b", "kv_a_norm_g", "w_kv_b", "rel_bias", "sinks", "w_out", "norm2_g", "w_up",
             "conv_w", "conv_b", "w_down", "final_norm_g"]
    small_names = ["norm1_g", "q_a_norm_g", "kv_a_norm_g", "rel_bias", "sinks", "norm2_g", "conv_b", "final_norm_g"]
    offs, o = {}, 0
    for n, a in zip(small_names, small_w):
        offs[n] = (o, a.size, a.shape)
        o += a.size
    outs = [loss, grad_x[None]]
    for kind in range(4):
        for n in order:
            if n in big:
                outs.append(tr(big[n][kind]) if n in transposed else big[n][kind])
            else:
                o, size, shape = offs[n]
                outs.append(sm[kind][0, 0, o:o + size].reshape(shape))
    return tuple(outs)
```

```python
import math

import jax
import jax.numpy as jnp
from jax import lax
from jax.experimental import pallas as pl
from jax.experimental.pallas import tpu as pltpu

F32 = jnp.float32
BF16 = jnp.bfloat16

N_DEV = 8
D_MODEL = 1024
EPS = 1e-6
H_A, QK_NOPE, QK_ROPE, V_DIM, Q_LORA, KV_LORA = 8, 128, 64, 128, 256, 128
QK_HEAD = QK_NOPE + QK_ROPE
ROPE_THETA = 10000.0
H_B, KV_B, GROUP, HD_B, WINDOW, Q_BLOCK = 16, 4, 4, 64, 128, 128
SPAN = Q_BLOCK + 2 * WINDOW
NUM_BUCKETS, MAX_DISTANCE = 32, 128
D_FF = 2816
ADAM_LR, ADAM_B1, ADAM_B2, ADAM_EPS, ADAM_WD, ADAM_STEP = 0.001, 0.9, 0.999, 1e-08, 0.01, 10

W_IN_SIZES = (Q_LORA, KV_LORA + QK_ROPE, H_B * HD_B, KV_B * HD_B, KV_B * HD_B, D_MODEL, D_MODEL)
W_IN_COLS = sum(W_IN_SIZES)
PROJ_P = 4096
PROJ_GA, PROJ_GB, PROJ_QB, PROJ_QLAT, PROJ_KB, PROJ_VB, PROJ_CKV, PROJ_KROPE = 0, 1, 2, 12, 13, 14, 30, 31

VMEM_LIMIT = 56 * 1024 * 1024

NN = (((1,), (0,)), ((), ()))
NT = (((1,), (1,)), ((), ()))
TN = (((0,), (0,)), ((), ()))


def _pcall(body, *, name, grid, in_specs, out_specs, out_shape, scratch_shapes=(), dims=None, comm=None, aliases=None, two_level=False):
    if comm is None:
        params = pltpu.CompilerParams(dimension_semantics=dims, vmem_limit_bytes=VMEM_LIMIT)
        return pl.pallas_call(body, name=name, grid=grid, in_specs=in_specs, out_specs=out_specs, out_shape=out_shape,
                              scratch_shapes=list(scratch_shapes), input_output_aliases=aliases or {}, compiler_params=params)
    assert not aliases
    stacked, replicated = comm
    arrs = [*stacked, *replicated]
    n_st, n_arr = len(stacked), len(arrs)
    single = not isinstance(out_specs, (list, tuple))
    o_specs, o_shape = ([out_specs], [out_shape]) if single else (list(out_specs), list(out_shape))
    n_in, n_out = len(in_specs), len(o_specs)

    def wrapped(*refs):
        c_in = refs[n_in:n_in + n_arr]
        c_out = refs[n_in + n_arr + n_out:n_in + 2 * n_arr + n_out]
        sems = refs[len(refs) - 3:]
        own = (*refs[:n_in], *refs[n_in + n_arr:n_in + n_arr + n_out], *refs[n_in + 2 * n_arr + n_out:len(refs) - 3])
        if two_level:
            assert n_st == 0
            start, finish = (lambda: _gather2(c_in, c_out, sems, False)), (lambda: _gather2(c_in, c_out, sems, True))
        else:
            start, finish = (lambda: _xchg_start(c_in, c_out, sems, n_st)), (lambda: _xchg_finish(c_in, c_out, sems, n_st))
        if not grid:
            start()
            finish()
            return
        first = last = None
        for d, n in enumerate(grid):
            pid = pl.program_id(d)
            first = (pid == 0) if first is None else first & (pid == 0)
            last = (pid == n - 1) if last is None else last & (pid == n - 1)

        pl.when(first)(start)
        body(*own)
        pl.when(last)(finish)

    params = pltpu.CompilerParams(dimension_semantics=("arbitrary",) * len(grid), vmem_limit_bytes=VMEM_LIMIT)
    call = pl.pallas_call(wrapped, name=name, grid=grid, in_specs=[*in_specs, *[ANY] * n_arr], out_specs=[*o_specs, *[ANY] * n_arr],
                          out_shape=[*o_shape, *_xchg_out_shapes(stacked, replicated)],
                          scratch_shapes=[*scratch_shapes, *_xchg_sems(n_arr)], compiler_params=params)

    def run(*args):
        res = call(*args, *arrs)
        outs, landed = res[:n_out], res[n_out:]
        return (outs[0] if single else outs), landed

    return run


def _dot(a, b, dn):
    return lax.dot_general(a, b, dn, preferred_element_type=F32)


def _tile(n, target):
    best = None
    for t in range(128, min(n, target) + 1, 128):
        if n % t == 0:
            best = t
    return n if best is None else best


def _matmul(a, b, mode, out_dtype, name, residual=None, tm=1024, tn=1024, comm=None, a2=None, epi=None):
    if mode == "nn":
        (M, K), N = a.shape, b.shape[1]
    elif mode == "nt":
        (M, K), N = a.shape, b.shape[0]
    else:
        (K, M), N = a.shape, b.shape[1]
    tm, tn = _tile(M, tm), _tile(N, tn)
    a_spec = pl.BlockSpec((K, tm), lambda i, j: (0, i)) if mode == "tn" else pl.BlockSpec((tm, K), lambda i, j: (i, 0))
    b_spec = pl.BlockSpec((tn, b.shape[1]), lambda i, j: (j, 0)) if mode == "nt" else pl.BlockSpec((K, tn), lambda i, j: (0, j))
    o_spec = pl.BlockSpec((tm, tn), lambda i, j: (i, j))
    in_specs, args = [a_spec, b_spec], [a, b]
    n1 = M // tm
    if a2 is not None and mode == "tn":
        assert M % tm == 0 and a2.shape[1] % tm == 0
        in_specs[0] = pl.BlockSpec((K, tm), lambda i, j: (0, jnp.minimum(i, n1 - 1)))
        in_specs.append(pl.BlockSpec((K, tm), lambda i, j: (0, jnp.maximum(i - n1, 0))))
        args.append(a2)
        M += a2.shape[1]
    elif a2 is not None:
        assert (mode == "nt" and K + a2.shape[1] == b.shape[1]) or (mode == "nn" and K + a2.shape[1] == b.shape[0])
        if mode == "nn":
            b_spec = in_specs[1] = pl.BlockSpec((b.shape[0], tn), lambda i, j: (0, j))
        in_specs.append(pl.BlockSpec((tm, a2.shape[1]), lambda i, j: (i, 0)))
        args.append(a2)
    if residual is not None:
        in_specs.append(o_spec)
        args.append(residual)
    n_mm = len(args)
    scratch = [pltpu.VMEM((tm, K), a.dtype)] if mode == "tn" else []
    if epi is None:
        out_specs, out_shape, is_acc = o_spec, jax.ShapeDtypeStruct((M, N), out_dtype), None
    else:
        assert tn == N
        fn, epi_ins, epi_outs = epi
        in_specs += [mk(tm) for _, mk in epi_ins]
        args += [arr for arr, _ in epi_ins]
        out_specs, out_shape, is_acc = _row_out_specs(epi_outs, M, tm)

    def body(*refs):
        a_ref, b_ref = refs[0], refs[1]
        n_out = 1 if epi is None else len(is_acc)
        out_refs = refs[len(args):len(args) + n_out]
        if mode == "tn":
            at_ref = refs[len(args) + n_out]

            first_col = pl.program_id(1) == 0
            from_a = first_col if a2 is None else first_col & (pl.program_id(0) < n1)

            @pl.when(from_a)
            def _():
                at_ref[...] = a_ref[...].T

            if a2 is not None:
                @pl.when(first_col & (pl.program_id(0) >= n1))
                def _():
                    at_ref[...] = refs[2][...].T

            acc = _dot(at_ref[...], b_ref[...], NN)
        elif a2 is not None and mode == "nt":
            acc = _dot(a_ref[...], b_ref[:, :K], NT) + _dot(refs[2][...], b_ref[:, K:], NT)
        elif a2 is not None:
            acc = _dot(a_ref[...], b_ref[:K, :], NN) + _dot(refs[2][...], b_ref[K:, :], NN)
        else:
            acc = _dot(a_ref[...], b_ref[...], NT if mode == "nt" else NN)
        if residual is not None:
            acc = acc + refs[n_mm - 1][...]
        if epi is None:
            out_refs[0][...] = acc.astype(out_dtype)
        else:
            _store_rows(out_refs, fn(acc, *[_load_f32(r) for r in refs[n_mm:len(args)]]), is_acc)

    return _pcall(body, name=name, grid=(M // tm, N // tn), in_specs=in_specs, out_specs=out_specs,
                  out_shape=out_shape, scratch_shapes=scratch,
                  dims=("arbitrary" if epi is not None else "parallel", "arbitrary"), comm=comm)(*args)


def _rows(arr, width=None, col=0):
    width = arr.shape[1] if width is None else width
    return (arr, lambda tm: pl.BlockSpec((tm, width), lambda i, *_: (i, col)))


def _heads(arr):
    return (arr, lambda tm: pl.BlockSpec((arr.shape[0], tm, arr.shape[2]), lambda i, *_: (0, i, 0)))


def _whole(arr):
    nd = arr.ndim
    return (arr, lambda tm: pl.BlockSpec(arr.shape, lambda i, *_: (0,) * nd))


def _row_out_specs(outs, n_rows, tm):
    out_specs, out_shape, is_acc = [], [], []
    for o in outs:
        if o[0] == "rows":
            out_specs.append(pl.BlockSpec((tm, o[1]), lambda i, *_: (i, 0)))
            out_shape.append(jax.ShapeDtypeStruct((n_rows, o[1]), o[2]))
        elif o[0] == "cols":
            out_specs.append(pl.BlockSpec((tm, o[1]), lambda i, *_, c=o[2]: (i, c)))
            out_shape.append(jax.ShapeDtypeStruct((n_rows, o[3]), o[4]))
        elif o[0] == "heads":
            out_specs.append(pl.BlockSpec((o[1], tm, o[2]), lambda i, *_: (0, i, 0)))
            out_shape.append(jax.ShapeDtypeStruct((o[1], n_rows, o[2]), o[3]))
        else:
            out_specs.append(pl.BlockSpec((o[1], o[2]), lambda i, *_: (0, 0)))
            out_shape.append(jax.ShapeDtypeStruct((o[1], o[2]), F32))
        is_acc.append(o[0] == "acc")
    return out_specs, out_shape, is_acc


def _load_f32(r):
    v = r[...]
    return v.astype(F32) if v.dtype == BF16 else v


def _store_rows(out_refs, vals, is_acc):
    for r, v, acc in zip(out_refs, vals, is_acc):
        if acc:
            @pl.when(pl.program_id(0) == 0)
            def _():
                r[...] = jnp.zeros_like(r)

            r[...] += v
        else:
            r[...] = v.astype(r.dtype)


def _rowwise(fn, name, n_rows, tm, ins, outs, upcast=True, into=None, gather=None):
    tm = min(tm, n_rows)
    assert n_rows % tm == 0
    in_specs = [mk(tm) for _, mk in ins]
    out_specs, out_shape, is_acc = _row_out_specs(outs, n_rows, tm)
    n_in = len(ins)
    args = [a for a, _ in ins]
    aliases = {}
    if into is not None:
        in_specs.append(ANY)
        args.append(into[0])
        aliases = {n_in: into[1]}

    def body(*refs):
        vals = fn(*[_load_f32(r) if upcast else r[...] for r in refs[:n_in]])
        _store_rows(refs[len(args):], vals, is_acc)

    return _pcall(body, name=name, grid=(n_rows // tm,), in_specs=in_specs, out_specs=out_specs,
                  out_shape=out_shape, dims=("arbitrary",), aliases=aliases,
                  comm=None if gather is None else ([], gather), two_level=True)(*args)


def _rms(x, g):
    r = lax.rsqrt(jnp.mean(x * x, axis=-1, keepdims=True) + EPS)
    return x * r * g


def _rms_bwd(dy, x, g):
    r = lax.rsqrt(jnp.mean(x * x, axis=-1, keepdims=True) + EPS)
    xhat = x * r
    dxhat = dy * g
    dx = r * (dxhat - xhat * jnp.mean(dxhat * xhat, axis=-1, keepdims=True))
    return dx, jnp.sum(dy * xhat, axis=0, keepdims=True)


def _rope(x1, x2, cos, sin):
    return x1 * cos - x2 * sin, x2 * cos + x1 * sin


def _rope_bwd(d1, d2, cos, sin):
    return d1 * cos + d2 * sin, d2 * cos - d1 * sin


def _sigmoid(x):
    return 1.0 / (1.0 + jnp.exp(-x))


MLA_SCALE = 1.0 / math.sqrt(QK_HEAD)
MLA_PRESCALE = MLA_SCALE * math.log2(math.e)
MLA_TQ, MLA_KC = 1024, 1024


def _mla_fwd(q_full, k_full, kv, S, comm=None):
    tq, kc = min(MLA_TQ, S), min(MLA_KC, S)

    def body(q_ref, k_ref, v_ref, o_ref, lse_ref):
        q = q_ref[0]
        m = jnp.full((tq, 1), -1e30, F32)
        l = jnp.zeros((tq, 1), F32)
        acc = jnp.zeros((tq, V_DIM), F32)
        for c in range(S // kc):
            s = _dot(q, k_ref[0, c * kc:(c + 1) * kc, :], NT)
            m_new = jnp.maximum(m, jnp.max(s, axis=-1, keepdims=True))
            alpha = jnp.exp2(m - m_new)
            p = jnp.exp2(s - m_new)
            l = alpha * l + jnp.sum(p, axis=-1, keepdims=True)
            acc = alpha * acc + _dot(p.astype(BF16), v_ref[c * kc:(c + 1) * kc, :], NN)
            m = m_new
        o_ref[...] = (acc / l).astype(BF16)
        lse_ref[0] = m + jnp.log2(l)

    return _pcall(
        body, name="mla_fwd", grid=(H_A, S // tq),
        in_specs=[pl.BlockSpec((1, tq, QK_HEAD), lambda h, i: (h, i, 0)),
                  pl.BlockSpec((1, S, QK_HEAD), lambda h, i: (h, 0, 0)),
                  pl.BlockSpec((S, V_DIM), lambda h, i: (0, 2 * h + 1))],
        out_specs=[pl.BlockSpec((tq, V_DIM), lambda h, i: (i, h)),
                   pl.BlockSpec((1, tq, 1), lambda h, i: (h, i, 0))],
        out_shape=[jax.ShapeDtypeStruct((S, H_A * V_DIM), BF16), jax.ShapeDtypeStruct((H_A, S, 1), F32)],
        dims=("parallel", "parallel"), comm=comm)(q_full, k_full, kv)


def _mla_bwd(q_full, k_full, kv, do_a, o_a, lse, S, comm=None):
    tq, kc = min(MLA_TQ, S), min(MLA_KC, S)

    def body(q_ref, k_ref, v_ref, do_ref, o_ref, lse_ref, dqn_ref, dqr_ref, dkv_out, dkr_out, dk_ref, dv_ref):
        @pl.when(pl.program_id(1) == 0)
        def _():
            dk_ref[...] = jnp.zeros_like(dk_ref)
            dv_ref[...] = jnp.zeros_like(dv_ref)

        q = q_ref[0]
        do = do_ref[...]
        lse_q = lse_ref[0]
        delta = jnp.sum(do.astype(F32) * o_ref[...].astype(F32), axis=-1, keepdims=True)
        dq = jnp.zeros((tq, QK_HEAD), F32)
        for c in range(S // kc):
            k = k_ref[0, c * kc:(c + 1) * kc, :]
            v = v_ref[c * kc:(c + 1) * kc, :]
            p = jnp.exp2(_dot(q, k, NT) - lse_q)
            ds = (p * (_dot(do, v, NT) - delta)).astype(BF16)
            dq = dq + _dot(ds, k, NN)
            dk_ref[0, c * kc:(c + 1) * kc, :] += _dot(ds, q, TN)
            dv_ref[0, c * kc:(c + 1) * kc, :] += _dot(p.astype(BF16), do, TN)
        dq = dq * MLA_SCALE
        dqn_ref[...] = dq[:, :QK_NOPE].astype(BF16)
        dqr_ref[0] = dq[:, QK_NOPE:].astype(BF16)

        @pl.when(pl.program_id(1) == S // tq - 1)
        def _():
            dk = dk_ref[0] * math.log(2.0)
            dkv_out[...] = jnp.concatenate([dk[:, :QK_NOPE], dv_ref[0]], axis=1).astype(BF16)
            dkr_out[0] = dk[:, QK_NOPE:].astype(BF16)

    return _pcall(
        body, name="mla_bwd", grid=(H_A, S // tq),
        in_specs=[pl.BlockSpec((1, tq, QK_HEAD), lambda h, i: (h, i, 0)),
                  pl.BlockSpec((1, S, QK_HEAD), lambda h, i: (h, 0, 0)),
                  pl.BlockSpec((S, V_DIM), lambda h, i: (0, 2 * h + 1)),
                  pl.BlockSpec((tq, V_DIM), lambda h, i: (i, h)),
                  pl.BlockSpec((tq, V_DIM), lambda h, i: (i, h)),
                  pl.BlockSpec((1, tq, 1), lambda h, i: (h, i, 0))],
        out_specs=[pl.BlockSpec((tq, QK_NOPE), lambda h, i: (i, h)),
                   pl.BlockSpec((1, tq, QK_ROPE), lambda h, i: (h, i, 0)),
                   pl.BlockSpec((S, QK_NOPE + V_DIM), lambda h, i: (0, h)),
                   pl.BlockSpec((1, S, QK_ROPE), lambda h, i: (h, 0, 0))],
        out_shape=[jax.ShapeDtypeStruct((S, H_A * QK_NOPE), BF16), jax.ShapeDtypeStruct((H_A, S, QK_ROPE), BF16),
                   jax.ShapeDtypeStruct((S, H_A * (QK_NOPE + V_DIM)), BF16), jax.ShapeDtypeStruct((H_A, S, QK_ROPE), BF16)],
        scratch_shapes=[pltpu.VMEM((1, S, QK_HEAD), F32), pltpu.VMEM((1, S, V_DIM), F32)],
        dims=("parallel", "arbitrary"), comm=comm)(q_full, k_full, kv, do_a, o_a, lse)


WIN_SCALE = 1.0 / math.sqrt(HD_B)


WIN_PER_STEP = 4


def _win_specs(S):
    last, B = S // Q_BLOCK - 1, WIN_PER_STEP
    qspec = pl.BlockSpec((B * Q_BLOCK, H_B * HD_B), lambda i: (i, PROJ_QB))
    kspecs = [[pl.BlockSpec((Q_BLOCK, KV_B * HD_B), lambda i, d=d, c=c: (jnp.clip(B * i + d, 0, last), c)) for d in range(-1, B + 1)]
              for c in (PROJ_KB, PROJ_VB)]
    bias_spec = pl.BlockSpec((H_B, SPAN, Q_BLOCK), lambda i: (0, 0, 0))
    sink_spec = pl.BlockSpec((H_B, Q_BLOCK), lambda i: (0, 0))
    return qspec, kspecs, bias_spec, sink_spec


def _win_edge_ok(n, n_blk):
    row = lax.broadcasted_iota(jnp.int32, (SPAN, 1), 0)
    return jnp.logical_not(((n == 0) & (row < WINDOW)) | ((n == n_blk - 1) & (row >= SPAN - WINDOW)))


def _lanes4(pieces):
    return jnp.concatenate(pieces, axis=1)


def _win_probs(kg, q4t, bias_ref, sink_ref, g, edge_ok):
    bias4 = _lanes4([bias_ref[GROUP * g + j] for j in range(GROUP)])
    sink4 = _lanes4([sink_ref[GROUP * g + j:GROUP * g + j + 1, :] for j in range(GROUP)])
    s = jnp.where(edge_ok, _dot(kg, q4t, NN) + bias4, -1e30)
    m = jnp.maximum(jnp.max(s, axis=0, keepdims=True), sink4)
    p = jnp.exp(s - m)
    e_sink = jnp.exp(sink4 - m)
    inv_l = 1.0 / (jnp.sum(p, axis=0, keepdims=True) + e_sink)
    return p * inv_l, e_sink * inv_l


def _group_t(xt, g):
    return _lanes4([xt[HD_B * (GROUP * g + j):HD_B * (GROUP * g + j + 1), :] for j in range(GROUP)])


def _rows_of(ref, b):
    return ref[Q_BLOCK * b:Q_BLOCK * (b + 1), :]


def _win_fwd(proj, o_a, bias_t, sinks_b, S, comm=None):
    n_blk, B = S // Q_BLOCK, WIN_PER_STEP
    qspec, kspecs, bias_spec, sink_spec = _win_specs(S)
    rows = lambda col: pl.BlockSpec((B * Q_BLOCK, H_B * HD_B), lambda i: (i, col))

    def body(q_ref, *refs):
        k_refs, v_refs = refs[:B + 2], refs[B + 2:2 * B + 4]
        bias_ref, sink_ref, ga_ref, gb_ref, oa_ref, o_ref, mixed_ref = refs[2 * B + 4:]
        for b in range(B):
            edge_ok = _win_edge_ok(B * pl.program_id(0) + b, n_blk)
            k = jnp.concatenate([r[...] for r in k_refs[b:b + 3]], axis=0)
            vt = jnp.concatenate([r[...] for r in v_refs[b:b + 3]], axis=0).T
            qt = (_rows_of(q_ref, b).astype(F32) * WIN_SCALE).T.astype(BF16)
            parts = []
            for g in range(KV_B):
                p, _ = _win_probs(k[:, HD_B * g:HD_B * (g + 1)], _group_t(qt, g), bias_ref, sink_ref, g, edge_ok)
                o4t = _dot(vt[HD_B * g:HD_B * (g + 1), :], p.astype(BF16), NN)
                parts += [o4t[:, Q_BLOCK * j:Q_BLOCK * (j + 1)] for j in range(GROUP)]
            ob = jnp.concatenate(parts, axis=0).T
            o_ref[Q_BLOCK * b:Q_BLOCK * (b + 1), :] = ob.astype(BF16)
            ga, gb, oa = (_rows_of(r, b).astype(F32) for r in (ga_ref, gb_ref, oa_ref))
            mixed_ref[Q_BLOCK * b:Q_BLOCK * (b + 1), :] = (_sigmoid(ga) * oa + _sigmoid(gb) * ob).astype(BF16)

    return _pcall(body, name="win_fwd_mix", grid=(n_blk // B,),
                  in_specs=[qspec, *kspecs[0], *kspecs[1], bias_spec, sink_spec, rows(PROJ_GA), rows(PROJ_GB), rows(0)],
                  out_specs=[rows(0), rows(0)],
                  out_shape=[jax.ShapeDtypeStruct((S, H_B * HD_B), BF16)] * 2,
                  dims=("parallel",), comm=comm)(*[proj] * (2 * B + 5), bias_t, sinks_b, proj, proj, o_a)


def _win_bwd(proj, bias_t, sinks_b, do_b, d_proj, S):
    n_blk, B = S // Q_BLOCK, WIN_PER_STEP
    qspec, kspecs, bias_spec, sink_spec = _win_specs(S)

    def body(q_ref, *refs):
        k_refs, v_refs = refs[:B + 2], refs[B + 2:2 * B + 4]
        bias_ref, sink_ref, do_ref, _, dq_ref, dk_ref, dv_ref, dbias_ref, dsink_ref, dsink_acc = refs[2 * B + 4:]
        i = pl.program_id(0)

        @pl.when(i == 0)
        def _():
            dk_ref[...] = jnp.zeros_like(dk_ref)
            dv_ref[...] = jnp.zeros_like(dv_ref)
            dbias_ref[...] = jnp.zeros_like(dbias_ref)
            dsink_acc[...] = jnp.zeros_like(dsink_acc)

        d_bias, d_sink, dk_blocks, dv_blocks = {}, {}, [], []
        for b in range(B):
            edge_ok = _win_edge_ok(B * i + b, n_blk)
            k = jnp.concatenate([r[...] for r in k_refs[b:b + 3]], axis=0)
            v = jnp.concatenate([r[...] for r in v_refs[b:b + 3]], axis=0)
            kt = k.T
            qt = (_rows_of(q_ref, b).astype(F32) * WIN_SCALE).T.astype(BF16)
            dot_ = _rows_of(do_ref, b).astype(F32).T.astype(BF16)
            dq_parts, dks, dvs = [], [], []
            for g in range(KV_B):
                kg, vg = k[:, HD_B * g:HD_B * (g + 1)], v[:, HD_B * g:HD_B * (g + 1)]
                q4t, do4t = _group_t(qt, g), _group_t(dot_, g)
                p, p_sink = _win_probs(kg, q4t, bias_ref, sink_ref, g, edge_ok)
                dp = _dot(vg, do4t, NN)
                delta = jnp.sum(p * dp, axis=0, keepdims=True)
                ds = p * (dp - delta)
                d_bias[g] = ds if b == 0 else d_bias[g] + ds
                d_sink[g] = -p_sink * delta if b == 0 else d_sink[g] - p_sink * delta
                dsb = ds.astype(BF16)
                dq4t = _dot(kt[HD_B * g:HD_B * (g + 1), :], dsb, NN) * WIN_SCALE
                dq_parts += [dq4t[:, Q_BLOCK * j:Q_BLOCK * (j + 1)] for j in range(GROUP)]
                dks.append(_dot(dsb, q4t, NT))
                dvs.append(_dot(p.astype(BF16), do4t, NT))
            dq_ref[Q_BLOCK * b:Q_BLOCK * (b + 1), :] = jnp.concatenate(dq_parts, axis=0).T.astype(BF16)
            dk_blocks.append(jnp.concatenate(dks, axis=1))
            dv_blocks.append(jnp.concatenate(dvs, axis=1))

        for g in range(KV_B):
            for j in range(GROUP):
                dbias_ref[GROUP * g + j] += d_bias[g][:, Q_BLOCK * j:Q_BLOCK * (j + 1)]
            dsink_acc[g:g + 1, :] += d_sink[g]

        def overlap(blocks):
            out = blocks[0]
            for blk in blocks[1:]:
                keep = out.shape[0] - 2 * Q_BLOCK
                out = jnp.concatenate([out[:keep], out[keep:] + blk[:2 * Q_BLOCK], blk[2 * Q_BLOCK:]], axis=0)
            return out

        rows = pl.ds(pl.multiple_of(i * (B * Q_BLOCK), B * Q_BLOCK), (B + 2) * Q_BLOCK)
        dk_ref[rows, :] += overlap(dk_blocks)
        dv_ref[rows, :] += overlap(dv_blocks)

        @pl.when(i == n_blk // B - 1)
        def _():
            acc = dsink_acc[...]
            dsink_ref[...] = jnp.concatenate(
                [jnp.sum(acc[:, Q_BLOCK * j:Q_BLOCK * (j + 1)], axis=1, keepdims=True) for j in range(GROUP)], axis=1)

    whole = lambda shape: pl.BlockSpec(shape, lambda i: (0,) * len(shape))
    return _pcall(
        body, name="win_bwd", grid=(n_blk // B,),
        in_specs=[qspec, *kspecs[0], *kspecs[1], bias_spec, sink_spec, pl.BlockSpec((B * Q_BLOCK, H_B * HD_B), lambda i: (i, 0)), ANY],
        out_specs=[qspec, whole((S + 2 * WINDOW, KV_B * HD_B)),
                   whole((S + 2 * WINDOW, KV_B * HD_B)), whole((H_B, SPAN, Q_BLOCK)), whole((KV_B, GROUP))],
        out_shape=[jax.ShapeDtypeStruct((S, PROJ_P), BF16), jax.ShapeDtypeStruct((S + 2 * WINDOW, KV_B * HD_B), F32),
                   jax.ShapeDtypeStruct((S + 2 * WINDOW, KV_B * HD_B), F32), jax.ShapeDtypeStruct((H_B, SPAN, Q_BLOCK), F32),
                   jax.ShapeDtypeStruct((KV_B, GROUP), F32)],
        scratch_shapes=[pltpu.VMEM((KV_B, GROUP * Q_BLOCK), F32)],
        dims=("arbitrary",), aliases={2 * B + 8: 0})(*[proj] * (2 * B + 5), bias_t, sinks_b, do_b, d_proj)


def _bias_table(rel_bias_t, onehot_t, in_band):
    def body(rb_ref, oh_ref, band_ref, o_ref):
        t = lax.dot_general(rb_ref[...], oh_ref[...], NN, preferred_element_type=F32, precision=lax.Precision.HIGHEST)
        o_ref[...] = jnp.where(band_ref[...] > 0.5, t, -1e30)

    n = onehot_t.shape[1]
    tn = _tile(n, 8192)
    return _pcall(body, name="bias_table", grid=(n // tn,),
                  in_specs=[pl.BlockSpec((H_B, NUM_BUCKETS), lambda j: (0, 0)), pl.BlockSpec((NUM_BUCKETS, tn), lambda j: (0, j)),
                            pl.BlockSpec((1, tn), lambda j: (0, j))],
                  out_specs=pl.BlockSpec((H_B, tn), lambda j: (0, j)),
                  out_shape=jax.ShapeDtypeStruct((H_B, n), F32), dims=("parallel",))(rel_bias_t, onehot_t, in_band)


def _bias_table_bwd(dbias, onehot_t):
    n = onehot_t.shape[1]
    tk = _tile(n, 8192)

    def body(d_ref, oh_ref, o_ref):
        @pl.when(pl.program_id(0) == 0)
        def _():
            o_ref[...] = jnp.zeros_like(o_ref)

        o_ref[...] += lax.dot_general(d_ref[...], oh_ref[...], NT, preferred_element_type=F32, precision=lax.Precision.HIGHEST)

    return _pcall(body, name="bias_table_bwd", grid=(n // tk,),
                  in_specs=[pl.BlockSpec((H_B, tk), lambda j: (0, j)), pl.BlockSpec((NUM_BUCKETS, tk), lambda j: (0, j))],
                  out_specs=pl.BlockSpec((H_B, NUM_BUCKETS), lambda j: (0, 0)),
                  out_shape=jax.ShapeDtypeStruct((H_B, NUM_BUCKETS), F32), dims=("arbitrary",))(dbias, onehot_t)


CONV_STRIP = 128
N_STRIPS = D_FF // CONV_STRIP
CONV_ROWS = 128
HALO = 8


def _strip(rows, half):
    return pl.BlockSpec((rows, CONV_STRIP), lambda j: (0, j + half * N_STRIPS))


def _fill_padded(pad_ref, src_ref, halo, S):
    pad_ref[0:halo, :] = jnp.zeros((halo, CONV_STRIP), F32)
    pad_ref[halo + S:2 * halo + S, :] = jnp.zeros((halo, CONV_STRIP), F32)
    pad_ref[halo:halo + S, :] = src_ref[...].astype(F32)


def _conv_gate_fwd(u, conv_w, conv_b, S):
    R = min(CONV_ROWS, S)

    def body(ug_ref, uv_ref, wg_ref, wv_ref, bg_ref, bv_ref, a_ref, gpad, vpad):
        _fill_padded(gpad, ug_ref, HALO, S)
        _fill_padded(vpad, uv_ref, HALO, S)
        wg, wv, bg, bv = wg_ref[...], wv_ref[...], bg_ref[...], bv_ref[...]

        def conv(pad_ref, r0, w, b):
            dn, mid, up = (pad_ref[pl.ds(r0 + HALO + d, R), :] for d in (-1, 0, 1))
            return dn * w[0:1, :] + mid * w[1:2, :] + up * w[2:3, :] + b

        def step(c, carry):
            r0 = pl.multiple_of(c * R, R)
            g = conv(gpad, r0, wg, bg)
            val = conv(vpad, r0, wv, bv)
            a_ref[pl.ds(r0, R), :] = (g * _sigmoid(g) * val).astype(BF16)
            return carry

        lax.fori_loop(0, S // R, step, 0)

    return _pcall(body, name="conv_gate_fwd", grid=(N_STRIPS,),
                  in_specs=[_strip(S, 0), _strip(S, 1), _strip(3, 0), _strip(3, 1), _strip(1, 0), _strip(1, 1)],
                  out_specs=_strip(S, 0), out_shape=jax.ShapeDtypeStruct((S, D_FF), BF16),
                  scratch_shapes=[pltpu.VMEM((S + 2 * HALO, CONV_STRIP), F32)] * 2,
                  dims=("parallel",))(u, u, conv_w, conv_w, conv_b, conv_b)


def _conv_gate_bwd(u, conv_w, conv_b, da, S):
    R = min(CONV_ROWS, S)
    n = R + 2 * HALO

    def body(ug_ref, uv_ref, wg_ref, wv_ref, bg_ref, bv_ref, da_ref, dug_ref, duv_ref, dwg_ref, dwv_ref, dbg_ref, dbv_ref,
             gpad, vpad, dapad):
        _fill_padded(gpad, ug_ref, 2 * HALO, S)
        _fill_padded(vpad, uv_ref, 2 * HALO, S)
        _fill_padded(dapad, da_ref, HALO, S)
        wg, wv, bg, bv = wg_ref[...], wv_ref[...], bg_ref[...], bv_ref[...]

        def conv(pad_ref, r0, w, b):
            dn, mid, up = (pad_ref[pl.ds(r0 + HALO + d, n), :] for d in (-1, 0, 1))
            return dn * w[0:1, :] + mid * w[1:2, :] + up * w[2:3, :] + b, mid[HALO:HALO + R]

        def conv_bwd(duc, u_mid, w, r0, du_ref):
            dn, mid, up = pltpu.roll(duc, 1, axis=0)[HALO:HALO + R], duc[HALO:HALO + R], pltpu.roll(duc, n - 1, axis=0)[HALO:HALO + R]
            du_ref[pl.ds(r0, R), :] = (up * w[0:1, :] + mid * w[1:2, :] + dn * w[2:3, :]).astype(BF16)
            dw = jnp.concatenate([jnp.sum(up * u_mid, axis=0, keepdims=True), jnp.sum(mid * u_mid, axis=0, keepdims=True),
                                  jnp.sum(dn * u_mid, axis=0, keepdims=True)], axis=0)
            return dw, jnp.sum(mid, axis=0, keepdims=True)

        def step(c, carry):
            dw_g, db_g, dw_v, db_v = carry
            r0 = pl.multiple_of(c * R, R)
            g, ug_mid = conv(gpad, r0, wg, bg)
            val, uv_mid = conv(vpad, r0, wv, bv)
            da_ext = dapad[pl.ds(r0, n), :]
            sg = _sigmoid(g)
            ddw_v, ddb_v = conv_bwd(da_ext * (g * sg), uv_mid, wv, r0, duv_ref)
            ddw_g, ddb_g = conv_bwd(da_ext * val * (sg * (1.0 + g * (1.0 - sg))), ug_mid, wg, r0, dug_ref)
            return dw_g + ddw_g, db_g + ddb_g, dw_v + ddw_v, db_v + ddb_v

        z3, z1 = jnp.zeros((3, CONV_STRIP), F32), jnp.zeros((1, CONV_STRIP), F32)
        dwg_ref[...], dbg_ref[...], dwv_ref[...], dbv_ref[...] = lax.fori_loop(0, S // R, step, (z3, z1, z3, z1))

    half = lambda r, dt: (_strip(r, 0), jax.ShapeDtypeStruct((r, D_FF), dt))
    outs = [half(S, BF16), half(S, BF16), half(3, F32), half(3, F32), half(1, F32), half(1, F32)]
    return _pcall(
        body, name="conv_gate_bwd", grid=(N_STRIPS,),
        in_specs=[_strip(S, 0), _strip(S, 1), _strip(3, 0), _strip(3, 1), _strip(1, 0), _strip(1, 1), _strip(S, 0)],
        out_specs=[o[0] for o in outs], out_shape=[o[1] for o in outs],
        scratch_shapes=[pltpu.VMEM((S + 4 * HALO, CONV_STRIP), F32)] * 2 + [pltpu.VMEM((S + 2 * HALO, CONV_STRIP), F32)],
        dims=("parallel",))(u, u, conv_w, conv_w, conv_b, conv_b, da)


MESH = pl.DeviceIdType.MESH
ANY = pl.BlockSpec(memory_space=pl.ANY)


def _place():
    return lax.axis_index("x"), lax.axis_index("y"), lax.axis_index("c")


def _gather2(ins, outs, sems, finish):
    send_sems, recv_sems, local_sems = sems
    n_arr = len(ins)
    x, y, c = _place()
    me, sibling = (x, y, c), (x, y, 1 - c)
    chips = [(1 - x, y), (x, 1 - y), (1 - x, 1 - y)]

    def slot(a, p):
        return outs[a].at[4 * p[0] + 2 * p[1] + p[2]]

    def copy(a, k, block, to, src=None):
        return pltpu.make_async_remote_copy(
            src_ref=slot(a, block) if src is None else src, dst_ref=slot(a, block),
            send_sem=send_sems.at[a, k], recv_sem=recv_sems.at[a, k], device_id=to, device_id_type=MESH)

    mine = [pltpu.make_async_copy(ins[a], slot(a, me), local_sems.at[a]) for a in range(n_arr)]
    first = []
    for a in range(n_arr):
        first.append(copy(a, 0, me, sibling, src=ins[a]))
        first += [copy(a, 1 + j, me, (*chip, c), src=ins[a]) for j, chip in enumerate(chips)]
    if not finish:
        for cp in mine + first:
            cp.start()
        return
    passed = []
    for j, chip in enumerate(chips):
        for a in range(n_arr):
            copy(a, 1 + j, (*chip, c), me).wait_recv()
            cp = copy(a, 4 + j, (*chip, c), sibling)
            cp.start()
            passed.append(cp)
    for a in range(n_arr):
        copy(a, 0, sibling, me).wait_recv()
        for j, chip in enumerate(chips):
            copy(a, 4 + j, (*chip, 1 - c), me).wait_recv()
    for cp in first + passed:
        cp.wait_send()
    for cp in mine:
        cp.wait()


def _xchg_out_shapes(stacked, replicated):
    return ([jax.ShapeDtypeStruct(s.shape, s.dtype) for s in stacked]
            + [jax.ShapeDtypeStruct((N_DEV, *r.shape), r.dtype) for r in replicated])


def _xchg_sems(n_arr):
    return [pltpu.SemaphoreType.DMA((n_arr, 7)), pltpu.SemaphoreType.DMA((n_arr, 7)), pltpu.SemaphoreType.DMA((n_arr,))]


def _xchg_copies(ins, outs, sems, n_st, with_recv):
    send_sems, recv_sems, local_sems = sems
    n_arr = len(ins)
    x, y, c = _place()
    me = 4 * x + 2 * y + c

    def src(a, idx):
        return ins[a].at[idx] if a < n_st else ins[a]

    mine = [pltpu.make_async_copy(src(a, me), outs[a].at[me], local_sems.at[a]) for a in range(n_arr)]
    pairs = []
    for k in range(1, N_DEV):
        px, py, pc = x ^ (k >> 2), y ^ ((k >> 1) & 1), c ^ (k & 1)
        peer = 4 * px + 2 * py + pc
        for a in range(n_arr):
            sems_k = dict(send_sem=send_sems.at[a, k - 1], recv_sem=recv_sems.at[a, k - 1], device_id_type=MESH)
            send = pltpu.make_async_remote_copy(src_ref=src(a, peer), dst_ref=outs[a].at[me], device_id=(px, py, pc), **sems_k)
            recv = None
            if with_recv:
                recv = pltpu.make_async_remote_copy(src_ref=src(a, peer), dst_ref=outs[a].at[peer], device_id=(x, y, c), **sems_k)
            pairs.append((send, recv))
    return mine, pairs


def _xchg_start(ins, outs, sems, n_st):
    mine, pairs = _xchg_copies(ins, outs, sems, n_st, False)
    for cp in mine:
        cp.start()
    for send, _ in pairs:
        send.start()


def _xchg_finish(ins, outs, sems, n_st):
    mine, pairs = _xchg_copies(ins, outs, sems, n_st, True)
    for _, recv in pairs:
        recv.wait_recv()
    for send, _ in pairs:
        send.wait_send()
    for cp in mine:
        cp.wait()


def _exchange(stacked, replicated, name):
    _, landed = _pcall(lambda: None, name=name, grid=(), in_specs=[], out_specs=[], out_shape=[], comm=(stacked, replicated))()
    return landed


HBM = pl.BlockSpec(memory_space=pltpu.HBM)
SEMS = pl.BlockSpec(memory_space=pltpu.SEMAPHORE)
SIDE_EFFECT = pltpu.SideEffectType.DATAFLOW_SIDE_EFFECTING


N_SPLIT_SEMS = 2 * (N_DEV - 1)


def _split_copies(src, land, sems, with_recv):
    x, y, c = _place()
    me = 4 * x + 2 * y + c
    pairs = []
    for k in range(1, N_DEV):
        px, py, pc = x ^ (k >> 2), y ^ ((k >> 1) & 1), c ^ (k & 1)
        peer = 4 * px + 2 * py + pc
        sems_k = dict(send_sem=sems[k - 1], recv_sem=sems[N_DEV - 1 + k - 1], device_id_type=MESH)
        send = pltpu.make_async_remote_copy(src_ref=src.at[peer], dst_ref=land.at[me], device_id=(px, py, pc), **sems_k)
        recv = None
        if with_recv:
            recv = pltpu.make_async_remote_copy(src_ref=src.at[peer], dst_ref=land.at[peer], device_id=(x, y, c), **sems_k)
        pairs.append((send, recv))
    return pairs


def _exchange_start(stacked, name):
    def body(src, land, *rest):
        for send, _ in _split_copies(src, land, rest[:N_SPLIT_SEMS], False):
            send.start()
        rest[-1][...] = jnp.zeros_like(rest[-1])

    shape = pltpu.HBM(stacked.shape, stacked.dtype)
    res = pl.pallas_call(
        body, name=name, in_specs=[HBM, HBM],
        out_shape=(*[pltpu.SemaphoreType.DMA(())] * N_SPLIT_SEMS, shape, shape, jax.ShapeDtypeStruct((8, 128), F32)),
        out_specs=(*[SEMS] * N_SPLIT_SEMS, HBM, HBM, pl.BlockSpec(memory_space=pltpu.VMEM)),
        input_output_aliases={0: N_SPLIT_SEMS, 1: N_SPLIT_SEMS + 1},
        compiler_params=pltpu.CompilerParams(has_side_effects=SIDE_EFFECT),
    )(pltpu.with_memory_space_constraint(stacked, pltpu.HBM),
      pltpu.with_memory_space_constraint(lax.empty(stacked.shape, stacked.dtype), pltpu.HBM))
    return res[:N_SPLIT_SEMS], res[N_SPLIT_SEMS], res[N_SPLIT_SEMS + 1], res[-1]


def _exchange_wait(sems, src, land, after, name):
    def body(src_ref, land_ref, *rest):
        for send, recv in _split_copies(src_ref, land_ref, rest[:N_SPLIT_SEMS], True):
            send.wait_send()
            recv.wait_recv()

    shape = pltpu.HBM(src.shape, src.dtype)
    return pl.pallas_call(
        body, name=name, in_specs=[HBM, HBM, *[SEMS] * N_SPLIT_SEMS, ANY],
        out_shape=(shape, shape), out_specs=(HBM, HBM), input_output_aliases={0: 0, 1: 1},
        compiler_params=pltpu.CompilerParams(has_side_effects=SIDE_EFFECT))(src, land, *sems, after)[1]


def _adamw(parts, w, m, v, name):
    _, R, C = w.shape
    tr = R if R <= 512 else max(t for t in range(16, 513, 16) if R % t == 0)
    pr = tr if parts.shape[1] == R else -(-R // 16) * 16
    assert pr == tr or tr == R

    def body(p_ref, w_ref, m_ref, v_ref, g_ref, d_ref, nm_ref, nv_ref):
        g = p_ref[0].astype(F32)[:tr]
        for s in range(1, N_DEV):
            g = g + p_ref[s].astype(F32)[:tr]
        m2 = ADAM_B1 * m_ref[0] + (1.0 - ADAM_B1) * g
        v2 = ADAM_B2 * v_ref[0] + (1.0 - ADAM_B2) * (g * g)
        m_hat = m2 / (1.0 - ADAM_B1 ** ADAM_STEP)
        v_hat = v2 / (1.0 - ADAM_B2 ** ADAM_STEP)
        g_ref[0] = g
        d_ref[0] = -ADAM_LR * (m_hat / (jnp.sqrt(v_hat) + ADAM_EPS) + ADAM_WD * w_ref[0])
        nm_ref[0] = m2
        nv_ref[0] = v2

    blk = pl.BlockSpec((1, tr, C), lambda i: (0, i, 0))
    return _pcall(body, name=name, grid=(R // tr,),
                  in_specs=[pl.BlockSpec((N_DEV, pr, C), lambda i: (0, i, 0)), blk, blk, blk],
                  out_specs=[blk] * 4, out_shape=[jax.ShapeDtypeStruct((1, R, C), F32)] * 4,
                  dims=("parallel",))(parts, w, m, v)


def _t5_bucket(rel):
    nb = NUM_BUCKETS // 2
    max_exact = nb // 2
    base = (rel > 0).astype(jnp.int32) * nb
    n = jnp.abs(rel)
    nf = jnp.maximum(n, 1).astype(jnp.float32)
    large = max_exact + (jnp.log(nf / max_exact) / math.log(MAX_DISTANCE / max_exact) * (nb - max_exact)).astype(jnp.int32)
    large = jnp.minimum(large, nb - 1)
    return base + jnp.where(n < max_exact, n, large)


def _unstack_cols(g):
    return jnp.transpose(g, (1, 0, 2)).reshape(g.shape[1], N_DEV * g.shape[2])


def _stack_cols(w, n=N_DEV):
    R = w.shape[0]
    return jnp.transpose(w.reshape(R, n, w.shape[1] // n), (1, 0, 2))


def _stack_halves(g, v):
    return jnp.concatenate([_stack_cols(g, N_DEV // 2), _stack_cols(v, N_DEV // 2)], axis=0)


def kernel(x, positions, norm1_g, w_in, q_a_norm_g, w_q_b, kv_a_norm_g, w_kv_b, rel_bias, sinks, w_out, norm2_g, w_up, conv_w, conv_b, w_down, final_norm_g, loss_target, m_norm1_g, m_w_in, m_q_a_norm_g, m_w_q_b, m_kv_a_norm_g, m_w_kv_b, m_rel_bias, m_sinks, m_w_out, m_norm2_g, m_w_up, m_conv_w, m_conv_b, m_w_down, m_final_norm_g, v_norm1_g, v_w_in, v_q_a_norm_g, v_w_q_b, v_kv_a_norm_g, v_w_kv_b, v_rel_bias, v_sinks, v_w_out, v_norm2_g, v_w_up, v_conv_w, v_conv_b, v_w_down, v_final_norm_g):
    S = x.shape[1]
    x = x[0]
    target = loss_target[0]
    TM = 256

    tr = lambda w: jnp.swapaxes(w, 1, 2)
    (h1,), (g_in, g_qb, g_kvb) = _rowwise(lambda a, g: (_rms(a, g),), "norm1_gather", S, TM, [_rows(x), _whole(norm1_g)], [("rows", D_MODEL, BF16)],
                                          gather=[tr(w_in)[0].astype(BF16), tr(w_q_b)[0].astype(BF16), w_kv_b[0].astype(BF16)])
    late_weights = [w_out[0].astype(BF16), tr(w_up)[0].astype(BF16), conv_w[0]]
    wi = g_in.reshape(W_IN_COLS, D_MODEL)
    c0, c1, c2, c3, c4, c5 = (sum(W_IN_SIZES[:i + 1]) for i in range(6))
    w_in_pt = jnp.concatenate([wi[c4:c5], wi[c5:], wi[c1:c2], wi[:c0], wi[c2:c3], wi[c3:c4],
                               wi[c0:c0 + KV_LORA], wi[c0 + KV_LORA:c1], jnp.zeros((64, D_MODEL), BF16)], axis=0)
    wq = g_qb.reshape(H_A, QK_HEAD, Q_LORA)
    w_qb_pt = jnp.concatenate([wq[:, :QK_NOPE].reshape(H_A * QK_NOPE, Q_LORA), wq[:, QK_NOPE:].reshape(H_A * QK_ROPE, Q_LORA)], axis=0)
    w_kvb = _unstack_cols(g_kvb)

    half = QK_ROPE // 2
    inv_freq = ROPE_THETA ** (-jnp.arange(half, dtype=F32) / half)
    ang = positions.astype(F32)[:, None] * inv_freq[None, :]
    cos, sin = jnp.cos(ang), jnp.sin(ang)
    qa = jnp.arange(Q_BLOCK, dtype=jnp.int32)[:, None]
    kc = jnp.arange(SPAN, dtype=jnp.int32)[None, :]
    rel = (kc - WINDOW - qa).T
    in_band = (jnp.abs(rel) <= WINDOW).astype(F32).reshape(1, Q_BLOCK * SPAN)
    onehot_t = (_t5_bucket(rel).reshape(1, Q_BLOCK * SPAN) == jnp.arange(NUM_BUCKETS, dtype=jnp.int32)[:, None]).astype(F32)
    bias_t = _bias_table(rel_bias.T, onehot_t, in_band).reshape(H_B, SPAN, Q_BLOCK)
    sinks_b = jnp.broadcast_to(sinks.reshape(H_B, 1), (H_B, Q_BLOCK))

    proj =_matmul(h1, w_in_pt, "nt", BF16, "proj")

    def lat_fn(qlat, ckv, kr, gq, gkv, cs, sn):
        r1, r2 = _rope(kr[:, :half], kr[:, half:QK_ROPE], cs, sn)
        return _rms(qlat, gq), _rms(ckv, gkv), jnp.concatenate([r1, r2], axis=1)

    qn, ckvn, k_rope = _rowwise(lat_fn, "latents", S, TM,
                                [_rows(proj, 256, PROJ_QLAT), _rows(proj, 128, PROJ_CKV), _rows(proj, 128, PROJ_KROPE),
                                 _whole(q_a_norm_g), _whole(kv_a_norm_g), _rows(cos), _rows(sin)],
                                [("rows", Q_LORA, BF16), ("rows", KV_LORA, BF16), ("rows", QK_ROPE, BF16)])
    def q_heads_fn(q, cs, sn):
        q = q * MLA_PRESCALE
        outs = []
        for h in range(H_A):
            o = H_A * QK_NOPE + QK_ROPE * h
            r1, r2 = _rope(q[:, o:o + half], q[:, o + half:o + QK_ROPE], cs, sn)
            outs.append(jnp.concatenate([q[:, QK_NOPE * h:QK_NOPE * (h + 1)], r1, r2], axis=1)[None])
        return (jnp.concatenate(outs, axis=0),)

    (q_full,) = _matmul(qn, w_qb_pt, "nt", None, "q_up_heads", tm=512, tn=1536,
                        epi=(q_heads_fn, [_rows(cos), _rows(sin)], [("heads", H_A, QK_HEAD, BF16)]))

    def k_heads_fn(kvf, kr):
        return kvf, jnp.concatenate([jnp.concatenate([kvf[:, 256 * h:256 * h + QK_NOPE], kr], axis=1)[None] for h in range(H_A)], axis=0)

    kv, k_full = _matmul(ckvn, w_kvb, "nn", None, "kv_up_heads", tm=512, tn=2048,
                         epi=(k_heads_fn, [_rows(k_rope)], [("rows", H_A * (QK_NOPE + V_DIM), BF16), ("heads", H_A, QK_HEAD, BF16)]))
    (o_a, lse), (g_out, g_up, g_cw) = _mla_fwd(q_full, k_full, kv, S, comm=([], late_weights))
    w_out_f = g_out.reshape(D_MODEL, D_MODEL)
    w_up_t = g_up.reshape(2 * D_FF, D_MODEL)
    conv_w_f = _unstack_cols(g_cw)

    (o_b, mixed), (g_down,) = _win_fwd(proj, o_a, bias_t, sinks_b, S, comm=([], [w_down[0].astype(BF16)]))
    w_down_f = g_down.reshape(D_FF, D_MODEL)

    x1, h2 = _matmul(mixed, w_out_f, "nn", None, "out_proj", residual=x, tm=512,
                     epi=(lambda a, g: (a, _rms(a, g)), [_whole(norm2_g)], [("rows", D_MODEL, F32), ("rows", D_MODEL, BF16)]))
    u = _matmul(h2, w_up_t, "nt", BF16, "ffn_up", tn=1408)
    act = _conv_gate_fwd(u, conv_w_f, conv_b, S)

    def final_fn(a, g, t):
        err = _rms(a, g) - t
        loss = 0.5 * jnp.sum(jnp.mean(err * err, axis=-1, keepdims=True), axis=0, keepdims=True)
        dx, dg = _rms_bwd(err * (1.0 / D_MODEL), a, g)
        return dx, dx, dg, jnp.broadcast_to(loss, (1, 128))

    gfin = final_norm_g.reshape(1, D_MODEL)
    dx2, dx2_b, d_gfin, loss_row = _matmul(
        act, w_down_f, "nn", None, "ffn_down_loss", residual=x1, tm=512,
        epi=(final_fn, [_whole(gfin), _rows(target)],
             [("rows", D_MODEL, F32), ("rows", D_MODEL, BF16), ("acc", 1, D_MODEL), ("acc", 1, 128)]))
    d_act = _matmul(dx2_b, w_down_f, "nt", BF16, "ffn_down_dx", tn=1408)
    d_w_down = _matmul(act, dx2_b, "tn", BF16, "ffn_down_dw")
    du_g, du_v, dcw_g, dcw_v, dcb_g, dcb_v = _conv_gate_bwd(u, conv_w_f, conv_b, d_act, S)
    d_conv_b = jnp.concatenate([dcb_g, dcb_v], axis=1)

    def norm_bwd_fn(dh, a, g, dres):
        dx, dg = _rms_bwd(dh, a, g)
        dx = dx + dres
        return dx, dx, dg

    dx1, dx1_b, d_g2 = _matmul(du_g, w_up_t, "nn", None, "ffn_up_dx_norm2_bwd", tm=256, a2=du_v,
                               epi=(norm_bwd_fn, [_rows(x1), _whole(norm2_g), _rows(dx2)],
                                    [("rows", D_MODEL, F32), ("rows", D_MODEL, BF16), ("acc", 1, D_MODEL)]))
    d_w_up_t = _matmul(du_g, h2, "tn", BF16, "ffn_up_dw", tm=256, a2=du_v)
    d_w_out = _matmul(mixed, dx1_b, "tn", BF16, "out_proj_dw", tm=512)

    def gate_bwd_fn(dm, ga, gb, oa, ob):
        sa, sb = _sigmoid(ga), _sigmoid(gb)
        return jnp.concatenate([dm * oa * sa * (1.0 - sa), dm * ob * sb * (1.0 - sb)], axis=1), dm * sa, dm * sb

    d_proj, do_a, do_b = _matmul(
        dx1_b, w_out_f, "nt", None, "out_proj_dx_gate_bwd", tm=512,
        epi=(gate_bwd_fn, [_rows(proj, 1024, PROJ_GA), _rows(proj, 1024, PROJ_GB), _rows(o_a), _rows(o_b)],
             [("cols", 2 * D_MODEL, 0, PROJ_P, BF16), ("rows", D_MODEL, BF16), ("rows", D_MODEL, BF16)]))

    d_proj, dk_acc, dv_acc, d_bias, d_sinks_g = _win_bwd(proj, bias_t, sinks_b, do_b, d_proj, S)
    d_sinks = d_sinks_g.reshape(1, H_B)

    early = [d_w_out.reshape(N_DEV, D_MODEL // N_DEV, D_MODEL), d_w_up_t.reshape(N_DEV, 2 * D_FF // N_DEV, D_MODEL),
             d_w_down.reshape(N_DEV, D_FF // N_DEV, D_MODEL), _stack_halves(dcw_g, dcw_v)]
    (dq_nope, dqr_heads, dkv, dkr_heads), recv_early = _mla_bwd(q_full, k_full, kv, do_a, o_a, lse, S, comm=(early, []))

    def dqr_post_fn(dqr, cs, sn):
        rope = []
        for h in range(H_A):
            rope += list(_rope_bwd(dqr[h, :, :half], dqr[h, :, half:], cs, sn))
        return (jnp.concatenate(rope, axis=1),)

    (dq_rope,) = _rowwise(dqr_post_fn, "dqr_post", S, TM, [_heads(dqr_heads), _rows(cos), _rows(sin)], [("rows", H_A * QK_ROPE, BF16)])

    def dkr_post_fn(dkr_h, cs, sn):
        dkr = dkr_h[0]
        for h in range(1, H_A):
            dkr = dkr + dkr_h[h]
        r1, r2 = _rope_bwd(dkr[:, :half], dkr[:, half:], cs, sn)
        return (jnp.concatenate([r1, r2], axis=1),)

    (d_krope,) = _rowwise(dkr_post_fn, "dkr_post", S, TM, [_heads(dkr_heads), _rows(cos), _rows(sin)], [("rows", QK_ROPE, F32)])
    d_qn = _matmul(dq_nope, w_qb_pt, "nn", F32, "q_up_dx", a2=dq_rope)
    d_w_qb_pt = _matmul(dq_nope, qn, "tn", BF16, "q_up_dw", tm=512, a2=dq_rope)
    d_ckvn = _matmul(dkv, w_kvb, "nt", F32, "kv_up_dx")
    d_w_kvb = _matmul(ckvn, dkv, "tn", BF16, "kv_up_dw", tn=2048)

    def lat_bwd_fn(dqn, dckvn, dkr, qlat, ckv, gq, gkv, dkb, dvb):
        dql, dgq = _rms_bwd(dqn, qlat, gq)
        dck, dgkv = _rms_bwd(dckvn, ckv, gkv)
        tail = jnp.concatenate([dql, dkb, dvb, dck, dkr, jnp.zeros_like(dkr)], axis=1)
        return tail, dgq, dgkv

    shifted = lambda arr: (arr, lambda tm: pl.BlockSpec((tm, arr.shape[1]), lambda i, *_: (i + WINDOW // tm, 0)))
    TL = min(128, S)
    d_proj, d_gq, d_gkv = _rowwise(lat_bwd_fn, "latents_bwd", S, TL,
                                   [_rows(d_qn), _rows(d_ckvn), _rows(d_krope), _rows(proj, 256, PROJ_QLAT), _rows(proj, 128, PROJ_CKV),
                                    _whole(q_a_norm_g), _whole(kv_a_norm_g), shifted(dk_acc), shifted(dv_acc)],
                                   [("cols", 1024, 3, PROJ_P, BF16), ("acc", 1, Q_LORA), ("acc", 1, KV_LORA)], into=(d_proj, 0))
    d_w_qb_t = jnp.concatenate([d_w_qb_pt[:H_A * QK_NOPE].reshape(H_A, QK_NOPE, Q_LORA),
                                d_w_qb_pt[H_A * QK_NOPE:].reshape(H_A, QK_ROPE, Q_LORA)], axis=1)
    dp, recv_mid = _matmul(d_proj, h1, "tn", BF16, "proj_dw", tm=512, comm=([d_w_qb_t, _stack_cols(d_w_kvb)], []))

    late = jnp.concatenate([dp[3072:3328], dp[3840:3968], dp[3968:4032], dp[2048:3072], dp[3328:3584],
                            dp[3584:3840], dp[0:1024], dp[1024:2048]], axis=0).reshape(N_DEV, W_IN_COLS // N_DEV, D_MODEL)
    late_sems, late_src, late_land, started = _exchange_start(late, "late_grads_start")

    def norm1_bwd_fn(dh, a, g, dres):
        dx, dg = _rms_bwd(dh, a, g)
        return dx + dres, dg

    grad_x, d_g1 = _matmul(
        d_proj, w_in_pt, "nn", None, "proj_dx_norm1_bwd", tm=512,
        epi=(norm1_bwd_fn, [_rows(x), _whole(norm1_g + started[:1, :1]), _rows(dx1)], [("rows", D_MODEL, F32), ("acc", 1, D_MODEL)]))

    transposed = ("w_in", "w_q_b", "w_up")
    ready_names = ["w_out", "w_up", "w_down", "conv_w", "w_q_b", "w_kv_b"]
    ready_wmv = [(w_out, m_w_out, v_w_out), (tr(w_up), tr(m_w_up), tr(v_w_up)), (w_down, m_w_down, v_w_down), (conv_w, m_conv_w, v_conv_w),
                 (tr(w_q_b), tr(m_w_q_b), tr(v_w_q_b)), (w_kv_b, m_w_kv_b, v_w_kv_b)]
    big = {n: _adamw(r, *wmv, "adamw_" + n) for n, r, wmv in zip(ready_names, [*recv_early, *recv_mid], ready_wmv)}

    d_bias, _ = lax.optimization_barrier((d_bias, started))
    d_rel_bias = _bias_table_bwd(d_bias.reshape(H_B, Q_BLOCK * SPAN), onehot_t).T

    after = lax.optimization_barrier([big[n][0] for n in ready_names] + [d_rel_bias])
    landed = _exchange_wait(late_sems, late_src, late_land, after[-1], "late_grads_wait")
    me = 4 * lax.axis_index("x") + 2 * lax.axis_index("y") + lax.axis_index("c")
    landed = lax.dynamic_update_slice_in_dim(landed, lax.dynamic_slice_in_dim(late, me, 1, axis=0), me, axis=0)
    big["w_in"] = _adamw(landed, tr(w_in), tr(m_w_in), tr(v_w_in), "adamw_w_in")

    small_parts = [d_g1, d_gq, d_gkv, d_rel_bias.reshape(1, NUM_BUCKETS * H_B), d_sinks, d_g2, d_conv_b, d_gfin, loss_row[:, :1]]
    small = jnp.concatenate(small_parts, axis=1)
    n_small = small.shape[1]
    pad = (-n_small) % 128
    small = jnp.pad(small, ((0, 0), (0, pad)))
    small, _ = lax.optimization_barrier((small, [landed, *after]))
    (recv_small,) = _exchange([], [small], "exchange_small_grads")

    def flat(a):
        return a.reshape(1, -1)

    small_w = [norm1_g, q_a_norm_g, kv_a_norm_g, rel_bias, sinks, norm2_g, conv_b, final_norm_g]
    small_m = [m_norm1_g, m_q_a_norm_g, m_kv_a_norm_g, m_rel_bias, m_sinks, m_norm2_g, m_conv_b, m_final_norm_g]
    small_v = [v_norm1_g, v_q_a_norm_g, v_kv_a_norm_g, v_rel_bias, v_sinks, v_norm2_g, v_conv_b, v_final_norm_g]
    cat = lambda parts: jnp.pad(jnp.concatenate([flat(a) for a in parts], axis=1), ((0, 0), (0, pad + 1)))[None]
    sm = _adamw(recv_small, cat(small_w), cat(small_m), cat(small_v), "adamw_small")

    loss = sm[0][0, 0, n_small - 1]
    order =["norm1_g", "w_in", "q_a_norm_g", "w_q_b", "kv_a_norm_g", "w_kv_b", "rel_bias", "sinks", "w_out", "norm2_g", "w_up",
             "conv_w", "conv_b", "w_down", "final_norm_g"]
    small_names = ["norm1_g", "q_a_norm_g", "kv_a_norm_g", "rel_bias", "sinks", "norm2_g", "conv_b", "final_norm_g"]
    offs, o = {}, 0
    for n, a in zip(small_names, small_w):
        offs[n] = (o, a.size, a.shape)
        o += a.size
    outs = [loss, grad_x[None]]
    for kind in range(4):
        for n in order:
            if n in big:
                outs.append(tr(big[n][kind]) if n in transposed else big[n][kind])
            else:
                o, size, shape = offs[n]
                outs.append(sm[kind][0, 0, o:o + size].reshape(shape))
    return tuple(outs)
```

```python
import math

import jax
import jax.numpy as jnp
from jax import lax
from jax.experimental import pallas as pl
from jax.experimental.pallas import tpu as pltpu

F32 = jnp.float32
BF16 = jnp.bfloat16

N_DEV = 8
D_MODEL = 1024
EPS = 1e-6
H_A, QK_NOPE, QK_ROPE, V_DIM, Q_LORA, KV_LORA = 8, 128, 64, 128, 256, 128
QK_HEAD = QK_NOPE + QK_ROPE
ROPE_THETA = 10000.0
H_B, KV_B, GROUP, HD_B, WINDOW, Q_BLOCK = 16, 4, 4, 64, 128, 128
SPAN = Q_BLOCK + 2 * WINDOW
NUM_BUCKETS, MAX_DISTANCE = 32, 128
D_FF = 2816
ADAM_LR, ADAM_B1, ADAM_B2, ADAM_EPS, ADAM_WD, ADAM_STEP = 0.001, 0.9, 0.999, 1e-08, 0.01, 10

W_IN_SIZES = (Q_LORA, KV_LORA + QK_ROPE, H_B * HD_B, KV_B * HD_B, KV_B * HD_B, D_MODEL, D_MODEL)
W_IN_COLS = sum(W_IN_SIZES)
PROJ_P = 4096
PROJ_GA, PROJ_GB, PROJ_QB, PROJ_QLAT, PROJ_KB, PROJ_VB, PROJ_CKV, PROJ_KROPE = 0, 1, 2, 12, 13, 14, 30, 31

VMEM_LIMIT = 56 * 1024 * 1024

NN = (((1,), (0,)), ((), ()))
NT = (((1,), (1,)), ((), ()))
TN = (((0,), (0,)), ((), ()))


def _pcall(body, *, name, grid, in_specs, out_specs, out_shape, scratch_shapes=(), dims=None, comm=None, aliases=None, two_level=False):
    if comm is None:
        params = pltpu.CompilerParams(dimension_semantics=dims, vmem_limit_bytes=VMEM_LIMIT)
        return pl.pallas_call(body, name=name, grid=grid, in_specs=in_specs, out_specs=out_specs, out_shape=out_shape,
                              scratch_shapes=list(scratch_shapes), input_output_aliases=aliases or {}, compiler_params=params)
    assert not aliases
    stacked, replicated = comm
    arrs = [*stacked, *replicated]
    n_st, n_arr = len(stacked), len(arrs)
    single = not isinstance(out_specs, (list, tuple))
    o_specs, o_shape = ([out_specs], [out_shape]) if single else (list(out_specs), list(out_shape))
    n_in, n_out = len(in_specs), len(o_specs)

    def wrapped(*refs):
        c_in = refs[n_in:n_in + n_arr]
        c_out = refs[n_in + n_arr + n_out:n_in + 2 * n_arr + n_out]
        sems = refs[len(refs) - 3:]
        own = (*refs[:n_in], *refs[n_in + n_arr:n_in + n_arr + n_out], *refs[n_in + 2 * n_arr + n_out:len(refs) - 3])
        if two_level:
            assert n_st == 0
            start, finish = (lambda: _gather2(c_in, c_out, sems, False)), (lambda: _gather2(c_in, c_out, sems, True))
        else:
            start, finish = (lambda: _xchg_start(c_in, c_out, sems, n_st)), (lambda: _xchg_finish(c_in, c_out, sems, n_st))
        if not grid:
            start()
            finish()
            return
        first = last = None
        for d, n in enumerate(grid):
            pid = pl.program_id(d)
            first = (pid == 0) if first is None else first & (pid == 0)
            last = (pid == n - 1) if last is None else last & (pid == n - 1)

        pl.when(first)(start)
        body(*own)
        pl.when(last)(finish)

    params = pltpu.CompilerParams(dimension_semantics=("arbitrary",) * len(grid), vmem_limit_bytes=VMEM_LIMIT)
    call = pl.pallas_call(wrapped, name=name, grid=grid, in_specs=[*in_specs, *[ANY] * n_arr], out_specs=[*o_specs, *[ANY] * n_arr],
                          out_shape=[*o_shape, *_xchg_out_shapes(stacked, replicated)],
                          scratch_shapes=[*scratch_shapes, *_xchg_sems(n_arr)], compiler_params=params)

    def run(*args):
        res = call(*args, *arrs)
        outs, landed = res[:n_out], res[n_out:]
        return (outs[0] if single else outs), landed

    return run


def _dot(a, b, dn):
    return lax.dot_general(a, b, dn, preferred_element_type=F32)


def _tile(n, target):
    best = None
    for t in range(128, min(n, target) + 1, 128):
        if n % t == 0:
            best = t
    return n if best is None else best


def _matmul(a, b, mode, out_dtype, name, residual=None, tm=1024, tn=1024, comm=None, a2=None, epi=None):
    if mode == "nn":
        (M, K), N = a.shape, b.shape[1]
    elif mode == "nt":
        (M, K), N = a.shape, b.shape[0]
    else:
        (K, M), N = a.shape, b.shape[1]
    tm, tn = _tile(M, tm), _tile(N, tn)
    a_spec = pl.BlockSpec((K, tm), lambda i, j: (0, i)) if mode == "tn" else pl.BlockSpec((tm, K), lambda i, j: (i, 0))
    b_spec = pl.BlockSpec((tn, b.shape[1]), lambda i, j: (j, 0)) if mode == "nt" else pl.BlockSpec((K, tn), lambda i, j: (0, j))
    o_spec = pl.BlockSpec((tm, tn), lambda i, j: (i, j))
    in_specs, args = [a_spec, b_spec], [a, b]
    n1 = M // tm
    if a2 is not None and mode == "tn":
        assert M % tm == 0 and a2.shape[1] % tm == 0
        in_specs[0] = pl.BlockSpec((K, tm), lambda i, j: (0, jnp.minimum(i, n1 - 1)))
        in_specs.append(pl.BlockSpec((K, tm), lambda i, j: (0, jnp.maximum(i - n1, 0))))
        args.append(a2)
        M += a2.shape[1]
    elif a2 is not None:
        assert (mode == "nt" and K + a2.shape[1] == b.shape[1]) or (mode == "nn" and K + a2.shape[1] == b.shape[0])
        if mode == "nn":
            b_spec = in_specs[1] = pl.BlockSpec((b.shape[0], tn), lambda i, j: (0, j))
        in_specs.append(pl.BlockSpec((tm, a2.shape[1]), lambda i, j: (i, 0)))
        args.append(a2)
    if residual is not None:
        in_specs.append(o_spec)
        args.append(residual)
    n_mm = len(args)
    scratch = [pltpu.VMEM((tm, K), a.dtype)] if mode == "tn" else []
    if epi is None:
        out_specs, out_shape, is_acc = o_spec, jax.ShapeDtypeStruct((M, N), out_dtype), None
    else:
        assert tn == N
        fn, epi_ins, epi_outs = epi
        in_specs += [mk(tm) for _, mk in epi_ins]
        args += [arr for arr, _ in epi_ins]
        out_specs, out_shape, is_acc = _row_out_specs(epi_outs, M, tm)

    def body(*refs):
        a_ref, b_ref = refs[0], refs[1]
        n_out = 1 if epi is None else len(is_acc)
        out_refs = refs[len(args):len(args) + n_out]
        if mode == "tn":
            at_ref = refs[len(args) + n_out]

            first_col = pl.program_id(1) == 0
            from_a = first_col if a2 is None else first_col & (pl.program_id(0) < n1)

            @pl.when(from_a)
            def _():
                at_ref[...] = a_ref[...].T

            if a2 is not None:
                @pl.when(first_col & (pl.program_id(0) >= n1))
                def _():
                    at_ref[...] = refs[2][...].T

            acc = _dot(at_ref[...], b_ref[...], NN)
        elif a2 is not None and mode == "nt":
            acc = _dot(a_ref[...], b_ref[:, :K], NT) + _dot(refs[2][...], b_ref[:, K:], NT)
        elif a2 is not None:
            acc = _dot(a_ref[...], b_ref[:K, :], NN) + _dot(refs[2][...], b_ref[K:, :], NN)
        else:
            acc = _dot(a_ref[...], b_ref[...], NT if mode == "nt" else NN)
        if residual is not None:
            acc = acc + refs[n_mm - 1][...]
        if epi is None:
            out_refs[0][...] = acc.astype(out_dtype)
        else:
            _store_rows(out_refs, fn(acc, *[_load_f32(r) for r in refs[n_mm:len(args)]]), is_acc)

    return _pcall(body, name=name, grid=(M // tm, N // tn), in_specs=in_specs, out_specs=out_specs,
                  out_shape=out_shape, scratch_shapes=scratch,
                  dims=("arbitrary" if epi is not None else "parallel", "arbitrary"), comm=comm)(*args)


def _rows(arr, width=None, col=0):
    width = arr.shape[1] if width is None else width
    return (arr, lambda tm: pl.BlockSpec((tm, width), lambda i, *_: (i, col)))


def _heads(arr):
    return (arr, lambda tm: pl.BlockSpec((arr.shape[0], tm, arr.shape[2]), lambda i, *_: (0, i, 0)))


def _whole(arr):
    nd = arr.ndim
    return (arr, lambda tm: pl.BlockSpec(arr.shape, lambda i, *_: (0,) * nd))


def _row_out_specs(outs, n_rows, tm):
    out_specs, out_shape, is_acc = [], [], []
    for o in outs:
        if o[0] == "rows":
            out_specs.append(pl.BlockSpec((tm, o[1]), lambda i, *_: (i, 0)))
            out_shape.append(jax.ShapeDtypeStruct((n_rows, o[1]), o[2]))
        elif o[0] == "cols":
            out_specs.append(pl.BlockSpec((tm, o[1]), lambda i, *_, c=o[2]: (i, c)))
            out_shape.append(jax.ShapeDtypeStruct((n_rows, o[3]), o[4]))
        elif o[0] == "heads":
            out_specs.append(pl.BlockSpec((o[1], tm, o[2]), lambda i, *_: (0, i, 0)))
            out_shape.append(jax.ShapeDtypeStruct((o[1], n_rows, o[2]), o[3]))
        else:
            out_specs.append(pl.BlockSpec((o[1], o[2]), lambda i, *_: (0, 0)))
            out_shape.append(jax.ShapeDtypeStruct((o[1], o[2]), F32))
        is_acc.append(o[0] == "acc")
    return out_specs, out_shape, is_acc


def _load_f32(r):
    v = r[...]
    return v.astype(F32) if v.dtype == BF16 else v


def _store_rows(out_refs, vals, is_acc):
    for r, v, acc in zip(out_refs, vals, is_acc):
        if acc:
            @pl.when(pl.program_id(0) == 0)
            def _():
                r[...] = jnp.zeros_like(r)

            r[...] += v
        else:
            r[...] = v.astype(r.dtype)


def _rowwise(fn, name, n_rows, tm, ins, outs, upcast=True, into=None, gather=None):
    tm = min(tm, n_rows)
    assert n_rows % tm == 0
    in_specs = [mk(tm) for _, mk in ins]
    out_specs, out_shape, is_acc = _row_out_specs(outs, n_rows, tm)
    n_in = len(ins)
    args = [a for a, _ in ins]
    aliases = {}
    if into is not None:
        in_specs.append(ANY)
        args.append(into[0])
        aliases = {n_in: into[1]}

    def body(*refs):
        vals = fn(*[_load_f32(r) if upcast else r[...] for r in refs[:n_in]])
        _store_rows(refs[len(args):], vals, is_acc)

    return _pcall(body, name=name, grid=(n_rows // tm,), in_specs=in_specs, out_specs=out_specs,
                  out_shape=out_shape, dims=("arbitrary",), aliases=aliases,
                  comm=None if gather is None else ([], gather), two_level=True)(*args)


def _rms(x, g):
    r = lax.rsqrt(jnp.mean(x * x, axis=-1, keepdims=True) + EPS)
    return x * r * g


def _rms_bwd(dy, x, g):
    r = lax.rsqrt(jnp.mean(x * x, axis=-1, keepdims=True) + EPS)
    xhat = x * r
    dxhat = dy * g
    dx = r * (dxhat - xhat * jnp.mean(dxhat * xhat, axis=-1, keepdims=True))
    return dx, jnp.sum(dy * xhat, axis=0, keepdims=True)


def _rope(x1, x2, cos, sin):
    return x1 * cos - x2 * sin, x2 * cos + x1 * sin


def _rope_bwd(d1, d2, cos, sin):
    return d1 * cos + d2 * sin, d2 * cos - d1 * sin


def _sigmoid(x):
    return 1.0 / (1.0 + jnp.exp(-x))


MLA_SCALE = 1.0 / math.sqrt(QK_HEAD)
MLA_PRESCALE = MLA_SCALE * math.log2(math.e)
MLA_TQ, MLA_KC = 1024, 1024


def _mla_fwd(q_full, k_full, kv, S, comm=None):
    tq, kc = min(MLA_TQ, S), min(MLA_KC, S)

    def body(q_ref, k_ref, v_ref, o_ref, lse_ref):
        q = q_ref[0]
        m = jnp.full((tq, 1), -1e30, F32)
        l = jnp.zeros((tq, 1), F32)
        acc = jnp.zeros((tq, V_DIM), F32)
        for c in range(S // kc):
            s = _dot(q, k_ref[0, c * kc:(c + 1) * kc, :], NT)
            m_new = jnp.maximum(m, jnp.max(s, axis=-1, keepdims=True))
            alpha = jnp.exp2(m - m_new)
            p = jnp.exp2(s - m_new)
            l = alpha * l + jnp.sum(p, axis=-1, keepdims=True)
            acc = alpha * acc + _dot(p.astype(BF16), v_ref[c * kc:(c + 1) * kc, :], NN)
            m = m_new
        o_ref[...] = (acc / l).astype(BF16)
        lse_ref[0] = m + jnp.log2(l)

    return _pcall(
        body, name="mla_fwd", grid=(H_A, S // tq),
        in_specs=[pl.BlockSpec((1, tq, QK_HEAD), lambda h, i: (h, i, 0)),
                  pl.BlockSpec((1, S, QK_HEAD), lambda h, i: (h, 0, 0)),
                  pl.BlockSpec((S, V_DIM), lambda h, i: (0, 2 * h + 1))],
        out_specs=[pl.BlockSpec((tq, V_DIM), lambda h, i: (i, h)),
                   pl.BlockSpec((1, tq, 1), lambda h, i: (h, i, 0))],
        out_shape=[jax.ShapeDtypeStruct((S, H_A * V_DIM), BF16), jax.ShapeDtypeStruct((H_A, S, 1), F32)],
        dims=("parallel", "parallel"), comm=comm)(q_full, k_full, kv)


def _mla_bwd(q_full, k_full, kv, do_a, o_a, lse, S, comm=None):
    tq, kc = min(MLA_TQ, S), min(MLA_KC, S)

    def body(q_ref, k_ref, v_ref, do_ref, o_ref, lse_ref, dqn_ref, dqr_ref, dkv_out, dkr_out, dk_ref, dv_ref):
        @pl.when(pl.program_id(1) == 0)
        def _():
            dk_ref[...] = jnp.zeros_like(dk_ref)
            dv_ref[...] = jnp.zeros_like(dv_ref)

        q = q_ref[0]
        do = do_ref[...]
        lse_q = lse_ref[0]
        delta = jnp.sum(do.astype(F32) * o_ref[...].astype(F32), axis=-1, keepdims=True)
        dq = jnp.zeros((tq, QK_HEAD), F32)
        for c in range(S // kc):
            k = k_ref[0, c * kc:(c + 1) * kc, :]
            v = v_ref[c * kc:(c + 1) * kc, :]
            p = jnp.exp2(_dot(q, k, NT) - lse_q)
            ds = (p * (_dot(do, v, NT) - delta)).astype(BF16)
            dq = dq + _dot(ds, k, NN)
            dk_ref[0, c * kc:(c + 1) * kc, :] += _dot(ds, q, TN)
            dv_ref[0, c * kc:(c + 1) * kc, :] += _dot(p.astype(BF16), do, TN)
        dq = dq * MLA_SCALE
        dqn_ref[...] = dq[:, :QK_NOPE].astype(BF16)
        dqr_ref[0] = dq[:, QK_NOPE:].astype(BF16)

        @pl.when(pl.program_id(1) == S // tq - 1)
        def _():
            dk = dk_ref[0] * math.log(2.0)
            dkv_out[...] = jnp.concatenate([dk[:, :QK_NOPE], dv_ref[0]], axis=1).astype(BF16)
            dkr_out[0] = dk[:, QK_NOPE:].astype(BF16)

    return _pcall(
        body, name="mla_bwd", grid=(H_A, S // tq),
        in_specs=[pl.BlockSpec((1, tq, QK_HEAD), lambda h, i: (h, i, 0)),
                  pl.BlockSpec((1, S, QK_HEAD), lambda h, i: (h, 0, 0)),
                  pl.BlockSpec((S, V_DIM), lambda h, i: (0, 2 * h + 1)),
                  pl.BlockSpec((tq, V_DIM), lambda h, i: (i, h)),
                  pl.BlockSpec((tq, V_DIM), lambda h, i: (i, h)),
                  pl.BlockSpec((1, tq, 1), lambda h, i: (h, i, 0))],
        out_specs=[pl.BlockSpec((tq, QK_NOPE), lambda h, i: (i, h)),
                   pl.BlockSpec((1, tq, QK_ROPE), lambda h, i: (h, i, 0)),
                   pl.BlockSpec((S, QK_NOPE + V_DIM), lambda h, i: (0, h)),
                   pl.BlockSpec((1, S, QK_ROPE), lambda h, i: (h, 0, 0))],
        out_shape=[jax.ShapeDtypeStruct((S, H_A * QK_NOPE), BF16), jax.ShapeDtypeStruct((H_A, S, QK_ROPE), BF16),
                   jax.ShapeDtypeStruct((S, H_A * (QK_NOPE + V_DIM)), BF16), jax.ShapeDtypeStruct((H_A, S, QK_ROPE), BF16)],
        scratch_shapes=[pltpu.VMEM((1, S, QK_HEAD), F32), pltpu.VMEM((1, S, V_DIM), F32)],
        dims=("parallel", "arbitrary"), comm=comm)(q_full, k_full, kv, do_a, o_a, lse)


WIN_SCALE = 1.0 / math.sqrt(HD_B)


WIN_PER_STEP = 4


def _win_specs(S):
    last, B = S // Q_BLOCK - 1, WIN_PER_STEP
    qspec = pl.BlockSpec((B * Q_BLOCK, H_B * HD_B), lambda i: (i, PROJ_QB))
    kspecs = [[pl.BlockSpec((Q_BLOCK, KV_B * HD_B), lambda i, d=d, c=c: (jnp.clip(B * i + d, 0, last), c)) for d in range(-1, B + 1)]
              for c in (PROJ_KB, PROJ_VB)]
    bias_spec = pl.BlockSpec((H_B, SPAN, Q_BLOCK), lambda i: (0, 0, 0))
    sink_spec = pl.BlockSpec((H_B, Q_BLOCK), lambda i: (0, 0))
    return qspec, kspecs, bias_spec, sink_spec


def _win_edge_ok(n, n_blk):
    row = lax.broadcasted_iota(jnp.int32, (SPAN, 1), 0)
    return jnp.logical_not(((n == 0) & (row < WINDOW)) | ((n == n_blk - 1) & (row >= SPAN - WINDOW)))


def _lanes4(pieces):
    return jnp.concatenate(pieces, axis=1)


def _win_probs(kg, q4t, bias_ref, sink_ref, g, edge_ok):
    bias4 = _lanes4([bias_ref[GROUP * g + j] for j in range(GROUP)])
    sink4 = _lanes4([sink_ref[GROUP * g + j:GROUP * g + j + 1, :] for j in range(GROUP)])
    s = jnp.where(edge_ok, _dot(kg, q4t, NN) + bias4, -1e30)
    m = jnp.maximum(jnp.max(s, axis=0, keepdims=True), sink4)
    p = jnp.exp(s - m)
    e_sink = jnp.exp(sink4 - m)
    inv_l = 1.0 / (jnp.sum(p, axis=0, keepdims=True) + e_sink)
    return p * inv_l, e_sink * inv_l


def _group_t(xt, g):
    return _lanes4([xt[HD_B * (GROUP * g + j):HD_B * (GROUP * g + j + 1), :] for j in range(GROUP)])


def _rows_of(ref, b):
    return ref[Q_BLOCK * b:Q_BLOCK * (b + 1), :]


def _win_fwd(proj, o_a, bias_t, sinks_b, S, comm=None):
    n_blk, B = S // Q_BLOCK, WIN_PER_STEP
    qspec, kspecs, bias_spec, sink_spec = _win_specs(S)
    rows = lambda col: pl.BlockSpec((B * Q_BLOCK, H_B * HD_B), lambda i: (i, col))

    def body(q_ref, *refs):
        k_refs, v_refs = refs[:B + 2], refs[B + 2:2 * B + 4]
        bias_ref, sink_ref, ga_ref, gb_ref, oa_ref, o_ref, mixed_ref = refs[2 * B + 4:]
        for b in range(B):
            edge_ok = _win_edge_ok(B * pl.program_id(0) + b, n_blk)
            k = jnp.concatenate([r[...] for r in k_refs[b:b + 3]], axis=0)
            vt = jnp.concatenate([r[...] for r in v_refs[b:b + 3]], axis=0).T
            qt = (_rows_of(q_ref, b).astype(F32) * WIN_SCALE).T.astype(BF16)
            parts = []
            for g in range(KV_B):
                p, _ = _win_probs(k[:, HD_B * g:HD_B * (g + 1)], _group_t(qt, g), bias_ref, sink_ref, g, edge_ok)
                o4t = _dot(vt[HD_B * g:HD_B * (g + 1), :], p.astype(BF16), NN)
                parts += [o4t[:, Q_BLOCK * j:Q_BLOCK * (j + 1)] for j in range(GROUP)]
            ob = jnp.concatenate(parts, axis=0).T
            o_ref[Q_BLOCK * b:Q_BLOCK * (b + 1), :] = ob.astype(BF16)
            ga, gb, oa = (_rows_of(r, b).astype(F32) for r in (ga_ref, gb_ref, oa_ref))
            mixed_ref[Q_BLOCK * b:Q_BLOCK * (b + 1), :] = (_sigmoid(ga) * oa + _sigmoid(gb) * ob).astype(BF16)

    return _pcall(body, name="win_fwd_mix", grid=(n_blk // B,),
                  in_specs=[qspec, *kspecs[0], *kspecs[1], bias_spec, sink_spec, rows(PROJ_GA), rows(PROJ_GB), rows(0)],
                  out_specs=[rows(0), rows(0)],
                  out_shape=[jax.ShapeDtypeStruct((S, H_B * HD_B), BF16)] * 2,
                  dims=("parallel",), comm=comm)(*[proj] * (2 * B + 5), bias_t, sinks_b, proj, proj, o_a)


def _win_bwd(proj, bias_t, sinks_b, do_b, d_proj, S):
    n_blk, B = S // Q_BLOCK, WIN_PER_STEP
    qspec, kspecs, bias_spec, sink_spec = _win_specs(S)

    def body(q_ref, *refs):
        k_refs, v_refs = refs[:B + 2], refs[B + 2:2 * B + 4]
        bias_ref, sink_ref, do_ref, _, dq_ref, dk_ref, dv_ref, dbias_ref, dsink_ref, dsink_acc = refs[2 * B + 4:]
        i = pl.program_id(0)

        @pl.when(i == 0)
        def _():
            dk_ref[...] = jnp.zeros_like(dk_ref)
            dv_ref[...] = jnp.zeros_like(dv_ref)
            dbias_ref[...] = jnp.zeros_like(dbias_ref)
            dsink_acc[...] = jnp.zeros_like(dsink_acc)

        d_bias, d_sink, dk_blocks, dv_blocks = {}, {}, [], []
        for b in range(B):
            edge_ok = _win_edge_ok(B * i + b, n_blk)
            k = jnp.concatenate([r[...] for r in k_refs[b:b + 3]], axis=0)
            v = jnp.concatenate([r[...] for r in v_refs[b:b + 3]], axis=0)
            kt = k.T
            qt = (_rows_of(q_ref, b).astype(F32) * WIN_SCALE).T.astype(BF16)
            dot_ = _rows_of(do_ref, b).astype(F32).T.astype(BF16)
            dq_parts, dks, dvs = [], [], []
            for g in range(KV_B):
                kg, vg = k[:, HD_B * g:HD_B * (g + 1)], v[:, HD_B * g:HD_B * (g + 1)]
                q4t, do4t = _group_t(qt, g), _group_t(dot_, g)
                p, p_sink = _win_probs(kg, q4t, bias_ref, sink_ref, g, edge_ok)
                dp = _dot(vg, do4t, NN)
                delta = jnp.sum(p * dp, axis=0, keepdims=True)
                ds = p * (dp - delta)
                d_bias[g] = ds if b == 0 else d_bias[g] + ds
                d_sink[g] = -p_sink * delta if b == 0 else d_sink[g] - p_sink * delta
                dsb = ds.astype(BF16)
                dq4t = _dot(kt[HD_B * g:HD_B * (g + 1), :], dsb, NN) * WIN_SCALE
                dq_parts += [dq4t[:, Q_BLOCK * j:Q_BLOCK * (j + 1)] for j in range(GROUP)]
                dks.append(_dot(dsb, q4t, NT))
                dvs.append(_dot(p.astype(BF16), do4t, NT))
            dq_ref[Q_BLOCK * b:Q_BLOCK * (b + 1), :] = jnp.concatenate(dq_parts, axis=0).T.astype(BF16)
            dk_blocks.append(jnp.concatenate(dks, axis=1))
            dv_blocks.append(jnp.concatenate(dvs, axis=1))

        for g in range(KV_B):
            for j in range(GROUP):
                dbias_ref[GROUP * g + j] += d_bias[g][:, Q_BLOCK * j:Q_BLOCK * (j + 1)]
            dsink_acc[g:g + 1, :] += d_sink[g]

        def overlap(blocks):
            out = blocks[0]
            for blk in blocks[1:]:
                keep = out.shape[0] - 2 * Q_BLOCK
                out = jnp.concatenate([out[:keep], out[keep:] + blk[:2 * Q_BLOCK], blk[2 * Q_BLOCK:]], axis=0)
            return out

        rows = pl.ds(pl.multiple_of(i * (B * Q_BLOCK), B * Q_BLOCK), (B + 2) * Q_BLOCK)
        dk_ref[rows, :] += overlap(dk_blocks)
        dv_ref[rows, :] += overlap(dv_blocks)

        @pl.when(i == n_blk // B - 1)
        def _():
            acc = dsink_acc[...]
            dsink_ref[...] = jnp.concatenate(
                [jnp.sum(acc[:, Q_BLOCK * j:Q_BLOCK * (j + 1)], axis=1, keepdims=True) for j in range(GROUP)], axis=1)

    whole = lambda shape: pl.BlockSpec(shape, lambda i: (0,) * len(shape))
    return _pcall(
        body, name="win_bwd", grid=(n_blk // B,),
        in_specs=[qspec, *kspecs[0], *kspecs[1], bias_spec, sink_spec, pl.BlockSpec((B * Q_BLOCK, H_B * HD_B), lambda i: (i, 0)), ANY],
        out_specs=[qspec, whole((S + 2 * WINDOW, KV_B * HD_B)),
                   whole((S + 2 * WINDOW, KV_B * HD_B)), whole((H_B, SPAN, Q_BLOCK)), whole((KV_B, GROUP))],
        out_shape=[jax.ShapeDtypeStruct((S, PROJ_P), BF16), jax.ShapeDtypeStruct((S + 2 * WINDOW, KV_B * HD_B), F32),
                   jax.ShapeDtypeStruct((S + 2 * WINDOW, KV_B * HD_B), F32), jax.ShapeDtypeStruct((H_B, SPAN, Q_BLOCK), F32),
                   jax.ShapeDtypeStruct((KV_B, GROUP), F32)],
        scratch_shapes=[pltpu.VMEM((KV_B, GROUP * Q_BLOCK), F32)],
        dims=("arbitrary",), aliases={2 * B + 8: 0})(*[proj] * (2 * B + 5), bias_t, sinks_b, do_b, d_proj)


def _bias_table(rel_bias_t, onehot_t, in_band):
    def body(rb_ref, oh_ref, band_ref, o_ref):
        t = lax.dot_general(rb_ref[...], oh_ref[...], NN, preferred_element_type=F32, precision=lax.Precision.HIGHEST)
        o_ref[...] = jnp.where(band_ref[...] > 0.5, t, -1e30)

    n = onehot_t.shape[1]
    tn = _tile(n, 8192)
    return _pcall(body, name="bias_table", grid=(n // tn,),
                  in_specs=[pl.BlockSpec((H_B, NUM_BUCKETS), lambda j: (0, 0)), pl.BlockSpec((NUM_BUCKETS, tn), lambda j: (0, j)),
                            pl.BlockSpec((1, tn), lambda j: (0, j))],
                  out_specs=pl.BlockSpec((H_B, tn), lambda j: (0, j)),
                  out_shape=jax.ShapeDtypeStruct((H_B, n), F32), dims=("parallel",))(rel_bias_t, onehot_t, in_band)


def _bias_table_bwd(dbias, onehot_t):
    n = onehot_t.shape[1]
    tk = _tile(n, 8192)

    def body(d_ref, oh_ref, o_ref):
        @pl.when(pl.program_id(0) == 0)
        def _():
            o_ref[...] = jnp.zeros_like(o_ref)

        o_ref[...] += lax.dot_general(d_ref[...], oh_ref[...], NT, preferred_element_type=F32, precision=lax.Precision.HIGHEST)

    return _pcall(body, name="bias_table_bwd", grid=(n // tk,),
                  in_specs=[pl.BlockSpec((H_B, tk), lambda j: (0, j)), pl.BlockSpec((NUM_BUCKETS, tk), lambda j: (0, j))],
                  out_specs=pl.BlockSpec((H_B, NUM_BUCKETS), lambda j: (0, 0)),
                  out_shape=jax.ShapeDtypeStruct((H_B, NUM_BUCKETS), F32), dims=("arbitrary",))(dbias, onehot_t)


CONV_STRIP = 128
N_STRIPS = D_FF // CONV_STRIP
CONV_ROWS = 128
HALO = 8


def _strip(rows, half):
    return pl.BlockSpec((rows, CONV_STRIP), lambda j: (0, j + half * N_STRIPS))


def _fill_padded(pad_ref, src_ref, halo, S):
    pad_ref[0:halo, :] = jnp.zeros((halo, CONV_STRIP), F32)
    pad_ref[halo + S:2 * halo + S, :] = jnp.zeros((halo, CONV_STRIP), F32)
    pad_ref[halo:halo + S, :] = src_ref[...].astype(F32)


def _conv_gate_fwd(u, conv_w, conv_b, S):
    R = min(CONV_ROWS, S)

    def body(ug_ref, uv_ref, wg_ref, wv_ref, bg_ref, bv_ref, a_ref, gpad, vpad):
        _fill_padded(gpad, ug_ref, HALO, S)
        _fill_padded(vpad, uv_ref, HALO, S)
        wg, wv, bg, bv = wg_ref[...], wv_ref[...], bg_ref[...], bv_ref[...]

        def conv(pad_ref, r0, w, b):
            dn, mid, up = (pad_ref[pl.ds(r0 + HALO + d, R), :] for d in (-1, 0, 1))
            return dn * w[0:1, :] + mid * w[1:2, :] + up * w[2:3, :] + b

        def step(c, carry):
            r0 = pl.multiple_of(c * R, R)
            g = conv(gpad, r0, wg, bg)
            val = conv(vpad, r0, wv, bv)
            a_ref[pl.ds(r0, R), :] = (g * _sigmoid(g) * val).astype(BF16)
            return carry

        lax.fori_loop(0, S // R, step, 0)

    return _pcall(body, name="conv_gate_fwd", grid=(N_STRIPS,),
                  in_specs=[_strip(S, 0), _strip(S, 1), _strip(3, 0), _strip(3, 1), _strip(1, 0), _strip(1, 1)],
                  out_specs=_strip(S, 0), out_shape=jax.ShapeDtypeStruct((S, D_FF), BF16),
                  scratch_shapes=[pltpu.VMEM((S + 2 * HALO, CONV_STRIP), F32)] * 2,
                  dims=("parallel",))(u, u, conv_w, conv_w, conv_b, conv_b)


def _conv_gate_bwd(u, conv_w, conv_b, da, S):
    R = min(CONV_ROWS, S)
    n = R + 2 * HALO

    def body(ug_ref, uv_ref, wg_ref, wv_ref, bg_ref, bv_ref, da_ref, dug_ref, duv_ref, dwg_ref, dwv_ref, dbg_ref, dbv_ref,
             gpad, vpad, dapad):
        _fill_padded(gpad, ug_ref, 2 * HALO, S)
        _fill_padded(vpad, uv_ref, 2 * HALO, S)
        _fill_padded(dapad, da_ref, HALO, S)
        wg, wv, bg, bv = wg_ref[...], wv_ref[...], bg_ref[...], bv_ref[...]

        def conv(pad_ref, r0, w, b):
            dn, mid, up = (pad_ref[pl.ds(r0 + HALO + d, n), :] for d in (-1, 0, 1))
            return dn * w[0:1, :] + mid * w[1:2, :] + up * w[2:3, :] + b, mid[HALO:HALO + R]

        def conv_bwd(duc, u_mid, w, r0, du_ref):
            dn, mid, up = pltpu.roll(duc, 1, axis=0)[HALO:HALO + R], duc[HALO:HALO + R], pltpu.roll(duc, n - 1, axis=0)[HALO:HALO + R]
            du_ref[pl.ds(r0, R), :] = (up * w[0:1, :] + mid * w[1:2, :] + dn * w[2:3, :]).astype(BF16)
            dw = jnp.concatenate([jnp.sum(up * u_mid, axis=0, keepdims=True), jnp.sum(mid * u_mid, axis=0, keepdims=True),
                                  jnp.sum(dn * u_mid, axis=0, keepdims=True)], axis=0)
            return dw, jnp.sum(mid, axis=0, keepdims=True)

        def step(c, carry):
            dw_g, db_g, dw_v, db_v = carry
            r0 = pl.multiple_of(c * R, R)
            g, ug_mid = conv(gpad, r0, wg, bg)
            val, uv_mid = conv(vpad, r0, wv, bv)
            da_ext = dapad[pl.ds(r0, n), :]
            sg = _sigmoid(g)
            ddw_v, ddb_v = conv_bwd(da_ext * (g * sg), uv_mid, wv, r0, duv_ref)
            ddw_g, ddb_g = conv_bwd(da_ext * val * (sg * (1.0 + g * (1.0 - sg))), ug_mid, wg, r0, dug_ref)
            return dw_g + ddw_g, db_g + ddb_g, dw_v + ddw_v, db_v + ddb_v

        z3, z1 = jnp.zeros((3, CONV_STRIP), F32), jnp.zeros((1, CONV_STRIP), F32)
        dwg_ref[...], dbg_ref[...], dwv_ref[...], dbv_ref[...] = lax.fori_loop(0, S // R, step, (z3, z1, z3, z1))

    half = lambda r, dt: (_strip(r, 0), jax.ShapeDtypeStruct((r, D_FF), dt))
    outs = [half(S, BF16), half(S, BF16), half(3, F32), half(3, F32), half(1, F32), half(1, F32)]
    return _pcall(
        body, name="conv_gate_bwd", grid=(N_STRIPS,),
        in_specs=[_strip(S, 0), _strip(S, 1), _strip(3, 0), _strip(3, 1), _strip(1, 0), _strip(1, 1), _strip(S, 0)],
        out_specs=[o[0] for o in outs], out_shape=[o[1] for o in outs],
        scratch_shapes=[pltpu.VMEM((S + 4 * HALO, CONV_STRIP), F32)] * 2 + [pltpu.VMEM((S + 2 * HALO, CONV_STRIP), F32)],
        dims=("parallel",))(u, u, conv_w, conv_w, conv_b, conv_b, da)


MESH = pl.DeviceIdType.MESH
ANY = pl.BlockSpec(memory_space=pl.ANY)


def _place():
    return lax.axis_index("x"), lax.axis_index("y"), lax.axis_index("c")


def _gather2(ins, outs, sems, finish):
    send_sems, recv_sems, local_sems = sems
    n_arr = len(ins)
    x, y, c = _place()
    me, sibling = (x, y, c), (x, y, 1 - c)
    chips = [(1 - x, y), (x, 1 - y), (1 - x, 1 - y)]

    def slot(a, p):
        return outs[a].at[4 * p[0] + 2 * p[1] + p[2]]

    def copy(a, k, block, to, src=None):
        return pltpu.make_async_remote_copy(
            src_ref=slot(a, block) if src is None else src, dst_ref=slot(a, block),
            send_sem=send_sems.at[a, k], recv_sem=recv_sems.at[a, k], device_id=to, device_id_type=MESH)

    mine = [pltpu.make_async_copy(ins[a], slot(a, me), local_sems.at[a]) for a in range(n_arr)]
    first = []
    for a in range(n_arr):
        first.append(copy(a, 0, me, sibling, src=ins[a]))
        first += [copy(a, 1 + j, me, (*chip, c), src=ins[a]) for j, chip in enumerate(chips)]
    if not finish:
        for cp in mine + first:
            cp.start()
        return
    passed = []
    for j, chip in enumerate(chips):
        for a in range(n_arr):
            copy(a, 1 + j, (*chip, c), me).wait_recv()
            cp = copy(a, 4 + j, (*chip, c), sibling)
            cp.start()
            passed.append(cp)
    for a in range(n_arr):
        copy(a, 0, sibling, me).wait_recv()
        for j, chip in enumerate(chips):
            copy(a, 4 + j, (*chip, 1 - c), me).wait_recv()
    for cp in first + passed:
        cp.wait_send()
    for cp in mine:
        cp.wait()


def _xchg_out_shapes(stacked, replicated):
    return ([jax.ShapeDtypeStruct(s.shape, s.dtype) for s in stacked]
            + [jax.ShapeDtypeStruct((N_DEV, *r.shape), r.dtype) for r in replicated])


def _xchg_sems(n_arr):
    return [pltpu.SemaphoreType.DMA((n_arr, 7)), pltpu.SemaphoreType.DMA((n_arr, 7)), pltpu.SemaphoreType.DMA((n_arr,))]


def _xchg_copies(ins, outs, sems, n_st, with_recv):
    send_sems, recv_sems, local_sems = sems
    n_arr = len(ins)
    x, y, c = _place()
    me = 4 * x + 2 * y + c

    def src(a, idx):
        return ins[a].at[idx] if a < n_st else ins[a]

    mine = [pltpu.make_async_copy(src(a, me), outs[a].at[me], local_sems.at[a]) for a in range(n_arr)]
    pairs = []
    for k in range(1, N_DEV):
        px, py, pc = x ^ (k >> 2), y ^ ((k >> 1) & 1), c ^ (k & 1)
        peer = 4 * px + 2 * py + pc
        for a in range(n_arr):
            sems_k = dict(send_sem=send_sems.at[a, k - 1], recv_sem=recv_sems.at[a, k - 1], device_id_type=MESH)
            send = pltpu.make_async_remote_copy(src_ref=src(a, peer), dst_ref=outs[a].at[me], device_id=(px, py, pc), **sems_k)
            recv = None
            if with_recv:
                recv = pltpu.make_async_remote_copy(src_ref=src(a, peer), dst_ref=outs[a].at[peer], device_id=(x, y, c), **sems_k)
            pairs.append((send, recv))
    return mine, pairs


def _xchg_start(ins, outs, sems, n_st):
    mine, pairs = _xchg_copies(ins, outs, sems, n_st, False)
    for cp in mine:
        cp.start()
    for send, _ in pairs:
        send.start()


def _xchg_finish(ins, outs, sems, n_st):
    mine, pairs = _xchg_copies(ins, outs, sems, n_st, True)
    for _, recv in pairs:
        recv.wait_recv()
    for send, _ in pairs:
        send.wait_send()
    for cp in mine:
        cp.wait()


def _exchange(stacked, replicated, name):
    _, landed = _pcall(lambda: None, name=name, grid=(), in_specs=[], out_specs=[], out_shape=[], comm=(stacked, replicated))()
    return landed


HBM = pl.BlockSpec(memory_space=pltpu.HBM)
SEMS = pl.BlockSpec(memory_space=pltpu.SEMAPHORE)
SIDE_EFFECT = pltpu.SideEffectType.DATAFLOW_SIDE_EFFECTING


N_SPLIT_SEMS = 2 * (N_DEV - 1)


def _split_copies(src, land, sems, with_recv):
    x, y, c = _place()
    me = 4 * x + 2 * y + c
    pairs = []
    for k in range(1, N_DEV):
        px, py, pc = x ^ (k >> 2), y ^ ((k >> 1) & 1), c ^ (k & 1)
        peer = 4 * px + 2 * py + pc
        sems_k = dict(send_sem=sems[k - 1], recv_sem=sems[N_DEV - 1 + k - 1], device_id_type=MESH)
        send = pltpu.make_async_remote_copy(src_ref=src.at[peer], dst_ref=land.at[me], device_id=(px, py, pc), **sems_k)
        recv = None
        if with_recv:
            recv = pltpu.make_async_remote_copy(src_ref=src.at[peer], dst_ref=land.at[peer], device_id=(x, y, c), **sems_k)
        pairs.append((send, recv))
    return pairs


def _exchange_start(stacked, name):
    def body(src, land, *rest):
        for send, _ in _split_copies(src, land, rest[:N_SPLIT_SEMS], False):
            send.start()
        rest[-1][...] = jnp.zeros_like(rest[-1])

    shape = pltpu.HBM(stacked.shape, stacked.dtype)
    res = pl.pallas_call(
        body, name=name, in_specs=[HBM, HBM],
        out_shape=(*[pltpu.SemaphoreType.DMA(())] * N_SPLIT_SEMS, shape, shape, jax.ShapeDtypeStruct((8, 128), F32)),
        out_specs=(*[SEMS] * N_SPLIT_SEMS, HBM, HBM, pl.BlockSpec(memory_space=pltpu.VMEM)),
        input_output_aliases={0: N_SPLIT_SEMS, 1: N_SPLIT_SEMS + 1},
        compiler_params=pltpu.CompilerParams(has_side_effects=SIDE_EFFECT),
    )(pltpu.with_memory_space_constraint(stacked, pltpu.HBM),
      pltpu.with_memory_space_constraint(lax.empty(stacked.shape, stacked.dtype), pltpu.HBM))
    return res[:N_SPLIT_SEMS], res[N_SPLIT_SEMS], res[N_SPLIT_SEMS + 1], res[-1]


def _exchange_wait(sems, src, land, after, name):
    def body(src_ref, land_ref, *rest):
        for send, recv in _split_copies(src_ref, land_ref, rest[:N_SPLIT_SEMS], True):
            send.wait_send()
            recv.wait_recv()

    shape = pltpu.HBM(src.shape, src.dtype)
    return pl.pallas_call(
        body, name=name, in_specs=[HBM, HBM, *[SEMS] * N_SPLIT_SEMS, ANY],
        out_shape=(shape, shape), out_specs=(HBM, HBM), input_output_aliases={0: 0, 1: 1},
        compiler_params=pltpu.CompilerParams(has_side_effects=SIDE_EFFECT))(src, land, *sems, after)[1]


def _adamw(parts, w, m, v, name):
    _, R, C = w.shape
    tr = R if R <= 512 else max(t for t in range(16, 513, 16) if R % t == 0)
    pr = tr if parts.shape[1] == R else -(-R // 16) * 16
    assert pr == tr or tr == R

    def body(p_ref, w_ref, m_ref, v_ref, g_ref, d_ref, nm_ref, nv_ref):
        g = p_ref[0].astype(F32)[:tr]
        for s in range(1, N_DEV):
            g = g + p_ref[s].astype(F32)[:tr]
        m2 = ADAM_B1 * m_ref[0] + (1.0 - ADAM_B1) * g
        v2 = ADAM_B2 * v_ref[0] + (1.0 - ADAM_B2) * (g * g)
        m_hat = m2 / (1.0 - ADAM_B1 ** ADAM_STEP)
        v_hat = v2 / (1.0 - ADAM_B2 ** ADAM_STEP)
        g_ref[0] = g
        d_ref[0] = -ADAM_LR * (m_hat / (jnp.sqrt(v_hat) + ADAM_EPS) + ADAM_WD * w_ref[0])
        nm_ref[0] = m2
        nv_ref[0] = v2

    blk = pl.BlockSpec((1, tr, C), lambda i: (0, i, 0))
    return _pcall(body, name=name, grid=(R // tr,),
                  in_specs=[pl.BlockSpec((N_DEV, pr, C), lambda i: (0, i, 0)), blk, blk, blk],
                  out_specs=[blk] * 4, out_shape=[jax.ShapeDtypeStruct((1, R, C), F32)] * 4,
                  dims=("parallel",))(parts, w, m, v)


def _t5_bucket(rel):
    nb = NUM_BUCKETS // 2
    max_exact = nb // 2
    base = (rel > 0).astype(jnp.int32) * nb
    n = jnp.abs(rel)
    nf = jnp.maximum(n, 1).astype(jnp.float32)
    large = max_exact + (jnp.log(nf / max_exact) / math.log(MAX_DISTANCE / max_exact) * (nb - max_exact)).astype(jnp.int32)
    large = jnp.minimum(large, nb - 1)
    return base + jnp.where(n < max_exact, n, large)


def _unstack_cols(g):
    return jnp.transpose(g, (1, 0, 2)).reshape(g.shape[1], N_DEV * g.shape[2])


def _stack_cols(w, n=N_DEV):
    R = w.shape[0]
    return jnp.transpose(w.reshape(R, n, w.shape[1] // n), (1, 0, 2))


def _stack_halves(g, v):
    return jnp.concatenate([_stack_cols(g, N_DEV // 2), _stack_cols(v, N_DEV // 2)], axis=0)


def kernel(x, positions, norm1_g, w_in, q_a_norm_g, w_q_b, kv_a_norm_g, w_kv_b, rel_bias, sinks, w_out, norm2_g, w_up, conv_w, conv_b, w_down, final_norm_g, loss_target, m_norm1_g, m_w_in, m_q_a_norm_g, m_w_q_b, m_kv_a_norm_g, m_w_kv_b, m_rel_bias, m_sinks, m_w_out, m_norm2_g, m_w_up, m_conv_w, m_conv_b, m_w_down, m_final_norm_g, v_norm1_g, v_w_in, v_q_a_norm_g, v_w_q_b, v_kv_a_norm_g, v_w_kv_b, v_rel_bias, v_sinks, v_w_out, v_norm2_g, v_w_up, v_conv_w, v_conv_b, v_w_down, v_final_norm_g):
    S = x.shape[1]
    x = x[0]
    target = loss_target[0]
    TM = 256

    tr = lambda w: jnp.swapaxes(w, 1, 2)
    (h1,), (g_in, g_qb, g_kvb) = _rowwise(lambda a, g: (_rms(a, g),), "norm1_gather", S, TM, [_rows(x), _whole(norm1_g)], [("rows", D_MODEL, BF16)],
                                          gather=[tr(w_in)[0].astype(BF16), tr(w_q_b)[0].astype(BF16), w_kv_b[0].astype(BF16)])
    late_weights = [w_out[0].astype(BF16), tr(w_up)[0].astype(BF16), conv_w[0]]
    wi = g_in.reshape(W_IN_COLS, D_MODEL)
    c0, c1, c2, c3, c4, c5 = (sum(W_IN_SIZES[:i + 1]) for i in range(6))
    w_in_pt = jnp.concatenate([wi[c4:c5], wi[c5:], wi[c1:c2], wi[:c0], wi[c2:c3], wi[c3:c4],
                               wi[c0:c0 + KV_LORA], wi[c0 + KV_LORA:c1], jnp.zeros((64, D_MODEL), BF16)], axis=0)
    wq = g_qb.reshape(H_A, QK_HEAD, Q_LORA)
    w_qb_pt = jnp.concatenate([wq[:, :QK_NOPE].reshape(H_A * QK_NOPE, Q_LORA), wq[:, QK_NOPE:].reshape(H_A * QK_ROPE, Q_LORA)], axis=0)
    w_kvb = _unstack_cols(g_kvb)

    half = QK_ROPE // 2
    inv_freq = ROPE_THETA ** (-jnp.arange(half, dtype=F32) / half)
    ang = positions.astype(F32)[:, None] * inv_freq[None, :]
    cos, sin = jnp.cos(ang), jnp.sin(ang)
    qa = jnp.arange(Q_BLOCK, dtype=jnp.int32)[:, None]
    kc = jnp.arange(SPAN, dtype=jnp.int32)[None, :]
    rel = (kc - WINDOW - qa).T
    in_band = (jnp.abs(rel) <= WINDOW).astype(F32).reshape(1, Q_BLOCK * SPAN)
    onehot_t = (_t5_bucket(rel).reshape(1, Q_BLOCK * SPAN) == jnp.arange(NUM_BUCKETS, dtype=jnp.int32)[:, None]).astype(F32)
    bias_t = _bias_table(rel_bias.T, onehot_t, in_band).reshape(H_B, SPAN, Q_BLOCK)
    sinks_b = jnp.broadcast_to(sinks.reshape(H_B, 1), (H_B, Q_BLOCK))

    proj =_matmul(h1, w_in_pt, "nt", BF16, "proj")

    def lat_fn(qlat, ckv, kr, gq, gkv, cs, sn):
        r1, r2 = _rope(kr[:, :half], kr[:, half:QK_ROPE], cs, sn)
        return _rms(qlat, gq), _rms(ckv, gkv), jnp.concatenate([r1, r2], axis=1)

    qn, ckvn, k_rope = _rowwise(lat_fn, "latents", S, TM,
                                [_rows(proj, 256, PROJ_QLAT), _rows(proj, 128, PROJ_CKV), _rows(proj, 128, PROJ_KROPE),
                                 _whole(q_a_norm_g), _whole(kv_a_norm_g), _rows(cos), _rows(sin)],
                                [("rows", Q_LORA, BF16), ("rows", KV_LORA, BF16), ("rows", QK_ROPE, BF16)])
    def q_heads_fn(q, cs, sn):
        q = q * MLA_PRESCALE
        outs = []
        for h in range(H_A):
            o = H_A * QK_NOPE + QK_ROPE * h
            r1, r2 = _rope(q[:, o:o + half], q[:, o + half:o + QK_ROPE], cs, sn)
            outs.append(jnp.concatenate([q[:, QK_NOPE * h:QK_NOPE * (h + 1)], r1, r2], axis=1)[None])
        return (jnp.concatenate(outs, axis=0),)

    (q_full,) = _matmul(qn, w_qb_pt, "nt", None, "q_up_heads", tm=512, tn=1536,
                        epi=(q_heads_fn, [_rows(cos), _rows(sin)], [("heads", H_A, QK_HEAD, BF16)]))

    def k_heads_fn(kvf, kr):
        return kvf, jnp.concatenate([jnp.concatenate([kvf[:, 256 * h:256 * h + QK_NOPE], kr], axis=1)[None] for h in range(H_A)], axis=0)

    kv, k_full = _matmul(ckvn, w_kvb, "nn", None, "kv_up_heads", tm=512, tn=2048,
                         epi=(k_heads_fn, [_rows(k_rope)], [("rows", H_A * (QK_NOPE + V_DIM), BF16), ("heads", H_A, QK_HEAD, BF16)]))
    (o_a, lse), (g_out, g_up, g_cw) = _mla_fwd(q_full, k_full, kv, S, comm=([], late_weights))
    w_out_f = g_out.reshape(D_MODEL, D_MODEL)
    w_up_t = g_up.reshape(2 * D_FF, D_MODEL)
    conv_w_f = _unstack_cols(g_cw)

    (o_b, mixed), (g_down,) = _win_fwd(proj, o_a, bias_t, sinks_b, S, comm=([], [w_down[0].astype(BF16)]))
    w_down_f = g_down.reshape(D_FF, D_MODEL)

    x1, h2 = _matmul(mixed, w_out_f, "nn", None, "out_proj", residual=x, tm=512,
                     epi=(lambda a, g: (a, _rms(a, g)), [_whole(norm2_g)], [("rows", D_MODEL, F32), ("rows", D_MODEL, BF16)]))
    u = _matmul(h2, w_up_t, "nt", BF16, "ffn_up", tn=1408)
    act = _conv_gate_fwd(u, conv_w_f, conv_b, S)

    def final_fn(a, g, t):
        err = _rms(a, g) - t
        loss = 0.5 * jnp.sum(jnp.mean(err * err, axis=-1, keepdims=True), axis=0, keepdims=True)
        dx, dg = _rms_bwd(err * (1.0 / D_MODEL), a, g)
        return dx, dx, dg, jnp.broadcast_to(loss, (1, 128))

    gfin = final_norm_g.reshape(1, D_MODEL)
    dx2, dx2_b, d_gfin, loss_row = _matmul(
        act, w_down_f, "nn", None, "ffn_down_loss", residual=x1, tm=512,
        epi=(final_fn, [_whole(gfin), _rows(target)],
             [("rows", D_MODEL, F32), ("rows", D_MODEL, BF16), ("acc", 1, D_MODEL), ("acc", 1, 128)]))
    d_act = _matmul(dx2_b, w_down_f, "nt", BF16, "ffn_down_dx", tn=1408)
    d_w_down = _matmul(act, dx2_b, "tn", BF16, "ffn_down_dw")
    du_g, du_v, dcw_g, dcw_v, dcb_g, dcb_v = _conv_gate_bwd(u, conv_w_f, conv_b, d_act, S)
    d_conv_b = jnp.concatenate([dcb_g, dcb_v], axis=1)

    def norm_bwd_fn(dh, a, g, dres):
        dx, dg = _rms_bwd(dh, a, g)
        dx = dx + dres
        return dx, dx, dg

    dx1, dx1_b, d_g2 = _matmul(du_g, w_up_t, "nn", None, "ffn_up_dx_norm2_bwd", tm=256, a2=du_v,
                               epi=(norm_bwd_fn, [_rows(x1), _whole(norm2_g), _rows(dx2)],
                                    [("rows", D_MODEL, F32), ("rows", D_MODEL, BF16), ("acc", 1, D_MODEL)]))
    d_w_up_t = _matmul(du_g, h2, "tn", BF16, "ffn_up_dw", tm=256, a2=du_v)
    d_w_out = _matmul(mixed, dx1_b, "tn", BF16, "out_proj_dw", tm=512)

    def gate_bwd_fn(dm, ga, gb, oa, ob):
        sa, sb = _sigmoid(ga), _sigmoid(gb)
        return jnp.concatenate([dm * oa * sa * (1.0 - sa), dm * ob * sb * (1.0 - sb)], axis=1), dm * sa, dm * sb

    d_proj, do_a, do_b = _matmul(
        dx1_b, w_out_f, "nt", None, "out_proj_dx_gate_bwd", tm=512,
        epi=(gate_bwd_fn, [_rows(proj, 1024, PROJ_GA), _rows(proj, 1024, PROJ_GB), _rows(o_a), _rows(o_b)],
             [("cols", 2 * D_MODEL, 0, PROJ_P, BF16), ("rows", D_MODEL, BF16), ("rows", D_MODEL, BF16)]))

    d_proj, dk_acc, dv_acc, d_bias, d_sinks_g = _win_bwd(proj, bias_t, sinks_b, do_b, d_proj, S)
    d_sinks = d_sinks_g.reshape(1, H_B)

    early = [d_w_out.reshape(N_DEV, D_MODEL // N_DEV, D_MODEL), d_w_up_t.reshape(N_DEV, 2 * D_FF // N_DEV, D_MODEL),
             d_w_down.reshape(N_DEV, D_FF // N_DEV, D_MODEL), _stack_halves(dcw_g, dcw_v)]
    (dq_nope, dqr_heads, dkv, dkr_heads), recv_early = _mla_bwd(q_full, k_full, kv, do_a, o_a, lse, S, comm=(early, []))

    def dqr_post_fn(dqr, cs, sn):
        rope = []
        for h in range(H_A):
            rope += list(_rope_bwd(dqr[h, :, :half], dqr[h, :, half:], cs, sn))
        return (jnp.concatenate(rope, axis=1),)

    (dq_rope,) = _rowwise(dqr_post_fn, "dqr_post", S, TM, [_heads(dqr_heads), _rows(cos), _rows(sin)], [("rows", H_A * QK_ROPE, BF16)])

    d_qn =_matmul(dq_nope, w_qb_pt, "nn", F32, "q_up_dx", a2=dq_rope)
    d_w_qb_pt = _matmul(dq_nope, qn, "tn", BF16, "q_up_dw", tm=512, a2=dq_rope)
    d_ckvn = _matmul(dkv, w_kvb, "nt", F32, "kv_up_dx")
    d_w_kvb = _matmul(ckvn, dkv, "tn", BF16, "kv_up_dw", tn=2048)

    def lat_bwd_fn(dqn, dckvn, dkr_h, cs, sn, qlat, ckv, gq, gkv, dkb, dvb):
        dql, dgq = _rms_bwd(dqn, qlat, gq)
        dck, dgkv = _rms_bwd(dckvn, ckv, gkv)
        dkr = dkr_h[0]
        for h in range(1, H_A):
            dkr = dkr + dkr_h[h]
        r1, r2 = _rope_bwd(dkr[:, :half], dkr[:, half:], cs, sn)
        tail = jnp.concatenate([dql, dkb, dvb, dck, r1, r2, jnp.zeros_like(dkr)], axis=1)
        return tail, dgq, dgkv

    shifted = lambda arr: (arr, lambda tm: pl.BlockSpec((tm, arr.shape[1]), lambda i, *_: (i + WINDOW // tm, 0)))
    TL = min(128, S)
    d_proj, d_gq, d_gkv = _rowwise(lat_bwd_fn, "latents_bwd", S, TL,
                                   [_rows(d_qn), _rows(d_ckvn), _heads(dkr_heads), _rows(cos), _rows(sin), _rows(proj, 256, PROJ_QLAT), _rows(proj, 128, PROJ_CKV),
                                    _whole(q_a_norm_g), _whole(kv_a_norm_g), shifted(dk_acc), shifted(dv_acc)],
                                   [("cols", 1024, 3, PROJ_P, BF16), ("acc", 1, Q_LORA), ("acc", 1, KV_LORA)], into=(d_proj, 0))
    d_w_qb_t = jnp.concatenate([d_w_qb_pt[:H_A * QK_NOPE].reshape(H_A, QK_NOPE, Q_LORA),
                                d_w_qb_pt[H_A * QK_NOPE:].reshape(H_A, QK_ROPE, Q_LORA)], axis=1)
    dp, recv_mid = _matmul(d_proj, h1, "tn", BF16, "proj_dw", tm=512, comm=([d_w_qb_t, _stack_cols(d_w_kvb)], []))

    late = jnp.concatenate([dp[3072:3328], dp[3840:3968], dp[3968:4032], dp[2048:3072], dp[3328:3584],
                            dp[3584:3840], dp[0:1024], dp[1024:2048]], axis=0).reshape(N_DEV, W_IN_COLS // N_DEV, D_MODEL)
    late_sems, late_src, late_land, started = _exchange_start(late, "late_grads_start")

    def norm1_bwd_fn(dh, a, g, dres):
        dx, dg = _rms_bwd(dh, a, g)
        return dx + dres, dg

    grad_x, d_g1 = _matmul(
        d_proj, w_in_pt, "nn", None, "proj_dx_norm1_bwd", tm=512,
        epi=(norm1_bwd_fn, [_rows(x), _whole(norm1_g + started[:1, :1]), _rows(dx1)], [("rows", D_MODEL, F32), ("acc", 1, D_MODEL)]))

    transposed = ("w_in", "w_q_b", "w_up")
    ready_names = ["w_out", "w_up", "w_down", "conv_w", "w_q_b", "w_kv_b"]
    ready_wmv = [(w_out, m_w_out, v_w_out), (tr(w_up), tr(m_w_up), tr(v_w_up)), (w_down, m_w_down, v_w_down), (conv_w, m_conv_w, v_conv_w),
                 (tr(w_q_b), tr(m_w_q_b), tr(v_w_q_b)), (w_kv_b, m_w_kv_b, v_w_kv_b)]
    big = {n: _adamw(r, *wmv, "adamw_" + n) for n, r, wmv in zip(ready_names, [*recv_early, *recv_mid], ready_wmv)}

    d_bias, _ = lax.optimization_barrier((d_bias, started))
    d_rel_bias = _bias_table_bwd(d_bias.reshape(H_B, Q_BLOCK * SPAN), onehot_t).T

    after = lax.optimization_barrier([big[n][0] for n in ready_names] + [d_rel_bias])
    landed = _exchange_wait(late_sems, late_src, late_land, after[-1], "late_grads_wait")
    me = 4 * lax.axis_index("x") + 2 * lax.axis_index("y") + lax.axis_index("c")
    landed = lax.dynamic_update_slice_in_dim(landed, lax.dynamic_slice_in_dim(late, me, 1, axis=0), me, axis=0)
    big["w_in"] = _adamw(landed, tr(w_in), tr(m_w_in), tr(v_w_in), "adamw_w_in")

    small_parts = [d_g1, d_gq, d_gkv, d_rel_bias.reshape(1, NUM_BUCKETS * H_B), d_sinks, d_g2, d_conv_b, d_gfin, loss_row[:, :1]]
    small = jnp.concatenate(small_parts, axis=1)
    n_small = small.shape[1]
    pad = (-n_small) % 128
    small = jnp.pad(small, ((0, 0), (0, pad)))
    small, _ = lax.optimization_barrier((small, [landed, *after]))
    (recv_small,) = _exchange([], [small], "exchange_small_grads")

    def flat(a):
        return a.reshape(1, -1)

    small_w = [norm1_g, q_a_norm_g, kv_a_norm_g, rel_bias, sinks, norm2_g, conv_b, final_norm_g]
    small_m = [m_norm1_g, m_q_a_norm_g, m_kv_a_norm_g, m_rel_bias, m_sinks, m_norm2_g, m_conv_b, m_final_norm_g]
    small_v = [v_norm1_g, v_q_a_norm_g, v_kv_a_norm_g, v_rel_bias, v_sinks, v_norm2_g, v_conv_b, v_final_norm_g]
    cat = lambda parts: jnp.pad(jnp.concatenate([flat(a) for a in parts], axis=1), ((0, 0), (0, pad + 1)))[None]
    sm = _adamw(recv_small, cat(small_w), cat(small_m), cat(small_v), "adamw_small")

    loss = sm[0][0, 0, n_small - 1]
    order =["norm1_g", "w_in", "q_a_norm_g", "w_q_b", "kv_a_norm_g", "w_kv_b", "rel_bias", "sinks", "w_out", "norm2_g", "w_up",
             "conv_w", "conv_b", "w_down", "final_norm_g"]
    small_names = ["norm1_g", "q_a_norm_g", "kv_a_norm_g", "rel_bias", "sinks", "norm2_g", "conv_b", "final_norm_g"]
    offs, o = {}, 0
    for n, a in zip(small_names, small_w):
        offs[n] = (o, a.size, a.shape)
        o += a.size
    outs = [loss, grad_x[None]]
    for kind in range(4):
        for n in order:
            if n in big:
                outs.append(tr(big[n][kind]) if n in transposed else big[n][kind])
            else:
                o, size, shape = offs[n]
                outs.append(sm[kind][0, 0, o:o + size].reshape(shape))
    return tuple(outs)
```

```python
import math

import jax
import jax.numpy as jnp
from jax import lax
from jax.experimental import pallas as pl
from jax.experimental.pallas import tpu as pltpu

F32 = jnp.float32
BF16 = jnp.bfloat16

N_DEV = 8
D_MODEL = 1024
EPS = 1e-6
H_A, QK_NOPE, QK_ROPE, V_DIM, Q_LORA, KV_LORA = 8, 128, 64, 128, 256, 128
QK_HEAD = QK_NOPE + QK_ROPE
ROPE_THETA = 10000.0
H_B, KV_B, GROUP, HD_B, WINDOW, Q_BLOCK = 16, 4, 4, 64, 128, 128
SPAN = Q_BLOCK + 2 * WINDOW
NUM_BUCKETS, MAX_DISTANCE = 32, 128
D_FF = 2816
ADAM_LR, ADAM_B1, ADAM_B2, ADAM_EPS, ADAM_WD, ADAM_STEP = 0.001, 0.9, 0.999, 1e-08, 0.01, 10

W_IN_SIZES = (Q_LORA, KV_LORA + QK_ROPE, H_B * HD_B, KV_B * HD_B, KV_B * HD_B, D_MODEL, D_MODEL)
W_IN_COLS = sum(W_IN_SIZES)
PROJ_P = 4096
PROJ_GA, PROJ_GB, PROJ_QB, PROJ_QLAT, PROJ_KB, PROJ_VB, PROJ_CKV, PROJ_KROPE = 0, 1, 2, 12, 13, 14, 30, 31

VMEM_LIMIT = 56 * 1024 * 1024

NN = (((1,), (0,)), ((), ()))
NT = (((1,), (1,)), ((), ()))
TN = (((0,), (0,)), ((), ()))


def _pcall(body, *, name, grid, in_specs, out_specs, out_shape, scratch_shapes=(), dims=None, comm=None, aliases=None, two_level=False):
    if comm is None:
        params = pltpu.CompilerParams(dimension_semantics=dims, vmem_limit_bytes=VMEM_LIMIT)
        return pl.pallas_call(body, name=name, grid=grid, in_specs=in_specs, out_specs=out_specs, out_shape=out_shape,
                              scratch_shapes=list(scratch_shapes), input_output_aliases=aliases or {}, compiler_params=params)
    assert not aliases
    stacked, replicated = comm
    arrs = [*stacked, *replicated]
    n_st, n_arr = len(stacked), len(arrs)
    single = not isinstance(out_specs, (list, tuple))
    o_specs, o_shape = ([out_specs], [out_shape]) if single else (list(out_specs), list(out_shape))
    n_in, n_out = len(in_specs), len(o_specs)

    def wrapped(*refs):
        c_in = refs[n_in:n_in + n_arr]
        c_out = refs[n_in + n_arr + n_out:n_in + 2 * n_arr + n_out]
        sems = refs[len(refs) - 3:]
        own = (*refs[:n_in], *refs[n_in + n_arr:n_in + n_arr + n_out], *refs[n_in + 2 * n_arr + n_out:len(refs) - 3])
        if two_level:
            assert n_st == 0
            start, finish = (lambda: _gather2(c_in, c_out, sems, False)), (lambda: _gather2(c_in, c_out, sems, True))
        else:
            start, finish = (lambda: _xchg_start(c_in, c_out, sems, n_st)), (lambda: _xchg_finish(c_in, c_out, sems, n_st))
        if not grid:
            start()
            finish()
            return
        first = last = None
        for d, n in enumerate(grid):
            pid = pl.program_id(d)
            first = (pid == 0) if first is None else first & (pid == 0)
            last = (pid == n - 1) if last is None else last & (pid == n - 1)

        pl.when(first)(start)
        body(*own)
        pl.when(last)(finish)

    params = pltpu.CompilerParams(dimension_semantics=("arbitrary",) * len(grid), vmem_limit_bytes=VMEM_LIMIT)
    call = pl.pallas_call(wrapped, name=name, grid=grid, in_specs=[*in_specs, *[ANY] * n_arr], out_specs=[*o_specs, *[ANY] * n_arr],
                          out_shape=[*o_shape, *_xchg_out_shapes(stacked, replicated)],
                          scratch_shapes=[*scratch_shapes, *_xchg_sems(n_arr)], compiler_params=params)

    def run(*args):
        res = call(*args, *arrs)
        outs, landed = res[:n_out], res[n_out:]
        return (outs[0] if single else outs), landed

    return run


def _dot(a, b, dn):
    return lax.dot_general(a, b, dn, preferred_element_type=F32)


def _tile(n, target):
    best = None
    for t in range(128, min(n, target) + 1, 128):
        if n % t == 0:
            best = t
    return n if best is None else best


def _matmul(a, b, mode, out_dtype, name, residual=None, tm=1024, tn=1024, comm=None, a2=None, epi=None):
    if mode == "nn":
        (M, K), N = a.shape, b.shape[1]
    elif mode == "nt":
        (M, K), N = a.shape, b.shape[0]
    else:
        (K, M), N = a.shape, b.shape[1]
    tm, tn = _tile(M, tm), _tile(N, tn)
    a_spec = pl.BlockSpec((K, tm), lambda i, j: (0, i)) if mode == "tn" else pl.BlockSpec((tm, K), lambda i, j: (i, 0))
    b_spec = pl.BlockSpec((tn, b.shape[1]), lambda i, j: (j, 0)) if mode == "nt" else pl.BlockSpec((K, tn), lambda i, j: (0, j))
    o_spec = pl.BlockSpec((tm, tn), lambda i, j: (i, j))
    in_specs, args = [a_spec, b_spec], [a, b]
    n1 = M // tm
    if a2 is not None and mode == "tn":
        assert M % tm == 0 and a2.shape[1] % tm == 0
        in_specs[0] = pl.BlockSpec((K, tm), lambda i, j: (0, jnp.minimum(i, n1 - 1)))
        in_specs.append(pl.BlockSpec((K, tm), lambda i, j: (0, jnp.maximum(i - n1, 0))))
        args.append(a2)
        M += a2.shape[1]
    elif a2 is not None:
        assert (mode == "nt" and K + a2.shape[1] == b.shape[1]) or (mode == "nn" and K + a2.shape[1] == b.shape[0])
        if mode == "nn":
            b_spec = in_specs[1] = pl.BlockSpec((b.shape[0], tn), lambda i, j: (0, j))
        in_specs.append(pl.BlockSpec((tm, a2.shape[1]), lambda i, j: (i, 0)))
        args.append(a2)
    if residual is not None:
        in_specs.append(o_spec)
        args.append(residual)
    n_mm = len(args)
    scratch = [pltpu.VMEM((tm, K), a.dtype)] if mode == "tn" else []
    if epi is None:
        out_specs, out_shape, is_acc = o_spec, jax.ShapeDtypeStruct((M, N), out_dtype), None
    else:
        assert tn == N
        fn, epi_ins, epi_outs = epi
        in_specs += [mk(tm) for _, mk in epi_ins]
        args += [arr for arr, _ in epi_ins]
        out_specs, out_shape, is_acc = _row_out_specs(epi_outs, M, tm)

    def body(*refs):
        a_ref, b_ref = refs[0], refs[1]
        n_out = 1 if epi is None else len(is_acc)
        out_refs = refs[len(args):len(args) + n_out]
        if mode == "tn":
            at_ref = refs[len(args) + n_out]

            first_col = pl.program_id(1) == 0
            from_a = first_col if a2 is None else first_col & (pl.program_id(0) < n1)

            @pl.when(from_a)
            def _():
                at_ref[...] = a_ref[...].T

            if a2 is not None:
                @pl.when(first_col & (pl.program_id(0) >= n1))
                def _():
                    at_ref[...] = refs[2][...].T

            acc = _dot(at_ref[...], b_ref[...], NN)
        elif a2 is not None and mode == "nt":
            acc = _dot(a_ref[...], b_ref[:, :K], NT) + _dot(refs[2][...], b_ref[:, K:], NT)
        elif a2 is not None:
            acc = _dot(a_ref[...], b_ref[:K, :], NN) + _dot(refs[2][...], b_ref[K:, :], NN)
        else:
            acc = _dot(a_ref[...], b_ref[...], NT if mode == "nt" else NN)
        if residual is not None:
            acc = acc + refs[n_mm - 1][...]
        if epi is None:
            out_refs[0][...] = acc.astype(out_dtype)
        else:
            _store_rows(out_refs, fn(acc, *[_load_f32(r) for r in refs[n_mm:len(args)]]), is_acc)

    return _pcall(body, name=name, grid=(M // tm, N // tn), in_specs=in_specs, out_specs=out_specs,
                  out_shape=out_shape, scratch_shapes=scratch,
                  dims=("arbitrary" if epi is not None else "parallel", "arbitrary"), comm=comm)(*args)


def _rows(arr, width=None, col=0):
    width = arr.shape[1] if width is None else width
    return (arr, lambda tm: pl.BlockSpec((tm, width), lambda i, *_: (i, col)))


def _heads(arr):
    return (arr, lambda tm: pl.BlockSpec((arr.shape[0], tm, arr.shape[2]), lambda i, *_: (0, i, 0)))


def _whole(arr):
    nd = arr.ndim
    return (arr, lambda tm: pl.BlockSpec(arr.shape, lambda i, *_: (0,) * nd))


def _row_out_specs(outs, n_rows, tm):
    out_specs, out_shape, is_acc = [], [], []
    for o in outs:
        if o[0] == "rows":
            out_specs.append(pl.BlockSpec((tm, o[1]), lambda i, *_: (i, 0)))
            out_shape.append(jax.ShapeDtypeStruct((n_rows, o[1]), o[2]))
        elif o[0] == "cols":
            out_specs.append(pl.BlockSpec((tm, o[1]), lambda i, *_, c=o[2]: (i, c)))
            out_shape.append(jax.ShapeDtypeStruct((n_rows, o[3]), o[4]))
        elif o[0] == "heads":
            out_specs.append(pl.BlockSpec((o[1], tm, o[2]), lambda i, *_: (0, i, 0)))
            out_shape.append(jax.ShapeDtypeStruct((o[1], n_rows, o[2]), o[3]))
        else:
            out_specs.append(pl.BlockSpec((o[1], o[2]), lambda i, *_: (0, 0)))
            out_shape.append(jax.ShapeDtypeStruct((o[1], o[2]), F32))
        is_acc.append(o[0] == "acc")
    return out_specs, out_shape, is_acc


def _load_f32(r):
    v = r[...]
    return v.astype(F32) if v.dtype == BF16 else v


def _store_rows(out_refs, vals, is_acc):
    for r, v, acc in zip(out_refs, vals, is_acc):
        if acc:
            @pl.when(pl.program_id(0) == 0)
            def _():
                r[...] = jnp.zeros_like(r)

            r[...] += v
        else:
            r[...] = v.astype(r.dtype)


def _rowwise(fn, name, n_rows, tm, ins, outs, upcast=True, into=None, gather=None):
    tm = min(tm, n_rows)
    assert n_rows % tm == 0
    in_specs = [mk(tm) for _, mk in ins]
    out_specs, out_shape, is_acc = _row_out_specs(outs, n_rows, tm)
    n_in = len(ins)
    args = [a for a, _ in ins]
    aliases = {}
    if into is not None:
        in_specs.append(ANY)
        args.append(into[0])
        aliases = {n_in: into[1]}

    def body(*refs):
        vals = fn(*[_load_f32(r) if upcast else r[...] for r in refs[:n_in]])
        _store_rows(refs[len(args):], vals, is_acc)

    return _pcall(body, name=name, grid=(n_rows // tm,), in_specs=in_specs, out_specs=out_specs,
                  out_shape=out_shape, dims=("arbitrary",), aliases=aliases,
                  comm=None if gather is None else ([], gather), two_level=True)(*args)


def _rms(x, g):
    r = lax.rsqrt(jnp.mean(x * x, axis=-1, keepdims=True) + EPS)
    return x * r * g


def _rms_bwd(dy, x, g):
    r = lax.rsqrt(jnp.mean(x * x, axis=-1, keepdims=True) + EPS)
    xhat = x * r
    dxhat = dy * g
    dx = r * (dxhat - xhat * jnp.mean(dxhat * xhat, axis=-1, keepdims=True))
    return dx, jnp.sum(dy * xhat, axis=0, keepdims=True)


def _rope(x1, x2, cos, sin):
    return x1 * cos - x2 * sin, x2 * cos + x1 * sin


def _rope_bwd(d1, d2, cos, sin):
    return d1 * cos + d2 * sin, d2 * cos - d1 * sin


def _sigmoid(x):
    return 1.0 / (1.0 + jnp.exp(-x))


MLA_SCALE = 1.0 / math.sqrt(QK_HEAD)
MLA_PRESCALE = MLA_SCALE * math.log2(math.e)
MLA_TQ, MLA_KC = 1024, 1024


def _mla_fwd(q_full, k_full, kv, S, comm=None):
    tq, kc = min(MLA_TQ, S), min(MLA_KC, S)

    def body(q_ref, k_ref, v_ref, o_ref, lse_ref):
        q = q_ref[0]
        m = jnp.full((tq, 1), -1e30, F32)
        l = jnp.zeros((tq, 1), F32)
        acc = jnp.zeros((tq, V_DIM), F32)
        for c in range(S // kc):
            s = _dot(q, k_ref[0, c * kc:(c + 1) * kc, :], NT)
            m_new = jnp.maximum(m, jnp.max(s, axis=-1, keepdims=True))
            alpha = jnp.exp2(m - m_new)
            p = jnp.exp2(s - m_new)
            l = alpha * l + jnp.sum(p, axis=-1, keepdims=True)
            acc = alpha * acc + _dot(p.astype(BF16), v_ref[c * kc:(c + 1) * kc, :], NN)
            m = m_new
        o_ref[...] = (acc / l).astype(BF16)
        lse_ref[0] = m + jnp.log2(l)

    return _pcall(
        body, name="mla_fwd", grid=(H_A, S // tq),
        in_specs=[pl.BlockSpec((1, tq, QK_HEAD), lambda h, i: (h, i, 0)),
                  pl.BlockSpec((1, S, QK_HEAD), lambda h, i: (h, 0, 0)),
                  pl.BlockSpec((S, V_DIM), lambda h, i: (0, 2 * h + 1))],
        out_specs=[pl.BlockSpec((tq, V_DIM), lambda h, i: (i, h)),
                   pl.BlockSpec((1, tq, 1), lambda h, i: (h, i, 0))],
        out_shape=[jax.ShapeDtypeStruct((S, H_A * V_DIM), BF16), jax.ShapeDtypeStruct((H_A, S, 1), F32)],
        dims=("parallel", "parallel"), comm=comm)(q_full, k_full, kv)


def _mla_bwd(q_full, k_full, kv, do_a, o_a, lse, S, comm=None):
    tq, kc = min(MLA_TQ, S), min(MLA_KC, S)

    def body(q_ref, k_ref, v_ref, do_ref, o_ref, lse_ref, dqn_ref, dqr_ref, dkv_out, dkr_out, dk_ref, dv_ref):
        @pl.when(pl.program_id(1) == 0)
        def _():
            dk_ref[...] = jnp.zeros_like(dk_ref)
            dv_ref[...] = jnp.zeros_like(dv_ref)

        q = q_ref[0]
        do = do_ref[...]
        lse_q = lse_ref[0]
        delta = jnp.sum(do.astype(F32) * o_ref[...].astype(F32), axis=-1, keepdims=True)
        dq = jnp.zeros((tq, QK_HEAD), F32)
        for c in range(S // kc):
            k = k_ref[0, c * kc:(c + 1) * kc, :]
            v = v_ref[c * kc:(c + 1) * kc, :]
            p = jnp.exp2(_dot(q, k, NT) - lse_q)
            ds = (p * (_dot(do, v, NT) - delta)).astype(BF16)
            dq = dq + _dot(ds, k, NN)
            dk_ref[0, c * kc:(c + 1) * kc, :] += _dot(ds, q, TN)
            dv_ref[0, c * kc:(c + 1) * kc, :] += _dot(p.astype(BF16), do, TN)
        dq = dq * MLA_SCALE
        dqn_ref[...] = dq[:, :QK_NOPE].astype(BF16)
        dqr_ref[0] = dq[:, QK_NOPE:].astype(BF16)

        @pl.when(pl.program_id(1) == S // tq - 1)
        def _():
            dk = dk_ref[0] * math.log(2.0)
            dkv_out[...] = jnp.concatenate([dk[:, :QK_NOPE], dv_ref[0]], axis=1).astype(BF16)
            dkr_out[0] = dk[:, QK_NOPE:].astype(BF16)

    return _pcall(
        body, name="mla_bwd", grid=(H_A, S // tq),
        in_specs=[pl.BlockSpec((1, tq, QK_HEAD), lambda h, i: (h, i, 0)),
                  pl.BlockSpec((1, S, QK_HEAD), lambda h, i: (h, 0, 0)),
                  pl.BlockSpec((S, V_DIM), lambda h, i: (0, 2 * h + 1)),
                  pl.BlockSpec((tq, V_DIM), lambda h, i: (i, h)),
                  pl.BlockSpec((tq, V_DIM), lambda h, i: (i, h)),
                  pl.BlockSpec((1, tq, 1), lambda h, i: (h, i, 0))],
        out_specs=[pl.BlockSpec((tq, QK_NOPE), lambda h, i: (i, h)),
                   pl.BlockSpec((1, tq, QK_ROPE), lambda h, i: (h, i, 0)),
                   pl.BlockSpec((S, QK_NOPE + V_DIM), lambda h, i: (0, h)),
                   pl.BlockSpec((1, S, QK_ROPE), lambda h, i: (h, 0, 0))],
        out_shape=[jax.ShapeDtypeStruct((S, H_A * QK_NOPE), BF16), jax.ShapeDtypeStruct((H_A, S, QK_ROPE), BF16),
                   jax.ShapeDtypeStruct((S, H_A * (QK_NOPE + V_DIM)), BF16), jax.ShapeDtypeStruct((H_A, S, QK_ROPE), BF16)],
        scratch_shapes=[pltpu.VMEM((1, S, QK_HEAD), F32), pltpu.VMEM((1, S, V_DIM), F32)],
        dims=("parallel", "arbitrary"), comm=comm)(q_full, k_full, kv, do_a, o_a, lse)


WIN_SCALE = 1.0 / math.sqrt(HD_B)


WIN_PER_STEP = 4


def _win_specs(S):
    last, B = S // Q_BLOCK - 1, WIN_PER_STEP
    qspec = pl.BlockSpec((B * Q_BLOCK, H_B * HD_B), lambda i: (i, PROJ_QB))
    kspecs = [[pl.BlockSpec((Q_BLOCK, KV_B * HD_B), lambda i, d=d, c=c: (jnp.clip(B * i + d, 0, last), c)) for d in range(-1, B + 1)]
              for c in (PROJ_KB, PROJ_VB)]
    bias_spec = pl.BlockSpec((H_B, SPAN, Q_BLOCK), lambda i: (0, 0, 0))
    sink_spec = pl.BlockSpec((H_B, Q_BLOCK), lambda i: (0, 0))
    return qspec, kspecs, bias_spec, sink_spec


def _win_edge_ok(n, n_blk):
    row = lax.broadcasted_iota(jnp.int32, (SPAN, 1), 0)
    return jnp.logical_not(((n == 0) & (row < WINDOW)) | ((n == n_blk - 1) & (row >= SPAN - WINDOW)))


def _lanes4(pieces):
    return jnp.concatenate(pieces, axis=1)


def _win_probs(kg, q4t, bias_ref, sink_ref, g, edge_ok):
    bias4 = _lanes4([bias_ref[GROUP * g + j] for j in range(GROUP)])
    sink4 = _lanes4([sink_ref[GROUP * g + j:GROUP * g + j + 1, :] for j in range(GROUP)])
    s = jnp.where(edge_ok, _dot(kg, q4t, NN) + bias4, -1e30)
    m = jnp.maximum(jnp.max(s, axis=0, keepdims=True), sink4)
    p = jnp.exp(s - m)
    e_sink = jnp.exp(sink4 - m)
    inv_l = 1.0 / (jnp.sum(p, axis=0, keepdims=True) + e_sink)
    return p * inv_l, e_sink * inv_l


def _group_t(xt, g):
    return _lanes4([xt[HD_B * (GROUP * g + j):HD_B * (GROUP * g + j + 1), :] for j in range(GROUP)])


def _rows_of(ref, b):
    return ref[Q_BLOCK * b:Q_BLOCK * (b + 1), :]


def _win_fwd(proj, o_a, bias_t, sinks_b, S, comm=None):
    n_blk, B = S // Q_BLOCK, WIN_PER_STEP
    qspec, kspecs, bias_spec, sink_spec = _win_specs(S)
    rows = lambda col: pl.BlockSpec((B * Q_BLOCK, H_B * HD_B), lambda i: (i, col))

    def body(q_ref, *refs):
        k_refs, v_refs = refs[:B + 2], refs[B + 2:2 * B + 4]
        bias_ref, sink_ref, ga_ref, gb_ref, oa_ref, o_ref, mixed_ref = refs[2 * B + 4:]
        for b in range(B):
            edge_ok = _win_edge_ok(B * pl.program_id(0) + b, n_blk)
            k = jnp.concatenate([r[...] for r in k_refs[b:b + 3]], axis=0)
            vt = jnp.concatenate([r[...] for r in v_refs[b:b + 3]], axis=0).T
            qt = (_rows_of(q_ref, b).astype(F32) * WIN_SCALE).T.astype(BF16)
            parts = []
            for g in range(KV_B):
                p, _ = _win_probs(k[:, HD_B * g:HD_B * (g + 1)], _group_t(qt, g), bias_ref, sink_ref, g, edge_ok)
                o4t = _dot(vt[HD_B * g:HD_B * (g + 1), :], p.astype(BF16), NN)
                parts += [o4t[:, Q_BLOCK * j:Q_BLOCK * (j + 1)] for j in range(GROUP)]
            ob = jnp.concatenate(parts, axis=0).T
            o_ref[Q_BLOCK * b:Q_BLOCK * (b + 1), :] = ob.astype(BF16)
            ga, gb, oa = (_rows_of(r, b).astype(F32) for r in (ga_ref, gb_ref, oa_ref))
            mixed_ref[Q_BLOCK * b:Q_BLOCK * (b + 1), :] = (_sigmoid(ga) * oa + _sigmoid(gb) * ob).astype(BF16)

    return _pcall(body, name="win_fwd_mix", grid=(n_blk // B,),
                  in_specs=[qspec, *kspecs[0], *kspecs[1], bias_spec, sink_spec, rows(PROJ_GA), rows(PROJ_GB), rows(0)],
                  out_specs=[rows(0), rows(0)],
                  out_shape=[jax.ShapeDtypeStruct((S, H_B * HD_B), BF16)] * 2,
                  dims=("parallel",), comm=comm)(*[proj] * (2 * B + 5), bias_t, sinks_b, proj, proj, o_a)


def _win_bwd(proj, bias_t, sinks_b, do_b, d_proj, S):
    n_blk, B = S // Q_BLOCK, WIN_PER_STEP
    qspec, kspecs, bias_spec, sink_spec = _win_specs(S)

    def body(q_ref, *refs):
        k_refs, v_refs = refs[:B + 2], refs[B + 2:2 * B + 4]
        bias_ref, sink_ref, do_ref, _, dq_ref, dk_ref, dv_ref, dbias_ref, dsink_ref, dsink_acc = refs[2 * B + 4:]
        i = pl.program_id(0)

        @pl.when(i == 0)
        def _():
            dk_ref[...] = jnp.zeros_like(dk_ref)
            dv_ref[...] = jnp.zeros_like(dv_ref)
            dbias_ref[...] = jnp.zeros_like(dbias_ref)
            dsink_acc[...] = jnp.zeros_like(dsink_acc)

        d_bias, d_sink, dk_blocks, dv_blocks = {}, {}, [], []
        for b in range(B):
            edge_ok = _win_edge_ok(B * i + b, n_blk)
            k = jnp.concatenate([r[...] for r in k_refs[b:b + 3]], axis=0)
            v = jnp.concatenate([r[...] for r in v_refs[b:b + 3]], axis=0)
            kt = k.T
            qt = (_rows_of(q_ref, b).astype(F32) * WIN_SCALE).T.astype(BF16)
            dot_ = _rows_of(do_ref, b).astype(F32).T.astype(BF16)
            dq_parts, dks, dvs = [], [], []
            for g in range(KV_B):
                kg, vg = k[:, HD_B * g:HD_B * (g + 1)], v[:, HD_B * g:HD_B * (g + 1)]
                q4t, do4t = _group_t(qt, g), _group_t(dot_, g)
                p, p_sink = _win_probs(kg, q4t, bias_ref, sink_ref, g, edge_ok)
                dp = _dot(vg, do4t, NN)
                delta = jnp.sum(p * dp, axis=0, keepdims=True)
                ds = p * (dp - delta)
                d_bias[g] = ds if b == 0 else d_bias[g] + ds
                d_sink[g] = -p_sink * delta if b == 0 else d_sink[g] - p_sink * delta
                dsb = ds.astype(BF16)
                dq4t = _dot(kt[HD_B * g:HD_B * (g + 1), :], dsb, NN) * WIN_SCALE
                dq_parts += [dq4t[:, Q_BLOCK * j:Q_BLOCK * (j + 1)] for j in range(GROUP)]
                dks.append(_dot(dsb, q4t, NT))
                dvs.append(_dot(p.astype(BF16), do4t, NT))
            dq_ref[Q_BLOCK * b:Q_BLOCK * (b + 1), :] = jnp.concatenate(dq_parts, axis=0).T.astype(BF16)
            dk_blocks.append(jnp.concatenate(dks, axis=1))
            dv_blocks.append(jnp.concatenate(dvs, axis=1))

        for g in range(KV_B):
            for j in range(GROUP):
                dbias_ref[GROUP * g + j] += d_bias[g][:, Q_BLOCK * j:Q_BLOCK * (j + 1)]
            dsink_acc[g:g + 1, :] += d_sink[g]

        def overlap(blocks):
            out = blocks[0]
            for blk in blocks[1:]:
                keep = out.shape[0] - 2 * Q_BLOCK
                out = jnp.concatenate([out[:keep], out[keep:] + blk[:2 * Q_BLOCK], blk[2 * Q_BLOCK:]], axis=0)
            return out

        rows = pl.ds(pl.multiple_of(i * (B * Q_BLOCK), B * Q_BLOCK), (B + 2) * Q_BLOCK)
        dk_ref[rows, :] += overlap(dk_blocks)
        dv_ref[rows, :] += overlap(dv_blocks)

        @pl.when(i == n_blk // B - 1)
        def _():
            acc = dsink_acc[...]
            dsink_ref[...] = jnp.concatenate(
                [jnp.sum(acc[:, Q_BLOCK * j:Q_BLOCK * (j + 1)], axis=1, keepdims=True) for j in range(GROUP)], axis=1)

    whole = lambda shape: pl.BlockSpec(shape, lambda i: (0,) * len(shape))
    return _pcall(
        body, name="win_bwd", grid=(n_blk // B,),
        in_specs=[qspec, *kspecs[0], *kspecs[1], bias_spec, sink_spec, pl.BlockSpec((B * Q_BLOCK, H_B * HD_B), lambda i: (i, 0)), ANY],
        out_specs=[qspec, whole((S + 2 * WINDOW, KV_B * HD_B)),
                   whole((S + 2 * WINDOW, KV_B * HD_B)), whole((H_B, SPAN, Q_BLOCK)), whole((KV_B, GROUP))],
        out_shape=[jax.ShapeDtypeStruct((S, PROJ_P), BF16), jax.ShapeDtypeStruct((S + 2 * WINDOW, KV_B * HD_B), F32),
                   jax.ShapeDtypeStruct((S + 2 * WINDOW, KV_B * HD_B), F32), jax.ShapeDtypeStruct((H_B, SPAN, Q_BLOCK), F32),
                   jax.ShapeDtypeStruct((KV_B, GROUP), F32)],
        scratch_shapes=[pltpu.VMEM((KV_B, GROUP * Q_BLOCK), F32)],
        dims=("arbitrary",), aliases={2 * B + 8: 0})(*[proj] * (2 * B + 5), bias_t, sinks_b, do_b, d_proj)


def _bias_table(rel_bias_t, onehot_t, in_band):
    def body(rb_ref, oh_ref, band_ref, o_ref):
        t = lax.dot_general(rb_ref[...], oh_ref[...], NN, preferred_element_type=F32, precision=lax.Precision.HIGHEST)
        o_ref[...] = jnp.where(band_ref[...] > 0.5, t, -1e30)

    n = onehot_t.shape[1]
    tn = _tile(n, 8192)
    return _pcall(body, name="bias_table", grid=(n // tn,),
                  in_specs=[pl.BlockSpec((H_B, NUM_BUCKETS), lambda j: (0, 0)), pl.BlockSpec((NUM_BUCKETS, tn), lambda j: (0, j)),
                            pl.BlockSpec((1, tn), lambda j: (0, j))],
                  out_specs=pl.BlockSpec((H_B, tn), lambda j: (0, j)),
                  out_shape=jax.ShapeDtypeStruct((H_B, n), F32), dims=("parallel",))(rel_bias_t, onehot_t, in_band)


def _bias_table_bwd(dbias, onehot_t):
    n = onehot_t.shape[1]
    tk = _tile(n, 8192)

    def body(d_ref, oh_ref, o_ref):
        @pl.when(pl.program_id(0) == 0)
        def _():
            o_ref[...] = jnp.zeros_like(o_ref)

        o_ref[...] += lax.dot_general(d_ref[...], oh_ref[...], NT, preferred_element_type=F32, precision=lax.Precision.HIGHEST)

    return _pcall(body, name="bias_table_bwd", grid=(n // tk,),
                  in_specs=[pl.BlockSpec((H_B, tk), lambda j: (0, j)), pl.BlockSpec((NUM_BUCKETS, tk), lambda j: (0, j))],
                  out_specs=pl.BlockSpec((H_B, NUM_BUCKETS), lambda j: (0, 0)),
                  out_shape=jax.ShapeDtypeStruct((H_B, NUM_BUCKETS), F32), dims=("arbitrary",))(dbias, onehot_t)


CONV_STRIP = 128
N_STRIPS = D_FF // CONV_STRIP
CONV_ROWS = 128
HALO = 8


def _strip(rows, half):
    return pl.BlockSpec((rows, CONV_STRIP), lambda j: (0, j + half * N_STRIPS))


def _fill_padded(pad_ref, src_ref, halo, S):
    pad_ref[0:halo, :] = jnp.zeros((halo, CONV_STRIP), F32)
    pad_ref[halo + S:2 * halo + S, :] = jnp.zeros((halo, CONV_STRIP), F32)
    pad_ref[halo:halo + S, :] = src_ref[...].astype(F32)


def _conv_gate_fwd(u, conv_w, conv_b, S):
    R = min(CONV_ROWS, S)

    def body(ug_ref, uv_ref, wg_ref, wv_ref, bg_ref, bv_ref, a_ref, gpad, vpad):
        _fill_padded(gpad, ug_ref, HALO, S)
        _fill_padded(vpad, uv_ref, HALO, S)
        wg, wv, bg, bv = wg_ref[...], wv_ref[...], bg_ref[...], bv_ref[...]

        def conv(pad_ref, r0, w, b):
            dn, mid, up = (pad_ref[pl.ds(r0 + HALO + d, R), :] for d in (-1, 0, 1))
            return dn * w[0:1, :] + mid * w[1:2, :] + up * w[2:3, :] + b

        def step(c, carry):
            r0 = pl.multiple_of(c * R, R)
            g = conv(gpad, r0, wg, bg)
            val = conv(vpad, r0, wv, bv)
            a_ref[pl.ds(r0, R), :] = (g * _sigmoid(g) * val).astype(BF16)
            return carry

        lax.fori_loop(0, S // R, step, 0)

    return _pcall(body, name="conv_gate_fwd", grid=(N_STRIPS,),
                  in_specs=[_strip(S, 0), _strip(S, 1), _strip(3, 0), _strip(3, 1), _strip(1, 0), _strip(1, 1)],
                  out_specs=_strip(S, 0), out_shape=jax.ShapeDtypeStruct((S, D_FF), BF16),
                  scratch_shapes=[pltpu.VMEM((S + 2 * HALO, CONV_STRIP), F32)] * 2,
                  dims=("parallel",))(u, u, conv_w, conv_w, conv_b, conv_b)


def _conv_gate_bwd(u, conv_w, conv_b, da, S):
    R = min(CONV_ROWS, S)
    n = R + 2 * HALO

    def body(ug_ref, uv_ref, wg_ref, wv_ref, bg_ref, bv_ref, da_ref, dug_ref, duv_ref, dwg_ref, dwv_ref, dbg_ref, dbv_ref,
             gpad, vpad, dapad):
        _fill_padded(gpad, ug_ref, 2 * HALO, S)
        _fill_padded(vpad, uv_ref, 2 * HALO, S)
        _fill_padded(dapad, da_ref, HALO, S)
        wg, wv, bg, bv = wg_ref[...], wv_ref[...], bg_ref[...], bv_ref[...]

        def conv(pad_ref, r0, w, b):
            dn, mid, up = (pad_ref[pl.ds(r0 + HALO + d, n), :] for d in (-1, 0, 1))
            return dn * w[0:1, :] + mid * w[1:2, :] + up * w[2:3, :] + b, mid[HALO:HALO + R]

        def conv_bwd(duc, u_mid, w, r0, du_ref):
            dn, mid, up = pltpu.roll(duc, 1, axis=0)[HALO:HALO + R], duc[HALO:HALO + R], pltpu.roll(duc, n - 1, axis=0)[HALO:HALO + R]
            du_ref[pl.ds(r0, R), :] = (up * w[0:1, :] + mid * w[1:2, :] + dn * w[2:3, :]).astype(BF16)
            dw = jnp.concatenate([jnp.sum(up * u_mid, axis=0, keepdims=True), jnp.sum(mid * u_mid, axis=0, keepdims=True),
                                  jnp.sum(dn * u_mid, axis=0, keepdims=True)], axis=0)
            return dw, jnp.sum(mid, axis=0, keepdims=True)

        def step(c, carry):
            dw_g, db_g, dw_v, db_v = carry
            r0 = pl.multiple_of(c * R, R)
            g, ug_mid = conv(gpad, r0, wg, bg)
            val, uv_mid = conv(vpad, r0, wv, bv)
            da_ext = dapad[pl.ds(r0, n), :]
            sg = _sigmoid(g)
            ddw_v, ddb_v = conv_bwd(da_ext * (g * sg), uv_mid, wv, r0, duv_ref)
            ddw_g, ddb_g = conv_bwd(da_ext * val * (sg * (1.0 + g * (1.0 - sg))), ug_mid, wg, r0, dug_ref)
            return dw_g + ddw_g, db_g + ddb_g, dw_v + ddw_v, db_v + ddb_v

        z3, z1 = jnp.zeros((3, CONV_STRIP), F32), jnp.zeros((1, CONV_STRIP), F32)
        dwg_ref[...], dbg_ref[...], dwv_ref[...], dbv_ref[...] = lax.fori_loop(0, S // R, step, (z3, z1, z3, z1))

    half = lambda r, dt: (_strip(r, 0), jax.ShapeDtypeStruct((r, D_FF), dt))
    outs = [half(S, BF16), half(S, BF16), half(3, F32), half(3, F32), half(1, F32), half(1, F32)]
    return _pcall(
        body, name="conv_gate_bwd", grid=(N_STRIPS,),
        in_specs=[_strip(S, 0), _strip(S, 1), _strip(3, 0), _strip(3, 1), _strip(1, 0), _strip(1, 1), _strip(S, 0)],
        out_specs=[o[0] for o in outs], out_shape=[o[1] for o in outs],
        scratch_shapes=[pltpu.VMEM((S + 4 * HALO, CONV_STRIP), F32)] * 2 + [pltpu.VMEM((S + 2 * HALO, CONV_STRIP), F32)],
        dims=("parallel",))(u, u, conv_w, conv_w, conv_b, conv_b, da)


MESH = pl.DeviceIdType.MESH
ANY = pl.BlockSpec(memory_space=pl.ANY)


def _place():
    return lax.axis_index("x"), lax.axis_index("y"), lax.axis_index("c")


def _gather2(ins, outs, sems, finish):
    send_sems, recv_sems, local_sems = sems
    n_arr = len(ins)
    x, y, c = _place()
    me, sibling = (x, y, c), (x, y, 1 - c)
    chips = [(1 - x, y), (x, 1 - y), (1 - x, 1 - y)]

    def slot(a, p):
        return outs[a].at[4 * p[0] + 2 * p[1] + p[2]]

    def copy(a, k, block, to, src=None):
        return pltpu.make_async_remote_copy(
            src_ref=slot(a, block) if src is None else src, dst_ref=slot(a, block),
            send_sem=send_sems.at[a, k], recv_sem=recv_sems.at[a, k], device_id=to, device_id_type=MESH)

    mine = [pltpu.make_async_copy(ins[a], slot(a, me), local_sems.at[a]) for a in range(n_arr)]
    first = []
    for a in range(n_arr):
        first.append(copy(a, 0, me, sibling, src=ins[a]))
        first += [copy(a, 1 + j, me, (*chip, c), src=ins[a]) for j, chip in enumerate(chips)]
    if not finish:
        for cp in mine + first:
            cp.start()
        return
    passed = []
    for j, chip in enumerate(chips):
        for a in range(n_arr):
            copy(a, 1 + j, (*chip, c), me).wait_recv()
            cp = copy(a, 4 + j, (*chip, c), sibling)
            cp.start()
            passed.append(cp)
    for a in range(n_arr):
        copy(a, 0, sibling, me).wait_recv()
        for j, chip in enumerate(chips):
            copy(a, 4 + j, (*chip, 1 - c), me).wait_recv()
    for cp in first + passed:
        cp.wait_send()
    for cp in mine:
        cp.wait()


def _xchg_out_shapes(stacked, replicated):
    return ([jax.ShapeDtypeStruct(s.shape, s.dtype) for s in stacked]
            + [jax.ShapeDtypeStruct((N_DEV, *r.shape), r.dtype) for r in replicated])


def _xchg_sems(n_arr):
    return [pltpu.SemaphoreType.DMA((n_arr, 7)), pltpu.SemaphoreType.DMA((n_arr, 7)), pltpu.SemaphoreType.DMA((n_arr,))]


def _xchg_copies(ins, outs, sems, n_st, with_recv):
    send_sems, recv_sems, local_sems = sems
    n_arr = len(ins)
    x, y, c = _place()
    me = 4 * x + 2 * y + c

    def src(a, idx):
        return ins[a].at[idx] if a < n_st else ins[a]

    mine = [pltpu.make_async_copy(src(a, me), outs[a].at[me], local_sems.at[a]) for a in range(n_arr)]
    pairs = []
    for k in range(1, N_DEV):
        px, py, pc = x ^ (k >> 2), y ^ ((k >> 1) & 1), c ^ (k & 1)
        peer = 4 * px + 2 * py + pc
        for a in range(n_arr):
            sems_k = dict(send_sem=send_sems.at[a, k - 1], recv_sem=recv_sems.at[a, k - 1], device_id_type=MESH)
            send = pltpu.make_async_remote_copy(src_ref=src(a, peer), dst_ref=outs[a].at[me], device_id=(px, py, pc), **sems_k)
            recv = None
            if with_recv:
                recv = pltpu.make_async_remote_copy(src_ref=src(a, peer), dst_ref=outs[a].at[peer], device_id=(x, y, c), **sems_k)
            pairs.append((send, recv))
    return mine, pairs


def _xchg_start(ins, outs, sems, n_st):
    mine, pairs = _xchg_copies(ins, outs, sems, n_st, False)
    for cp in mine:
        cp.start()
    for send, _ in pairs:
        send.start()


def _xchg_finish(ins, outs, sems, n_st):
    mine, pairs = _xchg_copies(ins, outs, sems, n_st, True)
    for _, recv in pairs:
        recv.wait_recv()
    for send, _ in pairs:
        send.wait_send()
    for cp in mine:
        cp.wait()


def _exchange(stacked, replicated, name):
    _, landed = _pcall(lambda: None, name=name, grid=(), in_specs=[], out_specs=[], out_shape=[], comm=(stacked, replicated))()
    return landed


HBM = pl.BlockSpec(memory_space=pltpu.HBM)
SEMS = pl.BlockSpec(memory_space=pltpu.SEMAPHORE)
SIDE_EFFECT = pltpu.SideEffectType.DATAFLOW_SIDE_EFFECTING


N_SPLIT_SEMS = 2 * (N_DEV - 1)


def _split_copies(src, land, sems, with_recv):
    x, y, c = _place()
    me = 4 * x + 2 * y + c
    pairs = []
    for k in range(1, N_DEV):
        px, py, pc = x ^ (k >> 2), y ^ ((k >> 1) & 1), c ^ (k & 1)
        peer = 4 * px + 2 * py + pc
        sems_k = dict(send_sem=sems[k - 1], recv_sem=sems[N_DEV - 1 + k - 1], device_id_type=MESH)
        send = pltpu.make_async_remote_copy(src_ref=src.at[peer], dst_ref=land.at[me], device_id=(px, py, pc), **sems_k)
        recv = None
        if with_recv:
            recv = pltpu.make_async_remote_copy(src_ref=src.at[peer], dst_ref=land.at[peer], device_id=(x, y, c), **sems_k)
        pairs.append((send, recv))
    return pairs


def _exchange_start(stacked, name):
    def body(src, land, *rest):
        for send, _ in _split_copies(src, land, rest[:N_SPLIT_SEMS], False):
            send.start()
        rest[-1][...] = jnp.zeros_like(rest[-1])

    shape = pltpu.HBM(stacked.shape, stacked.dtype)
    res = pl.pallas_call(
        body, name=name, in_specs=[HBM, HBM],
        out_shape=(*[pltpu.SemaphoreType.DMA(())] * N_SPLIT_SEMS, shape, shape, jax.ShapeDtypeStruct((8, 128), F32)),
        out_specs=(*[SEMS] * N_SPLIT_SEMS, HBM, HBM, pl.BlockSpec(memory_space=pltpu.VMEM)),
        input_output_aliases={0: N_SPLIT_SEMS, 1: N_SPLIT_SEMS + 1},
        compiler_params=pltpu.CompilerParams(has_side_effects=SIDE_EFFECT),
    )(pltpu.with_memory_space_constraint(stacked, pltpu.HBM),
      pltpu.with_memory_space_constraint(lax.empty(stacked.shape, stacked.dtype), pltpu.HBM))
    return res[:N_SPLIT_SEMS], res[N_SPLIT_SEMS], res[N_SPLIT_SEMS + 1], res[-1]


def _exchange_wait(sems, src, land, after, name):
    def body(src_ref, land_ref, *rest):
        for send, recv in _split_copies(src_ref, land_ref, rest[:N_SPLIT_SEMS], True):
            send.wait_send()
            recv.wait_recv()

    shape = pltpu.HBM(src.shape, src.dtype)
    return pl.pallas_call(
        body, name=name, in_specs=[HBM, HBM, *[SEMS] * N_SPLIT_SEMS, ANY],
        out_shape=(shape, shape), out_specs=(HBM, HBM), input_output_aliases={0: 0, 1: 1},
        compiler_params=pltpu.CompilerParams(has_side_effects=SIDE_EFFECT))(src, land, *sems, after)[1]


def _adamw(parts, w, m, v, name):
    _, R, C = w.shape
    tr = R if R <= 512 else max(t for t in range(16, 513, 16) if R % t == 0)
    pr = tr if parts.shape[1] == R else -(-R // 16) * 16
    assert pr == tr or tr == R

    def body(p_ref, w_ref, m_ref, v_ref, g_ref, d_ref, nm_ref, nv_ref):
        g = p_ref[0].astype(F32)[:tr]
        for s in range(1, N_DEV):
            g = g + p_ref[s].astype(F32)[:tr]
        m2 = ADAM_B1 * m_ref[0] + (1.0 - ADAM_B1) * g
        v2 = ADAM_B2 * v_ref[0] + (1.0 - ADAM_B2) * (g * g)
        m_hat = m2 / (1.0 - ADAM_B1 ** ADAM_STEP)
        v_hat = v2 / (1.0 - ADAM_B2 ** ADAM_STEP)
        g_ref[0] = g
        d_ref[0] = -ADAM_LR * (m_hat / (jnp.sqrt(v_hat) + ADAM_EPS) + ADAM_WD * w_ref[0])
        nm_ref[0] = m2
        nv_ref[0] = v2

    blk = pl.BlockSpec((1, tr, C), lambda i: (0, i, 0))
    return _pcall(body, name=name, grid=(R // tr,),
                  in_specs=[pl.BlockSpec((N_DEV, pr, C), lambda i: (0, i, 0)), blk, blk, blk],
                  out_specs=[blk] * 4, out_shape=[jax.ShapeDtypeStruct((1, R, C), F32)] * 4,
                  dims=("parallel",))(parts, w, m, v)


def _t5_bucket(rel):
    nb = NUM_BUCKETS // 2
    max_exact = nb // 2
    base = (rel > 0).astype(jnp.int32) * nb
    n = jnp.abs(rel)
    nf = jnp.maximum(n, 1).astype(jnp.float32)
    large = max_exact + (jnp.log(nf / max_exact) / math.log(MAX_DISTANCE / max_exact) * (nb - max_exact)).astype(jnp.int32)
    large = jnp.minimum(large, nb - 1)
    return base + jnp.where(n < max_exact, n, large)


def _unstack_cols(g):
    return jnp.transpose(g, (1, 0, 2)).reshape(g.shape[1], N_DEV * g.shape[2])


def _stack_cols(w, n=N_DEV):
    R = w.shape[0]
    return jnp.transpose(w.reshape(R, n, w.shape[1] // n), (1, 0, 2))


def _stack_halves(g, v):
    return jnp.concatenate([_stack_cols(g, N_DEV // 2), _stack_cols(v, N_DEV // 2)], axis=0)


def kernel(x, positions, norm1_g, w_in, q_a_norm_g, w_q_b, kv_a_norm_g, w_kv_b, rel_bias, sinks, w_out, norm2_g, w_up, conv_w, conv_b, w_down, final_norm_g, loss_target, m_norm1_g, m_w_in, m_q_a_norm_g, m_w_q_b, m_kv_a_norm_g, m_w_kv_b, m_rel_bias, m_sinks, m_w_out, m_norm2_g, m_w_up, m_conv_w, m_conv_b, m_w_down, m_final_norm_g, v_norm1_g, v_w_in, v_q_a_norm_g, v_w_q_b, v_kv_a_norm_g, v_w_kv_b, v_rel_bias, v_sinks, v_w_out, v_norm2_g, v_w_up, v_conv_w, v_conv_b, v_w_down, v_final_norm_g):
    S = x.shape[1]
    x = x[0]
    target = loss_target[0]
    TM = 256

    tr = lambda w: jnp.swapaxes(w, 1, 2)
    (h1,), (g_in, g_qb, g_kvb) = _rowwise(lambda a, g: (_rms(a, g),), "norm1_gather", S, TM, [_rows(x), _whole(norm1_g)], [("rows", D_MODEL, BF16)],
                                          gather=[tr(w_in)[0].astype(BF16), tr(w_q_b)[0].astype(BF16), w_kv_b[0].astype(BF16)])
    late_weights = [w_out[0].astype(BF16), tr(w_up)[0].astype(BF16), conv_w[0]]
    wi = g_in.reshape(W_IN_COLS, D_MODEL)
    c0, c1, c2, c3, c4, c5 = (sum(W_IN_SIZES[:i + 1]) for i in range(6))
    w_in_pt = jnp.concatenate([wi[c4:c5], wi[c5:], wi[c1:c2], wi[:c0], wi[c2:c3], wi[c3:c4],
                               wi[c0:c0 + KV_LORA], wi[c0 + KV_LORA:c1], jnp.zeros((64, D_MODEL), BF16)], axis=0)
    wq = g_qb.reshape(H_A, QK_HEAD, Q_LORA)
    w_qb_pt = jnp.concatenate([wq[:, :QK_NOPE].reshape(H_A * QK_NOPE, Q_LORA), wq[:, QK_NOPE:].reshape(H_A * QK_ROPE, Q_LORA)], axis=0)
    w_kvb = _unstack_cols(g_kvb)

    half = QK_ROPE // 2
    inv_freq = ROPE_THETA ** (-jnp.arange(half, dtype=F32) / half)
    ang = positions.astype(F32)[:, None] * inv_freq[None, :]
    cos, sin = jnp.cos(ang), jnp.sin(ang)
    qa = jnp.arange(Q_BLOCK, dtype=jnp.int32)[:, None]
    kc = jnp.arange(SPAN, dtype=jnp.int32)[None, :]
    rel = (kc - WINDOW - qa).T
    in_band = (jnp.abs(rel) <= WINDOW).astype(F32).reshape(1, Q_BLOCK * SPAN)
    onehot_t = (_t5_bucket(rel).reshape(1, Q_BLOCK * SPAN) == jnp.arange(NUM_BUCKETS, dtype=jnp.int32)[:, None]).astype(F32)
    bias_t = _bias_table(rel_bias.T, onehot_t, in_band).reshape(H_B, SPAN, Q_BLOCK)
    sinks_b = jnp.broadcast_to(sinks.reshape(H_B, 1), (H_B, Q_BLOCK))

    proj =_matmul(h1, w_in_pt, "nt", BF16, "proj")

    def lat_fn(qlat, ckv, kr, gq, gkv, cs, sn):
        r1, r2 = _rope(kr[:, :half], kr[:, half:QK_ROPE], cs, sn)
        return _rms(qlat, gq), _rms(ckv, gkv), jnp.concatenate([r1, r2], axis=1)

    qn, ckvn, k_rope = _rowwise(lat_fn, "latents", S, TM,
                                [_rows(proj, 256, PROJ_QLAT), _rows(proj, 128, PROJ_CKV), _rows(proj, 128, PROJ_KROPE),
                                 _whole(q_a_norm_g), _whole(kv_a_norm_g), _rows(cos), _rows(sin)],
                                [("rows", Q_LORA, BF16), ("rows", KV_LORA, BF16), ("rows", QK_ROPE, BF16)])
    def q_heads_fn(q, cs, sn):
        q = q * MLA_PRESCALE
        outs = []
        for h in range(H_A):
            o = H_A * QK_NOPE + QK_ROPE * h
            r1, r2 = _rope(q[:, o:o + half], q[:, o + half:o + QK_ROPE], cs, sn)
            outs.append(jnp.concatenate([q[:, QK_NOPE * h:QK_NOPE * (h + 1)], r1, r2], axis=1)[None])
        return (jnp.concatenate(outs, axis=0),)

    (q_full,) = _matmul(qn, w_qb_pt, "nt", None, "q_up_heads", tm=512, tn=1536,
                        epi=(q_heads_fn, [_rows(cos), _rows(sin)], [("heads", H_A, QK_HEAD, BF16)]))

    def k_heads_fn(kvf, kr):
        return kvf, jnp.concatenate([jnp.concatenate([kvf[:, 256 * h:256 * h + QK_NOPE], kr], axis=1)[None] for h in range(H_A)], axis=0)

    kv, k_full = _matmul(ckvn, w_kvb, "nn", None, "kv_up_heads", tm=512, tn=2048,
                         epi=(k_heads_fn, [_rows(k_rope)], [("rows", H_A * (QK_NOPE + V_DIM), BF16), ("heads", H_A, QK_HEAD, BF16)]))
    (o_a, lse), (g_out, g_up, g_cw) = _mla_fwd(q_full, k_full, kv, S, comm=([], late_weights))
    w_out_f = g_out.reshape(D_MODEL, D_MODEL)
    w_up_t = g_up.reshape(2 * D_FF, D_MODEL)
    conv_w_f = _unstack_cols(g_cw)

    (o_b, mixed), (g_down,) = _win_fwd(proj, o_a, bias_t, sinks_b, S, comm=([], [w_down[0].astype(BF16)]))
    w_down_f = g_down.reshape(D_FF, D_MODEL)

    x1, h2 = _matmul(mixed, w_out_f, "nn", None, "out_proj", residual=x, tm=512,
                     epi=(lambda a, g: (a, _rms(a, g)), [_whole(norm2_g)], [("rows", D_MODEL, F32), ("rows", D_MODEL, BF16)]))
    u = _matmul(h2, w_up_t, "nt", BF16, "ffn_up", tn=1408)
    act = _conv_gate_fwd(u, conv_w_f, conv_b, S)

    def final_fn(a, g, t):
        err = _rms(a, g) - t
        loss = 0.5 * jnp.sum(jnp.mean(err * err, axis=-1, keepdims=True), axis=0, keepdims=True)
        dx, dg = _rms_bwd(err * (1.0 / D_MODEL), a, g)
        return dx, dx, dg, jnp.broadcast_to(loss, (1, 128))

    gfin = final_norm_g.reshape(1, D_MODEL)
    dx2, dx2_b, d_gfin, loss_row = _matmul(
        act, w_down_f, "nn", None, "ffn_down_loss", residual=x1, tm=512,
        epi=(final_fn, [_whole(gfin), _rows(target)],
             [("rows", D_MODEL, F32), ("rows", D_MODEL, BF16), ("acc", 1, D_MODEL), ("acc", 1, 128)]))
    d_act = _matmul(dx2_b, w_down_f, "nt", BF16, "ffn_down_dx", tn=1408)
    d_w_down = _matmul(act, dx2_b, "tn", BF16, "ffn_down_dw")
    du_g, du_v, dcw_g, dcw_v, dcb_g, dcb_v = _conv_gate_bwd(u, conv_w_f, conv_b, d_act, S)
    d_conv_b = jnp.concatenate([dcb_g, dcb_v], axis=1)

    def norm_bwd_fn(dh, a, g, dres):
        dx, dg = _rms_bwd(dh, a, g)
        dx = dx + dres
        return dx, dx, dg

    dx1, dx1_b, d_g2 = _matmul(du_g, w_up_t, "nn", None, "ffn_up_dx_norm2_bwd", tm=256, a2=du_v,
                               epi=(norm_bwd_fn, [_rows(x1), _whole(norm2_g), _rows(dx2)],
                                    [("rows", D_MODEL, F32), ("rows", D_MODEL, BF16), ("acc", 1, D_MODEL)]))
    d_w_up_t = _matmul(du_g, h2, "tn", BF16, "ffn_up_dw", tm=256, a2=du_v)
    d_w_out = _matmul(mixed, dx1_b, "tn", BF16, "out_proj_dw", tm=512)

    def gate_bwd_fn(dm, ga, gb, oa, ob):
        sa, sb = _sigmoid(ga), _sigmoid(gb)
        return jnp.concatenate([dm * oa * sa * (1.0 - sa), dm * ob * sb * (1.0 - sb)], axis=1), dm * sa, dm * sb

    d_proj, do_a, do_b = _matmul(
        dx1_b, w_out_f, "nt", None, "out_proj_dx_gate_bwd", tm=512,
        epi=(gate_bwd_fn, [_rows(proj, 1024, PROJ_GA), _rows(proj, 1024, PROJ_GB), _rows(o_a), _rows(o_b)],
             [("cols", 2 * D_MODEL, 0, PROJ_P, BF16), ("rows", D_MODEL, BF16), ("rows", D_MODEL, BF16)]))

    d_proj, dk_acc, dv_acc, d_bias, d_sinks_g = _win_bwd(proj, bias_t, sinks_b, do_b, d_proj, S)
    d_sinks = d_sinks_g.reshape(1, H_B)

    early = [d_w_out.reshape(N_DEV, D_MODEL // N_DEV, D_MODEL), d_w_up_t.reshape(N_DEV, 2 * D_FF // N_DEV, D_MODEL),
             d_w_down.reshape(N_DEV, D_FF // N_DEV, D_MODEL), _stack_halves(dcw_g, dcw_v)]
    (dq_nope, dqr_heads, dkv, dkr_heads), recv_early = _mla_bwd(q_full, k_full, kv, do_a, o_a, lse, S, comm=(early, []))

    def dqr_post_fn(dqr, cs, sn):
        rope = []
        for h in range(H_A):
            rope += list(_rope_bwd(dqr[h, :, :half], dqr[h, :, half:], cs, sn))
        return (jnp.concatenate(rope, axis=1),)

    (dq_rope,) = _rowwise(dqr_post_fn, "dqr_post", S, TM, [_heads(dqr_heads), _rows(cos), _rows(sin)], [("rows", H_A * QK_ROPE, BF16)])

    d_qn = _matmul(dq_nope, w_qb_pt, "nn", F32, "q_up_dx", a2=dq_rope)
    d_ckvn = _matmul(dkv, w_kvb, "nt", F32, "kv_up_dx")

    def lat_bwd_fn(dqn, dckvn, dkr_h, cs, sn, qlat, ckv, gq, gkv, dkb, dvb):
        dql, dgq = _rms_bwd(dqn, qlat, gq)
        dck, dgkv = _rms_bwd(dckvn, ckv, gkv)
        dkr = dkr_h[0]
        for h in range(1, H_A):
            dkr = dkr + dkr_h[h]
        r1, r2 = _rope_bwd(dkr[:, :half], dkr[:, half:], cs, sn)
        tail = jnp.concatenate([dql, dkb, dvb, dck, r1, r2, jnp.zeros_like(dkr)], axis=1)
        return tail, dgq, dgkv

    shifted = lambda arr: (arr, lambda tm: pl.BlockSpec((tm, arr.shape[1]), lambda i, *_: (i + WINDOW // tm, 0)))
    TL = min(128, S)
    d_proj, d_gq, d_gkv = _rowwise(lat_bwd_fn, "latents_bwd", S, TL,
                                   [_rows(d_qn), _rows(d_ckvn), _heads(dkr_heads), _rows(cos), _rows(sin), _rows(proj, 256, PROJ_QLAT), _rows(proj, 128, PROJ_CKV),
                                    _whole(q_a_norm_g), _whole(kv_a_norm_g), shifted(dk_acc), shifted(dv_acc)],
                                   [("cols", 1024, 3, PROJ_P, BF16), ("acc", 1, Q_LORA), ("acc", 1, KV_LORA)], into=(d_proj, 0))
    dp = _matmul(d_proj, h1, "tn", BF16, "proj_dw", tm=512)

    late = jnp.concatenate([dp[3072:3328], dp[3840:3968], dp[3968:4032], dp[2048:3072], dp[3328:3584],
                            dp[3584:3840], dp[0:1024], dp[1024:2048]], axis=0).reshape(N_DEV, W_IN_COLS // N_DEV, D_MODEL)
    late_sems, late_src, late_land, started = _exchange_start(late, "late_grads_start")

    def norm1_bwd_fn(dh, a, g, dres):
        dx, dg = _rms_bwd(dh, a, g)
        return dx + dres, dg

    grad_x, d_g1 = _matmul(
        d_proj, w_in_pt, "nn", None, "proj_dx_norm1_bwd", tm=512,
        epi=(norm1_bwd_fn, [_rows(x), _whole(norm1_g + started[:1, :1]), _rows(dx1)], [("rows", D_MODEL, F32), ("acc", 1, D_MODEL)]))

    transposed = ("w_in", "w_q_b", "w_up")
    ready_names = ["w_out", "w_up", "w_down", "conv_w"]
    ready_wmv = [(w_out, m_w_out, v_w_out), (tr(w_up), tr(m_w_up), tr(v_w_up)), (w_down, m_w_down, v_w_down), (conv_w, m_conv_w, v_conv_w)]
    big = {n: _adamw(r, *wmv, "adamw_" + n) for n, r, wmv in zip(ready_names, recv_early, ready_wmv)}

    (d_bias, dq_nope_l, dkv_l), _ = lax.optimization_barrier(((d_bias, dq_nope, dkv), started))
    d_rel_bias = _bias_table_bwd(d_bias.reshape(H_B, Q_BLOCK * SPAN), onehot_t).T
    d_w_qb_pt = _matmul(dq_nope_l, qn, "tn", BF16, "q_up_dw", tm=512, a2=dq_rope)
    d_w_kvb = _matmul(ckvn, dkv_l, "tn", BF16, "kv_up_dw", tn=2048)
    d_w_qb_t = jnp.concatenate([d_w_qb_pt[:H_A * QK_NOPE].reshape(H_A, QK_NOPE, Q_LORA),
                                d_w_qb_pt[H_A * QK_NOPE:].reshape(H_A, QK_ROPE, Q_LORA)], axis=1)

    after = lax.optimization_barrier([big[n][0] for n in ready_names] + [d_rel_bias, d_w_qb_t, d_w_kvb])
    landed = _exchange_wait(late_sems, late_src, late_land, after[-1], "late_grads_wait")
    me = 4 * lax.axis_index("x") + 2 * lax.axis_index("y") + lax.axis_index("c")
    landed = lax.dynamic_update_slice_in_dim(landed, lax.dynamic_slice_in_dim(late, me, 1, axis=0), me, axis=0)
    big["w_in"] = _adamw(landed, tr(w_in), tr(m_w_in), tr(v_w_in), "adamw_w_in")

    small_parts = [d_g1, d_gq, d_gkv, d_rel_bias.reshape(1, NUM_BUCKETS * H_B), d_sinks, d_g2, d_conv_b, d_gfin, loss_row[:, :1]]
    small = jnp.concatenate(small_parts, axis=1)
    n_small = small.shape[1]
    pad = (-n_small) % 128
    small = jnp.pad(small, ((0, 0), (0, pad)))
    small, _ = lax.optimization_barrier((small, [landed, *after]))
    recv_qb, recv_kvb, recv_small = _exchange([after[-2], _stack_cols(after[-1])], [small], "exchange_small_grads")
    big["w_q_b"] = _adamw(recv_qb, tr(w_q_b), tr(m_w_q_b), tr(v_w_q_b), "adamw_w_q_b")
    big["w_kv_b"] = _adamw(recv_kvb, w_kv_b, m_w_kv_b, v_w_kv_b, "adamw_w_kv_b")

    def flat(a):
        return a.reshape(1, -1)

    small_w = [norm1_g, q_a_norm_g, kv_a_norm_g, rel_bias, sinks, norm2_g, conv_b, final_norm_g]
    small_m = [m_norm1_g, m_q_a_norm_g, m_kv_a_norm_g, m_rel_bias, m_sinks, m_norm2_g, m_conv_b, m_final_norm_g]
    small_v = [v_norm1_g, v_q_a_norm_g, v_kv_a_norm_g, v_rel_bias, v_sinks, v_norm2_g, v_conv_b, v_final_norm_g]
    cat = lambda parts: jnp.pad(jnp.concatenate([flat(a) for a in parts], axis=1), ((0, 0), (0, pad + 1)))[None]
    sm = _adamw(recv_small, cat(small_w), cat(small_m), cat(small_v), "adamw_small")

    loss = sm[0][0, 0, n_small - 1]
    order =["norm1_g", "w_in", "q_a_norm_g", "w_q_b", "kv_a_norm_g", "w_kv_b", "rel_bias", "sinks", "w_out", "norm2_g", "w_up",
             "conv_w", "conv_b", "w_down", "final_norm_g"]
    small_names = ["norm1_g", "q_a_norm_g", "kv_a_norm_g", "rel_bias", "sinks", "norm2_g", "conv_b", "final_norm_g"]
    offs, o = {}, 0
    for n, a in zip(small_names, small_w):
        offs[n] = (o, a.size, a.shape)
        o += a.size
    outs = [loss, grad_x[None]]
    for kind in range(4):
        for n in order:
            if n in big:
                outs.append(tr(big[n][kind]) if n in transposed else big[n][kind])
            else:
                o, size, shape = offs[n]
                outs.append(sm[kind][0, 0, o:o + size].reshape(shape))
    return tuple(outs)
```

```python
import math

import jax
import jax.numpy as jnp
from jax import lax
from jax.experimental import pallas as pl
from jax.experimental.pallas import tpu as pltpu

F32 = jnp.float32
BF16 = jnp.bfloat16

N_DEV = 8
D_MODEL = 1024
EPS = 1e-6
H_A, QK_NOPE, QK_ROPE, V_DIM, Q_LORA, KV_LORA = 8, 128, 64, 128, 256, 128
QK_HEAD = QK_NOPE + QK_ROPE
ROPE_THETA = 10000.0
H_B, KV_B, GROUP, HD_B, WINDOW, Q_BLOCK = 16, 4, 4, 64, 128, 128
SPAN = Q_BLOCK + 2 * WINDOW
NUM_BUCKETS, MAX_DISTANCE = 32, 128
D_FF = 2816
ADAM_LR, ADAM_B1, ADAM_B2, ADAM_EPS, ADAM_WD, ADAM_STEP = 0.001, 0.9, 0.999, 1e-08, 0.01, 10

W_IN_SIZES = (Q_LORA, KV_LORA + QK_ROPE, H_B * HD_B, KV_B * HD_B, KV_B * HD_B, D_MODEL, D_MODEL)
W_IN_COLS = sum(W_IN_SIZES)
PROJ_P = 4096
PROJ_GA, PROJ_GB, PROJ_QB, PROJ_QLAT, PROJ_KB, PROJ_VB, PROJ_CKV, PROJ_KROPE = 0, 1, 2, 12, 13, 14, 30, 31

VMEM_LIMIT = 56 * 1024 * 1024

NN = (((1,), (0,)), ((), ()))
NT = (((1,), (1,)), ((), ()))
TN = (((0,), (0,)), ((), ()))


def _pcall(body, *, name, grid, in_specs, out_specs, out_shape, scratch_shapes=(), dims=None, comm=None, aliases=None, two_level=False):
    if comm is None:
        params = pltpu.CompilerParams(dimension_semantics=dims, vmem_limit_bytes=VMEM_LIMIT)
        return pl.pallas_call(body, name=name, grid=grid, in_specs=in_specs, out_specs=out_specs, out_shape=out_shape,
                              scratch_shapes=list(scratch_shapes), input_output_aliases=aliases or {}, compiler_params=params)
    assert not aliases
    stacked, replicated = comm
    arrs = [*stacked, *replicated]
    n_st, n_arr = len(stacked), len(arrs)
    single = not isinstance(out_specs, (list, tuple))
    o_specs, o_shape = ([out_specs], [out_shape]) if single else (list(out_specs), list(out_shape))
    n_in, n_out = len(in_specs), len(o_specs)

    def wrapped(*refs):
        c_in = refs[n_in:n_in + n_arr]
        c_out = refs[n_in + n_arr + n_out:n_in + 2 * n_arr + n_out]
        sems = refs[len(refs) - 3:]
        own = (*refs[:n_in], *refs[n_in + n_arr:n_in + n_arr + n_out], *refs[n_in + 2 * n_arr + n_out:len(refs) - 3])
        if two_level:
            assert n_st == 0
            start, finish = (lambda: _gather2(c_in, c_out, sems, False)), (lambda: _gather2(c_in, c_out, sems, True))
        else:
            start, finish = (lambda: _xchg_start(c_in, c_out, sems, n_st)), (lambda: _xchg_finish(c_in, c_out, sems, n_st))
        if not grid:
            start()
            finish()
            return
        first = last = None
        for d, n in enumerate(grid):
            pid = pl.program_id(d)
            first = (pid == 0) if first is None else first & (pid == 0)
            last = (pid == n - 1) if last is None else last & (pid == n - 1)

        pl.when(first)(start)
        body(*own)
        pl.when(last)(finish)

    params = pltpu.CompilerParams(dimension_semantics=("arbitrary",) * len(grid), vmem_limit_bytes=VMEM_LIMIT)
    call = pl.pallas_call(wrapped, name=name, grid=grid, in_specs=[*in_specs, *[ANY] * n_arr], out_specs=[*o_specs, *[ANY] * n_arr],
                          out_shape=[*o_shape, *_xchg_out_shapes(stacked, replicated)],
                          scratch_shapes=[*scratch_shapes, *_xchg_sems(n_arr)], compiler_params=params)

    def run(*args):
        res = call(*args, *arrs)
        outs, landed = res[:n_out], res[n_out:]
        return (outs[0] if single else outs), landed

    return run


def _dot(a, b, dn):
    return lax.dot_general(a, b, dn, preferred_element_type=F32)


def _tile(n, target):
    best = None
    for t in range(128, min(n, target) + 1, 128):
        if n % t == 0:
            best = t
    return n if best is None else best


def _matmul(a, b, mode, out_dtype, name, residual=None, tm=1024, tn=1024, comm=None, a2=None, epi=None):
    if mode == "nn":
        (M, K), N = a.shape, b.shape[1]
    elif mode == "nt":
        (M, K), N = a.shape, b.shape[0]
    else:
        (K, M), N = a.shape, b.shape[1]
    tm, tn = _tile(M, tm), _tile(N, tn)
    a_spec = pl.BlockSpec((K, tm), lambda i, j: (0, i)) if mode == "tn" else pl.BlockSpec((tm, K), lambda i, j: (i, 0))
    b_spec = pl.BlockSpec((tn, b.shape[1]), lambda i, j: (j, 0)) if mode == "nt" else pl.BlockSpec((K, tn), lambda i, j: (0, j))
    o_spec = pl.BlockSpec((tm, tn), lambda i, j: (i, j))
    in_specs, args = [a_spec, b_spec], [a, b]
    n1 = M // tm
    if a2 is not None and mode == "tn":
        assert M % tm == 0 and a2.shape[1] % tm == 0
        in_specs[0] = pl.BlockSpec((K, tm), lambda i, j: (0, jnp.minimum(i, n1 - 1)))
        in_specs.append(pl.BlockSpec((K, tm), lambda i, j: (0, jnp.maximum(i - n1, 0))))
        args.append(a2)
        M += a2.shape[1]
    elif a2 is not None:
        assert (mode == "nt" and K + a2.shape[1] == b.shape[1]) or (mode == "nn" and K + a2.shape[1] == b.shape[0])
        if mode == "nn":
            b_spec = in_specs[1] = pl.BlockSpec((b.shape[0], tn), lambda i, j: (0, j))
        in_specs.append(pl.BlockSpec((tm, a2.shape[1]), lambda i, j: (i, 0)))
        args.append(a2)
    if residual is not None:
        in_specs.append(o_spec)
        args.append(residual)
    n_mm = len(args)
    scratch = [pltpu.VMEM((tm, K), a.dtype)] if mode == "tn" else []
    if epi is None:
        out_specs, out_shape, is_acc = o_spec, jax.ShapeDtypeStruct((M, N), out_dtype), None
    else:
        assert tn == N
        fn, epi_ins, epi_outs = epi
        in_specs += [mk(tm) for _, mk in epi_ins]
        args += [arr for arr, _ in epi_ins]
        out_specs, out_shape, is_acc = _row_out_specs(epi_outs, M, tm)

    def body(*refs):
        a_ref, b_ref = refs[0], refs[1]
        n_out = 1 if epi is None else len(is_acc)
        out_refs = refs[len(args):len(args) + n_out]
        if mode == "tn":
            at_ref = refs[len(args) + n_out]

            first_col = pl.program_id(1) == 0
            from_a = first_col if a2 is None else first_col & (pl.program_id(0) < n1)

            @pl.when(from_a)
            def _():
                at_ref[...] = a_ref[...].T

            if a2 is not None:
                @pl.when(first_col & (pl.program_id(0) >= n1))
                def _():
                    at_ref[...] = refs[2][...].T

            acc = _dot(at_ref[...], b_ref[...], NN)
        elif a2 is not None and mode == "nt":
            acc = _dot(a_ref[...], b_ref[:, :K], NT) + _dot(refs[2][...], b_ref[:, K:], NT)
        elif a2 is not None:
            acc = _dot(a_ref[...], b_ref[:K, :], NN) + _dot(refs[2][...], b_ref[K:, :], NN)
        else:
            acc = _dot(a_ref[...], b_ref[...], NT if mode == "nt" else NN)
        if residual is not None:
            acc = acc + refs[n_mm - 1][...]
        if epi is None:
            out_refs[0][...] = acc.astype(out_dtype)
        else:
            _store_rows(out_refs, fn(acc, *[_load_f32(r) for r in refs[n_mm:len(args)]]), is_acc)

    return _pcall(body, name=name, grid=(M // tm, N // tn), in_specs=in_specs, out_specs=out_specs,
                  out_shape=out_shape, scratch_shapes=scratch,
                  dims=("arbitrary" if epi is not None else "parallel", "arbitrary"), comm=comm)(*args)


def _rows(arr, width=None, col=0):
    width = arr.shape[1] if width is None else width
    return (arr, lambda tm: pl.BlockSpec((tm, width), lambda i, *_: (i, col)))


def _heads(arr):
    return (arr, lambda tm: pl.BlockSpec((arr.shape[0], tm, arr.shape[2]), lambda i, *_: (0, i, 0)))


def _whole(arr):
    nd = arr.ndim
    return (arr, lambda tm: pl.BlockSpec(arr.shape, lambda i, *_: (0,) * nd))


def _row_out_specs(outs, n_rows, tm):
    out_specs, out_shape, is_acc = [], [], []
    for o in outs:
        if o[0] == "rows":
            out_specs.append(pl.BlockSpec((tm, o[1]), lambda i, *_: (i, 0)))
            out_shape.append(jax.ShapeDtypeStruct((n_rows, o[1]), o[2]))
        elif o[0] == "cols":
            out_specs.append(pl.BlockSpec((tm, o[1]), lambda i, *_, c=o[2]: (i, c)))
            out_shape.append(jax.ShapeDtypeStruct((n_rows, o[3]), o[4]))
        elif o[0] == "heads":
            out_specs.append(pl.BlockSpec((o[1], tm, o[2]), lambda i, *_: (0, i, 0)))
            out_shape.append(jax.ShapeDtypeStruct((o[1], n_rows, o[2]), o[3]))
        else:
            out_specs.append(pl.BlockSpec((o[1], o[2]), lambda i, *_: (0, 0)))
            out_shape.append(jax.ShapeDtypeStruct((o[1], o[2]), F32))
        is_acc.append(o[0] == "acc")
    return out_specs, out_shape, is_acc


def _load_f32(r):
    v = r[...]
    return v.astype(F32) if v.dtype == BF16 else v


def _store_rows(out_refs, vals, is_acc):
    for r, v, acc in zip(out_refs, vals, is_acc):
        if acc:
            @pl.when(pl.program_id(0) == 0)
            def _():
                r[...] = jnp.zeros_like(r)

            r[...] += v
        else:
            r[...] = v.astype(r.dtype)


def _rowwise(fn, name, n_rows, tm, ins, outs, upcast=True, into=None, gather=None):
    tm = min(tm, n_rows)
    assert n_rows % tm == 0
    in_specs = [mk(tm) for _, mk in ins]
    out_specs, out_shape, is_acc = _row_out_specs(outs, n_rows, tm)
    n_in = len(ins)
    args = [a for a, _ in ins]
    aliases = {}
    if into is not None:
        in_specs.append(ANY)
        args.append(into[0])
        aliases = {n_in: into[1]}

    def body(*refs):
        vals = fn(*[_load_f32(r) if upcast else r[...] for r in refs[:n_in]])
        _store_rows(refs[len(args):], vals, is_acc)

    return _pcall(body, name=name, grid=(n_rows // tm,), in_specs=in_specs, out_specs=out_specs,
                  out_shape=out_shape, dims=("arbitrary",), aliases=aliases,
                  comm=None if gather is None else ([], gather), two_level=True)(*args)


def _rms(x, g):
    r = lax.rsqrt(jnp.mean(x * x, axis=-1, keepdims=True) + EPS)
    return x * r * g


def _rms_bwd(dy, x, g):
    r = lax.rsqrt(jnp.mean(x * x, axis=-1, keepdims=True) + EPS)
    xhat = x * r
    dxhat = dy * g
    dx = r * (dxhat - xhat * jnp.mean(dxhat * xhat, axis=-1, keepdims=True))
    return dx, jnp.sum(dy * xhat, axis=0, keepdims=True)


def _rope(x1, x2, cos, sin):
    return x1 * cos - x2 * sin, x2 * cos + x1 * sin


def _rope_bwd(d1, d2, cos, sin):
    return d1 * cos + d2 * sin, d2 * cos - d1 * sin


def _sigmoid(x):
    return 1.0 / (1.0 + jnp.exp(-x))


MLA_SCALE = 1.0 / math.sqrt(QK_HEAD)
MLA_PRESCALE = MLA_SCALE * math.log2(math.e)
MLA_TQ, MLA_KC = 1024, 1024


def _mla_fwd(q_full, k_full, kv, S, comm=None):
    tq, kc = min(MLA_TQ, S), min(MLA_KC, S)

    def body(q_ref, k_ref, v_ref, o_ref, lse_ref):
        q = q_ref[0]
        m = jnp.full((tq, 1), -1e30, F32)
        l = jnp.zeros((tq, 1), F32)
        acc = jnp.zeros((tq, V_DIM), F32)
        for c in range(S // kc):
            s = _dot(q, k_ref[0, c * kc:(c + 1) * kc, :], NT)
            m_new = jnp.maximum(m, jnp.max(s, axis=-1, keepdims=True))
            alpha = jnp.exp2(m - m_new)
            p = jnp.exp2(s - m_new)
            l = alpha * l + jnp.sum(p, axis=-1, keepdims=True)
            acc = alpha * acc + _dot(p.astype(BF16), v_ref[c * kc:(c + 1) * kc, :], NN)
            m = m_new
        o_ref[...] = (acc / l).astype(BF16)
        lse_ref[0] = m + jnp.log2(l)

    return _pcall(
        body, name="mla_fwd", grid=(H_A, S // tq),
        in_specs=[pl.BlockSpec((1, tq, QK_HEAD), lambda h, i: (h, i, 0)),
                  pl.BlockSpec((1, S, QK_HEAD), lambda h, i: (h, 0, 0)),
                  pl.BlockSpec((S, V_DIM), lambda h, i: (0, 2 * h + 1))],
        out_specs=[pl.BlockSpec((tq, V_DIM), lambda h, i: (i, h)),
                   pl.BlockSpec((1, tq, 1), lambda h, i: (h, i, 0))],
        out_shape=[jax.ShapeDtypeStruct((S, H_A * V_DIM), BF16), jax.ShapeDtypeStruct((H_A, S, 1), F32)],
        dims=("parallel", "parallel"), comm=comm)(q_full, k_full, kv)


def _mla_bwd(q_full, k_full, kv, do_a, o_a, lse, S, comm=None):
    tq, kc = min(MLA_TQ, S), min(MLA_KC, S)

    def body(q_ref, k_ref, v_ref, do_ref, o_ref, lse_ref, dqn_ref, dqr_ref, dkv_out, dkr_out, dk_ref, dv_ref):
        @pl.when(pl.program_id(1) == 0)
        def _():
            dk_ref[...] = jnp.zeros_like(dk_ref)
            dv_ref[...] = jnp.zeros_like(dv_ref)

        q = q_ref[0]
        do = do_ref[...]
        lse_q = lse_ref[0]
        delta = jnp.sum(do.astype(F32) * o_ref[...].astype(F32), axis=-1, keepdims=True)
        dq = jnp.zeros((tq, QK_HEAD), F32)
        for c in range(S // kc):
            k = k_ref[0, c * kc:(c + 1) * kc, :]
            v = v_ref[c * kc:(c + 1) * kc, :]
            p = jnp.exp2(_dot(q, k, NT) - lse_q)
            ds = (p * (_dot(do, v, NT) - delta)).astype(BF16)
            dq = dq + _dot(ds, k, NN)
            dk_ref[0, c * kc:(c + 1) * kc, :] += _dot(ds, q, TN)
            dv_ref[0, c * kc:(c + 1) * kc, :] += _dot(p.astype(BF16), do, TN)
        dq = dq * MLA_SCALE
        dqn_ref[...] = dq[:, :QK_NOPE].astype(BF16)
        dqr_ref[0] = dq[:, QK_NOPE:].astype(BF16)

        @pl.when(pl.program_id(1) == S // tq - 1)
        def _():
            dk = dk_ref[0] * math.log(2.0)
            dkv_out[...] = jnp.concatenate([dk[:, :QK_NOPE], dv_ref[0]], axis=1).astype(BF16)
            dkr_out[0] = dk[:, QK_NOPE:].astype(BF16)

    return _pcall(
        body, name="mla_bwd", grid=(H_A, S // tq),
        in_specs=[pl.BlockSpec((1, tq, QK_HEAD), lambda h, i: (h, i, 0)),
                  pl.BlockSpec((1, S, QK_HEAD), lambda h, i: (h, 0, 0)),
                  pl.BlockSpec((S, V_DIM), lambda h, i: (0, 2 * h + 1)),
                  pl.BlockSpec((tq, V_DIM), lambda h, i: (i, h)),
                  pl.BlockSpec((tq, V_DIM), lambda h, i: (i, h)),
                  pl.BlockSpec((1, tq, 1), lambda h, i: (h, i, 0))],
        out_specs=[pl.BlockSpec((tq, QK_NOPE), lambda h, i: (i, h)),
                   pl.BlockSpec((1, tq, QK_ROPE), lambda h, i: (h, i, 0)),
                   pl.BlockSpec((S, QK_NOPE + V_DIM), lambda h, i: (0, h)),
                   pl.BlockSpec((1, S, QK_ROPE), lambda h, i: (h, 0, 0))],
        out_shape=[jax.ShapeDtypeStruct((S, H_A * QK_NOPE), BF16), jax.ShapeDtypeStruct((H_A, S, QK_ROPE), BF16),
                   jax.ShapeDtypeStruct((S, H_A * (QK_NOPE + V_DIM)), BF16), jax.ShapeDtypeStruct((H_A, S, QK_ROPE), BF16)],
        scratch_shapes=[pltpu.VMEM((1, S, QK_HEAD), F32), pltpu.VMEM((1, S, V_DIM), F32)],
        dims=("parallel", "arbitrary"), comm=comm)(q_full, k_full, kv, do_a, o_a, lse)


WIN_SCALE = 1.0 / math.sqrt(HD_B)


WIN_PER_STEP = 4


def _win_specs(S):
    last, B = S // Q_BLOCK - 1, WIN_PER_STEP
    qspec = pl.BlockSpec((B * Q_BLOCK, H_B * HD_B), lambda i: (i, PROJ_QB))
    kspecs = [[pl.BlockSpec((Q_BLOCK, KV_B * HD_B), lambda i, d=d, c=c: (jnp.clip(B * i + d, 0, last), c)) for d in range(-1, B + 1)]
              for c in (PROJ_KB, PROJ_VB)]
    bias_spec = pl.BlockSpec((H_B, SPAN, Q_BLOCK), lambda i: (0, 0, 0))
    sink_spec = pl.BlockSpec((H_B, Q_BLOCK), lambda i: (0, 0))
    return qspec, kspecs, bias_spec, sink_spec


def _win_edge_ok(n, n_blk):
    row = lax.broadcasted_iota(jnp.int32, (SPAN, 1), 0)
    return jnp.logical_not(((n == 0) & (row < WINDOW)) | ((n == n_blk - 1) & (row >= SPAN - WINDOW)))


def _lanes4(pieces):
    return jnp.concatenate(pieces, axis=1)


def _win_probs(kg, q4t, bias_ref, sink_ref, g, edge_ok):
    bias4 = _lanes4([bias_ref[GROUP * g + j] for j in range(GROUP)])
    sink4 = _lanes4([sink_ref[GROUP * g + j:GROUP * g + j + 1, :] for j in range(GROUP)])
    s = jnp.where(edge_ok, _dot(kg, q4t, NN) + bias4, -1e30)
    m = jnp.maximum(jnp.max(s, axis=0, keepdims=True), sink4)
    p = jnp.exp(s - m)
    e_sink = jnp.exp(sink4 - m)
    inv_l = 1.0 / (jnp.sum(p, axis=0, keepdims=True) + e_sink)
    return p * inv_l, e_sink * inv_l


def _group_t(xt, g):
    return _lanes4([xt[HD_B * (GROUP * g + j):HD_B * (GROUP * g + j + 1), :] for j in range(GROUP)])


def _rows_of(ref, b):
    return ref[Q_BLOCK * b:Q_BLOCK * (b + 1), :]


def _win_fwd(proj, o_a, bias_t, sinks_b, S, comm=None):
    n_blk, B = S // Q_BLOCK, WIN_PER_STEP
    qspec, kspecs, bias_spec, sink_spec = _win_specs(S)
    rows = lambda col: pl.BlockSpec((B * Q_BLOCK, H_B * HD_B), lambda i: (i, col))

    def body(q_ref, *refs):
        k_refs, v_refs = refs[:B + 2], refs[B + 2:2 * B + 4]
        bias_ref, sink_ref, ga_ref, gb_ref, oa_ref, o_ref, mixed_ref = refs[2 * B + 4:]
        for b in range(B):
            edge_ok = _win_edge_ok(B * pl.program_id(0) + b, n_blk)
            k = jnp.concatenate([r[...] for r in k_refs[b:b + 3]], axis=0)
            vt = jnp.concatenate([r[...] for r in v_refs[b:b + 3]], axis=0).T
            qt = (_rows_of(q_ref, b).astype(F32) * WIN_SCALE).T.astype(BF16)
            parts = []
            for g in range(KV_B):
                p, _ = _win_probs(k[:, HD_B * g:HD_B * (g + 1)], _group_t(qt, g), bias_ref, sink_ref, g, edge_ok)
                o4t = _dot(vt[HD_B * g:HD_B * (g + 1), :], p.astype(BF16), NN)
                parts += [o4t[:, Q_BLOCK * j:Q_BLOCK * (j + 1)] for j in range(GROUP)]
            ob = jnp.concatenate(parts, axis=0).T
            o_ref[Q_BLOCK * b:Q_BLOCK * (b + 1), :] = ob.astype(BF16)
            ga, gb, oa = (_rows_of(r, b).astype(F32) for r in (ga_ref, gb_ref, oa_ref))
            mixed_ref[Q_BLOCK * b:Q_BLOCK * (b + 1), :] = (_sigmoid(ga) * oa + _sigmoid(gb) * ob).astype(BF16)

    return _pcall(body, name="win_fwd_mix", grid=(n_blk // B,),
                  in_specs=[qspec, *kspecs[0], *kspecs[1], bias_spec, sink_spec, rows(PROJ_GA), rows(PROJ_GB), rows(0)],
                  out_specs=[rows(0), rows(0)],
                  out_shape=[jax.ShapeDtypeStruct((S, H_B * HD_B), BF16)] * 2,
                  dims=("parallel",), comm=comm)(*[proj] * (2 * B + 5), bias_t, sinks_b, proj, proj, o_a)


def _win_bwd(proj, bias_t, sinks_b, do_b, d_proj, S):
    n_blk, B = S // Q_BLOCK, WIN_PER_STEP
    qspec, kspecs, bias_spec, sink_spec = _win_specs(S)

    def body(q_ref, *refs):
        k_refs, v_refs = refs[:B + 2], refs[B + 2:2 * B + 4]
        bias_ref, sink_ref, do_ref, _, dq_ref, dk_ref, dv_ref, dbias_ref, dsink_ref, dsink_acc = refs[2 * B + 4:]
        i = pl.program_id(0)

        @pl.when(i == 0)
        def _():
            dk_ref[...] = jnp.zeros_like(dk_ref)
            dv_ref[...] = jnp.zeros_like(dv_ref)
            dbias_ref[...] = jnp.zeros_like(dbias_ref)
            dsink_acc[...] = jnp.zeros_like(dsink_acc)

        d_bias, d_sink, dk_blocks, dv_blocks = {}, {}, [], []
        for b in range(B):
            edge_ok = _win_edge_ok(B * i + b, n_blk)
            k = jnp.concatenate([r[...] for r in k_refs[b:b + 3]], axis=0)
            v = jnp.concatenate([r[...] for r in v_refs[b:b + 3]], axis=0)
            kt = k.T
            qt = (_rows_of(q_ref, b).astype(F32) * WIN_SCALE).T.astype(BF16)
            dot_ = _rows_of(do_ref, b).astype(F32).T.astype(BF16)
            dq_parts, dks, dvs = [], [], []
            for g in range(KV_B):
                kg, vg = k[:, HD_B * g:HD_B * (g + 1)], v[:, HD_B * g:HD_B * (g + 1)]
                q4t, do4t = _group_t(qt, g), _group_t(dot_, g)
                p, p_sink = _win_probs(kg, q4t, bias_ref, sink_ref, g, edge_ok)
                dp = _dot(vg, do4t, NN)
                delta = jnp.sum(p * dp, axis=0, keepdims=True)
                ds = p * (dp - delta)
                d_bias[g] = ds if b == 0 else d_bias[g] + ds
                d_sink[g] = -p_sink * delta if b == 0 else d_sink[g] - p_sink * delta
                dsb = ds.astype(BF16)
                dq4t = _dot(kt[HD_B * g:HD_B * (g + 1), :], dsb, NN) * WIN_SCALE
                dq_parts += [dq4t[:, Q_BLOCK * j:Q_BLOCK * (j + 1)] for j in range(GROUP)]
                dks.append(_dot(dsb, q4t, NT))
                dvs.append(_dot(p.astype(BF16), do4t, NT))
            dq_ref[Q_BLOCK * b:Q_BLOCK * (b + 1), :] = jnp.concatenate(dq_parts, axis=0).T.astype(BF16)
            dk_blocks.append(jnp.concatenate(dks, axis=1))
            dv_blocks.append(jnp.concatenate(dvs, axis=1))

        for g in range(KV_B):
            for j in range(GROUP):
                dbias_ref[GROUP * g + j] += d_bias[g][:, Q_BLOCK * j:Q_BLOCK * (j + 1)]
            dsink_acc[g:g + 1, :] += d_sink[g]

        def overlap(blocks):
            out = blocks[0]
            for blk in blocks[1:]:
                keep = out.shape[0] - 2 * Q_BLOCK
                out = jnp.concatenate([out[:keep], out[keep:] + blk[:2 * Q_BLOCK], blk[2 * Q_BLOCK:]], axis=0)
            return out

        rows = pl.ds(pl.multiple_of(i * (B * Q_BLOCK), B * Q_BLOCK), (B + 2) * Q_BLOCK)
        dk_ref[rows, :] += overlap(dk_blocks)
        dv_ref[rows, :] += overlap(dv_blocks)

        @pl.when(i == n_blk // B - 1)
        def _():
            acc = dsink_acc[...]
            dsink_ref[...] = jnp.concatenate(
                [jnp.sum(acc[:, Q_BLOCK * j:Q_BLOCK * (j + 1)], axis=1, keepdims=True) for j in range(GROUP)], axis=1)

    whole = lambda shape: pl.BlockSpec(shape, lambda i: (0,) * len(shape))
    return _pcall(
        body, name="win_bwd", grid=(n_blk // B,),
        in_specs=[qspec, *kspecs[0], *kspecs[1], bias_spec, sink_spec, pl.BlockSpec((B * Q_BLOCK, H_B * HD_B), lambda i: (i, 0)), ANY],
        out_specs=[qspec, whole((S + 2 * WINDOW, KV_B * HD_B)),
                   whole((S + 2 * WINDOW, KV_B * HD_B)), whole((H_B, SPAN, Q_BLOCK)), whole((KV_B, GROUP))],
        out_shape=[jax.ShapeDtypeStruct((S, PROJ_P), BF16), jax.ShapeDtypeStruct((S + 2 * WINDOW, KV_B * HD_B), F32),
                   jax.ShapeDtypeStruct((S + 2 * WINDOW, KV_B * HD_B), F32), jax.ShapeDtypeStruct((H_B, SPAN, Q_BLOCK), F32),
                   jax.ShapeDtypeStruct((KV_B, GROUP), F32)],
        scratch_shapes=[pltpu.VMEM((KV_B, GROUP * Q_BLOCK), F32)],
        dims=("arbitrary",), aliases={2 * B + 8: 0})(*[proj] * (2 * B + 5), bias_t, sinks_b, do_b, d_proj)


def _bias_table(rel_bias_t, onehot_t, in_band):
    def body(rb_ref, oh_ref, band_ref, o_ref):
        t = lax.dot_general(rb_ref[...], oh_ref[...], NN, preferred_element_type=F32, precision=lax.Precision.HIGHEST)
        o_ref[...] = jnp.where(band_ref[...] > 0.5, t, -1e30)

    n = onehot_t.shape[1]
    tn = _tile(n, 8192)
    return _pcall(body, name="bias_table", grid=(n // tn,),
                  in_specs=[pl.BlockSpec((H_B, NUM_BUCKETS), lambda j: (0, 0)), pl.BlockSpec((NUM_BUCKETS, tn), lambda j: (0, j)),
                            pl.BlockSpec((1, tn), lambda j: (0, j))],
                  out_specs=pl.BlockSpec((H_B, tn), lambda j: (0, j)),
                  out_shape=jax.ShapeDtypeStruct((H_B, n), F32), dims=("parallel",))(rel_bias_t, onehot_t, in_band)


def _bias_table_bwd(dbias, onehot_t):
    n = onehot_t.shape[1]
    tk = _tile(n, 8192)

    def body(d_ref, oh_ref, o_ref):
        @pl.when(pl.program_id(0) == 0)
        def _():
            o_ref[...] = jnp.zeros_like(o_ref)

        o_ref[...] += lax.dot_general(d_ref[...], oh_ref[...], NT, preferred_element_type=F32, precision=lax.Precision.HIGHEST)

    return _pcall(body, name="bias_table_bwd", grid=(n // tk,),
                  in_specs=[pl.BlockSpec((H_B, tk), lambda j: (0, j)), pl.BlockSpec((NUM_BUCKETS, tk), lambda j: (0, j))],
                  out_specs=pl.BlockSpec((H_B, NUM_BUCKETS), lambda j: (0, 0)),
                  out_shape=jax.ShapeDtypeStruct((H_B, NUM_BUCKETS), F32), dims=("arbitrary",))(dbias, onehot_t)


CONV_STRIP = 128
N_STRIPS = D_FF // CONV_STRIP
CONV_ROWS = 128
HALO = 8


def _strip(rows, half):
    return pl.BlockSpec((rows, CONV_STRIP), lambda j: (0, j + half * N_STRIPS))


def _fill_padded(pad_ref, src_ref, halo, S):
    pad_ref[0:halo, :] = jnp.zeros((halo, CONV_STRIP), F32)
    pad_ref[halo + S:2 * halo + S, :] = jnp.zeros((halo, CONV_STRIP), F32)
    pad_ref[halo:halo + S, :] = src_ref[...].astype(F32)


def _conv_gate_fwd(u, conv_w, conv_b, S):
    R = min(CONV_ROWS, S)

    def body(ug_ref, uv_ref, wg_ref, wv_ref, bg_ref, bv_ref, a_ref, gpad, vpad):
        _fill_padded(gpad, ug_ref, HALO, S)
        _fill_padded(vpad, uv_ref, HALO, S)
        wg, wv, bg, bv = wg_ref[...], wv_ref[...], bg_ref[...], bv_ref[...]

        def conv(pad_ref, r0, w, b):
            dn, mid, up = (pad_ref[pl.ds(r0 + HALO + d, R), :] for d in (-1, 0, 1))
            return dn * w[0:1, :] + mid * w[1:2, :] + up * w[2:3, :] + b

        def step(c, carry):
            r0 = pl.multiple_of(c * R, R)
            g = conv(gpad, r0, wg, bg)
            val = conv(vpad, r0, wv, bv)
            a_ref[pl.ds(r0, R), :] = (g * _sigmoid(g) * val).astype(BF16)
            return carry

        lax.fori_loop(0, S // R, step, 0)

    return _pcall(body, name="conv_gate_fwd", grid=(N_STRIPS,),
                  in_specs=[_strip(S, 0), _strip(S, 1), _strip(3, 0), _strip(3, 1), _strip(1, 0), _strip(1, 1)],
                  out_specs=_strip(S, 0), out_shape=jax.ShapeDtypeStruct((S, D_FF), BF16),
                  scratch_shapes=[pltpu.VMEM((S + 2 * HALO, CONV_STRIP), F32)] * 2,
                  dims=("parallel",))(u, u, conv_w, conv_w, conv_b, conv_b)


def _conv_gate_bwd(u, conv_w, conv_b, da, S):
    R = min(CONV_ROWS, S)
    n = R + 2 * HALO

    def body(ug_ref, uv_ref, wg_ref, wv_ref, bg_ref, bv_ref, da_ref, dug_ref, duv_ref, dwg_ref, dwv_ref, dbg_ref, dbv_ref,
             gpad, vpad, dapad):
        _fill_padded(gpad, ug_ref, 2 * HALO, S)
        _fill_padded(vpad, uv_ref, 2 * HALO, S)
        _fill_padded(dapad, da_ref, HALO, S)
        wg, wv, bg, bv = wg_ref[...], wv_ref[...], bg_ref[...], bv_ref[...]

        def conv(pad_ref, r0, w, b):
            dn, mid, up = (pad_ref[pl.ds(r0 + HALO + d, n), :] for d in (-1, 0, 1))
            return dn * w[0:1, :] + mid * w[1:2, :] + up * w[2:3, :] + b, mid[HALO:HALO + R]

        def conv_bwd(duc, u_mid, w, r0, du_ref):
            dn, mid, up = pltpu.roll(duc, 1, axis=0)[HALO:HALO + R], duc[HALO:HALO + R], pltpu.roll(duc, n - 1, axis=0)[HALO:HALO + R]
            du_ref[pl.ds(r0, R), :] = (up * w[0:1, :] + mid * w[1:2, :] + dn * w[2:3, :]).astype(BF16)
            dw = jnp.concatenate([jnp.sum(up * u_mid, axis=0, keepdims=True), jnp.sum(mid * u_mid, axis=0, keepdims=True),
                                  jnp.sum(dn * u_mid, axis=0, keepdims=True)], axis=0)
            return dw, jnp.sum(mid, axis=0, keepdims=True)

        def step(c, carry):
            dw_g, db_g, dw_v, db_v = carry
            r0 = pl.multiple_of(c * R, R)
            g, ug_mid = conv(gpad, r0, wg, bg)
            val, uv_mid = conv(vpad, r0, wv, bv)
            da_ext = dapad[pl.ds(r0, n), :]
            sg = _sigmoid(g)
            ddw_v, ddb_v = conv_bwd(da_ext * (g * sg), uv_mid, wv, r0, duv_ref)
            ddw_g, ddb_g = conv_bwd(da_ext * val * (sg * (1.0 + g * (1.0 - sg))), ug_mid, wg, r0, dug_ref)
            return dw_g + ddw_g, db_g + ddb_g, dw_v + ddw_v, db_v + ddb_v

        z3, z1 = jnp.zeros((3, CONV_STRIP), F32), jnp.zeros((1, CONV_STRIP), F32)
        dwg_ref[...], dbg_ref[...], dwv_ref[...], dbv_ref[...] = lax.fori_loop(0, S // R, step, (z3, z1, z3, z1))

    half = lambda r, dt: (_strip(r, 0), jax.ShapeDtypeStruct((r, D_FF), dt))
    outs = [half(S, BF16), half(S, BF16), half(3, F32), half(3, F32), half(1, F32), half(1, F32)]
    return _pcall(
        body, name="conv_gate_bwd", grid=(N_STRIPS,),
        in_specs=[_strip(S, 0), _strip(S, 1), _strip(3, 0), _strip(3, 1), _strip(1, 0), _strip(1, 1), _strip(S, 0)],
        out_specs=[o[0] for o in outs], out_shape=[o[1] for o in outs],
        scratch_shapes=[pltpu.VMEM((S + 4 * HALO, CONV_STRIP), F32)] * 2 + [pltpu.VMEM((S + 2 * HALO, CONV_STRIP), F32)],
        dims=("parallel",))(u, u, conv_w, conv_w, conv_b, conv_b, da)


MESH = pl.DeviceIdType.MESH
ANY = pl.BlockSpec(memory_space=pl.ANY)


def _place():
    return lax.axis_index("x"), lax.axis_index("y"), lax.axis_index("c")


def _gather2(ins, outs, sems, finish):
    send_sems, recv_sems, local_sems = sems
    n_arr = len(ins)
    x, y, c = _place()
    me, sibling = (x, y, c), (x, y, 1 - c)
    chips = [(1 - x, y), (x, 1 - y), (1 - x, 1 - y)]

    def slot(a, p):
        return outs[a].at[4 * p[0] + 2 * p[1] + p[2]]

    def copy(a, k, block, to, src=None):
        return pltpu.make_async_remote_copy(
            src_ref=slot(a, block) if src is None else src, dst_ref=slot(a, block),
            send_sem=send_sems.at[a, k], recv_sem=recv_sems.at[a, k], device_id=to, device_id_type=MESH)

    mine = [pltpu.make_async_copy(ins[a], slot(a, me), local_sems.at[a]) for a in range(n_arr)]
    first = []
    for a in range(n_arr):
        first.append(copy(a, 0, me, sibling, src=ins[a]))
        first += [copy(a, 1 + j, me, (*chip, c), src=ins[a]) for j, chip in enumerate(chips)]
    if not finish:
        for cp in mine + first:
            cp.start()
        return
    passed = []
    for j, chip in enumerate(chips):
        for a in range(n_arr):
            copy(a, 1 + j, (*chip, c), me).wait_recv()
            cp = copy(a, 4 + j, (*chip, c), sibling)
            cp.start()
            passed.append(cp)
    for a in range(n_arr):
        copy(a, 0, sibling, me).wait_recv()
        for j, chip in enumerate(chips):
            copy(a, 4 + j, (*chip, 1 - c), me).wait_recv()
    for cp in first + passed:
        cp.wait_send()
    for cp in mine:
        cp.wait()


def _xchg_out_shapes(stacked, replicated):
    return ([jax.ShapeDtypeStruct(s.shape, s.dtype) for s in stacked]
            + [jax.ShapeDtypeStruct((N_DEV, *r.shape), r.dtype) for r in replicated])


def _xchg_sems(n_arr):
    return [pltpu.SemaphoreType.DMA((n_arr, 7)), pltpu.SemaphoreType.DMA((n_arr, 7)), pltpu.SemaphoreType.DMA((n_arr,))]


def _xchg_copies(ins, outs, sems, n_st, with_recv):
    send_sems, recv_sems, local_sems = sems
    n_arr = len(ins)
    x, y, c = _place()
    me = 4 * x + 2 * y + c

    def src(a, idx):
        return ins[a].at[idx] if a < n_st else ins[a]

    mine = [pltpu.make_async_copy(src(a, me), outs[a].at[me], local_sems.at[a]) for a in range(n_arr)]
    pairs = []
    for k in range(1, N_DEV):
        px, py, pc = x ^ (k >> 2), y ^ ((k >> 1) & 1), c ^ (k & 1)
        peer = 4 * px + 2 * py + pc
        for a in range(n_arr):
            sems_k = dict(send_sem=send_sems.at[a, k - 1], recv_sem=recv_sems.at[a, k - 1], device_id_type=MESH)
            send = pltpu.make_async_remote_copy(src_ref=src(a, peer), dst_ref=outs[a].at[me], device_id=(px, py, pc), **sems_k)
            recv = None
            if with_recv:
                recv = pltpu.make_async_remote_copy(src_ref=src(a, peer), dst_ref=outs[a].at[peer], device_id=(x, y, c), **sems_k)
            pairs.append((send, recv))
    return mine, pairs


def _xchg_start(ins, outs, sems, n_st):
    mine, pairs = _xchg_copies(ins, outs, sems, n_st, False)
    for cp in mine:
        cp.start()
    for send, _ in pairs:
        send.start()


def _xchg_finish(ins, outs, sems, n_st):
    mine, pairs = _xchg_copies(ins, outs, sems, n_st, True)
    for _, recv in pairs:
        recv.wait_recv()
    for send, _ in pairs:
        send.wait_send()
    for cp in mine:
        cp.wait()


def _exchange(stacked, replicated, name):
    _, landed = _pcall(lambda: None, name=name, grid=(), in_specs=[], out_specs=[], out_shape=[], comm=(stacked, replicated))()
    return landed


HBM = pl.BlockSpec(memory_space=pltpu.HBM)
SEMS = pl.BlockSpec(memory_space=pltpu.SEMAPHORE)
SIDE_EFFECT = pltpu.SideEffectType.DATAFLOW_SIDE_EFFECTING


N_SPLIT_SEMS = 2 * (N_DEV - 1)


def _split_copies(src, land, sems, with_recv):
    x, y, c = _place()
    me = 4 * x + 2 * y + c
    pairs = []
    for k in range(1, N_DEV):
        px, py, pc = x ^ (k >> 2), y ^ ((k >> 1) & 1), c ^ (k & 1)
        peer = 4 * px + 2 * py + pc
        sems_k = dict(send_sem=sems[k - 1], recv_sem=sems[N_DEV - 1 + k - 1], device_id_type=MESH)
        send = pltpu.make_async_remote_copy(src_ref=src.at[peer], dst_ref=land.at[me], device_id=(px, py, pc), **sems_k)
        recv = None
        if with_recv:
            recv = pltpu.make_async_remote_copy(src_ref=src.at[peer], dst_ref=land.at[peer], device_id=(x, y, c), **sems_k)
        pairs.append((send, recv))
    return pairs


def _exchange_start(stacked, name):
    def body(src, land, *rest):
        for send, _ in _split_copies(src, land, rest[:N_SPLIT_SEMS], False):
            send.start()
        rest[-1][...] = jnp.zeros_like(rest[-1])

    shape = pltpu.HBM(stacked.shape, stacked.dtype)
    res = pl.pallas_call(
        body, name=name, in_specs=[HBM, HBM],
        out_shape=(*[pltpu.SemaphoreType.DMA(())] * N_SPLIT_SEMS, shape, shape, jax.ShapeDtypeStruct((8, 128), F32)),
        out_specs=(*[SEMS] * N_SPLIT_SEMS, HBM, HBM, pl.BlockSpec(memory_space=pltpu.VMEM)),
        input_output_aliases={0: N_SPLIT_SEMS, 1: N_SPLIT_SEMS + 1},
        compiler_params=pltpu.CompilerParams(has_side_effects=SIDE_EFFECT),
    )(pltpu.with_memory_space_constraint(stacked, pltpu.HBM),
      pltpu.with_memory_space_constraint(lax.empty(stacked.shape, stacked.dtype), pltpu.HBM))
    return res[:N_SPLIT_SEMS], res[N_SPLIT_SEMS], res[N_SPLIT_SEMS + 1], res[-1]


def _exchange_wait(sems, src, land, after, name):
    def body(src_ref, land_ref, *rest):
        for send, recv in _split_copies(src_ref, land_ref, rest[:N_SPLIT_SEMS], True):
            send.wait_send()
            recv.wait_recv()

    shape = pltpu.HBM(src.shape, src.dtype)
    return pl.pallas_call(
        body, name=name, in_specs=[HBM, HBM, *[SEMS] * N_SPLIT_SEMS, ANY],
        out_shape=(shape, shape), out_specs=(HBM, HBM), input_output_aliases={0: 0, 1: 1},
        compiler_params=pltpu.CompilerParams(has_side_effects=SIDE_EFFECT))(src, land, *sems, after)[1]


def _adamw(parts, w, m, v, name):
    _, R, C = w.shape
    tr = R if R <= 512 else max(t for t in range(16, 513, 16) if R % t == 0)
    pr = tr if parts.shape[1] == R else -(-R // 16) * 16
    assert pr == tr or tr == R

    def body(p_ref, w_ref, m_ref, v_ref, g_ref, d_ref, nm_ref, nv_ref):
        g = p_ref[0].astype(F32)[:tr]
        for s in range(1, N_DEV):
            g = g + p_ref[s].astype(F32)[:tr]
        m2 = ADAM_B1 * m_ref[0] + (1.0 - ADAM_B1) * g
        v2 = ADAM_B2 * v_ref[0] + (1.0 - ADAM_B2) * (g * g)
        m_hat = m2 / (1.0 - ADAM_B1 ** ADAM_STEP)
        v_hat = v2 / (1.0 - ADAM_B2 ** ADAM_STEP)
        g_ref[0] = g
        d_ref[0] = -ADAM_LR * (m_hat / (jnp.sqrt(v_hat) + ADAM_EPS) + ADAM_WD * w_ref[0])
        nm_ref[0] = m2
        nv_ref[0] = v2

    blk = pl.BlockSpec((1, tr, C), lambda i: (0, i, 0))
    return _pcall(body, name=name, grid=(R // tr,),
                  in_specs=[pl.BlockSpec((N_DEV, pr, C), lambda i: (0, i, 0)), blk, blk, blk],
                  out_specs=[blk] * 4, out_shape=[jax.ShapeDtypeStruct((1, R, C), F32)] * 4,
                  dims=("parallel",))(parts, w, m, v)


def _t5_bucket(rel):
    nb = NUM_BUCKETS // 2
    max_exact = nb // 2
    base = (rel > 0).astype(jnp.int32) * nb
    n = jnp.abs(rel)
    nf = jnp.maximum(n, 1).astype(jnp.float32)
    large = max_exact + (jnp.log(nf / max_exact) / math.log(MAX_DISTANCE / max_exact) * (nb - max_exact)).astype(jnp.int32)
    large = jnp.minimum(large, nb - 1)
    return base + jnp.where(n < max_exact, n, large)


def _unstack_cols(g):
    return jnp.transpose(g, (1, 0, 2)).reshape(g.shape[1], N_DEV * g.shape[2])


def _stack_cols(w, n=N_DEV):
    R = w.shape[0]
    return jnp.transpose(w.reshape(R, n, w.shape[1] // n), (1, 0, 2))


def _stack_halves(g, v):
    return jnp.concatenate([_stack_cols(g, N_DEV // 2), _stack_cols(v, N_DEV // 2)], axis=0)


def kernel(x, positions, norm1_g, w_in, q_a_norm_g, w_q_b, kv_a_norm_g, w_kv_b, rel_bias, sinks, w_out, norm2_g, w_up, conv_w, conv_b, w_down, final_norm_g, loss_target, m_norm1_g, m_w_in, m_q_a_norm_g, m_w_q_b, m_kv_a_norm_g, m_w_kv_b, m_rel_bias, m_sinks, m_w_out, m_norm2_g, m_w_up, m_conv_w, m_conv_b, m_w_down, m_final_norm_g, v_norm1_g, v_w_in, v_q_a_norm_g, v_w_q_b, v_kv_a_norm_g, v_w_kv_b, v_rel_bias, v_sinks, v_w_out, v_norm2_g, v_w_up, v_conv_w, v_conv_b, v_w_down, v_final_norm_g):
    S = x.shape[1]
    x = x[0]
    target = loss_target[0]
    TM = 256

    tr = lambda w: jnp.swapaxes(w, 1, 2)
    (h1,), (g_in,) = _rowwise(lambda a, g: (_rms(a, g),), "norm1_gather", S, TM, [_rows(x), _whole(norm1_g)], [("rows", D_MODEL, BF16)],
                              gather=[tr(w_in)[0].astype(BF16)])
    late_weights = [w_out[0].astype(BF16), tr(w_up)[0].astype(BF16), conv_w[0]]
    wi = g_in.reshape(W_IN_COLS, D_MODEL)
    c0, c1, c2, c3, c4, c5 = (sum(W_IN_SIZES[:i + 1]) for i in range(6))
    w_in_pt = jnp.concatenate([wi[c4:c5], wi[c5:], wi[c1:c2], wi[:c0], wi[c2:c3], wi[c3:c4],
                               wi[c0:c0 + KV_LORA], wi[c0 + KV_LORA:c1], jnp.zeros((64, D_MODEL), BF16)], axis=0)

    half = QK_ROPE // 2
    inv_freq = ROPE_THETA ** (-jnp.arange(half, dtype=F32) / half)
    ang = positions.astype(F32)[:, None] * inv_freq[None, :]
    cos, sin = jnp.cos(ang), jnp.sin(ang)
    qa = jnp.arange(Q_BLOCK, dtype=jnp.int32)[:, None]
    kc = jnp.arange(SPAN, dtype=jnp.int32)[None, :]
    rel = (kc - WINDOW - qa).T
    in_band = (jnp.abs(rel) <= WINDOW).astype(F32).reshape(1, Q_BLOCK * SPAN)
    onehot_t = (_t5_bucket(rel).reshape(1, Q_BLOCK * SPAN) == jnp.arange(NUM_BUCKETS, dtype=jnp.int32)[:, None]).astype(F32)
    bias_t = _bias_table(rel_bias.T, onehot_t, in_band).reshape(H_B, SPAN, Q_BLOCK)
    sinks_b = jnp.broadcast_to(sinks.reshape(H_B, 1), (H_B, Q_BLOCK))

    proj, (g_qb, g_kvb) = _matmul(h1, w_in_pt, "nt", BF16, "proj", comm=([], [tr(w_q_b)[0].astype(BF16), w_kv_b[0].astype(BF16)]))
    wq = g_qb.reshape(H_A, QK_HEAD, Q_LORA)
    w_qb_pt = jnp.concatenate([wq[:, :QK_NOPE].reshape(H_A * QK_NOPE, Q_LORA), wq[:, QK_NOPE:].reshape(H_A * QK_ROPE, Q_LORA)], axis=0)
    w_kvb = _unstack_cols(g_kvb)

    def lat_fn(qlat, ckv, kr, gq, gkv, cs, sn):
        r1, r2 = _rope(kr[:, :half], kr[:, half:QK_ROPE], cs, sn)
        return _rms(qlat, gq), _rms(ckv, gkv), jnp.concatenate([r1, r2], axis=1)

    qn, ckvn, k_rope = _rowwise(lat_fn, "latents", S, TM,
                                [_rows(proj, 256, PROJ_QLAT), _rows(proj, 128, PROJ_CKV), _rows(proj, 128, PROJ_KROPE),
                                 _whole(q_a_norm_g), _whole(kv_a_norm_g), _rows(cos), _rows(sin)],
                                [("rows", Q_LORA, BF16), ("rows", KV_LORA, BF16), ("rows", QK_ROPE, BF16)])
    def q_heads_fn(q, cs, sn):
        q = q * MLA_PRESCALE
        outs = []
        for h in range(H_A):
            o = H_A * QK_NOPE + QK_ROPE * h
            r1, r2 = _rope(q[:, o:o + half], q[:, o + half:o + QK_ROPE], cs, sn)
            outs.append(jnp.concatenate([q[:, QK_NOPE * h:QK_NOPE * (h + 1)], r1, r2], axis=1)[None])
        return (jnp.concatenate(outs, axis=0),)

    (q_full,) = _matmul(qn, w_qb_pt, "nt", None, "q_up_heads", tm=512, tn=1536,
                        epi=(q_heads_fn, [_rows(cos), _rows(sin)], [("heads", H_A, QK_HEAD, BF16)]))

    def k_heads_fn(kvf, kr):
        return kvf, jnp.concatenate([jnp.concatenate([kvf[:, 256 * h:256 * h + QK_NOPE], kr], axis=1)[None] for h in range(H_A)], axis=0)

    kv, k_full = _matmul(ckvn, w_kvb, "nn", None, "kv_up_heads", tm=512, tn=2048,
                         epi=(k_heads_fn, [_rows(k_rope)], [("rows", H_A * (QK_NOPE + V_DIM), BF16), ("heads", H_A, QK_HEAD, BF16)]))
    (o_a, lse), (g_out, g_up, g_cw) = _mla_fwd(q_full, k_full, kv, S, comm=([], late_weights))
    w_out_f = g_out.reshape(D_MODEL, D_MODEL)
    w_up_t = g_up.reshape(2 * D_FF, D_MODEL)
    conv_w_f = _unstack_cols(g_cw)

    (o_b, mixed), (g_down,) = _win_fwd(proj, o_a, bias_t, sinks_b, S, comm=([], [w_down[0].astype(BF16)]))
    w_down_f = g_down.reshape(D_FF, D_MODEL)

    x1, h2 = _matmul(mixed, w_out_f, "nn", None, "out_proj", residual=x, tm=512,
                     epi=(lambda a, g: (a, _rms(a, g)), [_whole(norm2_g)], [("rows", D_MODEL, F32), ("rows", D_MODEL, BF16)]))
    u = _matmul(h2, w_up_t, "nt", BF16, "ffn_up", tn=1408)
    act = _conv_gate_fwd(u, conv_w_f, conv_b, S)

    def final_fn(a, g, t):
        err = _rms(a, g) - t
        loss = 0.5 * jnp.sum(jnp.mean(err * err, axis=-1, keepdims=True), axis=0, keepdims=True)
        dx, dg = _rms_bwd(err * (1.0 / D_MODEL), a, g)
        return dx, dx, dg, jnp.broadcast_to(loss, (1, 128))

    gfin = final_norm_g.reshape(1, D_MODEL)
    dx2, dx2_b, d_gfin, loss_row = _matmul(
        act, w_down_f, "nn", None, "ffn_down_loss", residual=x1, tm=512,
        epi=(final_fn, [_whole(gfin), _rows(target)],
             [("rows", D_MODEL, F32), ("rows", D_MODEL, BF16), ("acc", 1, D_MODEL), ("acc", 1, 128)]))
    d_act = _matmul(dx2_b, w_down_f, "nt", BF16, "ffn_down_dx", tn=1408)
    d_w_down = _matmul(act, dx2_b, "tn", BF16, "ffn_down_dw")
    du_g, du_v, dcw_g, dcw_v, dcb_g, dcb_v = _conv_gate_bwd(u, conv_w_f, conv_b, d_act, S)
    d_conv_b = jnp.concatenate([dcb_g, dcb_v], axis=1)

    def norm_bwd_fn(dh, a, g, dres):
        dx, dg = _rms_bwd(dh, a, g)
        dx = dx + dres
        return dx, dx, dg

    dx1, dx1_b, d_g2 = _matmul(du_g, w_up_t, "nn", None, "ffn_up_dx_norm2_bwd", tm=256, a2=du_v,
                               epi=(norm_bwd_fn, [_rows(x1), _whole(norm2_g), _rows(dx2)],
                                    [("rows", D_MODEL, F32), ("rows", D_MODEL, BF16), ("acc", 1, D_MODEL)]))
    d_w_up_t = _matmul(du_g, h2, "tn", BF16, "ffn_up_dw", tm=256, a2=du_v)
    d_w_out = _matmul(mixed, dx1_b, "tn", BF16, "out_proj_dw", tm=512)

    def gate_bwd_fn(dm, ga, gb, oa, ob):
        sa, sb = _sigmoid(ga), _sigmoid(gb)
        return jnp.concatenate([dm * oa * sa * (1.0 - sa), dm * ob * sb * (1.0 - sb)], axis=1), dm * sa, dm * sb

    d_proj, do_a, do_b = _matmul(
        dx1_b, w_out_f, "nt", None, "out_proj_dx_gate_bwd", tm=512,
        epi=(gate_bwd_fn, [_rows(proj, 1024, PROJ_GA), _rows(proj, 1024, PROJ_GB), _rows(o_a), _rows(o_b)],
             [("cols", 2 * D_MODEL, 0, PROJ_P, BF16), ("rows", D_MODEL, BF16), ("rows", D_MODEL, BF16)]))

    d_proj, dk_acc, dv_acc, d_bias, d_sinks_g = _win_bwd(proj, bias_t, sinks_b, do_b, d_proj, S)
    d_sinks = d_sinks_g.reshape(1, H_B)

    early = [d_w_out.reshape(N_DEV, D_MODEL // N_DEV, D_MODEL), d_w_up_t.reshape(N_DEV, 2 * D_FF // N_DEV, D_MODEL),
             d_w_down.reshape(N_DEV, D_FF // N_DEV, D_MODEL), _stack_halves(dcw_g, dcw_v)]
    (dq_nope, dqr_heads, dkv, dkr_heads), recv_early = _mla_bwd(q_full, k_full, kv, do_a, o_a, lse, S, comm=(early, []))

    def dqr_post_fn(dqr, cs, sn):
        rope = []
        for h in range(H_A):
            rope += list(_rope_bwd(dqr[h, :, :half], dqr[h, :, half:], cs, sn))
        return (jnp.concatenate(rope, axis=1),)

    (dq_rope,) = _rowwise(dqr_post_fn, "dqr_post", S, TM, [_heads(dqr_heads), _rows(cos), _rows(sin)], [("rows", H_A * QK_ROPE, BF16)])

    d_qn = _matmul(dq_nope, w_qb_pt, "nn", F32, "q_up_dx", a2=dq_rope)
    d_ckvn = _matmul(dkv, w_kvb, "nt", F32, "kv_up_dx")

    def lat_bwd_fn(dqn, dckvn, dkr_h, cs, sn, qlat, ckv, gq, gkv, dkb, dvb):
        dql, dgq = _rms_bwd(dqn, qlat, gq)
        dck, dgkv = _rms_bwd(dckvn, ckv, gkv)
        dkr = dkr_h[0]
        for h in range(1, H_A):
            dkr = dkr + dkr_h[h]
        r1, r2 = _rope_bwd(dkr[:, :half], dkr[:, half:], cs, sn)
        tail = jnp.concatenate([dql, dkb, dvb, dck, r1, r2, jnp.zeros_like(dkr)], axis=1)
        return tail, dgq, dgkv

    shifted = lambda arr: (arr, lambda tm: pl.BlockSpec((tm, arr.shape[1]), lambda i, *_: (i + WINDOW // tm, 0)))
    TL = min(128, S)
    d_proj, d_gq, d_gkv = _rowwise(lat_bwd_fn, "latents_bwd", S, TL,
                                   [_rows(d_qn), _rows(d_ckvn), _heads(dkr_heads), _rows(cos), _rows(sin), _rows(proj, 256, PROJ_QLAT), _rows(proj, 128, PROJ_CKV),
                                    _whole(q_a_norm_g), _whole(kv_a_norm_g), shifted(dk_acc), shifted(dv_acc)],
                                   [("cols", 1024, 3, PROJ_P, BF16), ("acc", 1, Q_LORA), ("acc", 1, KV_LORA)], into=(d_proj, 0))
    dp = _matmul(d_proj, h1, "tn", BF16, "proj_dw", tm=512)

    late = jnp.concatenate([dp[3072:3328], dp[3840:3968], dp[3968:4032], dp[2048:3072], dp[3328:3584],
                            dp[3584:3840], dp[0:1024], dp[1024:2048]], axis=0).reshape(N_DEV, W_IN_COLS // N_DEV, D_MODEL)
    late_sems, late_src, late_land, started = _exchange_start(late, "late_grads_start")

    def norm1_bwd_fn(dh, a, g, dres):
        dx, dg = _rms_bwd(dh, a, g)
        return dx + dres, dg

    grad_x, d_g1 = _matmul(
        d_proj, w_in_pt, "nn", None, "proj_dx_norm1_bwd", tm=512,
        epi=(norm1_bwd_fn, [_rows(x), _whole(norm1_g + started[:1, :1]), _rows(dx1)], [("rows", D_MODEL, F32), ("acc", 1, D_MODEL)]))

    transposed = ("w_in", "w_q_b", "w_up")
    ready_names = ["w_out", "w_up", "w_down", "conv_w"]
    ready_wmv = [(w_out, m_w_out, v_w_out), (tr(w_up), tr(m_w_up), tr(v_w_up)), (w_down, m_w_down, v_w_down), (conv_w, m_conv_w, v_conv_w)]
    big = {n: _adamw(r, *wmv, "adamw_" + n) for n, r, wmv in zip(ready_names, recv_early, ready_wmv)}

    (d_bias, dq_nope_l, dkv_l), _ = lax.optimization_barrier(((d_bias, dq_nope, dkv), started))
    d_rel_bias = _bias_table_bwd(d_bias.reshape(H_B, Q_BLOCK * SPAN), onehot_t).T
    d_w_qb_pt = _matmul(dq_nope_l, qn, "tn", BF16, "q_up_dw", tm=512, a2=dq_rope)
    d_w_kvb = _matmul(ckvn, dkv_l, "tn", BF16, "kv_up_dw", tn=2048)
    d_w_qb_t = jnp.concatenate([d_w_qb_pt[:H_A * QK_NOPE].reshape(H_A, QK_NOPE, Q_LORA),
                                d_w_qb_pt[H_A * QK_NOPE:].reshape(H_A, QK_ROPE, Q_LORA)], axis=1)

    after = lax.optimization_barrier([big[n][0] for n in ready_names] + [d_rel_bias, d_w_qb_t, d_w_kvb])
    landed = _exchange_wait(late_sems, late_src, late_land, after[-1], "late_grads_wait")
    me = 4 * lax.axis_index("x") + 2 * lax.axis_index("y") + lax.axis_index("c")
    landed = lax.dynamic_update_slice_in_dim(landed, lax.dynamic_slice_in_dim(late, me, 1, axis=0), me, axis=0)
    big["w_in"] = _adamw(landed, tr(w_in), tr(m_w_in), tr(v_w_in), "adamw_w_in")

    small_parts = [d_g1, d_gq, d_gkv, d_rel_bias.reshape(1, NUM_BUCKETS * H_B), d_sinks, d_g2, d_conv_b, d_gfin, loss_row[:, :1]]
    small = jnp.concatenate(small_parts, axis=1)
    n_small = small.shape[1]
    pad = (-n_small) % 128
    small = jnp.pad(small, ((0, 0), (0, pad)))
    small, _ = lax.optimization_barrier((small, [landed, *after]))
    recv_qb, recv_kvb, recv_small = _exchange([after[-2], _stack_cols(after[-1])], [small], "exchange_small_grads")
    big["w_q_b"] = _adamw(recv_qb, tr(w_q_b), tr(m_w_q_b), tr(v_w_q_b), "adamw_w_q_b")
    big["w_kv_b"] = _adamw(recv_kvb, w_kv_b, m_w_kv_b, v_w_kv_b, "adamw_w_kv_b")

    def flat(a):
        return a.reshape(1, -1)

    small_w = [norm1_g, q_a_norm_g, kv_a_norm_g, rel_bias, sinks, norm2_g, conv_b, final_norm_g]
    small_m = [m_norm1_g, m_q_a_norm_g, m_kv_a_norm_g, m_rel_bias, m_sinks, m_norm2_g, m_conv_b, m_final_norm_g]
    small_v = [v_norm1_g, v_q_a_norm_g, v_kv_a_norm_g, v_rel_bias, v_sinks, v_norm2_g, v_conv_b, v_final_norm_g]
    cat = lambda parts: jnp.pad(jnp.concatenate([flat(a) for a in parts], axis=1), ((0, 0), (0, pad + 1)))[None]
    sm = _adamw(recv_small, cat(small_w), cat(small_m), cat(small_v), "adamw_small")

    loss = sm[0][0, 0, n_small - 1]
    order =["norm1_g", "w_in", "q_a_norm_g", "w_q_b", "kv_a_norm_g", "w_kv_b", "rel_bias", "sinks", "w_out", "norm2_g", "w_up",
             "conv_w", "conv_b", "w_down", "final_norm_g"]
    small_names = ["norm1_g", "q_a_norm_g", "kv_a_norm_g", "rel_bias", "sinks", "norm2_g", "conv_b", "final_norm_g"]
    offs, o = {}, 0
    for n, a in zip(small_names, small_w):
        offs[n] = (o, a.size, a.shape)
        o += a.size
    outs = [loss, grad_x[None]]
    for kind in range(4):
        for n in order:
            if n in big:
                outs.append(tr(big[n][kind]) if n in transposed else big[n][kind])
            else:
                o, size, shape = offs[n]
                outs.append(sm[kind][0, 0, o:o + size].reshape(shape))
    return tuple(outs)
```

```python
import math

import jax
import jax.numpy as jnp
from jax import lax
from jax.experimental import pallas as pl
from jax.experimental.pallas import tpu as pltpu

F32 = jnp.float32
BF16 = jnp.bfloat16

N_DEV = 8
D_MODEL = 1024
EPS = 1e-6
H_A, QK_NOPE, QK_ROPE, V_DIM, Q_LORA, KV_LORA = 8, 128, 64, 128, 256, 128
QK_HEAD = QK_NOPE + QK_ROPE
HEAD_PAD = 256
ROPE_THETA = 10000.0
H_B, KV_B, GROUP, HD_B, WINDOW, Q_BLOCK = 16, 4, 4, 64, 128, 128
SPAN = Q_BLOCK + 2 * WINDOW
NUM_BUCKETS, MAX_DISTANCE = 32, 128
D_FF = 2816
ADAM_LR, ADAM_B1, ADAM_B2, ADAM_EPS, ADAM_WD, ADAM_STEP = 0.001, 0.9, 0.999, 1e-08, 0.01, 10

W_IN_SIZES = (Q_LORA, KV_LORA + QK_ROPE, H_B * HD_B, KV_B * HD_B, KV_B * HD_B, D_MODEL, D_MODEL)
W_IN_COLS = sum(W_IN_SIZES)
PROJ_P = 4096
PROJ_GA, PROJ_GB, PROJ_QB, PROJ_QLAT, PROJ_KB, PROJ_VB, PROJ_CKV, PROJ_KROPE = 0, 1, 2, 12, 13, 14, 30, 31

VMEM_LIMIT = 56 * 1024 * 1024

NN = (((1,), (0,)), ((), ()))
NT = (((1,), (1,)), ((), ()))
TN = (((0,), (0,)), ((), ()))


def _pcall(body, *, name, grid, in_specs, out_specs, out_shape, scratch_shapes=(), dims=None, comm=None, aliases=None, two_level=False):
    if comm is None:
        params = pltpu.CompilerParams(dimension_semantics=dims, vmem_limit_bytes=VMEM_LIMIT)
        return pl.pallas_call(body, name=name, grid=grid, in_specs=in_specs, out_specs=out_specs, out_shape=out_shape,
                              scratch_shapes=list(scratch_shapes), input_output_aliases=aliases or {}, compiler_params=params)
    assert not aliases
    stacked, replicated = comm
    arrs = [*stacked, *replicated]
    n_st, n_arr = len(stacked), len(arrs)
    single = not isinstance(out_specs, (list, tuple))
    o_specs, o_shape = ([out_specs], [out_shape]) if single else (list(out_specs), list(out_shape))
    n_in, n_out = len(in_specs), len(o_specs)

    def wrapped(*refs):
        c_in = refs[n_in:n_in + n_arr]
        c_out = refs[n_in + n_arr + n_out:n_in + 2 * n_arr + n_out]
        sems = refs[len(refs) - 3:]
        own = (*refs[:n_in], *refs[n_in + n_arr:n_in + n_arr + n_out], *refs[n_in + 2 * n_arr + n_out:len(refs) - 3])
        if two_level:
            assert n_st == 0
            start, finish = (lambda: _gather2(c_in, c_out, sems, False)), (lambda: _gather2(c_in, c_out, sems, True))
        else:
            start, finish = (lambda: _xchg_start(c_in, c_out, sems, n_st)), (lambda: _xchg_finish(c_in, c_out, sems, n_st))
        if not grid:
            start()
            finish()
            return
        first = last = None
        for d, n in enumerate(grid):
            pid = pl.program_id(d)
            first = (pid == 0) if first is None else first & (pid == 0)
            last = (pid == n - 1) if last is None else last & (pid == n - 1)

        pl.when(first)(start)
        body(*own)
        pl.when(last)(finish)

    params = pltpu.CompilerParams(dimension_semantics=("arbitrary",) * len(grid), vmem_limit_bytes=VMEM_LIMIT)
    call = pl.pallas_call(wrapped, name=name, grid=grid, in_specs=[*in_specs, *[ANY] * n_arr], out_specs=[*o_specs, *[ANY] * n_arr],
                          out_shape=[*o_shape, *_xchg_out_shapes(stacked, replicated)],
                          scratch_shapes=[*scratch_shapes, *_xchg_sems(n_arr)], compiler_params=params)

    def run(*args):
        res = call(*args, *arrs)
        outs, landed = res[:n_out], res[n_out:]
        return (outs[0] if single else outs), landed

    return run


def _dot(a, b, dn):
    return lax.dot_general(a, b, dn, preferred_element_type=F32)


def _tile(n, target):
    best = None
    for t in range(128, min(n, target) + 1, 128):
        if n % t == 0:
            best = t
    return n if best is None else best


def _matmul(a, b, mode, out_dtype, name, residual=None, tm=1024, tn=1024, comm=None, a2=None, epi=None):
    if mode == "nn":
        (M, K), N = a.shape, b.shape[1]
    elif mode == "nt":
        (M, K), N = a.shape, b.shape[0]
    else:
        (K, M), N = a.shape, b.shape[1]
    tm, tn = _tile(M, tm), _tile(N, tn)
    a_spec = pl.BlockSpec((K, tm), lambda i, j: (0, i)) if mode == "tn" else pl.BlockSpec((tm, K), lambda i, j: (i, 0))
    b_spec = pl.BlockSpec((tn, b.shape[1]), lambda i, j: (j, 0)) if mode == "nt" else pl.BlockSpec((K, tn), lambda i, j: (0, j))
    o_spec = pl.BlockSpec((tm, tn), lambda i, j: (i, j))
    in_specs, args = [a_spec, b_spec], [a, b]
    n1 = M // tm
    if a2 is not None and mode == "tn":
        assert M % tm == 0 and a2.shape[1] % tm == 0
        in_specs[0] = pl.BlockSpec((K, tm), lambda i, j: (0, jnp.minimum(i, n1 - 1)))
        in_specs.append(pl.BlockSpec((K, tm), lambda i, j: (0, jnp.maximum(i - n1, 0))))
        args.append(a2)
        M += a2.shape[1]
    elif a2 is not None:
        assert (mode == "nt" and K + a2.shape[1] == b.shape[1]) or (mode == "nn" and K + a2.shape[1] == b.shape[0])
        if mode == "nn":
            b_spec = in_specs[1] = pl.BlockSpec((b.shape[0], tn), lambda i, j: (0, j))
        in_specs.append(pl.BlockSpec((tm, a2.shape[1]), lambda i, j: (i, 0)))
        args.append(a2)
    if residual is not None:
        in_specs.append(o_spec)
        args.append(residual)
    n_mm = len(args)
    scratch = [pltpu.VMEM((tm, K), a.dtype)] if mode == "tn" else []
    if epi is None:
        out_specs, out_shape, is_acc = o_spec, jax.ShapeDtypeStruct((M, N), out_dtype), None
    else:
        assert tn == N
        fn, epi_ins, epi_outs = epi
        in_specs += [mk(tm) for _, mk in epi_ins]
        args += [arr for arr, _ in epi_ins]
        out_specs, out_shape, is_acc = _row_out_specs(epi_outs, M, tm)

    def body(*refs):
        a_ref, b_ref = refs[0], refs[1]
        n_out = 1 if epi is None else len(is_acc)
        out_refs = refs[len(args):len(args) + n_out]
        if mode == "tn":
            at_ref = refs[len(args) + n_out]

            first_col = pl.program_id(1) == 0
            from_a = first_col if a2 is None else first_col & (pl.program_id(0) < n1)

            @pl.when(from_a)
            def _():
                at_ref[...] = a_ref[...].T

            if a2 is not None:
                @pl.when(first_col & (pl.program_id(0) >= n1))
                def _():
                    at_ref[...] = refs[2][...].T

            acc = _dot(at_ref[...], b_ref[...], NN)
        elif a2 is not None and mode == "nt":
            acc = _dot(a_ref[...], b_ref[:, :K], NT) + _dot(refs[2][...], b_ref[:, K:], NT)
        elif a2 is not None:
            acc = _dot(a_ref[...], b_ref[:K, :], NN) + _dot(refs[2][...], b_ref[K:, :], NN)
        else:
            acc = _dot(a_ref[...], b_ref[...], NT if mode == "nt" else NN)
        if residual is not None:
            acc = acc + refs[n_mm - 1][...]
        if epi is None:
            out_refs[0][...] = acc.astype(out_dtype)
        else:
            _store_rows(out_refs, fn(acc, *[_load_f32(r) for r in refs[n_mm:len(args)]]), is_acc)

    return _pcall(body, name=name, grid=(M // tm, N // tn), in_specs=in_specs, out_specs=out_specs,
                  out_shape=out_shape, scratch_shapes=scratch,
                  dims=("arbitrary" if epi is not None else "parallel", "arbitrary"), comm=comm)(*args)


def _rows(arr, width=None, col=0):
    width = arr.shape[1] if width is None else width
    return (arr, lambda tm: pl.BlockSpec((tm, width), lambda i, *_: (i, col)))


def _heads(arr):
    return (arr, lambda tm: pl.BlockSpec((arr.shape[0], tm, arr.shape[2]), lambda i, *_: (0, i, 0)))


def _whole(arr):
    nd = arr.ndim
    return (arr, lambda tm: pl.BlockSpec(arr.shape, lambda i, *_: (0,) * nd))


def _row_out_specs(outs, n_rows, tm):
    out_specs, out_shape, is_acc = [], [], []
    for o in outs:
        if o[0] == "rows":
            out_specs.append(pl.BlockSpec((tm, o[1]), lambda i, *_: (i, 0)))
            out_shape.append(jax.ShapeDtypeStruct((n_rows, o[1]), o[2]))
        elif o[0] == "cols":
            out_specs.append(pl.BlockSpec((tm, o[1]), lambda i, *_, c=o[2]: (i, c)))
            out_shape.append(jax.ShapeDtypeStruct((n_rows, o[3]), o[4]))
        elif o[0] == "heads":
            out_specs.append(pl.BlockSpec((o[1], tm, o[2]), lambda i, *_: (0, i, 0)))
            out_shape.append(jax.ShapeDtypeStruct((o[1], n_rows, o[2]), o[3]))
        else:
            out_specs.append(pl.BlockSpec((o[1], o[2]), lambda i, *_: (0, 0)))
            out_shape.append(jax.ShapeDtypeStruct((o[1], o[2]), F32))
        is_acc.append(o[0] == "acc")
    return out_specs, out_shape, is_acc


def _load_f32(r):
    v = r[...]
    return v.astype(F32) if v.dtype == BF16 else v


def _store_rows(out_refs, vals, is_acc):
    for r, v, acc in zip(out_refs, vals, is_acc):
        if acc:
            @pl.when(pl.program_id(0) == 0)
            def _():
                r[...] = jnp.zeros_like(r)

            r[...] += v
        else:
            r[...] = v.astype(r.dtype)


def _rowwise(fn, name, n_rows, tm, ins, outs, upcast=True, into=None, gather=None):
    tm = min(tm, n_rows)
    assert n_rows % tm == 0
    in_specs = [mk(tm) for _, mk in ins]
    out_specs, out_shape, is_acc = _row_out_specs(outs, n_rows, tm)
    n_in = len(ins)
    args = [a for a, _ in ins]
    aliases = {}
    if into is not None:
        in_specs.append(ANY)
        args.append(into[0])
        aliases = {n_in: into[1]}

    def body(*refs):
        vals = fn(*[_load_f32(r) if upcast else r[...] for r in refs[:n_in]])
        _store_rows(refs[len(args):], vals, is_acc)

    return _pcall(body, name=name, grid=(n_rows // tm,), in_specs=in_specs, out_specs=out_specs,
                  out_shape=out_shape, dims=("arbitrary",), aliases=aliases,
                  comm=None if gather is None else ([], gather), two_level=True)(*args)


def _rms(x, g):
    r = lax.rsqrt(jnp.mean(x * x, axis=-1, keepdims=True) + EPS)
    return x * r * g


def _rms_bwd(dy, x, g):
    r = lax.rsqrt(jnp.mean(x * x, axis=-1, keepdims=True) + EPS)
    xhat = x * r
    dxhat = dy * g
    dx = r * (dxhat - xhat * jnp.mean(dxhat * xhat, axis=-1, keepdims=True))
    return dx, jnp.sum(dy * xhat, axis=0, keepdims=True)


def _rope_bwd(d1, d2, cos, sin):
    return d1 * cos + d2 * sin, d2 * cos - d1 * sin


def _rope128(x, cos_p, sin_p):
    lane = lax.broadcasted_iota(jnp.int32, x.shape, 1)
    swapped = jnp.where(lane < QK_ROPE // 2, pltpu.roll(x, 128 - QK_ROPE // 2, axis=1), pltpu.roll(x, QK_ROPE // 2, axis=1))
    return x * cos_p + swapped * sin_p


def _sigmoid(x):
    return 1.0 / (1.0 + jnp.exp(-x))


MLA_SCALE = 1.0 / math.sqrt(QK_HEAD)
MLA_PRESCALE = MLA_SCALE * math.log2(math.e)
MLA_TQ, MLA_KC = 1024, 1024


def _mla_fwd(q_full, k_full, kv, S, comm=None):
    tq, kc = min(2 * MLA_TQ, S), min(MLA_KC, S)

    def body(q_ref, k_ref, v_ref, o_ref, lse_ref):
        q = q_ref[0]
        m = jnp.full((tq, 1), -1e30, F32)
        l = jnp.zeros((tq, 1), F32)
        acc = jnp.zeros((tq, V_DIM), F32)
        for c in range(S // kc):
            s = _dot(q, k_ref[0, c * kc:(c + 1) * kc, :], NT)
            m_new = jnp.maximum(m, jnp.max(s, axis=-1, keepdims=True))
            alpha = jnp.exp2(m - m_new)
            p = jnp.exp2(s - m_new)
            l = alpha * l + jnp.sum(p, axis=-1, keepdims=True)
            acc = alpha * acc + _dot(p.astype(BF16), v_ref[c * kc:(c + 1) * kc, :], NN)
            m = m_new
        o_ref[...] = (acc / l).astype(BF16)
        lse_ref[0] = m + jnp.log2(l)

    return _pcall(
        body, name="mla_fwd", grid=(H_A, S // tq),
        in_specs=[pl.BlockSpec((1, tq, HEAD_PAD), lambda h, i: (h, i, 0)),
                  pl.BlockSpec((1, S, HEAD_PAD), lambda h, i: (h, 0, 0)),
                  pl.BlockSpec((S, V_DIM), lambda h, i: (0, 2 * h + 1))],
        out_specs=[pl.BlockSpec((tq, V_DIM), lambda h, i: (i, h)),
                   pl.BlockSpec((1, tq, 1), lambda h, i: (h, i, 0))],
        out_shape=[jax.ShapeDtypeStruct((S, H_A * V_DIM), BF16), jax.ShapeDtypeStruct((H_A, S, 1), F32)],
        dims=("parallel", "parallel"), comm=comm)(q_full, k_full, kv)


def _mla_bwd(q_full, k_full, kv, do_a, o_a, lse, cos_p, sin_back, S, comm=None):
    tq, kc = min(MLA_TQ, S), min(MLA_KC, S)

    def body(q_ref, k_ref, v_ref, do_ref, o_ref, lse_ref, cos_ref, sin_ref, dqn_ref, dqr_ref, dkv_out, dkr_out, dk_ref, dv_ref):
        @pl.when(pl.program_id(1) == 0)
        def _():
            dk_ref[...] = jnp.zeros_like(dk_ref)
            dv_ref[...] = jnp.zeros_like(dv_ref)

        q = q_ref[0]
        do = do_ref[...]
        lse_q = lse_ref[0]
        delta = jnp.sum(do.astype(F32) * o_ref[...].astype(F32), axis=-1, keepdims=True)
        dq = jnp.zeros((tq, HEAD_PAD), F32)
        for c in range(S // kc):
            k = k_ref[0, c * kc:(c + 1) * kc, :]
            v = v_ref[c * kc:(c + 1) * kc, :]
            p = jnp.exp2(_dot(q, k, NT) - lse_q)
            ds = (p * (_dot(do, v, NT) - delta)).astype(BF16)
            dq = dq + _dot(ds, k, NN)
            dk_ref[0, c * kc:(c + 1) * kc, :] += _dot(ds, q, TN)
            dv_ref[0, c * kc:(c + 1) * kc, :] += _dot(p.astype(BF16), do, TN)
        dq = dq * MLA_SCALE
        dqn_ref[...] = dq[:, :QK_NOPE].astype(BF16)
        dqr_ref[...] = _rope128(dq[:, QK_NOPE:], cos_ref[...], sin_ref[...]).astype(BF16)

        @pl.when(pl.program_id(1) == S // tq - 1)
        def _():
            dk = dk_ref[0] * math.log(2.0)
            dkv_out[...] = jnp.concatenate([dk[:, :QK_NOPE], dv_ref[0]], axis=1).astype(BF16)
            dkr_out[0] = dk[:, QK_NOPE:QK_HEAD].astype(BF16)

    return _pcall(
        body, name="mla_bwd", grid=(H_A, S // tq),
        in_specs=[pl.BlockSpec((1, tq, HEAD_PAD), lambda h, i: (h, i, 0)),
                  pl.BlockSpec((1, S, HEAD_PAD), lambda h, i: (h, 0, 0)),
                  pl.BlockSpec((S, V_DIM), lambda h, i: (0, 2 * h + 1)),
                  pl.BlockSpec((tq, V_DIM), lambda h, i: (i, h)),
                  pl.BlockSpec((tq, V_DIM), lambda h, i: (i, h)),
                  pl.BlockSpec((1, tq, 1), lambda h, i: (h, i, 0)),
                  pl.BlockSpec((tq, 128), lambda h, i: (i, 0)), pl.BlockSpec((tq, 128), lambda h, i: (i, 0))],
        out_specs=[pl.BlockSpec((tq, QK_NOPE), lambda h, i: (i, h)),
                   pl.BlockSpec((tq, 128), lambda h, i: (i, h)),
                   pl.BlockSpec((S, QK_NOPE + V_DIM), lambda h, i: (0, h)),
                   pl.BlockSpec((1, S, QK_ROPE), lambda h, i: (h, 0, 0))],
        out_shape=[jax.ShapeDtypeStruct((S, H_A * QK_NOPE), BF16), jax.ShapeDtypeStruct((S, H_A * 128), BF16),
                   jax.ShapeDtypeStruct((S, H_A * (QK_NOPE + V_DIM)), BF16), jax.ShapeDtypeStruct((H_A, S, QK_ROPE), BF16)],
        scratch_shapes=[pltpu.VMEM((1, S, HEAD_PAD), F32), pltpu.VMEM((1, S, V_DIM), F32)],
        dims=("parallel", "arbitrary"), comm=comm)(q_full, k_full, kv, do_a, o_a, lse, cos_p, sin_back)


WIN_SCALE = 1.0 / math.sqrt(HD_B)


WIN_PER_STEP = 4


def _win_specs(S):
    last, B = S // Q_BLOCK - 1, WIN_PER_STEP
    qspec = pl.BlockSpec((B * Q_BLOCK, H_B * HD_B), lambda i: (i, PROJ_QB))
    kspecs = [[pl.BlockSpec((Q_BLOCK, KV_B * HD_B), lambda i, d=d, c=c: (jnp.clip(B * i + d, 0, last), c)) for d in range(-1, B + 1)]
              for c in (PROJ_KB, PROJ_VB)]
    bias_spec = pl.BlockSpec((H_B, SPAN, Q_BLOCK), lambda i: (0, 0, 0))
    sink_spec = pl.BlockSpec((H_B, Q_BLOCK), lambda i: (0, 0))
    return qspec, kspecs, bias_spec, sink_spec


def _win_edge_ok(n, n_blk):
    row = lax.broadcasted_iota(jnp.int32, (SPAN, 1), 0)
    return jnp.logical_not(((n == 0) & (row < WINDOW)) | ((n == n_blk - 1) & (row >= SPAN - WINDOW)))


def _lanes4(pieces):
    return jnp.concatenate(pieces, axis=1)


def _win_probs(kg, q4t, bias_ref, sink_ref, g, edge_ok):
    bias4 = _lanes4([bias_ref[GROUP * g + j] for j in range(GROUP)])
    sink4 = _lanes4([sink_ref[GROUP * g + j:GROUP * g + j + 1, :] for j in range(GROUP)])
    s = jnp.where(edge_ok, _dot(kg, q4t, NN) + bias4, -1e30)
    m = jnp.maximum(jnp.max(s, axis=0, keepdims=True), sink4)
    p = jnp.exp(s - m)
    e_sink = jnp.exp(sink4 - m)
    inv_l = 1.0 / (jnp.sum(p, axis=0, keepdims=True) + e_sink)
    return p * inv_l, e_sink * inv_l


def _group_t(xt, g):
    return _lanes4([xt[HD_B * (GROUP * g + j):HD_B * (GROUP * g + j + 1), :] for j in range(GROUP)])


def _rows_of(ref, b):
    return ref[Q_BLOCK * b:Q_BLOCK * (b + 1), :]


def _win_fwd(proj, o_a, bias_t, sinks_b, S, comm=None):
    n_blk, B = S // Q_BLOCK, WIN_PER_STEP
    qspec, kspecs, bias_spec, sink_spec = _win_specs(S)
    rows = lambda col: pl.BlockSpec((B * Q_BLOCK, H_B * HD_B), lambda i: (i, col))

    def body(q_ref, *refs):
        k_refs, v_refs = refs[:B + 2], refs[B + 2:2 * B + 4]
        bias_ref, sink_ref, ga_ref, gb_ref, oa_ref, o_ref, mixed_ref = refs[2 * B + 4:]
        for b in range(B):
            edge_ok = _win_edge_ok(B * pl.program_id(0) + b, n_blk)
            k = jnp.concatenate([r[...] for r in k_refs[b:b + 3]], axis=0)
            vt = jnp.concatenate([r[...] for r in v_refs[b:b + 3]], axis=0).T
            qt = (_rows_of(q_ref, b).astype(F32) * WIN_SCALE).T.astype(BF16)
            parts = []
            for g in range(KV_B):
                p, _ = _win_probs(k[:, HD_B * g:HD_B * (g + 1)], _group_t(qt, g), bias_ref, sink_ref, g, edge_ok)
                o4t = _dot(vt[HD_B * g:HD_B * (g + 1), :], p.astype(BF16), NN)
                parts += [o4t[:, Q_BLOCK * j:Q_BLOCK * (j + 1)] for j in range(GROUP)]
            ob = jnp.concatenate(parts, axis=0).T
            o_ref[Q_BLOCK * b:Q_BLOCK * (b + 1), :] = ob.astype(BF16)
            ga, gb, oa = (_rows_of(r, b).astype(F32) for r in (ga_ref, gb_ref, oa_ref))
            mixed_ref[Q_BLOCK * b:Q_BLOCK * (b + 1), :] = (_sigmoid(ga) * oa + _sigmoid(gb) * ob).astype(BF16)

    return _pcall(body, name="win_fwd_mix", grid=(n_blk // B,),
                  in_specs=[qspec, *kspecs[0], *kspecs[1], bias_spec, sink_spec, rows(PROJ_GA), rows(PROJ_GB), rows(0)],
                  out_specs=[rows(0), rows(0)],
                  out_shape=[jax.ShapeDtypeStruct((S, H_B * HD_B), BF16)] * 2,
                  dims=("parallel",), comm=comm)(*[proj] * (2 * B + 5), bias_t, sinks_b, proj, proj, o_a)


def _win_bwd(proj, bias_t, sinks_b, do_b, d_proj, S):
    n_blk, B = S // Q_BLOCK, WIN_PER_STEP
    qspec, kspecs, bias_spec, sink_spec = _win_specs(S)

    def body(q_ref, *refs):
        k_refs, v_refs = refs[:B + 2], refs[B + 2:2 * B + 4]
        bias_ref, sink_ref, do_ref, _, dq_ref, dk_ref, dv_ref, dbias_ref, dsink_ref, dsink_acc = refs[2 * B + 4:]
        i = pl.program_id(0)

        @pl.when(i == 0)
        def _():
            dk_ref[...] = jnp.zeros_like(dk_ref)
            dv_ref[...] = jnp.zeros_like(dv_ref)
            dbias_ref[...] = jnp.zeros_like(dbias_ref)
            dsink_acc[...] = jnp.zeros_like(dsink_acc)

        d_bias, d_sink, dk_blocks, dv_blocks = {}, {}, [], []
        for b in range(B):
            edge_ok = _win_edge_ok(B * i + b, n_blk)
            k = jnp.concatenate([r[...] for r in k_refs[b:b + 3]], axis=0)
            v = jnp.concatenate([r[...] for r in v_refs[b:b + 3]], axis=0)
            kt = k.T
            qt = (_rows_of(q_ref, b).astype(F32) * WIN_SCALE).T.astype(BF16)
            dot_ = _rows_of(do_ref, b).astype(F32).T.astype(BF16)
            dq_parts, dks, dvs = [], [], []
            for g in range(KV_B):
                kg, vg = k[:, HD_B * g:HD_B * (g + 1)], v[:, HD_B * g:HD_B * (g + 1)]
                q4t, do4t = _group_t(qt, g), _group_t(dot_, g)
                p, p_sink = _win_probs(kg, q4t, bias_ref, sink_ref, g, edge_ok)
                dp = _dot(vg, do4t, NN)
                delta = jnp.sum(p * dp, axis=0, keepdims=True)
                ds = p * (dp - delta)
                d_bias[g] = ds if b == 0 else d_bias[g] + ds
                d_sink[g] = -p_sink * delta if b == 0 else d_sink[g] - p_sink * delta
                dsb = ds.astype(BF16)
                dq4t = _dot(kt[HD_B * g:HD_B * (g + 1), :], dsb, NN) * WIN_SCALE
                dq_parts += [dq4t[:, Q_BLOCK * j:Q_BLOCK * (j + 1)] for j in range(GROUP)]
                dks.append(_dot(dsb, q4t, NT))
                dvs.append(_dot(p.astype(BF16), do4t, NT))
            dq_ref[Q_BLOCK * b:Q_BLOCK * (b + 1), :] = jnp.concatenate(dq_parts, axis=0).T.astype(BF16)
            dk_blocks.append(jnp.concatenate(dks, axis=1))
            dv_blocks.append(jnp.concatenate(dvs, axis=1))

        for g in range(KV_B):
            for j in range(GROUP):
                dbias_ref[GROUP * g + j] += d_bias[g][:, Q_BLOCK * j:Q_BLOCK * (j + 1)]
            dsink_acc[g:g + 1, :] += d_sink[g]

        def overlap(blocks):
            out = blocks[0]
            for blk in blocks[1:]:
                keep = out.shape[0] - 2 * Q_BLOCK
                out = jnp.concatenate([out[:keep], out[keep:] + blk[:2 * Q_BLOCK], blk[2 * Q_BLOCK:]], axis=0)
            return out

        rows = pl.ds(pl.multiple_of(i * (B * Q_BLOCK), B * Q_BLOCK), (B + 2) * Q_BLOCK)
        dk_ref[rows, :] += overlap(dk_blocks)
        dv_ref[rows, :] += overlap(dv_blocks)

        @pl.when(i == n_blk // B - 1)
        def _():
            acc = dsink_acc[...]
            dsink_ref[...] = jnp.concatenate(
                [jnp.sum(acc[:, Q_BLOCK * j:Q_BLOCK * (j + 1)], axis=1, keepdims=True) for j in range(GROUP)], axis=1)

    whole = lambda shape: pl.BlockSpec(shape, lambda i: (0,) * len(shape))
    return _pcall(
        body, name="win_bwd", grid=(n_blk // B,),
        in_specs=[qspec, *kspecs[0], *kspecs[1], bias_spec, sink_spec, pl.BlockSpec((B * Q_BLOCK, H_B * HD_B), lambda i: (i, 0)), ANY],
        out_specs=[qspec, whole((S + 2 * WINDOW, KV_B * HD_B)),
                   whole((S + 2 * WINDOW, KV_B * HD_B)), whole((H_B, SPAN, Q_BLOCK)), whole((KV_B, GROUP))],
        out_shape=[jax.ShapeDtypeStruct((S, PROJ_P), BF16), jax.ShapeDtypeStruct((S + 2 * WINDOW, KV_B * HD_B), F32),
                   jax.ShapeDtypeStruct((S + 2 * WINDOW, KV_B * HD_B), F32), jax.ShapeDtypeStruct((H_B, SPAN, Q_BLOCK), F32),
                   jax.ShapeDtypeStruct((KV_B, GROUP), F32)],
        scratch_shapes=[pltpu.VMEM((KV_B, GROUP * Q_BLOCK), F32)],
        dims=("arbitrary",), aliases={2 * B + 8: 0})(*[proj] * (2 * B + 5), bias_t, sinks_b, do_b, d_proj)


def _bias_table(rel_bias_t, onehot_t, in_band):
    def body(rb_ref, oh_ref, band_ref, o_ref):
        t = lax.dot_general(rb_ref[...], oh_ref[...], NN, preferred_element_type=F32, precision=lax.Precision.HIGHEST)
        o_ref[...] = jnp.where(band_ref[...] > 0.5, t, -1e30)

    n = onehot_t.shape[1]
    tn = _tile(n, 8192)
    return _pcall(body, name="bias_table", grid=(n // tn,),
                  in_specs=[pl.BlockSpec((H_B, NUM_BUCKETS), lambda j: (0, 0)), pl.BlockSpec((NUM_BUCKETS, tn), lambda j: (0, j)),
                            pl.BlockSpec((1, tn), lambda j: (0, j))],
                  out_specs=pl.BlockSpec((H_B, tn), lambda j: (0, j)),
                  out_shape=jax.ShapeDtypeStruct((H_B, n), F32), dims=("parallel",))(rel_bias_t, onehot_t, in_band)


def _bias_table_bwd(dbias, onehot_t):
    n = onehot_t.shape[1]
    tk = _tile(n, 8192)

    def body(d_ref, oh_ref, o_ref):
        @pl.when(pl.program_id(0) == 0)
        def _():
            o_ref[...] = jnp.zeros_like(o_ref)

        o_ref[...] += lax.dot_general(d_ref[...], oh_ref[...], NT, preferred_element_type=F32, precision=lax.Precision.HIGHEST)

    return _pcall(body, name="bias_table_bwd", grid=(n // tk,),
                  in_specs=[pl.BlockSpec((H_B, tk), lambda j: (0, j)), pl.BlockSpec((NUM_BUCKETS, tk), lambda j: (0, j))],
                  out_specs=pl.BlockSpec((H_B, NUM_BUCKETS), lambda j: (0, 0)),
                  out_shape=jax.ShapeDtypeStruct((H_B, NUM_BUCKETS), F32), dims=("arbitrary",))(dbias, onehot_t)


CONV_STRIP = 128
N_STRIPS = D_FF // CONV_STRIP
CONV_ROWS = 128
HALO = 8


def _strip(rows, half):
    return pl.BlockSpec((rows, CONV_STRIP), lambda j: (0, j + half * N_STRIPS))


def _fill_padded(pad_ref, src_ref, halo, S):
    pad_ref[0:halo, :] = jnp.zeros((halo, CONV_STRIP), F32)
    pad_ref[halo + S:2 * halo + S, :] = jnp.zeros((halo, CONV_STRIP), F32)
    pad_ref[halo:halo + S, :] = src_ref[...].astype(F32)


def _conv_gate_fwd(u, conv_w, conv_b, S):
    R = min(CONV_ROWS, S)

    def body(ug_ref, uv_ref, wg_ref, wv_ref, bg_ref, bv_ref, a_ref, gpad, vpad):
        _fill_padded(gpad, ug_ref, HALO, S)
        _fill_padded(vpad, uv_ref, HALO, S)
        wg, wv, bg, bv = wg_ref[...], wv_ref[...], bg_ref[...], bv_ref[...]

        def conv(pad_ref, r0, w, b):
            dn, mid, up = (pad_ref[pl.ds(r0 + HALO + d, R), :] for d in (-1, 0, 1))
            return dn * w[0:1, :] + mid * w[1:2, :] + up * w[2:3, :] + b

        def step(c, carry):
            r0 = pl.multiple_of(c * R, R)
            g = conv(gpad, r0, wg, bg)
            val = conv(vpad, r0, wv, bv)
            a_ref[pl.ds(r0, R), :] = (g * _sigmoid(g) * val).astype(BF16)
            return carry

        lax.fori_loop(0, S // R, step, 0)

    return _pcall(body, name="conv_gate_fwd", grid=(N_STRIPS,),
                  in_specs=[_strip(S, 0), _strip(S, 1), _strip(3, 0), _strip(3, 1), _strip(1, 0), _strip(1, 1)],
                  out_specs=_strip(S, 0), out_shape=jax.ShapeDtypeStruct((S, D_FF), BF16),
                  scratch_shapes=[pltpu.VMEM((S + 2 * HALO, CONV_STRIP), F32)] * 2,
                  dims=("parallel",))(u, u, conv_w, conv_w, conv_b, conv_b)


def _conv_gate_bwd(u, conv_w, conv_b, da, S):
    R = min(CONV_ROWS, S)
    n = R + 2 * HALO

    def body(ug_ref, uv_ref, wg_ref, wv_ref, bg_ref, bv_ref, da_ref, dug_ref, duv_ref, dwg_ref, dwv_ref, dbg_ref, dbv_ref,
             gpad, vpad, dapad):
        _fill_padded(gpad, ug_ref, 2 * HALO, S)
        _fill_padded(vpad, uv_ref, 2 * HALO, S)
        _fill_padded(dapad, da_ref, HALO, S)
        wg, wv, bg, bv = wg_ref[...], wv_ref[...], bg_ref[...], bv_ref[...]

        def conv(pad_ref, r0, w, b):
            dn, mid, up = (pad_ref[pl.ds(r0 + HALO + d, n), :] for d in (-1, 0, 1))
            return dn * w[0:1, :] + mid * w[1:2, :] + up * w[2:3, :] + b, mid[HALO:HALO + R]

        def conv_bwd(duc, u_mid, w, r0, du_ref):
            dn, mid, up = pltpu.roll(duc, 1, axis=0)[HALO:HALO + R], duc[HALO:HALO + R], pltpu.roll(duc, n - 1, axis=0)[HALO:HALO + R]
            du_ref[pl.ds(r0, R), :] = (up * w[0:1, :] + mid * w[1:2, :] + dn * w[2:3, :]).astype(BF16)
            dw = jnp.concatenate([jnp.sum(up * u_mid, axis=0, keepdims=True), jnp.sum(mid * u_mid, axis=0, keepdims=True),
                                  jnp.sum(dn * u_mid, axis=0, keepdims=True)], axis=0)
            return dw, jnp.sum(mid, axis=0, keepdims=True)

        def step(c, carry):
            dw_g, db_g, dw_v, db_v = carry
            r0 = pl.multiple_of(c * R, R)
            g, ug_mid = conv(gpad, r0, wg, bg)
            val, uv_mid = conv(vpad, r0, wv, bv)
            da_ext = dapad[pl.ds(r0, n), :]
            sg = _sigmoid(g)
            ddw_v, ddb_v = conv_bwd(da_ext * (g * sg), uv_mid, wv, r0, duv_ref)
            ddw_g, ddb_g = conv_bwd(da_ext * val * (sg * (1.0 + g * (1.0 - sg))), ug_mid, wg, r0, dug_ref)
            return dw_g + ddw_g, db_g + ddb_g, dw_v + ddw_v, db_v + ddb_v

        z3, z1 = jnp.zeros((3, CONV_STRIP), F32), jnp.zeros((1, CONV_STRIP), F32)
        dwg_ref[...], dbg_ref[...], dwv_ref[...], dbv_ref[...] = lax.fori_loop(0, S // R, step, (z3, z1, z3, z1))

    half = lambda r, dt: (_strip(r, 0), jax.ShapeDtypeStruct((r, D_FF), dt))
    outs = [half(S, BF16), half(S, BF16), half(3, F32), half(3, F32), half(1, F32), half(1, F32)]
    return _pcall(
        body, name="conv_gate_bwd", grid=(N_STRIPS,),
        in_specs=[_strip(S, 0), _strip(S, 1), _strip(3, 0), _strip(3, 1), _strip(1, 0), _strip(1, 1), _strip(S, 0)],
        out_specs=[o[0] for o in outs], out_shape=[o[1] for o in outs],
        scratch_shapes=[pltpu.VMEM((S + 4 * HALO, CONV_STRIP), F32)] * 2 + [pltpu.VMEM((S + 2 * HALO, CONV_STRIP), F32)],
        dims=("parallel",))(u, u, conv_w, conv_w, conv_b, conv_b, da)


MESH = pl.DeviceIdType.MESH
ANY = pl.BlockSpec(memory_space=pl.ANY)


def _place():
    return lax.axis_index("x"), lax.axis_index("y"), lax.axis_index("c")


def _gather2(ins, outs, sems, finish):
    send_sems, recv_sems, local_sems = sems
    n_arr = len(ins)
    x, y, c = _place()
    me, sibling = (x, y, c), (x, y, 1 - c)
    chips = [(1 - x, y), (x, 1 - y), (1 - x, 1 - y)]

    def slot(a, p):
        return outs[a].at[4 * p[0] + 2 * p[1] + p[2]]

    def copy(a, k, block, to, src=None):
        return pltpu.make_async_remote_copy(
            src_ref=slot(a, block) if src is None else src, dst_ref=slot(a, block),
            send_sem=send_sems.at[a, k], recv_sem=recv_sems.at[a, k], device_id=to, device_id_type=MESH)

    mine = [pltpu.make_async_copy(ins[a], slot(a, me), local_sems.at[a]) for a in range(n_arr)]
    first = []
    for a in range(n_arr):
        first.append(copy(a, 0, me, sibling, src=ins[a]))
        first += [copy(a, 1 + j, me, (*chip, c), src=ins[a]) for j, chip in enumerate(chips)]
    if not finish:
        for cp in mine + first:
            cp.start()
        return
    passed = []
    for j, chip in enumerate(chips):
        for a in range(n_arr):
            copy(a, 1 + j, (*chip, c), me).wait_recv()
            cp = copy(a, 4 + j, (*chip, c), sibling)
            cp.start()
            passed.append(cp)
    for a in range(n_arr):
        copy(a, 0, sibling, me).wait_recv()
        for j, chip in enumerate(chips):
            copy(a, 4 + j, (*chip, 1 - c), me).wait_recv()
    for cp in first + passed:
        cp.wait_send()
    for cp in mine:
        cp.wait()


def _xchg_out_shapes(stacked, replicated):
    return ([jax.ShapeDtypeStruct(s.shape, s.dtype) for s in stacked]
            + [jax.ShapeDtypeStruct((N_DEV, *r.shape), r.dtype) for r in replicated])


def _xchg_sems(n_arr):
    return [pltpu.SemaphoreType.DMA((n_arr, 7)), pltpu.SemaphoreType.DMA((n_arr, 7)), pltpu.SemaphoreType.DMA((n_arr,))]


def _xchg_copies(ins, outs, sems, n_st, with_recv):
    send_sems, recv_sems, local_sems = sems
    n_arr = len(ins)
    x, y, c = _place()
    me = 4 * x + 2 * y + c

    def src(a, idx):
        return ins[a].at[idx] if a < n_st else ins[a]

    mine = [pltpu.make_async_copy(src(a, me), outs[a].at[me], local_sems.at[a]) for a in range(n_arr)]
    pairs = []
    for k in range(1, N_DEV):
        px, py, pc = x ^ (k >> 2), y ^ ((k >> 1) & 1), c ^ (k & 1)
        peer = 4 * px + 2 * py + pc
        for a in range(n_arr):
            sems_k = dict(send_sem=send_sems.at[a, k - 1], recv_sem=recv_sems.at[a, k - 1], device_id_type=MESH)
            send = pltpu.make_async_remote_copy(src_ref=src(a, peer), dst_ref=outs[a].at[me], device_id=(px, py, pc), **sems_k)
            recv = None
            if with_recv:
                recv = pltpu.make_async_remote_copy(src_ref=src(a, peer), dst_ref=outs[a].at[peer], device_id=(x, y, c), **sems_k)
            pairs.append((send, recv))
    return mine, pairs


def _xchg_start(ins, outs, sems, n_st):
    mine, pairs = _xchg_copies(ins, outs, sems, n_st, False)
    for cp in mine:
        cp.start()
    for send, _ in pairs:
        send.start()


def _xchg_finish(ins, outs, sems, n_st):
    mine, pairs = _xchg_copies(ins, outs, sems, n_st, True)
    for _, recv in pairs:
        recv.wait_recv()
    for send, _ in pairs:
        send.wait_send()
    for cp in mine:
        cp.wait()


def _exchange(stacked, replicated, name):
    _, landed = _pcall(lambda: None, name=name, grid=(), in_specs=[], out_specs=[], out_shape=[], comm=(stacked, replicated))()
    return landed


HBM = pl.BlockSpec(memory_space=pltpu.HBM)
SEMS = pl.BlockSpec(memory_space=pltpu.SEMAPHORE)
SIDE_EFFECT = pltpu.SideEffectType.DATAFLOW_SIDE_EFFECTING


N_SPLIT_SEMS = 2 * (N_DEV - 1)


def _split_copies(src, land, sems, with_recv):
    x, y, c = _place()
    me = 4 * x + 2 * y + c
    pairs = []
    for k in range(1, N_DEV):
        px, py, pc = x ^ (k >> 2), y ^ ((k >> 1) & 1), c ^ (k & 1)
        peer = 4 * px + 2 * py + pc
        sems_k = dict(send_sem=sems[k - 1], recv_sem=sems[N_DEV - 1 + k - 1], device_id_type=MESH)
        send = pltpu.make_async_remote_copy(src_ref=src.at[peer], dst_ref=land.at[me], device_id=(px, py, pc), **sems_k)
        recv = None
        if with_recv:
            recv = pltpu.make_async_remote_copy(src_ref=src.at[peer], dst_ref=land.at[peer], device_id=(x, y, c), **sems_k)
        pairs.append((send, recv))
    return pairs


def _exchange_start(stacked, name):
    def body(src, land, *rest):
        for send, _ in _split_copies(src, land, rest[:N_SPLIT_SEMS], False):
            send.start()
        rest[-1][...] = jnp.zeros_like(rest[-1])

    shape = pltpu.HBM(stacked.shape, stacked.dtype)
    res = pl.pallas_call(
        body, name=name, in_specs=[HBM, HBM],
        out_shape=(*[pltpu.SemaphoreType.DMA(())] * N_SPLIT_SEMS, shape, shape, jax.ShapeDtypeStruct((8, 128), F32)),
        out_specs=(*[SEMS] * N_SPLIT_SEMS, HBM, HBM, pl.BlockSpec(memory_space=pltpu.VMEM)),
        input_output_aliases={0: N_SPLIT_SEMS, 1: N_SPLIT_SEMS + 1},
        compiler_params=pltpu.CompilerParams(has_side_effects=SIDE_EFFECT),
    )(pltpu.with_memory_space_constraint(stacked, pltpu.HBM),
      pltpu.with_memory_space_constraint(lax.empty(stacked.shape, stacked.dtype), pltpu.HBM))
    return res[:N_SPLIT_SEMS], res[N_SPLIT_SEMS], res[N_SPLIT_SEMS + 1], res[-1]


def _exchange_wait(sems, src, land, after, name):
    def body(src_ref, land_ref, *rest):
        for send, recv in _split_copies(src_ref, land_ref, rest[:N_SPLIT_SEMS], True):
            send.wait_send()
            recv.wait_recv()

    shape = pltpu.HBM(src.shape, src.dtype)
    return pl.pallas_call(
        body, name=name, in_specs=[HBM, HBM, *[SEMS] * N_SPLIT_SEMS, ANY],
        out_shape=(shape, shape), out_specs=(HBM, HBM), input_output_aliases={0: 0, 1: 1},
        compiler_params=pltpu.CompilerParams(has_side_effects=SIDE_EFFECT))(src, land, *sems, after)[1]


def _adamw(parts, w, m, v, name):
    _, R, C = w.shape
    tr = R if R <= 512 else max(t for t in range(16, 513, 16) if R % t == 0)
    pr = tr if parts.shape[1] == R else -(-R // 16) * 16
    assert pr == tr or tr == R

    def body(p_ref, w_ref, m_ref, v_ref, g_ref, d_ref, nm_ref, nv_ref):
        g = p_ref[0].astype(F32)[:tr]
        for s in range(1, N_DEV):
            g = g + p_ref[s].astype(F32)[:tr]
        m2 = ADAM_B1 * m_ref[0] + (1.0 - ADAM_B1) * g
        v2 = ADAM_B2 * v_ref[0] + (1.0 - ADAM_B2) * (g * g)
        m_hat = m2 / (1.0 - ADAM_B1 ** ADAM_STEP)
        v_hat = v2 / (1.0 - ADAM_B2 ** ADAM_STEP)
        g_ref[0] = g
        d_ref[0] = -ADAM_LR * (m_hat / (jnp.sqrt(v_hat) + ADAM_EPS) + ADAM_WD * w_ref[0])
        nm_ref[0] = m2
        nv_ref[0] = v2

    blk = pl.BlockSpec((1, tr, C), lambda i: (0, i, 0))
    return _pcall(body, name=name, grid=(R // tr,),
                  in_specs=[pl.BlockSpec((N_DEV, pr, C), lambda i: (0, i, 0)), blk, blk, blk],
                  out_specs=[blk] * 4, out_shape=[jax.ShapeDtypeStruct((1, R, C), F32)] * 4,
                  dims=("parallel",))(parts, w, m, v)


def _t5_bucket(rel):
    nb = NUM_BUCKETS // 2
    max_exact = nb // 2
    base = (rel > 0).astype(jnp.int32) * nb
    n = jnp.abs(rel)
    nf = jnp.maximum(n, 1).astype(jnp.float32)
    large = max_exact + (jnp.log(nf / max_exact) / math.log(MAX_DISTANCE / max_exact) * (nb - max_exact)).astype(jnp.int32)
    large = jnp.minimum(large, nb - 1)
    return base + jnp.where(n < max_exact, n, large)


def _unstack_cols(g):
    return jnp.transpose(g, (1, 0, 2)).reshape(g.shape[1], N_DEV * g.shape[2])


def _stack_cols(w, n=N_DEV):
    R = w.shape[0]
    return jnp.transpose(w.reshape(R, n, w.shape[1] // n), (1, 0, 2))


def _stack_halves(g, v):
    return jnp.concatenate([_stack_cols(g, N_DEV // 2), _stack_cols(v, N_DEV // 2)], axis=0)


def kernel(x, positions, norm1_g, w_in, q_a_norm_g, w_q_b, kv_a_norm_g, w_kv_b, rel_bias, sinks, w_out, norm2_g, w_up, conv_w, conv_b, w_down, final_norm_g, loss_target, m_norm1_g, m_w_in, m_q_a_norm_g, m_w_q_b, m_kv_a_norm_g, m_w_kv_b, m_rel_bias, m_sinks, m_w_out, m_norm2_g, m_w_up, m_conv_w, m_conv_b, m_w_down, m_final_norm_g, v_norm1_g, v_w_in, v_q_a_norm_g, v_w_q_b, v_kv_a_norm_g, v_w_kv_b, v_rel_bias, v_sinks, v_w_out, v_norm2_g, v_w_up, v_conv_w, v_conv_b, v_w_down, v_final_norm_g):
    S = x.shape[1]
    x = x[0]
    target = loss_target[0]
    TM = 256

    tr = lambda w: jnp.swapaxes(w, 1, 2)
    (h1,), (g_in,) = _rowwise(lambda a, g: (_rms(a, g),), "norm1_gather", S, TM, [_rows(x), _whole(norm1_g)], [("rows", D_MODEL, BF16)],
                              gather=[tr(w_in)[0].astype(BF16)])
    late_weights = [w_out[0].astype(BF16), tr(w_up)[0].astype(BF16), conv_w[0]]
    wi = g_in.reshape(W_IN_COLS, D_MODEL)
    c0, c1, c2, c3, c4, c5 = (sum(W_IN_SIZES[:i + 1]) for i in range(6))
    w_in_pt = jnp.concatenate([wi[c4:c5], wi[c5:], wi[c1:c2], wi[:c0], wi[c2:c3], wi[c3:c4],
                               wi[c0:c0 + KV_LORA], wi[c0 + KV_LORA:c1], jnp.zeros((64, D_MODEL), BF16)], axis=0)

    half = QK_ROPE // 2
    inv_freq = ROPE_THETA ** (-jnp.arange(half, dtype=F32) / half)
    ang = positions.astype(F32)[:, None] * inv_freq[None, :]
    cos, sin = jnp.cos(ang), jnp.sin(ang)
    zeros64 = jnp.zeros((S, 128 - QK_ROPE), F32)
    cos_p = jnp.concatenate([cos, cos, zeros64], axis=1)
    sin_fwd, sin_back = jnp.concatenate([-sin, sin, zeros64], axis=1), jnp.concatenate([sin, -sin, zeros64], axis=1)
    qa = jnp.arange(Q_BLOCK, dtype=jnp.int32)[:, None]
    kc = jnp.arange(SPAN, dtype=jnp.int32)[None, :]
    rel = (kc - WINDOW - qa).T
    in_band = (jnp.abs(rel) <= WINDOW).astype(F32).reshape(1, Q_BLOCK * SPAN)
    onehot_t = (_t5_bucket(rel).reshape(1, Q_BLOCK * SPAN) == jnp.arange(NUM_BUCKETS, dtype=jnp.int32)[:, None]).astype(F32)
    bias_t = _bias_table(rel_bias.T, onehot_t, in_band).reshape(H_B, SPAN, Q_BLOCK)
    sinks_b = jnp.broadcast_to(sinks.reshape(H_B, 1), (H_B, Q_BLOCK))

    proj, (g_qb, g_kvb) = _matmul(h1, w_in_pt, "nt", BF16, "proj", comm=([], [tr(w_q_b)[0].astype(BF16), w_kv_b[0].astype(BF16)]))
    wq = g_qb.reshape(H_A, QK_HEAD, Q_LORA)
    w_qb_pt = jnp.concatenate([wq[:, :QK_NOPE].reshape(H_A * QK_NOPE, Q_LORA),
                               jnp.pad(wq[:, QK_NOPE:], ((0, 0), (0, 128 - QK_ROPE), (0, 0))).reshape(H_A * 128, Q_LORA)], axis=0)
    w_kvb = _unstack_cols(g_kvb)

    def lat_fn(qlat, ckv, kr, gq, gkv, cs, sn):
        return _rms(qlat, gq), _rms(ckv, gkv), _rope128(kr, cs, sn)

    qn, ckvn, k_rope = _rowwise(lat_fn, "latents", S, TM,
                                [_rows(proj, 256, PROJ_QLAT), _rows(proj, 128, PROJ_CKV), _rows(proj, 128, PROJ_KROPE),
                                 _whole(q_a_norm_g), _whole(kv_a_norm_g), _rows(cos_p), _rows(sin_fwd)],
                                [("rows", Q_LORA, BF16), ("rows", KV_LORA, BF16), ("rows", 128, BF16)])
    def q_heads_fn(q, cs, sn):
        q = q * MLA_PRESCALE
        return (jnp.concatenate([jnp.concatenate([q[:, 128 * h:128 * (h + 1)], _rope128(q[:, 128 * (H_A + h):128 * (H_A + h + 1)], cs, sn)],
                                                 axis=1)[None] for h in range(H_A)], axis=0),)

    (q_full,) = _matmul(qn, w_qb_pt, "nt", None, "q_up_heads", tm=512, tn=2048,
                        epi=(q_heads_fn, [_rows(cos_p), _rows(sin_fwd)], [("heads", H_A, HEAD_PAD, BF16)]))

    def k_heads_fn(kvf, kr):
        return kvf, jnp.concatenate([jnp.concatenate([kvf[:, 256 * h:256 * h + QK_NOPE], kr], axis=1)[None] for h in range(H_A)], axis=0)

    kv, k_full = _matmul(ckvn, w_kvb, "nn", None, "kv_up_heads", tm=512, tn=2048,
                         epi=(k_heads_fn, [_rows(k_rope)], [("rows", H_A * (QK_NOPE + V_DIM), BF16), ("heads", H_A, HEAD_PAD, BF16)]))
    (o_a, lse), (g_out, g_up, g_cw) = _mla_fwd(q_full, k_full, kv, S, comm=([], late_weights))
    w_out_f = g_out.reshape(D_MODEL, D_MODEL)
    w_up_t = g_up.reshape(2 * D_FF, D_MODEL)
    conv_w_f = _unstack_cols(g_cw)

    (o_b, mixed), (g_down,) = _win_fwd(proj, o_a, bias_t, sinks_b, S, comm=([], [w_down[0].astype(BF16)]))
    w_down_f = g_down.reshape(D_FF, D_MODEL)

    x1, h2 = _matmul(mixed, w_out_f, "nn", None, "out_proj", residual=x, tm=512,
                     epi=(lambda a, g: (a, _rms(a, g)), [_whole(norm2_g)], [("rows", D_MODEL, F32), ("rows", D_MODEL, BF16)]))
    u = _matmul(h2, w_up_t, "nt", BF16, "ffn_up", tn=1408)
    act = _conv_gate_fwd(u, conv_w_f, conv_b, S)

    def final_fn(a, g, t):
        err = _rms(a, g) - t
        loss = 0.5 * jnp.sum(jnp.mean(err * err, axis=-1, keepdims=True), axis=0, keepdims=True)
        dx, dg = _rms_bwd(err * (1.0 / D_MODEL), a, g)
        return dx, dx, dg, jnp.broadcast_to(loss, (1, 128))

    gfin = final_norm_g.reshape(1, D_MODEL)
    dx2, dx2_b, d_gfin, loss_row = _matmul(
        act, w_down_f, "nn", None, "ffn_down_loss", residual=x1, tm=512,
        epi=(final_fn, [_whole(gfin), _rows(target)],
             [("rows", D_MODEL, F32), ("rows", D_MODEL, BF16), ("acc", 1, D_MODEL), ("acc", 1, 128)]))
    d_act = _matmul(dx2_b, w_down_f, "nt", BF16, "ffn_down_dx", tn=1408)
    d_w_down = _matmul(act, dx2_b, "tn", BF16, "ffn_down_dw")
    du_g, du_v, dcw_g, dcw_v, dcb_g, dcb_v = _conv_gate_bwd(u, conv_w_f, conv_b, d_act, S)
    d_conv_b = jnp.concatenate([dcb_g, dcb_v], axis=1)

    def norm_bwd_fn(dh, a, g, dres):
        dx, dg = _rms_bwd(dh, a, g)
        dx = dx + dres
        return dx, dx, dg

    dx1, dx1_b, d_g2 = _matmul(du_g, w_up_t, "nn", None, "ffn_up_dx_norm2_bwd", tm=256, a2=du_v,
                               epi=(norm_bwd_fn, [_rows(x1), _whole(norm2_g), _rows(dx2)],
                                    [("rows", D_MODEL, F32), ("rows", D_MODEL, BF16), ("acc", 1, D_MODEL)]))
    d_w_up_t = _matmul(du_g, h2, "tn", BF16, "ffn_up_dw", tm=256, a2=du_v)
    d_w_out = _matmul(mixed, dx1_b, "tn", BF16, "out_proj_dw", tm=512)

    def gate_bwd_fn(dm, ga, gb, oa, ob):
        sa, sb = _sigmoid(ga), _sigmoid(gb)
        return jnp.concatenate([dm * oa * sa * (1.0 - sa), dm * ob * sb * (1.0 - sb)], axis=1), dm * sa, dm * sb

    d_proj, do_a, do_b = _matmul(
        dx1_b, w_out_f, "nt", None, "out_proj_dx_gate_bwd", tm=512,
        epi=(gate_bwd_fn, [_rows(proj, 1024, PROJ_GA), _rows(proj, 1024, PROJ_GB), _rows(o_a), _rows(o_b)],
             [("cols", 2 * D_MODEL, 0, PROJ_P, BF16), ("rows", D_MODEL, BF16), ("rows", D_MODEL, BF16)]))

    d_proj, dk_acc, dv_acc, d_bias, d_sinks_g = _win_bwd(proj, bias_t, sinks_b, do_b, d_proj, S)
    d_sinks = d_sinks_g.reshape(1, H_B)

    early = [d_w_out.reshape(N_DEV, D_MODEL // N_DEV, D_MODEL), d_w_up_t.reshape(N_DEV, 2 * D_FF // N_DEV, D_MODEL),
             d_w_down.reshape(N_DEV, D_FF // N_DEV, D_MODEL), _stack_halves(dcw_g, dcw_v)]
    (dq_nope, dq_rope, dkv, dkr_heads), recv_early = _mla_bwd(q_full, k_full, kv, do_a, o_a, lse, cos_p, sin_back, S, comm=(early, []))

    d_qn = _matmul(dq_nope, w_qb_pt, "nn", F32, "q_up_dx", a2=dq_rope)
    d_ckvn = _matmul(dkv, w_kvb, "nt", F32, "kv_up_dx")

    def lat_bwd_fn(dqn, dckvn, dkr_h, cs, sn, qlat, ckv, gq, gkv, dkb, dvb):
        dql, dgq = _rms_bwd(dqn, qlat, gq)
        dck, dgkv = _rms_bwd(dckvn, ckv, gkv)
        dkr = dkr_h[0]
        for h in range(1, H_A):
            dkr = dkr + dkr_h[h]
        r1, r2 = _rope_bwd(dkr[:, :half], dkr[:, half:], cs, sn)
        tail = jnp.concatenate([dql, dkb, dvb, dck, r1, r2, jnp.zeros_like(dkr)], axis=1)
        return tail, dgq, dgkv

    shifted = lambda arr: (arr, lambda tm: pl.BlockSpec((tm, arr.shape[1]), lambda i, *_: (i + WINDOW // tm, 0)))
    TL = min(128, S)
    d_proj, d_gq, d_gkv = _rowwise(lat_bwd_fn, "latents_bwd", S, TL,
                                   [_rows(d_qn), _rows(d_ckvn), _heads(dkr_heads), _rows(cos), _rows(sin), _rows(proj, 256, PROJ_QLAT), _rows(proj, 128, PROJ_CKV),
                                    _whole(q_a_norm_g), _whole(kv_a_norm_g), shifted(dk_acc), shifted(dv_acc)],
                                   [("cols", 1024, 3, PROJ_P, BF16), ("acc", 1, Q_LORA), ("acc", 1, KV_LORA)], into=(d_proj, 0))
    dp = _matmul(d_proj, h1, "tn", BF16, "proj_dw", tm=512)

    late = jnp.concatenate([dp[3072:3328], dp[3840:3968], dp[3968:4032], dp[2048:3072], dp[3328:3584],
                            dp[3584:3840], dp[0:1024], dp[1024:2048]], axis=0).reshape(N_DEV, W_IN_COLS // N_DEV, D_MODEL)
    late_sems, late_src, late_land, started = _exchange_start(late, "late_grads_start")

    def norm1_bwd_fn(dh, a, g, dres):
        dx, dg = _rms_bwd(dh, a, g)
        return dx + dres, dg

    grad_x, d_g1 = _matmul(
        d_proj, w_in_pt, "nn", None, "proj_dx_norm1_bwd", tm=512,
        epi=(norm1_bwd_fn, [_rows(x), _whole(norm1_g + started[:1, :1]), _rows(dx1)], [("rows", D_MODEL, F32), ("acc", 1, D_MODEL)]))

    transposed = ("w_in", "w_q_b", "w_up")
    ready_names = ["w_out", "w_up", "w_down", "conv_w"]
    ready_wmv = [(w_out, m_w_out, v_w_out), (tr(w_up), tr(m_w_up), tr(v_w_up)), (w_down, m_w_down, v_w_down), (conv_w, m_conv_w, v_conv_w)]
    big = {n: _adamw(r, *wmv, "adamw_" + n) for n, r, wmv in zip(ready_names, recv_early, ready_wmv)}

    (d_bias, dq_nope_l, dkv_l), _ = lax.optimization_barrier(((d_bias, dq_nope, dkv), started))
    d_rel_bias = _bias_table_bwd(d_bias.reshape(H_B, Q_BLOCK * SPAN), onehot_t).T
    d_w_qb_pt = _matmul(dq_nope_l, qn, "tn", BF16, "q_up_dw", tm=512, a2=dq_rope)
    d_w_kvb = _matmul(ckvn, dkv_l, "tn", BF16, "kv_up_dw", tn=2048)
    d_w_qb_t = jnp.concatenate([d_w_qb_pt[:H_A * QK_NOPE].reshape(H_A, QK_NOPE, Q_LORA),
                                d_w_qb_pt[H_A * QK_NOPE:].reshape(H_A, 128, Q_LORA)[:, :QK_ROPE]], axis=1)

    after = lax.optimization_barrier([big[n][0] for n in ready_names] + [d_rel_bias, d_w_qb_t, d_w_kvb])
    landed = _exchange_wait(late_sems, late_src, late_land, after[-1], "late_grads_wait")
    me = 4 * lax.axis_index("x") + 2 * lax.axis_index("y") + lax.axis_index("c")
    landed = lax.dynamic_update_slice_in_dim(landed, lax.dynamic_slice_in_dim(late, me, 1, axis=0), me, axis=0)
    big["w_in"] = _adamw(landed, tr(w_in), tr(m_w_in), tr(v_w_in), "adamw_w_in")

    small_parts = [d_g1, d_gq, d_gkv, d_rel_bias.reshape(1, NUM_BUCKETS * H_B), d_sinks, d_g2, d_conv_b, d_gfin, loss_row[:, :1]]
    small = jnp.concatenate(small_parts, axis=1)
    n_small = small.shape[1]
    pad = (-n_small) % 128
    small = jnp.pad(small, ((0, 0), (0, pad)))
    small, _ = lax.optimization_barrier((small, [landed, *after]))
    recv_qb, recv_kvb, recv_small = _exchange([after[-2], _stack_cols(after[-1])], [small], "exchange_small_grads")
    big["w_q_b"] = _adamw(recv_qb, tr(w_q_b), tr(m_w_q_b), tr(v_w_q_b), "adamw_w_q_b")
    big["w_kv_b"] = _adamw(recv_kvb, w_kv_b, m_w_kv_b, v_w_kv_b, "adamw_w_kv_b")

    def flat(a):
        return a.reshape(1, -1)

    small_w = [norm1_g, q_a_norm_g, kv_a_norm_g, rel_bias, sinks, norm2_g, conv_b, final_norm_g]
    small_m = [m_norm1_g, m_q_a_norm_g, m_kv_a_norm_g, m_rel_bias, m_sinks, m_norm2_g, m_conv_b, m_final_norm_g]
    small_v = [v_norm1_g, v_q_a_norm_g, v_kv_a_norm_g, v_rel_bias, v_sinks, v_norm2_g, v_conv_b, v_final_norm_g]
    cat = lambda parts: jnp.pad(jnp.concatenate([flat(a) for a in parts], axis=1), ((0, 0), (0, pad + 1)))[None]
    sm = _adamw(recv_small, cat(small_w), cat(small_m), cat(small_v), "adamw_small")

    loss = sm[0][0, 0, n_small - 1]
    order =["norm1_g", "w_in", "q_a_norm_g", "w_q_b", "kv_a_norm_g", "w_kv_b", "rel_bias", "sinks", "w_out", "norm2_g", "w_up",
             "conv_w", "conv_b", "w_down", "final_norm_g"]
    small_names = ["norm1_g", "q_a_norm_g", "kv_a_norm_g", "rel_bias", "sinks", "norm2_g", "conv_b", "final_norm_g"]
    offs, o = {}, 0
    for n, a in zip(small_names, small_w):
        offs[n] = (o, a.size, a.shape)
        o += a.size
    outs = [loss, grad_x[None]]
    for kind in range(4):
        for n in order:
            if n in big:
                outs.append(tr(big[n][kind]) if n in transposed else big[n][kind])
            else:
                o, size, shape = offs[n]
                outs.append(sm[kind][0, 0, o:o + size].reshape(shape))
    return tuple(outs)
```

```python
import math

import jax
import jax.numpy as jnp
from jax import lax
from jax.experimental import pallas as pl
from jax.experimental.pallas import tpu as pltpu

F32 = jnp.float32
BF16 = jnp.bfloat16

N_DEV = 8
D_MODEL = 1024
EPS = 1e-6
H_A, QK_NOPE, QK_ROPE, V_DIM, Q_LORA, KV_LORA = 8, 128, 64, 128, 256, 128
QK_HEAD = QK_NOPE + QK_ROPE
HEAD_PAD = 256
ROPE_THETA = 10000.0
H_B, KV_B, GROUP, HD_B, WINDOW, Q_BLOCK = 16, 4, 4, 64, 128, 128
SPAN = Q_BLOCK + 2 * WINDOW
NUM_BUCKETS, MAX_DISTANCE = 32, 128
D_FF = 2816
ADAM_LR, ADAM_B1, ADAM_B2, ADAM_EPS, ADAM_WD, ADAM_STEP = 0.001, 0.9, 0.999, 1e-08, 0.01, 10

W_IN_SIZES = (Q_LORA, KV_LORA + QK_ROPE, H_B * HD_B, KV_B * HD_B, KV_B * HD_B, D_MODEL, D_MODEL)
W_IN_COLS = sum(W_IN_SIZES)
PROJ_P = 4096
PROJ_GA, PROJ_GB, PROJ_QB, PROJ_QLAT, PROJ_KB, PROJ_VB, PROJ_CKV, PROJ_KROPE = 0, 1, 2, 12, 13, 14, 30, 31

VMEM_LIMIT = 56 * 1024 * 1024

NN = (((1,), (0,)), ((), ()))
NT = (((1,), (1,)), ((), ()))
TN = (((0,), (0,)), ((), ()))


def _pcall(body, *, name, grid, in_specs, out_specs, out_shape, scratch_shapes=(), dims=None, comm=None, aliases=None, two_level=False):
    if comm is None:
        params = pltpu.CompilerParams(dimension_semantics=dims, vmem_limit_bytes=VMEM_LIMIT)
        return pl.pallas_call(body, name=name, grid=grid, in_specs=in_specs, out_specs=out_specs, out_shape=out_shape,
                              scratch_shapes=list(scratch_shapes), input_output_aliases=aliases or {}, compiler_params=params)
    assert not aliases
    stacked, replicated = comm
    arrs = [*stacked, *replicated]
    n_st, n_arr = len(stacked), len(arrs)
    single = not isinstance(out_specs, (list, tuple))
    o_specs, o_shape = ([out_specs], [out_shape]) if single else (list(out_specs), list(out_shape))
    n_in, n_out = len(in_specs), len(o_specs)

    def wrapped(*refs):
        c_in = refs[n_in:n_in + n_arr]
        c_out = refs[n_in + n_arr + n_out:n_in + 2 * n_arr + n_out]
        sems = refs[len(refs) - 3:]
        own = (*refs[:n_in], *refs[n_in + n_arr:n_in + n_arr + n_out], *refs[n_in + 2 * n_arr + n_out:len(refs) - 3])
        if two_level:
            assert n_st == 0
            start, finish = (lambda: _gather2(c_in, c_out, sems, False)), (lambda: _gather2(c_in, c_out, sems, True))
        else:
            start, finish = (lambda: _xchg_start(c_in, c_out, sems, n_st)), (lambda: _xchg_finish(c_in, c_out, sems, n_st))
        if not grid:
            start()
            finish()
            return
        first = last = None
        for d, n in enumerate(grid):
            pid = pl.program_id(d)
            first = (pid == 0) if first is None else first & (pid == 0)
            last = (pid == n - 1) if last is None else last & (pid == n - 1)

        pl.when(first)(start)
        body(*own)
        pl.when(last)(finish)

    params = pltpu.CompilerParams(dimension_semantics=("arbitrary",) * len(grid), vmem_limit_bytes=VMEM_LIMIT)
    call = pl.pallas_call(wrapped, name=name, grid=grid, in_specs=[*in_specs, *[ANY] * n_arr], out_specs=[*o_specs, *[ANY] * n_arr],
                          out_shape=[*o_shape, *_xchg_out_shapes(stacked, replicated)],
                          scratch_shapes=[*scratch_shapes, *_xchg_sems(n_arr)], compiler_params=params)

    def run(*args):
        res = call(*args, *arrs)
        outs, landed = res[:n_out], res[n_out:]
        return (outs[0] if single else outs), landed

    return run


def _dot(a, b, dn):
    return lax.dot_general(a, b, dn, preferred_element_type=F32)


def _tile(n, target):
    best = None
    for t in range(128, min(n, target) + 1, 128):
        if n % t == 0:
            best = t
    return n if best is None else best


def _matmul(a, b, mode, out_dtype, name, residual=None, tm=1024, tn=1024, comm=None, a2=None, epi=None):
    if mode == "nn":
        (M, K), N = a.shape, b.shape[1]
    elif mode == "nt":
        (M, K), N = a.shape, b.shape[0]
    else:
        (K, M), N = a.shape, b.shape[1]
    tm, tn = _tile(M, tm), _tile(N, tn)
    a_spec = pl.BlockSpec((K, tm), lambda i, j: (0, i)) if mode == "tn" else pl.BlockSpec((tm, K), lambda i, j: (i, 0))
    b_spec = pl.BlockSpec((tn, b.shape[1]), lambda i, j: (j, 0)) if mode == "nt" else pl.BlockSpec((K, tn), lambda i, j: (0, j))
    o_spec = pl.BlockSpec((tm, tn), lambda i, j: (i, j))
    in_specs, args = [a_spec, b_spec], [a, b]
    n1 = M // tm
    if a2 is not None and mode == "tn":
        assert M % tm == 0 and a2.shape[1] % tm == 0
        in_specs[0] = pl.BlockSpec((K, tm), lambda i, j: (0, jnp.minimum(i, n1 - 1)))
        in_specs.append(pl.BlockSpec((K, tm), lambda i, j: (0, jnp.maximum(i - n1, 0))))
        args.append(a2)
        M += a2.shape[1]
    elif a2 is not None:
        assert (mode == "nt" and K + a2.shape[1] == b.shape[1]) or (mode == "nn" and K + a2.shape[1] == b.shape[0])
        if mode == "nn":
            b_spec = in_specs[1] = pl.BlockSpec((b.shape[0], tn), lambda i, j: (0, j))
        in_specs.append(pl.BlockSpec((tm, a2.shape[1]), lambda i, j: (i, 0)))
        args.append(a2)
    if residual is not None:
        in_specs.append(o_spec)
        args.append(residual)
    n_mm = len(args)
    scratch = [pltpu.VMEM((tm, K), a.dtype)] if mode == "tn" else []
    if epi is None:
        out_specs, out_shape, is_acc = o_spec, jax.ShapeDtypeStruct((M, N), out_dtype), None
    else:
        assert tn == N
        fn, epi_ins, epi_outs = epi
        in_specs += [mk(tm) for _, mk in epi_ins]
        args += [arr for arr, _ in epi_ins]
        out_specs, out_shape, is_acc = _row_out_specs(epi_outs, M, tm)

    def body(*refs):
        a_ref, b_ref = refs[0], refs[1]
        n_out = 1 if epi is None else len(is_acc)
        out_refs = refs[len(args):len(args) + n_out]
        if mode == "tn":
            at_ref = refs[len(args) + n_out]

            first_col = pl.program_id(1) == 0
            from_a = first_col if a2 is None else first_col & (pl.program_id(0) < n1)

            @pl.when(from_a)
            def _():
                at_ref[...] = a_ref[...].T

            if a2 is not None:
                @pl.when(first_col & (pl.program_id(0) >= n1))
                def _():
                    at_ref[...] = refs[2][...].T

            acc = _dot(at_ref[...], b_ref[...], NN)
        elif a2 is not None and mode == "nt":
            acc = _dot(a_ref[...], b_ref[:, :K], NT) + _dot(refs[2][...], b_ref[:, K:], NT)
        elif a2 is not None:
            acc = _dot(a_ref[...], b_ref[:K, :], NN) + _dot(refs[2][...], b_ref[K:, :], NN)
        else:
            acc = _dot(a_ref[...], b_ref[...], NT if mode == "nt" else NN)
        if residual is not None:
            acc = acc + refs[n_mm - 1][...]
        if epi is None:
            out_refs[0][...] = acc.astype(out_dtype)
        else:
            _store_rows(out_refs, fn(acc, *[_load_f32(r) for r in refs[n_mm:len(args)]]), is_acc)

    return _pcall(body, name=name, grid=(M // tm, N // tn), in_specs=in_specs, out_specs=out_specs,
                  out_shape=out_shape, scratch_shapes=scratch,
                  dims=("arbitrary" if epi is not None else "parallel", "arbitrary"), comm=comm)(*args)


def _rows(arr, width=None, col=0):
    width = arr.shape[1] if width is None else width
    return (arr, lambda tm: pl.BlockSpec((tm, width), lambda i, *_: (i, col)))


def _heads(arr):
    return (arr, lambda tm: pl.BlockSpec((arr.shape[0], tm, arr.shape[2]), lambda i, *_: (0, i, 0)))


def _whole(arr):
    nd = arr.ndim
    return (arr, lambda tm: pl.BlockSpec(arr.shape, lambda i, *_: (0,) * nd))


def _row_out_specs(outs, n_rows, tm):
    out_specs, out_shape, is_acc = [], [], []
    for o in outs:
        if o[0] == "rows":
            out_specs.append(pl.BlockSpec((tm, o[1]), lambda i, *_: (i, 0)))
            out_shape.append(jax.ShapeDtypeStruct((n_rows, o[1]), o[2]))
        elif o[0] == "cols":
            out_specs.append(pl.BlockSpec((tm, o[1]), lambda i, *_, c=o[2]: (i, c)))
            out_shape.append(jax.ShapeDtypeStruct((n_rows, o[3]), o[4]))
        elif o[0] == "heads":
            out_specs.append(pl.BlockSpec((o[1], tm, o[2]), lambda i, *_: (0, i, 0)))
            out_shape.append(jax.ShapeDtypeStruct((o[1], n_rows, o[2]), o[3]))
        else:
            out_specs.append(pl.BlockSpec((o[1], o[2]), lambda i, *_: (0, 0)))
            out_shape.append(jax.ShapeDtypeStruct((o[1], o[2]), F32))
        is_acc.append(o[0] == "acc")
    return out_specs, out_shape, is_acc


def _load_f32(r):
    v = r[...]
    return v.astype(F32) if v.dtype == BF16 else v


def _store_rows(out_refs, vals, is_acc):
    for r, v, acc in zip(out_refs, vals, is_acc):
        if acc:
            @pl.when(pl.program_id(0) == 0)
            def _():
                r[...] = jnp.zeros_like(r)

            r[...] += v
        else:
            r[...] = v.astype(r.dtype)


def _rowwise(fn, name, n_rows, tm, ins, outs, upcast=True, into=None, gather=None):
    tm = min(tm, n_rows)
    assert n_rows % tm == 0
    in_specs = [mk(tm) for _, mk in ins]
    out_specs, out_shape, is_acc = _row_out_specs(outs, n_rows, tm)
    n_in = len(ins)
    args = [a for a, _ in ins]
    aliases = {}
    if into is not None:
        in_specs.append(ANY)
        args.append(into[0])
        aliases = {n_in: into[1]}

    def body(*refs):
        vals = fn(*[_load_f32(r) if upcast else r[...] for r in refs[:n_in]])
        _store_rows(refs[len(args):], vals, is_acc)

    return _pcall(body, name=name, grid=(n_rows // tm,), in_specs=in_specs, out_specs=out_specs,
                  out_shape=out_shape, dims=("arbitrary",), aliases=aliases,
                  comm=None if gather is None else ([], gather), two_level=True)(*args)


def _rms(x, g):
    r = lax.rsqrt(jnp.mean(x * x, axis=-1, keepdims=True) + EPS)
    return x * r * g


def _rms_bwd(dy, x, g):
    r = lax.rsqrt(jnp.mean(x * x, axis=-1, keepdims=True) + EPS)
    xhat = x * r
    dxhat = dy * g
    dx = r * (dxhat - xhat * jnp.mean(dxhat * xhat, axis=-1, keepdims=True))
    return dx, jnp.sum(dy * xhat, axis=0, keepdims=True)


def _rope_bwd(d1, d2, cos, sin):
    return d1 * cos + d2 * sin, d2 * cos - d1 * sin


def _rope128(x, cos_p, sin_p):
    lane = lax.broadcasted_iota(jnp.int32, x.shape, 1)
    swapped = jnp.where(lane < QK_ROPE // 2, pltpu.roll(x, 128 - QK_ROPE // 2, axis=1), pltpu.roll(x, QK_ROPE // 2, axis=1))
    return x * cos_p + swapped * sin_p


def _sigmoid(x):
    return 1.0 / (1.0 + jnp.exp(-x))


MLA_SCALE = 1.0 / math.sqrt(QK_HEAD)
MLA_PRESCALE = MLA_SCALE * math.log2(math.e)
MLA_TQ, MLA_KC = 1024, 1024


def _mla_fwd(q_full, k_full, kv, S, comm=None):
    tq, kc = min(2 * MLA_TQ, S), min(MLA_KC, S)

    def body(q_ref, k_ref, v_ref, o_ref, lse_ref):
        q = q_ref[0]
        m = jnp.full((tq, 1), -1e30, F32)
        l = jnp.zeros((tq, 1), F32)
        acc = jnp.zeros((tq, V_DIM), F32)
        for c in range(S // kc):
            s = _dot(q, k_ref[0, c * kc:(c + 1) * kc, :], NT)
            m_new = jnp.maximum(m, jnp.max(s, axis=-1, keepdims=True))
            alpha = jnp.exp2(m - m_new)
            p = jnp.exp2(s - m_new)
            l = alpha * l + jnp.sum(p, axis=-1, keepdims=True)
            acc = alpha * acc + _dot(p.astype(BF16), v_ref[c * kc:(c + 1) * kc, :], NN)
            m = m_new
        o_ref[...] = (acc / l).astype(BF16)
        lse_ref[0] = m + jnp.log2(l)

    return _pcall(
        body, name="mla_fwd", grid=(H_A, S // tq),
        in_specs=[pl.BlockSpec((1, tq, HEAD_PAD), lambda h, i: (h, i, 0)),
                  pl.BlockSpec((1, S, HEAD_PAD), lambda h, i: (h, 0, 0)),
                  pl.BlockSpec((S, V_DIM), lambda h, i: (0, 2 * h + 1))],
        out_specs=[pl.BlockSpec((tq, V_DIM), lambda h, i: (i, h)),
                   pl.BlockSpec((1, tq, 1), lambda h, i: (h, i, 0))],
        out_shape=[jax.ShapeDtypeStruct((S, H_A * V_DIM), BF16), jax.ShapeDtypeStruct((H_A, S, 1), F32)],
        dims=("parallel", "parallel"), comm=comm)(q_full, k_full, kv)


def _mla_bwd(q_full, k_full, kv, do_a, o_a, lse, cos_p, sin_back, S, comm=None):
    tq, kc = min(MLA_TQ, S), min(MLA_KC, S)

    def body(q_ref, k_ref, v_ref, do_ref, o_ref, lse_ref, cos_ref, sin_ref, dqn_ref, dqr_ref, dkv_out, dkr_out, dk_ref, dv_ref):
        @pl.when(pl.program_id(1) == 0)
        def _():
            dk_ref[...] = jnp.zeros_like(dk_ref)
            dv_ref[...] = jnp.zeros_like(dv_ref)

        q = q_ref[0]
        do = do_ref[...]
        lse_q = lse_ref[0]
        delta = jnp.sum(do.astype(F32) * o_ref[...].astype(F32), axis=-1, keepdims=True)
        dq = jnp.zeros((tq, HEAD_PAD), F32)
        for c in range(S // kc):
            k = k_ref[0, c * kc:(c + 1) * kc, :]
            v = v_ref[c * kc:(c + 1) * kc, :]
            p = jnp.exp2(_dot(q, k, NT) - lse_q)
            ds = (p * (_dot(do, v, NT) - delta)).astype(BF16)
            dq = dq + _dot(ds, k, NN)
            dk_ref[0, c * kc:(c + 1) * kc, :] += _dot(ds, q, TN)
            dv_ref[0, c * kc:(c + 1) * kc, :] += _dot(p.astype(BF16), do, TN)
        dq = dq * MLA_SCALE
        dqn_ref[...] = dq[:, :QK_NOPE].astype(BF16)
        dqr_ref[...] = _rope128(dq[:, QK_NOPE:], cos_ref[...], sin_ref[...]).astype(BF16)

        @pl.when(pl.program_id(1) == S // tq - 1)
        def _():
            dk = dk_ref[0] * math.log(2.0)
            dkv_out[...] = jnp.concatenate([dk[:, :QK_NOPE], dv_ref[0]], axis=1).astype(BF16)
            dkr_out[0] = dk[:, QK_NOPE:QK_HEAD].astype(BF16)

    return _pcall(
        body, name="mla_bwd", grid=(H_A, S // tq),
        in_specs=[pl.BlockSpec((1, tq, HEAD_PAD), lambda h, i: (h, i, 0)),
                  pl.BlockSpec((1, S, HEAD_PAD), lambda h, i: (h, 0, 0)),
                  pl.BlockSpec((S, V_DIM), lambda h, i: (0, 2 * h + 1)),
                  pl.BlockSpec((tq, V_DIM), lambda h, i: (i, h)),
                  pl.BlockSpec((tq, V_DIM), lambda h, i: (i, h)),
                  pl.BlockSpec((1, tq, 1), lambda h, i: (h, i, 0)),
                  pl.BlockSpec((tq, 128), lambda h, i: (i, 0)), pl.BlockSpec((tq, 128), lambda h, i: (i, 0))],
        out_specs=[pl.BlockSpec((tq, QK_NOPE), lambda h, i: (i, h)),
                   pl.BlockSpec((tq, 128), lambda h, i: (i, h)),
                   pl.BlockSpec((S, QK_NOPE + V_DIM), lambda h, i: (0, h)),
                   pl.BlockSpec((1, S, QK_ROPE), lambda h, i: (h, 0, 0))],
        out_shape=[jax.ShapeDtypeStruct((S, H_A * QK_NOPE), BF16), jax.ShapeDtypeStruct((S, H_A * 128), BF16),
                   jax.ShapeDtypeStruct((S, H_A * (QK_NOPE + V_DIM)), BF16), jax.ShapeDtypeStruct((H_A, S, QK_ROPE), BF16)],
        scratch_shapes=[pltpu.VMEM((1, S, HEAD_PAD), F32), pltpu.VMEM((1, S, V_DIM), F32)],
        dims=("parallel", "arbitrary"), comm=comm)(q_full, k_full, kv, do_a, o_a, lse, cos_p, sin_back)


WIN_SCALE = 1.0 / math.sqrt(HD_B)


WIN_PER_STEP = 4


def _win_specs(S):
    last, B = S // Q_BLOCK - 1, WIN_PER_STEP
    qspec = pl.BlockSpec((B * Q_BLOCK, H_B * HD_B), lambda i: (i, PROJ_QB))
    kspecs = [[pl.BlockSpec((Q_BLOCK, KV_B * HD_B), lambda i, d=d, c=c: (jnp.clip(B * i + d, 0, last), c)) for d in range(-1, B + 1)]
              for c in (PROJ_KB, PROJ_VB)]
    bias_spec = pl.BlockSpec((H_B, SPAN, Q_BLOCK), lambda i: (0, 0, 0))
    sink_spec = pl.BlockSpec((H_B, Q_BLOCK), lambda i: (0, 0))
    return qspec, kspecs, bias_spec, sink_spec


def _win_edge_ok(n, n_blk):
    row = lax.broadcasted_iota(jnp.int32, (SPAN, 1), 0)
    return jnp.logical_not(((n == 0) & (row < WINDOW)) | ((n == n_blk - 1) & (row >= SPAN - WINDOW)))


def _lanes4(pieces):
    return jnp.concatenate(pieces, axis=1)


def _win_probs(kg, q4t, bias_ref, sink_ref, g, edge_ok):
    bias4 = _lanes4([bias_ref[GROUP * g + j] for j in range(GROUP)])
    sink4 = _lanes4([sink_ref[GROUP * g + j:GROUP * g + j + 1, :] for j in range(GROUP)])
    s = jnp.where(edge_ok, _dot(kg, q4t, NN) + bias4, -1e30)
    m = jnp.maximum(jnp.max(s, axis=0, keepdims=True), sink4)
    p = jnp.exp(s - m)
    e_sink = jnp.exp(sink4 - m)
    inv_l = 1.0 / (jnp.sum(p, axis=0, keepdims=True) + e_sink)
    return p * inv_l, e_sink * inv_l


def _group_t(xt, g):
    return _lanes4([xt[HD_B * (GROUP * g + j):HD_B * (GROUP * g + j + 1), :] for j in range(GROUP)])


def _rows_of(ref, b):
    return ref[Q_BLOCK * b:Q_BLOCK * (b + 1), :]


def _win_fwd(proj, o_a, bias_t, sinks_b, S, comm=None):
    n_blk, B = S // Q_BLOCK, WIN_PER_STEP
    qspec, kspecs, bias_spec, sink_spec = _win_specs(S)
    rows = lambda col: pl.BlockSpec((B * Q_BLOCK, H_B * HD_B), lambda i: (i, col))

    def body(q_ref, *refs):
        k_refs, v_refs = refs[:B + 2], refs[B + 2:2 * B + 4]
        bias_ref, sink_ref, ga_ref, gb_ref, oa_ref, o_ref, mixed_ref = refs[2 * B + 4:]
        for b in range(B):
            edge_ok = _win_edge_ok(B * pl.program_id(0) + b, n_blk)
            k = jnp.concatenate([r[...] for r in k_refs[b:b + 3]], axis=0)
            vt = jnp.concatenate([r[...] for r in v_refs[b:b + 3]], axis=0).T
            qt = (_rows_of(q_ref, b).astype(F32) * WIN_SCALE).T.astype(BF16)
            parts = []
            for g in range(KV_B):
                p, _ = _win_probs(k[:, HD_B * g:HD_B * (g + 1)], _group_t(qt, g), bias_ref, sink_ref, g, edge_ok)
                o4t = _dot(vt[HD_B * g:HD_B * (g + 1), :], p.astype(BF16), NN)
                parts += [o4t[:, Q_BLOCK * j:Q_BLOCK * (j + 1)] for j in range(GROUP)]
            ob = jnp.concatenate(parts, axis=0).T
            o_ref[Q_BLOCK * b:Q_BLOCK * (b + 1), :] = ob.astype(BF16)
            ga, gb, oa = (_rows_of(r, b).astype(F32) for r in (ga_ref, gb_ref, oa_ref))
            mixed_ref[Q_BLOCK * b:Q_BLOCK * (b + 1), :] = (_sigmoid(ga) * oa + _sigmoid(gb) * ob).astype(BF16)

    return _pcall(body, name="win_fwd_mix", grid=(n_blk // B,),
                  in_specs=[qspec, *kspecs[0], *kspecs[1], bias_spec, sink_spec, rows(PROJ_GA), rows(PROJ_GB), rows(0)],
                  out_specs=[rows(0), rows(0)],
                  out_shape=[jax.ShapeDtypeStruct((S, H_B * HD_B), BF16)] * 2,
                  dims=("parallel",), comm=comm)(*[proj] * (2 * B + 5), bias_t, sinks_b, proj, proj, o_a)


def _win_bwd(proj, bias_t, sinks_b, do_b, d_proj, S):
    n_blk, B = S // Q_BLOCK, WIN_PER_STEP
    qspec, kspecs, bias_spec, sink_spec = _win_specs(S)

    def body(q_ref, *refs):
        k_refs, v_refs = refs[:B + 2], refs[B + 2:2 * B + 4]
        bias_ref, sink_ref, do_ref, _, dq_ref, dk_ref, dv_ref, dbias_ref, dsink_ref, dsink_acc = refs[2 * B + 4:]
        i = pl.program_id(0)

        @pl.when(i == 0)
        def _():
            dk_ref[...] = jnp.zeros_like(dk_ref)
            dv_ref[...] = jnp.zeros_like(dv_ref)
            dbias_ref[...] = jnp.zeros_like(dbias_ref)
            dsink_acc[...] = jnp.zeros_like(dsink_acc)

        d_bias, d_sink, dk_blocks, dv_blocks = {}, {}, [], []
        for b in range(B):
            edge_ok = _win_edge_ok(B * i + b, n_blk)
            k = jnp.concatenate([r[...] for r in k_refs[b:b + 3]], axis=0)
            v = jnp.concatenate([r[...] for r in v_refs[b:b + 3]], axis=0)
            kt = k.T
            qt = (_rows_of(q_ref, b).astype(F32) * WIN_SCALE).T.astype(BF16)
            dot_ = _rows_of(do_ref, b).astype(F32).T.astype(BF16)
            dq_parts, dks, dvs = [], [], []
            for g in range(KV_B):
                kg, vg = k[:, HD_B * g:HD_B * (g + 1)], v[:, HD_B * g:HD_B * (g + 1)]
                q4t, do4t = _group_t(qt, g), _group_t(dot_, g)
                p, p_sink = _win_probs(kg, q4t, bias_ref, sink_ref, g, edge_ok)
                dp = _dot(vg, do4t, NN)
                delta = jnp.sum(p * dp, axis=0, keepdims=True)
                ds = p * (dp - delta)
                d_bias[g] = ds if b == 0 else d_bias[g] + ds
                d_sink[g] = -p_sink * delta if b == 0 else d_sink[g] - p_sink * delta
                dsb = ds.astype(BF16)
                dq4t = _dot(kt[HD_B * g:HD_B * (g + 1), :], dsb, NN) * WIN_SCALE
                dq_parts += [dq4t[:, Q_BLOCK * j:Q_BLOCK * (j + 1)] for j in range(GROUP)]
                dks.append(_dot(dsb, q4t, NT))
                dvs.append(_dot(p.astype(BF16), do4t, NT))
            dq_ref[Q_BLOCK * b:Q_BLOCK * (b + 1), :] = jnp.concatenate(dq_parts, axis=0).T.astype(BF16)
            dk_blocks.append(jnp.concatenate(dks, axis=1))
            dv_blocks.append(jnp.concatenate(dvs, axis=1))

        for g in range(KV_B):
            for j in range(GROUP):
                dbias_ref[GROUP * g + j] += d_bias[g][:, Q_BLOCK * j:Q_BLOCK * (j + 1)]
            dsink_acc[g:g + 1, :] += d_sink[g]

        def overlap(blocks):
            out = blocks[0]
            for blk in blocks[1:]:
                keep = out.shape[0] - 2 * Q_BLOCK
                out = jnp.concatenate([out[:keep], out[keep:] + blk[:2 * Q_BLOCK], blk[2 * Q_BLOCK:]], axis=0)
            return out

        rows = pl.ds(pl.multiple_of(i * (B * Q_BLOCK), B * Q_BLOCK), (B + 2) * Q_BLOCK)
        dk_ref[rows, :] += overlap(dk_blocks)
        dv_ref[rows, :] += overlap(dv_blocks)

        @pl.when(i == n_blk // B - 1)
        def _():
            acc = dsink_acc[...]
            dsink_ref[...] = jnp.concatenate(
                [jnp.sum(acc[:, Q_BLOCK * j:Q_BLOCK * (j + 1)], axis=1, keepdims=True) for j in range(GROUP)], axis=1)

    whole = lambda shape: pl.BlockSpec(shape, lambda i: (0,) * len(shape))
    return _pcall(
        body, name="win_bwd", grid=(n_blk // B,),
        in_specs=[qspec, *kspecs[0], *kspecs[1], bias_spec, sink_spec, pl.BlockSpec((B * Q_BLOCK, H_B * HD_B), lambda i: (i, 0)), ANY],
        out_specs=[qspec, whole((S + 2 * WINDOW, KV_B * HD_B)),
                   whole((S + 2 * WINDOW, KV_B * HD_B)), whole((H_B, SPAN, Q_BLOCK)), whole((KV_B, GROUP))],
        out_shape=[jax.ShapeDtypeStruct((S, PROJ_P), BF16), jax.ShapeDtypeStruct((S + 2 * WINDOW, KV_B * HD_B), F32),
                   jax.ShapeDtypeStruct((S + 2 * WINDOW, KV_B * HD_B), F32), jax.ShapeDtypeStruct((H_B, SPAN, Q_BLOCK), F32),
                   jax.ShapeDtypeStruct((KV_B, GROUP), F32)],
        scratch_shapes=[pltpu.VMEM((KV_B, GROUP * Q_BLOCK), F32)],
        dims=("arbitrary",), aliases={2 * B + 8: 0})(*[proj] * (2 * B + 5), bias_t, sinks_b, do_b, d_proj)


def _bias_table(rel_bias_t, onehot_t, in_band):
    def body(rb_ref, oh_ref, band_ref, o_ref):
        t = lax.dot_general(rb_ref[...], oh_ref[...], NN, preferred_element_type=F32, precision=lax.Precision.HIGHEST)
        o_ref[...] = jnp.where(band_ref[...] > 0.5, t, -1e30)

    n = onehot_t.shape[1]
    tn = _tile(n, 8192)
    return _pcall(body, name="bias_table", grid=(n // tn,),
                  in_specs=[pl.BlockSpec((H_B, NUM_BUCKETS), lambda j: (0, 0)), pl.BlockSpec((NUM_BUCKETS, tn), lambda j: (0, j)),
                            pl.BlockSpec((1, tn), lambda j: (0, j))],
                  out_specs=pl.BlockSpec((H_B, tn), lambda j: (0, j)),
                  out_shape=jax.ShapeDtypeStruct((H_B, n), F32), dims=("parallel",))(rel_bias_t, onehot_t, in_band)


def _bias_table_bwd(dbias, onehot_t):
    n = onehot_t.shape[1]
    tk = _tile(n, 8192)

    def body(d_ref, oh_ref, o_ref):
        @pl.when(pl.program_id(0) == 0)
        def _():
            o_ref[...] = jnp.zeros_like(o_ref)

        o_ref[...] += lax.dot_general(d_ref[...], oh_ref[...], NT, preferred_element_type=F32, precision=lax.Precision.HIGHEST)

    return _pcall(body, name="bias_table_bwd", grid=(n // tk,),
                  in_specs=[pl.BlockSpec((H_B, tk), lambda j: (0, j)), pl.BlockSpec((NUM_BUCKETS, tk), lambda j: (0, j))],
                  out_specs=pl.BlockSpec((H_B, NUM_BUCKETS), lambda j: (0, 0)),
                  out_shape=jax.ShapeDtypeStruct((H_B, NUM_BUCKETS), F32), dims=("arbitrary",))(dbias, onehot_t)


CONV_STRIP = 128
N_STRIPS = D_FF // CONV_STRIP
CONV_ROWS = 128
HALO = 8


def _strip(rows, half):
    return pl.BlockSpec((rows, CONV_STRIP), lambda j: (0, j + half * N_STRIPS))


def _fill_padded(pad_ref, src_ref, halo, S):
    pad_ref[0:halo, :] = jnp.zeros((halo, CONV_STRIP), F32)
    pad_ref[halo + S:2 * halo + S, :] = jnp.zeros((halo, CONV_STRIP), F32)
    pad_ref[halo:halo + S, :] = src_ref[...].astype(F32)


def _conv_gate_fwd(u, conv_w, conv_b, S):
    R = min(CONV_ROWS, S)

    def body(ug_ref, uv_ref, wg_ref, wv_ref, bg_ref, bv_ref, a_ref, gpad, vpad):
        _fill_padded(gpad, ug_ref, HALO, S)
        _fill_padded(vpad, uv_ref, HALO, S)
        wg, wv, bg, bv = wg_ref[...], wv_ref[...], bg_ref[...], bv_ref[...]

        def conv(pad_ref, r0, w, b):
            dn, mid, up = (pad_ref[pl.ds(r0 + HALO + d, R), :] for d in (-1, 0, 1))
            return dn * w[0:1, :] + mid * w[1:2, :] + up * w[2:3, :] + b

        def step(c, carry):
            r0 = pl.multiple_of(c * R, R)
            g = conv(gpad, r0, wg, bg)
            val = conv(vpad, r0, wv, bv)
            a_ref[pl.ds(r0, R), :] = (g * _sigmoid(g) * val).astype(BF16)
            return carry

        lax.fori_loop(0, S // R, step, 0)

    return _pcall(body, name="conv_gate_fwd", grid=(N_STRIPS,),
                  in_specs=[_strip(S, 0), _strip(S, 1), _strip(3, 0), _strip(3, 1), _strip(1, 0), _strip(1, 1)],
                  out_specs=_strip(S, 0), out_shape=jax.ShapeDtypeStruct((S, D_FF), BF16),
                  scratch_shapes=[pltpu.VMEM((S + 2 * HALO, CONV_STRIP), F32)] * 2,
                  dims=("parallel",))(u, u, conv_w, conv_w, conv_b, conv_b)


def _conv_gate_bwd(u, conv_w, conv_b, da, S):
    R = min(CONV_ROWS, S)
    n = R + 2 * HALO

    def body(ug_ref, uv_ref, wg_ref, wv_ref, bg_ref, bv_ref, da_ref, dug_ref, duv_ref, dwg_ref, dwv_ref, dbg_ref, dbv_ref,
             gpad, vpad, dapad):
        _fill_padded(gpad, ug_ref, 2 * HALO, S)
        _fill_padded(vpad, uv_ref, 2 * HALO, S)
        _fill_padded(dapad, da_ref, HALO, S)
        wg, wv, bg, bv = wg_ref[...], wv_ref[...], bg_ref[...], bv_ref[...]

        def conv(pad_ref, r0, w, b):
            dn, mid, up = (pad_ref[pl.ds(r0 + HALO + d, n), :] for d in (-1, 0, 1))
            return dn * w[0:1, :] + mid * w[1:2, :] + up * w[2:3, :] + b, mid[HALO:HALO + R]

        def conv_bwd(duc, u_mid, w, r0, du_ref):
            dn, mid, up = pltpu.roll(duc, 1, axis=0)[HALO:HALO + R], duc[HALO:HALO + R], pltpu.roll(duc, n - 1, axis=0)[HALO:HALO + R]
            du_ref[pl.ds(r0, R), :] = (up * w[0:1, :] + mid * w[1:2, :] + dn * w[2:3, :]).astype(BF16)
            dw = jnp.concatenate([jnp.sum(up * u_mid, axis=0, keepdims=True), jnp.sum(mid * u_mid, axis=0, keepdims=True),
                                  jnp.sum(dn * u_mid, axis=0, keepdims=True)], axis=0)
            return dw, jnp.sum(mid, axis=0, keepdims=True)

        def step(c, carry):
            dw_g, db_g, dw_v, db_v = carry
            r0 = pl.multiple_of(c * R, R)
            g, ug_mid = conv(gpad, r0, wg, bg)
            val, uv_mid = conv(vpad, r0, wv, bv)
            da_ext = dapad[pl.ds(r0, n), :]
            sg = _sigmoid(g)
            ddw_v, ddb_v = conv_bwd(da_ext * (g * sg), uv_mid, wv, r0, duv_ref)
            ddw_g, ddb_g = conv_bwd(da_ext * val * (sg * (1.0 + g * (1.0 - sg))), ug_mid, wg, r0, dug_ref)
            return dw_g + ddw_g, db_g + ddb_g, dw_v + ddw_v, db_v + ddb_v

        z3, z1 = jnp.zeros((3, CONV_STRIP), F32), jnp.zeros((1, CONV_STRIP), F32)
        dwg_ref[...], dbg_ref[...], dwv_ref[...], dbv_ref[...] = lax.fori_loop(0, S // R, step, (z3, z1, z3, z1))

    half = lambda r, dt: (_strip(r, 0), jax.ShapeDtypeStruct((r, D_FF), dt))
    outs = [half(S, BF16), half(S, BF16), half(3, F32), half(3, F32), half(1, F32), half(1, F32)]
    return _pcall(
        body, name="conv_gate_bwd", grid=(N_STRIPS,),
        in_specs=[_strip(S, 0), _strip(S, 1), _strip(3, 0), _strip(3, 1), _strip(1, 0), _strip(1, 1), _strip(S, 0)],
        out_specs=[o[0] for o in outs], out_shape=[o[1] for o in outs],
        scratch_shapes=[pltpu.VMEM((S + 4 * HALO, CONV_STRIP), F32)] * 2 + [pltpu.VMEM((S + 2 * HALO, CONV_STRIP), F32)],
        dims=("parallel",))(u, u, conv_w, conv_w, conv_b, conv_b, da)


MESH = pl.DeviceIdType.MESH
ANY = pl.BlockSpec(memory_space=pl.ANY)


def _place():
    return lax.axis_index("x"), lax.axis_index("y"), lax.axis_index("c")


def _gather2(ins, outs, sems, finish):
    send_sems, recv_sems, local_sems = sems
    n_arr = len(ins)
    x, y, c = _place()
    me, sibling = (x, y, c), (x, y, 1 - c)
    chips = [(1 - x, y), (x, 1 - y), (1 - x, 1 - y)]

    def slot(a, p):
        return outs[a].at[4 * p[0] + 2 * p[1] + p[2]]

    def copy(a, k, block, to, src=None):
        return pltpu.make_async_remote_copy(
            src_ref=slot(a, block) if src is None else src, dst_ref=slot(a, block),
            send_sem=send_sems.at[a, k], recv_sem=recv_sems.at[a, k], device_id=to, device_id_type=MESH)

    mine = [pltpu.make_async_copy(ins[a], slot(a, me), local_sems.at[a]) for a in range(n_arr)]
    first = []
    for a in range(n_arr):
        first.append(copy(a, 0, me, sibling, src=ins[a]))
        first += [copy(a, 1 + j, me, (*chip, c), src=ins[a]) for j, chip in enumerate(chips)]
    if not finish:
        for cp in mine + first:
            cp.start()
        return
    passed = []
    for j, chip in enumerate(chips):
        for a in range(n_arr):
            copy(a, 1 + j, (*chip, c), me).wait_recv()
            cp = copy(a, 4 + j, (*chip, c), sibling)
            cp.start()
            passed.append(cp)
    for a in range(n_arr):
        copy(a, 0, sibling, me).wait_recv()
        for j, chip in enumerate(chips):
            copy(a, 4 + j, (*chip, 1 - c), me).wait_recv()
    for cp in first + passed:
        cp.wait_send()
    for cp in mine:
        cp.wait()


def _xchg_out_shapes(stacked, replicated):
    return ([jax.ShapeDtypeStruct(s.shape, s.dtype) for s in stacked]
            + [jax.ShapeDtypeStruct((N_DEV, *r.shape), r.dtype) for r in replicated])


def _xchg_sems(n_arr):
    return [pltpu.SemaphoreType.DMA((n_arr, 7)), pltpu.SemaphoreType.DMA((n_arr, 7)), pltpu.SemaphoreType.DMA((n_arr,))]


def _xchg_copies(ins, outs, sems, n_st, with_recv):
    send_sems, recv_sems, local_sems = sems
    n_arr = len(ins)
    x, y, c = _place()
    me = 4 * x + 2 * y + c

    def src(a, idx):
        return ins[a].at[idx] if a < n_st else ins[a]

    mine = [pltpu.make_async_copy(src(a, me), outs[a].at[me], local_sems.at[a]) for a in range(n_arr)]
    pairs = []
    for k in range(1, N_DEV):
        px, py, pc = x ^ (k >> 2), y ^ ((k >> 1) & 1), c ^ (k & 1)
        peer = 4 * px + 2 * py + pc
        for a in range(n_arr):
            sems_k = dict(send_sem=send_sems.at[a, k - 1], recv_sem=recv_sems.at[a, k - 1], device_id_type=MESH)
            send = pltpu.make_async_remote_copy(src_ref=src(a, peer), dst_ref=outs[a].at[me], device_id=(px, py, pc), **sems_k)
            recv = None
            if with_recv:
                recv = pltpu.make_async_remote_copy(src_ref=src(a, peer), dst_ref=outs[a].at[peer], device_id=(x, y, c), **sems_k)
            pairs.append((send, recv))
    return mine, pairs


def _xchg_start(ins, outs, sems, n_st):
    mine, pairs = _xchg_copies(ins, outs, sems, n_st, False)
    for cp in mine:
        cp.start()
    for send, _ in pairs:
        send.start()


def _xchg_finish(ins, outs, sems, n_st):
    mine, pairs = _xchg_copies(ins, outs, sems, n_st, True)
    for _, recv in pairs:
        recv.wait_recv()
    for send, _ in pairs:
        send.wait_send()
    for cp in mine:
        cp.wait()


def _exchange(stacked, replicated, name):
    _, landed = _pcall(lambda: None, name=name, grid=(), in_specs=[], out_specs=[], out_shape=[], comm=(stacked, replicated))()
    return landed


HBM = pl.BlockSpec(memory_space=pltpu.HBM)
SEMS = pl.BlockSpec(memory_space=pltpu.SEMAPHORE)
SIDE_EFFECT = pltpu.SideEffectType.DATAFLOW_SIDE_EFFECTING


N_SPLIT_SEMS = 2 * (N_DEV - 1)


def _split_copies(src, land, sems, with_recv):
    x, y, c = _place()
    me = 4 * x + 2 * y + c
    pairs = []
    for k in range(1, N_DEV):
        px, py, pc = x ^ (k >> 2), y ^ ((k >> 1) & 1), c ^ (k & 1)
        peer = 4 * px + 2 * py + pc
        sems_k = dict(send_sem=sems[k - 1], recv_sem=sems[N_DEV - 1 + k - 1], device_id_type=MESH)
        send = pltpu.make_async_remote_copy(src_ref=src.at[peer], dst_ref=land.at[me], device_id=(px, py, pc), **sems_k)
        recv = None
        if with_recv:
            recv = pltpu.make_async_remote_copy(src_ref=src.at[peer], dst_ref=land.at[peer], device_id=(x, y, c), **sems_k)
        pairs.append((send, recv))
    return pairs


def _exchange_start(stacked, name):
    def body(src, land, *rest):
        for send, _ in _split_copies(src, land, rest[:N_SPLIT_SEMS], False):
            send.start()
        rest[-1][...] = jnp.zeros_like(rest[-1])

    shape = pltpu.HBM(stacked.shape, stacked.dtype)
    res = pl.pallas_call(
        body, name=name, in_specs=[HBM, HBM],
        out_shape=(*[pltpu.SemaphoreType.DMA(())] * N_SPLIT_SEMS, shape, shape, jax.ShapeDtypeStruct((8, 128), F32)),
        out_specs=(*[SEMS] * N_SPLIT_SEMS, HBM, HBM, pl.BlockSpec(memory_space=pltpu.VMEM)),
        input_output_aliases={0: N_SPLIT_SEMS, 1: N_SPLIT_SEMS + 1},
        compiler_params=pltpu.CompilerParams(has_side_effects=SIDE_EFFECT),
    )(pltpu.with_memory_space_constraint(stacked, pltpu.HBM),
      pltpu.with_memory_space_constraint(lax.empty(stacked.shape, stacked.dtype), pltpu.HBM))
    return res[:N_SPLIT_SEMS], res[N_SPLIT_SEMS], res[N_SPLIT_SEMS + 1], res[-1]


def _exchange_wait(sems, src, land, after, name):
    def body(src_ref, land_ref, *rest):
        for send, recv in _split_copies(src_ref, land_ref, rest[:N_SPLIT_SEMS], True):
            send.wait_send()
            recv.wait_recv()

    shape = pltpu.HBM(src.shape, src.dtype)
    return pl.pallas_call(
        body, name=name, in_specs=[HBM, HBM, *[SEMS] * N_SPLIT_SEMS, ANY],
        out_shape=(shape, shape), out_specs=(HBM, HBM), input_output_aliases={0: 0, 1: 1},
        compiler_params=pltpu.CompilerParams(has_side_effects=SIDE_EFFECT))(src, land, *sems, after)[1]


def _adamw(parts, w, m, v, name):
    _, R, C = w.shape
    tr = R if R <= 512 else max(t for t in range(16, 513, 16) if R % t == 0)
    pr = tr if parts.shape[1] == R else -(-R // 16) * 16
    assert pr == tr or tr == R

    def body(p_ref, w_ref, m_ref, v_ref, g_ref, d_ref, nm_ref, nv_ref):
        g = p_ref[0].astype(F32)[:tr]
        for s in range(1, N_DEV):
            g = g + p_ref[s].astype(F32)[:tr]
        m2 = ADAM_B1 * m_ref[0] + (1.0 - ADAM_B1) * g
        v2 = ADAM_B2 * v_ref[0] + (1.0 - ADAM_B2) * (g * g)
        m_hat = m2 / (1.0 - ADAM_B1 ** ADAM_STEP)
        v_hat = v2 / (1.0 - ADAM_B2 ** ADAM_STEP)
        g_ref[0] = g
        d_ref[0] = -ADAM_LR * (m_hat / (jnp.sqrt(v_hat) + ADAM_EPS) + ADAM_WD * w_ref[0])
        nm_ref[0] = m2
        nv_ref[0] = v2

    blk = pl.BlockSpec((1, tr, C), lambda i: (0, i, 0))
    return _pcall(body, name=name, grid=(R // tr,),
                  in_specs=[pl.BlockSpec((N_DEV, pr, C), lambda i: (0, i, 0)), blk, blk, blk],
                  out_specs=[blk] * 4, out_shape=[jax.ShapeDtypeStruct((1, R, C), F32)] * 4,
                  dims=("parallel",))(parts, w, m, v)


def _t5_bucket(rel):
    nb = NUM_BUCKETS // 2
    max_exact = nb // 2
    base = (rel > 0).astype(jnp.int32) * nb
    n = jnp.abs(rel)
    nf = jnp.maximum(n, 1).astype(jnp.float32)
    large = max_exact + (jnp.log(nf / max_exact) / math.log(MAX_DISTANCE / max_exact) * (nb - max_exact)).astype(jnp.int32)
    large = jnp.minimum(large, nb - 1)
    return base + jnp.where(n < max_exact, n, large)


def _unstack_cols(g):
    return jnp.transpose(g, (1, 0, 2)).reshape(g.shape[1], N_DEV * g.shape[2])


def _stack_cols(w, n=N_DEV):
    R = w.shape[0]
    return jnp.transpose(w.reshape(R, n, w.shape[1] // n), (1, 0, 2))


def _stack_halves(g, v):
    return jnp.concatenate([_stack_cols(g, N_DEV // 2), _stack_cols(v, N_DEV // 2)], axis=0)


def kernel(x, positions, norm1_g, w_in, q_a_norm_g, w_q_b, kv_a_norm_g, w_kv_b, rel_bias, sinks, w_out, norm2_g, w_up, conv_w, conv_b, w_down, final_norm_g, loss_target, m_norm1_g, m_w_in, m_q_a_norm_g, m_w_q_b, m_kv_a_norm_g, m_w_kv_b, m_rel_bias, m_sinks, m_w_out, m_norm2_g, m_w_up, m_conv_w, m_conv_b, m_w_down, m_final_norm_g, v_norm1_g, v_w_in, v_q_a_norm_g, v_w_q_b, v_kv_a_norm_g, v_w_kv_b, v_rel_bias, v_sinks, v_w_out, v_norm2_g, v_w_up, v_conv_w, v_conv_b, v_w_down, v_final_norm_g):
    S = x.shape[1]
    x = x[0]
    target = loss_target[0]
    TM = 256

    tr = lambda w: jnp.swapaxes(w, 1, 2)
    (h1,), (g_in,) = _rowwise(lambda a, g: (_rms(a, g),), "norm1_gather", S, TM, [_rows(x), _whole(norm1_g)], [("rows", D_MODEL, BF16)],
                              gather=[tr(w_in)[0].astype(BF16)])
    late_weights = [w_out[0].astype(BF16), tr(w_up)[0].astype(BF16), conv_w[0]]
    wi = g_in.reshape(W_IN_COLS, D_MODEL)
    c0, c1, c2, c3, c4, c5 = (sum(W_IN_SIZES[:i + 1]) for i in range(6))
    w_in_pt = jnp.concatenate([wi[c4:c5], wi[c5:], wi[c1:c2], wi[:c0], wi[c2:c3], wi[c3:c4],
                               wi[c0:c0 + KV_LORA], wi[c0 + KV_LORA:c1], jnp.zeros((64, D_MODEL), BF16)], axis=0)

    half = QK_ROPE // 2
    inv_freq = ROPE_THETA ** (-jnp.arange(half, dtype=F32) / half)
    ang = positions.astype(F32)[:, None] * inv_freq[None, :]
    cos, sin = jnp.cos(ang), jnp.sin(ang)
    zeros64 = jnp.zeros((S, 128 - QK_ROPE), F32)
    cos_p = jnp.concatenate([cos, cos, zeros64], axis=1)
    sin_fwd, sin_back = jnp.concatenate([-sin, sin, zeros64], axis=1), jnp.concatenate([sin, -sin, zeros64], axis=1)
    qa = jnp.arange(Q_BLOCK, dtype=jnp.int32)[:, None]
    kc = jnp.arange(SPAN, dtype=jnp.int32)[None, :]
    rel = (kc - WINDOW - qa).T
    in_band = (jnp.abs(rel) <= WINDOW).astype(F32).reshape(1, Q_BLOCK * SPAN)
    onehot_t = (_t5_bucket(rel).reshape(1, Q_BLOCK * SPAN) == jnp.arange(NUM_BUCKETS, dtype=jnp.int32)[:, None]).astype(F32)
    bias_t = _bias_table(rel_bias.T, onehot_t, in_band).reshape(H_B, SPAN, Q_BLOCK)
    sinks_b = jnp.broadcast_to(sinks.reshape(H_B, 1), (H_B, Q_BLOCK))

    proj, (g_qb, g_kvb) = _matmul(h1, w_in_pt, "nt", BF16, "proj", comm=([], [tr(w_q_b)[0].astype(BF16), w_kv_b[0].astype(BF16)]))
    wq = g_qb.reshape(H_A, QK_HEAD, Q_LORA)
    w_qb_pt = jnp.concatenate([wq[:, :QK_NOPE].reshape(H_A * QK_NOPE, Q_LORA),
                               jnp.pad(wq[:, QK_NOPE:], ((0, 0), (0, 128 - QK_ROPE), (0, 0))).reshape(H_A * 128, Q_LORA)], axis=0)
    w_kvb = _unstack_cols(g_kvb)

    def lat_fn(qlat, ckv, kr, gq, gkv, cs, sn):
        return _rms(qlat, gq), _rms(ckv, gkv), _rope128(kr, cs, sn)

    qn, ckvn, k_rope = _rowwise(lat_fn, "latents", S, TM,
                                [_rows(proj, 256, PROJ_QLAT), _rows(proj, 128, PROJ_CKV), _rows(proj, 128, PROJ_KROPE),
                                 _whole(q_a_norm_g), _whole(kv_a_norm_g), _rows(cos_p), _rows(sin_fwd)],
                                [("rows", Q_LORA, BF16), ("rows", KV_LORA, BF16), ("rows", 128, BF16)])
    def q_heads_fn(q, cs, sn):
        q = q * MLA_PRESCALE
        return (jnp.concatenate([jnp.concatenate([q[:, 128 * h:128 * (h + 1)], _rope128(q[:, 128 * (H_A + h):128 * (H_A + h + 1)], cs, sn)],
                                                 axis=1)[None] for h in range(H_A)], axis=0),)

    (q_full,) = _matmul(qn, w_qb_pt, "nt", None, "q_up_heads", tm=512, tn=2048,
                        epi=(q_heads_fn, [_rows(cos_p), _rows(sin_fwd)], [("heads", H_A, HEAD_PAD, BF16)]))

    def k_heads_fn(kvf, kr):
        return kvf, jnp.concatenate([jnp.concatenate([kvf[:, 256 * h:256 * h + QK_NOPE], kr], axis=1)[None] for h in range(H_A)], axis=0)

    kv, k_full = _matmul(ckvn, w_kvb, "nn", None, "kv_up_heads", tm=512, tn=2048,
                         epi=(k_heads_fn, [_rows(k_rope)], [("rows", H_A * (QK_NOPE + V_DIM), BF16), ("heads", H_A, HEAD_PAD, BF16)]))
    (o_a, lse), (g_out, g_up, g_cw) = _mla_fwd(q_full, k_full, kv, S, comm=([], late_weights))
    w_out_f = g_out.reshape(D_MODEL, D_MODEL)
    w_up_t = g_up.reshape(2 * D_FF, D_MODEL)
    conv_w_f = _unstack_cols(g_cw)

    (o_b, mixed), (g_down,) = _win_fwd(proj, o_a, bias_t, sinks_b, S, comm=([], [w_down[0].astype(BF16)]))
    w_down_f = g_down.reshape(D_FF, D_MODEL)

    x1, h2 = _matmul(mixed, w_out_f, "nn", None, "out_proj", residual=x, tm=512,
                     epi=(lambda a, g: (a, _rms(a, g)), [_whole(norm2_g)], [("rows", D_MODEL, F32), ("rows", D_MODEL, BF16)]))
    u = _matmul(h2, w_up_t, "nt", BF16, "ffn_up", tn=1408)
    act = _conv_gate_fwd(u, conv_w_f, conv_b, S)

    def final_fn(a, g, t):
        err = _rms(a, g) - t
        loss = 0.5 * jnp.sum(jnp.mean(err * err, axis=-1, keepdims=True), axis=0, keepdims=True)
        dx, dg = _rms_bwd(err * (1.0 / D_MODEL), a, g)
        return dx, dx, dg, jnp.broadcast_to(loss, (1, 128))

    gfin = final_norm_g.reshape(1, D_MODEL)
    dx2, dx2_b, d_gfin, loss_row = _matmul(
        act, w_down_f, "nn", None, "ffn_down_loss", residual=x1, tm=512,
        epi=(final_fn, [_whole(gfin), _rows(target)],
             [("rows", D_MODEL, F32), ("rows", D_MODEL, BF16), ("acc", 1, D_MODEL), ("acc", 1, 128)]))
    d_act = _matmul(dx2_b, w_down_f, "nt", BF16, "ffn_down_dx", tn=1408)
    d_w_down = _matmul(act, dx2_b, "tn", BF16, "ffn_down_dw")
    du_g, du_v, dcw_g, dcw_v, dcb_g, dcb_v = _conv_gate_bwd(u, conv_w_f, conv_b, d_act, S)
    d_conv_b = jnp.concatenate([dcb_g, dcb_v], axis=1)

    def norm_bwd_fn(dh, a, g, dres):
        dx, dg = _rms_bwd(dh, a, g)
        dx = dx + dres
        return dx, dx, dg

    dx1, dx1_b, d_g2 = _matmul(du_g, w_up_t, "nn", None, "ffn_up_dx_norm2_bwd", tm=256, a2=du_v,
                               epi=(norm_bwd_fn, [_rows(x1), _whole(norm2_g), _rows(dx2)],
                                    [("rows", D_MODEL, F32), ("rows", D_MODEL, BF16), ("acc", 1, D_MODEL)]))
    d_w_up_t = _matmul(du_g, h2, "tn", BF16, "ffn_up_dw", tm=256, a2=du_v)
    d_w_out = _matmul(mixed, dx1_b, "tn", BF16, "out_proj_dw", tm=512)

    def gate_bwd_fn(dm, ga, gb, oa, ob):
        sa, sb = _sigmoid(ga), _sigmoid(gb)
        return jnp.concatenate([dm * oa * sa * (1.0 - sa), dm * ob * sb * (1.0 - sb)], axis=1), dm * sa, dm * sb

    d_proj, do_a, do_b = _matmul(
        dx1_b, w_out_f, "nt", None, "out_proj_dx_gate_bwd", tm=512,
        epi=(gate_bwd_fn, [_rows(proj, 1024, PROJ_GA), _rows(proj, 1024, PROJ_GB), _rows(o_a), _rows(o_b)],
             [("cols", 2 * D_MODEL, 0, PROJ_P, BF16), ("rows", D_MODEL, BF16), ("rows", D_MODEL, BF16)]))

    d_proj, dk_acc, dv_acc, d_bias, d_sinks_g = _win_bwd(proj, bias_t, sinks_b, do_b, d_proj, S)
    d_sinks = d_sinks_g.reshape(1, H_B)

    early = [d_w_out.reshape(N_DEV, D_MODEL // N_DEV, D_MODEL), d_w_up_t.reshape(N_DEV, 2 * D_FF // N_DEV, D_MODEL),
             d_w_down.reshape(N_DEV, D_FF // N_DEV, D_MODEL), _stack_halves(dcw_g, dcw_v)]
    (dq_nope, dq_rope, dkv, dkr_heads), recv_early = _mla_bwd(q_full, k_full, kv, do_a, o_a, lse, cos_p, sin_back, S, comm=(early, []))

    d_qn = _matmul(dq_nope, w_qb_pt, "nn", F32, "q_up_dx", a2=dq_rope)
    d_ckvn = _matmul(dkv, w_kvb, "nt", F32, "kv_up_dx")

    def lat_bwd_fn(dqn, dckvn, dkr_h, cs, sn, qlat, ckv, gq, gkv, dkb, dvb):
        dql, dgq = _rms_bwd(dqn, qlat, gq)
        dck, dgkv = _rms_bwd(dckvn, ckv, gkv)
        dkr = dkr_h[0]
        for h in range(1, H_A):
            dkr = dkr + dkr_h[h]
        r1, r2 = _rope_bwd(dkr[:, :half], dkr[:, half:], cs, sn)
        tail = jnp.concatenate([dql, dkb, dvb, dck, r1, r2, jnp.zeros_like(dkr)], axis=1)
        return tail, dgq, dgkv

    shifted = lambda arr: (arr, lambda tm: pl.BlockSpec((tm, arr.shape[1]), lambda i, *_: (i + WINDOW // tm, 0)))
    TL = min(128, S)
    d_proj, d_gq, d_gkv = _rowwise(lat_bwd_fn, "latents_bwd", S, TL,
                                   [_rows(d_qn), _rows(d_ckvn), _heads(dkr_heads), _rows(cos), _rows(sin), _rows(proj, 256, PROJ_QLAT), _rows(proj, 128, PROJ_CKV),
                                    _whole(q_a_norm_g), _whole(kv_a_norm_g), shifted(dk_acc), shifted(dv_acc)],
                                   [("cols", 1024, 3, PROJ_P, BF16), ("acc", 1, Q_LORA), ("acc", 1, KV_LORA)], into=(d_proj, 0))
    dp = _matmul(d_proj, h1, "tn", BF16, "proj_dw", tm=512)

    late = jnp.concatenate([dp[3072:3328], dp[3840:3968], dp[3968:4032], dp[2048:3072], dp[3328:3584],
                            dp[3584:3840], dp[0:1024], dp[1024:2048]], axis=0).reshape(N_DEV, W_IN_COLS // N_DEV, D_MODEL)
    late_sems, late_src, late_land, started = _exchange_start(late, "late_grads_start")

    def norm1_bwd_fn(dh, a, g, dres):
        dx, dg = _rms_bwd(dh, a, g)
        return dx + dres, dg

    grad_x, d_g1 = _matmul(
        d_proj, w_in_pt, "nn", None, "proj_dx_norm1_bwd", tm=512,
        epi=(norm1_bwd_fn, [_rows(x), _whole(norm1_g + started[:1, :1]), _rows(dx1)], [("rows", D_MODEL, F32), ("acc", 1, D_MODEL)]))

    transposed = ("w_in", "w_q_b", "w_up")
    ready_names = ["w_out", "w_up", "w_down", "conv_w"]
    ready_wmv = [(w_out, m_w_out, v_w_out), (tr(w_up), tr(m_w_up), tr(v_w_up)), (w_down, m_w_down, v_w_down), (conv_w, m_conv_w, v_conv_w)]
    recv_early, _ = lax.optimization_barrier((list(recv_early), started))
    big = {n: _adamw(r, *wmv, "adamw_" + n) for n, r, wmv in zip(ready_names, recv_early, ready_wmv)}

    (d_bias, dq_nope_l, dkv_l), _ = lax.optimization_barrier(((d_bias, dq_nope, dkv), started))
    d_rel_bias = _bias_table_bwd(d_bias.reshape(H_B, Q_BLOCK * SPAN), onehot_t).T
    d_w_qb_pt = _matmul(dq_nope_l, qn, "tn", BF16, "q_up_dw", tm=512, a2=dq_rope)
    d_w_kvb = _matmul(ckvn, dkv_l, "tn", BF16, "kv_up_dw", tn=2048)
    d_w_qb_t = jnp.concatenate([d_w_qb_pt[:H_A * QK_NOPE].reshape(H_A, QK_NOPE, Q_LORA),
                                d_w_qb_pt[H_A * QK_NOPE:].reshape(H_A, 128, Q_LORA)[:, :QK_ROPE]], axis=1)

    after = lax.optimization_barrier([big[n][0] for n in ready_names] + [d_rel_bias, d_w_qb_t, d_w_kvb])
    landed = _exchange_wait(late_sems, late_src, late_land, after[-1], "late_grads_wait")
    me = 4 * lax.axis_index("x") + 2 * lax.axis_index("y") + lax.axis_index("c")
    landed = lax.dynamic_update_slice_in_dim(landed, lax.dynamic_slice_in_dim(late, me, 1, axis=0), me, axis=0)
    big["w_in"] = _adamw(landed, tr(w_in), tr(m_w_in), tr(v_w_in), "adamw_w_in")

    small_parts = [d_g1, d_gq, d_gkv, d_rel_bias.reshape(1, NUM_BUCKETS * H_B), d_sinks, d_g2, d_conv_b, d_gfin, loss_row[:, :1]]
    small = jnp.concatenate(small_parts, axis=1)
    n_small = small.shape[1]
    pad = (-n_small) % 128
    small = jnp.pad(small, ((0, 0), (0, pad)))
    small, _ = lax.optimization_barrier((small, [landed, *after]))
    recv_qb, recv_kvb, recv_small = _exchange([after[-2], _stack_cols(after[-1])], [small], "exchange_small_grads")
    big["w_q_b"] = _adamw(recv_qb, tr(w_q_b), tr(m_w_q_b), tr(v_w_q_b), "adamw_w_q_b")
    big["w_kv_b"] = _adamw(recv_kvb, w_kv_b, m_w_kv_b, v_w_kv_b, "adamw_w_kv_b")

    def flat(a):
        return a.reshape(1, -1)

    small_w = [norm1_g, q_a_norm_g, kv_a_norm_g, rel_bias, sinks, norm2_g, conv_b, final_norm_g]
    small_m = [m_norm1_g, m_q_a_norm_g, m_kv_a_norm_g, m_rel_bias, m_sinks, m_norm2_g, m_conv_b, m_final_norm_g]
    small_v = [v_norm1_g, v_q_a_norm_g, v_kv_a_norm_g, v_rel_bias, v_sinks, v_norm2_g, v_conv_b, v_final_norm_g]
    cat = lambda parts: jnp.pad(jnp.concatenate([flat(a) for a in parts], axis=1), ((0, 0), (0, pad + 1)))[None]
    sm = _adamw(recv_small, cat(small_w), cat(small_m), cat(small_v), "adamw_small")

    loss = sm[0][0, 0, n_small - 1]
    order =["norm1_g", "w_in", "q_a_norm_g", "w_q_b", "kv_a_norm_g", "w_kv_b", "rel_bias", "sinks", "w_out", "norm2_g", "w_up",
             "conv_w", "conv_b", "w_down", "final_norm_g"]
    small_names = ["norm1_g", "q_a_norm_g", "kv_a_norm_g", "rel_bias", "sinks", "norm2_g", "conv_b", "final_norm_g"]
    offs, o = {}, 0
    for n, a in zip(small_names, small_w):
        offs[n] = (o, a.size, a.shape)
        o += a.size
    outs = [loss, grad_x[None]]
    for kind in range(4):
        for n in order:
            if n in big:
                outs.append(tr(big[n][kind]) if n in transposed else big[n][kind])
            else:
                o, size, shape = offs[n]
                outs.append(sm[kind][0, 0, o:o + size].reshape(shape))
    return tuple(outs)
```

```python
import math

import jax
import jax.numpy as jnp
from jax import lax
from jax.experimental import pallas as pl
from jax.experimental.pallas import tpu as pltpu

F32 = jnp.float32
BF16 = jnp.bfloat16

N_DEV = 8
D_MODEL = 1024
EPS = 1e-6
H_A, QK_NOPE, QK_ROPE, V_DIM, Q_LORA, KV_LORA = 8, 128, 64, 128, 256, 128
QK_HEAD = QK_NOPE + QK_ROPE
HEAD_PAD = 256
ROPE_THETA = 10000.0
H_B, KV_B, GROUP, HD_B, WINDOW, Q_BLOCK = 16, 4, 4, 64, 128, 128
SPAN = Q_BLOCK + 2 * WINDOW
NUM_BUCKETS, MAX_DISTANCE = 32, 128
D_FF = 2816
ADAM_LR, ADAM_B1, ADAM_B2, ADAM_EPS, ADAM_WD, ADAM_STEP = 0.001, 0.9, 0.999, 1e-08, 0.01, 10

W_IN_SIZES = (Q_LORA, KV_LORA + QK_ROPE, H_B * HD_B, KV_B * HD_B, KV_B * HD_B, D_MODEL, D_MODEL)
W_IN_COLS = sum(W_IN_SIZES)
PROJ_P = 4096
PROJ_GA, PROJ_GB, PROJ_QB, PROJ_QLAT, PROJ_KB, PROJ_VB, PROJ_CKV, PROJ_KROPE = 0, 1, 2, 12, 13, 14, 30, 31

VMEM_LIMIT = 56 * 1024 * 1024

NN = (((1,), (0,)), ((), ()))
NT = (((1,), (1,)), ((), ()))
TN = (((0,), (0,)), ((), ()))


def _pcall(body, *, name, grid, in_specs, out_specs, out_shape, scratch_shapes=(), dims=None, comm=None, aliases=None, two_level=False):
    if comm is None:
        params = pltpu.CompilerParams(dimension_semantics=dims, vmem_limit_bytes=VMEM_LIMIT)
        return pl.pallas_call(body, name=name, grid=grid, in_specs=in_specs, out_specs=out_specs, out_shape=out_shape,
                              scratch_shapes=list(scratch_shapes), input_output_aliases=aliases or {}, compiler_params=params)
    assert not aliases
    stacked, replicated = comm
    arrs = [*stacked, *replicated]
    n_st, n_arr = len(stacked), len(arrs)
    single = not isinstance(out_specs, (list, tuple))
    o_specs, o_shape = ([out_specs], [out_shape]) if single else (list(out_specs), list(out_shape))
    n_in, n_out = len(in_specs), len(o_specs)

    def wrapped(*refs):
        c_in = refs[n_in:n_in + n_arr]
        c_out = refs[n_in + n_arr + n_out:n_in + 2 * n_arr + n_out]
        sems = refs[len(refs) - 3:]
        own = (*refs[:n_in], *refs[n_in + n_arr:n_in + n_arr + n_out], *refs[n_in + 2 * n_arr + n_out:len(refs) - 3])
        if two_level:
            assert n_st == 0
            start, finish = (lambda: _gather2(c_in, c_out, sems, False)), (lambda: _gather2(c_in, c_out, sems, True))
        else:
            start, finish = (lambda: _xchg_start(c_in, c_out, sems, n_st)), (lambda: _xchg_finish(c_in, c_out, sems, n_st))
        if not grid:
            start()
            finish()
            return
        first = last = None
        for d, n in enumerate(grid):
            pid = pl.program_id(d)
            first = (pid == 0) if first is None else first & (pid == 0)
            last = (pid == n - 1) if last is None else last & (pid == n - 1)

        pl.when(first)(start)
        body(*own)
        pl.when(last)(finish)

    params = pltpu.CompilerParams(dimension_semantics=("arbitrary",) * len(grid), vmem_limit_bytes=VMEM_LIMIT)
    call = pl.pallas_call(wrapped, name=name, grid=grid, in_specs=[*in_specs, *[ANY] * n_arr], out_specs=[*o_specs, *[ANY] * n_arr],
                          out_shape=[*o_shape, *_xchg_out_shapes(stacked, replicated)],
                          scratch_shapes=[*scratch_shapes, *_xchg_sems(n_arr)], compiler_params=params)

    def run(*args):
        res = call(*args, *arrs)
        outs, landed = res[:n_out], res[n_out:]
        return (outs[0] if single else outs), landed

    return run


def _dot(a, b, dn):
    return lax.dot_general(a, b, dn, preferred_element_type=F32)


def _tile(n, target):
    best = None
    for t in range(128, min(n, target) + 1, 128):
        if n % t == 0:
            best = t
    return n if best is None else best


def _matmul(a, b, mode, out_dtype, name, residual=None, tm=1024, tn=1024, comm=None, a2=None, epi=None):
    if mode == "nn":
        (M, K), N = a.shape, b.shape[1]
    elif mode == "nt":
        (M, K), N = a.shape, b.shape[0]
    else:
        (K, M), N = a.shape, b.shape[1]
    tm, tn = _tile(M, tm), _tile(N, tn)
    a_spec = pl.BlockSpec((K, tm), lambda i, j: (0, i)) if mode == "tn" else pl.BlockSpec((tm, K), lambda i, j: (i, 0))
    b_spec = pl.BlockSpec((tn, b.shape[1]), lambda i, j: (j, 0)) if mode == "nt" else pl.BlockSpec((K, tn), lambda i, j: (0, j))
    o_spec = pl.BlockSpec((tm, tn), lambda i, j: (i, j))
    in_specs, args = [a_spec, b_spec], [a, b]
    n1 = M // tm
    if a2 is not None and mode == "tn":
        assert M % tm == 0 and a2.shape[1] % tm == 0
        in_specs[0] = pl.BlockSpec((K, tm), lambda i, j: (0, jnp.minimum(i, n1 - 1)))
        in_specs.append(pl.BlockSpec((K, tm), lambda i, j: (0, jnp.maximum(i - n1, 0))))
        args.append(a2)
        M += a2.shape[1]
    elif a2 is not None:
        assert (mode == "nt" and K + a2.shape[1] == b.shape[1]) or (mode == "nn" and K + a2.shape[1] == b.shape[0])
        if mode == "nn":
            b_spec = in_specs[1] = pl.BlockSpec((b.shape[0], tn), lambda i, j: (0, j))
        in_specs.append(pl.BlockSpec((tm, a2.shape[1]), lambda i, j: (i, 0)))
        args.append(a2)
    if residual is not None:
        in_specs.append(o_spec)
        args.append(residual)
    n_mm = len(args)
    scratch = [pltpu.VMEM((tm, K), a.dtype)] if mode == "tn" else []
    if epi is None:
        out_specs, out_shape, is_acc = o_spec, jax.ShapeDtypeStruct((M, N), out_dtype), None
    else:
        assert tn == N
        fn, epi_ins, epi_outs = epi
        in_specs += [mk(tm) for _, mk in epi_ins]
        args += [arr for arr, _ in epi_ins]
        out_specs, out_shape, is_acc = _row_out_specs(epi_outs, M, tm)

    def body(*refs):
        a_ref, b_ref = refs[0], refs[1]
        n_out = 1 if epi is None else len(is_acc)
        out_refs = refs[len(args):len(args) + n_out]
        if mode == "tn":
            at_ref = refs[len(args) + n_out]

            first_col = pl.program_id(1) == 0
            from_a = first_col if a2 is None else first_col & (pl.program_id(0) < n1)

            @pl.when(from_a)
            def _():
                at_ref[...] = a_ref[...].T

            if a2 is not None:
                @pl.when(first_col & (pl.program_id(0) >= n1))
                def _():
                    at_ref[...] = refs[2][...].T

            acc = _dot(at_ref[...], b_ref[...], NN)
        elif a2 is not None and mode == "nt":
            acc = _dot(a_ref[...], b_ref[:, :K], NT) + _dot(refs[2][...], b_ref[:, K:], NT)
        elif a2 is not None:
            acc = _dot(a_ref[...], b_ref[:K, :], NN) + _dot(refs[2][...], b_ref[K:, :], NN)
        else:
            acc = _dot(a_ref[...], b_ref[...], NT if mode == "nt" else NN)
        if residual is not None:
            acc = acc + refs[n_mm - 1][...]
        if epi is None:
            out_refs[0][...] = acc.astype(out_dtype)
        else:
            _store_rows(out_refs, fn(acc, *[_load_f32(r) for r in refs[n_mm:len(args)]]), is_acc)

    return _pcall(body, name=name, grid=(M // tm, N // tn), in_specs=in_specs, out_specs=out_specs,
                  out_shape=out_shape, scratch_shapes=scratch,
                  dims=("arbitrary" if epi is not None else "parallel", "arbitrary"), comm=comm)(*args)


def _rows(arr, width=None, col=0):
    width = arr.shape[1] if width is None else width
    return (arr, lambda tm: pl.BlockSpec((tm, width), lambda i, *_: (i, col)))


def _heads(arr):
    return (arr, lambda tm: pl.BlockSpec((arr.shape[0], tm, arr.shape[2]), lambda i, *_: (0, i, 0)))


def _whole(arr):
    nd = arr.ndim
    return (arr, lambda tm: pl.BlockSpec(arr.shape, lambda i, *_: (0,) * nd))


def _row_out_specs(outs, n_rows, tm):
    out_specs, out_shape, is_acc = [], [], []
    for o in outs:
        if o[0] == "rows":
            out_specs.append(pl.BlockSpec((tm, o[1]), lambda i, *_: (i, 0)))
            out_shape.append(jax.ShapeDtypeStruct((n_rows, o[1]), o[2]))
        elif o[0] == "cols":
            out_specs.append(pl.BlockSpec((tm, o[1]), lambda i, *_, c=o[2]: (i, c)))
            out_shape.append(jax.ShapeDtypeStruct((n_rows, o[3]), o[4]))
        elif o[0] == "heads":
            out_specs.append(pl.BlockSpec((o[1], tm, o[2]), lambda i, *_: (0, i, 0)))
            out_shape.append(jax.ShapeDtypeStruct((o[1], n_rows, o[2]), o[3]))
        else:
            out_specs.append(pl.BlockSpec((o[1], o[2]), lambda i, *_: (0, 0)))
            out_shape.append(jax.ShapeDtypeStruct((o[1], o[2]), F32))
        is_acc.append(o[0] == "acc")
    return out_specs, out_shape, is_acc


def _load_f32(r):
    v = r[...]
    return v.astype(F32) if v.dtype == BF16 else v


def _store_rows(out_refs, vals, is_acc):
    for r, v, acc in zip(out_refs, vals, is_acc):
        if acc:
            @pl.when(pl.program_id(0) == 0)
            def _():
                r[...] = jnp.zeros_like(r)

            r[...] += v
        else:
            r[...] = v.astype(r.dtype)


def _rowwise(fn, name, n_rows, tm, ins, outs, upcast=True, into=None, gather=None):
    tm = min(tm, n_rows)
    assert n_rows % tm == 0
    in_specs = [mk(tm) for _, mk in ins]
    out_specs, out_shape, is_acc = _row_out_specs(outs, n_rows, tm)
    n_in = len(ins)
    args = [a for a, _ in ins]
    aliases = {}
    if into is not None:
        in_specs.append(ANY)
        args.append(into[0])
        aliases = {n_in: into[1]}

    def body(*refs):
        vals = fn(*[_load_f32(r) if upcast else r[...] for r in refs[:n_in]])
        _store_rows(refs[len(args):], vals, is_acc)

    return _pcall(body, name=name, grid=(n_rows // tm,), in_specs=in_specs, out_specs=out_specs,
                  out_shape=out_shape, dims=("arbitrary",), aliases=aliases,
                  comm=None if gather is None else ([], gather), two_level=True)(*args)


def _rms(x, g):
    r = lax.rsqrt(jnp.mean(x * x, axis=-1, keepdims=True) + EPS)
    return x * r * g


def _rms_bwd(dy, x, g):
    r = lax.rsqrt(jnp.mean(x * x, axis=-1, keepdims=True) + EPS)
    xhat = x * r
    dxhat = dy * g
    dx = r * (dxhat - xhat * jnp.mean(dxhat * xhat, axis=-1, keepdims=True))
    return dx, jnp.sum(dy * xhat, axis=0, keepdims=True)


def _rope_bwd(d1, d2, cos, sin):
    return d1 * cos + d2 * sin, d2 * cos - d1 * sin


def _rope128(x, cos_p, sin_p):
    lane = lax.broadcasted_iota(jnp.int32, x.shape, 1)
    swapped = jnp.where(lane < QK_ROPE // 2, pltpu.roll(x, 128 - QK_ROPE // 2, axis=1), pltpu.roll(x, QK_ROPE // 2, axis=1))
    return x * cos_p + swapped * sin_p


def _sigmoid(x):
    return 1.0 / (1.0 + jnp.exp(-x))


MLA_SCALE = 1.0 / math.sqrt(QK_HEAD)
MLA_PRESCALE = MLA_SCALE * math.log2(math.e)
MLA_TQ, MLA_KC = 1024, 1024


def _mla_fwd(q_full, k_full, kv, S, comm=None):
    tq, kc = min(2 * MLA_TQ, S), min(MLA_KC, S)

    def body(q_ref, k_ref, v_ref, o_ref, lse_ref):
        q = q_ref[0]
        m = jnp.full((tq, 1), -1e30, F32)
        l = jnp.zeros((tq, 1), F32)
        acc = jnp.zeros((tq, V_DIM), F32)
        for c in range(S // kc):
            s = _dot(q, k_ref[0, c * kc:(c + 1) * kc, :], NT)
            m_new = jnp.maximum(m, jnp.max(s, axis=-1, keepdims=True))
            alpha = jnp.exp2(m - m_new)
            p = jnp.exp2(s - m_new)
            l = alpha * l + jnp.sum(p, axis=-1, keepdims=True)
            acc = alpha * acc + _dot(p.astype(BF16), v_ref[c * kc:(c + 1) * kc, :], NN)
            m = m_new
        o_ref[...] = (acc / l).astype(BF16)
        lse_ref[0] = m + jnp.log2(l)

    return _pcall(
        body, name="mla_fwd", grid=(H_A, S // tq),
        in_specs=[pl.BlockSpec((1, tq, HEAD_PAD), lambda h, i: (h, i, 0)),
                  pl.BlockSpec((1, S, HEAD_PAD), lambda h, i: (h, 0, 0)),
                  pl.BlockSpec((S, V_DIM), lambda h, i: (0, 2 * h + 1))],
        out_specs=[pl.BlockSpec((tq, V_DIM), lambda h, i: (i, h)),
                   pl.BlockSpec((1, tq, 1), lambda h, i: (h, i, 0))],
        out_shape=[jax.ShapeDtypeStruct((S, H_A * V_DIM), BF16), jax.ShapeDtypeStruct((H_A, S, 1), F32)],
        dims=("parallel", "parallel"), comm=comm)(q_full, k_full, kv)


def _mla_bwd(q_full, k_full, kv, do_a, o_a, lse, cos_p, sin_back, S, comm=None):
    tq, kc = min(MLA_TQ, S), min(MLA_KC, S)

    def body(q_ref, k_ref, v_ref, do_ref, o_ref, lse_ref, cos_ref, sin_ref, dqn_ref, dqr_ref, dkv_out, dkr_out, dk_ref, dv_ref):
        @pl.when(pl.program_id(1) == 0)
        def _():
            dk_ref[...] = jnp.zeros_like(dk_ref)
            dv_ref[...] = jnp.zeros_like(dv_ref)

        q = q_ref[0]
        do = do_ref[...]
        lse_q = lse_ref[0]
        delta = jnp.sum(do.astype(F32) * o_ref[...].astype(F32), axis=-1, keepdims=True)
        dq = jnp.zeros((tq, HEAD_PAD), F32)
        for c in range(S // kc):
            k = k_ref[0, c * kc:(c + 1) * kc, :]
            v = v_ref[c * kc:(c + 1) * kc, :]
            p = jnp.exp2(_dot(q, k, NT) - lse_q)
            ds = (p * (_dot(do, v, NT) - delta)).astype(BF16)
            dq = dq + _dot(ds, k, NN)
            dk_ref[0, c * kc:(c + 1) * kc, :] += _dot(ds, q, TN)
            dv_ref[0, c * kc:(c + 1) * kc, :] += _dot(p.astype(BF16), do, TN)
        dq = dq * MLA_SCALE
        dqn_ref[...] = dq[:, :QK_NOPE].astype(BF16)
        dqr_ref[...] = _rope128(dq[:, QK_NOPE:], cos_ref[...], sin_ref[...]).astype(BF16)

        @pl.when(pl.program_id(1) == S // tq - 1)
        def _():
            dk = dk_ref[0] * math.log(2.0)
            dkv_out[...] = jnp.concatenate([dk[:, :QK_NOPE], dv_ref[0]], axis=1).astype(BF16)
            dkr_out[0] = dk[:, QK_NOPE:QK_HEAD].astype(BF16)

    return _pcall(
        body, name="mla_bwd", grid=(H_A, S // tq),
        in_specs=[pl.BlockSpec((1, tq, HEAD_PAD), lambda h, i: (h, i, 0)),
                  pl.BlockSpec((1, S, HEAD_PAD), lambda h, i: (h, 0, 0)),
                  pl.BlockSpec((S, V_DIM), lambda h, i: (0, 2 * h + 1)),
                  pl.BlockSpec((tq, V_DIM), lambda h, i: (i, h)),
                  pl.BlockSpec((tq, V_DIM), lambda h, i: (i, h)),
                  pl.BlockSpec((1, tq, 1), lambda h, i: (h, i, 0)),
                  pl.BlockSpec((tq, 128), lambda h, i: (i, 0)), pl.BlockSpec((tq, 128), lambda h, i: (i, 0))],
        out_specs=[pl.BlockSpec((tq, QK_NOPE), lambda h, i: (i, h)),
                   pl.BlockSpec((tq, 128), lambda h, i: (i, h)),
                   pl.BlockSpec((S, QK_NOPE + V_DIM), lambda h, i: (0, h)),
                   pl.BlockSpec((1, S, QK_ROPE), lambda h, i: (h, 0, 0))],
        out_shape=[jax.ShapeDtypeStruct((S, H_A * QK_NOPE), BF16), jax.ShapeDtypeStruct((S, H_A * 128), BF16),
                   jax.ShapeDtypeStruct((S, H_A * (QK_NOPE + V_DIM)), BF16), jax.ShapeDtypeStruct((H_A, S, QK_ROPE), BF16)],
        scratch_shapes=[pltpu.VMEM((1, S, HEAD_PAD), F32), pltpu.VMEM((1, S, V_DIM), F32)],
        dims=("parallel", "arbitrary"), comm=comm)(q_full, k_full, kv, do_a, o_a, lse, cos_p, sin_back)


WIN_SCALE = 1.0 / math.sqrt(HD_B)


WIN_PER_STEP = 4


def _win_specs(S):
    last, B = S // Q_BLOCK - 1, WIN_PER_STEP
    qspec = pl.BlockSpec((B * Q_BLOCK, H_B * HD_B), lambda i: (i, PROJ_QB))
    kspecs = [[pl.BlockSpec((Q_BLOCK, KV_B * HD_B), lambda i, d=d, c=c: (jnp.clip(B * i + d, 0, last), c)) for d in range(-1, B + 1)]
              for c in (PROJ_KB, PROJ_VB)]
    bias_spec = pl.BlockSpec((H_B, SPAN, Q_BLOCK), lambda i: (0, 0, 0))
    sink_spec = pl.BlockSpec((H_B, Q_BLOCK), lambda i: (0, 0))
    return qspec, kspecs, bias_spec, sink_spec


def _win_edge_ok(n, n_blk):
    row = lax.broadcasted_iota(jnp.int32, (SPAN, 1), 0)
    return jnp.logical_not(((n == 0) & (row < WINDOW)) | ((n == n_blk - 1) & (row >= SPAN - WINDOW)))


def _lanes4(pieces):
    return jnp.concatenate(pieces, axis=1)


def _win_probs(kg, q4t, bias_ref, sink_ref, g, edge_ok):
    bias4 = _lanes4([bias_ref[GROUP * g + j] for j in range(GROUP)])
    sink4 = _lanes4([sink_ref[GROUP * g + j:GROUP * g + j + 1, :] for j in range(GROUP)])
    s = jnp.where(edge_ok, _dot(kg, q4t, NN) + bias4, -1e30)
    m = jnp.maximum(jnp.max(s, axis=0, keepdims=True), sink4)
    p = jnp.exp(s - m)
    e_sink = jnp.exp(sink4 - m)
    inv_l = 1.0 / (jnp.sum(p, axis=0, keepdims=True) + e_sink)
    return p * inv_l, e_sink * inv_l


def _group_t(xt, g):
    return _lanes4([xt[HD_B * (GROUP * g + j):HD_B * (GROUP * g + j + 1), :] for j in range(GROUP)])


def _rows_of(ref, b):
    return ref[Q_BLOCK * b:Q_BLOCK * (b + 1), :]


def _win_fwd(proj, o_a, bias_t, sinks_b, S, comm=None):
    n_blk, B = S // Q_BLOCK, WIN_PER_STEP
    qspec, kspecs, bias_spec, sink_spec = _win_specs(S)
    rows = lambda col: pl.BlockSpec((B * Q_BLOCK, H_B * HD_B), lambda i: (i, col))

    def body(q_ref, *refs):
        k_refs, v_refs = refs[:B + 2], refs[B + 2:2 * B + 4]
        bias_ref, sink_ref, ga_ref, gb_ref, oa_ref, o_ref, mixed_ref = refs[2 * B + 4:]
        for b in range(B):
            edge_ok = _win_edge_ok(B * pl.program_id(0) + b, n_blk)
            k = jnp.concatenate([r[...] for r in k_refs[b:b + 3]], axis=0)
            vt = jnp.concatenate([r[...] for r in v_refs[b:b + 3]], axis=0).T
            qt = (_rows_of(q_ref, b).astype(F32) * WIN_SCALE).T.astype(BF16)
            parts = []
            for g in range(KV_B):
                p, _ = _win_probs(k[:, HD_B * g:HD_B * (g + 1)], _group_t(qt, g), bias_ref, sink_ref, g, edge_ok)
                o4t = _dot(vt[HD_B * g:HD_B * (g + 1), :], p.astype(BF16), NN)
                parts += [o4t[:, Q_BLOCK * j:Q_BLOCK * (j + 1)] for j in range(GROUP)]
            ob = jnp.concatenate(parts, axis=0).T
            o_ref[Q_BLOCK * b:Q_BLOCK * (b + 1), :] = ob.astype(BF16)
            ga, gb, oa = (_rows_of(r, b).astype(F32) for r in (ga_ref, gb_ref, oa_ref))
            mixed_ref[Q_BLOCK * b:Q_BLOCK * (b + 1), :] = (_sigmoid(ga) * oa + _sigmoid(gb) * ob).astype(BF16)

    return _pcall(body, name="win_fwd_mix", grid=(n_blk // B,),
                  in_specs=[qspec, *kspecs[0], *kspecs[1], bias_spec, sink_spec, rows(PROJ_GA), rows(PROJ_GB), rows(0)],
                  out_specs=[rows(0), rows(0)],
                  out_shape=[jax.ShapeDtypeStruct((S, H_B * HD_B), BF16)] * 2,
                  dims=("parallel",), comm=comm)(*[proj] * (2 * B + 5), bias_t, sinks_b, proj, proj, o_a)


def _win_bwd(proj, bias_t, sinks_b, do_b, d_proj, S):
    n_blk, B = S // Q_BLOCK, WIN_PER_STEP
    qspec, kspecs, bias_spec, sink_spec = _win_specs(S)

    def body(q_ref, *refs):
        k_refs, v_refs = refs[:B + 2], refs[B + 2:2 * B + 4]
        bias_ref, sink_ref, do_ref, _, dq_ref, dk_ref, dv_ref, dbias_ref, dsink_ref, dsink_acc = refs[2 * B + 4:]
        i = pl.program_id(0)

        @pl.when(i == 0)
        def _():
            dk_ref[...] = jnp.zeros_like(dk_ref)
            dv_ref[...] = jnp.zeros_like(dv_ref)
            dbias_ref[...] = jnp.zeros_like(dbias_ref)
            dsink_acc[...] = jnp.zeros_like(dsink_acc)

        d_bias, d_sink, dk_blocks, dv_blocks = {}, {}, [], []
        for b in range(B):
            edge_ok = _win_edge_ok(B * i + b, n_blk)
            k = jnp.concatenate([r[...] for r in k_refs[b:b + 3]], axis=0)
            v = jnp.concatenate([r[...] for r in v_refs[b:b + 3]], axis=0)
            kt = k.T
            qt = (_rows_of(q_ref, b).astype(F32) * WIN_SCALE).T.astype(BF16)
            dot_ = _rows_of(do_ref, b).astype(F32).T.astype(BF16)
            dq_parts, dks, dvs = [], [], []
            for g in range(KV_B):
                kg, vg = k[:, HD_B * g:HD_B * (g + 1)], v[:, HD_B * g:HD_B * (g + 1)]
                q4t, do4t = _group_t(qt, g), _group_t(dot_, g)
                p, p_sink = _win_probs(kg, q4t, bias_ref, sink_ref, g, edge_ok)
                dp = _dot(vg, do4t, NN)
                delta = jnp.sum(p * dp, axis=0, keepdims=True)
                ds = p * (dp - delta)
                d_bias[g] = ds if b == 0 else d_bias[g] + ds
                d_sink[g] = -p_sink * delta if b == 0 else d_sink[g] - p_sink * delta
                dsb = ds.astype(BF16)
                dq4t = _dot(kt[HD_B * g:HD_B * (g + 1), :], dsb, NN) * WIN_SCALE
                dq_parts += [dq4t[:, Q_BLOCK * j:Q_BLOCK * (j + 1)] for j in range(GROUP)]
                dks.append(_dot(dsb, q4t, NT))
                dvs.append(_dot(p.astype(BF16), do4t, NT))
            dq_ref[Q_BLOCK * b:Q_BLOCK * (b + 1), :] = jnp.concatenate(dq_parts, axis=0).T.astype(BF16)
            dk_blocks.append(jnp.concatenate(dks, axis=1))
            dv_blocks.append(jnp.concatenate(dvs, axis=1))

        for g in range(KV_B):
            for j in range(GROUP):
                dbias_ref[GROUP * g + j] += d_bias[g][:, Q_BLOCK * j:Q_BLOCK * (j + 1)]
            dsink_acc[g:g + 1, :] += d_sink[g]

        def overlap(blocks):
            out = blocks[0]
            for blk in blocks[1:]:
                keep = out.shape[0] - 2 * Q_BLOCK
                out = jnp.concatenate([out[:keep], out[keep:] + blk[:2 * Q_BLOCK], blk[2 * Q_BLOCK:]], axis=0)
            return out

        rows = pl.ds(pl.multiple_of(i * (B * Q_BLOCK), B * Q_BLOCK), (B + 2) * Q_BLOCK)
        dk_ref[rows, :] += overlap(dk_blocks)
        dv_ref[rows, :] += overlap(dv_blocks)

        @pl.when(i == n_blk // B - 1)
        def _():
            acc = dsink_acc[...]
            dsink_ref[...] = jnp.concatenate(
                [jnp.sum(acc[:, Q_BLOCK * j:Q_BLOCK * (j + 1)], axis=1, keepdims=True) for j in range(GROUP)], axis=1)

    whole = lambda shape: pl.BlockSpec(shape, lambda i: (0,) * len(shape))
    return _pcall(
        body, name="win_bwd", grid=(n_blk // B,),
        in_specs=[qspec, *kspecs[0], *kspecs[1], bias_spec, sink_spec, pl.BlockSpec((B * Q_BLOCK, H_B * HD_B), lambda i: (i, 0)), ANY],
        out_specs=[qspec, whole((S + 2 * WINDOW, KV_B * HD_B)),
                   whole((S + 2 * WINDOW, KV_B * HD_B)), whole((H_B, SPAN, Q_BLOCK)), whole((KV_B, GROUP))],
        out_shape=[jax.ShapeDtypeStruct((S, PROJ_P), BF16), jax.ShapeDtypeStruct((S + 2 * WINDOW, KV_B * HD_B), F32),
                   jax.ShapeDtypeStruct((S + 2 * WINDOW, KV_B * HD_B), F32), jax.ShapeDtypeStruct((H_B, SPAN, Q_BLOCK), F32),
                   jax.ShapeDtypeStruct((KV_B, GROUP), F32)],
        scratch_shapes=[pltpu.VMEM((KV_B, GROUP * Q_BLOCK), F32)],
        dims=("arbitrary",), aliases={2 * B + 8: 0})(*[proj] * (2 * B + 5), bias_t, sinks_b, do_b, d_proj)


def _bias_table(rel_bias_t, onehot_t, in_band):
    def body(rb_ref, oh_ref, band_ref, o_ref):
        t = lax.dot_general(rb_ref[...], oh_ref[...], NN, preferred_element_type=F32, precision=lax.Precision.HIGHEST)
        o_ref[...] = jnp.where(band_ref[...] > 0.5, t, -1e30)

    n = onehot_t.shape[1]
    tn = _tile(n, 8192)
    return _pcall(body, name="bias_table", grid=(n // tn,),
                  in_specs=[pl.BlockSpec((H_B, NUM_BUCKETS), lambda j: (0, 0)), pl.BlockSpec((NUM_BUCKETS, tn), lambda j: (0, j)),
                            pl.BlockSpec((1, tn), lambda j: (0, j))],
                  out_specs=pl.BlockSpec((H_B, tn), lambda j: (0, j)),
                  out_shape=jax.ShapeDtypeStruct((H_B, n), F32), dims=("parallel",))(rel_bias_t, onehot_t, in_band)


def _bias_table_bwd(dbias, onehot_t):
    n = onehot_t.shape[1]
    tk = _tile(n, 8192)

    def body(d_ref, oh_ref, o_ref):
        @pl.when(pl.program_id(0) == 0)
        def _():
            o_ref[...] = jnp.zeros_like(o_ref)

        o_ref[...] += lax.dot_general(d_ref[...], oh_ref[...], NT, preferred_element_type=F32, precision=lax.Precision.HIGHEST)

    return _pcall(body, name="bias_table_bwd", grid=(n // tk,),
                  in_specs=[pl.BlockSpec((H_B, tk), lambda j: (0, j)), pl.BlockSpec((NUM_BUCKETS, tk), lambda j: (0, j))],
                  out_specs=pl.BlockSpec((H_B, NUM_BUCKETS), lambda j: (0, 0)),
                  out_shape=jax.ShapeDtypeStruct((H_B, NUM_BUCKETS), F32), dims=("arbitrary",))(dbias, onehot_t)


CONV_STRIP = 128
N_STRIPS = D_FF // CONV_STRIP
CONV_ROWS = 128
HALO = 8


def _strip(rows, half):
    return pl.BlockSpec((rows, CONV_STRIP), lambda j: (0, j + half * N_STRIPS))


def _fill_padded(pad_ref, src_ref, halo, S):
    pad_ref[0:halo, :] = jnp.zeros((halo, CONV_STRIP), F32)
    pad_ref[halo + S:2 * halo + S, :] = jnp.zeros((halo, CONV_STRIP), F32)
    pad_ref[halo:halo + S, :] = src_ref[...].astype(F32)


def _conv_gate_fwd(u, conv_w, conv_b, S):
    R = min(CONV_ROWS, S)

    def body(ug_ref, uv_ref, wg_ref, wv_ref, bg_ref, bv_ref, a_ref, gpad, vpad):
        _fill_padded(gpad, ug_ref, HALO, S)
        _fill_padded(vpad, uv_ref, HALO, S)
        wg, wv, bg, bv = wg_ref[...], wv_ref[...], bg_ref[...], bv_ref[...]

        def conv(pad_ref, r0, w, b):
            dn, mid, up = (pad_ref[pl.ds(r0 + HALO + d, R), :] for d in (-1, 0, 1))
            return dn * w[0:1, :] + mid * w[1:2, :] + up * w[2:3, :] + b

        def step(c, carry):
            r0 = pl.multiple_of(c * R, R)
            g = conv(gpad, r0, wg, bg)
            val = conv(vpad, r0, wv, bv)
            a_ref[pl.ds(r0, R), :] = (g * _sigmoid(g) * val).astype(BF16)
            return carry

        lax.fori_loop(0, S // R, step, 0)

    return _pcall(body, name="conv_gate_fwd", grid=(N_STRIPS,),
                  in_specs=[_strip(S, 0), _strip(S, 1), _strip(3, 0), _strip(3, 1), _strip(1, 0), _strip(1, 1)],
                  out_specs=_strip(S, 0), out_shape=jax.ShapeDtypeStruct((S, D_FF), BF16),
                  scratch_shapes=[pltpu.VMEM((S + 2 * HALO, CONV_STRIP), F32)] * 2,
                  dims=("parallel",))(u, u, conv_w, conv_w, conv_b, conv_b)


def _conv_gate_bwd(u, conv_w, conv_b, da, S):
    R = min(CONV_ROWS, S)
    n = R + 2 * HALO

    def body(ug_ref, uv_ref, wg_ref, wv_ref, bg_ref, bv_ref, da_ref, dug_ref, duv_ref, dwg_ref, dwv_ref, dbg_ref, dbv_ref,
             gpad, vpad, dapad):
        _fill_padded(gpad, ug_ref, 2 * HALO, S)
        _fill_padded(vpad, uv_ref, 2 * HALO, S)
        _fill_padded(dapad, da_ref, HALO, S)
        wg, wv, bg, bv = wg_ref[...], wv_ref[...], bg_ref[...], bv_ref[...]

        def conv(pad_ref, r0, w, b):
            dn, mid, up = (pad_ref[pl.ds(r0 + HALO + d, n), :] for d in (-1, 0, 1))
            return dn * w[0:1, :] + mid * w[1:2, :] + up * w[2:3, :] + b, mid[HALO:HALO + R]

        def conv_bwd(duc, u_mid, w, r0, du_ref):
            dn, mid, up = pltpu.roll(duc, 1, axis=0)[HALO:HALO + R], duc[HALO:HALO + R], pltpu.roll(duc, n - 1, axis=0)[HALO:HALO + R]
            du_ref[pl.ds(r0, R), :] = (up * w[0:1, :] + mid * w[1:2, :] + dn * w[2:3, :]).astype(BF16)
            dw = jnp.concatenate([jnp.sum(up * u_mid, axis=0, keepdims=True), jnp.sum(mid * u_mid, axis=0, keepdims=True),
                                  jnp.sum(dn * u_mid, axis=0, keepdims=True)], axis=0)
            return dw, jnp.sum(mid, axis=0, keepdims=True)

        def step(c, carry):
            dw_g, db_g, dw_v, db_v = carry
            r0 = pl.multiple_of(c * R, R)
            g, ug_mid = conv(gpad, r0, wg, bg)
            val, uv_mid = conv(vpad, r0, wv, bv)
            da_ext = dapad[pl.ds(r0, n), :]
            sg = _sigmoid(g)
            ddw_v, ddb_v = conv_bwd(da_ext * (g * sg), uv_mid, wv, r0, duv_ref)
            ddw_g, ddb_g = conv_bwd(da_ext * val * (sg * (1.0 + g * (1.0 - sg))), ug_mid, wg, r0, dug_ref)
            return dw_g + ddw_g, db_g + ddb_g, dw_v + ddw_v, db_v + ddb_v

        z3, z1 = jnp.zeros((3, CONV_STRIP), F32), jnp.zeros((1, CONV_STRIP), F32)
        dwg_ref[...], dbg_ref[...], dwv_ref[...], dbv_ref[...] = lax.fori_loop(0, S // R, step, (z3, z1, z3, z1))

    half = lambda r, dt: (_strip(r, 0), jax.ShapeDtypeStruct((r, D_FF), dt))
    outs = [half(S, BF16), half(S, BF16), half(3, F32), half(3, F32), half(1, F32), half(1, F32)]
    return _pcall(
        body, name="conv_gate_bwd", grid=(N_STRIPS,),
        in_specs=[_strip(S, 0), _strip(S, 1), _strip(3, 0), _strip(3, 1), _strip(1, 0), _strip(1, 1), _strip(S, 0)],
        out_specs=[o[0] for o in outs], out_shape=[o[1] for o in outs],
        scratch_shapes=[pltpu.VMEM((S + 4 * HALO, CONV_STRIP), F32)] * 2 + [pltpu.VMEM((S + 2 * HALO, CONV_STRIP), F32)],
        dims=("parallel",))(u, u, conv_w, conv_w, conv_b, conv_b, da)


MESH = pl.DeviceIdType.MESH
ANY = pl.BlockSpec(memory_space=pl.ANY)


def _place():
    return lax.axis_index("x"), lax.axis_index("y"), lax.axis_index("c")


def _gather2(ins, outs, sems, finish):
    send_sems, recv_sems, local_sems = sems
    n_arr = len(ins)
    x, y, c = _place()
    me, sibling = (x, y, c), (x, y, 1 - c)
    chips = [(1 - x, y), (x, 1 - y), (1 - x, 1 - y)]

    def slot(a, p):
        return outs[a].at[4 * p[0] + 2 * p[1] + p[2]]

    def copy(a, k, block, to, src=None):
        return pltpu.make_async_remote_copy(
            src_ref=slot(a, block) if src is None else src, dst_ref=slot(a, block),
            send_sem=send_sems.at[a, k], recv_sem=recv_sems.at[a, k], device_id=to, device_id_type=MESH)

    mine = [pltpu.make_async_copy(ins[a], slot(a, me), local_sems.at[a]) for a in range(n_arr)]
    first = []
    for a in range(n_arr):
        first.append(copy(a, 0, me, sibling, src=ins[a]))
        first += [copy(a, 1 + j, me, (*chip, c), src=ins[a]) for j, chip in enumerate(chips)]
    if not finish:
        for cp in mine + first:
            cp.start()
        return
    passed = []
    for j, chip in enumerate(chips):
        for a in range(n_arr):
            copy(a, 1 + j, (*chip, c), me).wait_recv()
            cp = copy(a, 4 + j, (*chip, c), sibling)
            cp.start()
            passed.append(cp)
    for a in range(n_arr):
        copy(a, 0, sibling, me).wait_recv()
        for j, chip in enumerate(chips):
            copy(a, 4 + j, (*chip, 1 - c), me).wait_recv()
    for cp in first + passed:
        cp.wait_send()
    for cp in mine:
        cp.wait()


def _xchg_out_shapes(stacked, replicated):
    return ([jax.ShapeDtypeStruct(s.shape, s.dtype) for s in stacked]
            + [jax.ShapeDtypeStruct((N_DEV, *r.shape), r.dtype) for r in replicated])


def _xchg_sems(n_arr):
    return [pltpu.SemaphoreType.DMA((n_arr, 7)), pltpu.SemaphoreType.DMA((n_arr, 7)), pltpu.SemaphoreType.DMA((n_arr,))]


def _xchg_copies(ins, outs, sems, n_st, with_recv):
    send_sems, recv_sems, local_sems = sems
    n_arr = len(ins)
    x, y, c = _place()
    me = 4 * x + 2 * y + c

    def src(a, idx):
        return ins[a].at[idx] if a < n_st else ins[a]

    mine = [pltpu.make_async_copy(src(a, me), outs[a].at[me], local_sems.at[a]) for a in range(n_arr)]
    pairs = []
    for k in range(1, N_DEV):
        px, py, pc = x ^ (k >> 2), y ^ ((k >> 1) & 1), c ^ (k & 1)
        peer = 4 * px + 2 * py + pc
        for a in range(n_arr):
            sems_k = dict(send_sem=send_sems.at[a, k - 1], recv_sem=recv_sems.at[a, k - 1], device_id_type=MESH)
            send = pltpu.make_async_remote_copy(src_ref=src(a, peer), dst_ref=outs[a].at[me], device_id=(px, py, pc), **sems_k)
            recv = None
            if with_recv:
                recv = pltpu.make_async_remote_copy(src_ref=src(a, peer), dst_ref=outs[a].at[peer], device_id=(x, y, c), **sems_k)
            pairs.append((send, recv))
    return mine, pairs


def _xchg_start(ins, outs, sems, n_st):
    mine, pairs = _xchg_copies(ins, outs, sems, n_st, False)
    for cp in mine:
        cp.start()
    for send, _ in pairs:
        send.start()


def _xchg_finish(ins, outs, sems, n_st):
    mine, pairs = _xchg_copies(ins, outs, sems, n_st, True)
    for _, recv in pairs:
        recv.wait_recv()
    for send, _ in pairs:
        send.wait_send()
    for cp in mine:
        cp.wait()


def _exchange(stacked, replicated, name):
    _, landed = _pcall(lambda: None, name=name, grid=(), in_specs=[], out_specs=[], out_shape=[], comm=(stacked, replicated))()
    return landed


HBM = pl.BlockSpec(memory_space=pltpu.HBM)
SEMS = pl.BlockSpec(memory_space=pltpu.SEMAPHORE)
SIDE_EFFECT = pltpu.SideEffectType.DATAFLOW_SIDE_EFFECTING


N_SPLIT_SEMS = 2 * (N_DEV - 1)


def _split_copies(src, land, sems, with_recv):
    x, y, c = _place()
    me = 4 * x + 2 * y + c
    pairs = []
    for k in range(1, N_DEV):
        px, py, pc = x ^ (k >> 2), y ^ ((k >> 1) & 1), c ^ (k & 1)
        peer = 4 * px + 2 * py + pc
        sems_k = dict(send_sem=sems[k - 1], recv_sem=sems[N_DEV - 1 + k - 1], device_id_type=MESH)
        send = pltpu.make_async_remote_copy(src_ref=src.at[peer], dst_ref=land.at[me], device_id=(px, py, pc), **sems_k)
        recv = None
        if with_recv:
            recv = pltpu.make_async_remote_copy(src_ref=src.at[peer], dst_ref=land.at[peer], device_id=(x, y, c), **sems_k)
        pairs.append((send, recv))
    return pairs


def _exchange_start(stacked, name):
    def body(src, land, *rest):
        for send, _ in _split_copies(src, land, rest[:N_SPLIT_SEMS], False):
            send.start()
        rest[-1][...] = jnp.zeros_like(rest[-1])

    shape = pltpu.HBM(stacked.shape, stacked.dtype)
    res = pl.pallas_call(
        body, name=name, in_specs=[HBM, HBM],
        out_shape=(*[pltpu.SemaphoreType.DMA(())] * N_SPLIT_SEMS, shape, shape, jax.ShapeDtypeStruct((8, 128), F32)),
        out_specs=(*[SEMS] * N_SPLIT_SEMS, HBM, HBM, pl.BlockSpec(memory_space=pltpu.VMEM)),
        input_output_aliases={0: N_SPLIT_SEMS, 1: N_SPLIT_SEMS + 1},
        compiler_params=pltpu.CompilerParams(has_side_effects=SIDE_EFFECT),
    )(pltpu.with_memory_space_constraint(stacked, pltpu.HBM),
      pltpu.with_memory_space_constraint(lax.empty(stacked.shape, stacked.dtype), pltpu.HBM))
    return res[:N_SPLIT_SEMS], res[N_SPLIT_SEMS], res[N_SPLIT_SEMS + 1], res[-1]


def _exchange_wait(sems, src, land, after, name):
    def body(src_ref, land_ref, *rest):
        for send, recv in _split_copies(src_ref, land_ref, rest[:N_SPLIT_SEMS], True):
            send.wait_send()
            recv.wait_recv()

    shape = pltpu.HBM(src.shape, src.dtype)
    return pl.pallas_call(
        body, name=name, in_specs=[HBM, HBM, *[SEMS] * N_SPLIT_SEMS, ANY],
        out_shape=(shape, shape), out_specs=(HBM, HBM), input_output_aliases={0: 0, 1: 1},
        compiler_params=pltpu.CompilerParams(has_side_effects=SIDE_EFFECT))(src, land, *sems, after)[1]


def _adamw(parts, w, m, v, name):
    _, R, C = w.shape
    tr = R if R <= 512 else max(t for t in range(16, 513, 16) if R % t == 0)
    pr = tr if parts.shape[1] == R else -(-R // 16) * 16
    assert pr == tr or tr == R

    def body(p_ref, w_ref, m_ref, v_ref, g_ref, d_ref, nm_ref, nv_ref):
        g = p_ref[0].astype(F32)[:tr]
        for s in range(1, N_DEV):
            g = g + p_ref[s].astype(F32)[:tr]
        m2 = ADAM_B1 * m_ref[0] + (1.0 - ADAM_B1) * g
        v2 = ADAM_B2 * v_ref[0] + (1.0 - ADAM_B2) * (g * g)
        m_hat = m2 / (1.0 - ADAM_B1 ** ADAM_STEP)
        v_hat = v2 / (1.0 - ADAM_B2 ** ADAM_STEP)
        g_ref[0] = g
        d_ref[0] = -ADAM_LR * (m_hat / (jnp.sqrt(v_hat) + ADAM_EPS) + ADAM_WD * w_ref[0])
        nm_ref[0] = m2
        nv_ref[0] = v2

    blk = pl.BlockSpec((1, tr, C), lambda i: (0, i, 0))
    return _pcall(body, name=name, grid=(R // tr,),
                  in_specs=[pl.BlockSpec((N_DEV, pr, C), lambda i: (0, i, 0)), blk, blk, blk],
                  out_specs=[blk] * 4, out_shape=[jax.ShapeDtypeStruct((1, R, C), F32)] * 4,
                  dims=("parallel",))(parts, w, m, v)


def _t5_bucket(rel):
    nb = NUM_BUCKETS // 2
    max_exact = nb // 2
    base = (rel > 0).astype(jnp.int32) * nb
    n = jnp.abs(rel)
    nf = jnp.maximum(n, 1).astype(jnp.float32)
    large = max_exact + (jnp.log(nf / max_exact) / math.log(MAX_DISTANCE / max_exact) * (nb - max_exact)).astype(jnp.int32)
    large = jnp.minimum(large, nb - 1)
    return base + jnp.where(n < max_exact, n, large)


def _unstack_cols(g):
    return jnp.transpose(g, (1, 0, 2)).reshape(g.shape[1], N_DEV * g.shape[2])


def _stack_cols(w, n=N_DEV):
    R = w.shape[0]
    return jnp.transpose(w.reshape(R, n, w.shape[1] // n), (1, 0, 2))


def _stack_halves(g, v):
    return jnp.concatenate([_stack_cols(g, N_DEV // 2), _stack_cols(v, N_DEV // 2)], axis=0)


def kernel(x, positions, norm1_g, w_in, q_a_norm_g, w_q_b, kv_a_norm_g, w_kv_b, rel_bias, sinks, w_out, norm2_g, w_up, conv_w, conv_b, w_down, final_norm_g, loss_target, m_norm1_g, m_w_in, m_q_a_norm_g, m_w_q_b, m_kv_a_norm_g, m_w_kv_b, m_rel_bias, m_sinks, m_w_out, m_norm2_g, m_w_up, m_conv_w, m_conv_b, m_w_down, m_final_norm_g, v_norm1_g, v_w_in, v_q_a_norm_g, v_w_q_b, v_kv_a_norm_g, v_w_kv_b, v_rel_bias, v_sinks, v_w_out, v_norm2_g, v_w_up, v_conv_w, v_conv_b, v_w_down, v_final_norm_g):
    S = x.shape[1]
    x = x[0]
    target = loss_target[0]
    TM = 256

    tr = lambda w: jnp.swapaxes(w, 1, 2)
    (h1,), (g_in,) = _rowwise(lambda a, g: (_rms(a, g),), "norm1_gather", S, TM, [_rows(x), _whole(norm1_g)], [("rows", D_MODEL, BF16)],
                              gather=[tr(w_in)[0].astype(BF16)])
    late_weights = [w_out[0].astype(BF16), tr(w_up)[0].astype(BF16), conv_w[0]]
    wi = g_in.reshape(W_IN_COLS, D_MODEL)
    c0, c1, c2, c3, c4, c5 = (sum(W_IN_SIZES[:i + 1]) for i in range(6))
    w_in_pt = jnp.concatenate([wi[c4:c5], wi[c5:], wi[c1:c2], wi[:c0], wi[c2:c3], wi[c3:c4],
                               wi[c0:c0 + KV_LORA], wi[c0 + KV_LORA:c1], jnp.zeros((64, D_MODEL), BF16)], axis=0)

    half = QK_ROPE // 2
    inv_freq = ROPE_THETA ** (-jnp.arange(half, dtype=F32) / half)
    inv_tile = jnp.concatenate([inv_freq, inv_freq, jnp.zeros((128 - QK_ROPE,), F32)])
    ang = positions.astype(F32)[:, None] * inv_tile[None, :]
    cos_p, sin_p = jnp.cos(ang), jnp.sin(ang)
    sin_fwd = sin_p * jnp.concatenate([-jnp.ones((half,), F32), jnp.ones((half,), F32), jnp.zeros((128 - QK_ROPE,), F32)])[None, :]
    sin_back = -sin_fwd
    cos, sin = cos_p[:, :half], sin_p[:, :half]
    qa = jnp.arange(Q_BLOCK, dtype=jnp.int32)[:, None]
    kc = jnp.arange(SPAN, dtype=jnp.int32)[None, :]
    rel = (kc - WINDOW - qa).T
    in_band = (jnp.abs(rel) <= WINDOW).astype(F32).reshape(1, Q_BLOCK * SPAN)
    onehot_t = (_t5_bucket(rel).reshape(1, Q_BLOCK * SPAN) == jnp.arange(NUM_BUCKETS, dtype=jnp.int32)[:, None]).astype(F32)
    bias_t = _bias_table(rel_bias.T, onehot_t, in_band).reshape(H_B, SPAN, Q_BLOCK)
    sinks_b = jnp.broadcast_to(sinks.reshape(H_B, 1), (H_B, Q_BLOCK))

    proj, (g_qb, g_kvb) = _matmul(h1, w_in_pt, "nt", BF16, "proj", comm=([], [tr(w_q_b)[0].astype(BF16), w_kv_b[0].astype(BF16)]))
    wq = g_qb.reshape(H_A, QK_HEAD, Q_LORA)
    w_qb_pt = jnp.concatenate([wq[:, :QK_NOPE].reshape(H_A * QK_NOPE, Q_LORA),
                               jnp.pad(wq[:, QK_NOPE:], ((0, 0), (0, 128 - QK_ROPE), (0, 0))).reshape(H_A * 128, Q_LORA)], axis=0)
    w_kvb = _unstack_cols(g_kvb)

    def lat_fn(qlat, ckv, kr, gq, gkv, cs, sn):
        return _rms(qlat, gq), _rms(ckv, gkv), _rope128(kr, cs, sn)

    qn, ckvn, k_rope = _rowwise(lat_fn, "latents", S, TM,
                                [_rows(proj, 256, PROJ_QLAT), _rows(proj, 128, PROJ_CKV), _rows(proj, 128, PROJ_KROPE),
                                 _whole(q_a_norm_g), _whole(kv_a_norm_g), _rows(cos_p), _rows(sin_fwd)],
                                [("rows", Q_LORA, BF16), ("rows", KV_LORA, BF16), ("rows", 128, BF16)])
    def q_heads_fn(q, cs, sn):
        q = q * MLA_PRESCALE
        return (jnp.concatenate([jnp.concatenate([q[:, 128 * h:128 * (h + 1)], _rope128(q[:, 128 * (H_A + h):128 * (H_A + h + 1)], cs, sn)],
                                                 axis=1)[None] for h in range(H_A)], axis=0),)

    (q_full,) = _matmul(qn, w_qb_pt, "nt", None, "q_up_heads", tm=512, tn=2048,
                        epi=(q_heads_fn, [_rows(cos_p), _rows(sin_fwd)], [("heads", H_A, HEAD_PAD, BF16)]))

    def k_heads_fn(kvf, kr):
        return kvf, jnp.concatenate([jnp.concatenate([kvf[:, 256 * h:256 * h + QK_NOPE], kr], axis=1)[None] for h in range(H_A)], axis=0)

    kv, k_full = _matmul(ckvn, w_kvb, "nn", None, "kv_up_heads", tm=512, tn=2048,
                         epi=(k_heads_fn, [_rows(k_rope)], [("rows", H_A * (QK_NOPE + V_DIM), BF16), ("heads", H_A, HEAD_PAD, BF16)]))
    (o_a, lse), (g_out, g_up, g_cw) = _mla_fwd(q_full, k_full, kv, S, comm=([], late_weights))
    w_out_f = g_out.reshape(D_MODEL, D_MODEL)
    w_up_t = g_up.reshape(2 * D_FF, D_MODEL)
    conv_w_f = _unstack_cols(g_cw)

    (o_b, mixed), (g_down,) = _win_fwd(proj, o_a, bias_t, sinks_b, S, comm=([], [w_down[0].astype(BF16)]))
    w_down_f = g_down.reshape(D_FF, D_MODEL)

    x1, h2 = _matmul(mixed, w_out_f, "nn", None, "out_proj", residual=x, tm=512,
                     epi=(lambda a, g: (a, _rms(a, g)), [_whole(norm2_g)], [("rows", D_MODEL, F32), ("rows", D_MODEL, BF16)]))
    u = _matmul(h2, w_up_t, "nt", BF16, "ffn_up", tn=1408)
    act = _conv_gate_fwd(u, conv_w_f, conv_b, S)

    def final_fn(a, g, t):
        err = _rms(a, g) - t
        loss = 0.5 * jnp.sum(jnp.mean(err * err, axis=-1, keepdims=True), axis=0, keepdims=True)
        dx, dg = _rms_bwd(err * (1.0 / D_MODEL), a, g)
        return dx, dx, dg, jnp.broadcast_to(loss, (1, 128))

    gfin = final_norm_g.reshape(1, D_MODEL)
    dx2, dx2_b, d_gfin, loss_row = _matmul(
        act, w_down_f, "nn", None, "ffn_down_loss", residual=x1, tm=512,
        epi=(final_fn, [_whole(gfin), _rows(target)],
             [("rows", D_MODEL, F32), ("rows", D_MODEL, BF16), ("acc", 1, D_MODEL), ("acc", 1, 128)]))
    d_act = _matmul(dx2_b, w_down_f, "nt", BF16, "ffn_down_dx", tn=1408)
    d_w_down = _matmul(act, dx2_b, "tn", BF16, "ffn_down_dw")
    du_g, du_v, dcw_g, dcw_v, dcb_g, dcb_v = _conv_gate_bwd(u, conv_w_f, conv_b, d_act, S)
    d_conv_b = jnp.concatenate([dcb_g, dcb_v], axis=1)

    def norm_bwd_fn(dh, a, g, dres):
        dx, dg = _rms_bwd(dh, a, g)
        dx = dx + dres
        return dx, dx, dg

    dx1, dx1_b, d_g2 = _matmul(du_g, w_up_t, "nn", None, "ffn_up_dx_norm2_bwd", tm=256, a2=du_v,
                               epi=(norm_bwd_fn, [_rows(x1), _whole(norm2_g), _rows(dx2)],
                                    [("rows", D_MODEL, F32), ("rows", D_MODEL, BF16), ("acc", 1, D_MODEL)]))
    d_w_up_t = _matmul(du_g, h2, "tn", BF16, "ffn_up_dw", tm=256, a2=du_v)
    d_w_out = _matmul(mixed, dx1_b, "tn", BF16, "out_proj_dw", tm=512)

    def gate_bwd_fn(dm, ga, gb, oa, ob):
        sa, sb = _sigmoid(ga), _sigmoid(gb)
        return jnp.concatenate([dm * oa * sa * (1.0 - sa), dm * ob * sb * (1.0 - sb)], axis=1), dm * sa, dm * sb

    d_proj, do_a, do_b = _matmul(
        dx1_b, w_out_f, "nt", None, "out_proj_dx_gate_bwd", tm=512,
        epi=(gate_bwd_fn, [_rows(proj, 1024, PROJ_GA), _rows(proj, 1024, PROJ_GB), _rows(o_a), _rows(o_b)],
             [("cols", 2 * D_MODEL, 0, PROJ_P, BF16), ("rows", D_MODEL, BF16), ("rows", D_MODEL, BF16)]))

    d_proj, dk_acc, dv_acc, d_bias, d_sinks_g = _win_bwd(proj, bias_t, sinks_b, do_b, d_proj, S)
    d_sinks = d_sinks_g.reshape(1, H_B)

    early = [d_w_out.reshape(N_DEV, D_MODEL // N_DEV, D_MODEL), d_w_up_t.reshape(N_DEV, 2 * D_FF // N_DEV, D_MODEL),
             d_w_down.reshape(N_DEV, D_FF // N_DEV, D_MODEL), _stack_halves(dcw_g, dcw_v)]
    (dq_nope, dq_rope, dkv, dkr_heads), recv_early = _mla_bwd(q_full, k_full, kv, do_a, o_a, lse, cos_p, sin_back, S, comm=(early, []))

    d_qn = _matmul(dq_nope, w_qb_pt, "nn", F32, "q_up_dx", a2=dq_rope)
    d_ckvn = _matmul(dkv, w_kvb, "nt", F32, "kv_up_dx")

    def lat_bwd_fn(dqn, dckvn, dkr_h, cs, sn, qlat, ckv, gq, gkv, dkb, dvb):
        dql, dgq = _rms_bwd(dqn, qlat, gq)
        dck, dgkv = _rms_bwd(dckvn, ckv, gkv)
        dkr = dkr_h[0]
        for h in range(1, H_A):
            dkr = dkr + dkr_h[h]
        r1, r2 = _rope_bwd(dkr[:, :half], dkr[:, half:], cs, sn)
        tail = jnp.concatenate([dql, dkb, dvb, dck, r1, r2, jnp.zeros_like(dkr)], axis=1)
        return tail, dgq, dgkv

    shifted = lambda arr: (arr, lambda tm: pl.BlockSpec((tm, arr.shape[1]), lambda i, *_: (i + WINDOW // tm, 0)))
    TL = min(128, S)
    d_proj, d_gq, d_gkv = _rowwise(lat_bwd_fn, "latents_bwd", S, TL,
                                   [_rows(d_qn), _rows(d_ckvn), _heads(dkr_heads), _rows(cos), _rows(sin), _rows(proj, 256, PROJ_QLAT), _rows(proj, 128, PROJ_CKV),
                                    _whole(q_a_norm_g), _whole(kv_a_norm_g), shifted(dk_acc), shifted(dv_acc)],
                                   [("cols", 1024, 3, PROJ_P, BF16), ("acc", 1, Q_LORA), ("acc", 1, KV_LORA)], into=(d_proj, 0))
    dp = _matmul(d_proj, h1, "tn", BF16, "proj_dw", tm=512)

    late = jnp.concatenate([dp[3072:3328], dp[3840:3968], dp[3968:4032], dp[2048:3072], dp[3328:3584],
                            dp[3584:3840], dp[0:1024], dp[1024:2048]], axis=0).reshape(N_DEV, W_IN_COLS // N_DEV, D_MODEL)
    late_sems, late_src, late_land, started = _exchange_start(late, "late_grads_start")

    def norm1_bwd_fn(dh, a, g, dres):
        dx, dg = _rms_bwd(dh, a, g)
        return dx + dres, dg

    grad_x, d_g1 = _matmul(
        d_proj, w_in_pt, "nn", None, "proj_dx_norm1_bwd", tm=512,
        epi=(norm1_bwd_fn, [_rows(x), _whole(norm1_g + started[:1, :1]), _rows(dx1)], [("rows", D_MODEL, F32), ("acc", 1, D_MODEL)]))

    transposed = ("w_in", "w_q_b", "w_up")
    ready_names = ["w_out", "w_up", "w_down", "conv_w"]
    ready_wmv = [(w_out, m_w_out, v_w_out), (tr(w_up), tr(m_w_up), tr(v_w_up)), (w_down, m_w_down, v_w_down), (conv_w, m_conv_w, v_conv_w)]
    recv_early, _ = lax.optimization_barrier((list(recv_early), started))
    big = {n: _adamw(r, *wmv, "adamw_" + n) for n, r, wmv in zip(ready_names, recv_early, ready_wmv)}

    (d_bias, dq_nope_l, dkv_l), _ = lax.optimization_barrier(((d_bias, dq_nope, dkv), started))
    d_rel_bias = _bias_table_bwd(d_bias.reshape(H_B, Q_BLOCK * SPAN), onehot_t).T
    d_w_qb_pt = _matmul(dq_nope_l, qn, "tn", BF16, "q_up_dw", tm=512, a2=dq_rope)
    d_w_kvb = _matmul(ckvn, dkv_l, "tn", BF16, "kv_up_dw", tn=2048)
    d_w_qb_t = jnp.concatenate([d_w_qb_pt[:H_A * QK_NOPE].reshape(H_A, QK_NOPE, Q_LORA),
                                d_w_qb_pt[H_A * QK_NOPE:].reshape(H_A, 128, Q_LORA)[:, :QK_ROPE]], axis=1)

    after = lax.optimization_barrier([big[n][0] for n in ready_names] + [d_rel_bias, d_w_qb_t, d_w_kvb])
    landed = _exchange_wait(late_sems, late_src, late_land, after[-1], "late_grads_wait")
    me = 4 * lax.axis_index("x") + 2 * lax.axis_index("y") + lax.axis_index("c")
    landed = lax.dynamic_update_slice_in_dim(landed, lax.dynamic_slice_in_dim(late, me, 1, axis=0), me, axis=0)
    big["w_in"] = _adamw(landed, tr(w_in), tr(m_w_in), tr(v_w_in), "adamw_w_in")

    small_parts = [d_g1, d_gq, d_gkv, d_rel_bias.reshape(1, NUM_BUCKETS * H_B), d_sinks, d_g2, d_conv_b, d_gfin, loss_row[:, :1]]
    small = jnp.concatenate(small_parts, axis=1)
    n_small = small.shape[1]
    pad = (-n_small) % 128
    small = jnp.pad(small, ((0, 0), (0, pad)))
    small, _ = lax.optimization_barrier((small, [landed, *after]))
    recv_qb, recv_kvb, recv_small = _exchange([after[-2], _stack_cols(after[-1])], [small], "exchange_small_grads")
    big["w_q_b"] = _adamw(recv_qb, tr(w_q_b), tr(m_w_q_b), tr(v_w_q_b), "adamw_w_q_b")
    big["w_kv_b"] = _adamw(recv_kvb, w_kv_b, m_w_kv_b, v_w_kv_b, "adamw_w_kv_b")

    def flat(a):
        return a.reshape(1, -1)

    small_w = [norm1_g, q_a_norm_g, kv_a_norm_g, rel_bias, sinks, norm2_g, conv_b, final_norm_g]
    small_m = [m_norm1_g, m_q_a_norm_g, m_kv_a_norm_g, m_rel_bias, m_sinks, m_norm2_g, m_conv_b, m_final_norm_g]
    small_v = [v_norm1_g, v_q_a_norm_g, v_kv_a_norm_g, v_rel_bias, v_sinks, v_norm2_g, v_conv_b, v_final_norm_g]
    cat = lambda parts: jnp.pad(jnp.concatenate([flat(a) for a in parts], axis=1), ((0, 0), (0, pad + 1)))[None]
    sm = _adamw(recv_small, cat(small_w), cat(small_m), cat(small_v), "adamw_small")

    loss = sm[0][0, 0, n_small - 1]
    order =["norm1_g", "w_in", "q_a_norm_g", "w_q_b", "kv_a_norm_g", "w_kv_b", "rel_bias", "sinks", "w_out", "norm2_g", "w_up",
             "conv_w", "conv_b", "w_down", "final_norm_g"]
    small_names = ["norm1_g", "q_a_norm_g", "kv_a_norm_g", "rel_bias", "sinks", "norm2_g", "conv_b", "final_norm_g"]
    offs, o = {}, 0
    for n, a in zip(small_names, small_w):
        offs[n] = (o, a.size, a.shape)
        o += a.size
    outs = [loss, grad_x[None]]
    for kind in range(4):
        for n in order:
            if n in big:
                outs.append(tr(big[n][kind]) if n in transposed else big[n][kind])
            else:
                o, size, shape = offs[n]
                outs.append(sm[kind][0, 0, o:o + size].reshape(shape))
    return tuple(outs)
```

```python
import math

import jax
import jax.numpy as jnp
from jax import lax
from jax.experimental import pallas as pl
from jax.experimental.pallas import tpu as pltpu

F32 = jnp.float32
BF16 = jnp.bfloat16

N_DEV = 8
D_MODEL = 1024
EPS = 1e-6
H_A, QK_NOPE, QK_ROPE, V_DIM, Q_LORA, KV_LORA = 8, 128, 64, 128, 256, 128
QK_HEAD = QK_NOPE + QK_ROPE
HEAD_PAD = 256
ROPE_THETA = 10000.0
H_B, KV_B, GROUP, HD_B, WINDOW, Q_BLOCK = 16, 4, 4, 64, 128, 128
SPAN = Q_BLOCK + 2 * WINDOW
NUM_BUCKETS, MAX_DISTANCE = 32, 128
D_FF = 2816
ADAM_LR, ADAM_B1, ADAM_B2, ADAM_EPS, ADAM_WD, ADAM_STEP = 0.001, 0.9, 0.999, 1e-08, 0.01, 10

W_IN_SIZES = (Q_LORA, KV_LORA + QK_ROPE, H_B * HD_B, KV_B * HD_B, KV_B * HD_B, D_MODEL, D_MODEL)
W_IN_COLS = sum(W_IN_SIZES)
PROJ_P = 4096
PROJ_GA, PROJ_GB, PROJ_QB, PROJ_QLAT, PROJ_KB, PROJ_VB, PROJ_CKV, PROJ_KROPE = 0, 1, 2, 12, 13, 14, 30, 31

VMEM_LIMIT = 56 * 1024 * 1024

NN = (((1,), (0,)), ((), ()))
NT = (((1,), (1,)), ((), ()))
TN = (((0,), (0,)), ((), ()))


def _pcall(body, *, name, grid, in_specs, out_specs, out_shape, scratch_shapes=(), dims=None, comm=None, aliases=None, two_level=False):
    if comm is None:
        params = pltpu.CompilerParams(dimension_semantics=dims, vmem_limit_bytes=VMEM_LIMIT)
        return pl.pallas_call(body, name=name, grid=grid, in_specs=in_specs, out_specs=out_specs, out_shape=out_shape,
                              scratch_shapes=list(scratch_shapes), input_output_aliases=aliases or {}, compiler_params=params)
    assert not aliases
    stacked, replicated = comm
    arrs = [*stacked, *replicated]
    n_st, n_arr = len(stacked), len(arrs)
    single = not isinstance(out_specs, (list, tuple))
    o_specs, o_shape = ([out_specs], [out_shape]) if single else (list(out_specs), list(out_shape))
    n_in, n_out = len(in_specs), len(o_specs)

    def wrapped(*refs):
        c_in = refs[n_in:n_in + n_arr]
        c_out = refs[n_in + n_arr + n_out:n_in + 2 * n_arr + n_out]
        sems = refs[len(refs) - 3:]
        own = (*refs[:n_in], *refs[n_in + n_arr:n_in + n_arr + n_out], *refs[n_in + 2 * n_arr + n_out:len(refs) - 3])
        if two_level:
            assert n_st == 0
            start, finish = (lambda: _gather2(c_in, c_out, sems, False)), (lambda: _gather2(c_in, c_out, sems, True))
        else:
            start, finish = (lambda: _xchg_start(c_in, c_out, sems, n_st)), (lambda: _xchg_finish(c_in, c_out, sems, n_st))
        if not grid:
            start()
            finish()
            return
        first = last = None
        for d, n in enumerate(grid):
            pid = pl.program_id(d)
            first = (pid == 0) if first is None else first & (pid == 0)
            last = (pid == n - 1) if last is None else last & (pid == n - 1)

        pl.when(first)(start)
        body(*own)
        pl.when(last)(finish)

    params = pltpu.CompilerParams(dimension_semantics=("arbitrary",) * len(grid), vmem_limit_bytes=VMEM_LIMIT)
    call = pl.pallas_call(wrapped, name=name, grid=grid, in_specs=[*in_specs, *[ANY] * n_arr], out_specs=[*o_specs, *[ANY] * n_arr],
                          out_shape=[*o_shape, *_xchg_out_shapes(stacked, replicated)],
                          scratch_shapes=[*scratch_shapes, *_xchg_sems(n_arr)], compiler_params=params)

    def run(*args):
        res = call(*args, *arrs)
        outs, landed = res[:n_out], res[n_out:]
        return (outs[0] if single else outs), landed

    return run


def _dot(a, b, dn):
    return lax.dot_general(a, b, dn, preferred_element_type=F32)


def _tile(n, target):
    best = None
    for t in range(128, min(n, target) + 1, 128):
        if n % t == 0:
            best = t
    return n if best is None else best


def _matmul(a, b, mode, out_dtype, name, residual=None, tm=1024, tn=1024, comm=None, a2=None, epi=None):
    if mode == "nn":
        (M, K), N = a.shape, b.shape[1]
    elif mode == "nt":
        (M, K), N = a.shape, b.shape[0]
    else:
        (K, M), N = a.shape, b.shape[1]
    tm, tn = _tile(M, tm), _tile(N, tn)
    a_spec = pl.BlockSpec((K, tm), lambda i, j: (0, i)) if mode == "tn" else pl.BlockSpec((tm, K), lambda i, j: (i, 0))
    b_spec = pl.BlockSpec((tn, b.shape[1]), lambda i, j: (j, 0)) if mode == "nt" else pl.BlockSpec((K, tn), lambda i, j: (0, j))
    o_spec = pl.BlockSpec((tm, tn), lambda i, j: (i, j))
    in_specs, args = [a_spec, b_spec], [a, b]
    n1 = M // tm
    if a2 is not None and mode == "tn":
        assert M % tm == 0 and a2.shape[1] % tm == 0
        in_specs[0] = pl.BlockSpec((K, tm), lambda i, j: (0, jnp.minimum(i, n1 - 1)))
        in_specs.append(pl.BlockSpec((K, tm), lambda i, j: (0, jnp.maximum(i - n1, 0))))
        args.append(a2)
        M += a2.shape[1]
    elif a2 is not None:
        assert (mode == "nt" and K + a2.shape[1] == b.shape[1]) or (mode == "nn" and K + a2.shape[1] == b.shape[0])
        if mode == "nn":
            b_spec = in_specs[1] = pl.BlockSpec((b.shape[0], tn), lambda i, j: (0, j))
        in_specs.append(pl.BlockSpec((tm, a2.shape[1]), lambda i, j: (i, 0)))
        args.append(a2)
    if residual is not None:
        in_specs.append(o_spec)
        args.append(residual)
    n_mm = len(args)
    scratch = [pltpu.VMEM((tm, K), a.dtype)] if mode == "tn" else []
    if epi is None:
        out_specs, out_shape, is_acc = o_spec, jax.ShapeDtypeStruct((M, N), out_dtype), None
    else:
        assert tn == N
        fn, epi_ins, epi_outs = epi
        in_specs += [mk(tm) for _, mk in epi_ins]
        args += [arr for arr, _ in epi_ins]
        out_specs, out_shape, is_acc = _row_out_specs(epi_outs, M, tm)

    def body(*refs):
        a_ref, b_ref = refs[0], refs[1]
        n_out = 1 if epi is None else len(is_acc)
        out_refs = refs[len(args):len(args) + n_out]
        if mode == "tn":
            at_ref = refs[len(args) + n_out]

            first_col = pl.program_id(1) == 0
            from_a = first_col if a2 is None else first_col & (pl.program_id(0) < n1)

            @pl.when(from_a)
            def _():
                at_ref[...] = a_ref[...].T

            if a2 is not None:
                @pl.when(first_col & (pl.program_id(0) >= n1))
                def _():
                    at_ref[...] = refs[2][...].T

            acc = _dot(at_ref[...], b_ref[...], NN)
        elif a2 is not None and mode == "nt":
            acc = _dot(a_ref[...], b_ref[:, :K], NT) + _dot(refs[2][...], b_ref[:, K:], NT)
        elif a2 is not None:
            acc = _dot(a_ref[...], b_ref[:K, :], NN) + _dot(refs[2][...], b_ref[K:, :], NN)
        else:
            acc = _dot(a_ref[...], b_ref[...], NT if mode == "nt" else NN)
        if residual is not None:
            acc = acc + refs[n_mm - 1][...]
        if epi is None:
            out_refs[0][...] = acc.astype(out_dtype)
        else:
            _store_rows(out_refs, fn(acc, *[_load_f32(r) for r in refs[n_mm:len(args)]]), is_acc)

    return _pcall(body, name=name, grid=(M // tm, N // tn), in_specs=in_specs, out_specs=out_specs,
                  out_shape=out_shape, scratch_shapes=scratch,
                  dims=("arbitrary" if epi is not None else "parallel", "arbitrary"), comm=comm)(*args)


def _rows(arr, width=None, col=0):
    width = arr.shape[1] if width is None else width
    return (arr, lambda tm: pl.BlockSpec((tm, width), lambda i, *_: (i, col)))


def _heads(arr):
    return (arr, lambda tm: pl.BlockSpec((arr.shape[0], tm, arr.shape[2]), lambda i, *_: (0, i, 0)))


def _whole(arr):
    nd = arr.ndim
    return (arr, lambda tm: pl.BlockSpec(arr.shape, lambda i, *_: (0,) * nd))


def _row_out_specs(outs, n_rows, tm):
    out_specs, out_shape, is_acc = [], [], []
    for o in outs:
        if o[0] == "rows":
            out_specs.append(pl.BlockSpec((tm, o[1]), lambda i, *_: (i, 0)))
            out_shape.append(jax.ShapeDtypeStruct((n_rows, o[1]), o[2]))
        elif o[0] == "cols":
            out_specs.append(pl.BlockSpec((tm, o[1]), lambda i, *_, c=o[2]: (i, c)))
            out_shape.append(jax.ShapeDtypeStruct((n_rows, o[3]), o[4]))
        elif o[0] == "heads":
            out_specs.append(pl.BlockSpec((o[1], tm, o[2]), lambda i, *_: (0, i, 0)))
            out_shape.append(jax.ShapeDtypeStruct((o[1], n_rows, o[2]), o[3]))
        else:
            out_specs.append(pl.BlockSpec((o[1], o[2]), lambda i, *_: (0, 0)))
            out_shape.append(jax.ShapeDtypeStruct((o[1], o[2]), F32))
        is_acc.append(o[0] == "acc")
    return out_specs, out_shape, is_acc


def _load_f32(r):
    v = r[...]
    return v.astype(F32) if v.dtype == BF16 else v


def _store_rows(out_refs, vals, is_acc):
    for r, v, acc in zip(out_refs, vals, is_acc):
        if acc:
            @pl.when(pl.program_id(0) == 0)
            def _():
                r[...] = jnp.zeros_like(r)

            r[...] += v
        else:
            r[...] = v.astype(r.dtype)


def _rowwise(fn, name, n_rows, tm, ins, outs, upcast=True, into=None, gather=None):
    tm = min(tm, n_rows)
    assert n_rows % tm == 0
    in_specs = [mk(tm) for _, mk in ins]
    out_specs, out_shape, is_acc = _row_out_specs(outs, n_rows, tm)
    n_in = len(ins)
    args = [a for a, _ in ins]
    aliases = {}
    if into is not None:
        in_specs.append(ANY)
        args.append(into[0])
        aliases = {n_in: into[1]}

    def body(*refs):
        vals = fn(*[_load_f32(r) if upcast else r[...] for r in refs[:n_in]])
        _store_rows(refs[len(args):], vals, is_acc)

    return _pcall(body, name=name, grid=(n_rows // tm,), in_specs=in_specs, out_specs=out_specs,
                  out_shape=out_shape, dims=("arbitrary",), aliases=aliases,
                  comm=None if gather is None else ([], gather), two_level=True)(*args)


def _rms(x, g):
    r = lax.rsqrt(jnp.mean(x * x, axis=-1, keepdims=True) + EPS)
    return x * r * g


def _rms_bwd(dy, x, g):
    r = lax.rsqrt(jnp.mean(x * x, axis=-1, keepdims=True) + EPS)
    xhat = x * r
    dxhat = dy * g
    dx = r * (dxhat - xhat * jnp.mean(dxhat * xhat, axis=-1, keepdims=True))
    return dx, jnp.sum(dy * xhat, axis=0, keepdims=True)


def _rope_bwd(d1, d2, cos, sin):
    return d1 * cos + d2 * sin, d2 * cos - d1 * sin


def _rope128(x, cos_p, sin_p):
    lane = lax.broadcasted_iota(jnp.int32, x.shape, 1)
    swapped = jnp.where(lane < QK_ROPE // 2, pltpu.roll(x, 128 - QK_ROPE // 2, axis=1), pltpu.roll(x, QK_ROPE // 2, axis=1))
    return x * cos_p + swapped * sin_p


def _sigmoid(x):
    return 1.0 / (1.0 + jnp.exp(-x))


MLA_SCALE = 1.0 / math.sqrt(QK_HEAD)
MLA_PRESCALE = MLA_SCALE * math.log2(math.e)
MLA_TQ, MLA_KC = 1024, 1024


def _mla_fwd(q_full, k_full, kv, S, comm=None):
    tq, kc = min(2 * MLA_TQ, S), min(MLA_KC, S)

    def body(q_ref, k_ref, v_ref, o_ref, lse_ref):
        q = q_ref[0]
        m = jnp.full((tq, 1), -1e30, F32)
        l = jnp.zeros((tq, 1), F32)
        acc = jnp.zeros((tq, V_DIM), F32)
        for c in range(S // kc):
            s = _dot(q, k_ref[0, c * kc:(c + 1) * kc, :], NT)
            m_new = jnp.maximum(m, jnp.max(s, axis=-1, keepdims=True))
            alpha = jnp.exp2(m - m_new)
            p = jnp.exp2(s - m_new)
            l = alpha * l + jnp.sum(p, axis=-1, keepdims=True)
            acc = alpha * acc + _dot(p.astype(BF16), v_ref[c * kc:(c + 1) * kc, :], NN)
            m = m_new
        o_ref[...] = (acc / l).astype(BF16)
        lse_ref[0] = m + jnp.log2(l)

    return _pcall(
        body, name="mla_fwd", grid=(H_A, S // tq),
        in_specs=[pl.BlockSpec((1, tq, HEAD_PAD), lambda h, i: (h, i, 0)),
                  pl.BlockSpec((1, S, HEAD_PAD), lambda h, i: (h, 0, 0)),
                  pl.BlockSpec((S, V_DIM), lambda h, i: (0, 2 * h + 1))],
        out_specs=[pl.BlockSpec((tq, V_DIM), lambda h, i: (i, h)),
                   pl.BlockSpec((1, tq, 1), lambda h, i: (h, i, 0))],
        out_shape=[jax.ShapeDtypeStruct((S, H_A * V_DIM), BF16), jax.ShapeDtypeStruct((H_A, S, 1), F32)],
        dims=("parallel", "parallel"), comm=comm)(q_full, k_full, kv)


def _mla_bwd(q_full, k_full, kv, do_a, o_a, lse, cos_p, sin_back, S, comm=None):
    tq, kc = min(MLA_TQ, S), min(MLA_KC, S)

    def body(q_ref, k_ref, v_ref, do_ref, o_ref, lse_ref, cos_ref, sin_ref, dqn_ref, dqr_ref, dkv_out, dkr_out, dk_ref, dv_ref):
        @pl.when(pl.program_id(1) == 0)
        def _():
            dk_ref[...] = jnp.zeros_like(dk_ref)
            dv_ref[...] = jnp.zeros_like(dv_ref)

        q = q_ref[0]
        do = do_ref[...]
        lse_q = lse_ref[0]
        delta = jnp.sum(do.astype(F32) * o_ref[...].astype(F32), axis=-1, keepdims=True)
        dq = jnp.zeros((tq, HEAD_PAD), F32)
        for c in range(S // kc):
            k = k_ref[0, c * kc:(c + 1) * kc, :]
            v = v_ref[c * kc:(c + 1) * kc, :]
            p = jnp.exp2(_dot(q, k, NT) - lse_q)
            ds = (p * (_dot(do, v, NT) - delta)).astype(BF16)
            dq = dq + _dot(ds, k, NN)
            dk_ref[0, c * kc:(c + 1) * kc, :] += _dot(ds, q, TN)
            dv_ref[0, c * kc:(c + 1) * kc, :] += _dot(p.astype(BF16), do, TN)
        dq = dq * MLA_SCALE
        dqn_ref[...] = dq[:, :QK_NOPE].astype(BF16)
        dqr_ref[...] = _rope128(dq[:, QK_NOPE:], cos_ref[...], sin_ref[...]).astype(BF16)

        @pl.when(pl.program_id(1) == S // tq - 1)
        def _():
            dk = dk_ref[0] * math.log(2.0)
            dkv_out[...] = jnp.concatenate([dk[:, :QK_NOPE], dv_ref[0]], axis=1).astype(BF16)
            dkr_out[0] = dk[:, QK_NOPE:QK_HEAD].astype(BF16)

    return _pcall(
        body, name="mla_bwd", grid=(H_A, S // tq),
        in_specs=[pl.BlockSpec((1, tq, HEAD_PAD), lambda h, i: (h, i, 0)),
                  pl.BlockSpec((1, S, HEAD_PAD), lambda h, i: (h, 0, 0)),
                  pl.BlockSpec((S, V_DIM), lambda h, i: (0, 2 * h + 1)),
                  pl.BlockSpec((tq, V_DIM), lambda h, i: (i, h)),
                  pl.BlockSpec((tq, V_DIM), lambda h, i: (i, h)),
                  pl.BlockSpec((1, tq, 1), lambda h, i: (h, i, 0)),
                  pl.BlockSpec((tq, 128), lambda h, i: (i, 0)), pl.BlockSpec((tq, 128), lambda h, i: (i, 0))],
        out_specs=[pl.BlockSpec((tq, QK_NOPE), lambda h, i: (i, h)),
                   pl.BlockSpec((tq, 128), lambda h, i: (i, h)),
                   pl.BlockSpec((S, QK_NOPE + V_DIM), lambda h, i: (0, h)),
                   pl.BlockSpec((1, S, QK_ROPE), lambda h, i: (h, 0, 0))],
        out_shape=[jax.ShapeDtypeStruct((S, H_A * QK_NOPE), BF16), jax.ShapeDtypeStruct((S, H_A * 128), BF16),
                   jax.ShapeDtypeStruct((S, H_A * (QK_NOPE + V_DIM)), BF16), jax.ShapeDtypeStruct((H_A, S, QK_ROPE), BF16)],
        scratch_shapes=[pltpu.VMEM((1, S, HEAD_PAD), F32), pltpu.VMEM((1, S, V_DIM), F32)],
        dims=("parallel", "arbitrary"), comm=comm)(q_full, k_full, kv, do_a, o_a, lse, cos_p, sin_back)


WIN_SCALE = 1.0 / math.sqrt(HD_B)


WIN_PER_STEP = 4


def _win_specs(S):
    last, B = S // Q_BLOCK - 1, WIN_PER_STEP
    qspec = pl.BlockSpec((B * Q_BLOCK, H_B * HD_B), lambda i: (i, PROJ_QB))
    kspecs = [[pl.BlockSpec((Q_BLOCK, KV_B * HD_B), lambda i, d=d, c=c: (jnp.clip(B * i + d, 0, last), c)) for d in range(-1, B + 1)]
              for c in (PROJ_KB, PROJ_VB)]
    bias_spec = pl.BlockSpec((H_B, SPAN, Q_BLOCK), lambda i: (0, 0, 0))
    sink_spec = pl.BlockSpec((H_B, Q_BLOCK), lambda i: (0, 0))
    return qspec, kspecs, bias_spec, sink_spec


def _win_edge_ok(n, n_blk):
    row = lax.broadcasted_iota(jnp.int32, (SPAN, 1), 0)
    return jnp.logical_not(((n == 0) & (row < WINDOW)) | ((n == n_blk - 1) & (row >= SPAN - WINDOW)))


def _lanes4(pieces):
    return jnp.concatenate(pieces, axis=1)


def _win_probs(kg, q4t, bias_ref, sink_ref, g, edge_ok):
    bias4 = _lanes4([bias_ref[GROUP * g + j] for j in range(GROUP)])
    sink4 = _lanes4([sink_ref[GROUP * g + j:GROUP * g + j + 1, :] for j in range(GROUP)])
    s = jnp.where(edge_ok, _dot(kg, q4t, NN) + bias4, -1e30)
    m = jnp.maximum(jnp.max(s, axis=0, keepdims=True), sink4)
    p = jnp.exp(s - m)
    e_sink = jnp.exp(sink4 - m)
    inv_l = 1.0 / (jnp.sum(p, axis=0, keepdims=True) + e_sink)
    return p * inv_l, e_sink * inv_l


def _group_t(xt, g):
    return _lanes4([xt[HD_B * (GROUP * g + j):HD_B * (GROUP * g + j + 1), :] for j in range(GROUP)])


def _rows_of(ref, b):
    return ref[Q_BLOCK * b:Q_BLOCK * (b + 1), :]


def _win_fwd(proj, o_a, bias_t, sinks_b, S, comm=None):
    n_blk, B = S // Q_BLOCK, WIN_PER_STEP
    qspec, kspecs, bias_spec, sink_spec = _win_specs(S)
    rows = lambda col: pl.BlockSpec((B * Q_BLOCK, H_B * HD_B), lambda i: (i, col))

    def body(q_ref, *refs):
        k_refs, v_refs = refs[:B + 2], refs[B + 2:2 * B + 4]
        bias_ref, sink_ref, ga_ref, gb_ref, oa_ref, o_ref, mixed_ref = refs[2 * B + 4:]
        for b in range(B):
            edge_ok = _win_edge_ok(B * pl.program_id(0) + b, n_blk)
            k = jnp.concatenate([r[...] for r in k_refs[b:b + 3]], axis=0)
            vt = jnp.concatenate([r[...] for r in v_refs[b:b + 3]], axis=0).T
            qt = (_rows_of(q_ref, b).astype(F32) * WIN_SCALE).T.astype(BF16)
            parts = []
            for g in range(KV_B):
                p, _ = _win_probs(k[:, HD_B * g:HD_B * (g + 1)], _group_t(qt, g), bias_ref, sink_ref, g, edge_ok)
                o4t = _dot(vt[HD_B * g:HD_B * (g + 1), :], p.astype(BF16), NN)
                parts += [o4t[:, Q_BLOCK * j:Q_BLOCK * (j + 1)] for j in range(GROUP)]
            ob = jnp.concatenate(parts, axis=0).T
            o_ref[Q_BLOCK * b:Q_BLOCK * (b + 1), :] = ob.astype(BF16)
            ga, gb, oa = (_rows_of(r, b).astype(F32) for r in (ga_ref, gb_ref, oa_ref))
            mixed_ref[Q_BLOCK * b:Q_BLOCK * (b + 1), :] = (_sigmoid(ga) * oa + _sigmoid(gb) * ob).astype(BF16)

    return _pcall(body, name="win_fwd_mix", grid=(n_blk // B,),
                  in_specs=[qspec, *kspecs[0], *kspecs[1], bias_spec, sink_spec, rows(PROJ_GA), rows(PROJ_GB), rows(0)],
                  out_specs=[rows(0), rows(0)],
                  out_shape=[jax.ShapeDtypeStruct((S, H_B * HD_B), BF16)] * 2,
                  dims=("parallel",), comm=comm)(*[proj] * (2 * B + 5), bias_t, sinks_b, proj, proj, o_a)


def _win_bwd(proj, bias_t, sinks_b, do_b, d_proj, S):
    n_blk, B = S // Q_BLOCK, WIN_PER_STEP
    qspec, kspecs, bias_spec, sink_spec = _win_specs(S)

    def body(q_ref, *refs):
        k_refs, v_refs = refs[:B + 2], refs[B + 2:2 * B + 4]
        bias_ref, sink_ref, do_ref, _, dq_ref, dk_ref, dv_ref, dbias_ref, dsink_ref, dsink_acc = refs[2 * B + 4:]
        i = pl.program_id(0)

        @pl.when(i == 0)
        def _():
            dk_ref[...] = jnp.zeros_like(dk_ref)
            dv_ref[...] = jnp.zeros_like(dv_ref)
            dbias_ref[...] = jnp.zeros_like(dbias_ref)
            dsink_acc[...] = jnp.zeros_like(dsink_acc)

        d_bias, d_sink, dk_blocks, dv_blocks = {}, {}, [], []
        for b in range(B):
            edge_ok = _win_edge_ok(B * i + b, n_blk)
            k = jnp.concatenate([r[...] for r in k_refs[b:b + 3]], axis=0)
            v = jnp.concatenate([r[...] for r in v_refs[b:b + 3]], axis=0)
            kt = k.T
            qt = (_rows_of(q_ref, b).astype(F32) * WIN_SCALE).T.astype(BF16)
            dot_ = _rows_of(do_ref, b).astype(F32).T.astype(BF16)
            dq_parts, dks, dvs = [], [], []
            for g in range(KV_B):
                kg, vg = k[:, HD_B * g:HD_B * (g + 1)], v[:, HD_B * g:HD_B * (g + 1)]
                q4t, do4t = _group_t(qt, g), _group_t(dot_, g)
                p, p_sink = _win_probs(kg, q4t, bias_ref, sink_ref, g, edge_ok)
                dp = _dot(vg, do4t, NN)
                delta = jnp.sum(p * dp, axis=0, keepdims=True)
                ds = p * (dp - delta)
                d_bias[g] = ds if b == 0 else d_bias[g] + ds
                d_sink[g] = -p_sink * delta if b == 0 else d_sink[g] - p_sink * delta
                dsb = ds.astype(BF16)
                dq4t = _dot(kt[HD_B * g:HD_B * (g + 1), :], dsb, NN) * WIN_SCALE
                dq_parts += [dq4t[:, Q_BLOCK * j:Q_BLOCK * (j + 1)] for j in range(GROUP)]
                dks.append(_dot(dsb, q4t, NT))
                dvs.append(_dot(p.astype(BF16), do4t, NT))
            dq_ref[Q_BLOCK * b:Q_BLOCK * (b + 1), :] = jnp.concatenate(dq_parts, axis=0).T.astype(BF16)
            dk_blocks.append(jnp.concatenate(dks, axis=1))
            dv_blocks.append(jnp.concatenate(dvs, axis=1))

        for g in range(KV_B):
            for j in range(GROUP):
                dbias_ref[GROUP * g + j] += d_bias[g][:, Q_BLOCK * j:Q_BLOCK * (j + 1)]
            dsink_acc[g:g + 1, :] += d_sink[g]

        def overlap(blocks):
            out = blocks[0]
            for blk in blocks[1:]:
                keep = out.shape[0] - 2 * Q_BLOCK
                out = jnp.concatenate([out[:keep], out[keep:] + blk[:2 * Q_BLOCK], blk[2 * Q_BLOCK:]], axis=0)
            return out

        rows = pl.ds(pl.multiple_of(i * (B * Q_BLOCK), B * Q_BLOCK), (B + 2) * Q_BLOCK)
        dk_ref[rows, :] += overlap(dk_blocks)
        dv_ref[rows, :] += overlap(dv_blocks)

        @pl.when(i == n_blk // B - 1)
        def _():
            acc = dsink_acc[...]
            dsink_ref[...] = jnp.concatenate(
                [jnp.sum(acc[:, Q_BLOCK * j:Q_BLOCK * (j + 1)], axis=1, keepdims=True) for j in range(GROUP)], axis=1)

    whole = lambda shape: pl.BlockSpec(shape, lambda i: (0,) * len(shape))
    return _pcall(
        body, name="win_bwd", grid=(n_blk // B,),
        in_specs=[qspec, *kspecs[0], *kspecs[1], bias_spec, sink_spec, pl.BlockSpec((B * Q_BLOCK, H_B * HD_B), lambda i: (i, 0)), ANY],
        out_specs=[qspec, whole((S + 2 * WINDOW, KV_B * HD_B)),
                   whole((S + 2 * WINDOW, KV_B * HD_B)), whole((H_B, SPAN, Q_BLOCK)), whole((KV_B, GROUP))],
        out_shape=[jax.ShapeDtypeStruct((S, PROJ_P), BF16), jax.ShapeDtypeStruct((S + 2 * WINDOW, KV_B * HD_B), F32),
                   jax.ShapeDtypeStruct((S + 2 * WINDOW, KV_B * HD_B), F32), jax.ShapeDtypeStruct((H_B, SPAN, Q_BLOCK), F32),
                   jax.ShapeDtypeStruct((KV_B, GROUP), F32)],
        scratch_shapes=[pltpu.VMEM((KV_B, GROUP * Q_BLOCK), F32)],
        dims=("arbitrary",), aliases={2 * B + 8: 0})(*[proj] * (2 * B + 5), bias_t, sinks_b, do_b, d_proj)


def _bias_table(rel_bias_t, onehot_t, in_band):
    def body(rb_ref, oh_ref, band_ref, o_ref):
        t = lax.dot_general(rb_ref[...], oh_ref[...], NN, preferred_element_type=F32, precision=lax.Precision.HIGHEST)
        o_ref[...] = jnp.where(band_ref[...] > 0.5, t, -1e30)

    n = onehot_t.shape[1]
    tn = _tile(n, 8192)
    return _pcall(body, name="bias_table", grid=(n // tn,),
                  in_specs=[pl.BlockSpec((H_B, NUM_BUCKETS), lambda j: (0, 0)), pl.BlockSpec((NUM_BUCKETS, tn), lambda j: (0, j)),
                            pl.BlockSpec((1, tn), lambda j: (0, j))],
                  out_specs=pl.BlockSpec((H_B, tn), lambda j: (0, j)),
                  out_shape=jax.ShapeDtypeStruct((H_B, n), F32), dims=("parallel",))(rel_bias_t, onehot_t, in_band)


def _bias_table_bwd(dbias, onehot_t):
    n = onehot_t.shape[1]
    tk = _tile(n, 8192)

    def body(d_ref, oh_ref, o_ref):
        @pl.when(pl.program_id(0) == 0)
        def _():
            o_ref[...] = jnp.zeros_like(o_ref)

        o_ref[...] += lax.dot_general(d_ref[...], oh_ref[...], NT, preferred_element_type=F32, precision=lax.Precision.HIGHEST)

    return _pcall(body, name="bias_table_bwd", grid=(n // tk,),
                  in_specs=[pl.BlockSpec((H_B, tk), lambda j: (0, j)), pl.BlockSpec((NUM_BUCKETS, tk), lambda j: (0, j))],
                  out_specs=pl.BlockSpec((H_B, NUM_BUCKETS), lambda j: (0, 0)),
                  out_shape=jax.ShapeDtypeStruct((H_B, NUM_BUCKETS), F32), dims=("arbitrary",))(dbias, onehot_t)


CONV_STRIP = 128
N_STRIPS = D_FF // CONV_STRIP
CONV_ROWS = 128
HALO = 8


def _strip(rows, half):
    return pl.BlockSpec((rows, CONV_STRIP), lambda j: (0, j + half * N_STRIPS))


def _fill_padded(pad_ref, src_ref, halo, S):
    pad_ref[0:halo, :] = jnp.zeros((halo, CONV_STRIP), F32)
    pad_ref[halo + S:2 * halo + S, :] = jnp.zeros((halo, CONV_STRIP), F32)
    pad_ref[halo:halo + S, :] = src_ref[...].astype(F32)


def _conv_gate_fwd(u, conv_w, conv_b, S):
    R = min(CONV_ROWS, S)

    def body(ug_ref, uv_ref, wg_ref, wv_ref, bg_ref, bv_ref, a_ref, gpad, vpad):
        _fill_padded(gpad, ug_ref, HALO, S)
        _fill_padded(vpad, uv_ref, HALO, S)
        wg, wv, bg, bv = wg_ref[...], wv_ref[...], bg_ref[...], bv_ref[...]

        def conv(pad_ref, r0, w, b):
            dn, mid, up = (pad_ref[pl.ds(r0 + HALO + d, R), :] for d in (-1, 0, 1))
            return dn * w[0:1, :] + mid * w[1:2, :] + up * w[2:3, :] + b

        def step(c, carry):
            r0 = pl.multiple_of(c * R, R)
            g = conv(gpad, r0, wg, bg)
            val = conv(vpad, r0, wv, bv)
            a_ref[pl.ds(r0, R), :] = (g * _sigmoid(g) * val).astype(BF16)
            return carry

        lax.fori_loop(0, S // R, step, 0)

    return _pcall(body, name="conv_gate_fwd", grid=(N_STRIPS,),
                  in_specs=[_strip(S, 0), _strip(S, 1), _strip(3, 0), _strip(3, 1), _strip(1, 0), _strip(1, 1)],
                  out_specs=_strip(S, 0), out_shape=jax.ShapeDtypeStruct((S, D_FF), BF16),
                  scratch_shapes=[pltpu.VMEM((S + 2 * HALO, CONV_STRIP), F32)] * 2,
                  dims=("parallel",))(u, u, conv_w, conv_w, conv_b, conv_b)


def _conv_gate_bwd(u, conv_w, conv_b, da, S):
    R = min(CONV_ROWS, S)
    n = R + 2 * HALO

    def body(ug_ref, uv_ref, wg_ref, wv_ref, bg_ref, bv_ref, da_ref, dug_ref, duv_ref, dwg_ref, dwv_ref, dbg_ref, dbv_ref,
             gpad, vpad, dapad):
        _fill_padded(gpad, ug_ref, 2 * HALO, S)
        _fill_padded(vpad, uv_ref, 2 * HALO, S)
        _fill_padded(dapad, da_ref, HALO, S)
        wg, wv, bg, bv = wg_ref[...], wv_ref[...], bg_ref[...], bv_ref[...]

        def conv(pad_ref, r0, w, b):
            dn, mid, up = (pad_ref[pl.ds(r0 + HALO + d, n), :] for d in (-1, 0, 1))
            return dn * w[0:1, :] + mid * w[1:2, :] + up * w[2:3, :] + b, mid[HALO:HALO + R]

        def conv_bwd(duc, u_mid, w, r0, du_ref):
            dn, mid, up = pltpu.roll(duc, 1, axis=0)[HALO:HALO + R], duc[HALO:HALO + R], pltpu.roll(duc, n - 1, axis=0)[HALO:HALO + R]
            du_ref[pl.ds(r0, R), :] = (up * w[0:1, :] + mid * w[1:2, :] + dn * w[2:3, :]).astype(BF16)
            dw = jnp.concatenate([jnp.sum(up * u_mid, axis=0, keepdims=True), jnp.sum(mid * u_mid, axis=0, keepdims=True),
                                  jnp.sum(dn * u_mid, axis=0, keepdims=True)], axis=0)
            return dw, jnp.sum(mid, axis=0, keepdims=True)

        def step(c, carry):
            dw_g, db_g, dw_v, db_v = carry
            r0 = pl.multiple_of(c * R, R)
            g, ug_mid = conv(gpad, r0, wg, bg)
            val, uv_mid = conv(vpad, r0, wv, bv)
            da_ext = dapad[pl.ds(r0, n), :]
            sg = _sigmoid(g)
            ddw_v, ddb_v = conv_bwd(da_ext * (g * sg), uv_mid, wv, r0, duv_ref)
            ddw_g, ddb_g = conv_bwd(da_ext * val * (sg * (1.0 + g * (1.0 - sg))), ug_mid, wg, r0, dug_ref)
            return dw_g + ddw_g, db_g + ddb_g, dw_v + ddw_v, db_v + ddb_v

        z3, z1 = jnp.zeros((3, CONV_STRIP), F32), jnp.zeros((1, CONV_STRIP), F32)
        dwg_ref[...], dbg_ref[...], dwv_ref[...], dbv_ref[...] = lax.fori_loop(0, S // R, step, (z3, z1, z3, z1))

    half = lambda r, dt: (_strip(r, 0), jax.ShapeDtypeStruct((r, D_FF), dt))
    outs = [half(S, BF16), half(S, BF16), half(3, F32), half(3, F32), half(1, F32), half(1, F32)]
    return _pcall(
        body, name="conv_gate_bwd", grid=(N_STRIPS,),
        in_specs=[_strip(S, 0), _strip(S, 1), _strip(3, 0), _strip(3, 1), _strip(1, 0), _strip(1, 1), _strip(S, 0)],
        out_specs=[o[0] for o in outs], out_shape=[o[1] for o in outs],
        scratch_shapes=[pltpu.VMEM((S + 4 * HALO, CONV_STRIP), F32)] * 2 + [pltpu.VMEM((S + 2 * HALO, CONV_STRIP), F32)],
        dims=("parallel",))(u, u, conv_w, conv_w, conv_b, conv_b, da)


MESH = pl.DeviceIdType.MESH
ANY = pl.BlockSpec(memory_space=pl.ANY)


def _place():
    return lax.axis_index("x"), lax.axis_index("y"), lax.axis_index("c")


def _gather2(ins, outs, sems, finish):
    send_sems, recv_sems, local_sems = sems
    n_arr = len(ins)
    x, y, c = _place()
    me, sibling = (x, y, c), (x, y, 1 - c)
    chips = [(1 - x, y), (x, 1 - y), (1 - x, 1 - y)]

    def slot(a, p):
        return outs[a].at[4 * p[0] + 2 * p[1] + p[2]]

    def copy(a, k, block, to, src=None):
        return pltpu.make_async_remote_copy(
            src_ref=slot(a, block) if src is None else src, dst_ref=slot(a, block),
            send_sem=send_sems.at[a, k], recv_sem=recv_sems.at[a, k], device_id=to, device_id_type=MESH)

    mine = [pltpu.make_async_copy(ins[a], slot(a, me), local_sems.at[a]) for a in range(n_arr)]
    first = []
    for a in range(n_arr):
        first.append(copy(a, 0, me, sibling, src=ins[a]))
        first += [copy(a, 1 + j, me, (*chip, c), src=ins[a]) for j, chip in enumerate(chips)]
    if not finish:
        for cp in mine + first:
            cp.start()
        return
    passed = []
    for j, chip in enumerate(chips):
        for a in range(n_arr):
            copy(a, 1 + j, (*chip, c), me).wait_recv()
            cp = copy(a, 4 + j, (*chip, c), sibling)
            cp.start()
            passed.append(cp)
    for a in range(n_arr):
        copy(a, 0, sibling, me).wait_recv()
        for j, chip in enumerate(chips):
            copy(a, 4 + j, (*chip, 1 - c), me).wait_recv()
    for cp in first + passed:
        cp.wait_send()
    for cp in mine:
        cp.wait()


def _xchg_out_shapes(stacked, replicated):
    return ([jax.ShapeDtypeStruct(s.shape, s.dtype) for s in stacked]
            + [jax.ShapeDtypeStruct((N_DEV, *r.shape), r.dtype) for r in replicated])


def _xchg_sems(n_arr):
    return [pltpu.SemaphoreType.DMA((n_arr, 7)), pltpu.SemaphoreType.DMA((n_arr, 7)), pltpu.SemaphoreType.DMA((n_arr,))]


def _xchg_copies(ins, outs, sems, n_st, with_recv):
    send_sems, recv_sems, local_sems = sems
    n_arr = len(ins)
    x, y, c = _place()
    me = 4 * x + 2 * y + c

    def src(a, idx):
        return ins[a].at[idx] if a < n_st else ins[a]

    mine = [pltpu.make_async_copy(src(a, me), outs[a].at[me], local_sems.at[a]) for a in range(n_arr)]
    pairs = []
    for k in range(1, N_DEV):
        px, py, pc = x ^ (k >> 2), y ^ ((k >> 1) & 1), c ^ (k & 1)
        peer = 4 * px + 2 * py + pc
        for a in range(n_arr):
            sems_k = dict(send_sem=send_sems.at[a, k - 1], recv_sem=recv_sems.at[a, k - 1], device_id_type=MESH)
            send = pltpu.make_async_remote_copy(src_ref=src(a, peer), dst_ref=outs[a].at[me], device_id=(px, py, pc), **sems_k)
            recv = None
            if with_recv:
                recv = pltpu.make_async_remote_copy(src_ref=src(a, peer), dst_ref=outs[a].at[peer], device_id=(x, y, c), **sems_k)
            pairs.append((send, recv))
    return mine, pairs


def _xchg_start(ins, outs, sems, n_st):
    mine, pairs = _xchg_copies(ins, outs, sems, n_st, False)
    for cp in mine:
        cp.start()
    for send, _ in pairs:
        send.start()


def _xchg_finish(ins, outs, sems, n_st):
    mine, pairs = _xchg_copies(ins, outs, sems, n_st, True)
    for _, recv in pairs:
        recv.wait_recv()
    for send, _ in pairs:
        send.wait_send()
    for cp in mine:
        cp.wait()


def _exchange(stacked, replicated, name):
    _, landed = _pcall(lambda: None, name=name, grid=(), in_specs=[], out_specs=[], out_shape=[], comm=(stacked, replicated))()
    return landed


HBM = pl.BlockSpec(memory_space=pltpu.HBM)
SEMS = pl.BlockSpec(memory_space=pltpu.SEMAPHORE)
SIDE_EFFECT = pltpu.SideEffectType.DATAFLOW_SIDE_EFFECTING


N_SPLIT_SEMS = 2 * (N_DEV - 1)


def _split_copies(src, land, sems, with_recv):
    x, y, c = _place()
    me = 4 * x + 2 * y + c
    pairs = []
    for k in range(1, N_DEV):
        px, py, pc = x ^ (k >> 2), y ^ ((k >> 1) & 1), c ^ (k & 1)
        peer = 4 * px + 2 * py + pc
        sems_k = dict(send_sem=sems[k - 1], recv_sem=sems[N_DEV - 1 + k - 1], device_id_type=MESH)
        send = pltpu.make_async_remote_copy(src_ref=src.at[peer], dst_ref=land.at[me], device_id=(px, py, pc), **sems_k)
        recv = None
        if with_recv:
            recv = pltpu.make_async_remote_copy(src_ref=src.at[peer], dst_ref=land.at[peer], device_id=(x, y, c), **sems_k)
        pairs.append((send, recv))
    return pairs


def _exchange_start(stacked, name):
    def body(src, land, *rest):
        for send, _ in _split_copies(src, land, rest[:N_SPLIT_SEMS], False):
            send.start()
        rest[-1][...] = jnp.zeros_like(rest[-1])

    shape = pltpu.HBM(stacked.shape, stacked.dtype)
    res = pl.pallas_call(
        body, name=name, in_specs=[HBM, HBM],
        out_shape=(*[pltpu.SemaphoreType.DMA(())] * N_SPLIT_SEMS, shape, shape, jax.ShapeDtypeStruct((8, 128), F32)),
        out_specs=(*[SEMS] * N_SPLIT_SEMS, HBM, HBM, pl.BlockSpec(memory_space=pltpu.VMEM)),
        input_output_aliases={0: N_SPLIT_SEMS, 1: N_SPLIT_SEMS + 1},
        compiler_params=pltpu.CompilerParams(has_side_effects=SIDE_EFFECT),
    )(pltpu.with_memory_space_constraint(stacked, pltpu.HBM),
      pltpu.with_memory_space_constraint(lax.empty(stacked.shape, stacked.dtype), pltpu.HBM))
    return res[:N_SPLIT_SEMS], res[N_SPLIT_SEMS], res[N_SPLIT_SEMS + 1], res[-1]


def _exchange_wait(sems, src, land, after, name):
    def body(src_ref, land_ref, *rest):
        for send, recv in _split_copies(src_ref, land_ref, rest[:N_SPLIT_SEMS], True):
            send.wait_send()
            recv.wait_recv()

    shape = pltpu.HBM(src.shape, src.dtype)
    return pl.pallas_call(
        body, name=name, in_specs=[HBM, HBM, *[SEMS] * N_SPLIT_SEMS, ANY],
        out_shape=(shape, shape), out_specs=(HBM, HBM), input_output_aliases={0: 0, 1: 1},
        compiler_params=pltpu.CompilerParams(has_side_effects=SIDE_EFFECT))(src, land, *sems, after)[1]


def _adamw(parts, w, m, v, name):
    _, R, C = w.shape
    tr = R if R <= 512 else max(t for t in range(16, 513, 16) if R % t == 0)
    pr = tr if parts.shape[1] == R else -(-R // 16) * 16
    assert pr == tr or tr == R

    def body(p_ref, w_ref, m_ref, v_ref, g_ref, d_ref, nm_ref, nv_ref):
        g = p_ref[0].astype(F32)[:tr]
        for s in range(1, N_DEV):
            g = g + p_ref[s].astype(F32)[:tr]
        m2 = ADAM_B1 * m_ref[0] + (1.0 - ADAM_B1) * g
        v2 = ADAM_B2 * v_ref[0] + (1.0 - ADAM_B2) * (g * g)
        m_hat = m2 / (1.0 - ADAM_B1 ** ADAM_STEP)
        v_hat = v2 / (1.0 - ADAM_B2 ** ADAM_STEP)
        g_ref[0] = g
        d_ref[0] = -ADAM_LR * (m_hat / (jnp.sqrt(v_hat) + ADAM_EPS) + ADAM_WD * w_ref[0])
        nm_ref[0] = m2
        nv_ref[0] = v2

    blk = pl.BlockSpec((1, tr, C), lambda i: (0, i, 0))
    return _pcall(body, name=name, grid=(R // tr,),
                  in_specs=[pl.BlockSpec((N_DEV, pr, C), lambda i: (0, i, 0)), blk, blk, blk],
                  out_specs=[blk] * 4, out_shape=[jax.ShapeDtypeStruct((1, R, C), F32)] * 4,
                  dims=("parallel",))(parts, w, m, v)


def _t5_bucket(rel):
    nb = NUM_BUCKETS // 2
    max_exact = nb // 2
    base = (rel > 0).astype(jnp.int32) * nb
    n = jnp.abs(rel)
    nf = jnp.maximum(n, 1).astype(jnp.float32)
    large = max_exact + (jnp.log(nf / max_exact) / math.log(MAX_DISTANCE / max_exact) * (nb - max_exact)).astype(jnp.int32)
    large = jnp.minimum(large, nb - 1)
    return base + jnp.where(n < max_exact, n, large)


def _unstack_cols(g):
    return jnp.transpose(g, (1, 0, 2)).reshape(g.shape[1], N_DEV * g.shape[2])


def _stack_cols(w, n=N_DEV):
    R = w.shape[0]
    return jnp.transpose(w.reshape(R, n, w.shape[1] // n), (1, 0, 2))


def _stack_halves(g, v):
    return jnp.concatenate([_stack_cols(g, N_DEV // 2), _stack_cols(v, N_DEV // 2)], axis=0)


def kernel(x, positions, norm1_g, w_in, q_a_norm_g, w_q_b, kv_a_norm_g, w_kv_b, rel_bias, sinks, w_out, norm2_g, w_up, conv_w, conv_b, w_down, final_norm_g, loss_target, m_norm1_g, m_w_in, m_q_a_norm_g, m_w_q_b, m_kv_a_norm_g, m_w_kv_b, m_rel_bias, m_sinks, m_w_out, m_norm2_g, m_w_up, m_conv_w, m_conv_b, m_w_down, m_final_norm_g, v_norm1_g, v_w_in, v_q_a_norm_g, v_w_q_b, v_kv_a_norm_g, v_w_kv_b, v_rel_bias, v_sinks, v_w_out, v_norm2_g, v_w_up, v_conv_w, v_conv_b, v_w_down, v_final_norm_g):
    S = x.shape[1]
    x = x[0]
    target = loss_target[0]
    TM = 256

    tr = lambda w: jnp.swapaxes(w, 1, 2)
    half = QK_ROPE // 2
    inv_freq = ROPE_THETA ** (-jnp.arange(half, dtype=F32) / half)
    inv_tile = jnp.concatenate([inv_freq, inv_freq, jnp.zeros((128 - QK_ROPE,), F32)])[None, :]
    sign_tile = jnp.concatenate([-jnp.ones((half,), F32), jnp.ones((half,), F32), jnp.zeros((128 - QK_ROPE,), F32)])[None, :]

    def norm1_tables(a, g, pos, inv, sign):
        ang = pos * inv
        rot = jnp.sin(ang) * sign
        return _rms(a, g), jnp.cos(ang), rot, -rot

    (h1, cos_p, sin_fwd, sin_back), (g_in,) = _rowwise(
        norm1_tables, "norm1_gather", S, TM,
        [_rows(x), _whole(norm1_g), _rows(positions.astype(F32)[:, None]), _whole(inv_tile), _whole(sign_tile)],
        [("rows", D_MODEL, BF16)] + [("rows", 128, F32)] * 3, gather=[tr(w_in)[0].astype(BF16)])
    late_weights = [w_out[0].astype(BF16), tr(w_up)[0].astype(BF16), conv_w[0]]
    wi = g_in.reshape(W_IN_COLS, D_MODEL)
    c0, c1, c2, c3, c4, c5 = (sum(W_IN_SIZES[:i + 1]) for i in range(6))
    w_in_pt = jnp.concatenate([wi[c4:c5], wi[c5:], wi[c1:c2], wi[:c0], wi[c2:c3], wi[c3:c4],
                               wi[c0:c0 + KV_LORA], wi[c0 + KV_LORA:c1], jnp.zeros((64, D_MODEL), BF16)], axis=0)

    cos, sin = cos_p[:, :half], sin_back[:, :half]
    qa = jnp.arange(Q_BLOCK, dtype=jnp.int32)[:, None]
    kc = jnp.arange(SPAN, dtype=jnp.int32)[None, :]
    rel = (kc - WINDOW - qa).T
    in_band = (jnp.abs(rel) <= WINDOW).astype(F32).reshape(1, Q_BLOCK * SPAN)
    onehot_t = (_t5_bucket(rel).reshape(1, Q_BLOCK * SPAN) == jnp.arange(NUM_BUCKETS, dtype=jnp.int32)[:, None]).astype(F32)
    bias_t = _bias_table(rel_bias.T, onehot_t, in_band).reshape(H_B, SPAN, Q_BLOCK)
    sinks_b = jnp.broadcast_to(sinks.reshape(H_B, 1), (H_B, Q_BLOCK))

    proj, (g_qb, g_kvb) = _matmul(h1, w_in_pt, "nt", BF16, "proj", comm=([], [tr(w_q_b)[0].astype(BF16), w_kv_b[0].astype(BF16)]))
    wq = g_qb.reshape(H_A, QK_HEAD, Q_LORA)
    w_qb_pt = jnp.concatenate([wq[:, :QK_NOPE].reshape(H_A * QK_NOPE, Q_LORA),
                               jnp.pad(wq[:, QK_NOPE:], ((0, 0), (0, 128 - QK_ROPE), (0, 0))).reshape(H_A * 128, Q_LORA)], axis=0)
    w_kvb = _unstack_cols(g_kvb)

    def lat_fn(qlat, ckv, kr, gq, gkv, cs, sn):
        return _rms(qlat, gq), _rms(ckv, gkv), _rope128(kr, cs, sn)

    qn, ckvn, k_rope = _rowwise(lat_fn, "latents", S, TM,
                                [_rows(proj, 256, PROJ_QLAT), _rows(proj, 128, PROJ_CKV), _rows(proj, 128, PROJ_KROPE),
                                 _whole(q_a_norm_g), _whole(kv_a_norm_g), _rows(cos_p), _rows(sin_fwd)],
                                [("rows", Q_LORA, BF16), ("rows", KV_LORA, BF16), ("rows", 128, BF16)])
    def q_heads_fn(q, cs, sn):
        q = q * MLA_PRESCALE
        return (jnp.concatenate([jnp.concatenate([q[:, 128 * h:128 * (h + 1)], _rope128(q[:, 128 * (H_A + h):128 * (H_A + h + 1)], cs, sn)],
                                                 axis=1)[None] for h in range(H_A)], axis=0),)

    (q_full,) = _matmul(qn, w_qb_pt, "nt", None, "q_up_heads", tm=512, tn=2048,
                        epi=(q_heads_fn, [_rows(cos_p), _rows(sin_fwd)], [("heads", H_A, HEAD_PAD, BF16)]))

    def k_heads_fn(kvf, kr):
        return kvf, jnp.concatenate([jnp.concatenate([kvf[:, 256 * h:256 * h + QK_NOPE], kr], axis=1)[None] for h in range(H_A)], axis=0)

    kv, k_full = _matmul(ckvn, w_kvb, "nn", None, "kv_up_heads", tm=512, tn=2048,
                         epi=(k_heads_fn, [_rows(k_rope)], [("rows", H_A * (QK_NOPE + V_DIM), BF16), ("heads", H_A, HEAD_PAD, BF16)]))
    (o_a, lse), (g_out, g_up, g_cw) = _mla_fwd(q_full, k_full, kv, S, comm=([], late_weights))
    w_out_f = g_out.reshape(D_MODEL, D_MODEL)
    w_up_t = g_up.reshape(2 * D_FF, D_MODEL)
    conv_w_f = _unstack_cols(g_cw)

    (o_b, mixed), (g_down,) = _win_fwd(proj, o_a, bias_t, sinks_b, S, comm=([], [w_down[0].astype(BF16)]))
    w_down_f = g_down.reshape(D_FF, D_MODEL)

    x1, h2 = _matmul(mixed, w_out_f, "nn", None, "out_proj", residual=x, tm=512,
                     epi=(lambda a, g: (a, _rms(a, g)), [_whole(norm2_g)], [("rows", D_MODEL, F32), ("rows", D_MODEL, BF16)]))
    u = _matmul(h2, w_up_t, "nt", BF16, "ffn_up", tn=1408)
    act = _conv_gate_fwd(u, conv_w_f, conv_b, S)

    def final_fn(a, g, t):
        err = _rms(a, g) - t
        loss = 0.5 * jnp.sum(jnp.mean(err * err, axis=-1, keepdims=True), axis=0, keepdims=True)
        dx, dg = _rms_bwd(err * (1.0 / D_MODEL), a, g)
        return dx, dx, dg, jnp.broadcast_to(loss, (1, 128))

    gfin = final_norm_g.reshape(1, D_MODEL)
    dx2, dx2_b, d_gfin, loss_row = _matmul(
        act, w_down_f, "nn", None, "ffn_down_loss", residual=x1, tm=512,
        epi=(final_fn, [_whole(gfin), _rows(target)],
             [("rows", D_MODEL, F32), ("rows", D_MODEL, BF16), ("acc", 1, D_MODEL), ("acc", 1, 128)]))
    d_act = _matmul(dx2_b, w_down_f, "nt", BF16, "ffn_down_dx", tn=1408)
    d_w_down = _matmul(act, dx2_b, "tn", BF16, "ffn_down_dw")
    du_g, du_v, dcw_g, dcw_v, dcb_g, dcb_v = _conv_gate_bwd(u, conv_w_f, conv_b, d_act, S)
    d_conv_b = jnp.concatenate([dcb_g, dcb_v], axis=1)

    def norm_bwd_fn(dh, a, g, dres):
        dx, dg = _rms_bwd(dh, a, g)
        dx = dx + dres
        return dx, dx, dg

    dx1, dx1_b, d_g2 = _matmul(du_g, w_up_t, "nn", None, "ffn_up_dx_norm2_bwd", tm=256, a2=du_v,
                               epi=(norm_bwd_fn, [_rows(x1), _whole(norm2_g), _rows(dx2)],
                                    [("rows", D_MODEL, F32), ("rows", D_MODEL, BF16), ("acc", 1, D_MODEL)]))
    d_w_up_t = _matmul(du_g, h2, "tn", BF16, "ffn_up_dw", tm=256, a2=du_v)
    d_w_out = _matmul(mixed, dx1_b, "tn", BF16, "out_proj_dw", tm=512)

    def gate_bwd_fn(dm, ga, gb, oa, ob):
        sa, sb = _sigmoid(ga), _sigmoid(gb)
        return jnp.concatenate([dm * oa * sa * (1.0 - sa), dm * ob * sb * (1.0 - sb)], axis=1), dm * sa, dm * sb

    d_proj, do_a, do_b = _matmul(
        dx1_b, w_out_f, "nt", None, "out_proj_dx_gate_bwd", tm=512,
        epi=(gate_bwd_fn, [_rows(proj, 1024, PROJ_GA), _rows(proj, 1024, PROJ_GB), _rows(o_a), _rows(o_b)],
             [("cols", 2 * D_MODEL, 0, PROJ_P, BF16), ("rows", D_MODEL, BF16), ("rows", D_MODEL, BF16)]))

    d_proj, dk_acc, dv_acc, d_bias, d_sinks_g = _win_bwd(proj, bias_t, sinks_b, do_b, d_proj, S)
    d_sinks = d_sinks_g.reshape(1, H_B)

    early = [d_w_out.reshape(N_DEV, D_MODEL // N_DEV, D_MODEL), d_w_up_t.reshape(N_DEV, 2 * D_FF // N_DEV, D_MODEL),
             d_w_down.reshape(N_DEV, D_FF // N_DEV, D_MODEL), _stack_halves(dcw_g, dcw_v)]
    (dq_nope, dq_rope, dkv, dkr_heads), recv_early = _mla_bwd(q_full, k_full, kv, do_a, o_a, lse, cos_p, sin_back, S, comm=(early, []))

    d_qn = _matmul(dq_nope, w_qb_pt, "nn", F32, "q_up_dx", a2=dq_rope)
    d_ckvn = _matmul(dkv, w_kvb, "nt", F32, "kv_up_dx")

    def lat_bwd_fn(dqn, dckvn, dkr_h, cs, sn, qlat, ckv, gq, gkv, dkb, dvb):
        dql, dgq = _rms_bwd(dqn, qlat, gq)
        dck, dgkv = _rms_bwd(dckvn, ckv, gkv)
        dkr = dkr_h[0]
        for h in range(1, H_A):
            dkr = dkr + dkr_h[h]
        r1, r2 = _rope_bwd(dkr[:, :half], dkr[:, half:], cs, sn)
        tail = jnp.concatenate([dql, dkb, dvb, dck, r1, r2, jnp.zeros_like(dkr)], axis=1)
        return tail, dgq, dgkv

    shifted = lambda arr: (arr, lambda tm: pl.BlockSpec((tm, arr.shape[1]), lambda i, *_: (i + WINDOW // tm, 0)))
    TL = min(128, S)
    d_proj, d_gq, d_gkv = _rowwise(lat_bwd_fn, "latents_bwd", S, TL,
                                   [_rows(d_qn), _rows(d_ckvn), _heads(dkr_heads), _rows(cos), _rows(sin), _rows(proj, 256, PROJ_QLAT), _rows(proj, 128, PROJ_CKV),
                                    _whole(q_a_norm_g), _whole(kv_a_norm_g), shifted(dk_acc), shifted(dv_acc)],
                                   [("cols", 1024, 3, PROJ_P, BF16), ("acc", 1, Q_LORA), ("acc", 1, KV_LORA)], into=(d_proj, 0))
    dp = _matmul(d_proj, h1, "tn", BF16, "proj_dw", tm=512)

    late = jnp.concatenate([dp[3072:3328], dp[3840:3968], dp[3968:4032], dp[2048:3072], dp[3328:3584],
                            dp[3584:3840], dp[0:1024], dp[1024:2048]], axis=0).reshape(N_DEV, W_IN_COLS // N_DEV, D_MODEL)
    late_sems, late_src, late_land, started = _exchange_start(late, "late_grads_start")

    def norm1_bwd_fn(dh, a, g, dres):
        dx, dg = _rms_bwd(dh, a, g)
        return dx + dres, dg

    grad_x, d_g1 = _matmul(
        d_proj, w_in_pt, "nn", None, "proj_dx_norm1_bwd", tm=512,
        epi=(norm1_bwd_fn, [_rows(x), _whole(norm1_g + started[:1, :1]), _rows(dx1)], [("rows", D_MODEL, F32), ("acc", 1, D_MODEL)]))

    transposed = ("w_in", "w_q_b", "w_up")
    ready_names = ["w_out", "w_up", "w_down", "conv_w"]
    ready_wmv = [(w_out, m_w_out, v_w_out), (tr(w_up), tr(m_w_up), tr(v_w_up)), (w_down, m_w_down, v_w_down), (conv_w, m_conv_w, v_conv_w)]
    recv_early, _ = lax.optimization_barrier((list(recv_early), started))
    big = {n: _adamw(r, *wmv, "adamw_" + n) for n, r, wmv in zip(ready_names, recv_early, ready_wmv)}

    (d_bias, dq_nope_l, dkv_l), _ = lax.optimization_barrier(((d_bias, dq_nope, dkv), started))
    d_rel_bias = _bias_table_bwd(d_bias.reshape(H_B, Q_BLOCK * SPAN), onehot_t).T
    d_w_qb_pt = _matmul(dq_nope_l, qn, "tn", BF16, "q_up_dw", tm=512, a2=dq_rope)
    d_w_kvb = _matmul(ckvn, dkv_l, "tn", BF16, "kv_up_dw", tn=2048)
    d_w_qb_t = jnp.concatenate([d_w_qb_pt[:H_A * QK_NOPE].reshape(H_A, QK_NOPE, Q_LORA),
                                d_w_qb_pt[H_A * QK_NOPE:].reshape(H_A, 128, Q_LORA)[:, :QK_ROPE]], axis=1)

    after = lax.optimization_barrier([big[n][0] for n in ready_names] + [d_rel_bias, d_w_qb_t, d_w_kvb])
    landed = _exchange_wait(late_sems, late_src, late_land, after[-1], "late_grads_wait")
    me = 4 * lax.axis_index("x") + 2 * lax.axis_index("y") + lax.axis_index("c")
    landed = lax.dynamic_update_slice_in_dim(landed, lax.dynamic_slice_in_dim(late, me, 1, axis=0), me, axis=0)
    big["w_in"] = _adamw(landed, tr(w_in), tr(m_w_in), tr(v_w_in), "adamw_w_in")

    small_parts = [d_g1, d_gq, d_gkv, d_rel_bias.reshape(1, NUM_BUCKETS * H_B), d_sinks, d_g2, d_conv_b, d_gfin, loss_row[:, :1]]
    small = jnp.concatenate(small_parts, axis=1)
    n_small = small.shape[1]
    pad = (-n_small) % 128
    small = jnp.pad(small, ((0, 0), (0, pad)))
    small, _ = lax.optimization_barrier((small, [landed, *after]))
    recv_qb, recv_kvb, recv_small = _exchange([after[-2], _stack_cols(after[-1])], [small], "exchange_small_grads")
    big["w_q_b"] = _adamw(recv_qb, tr(w_q_b), tr(m_w_q_b), tr(v_w_q_b), "adamw_w_q_b")
    big["w_kv_b"] = _adamw(recv_kvb, w_kv_b, m_w_kv_b, v_w_kv_b, "adamw_w_kv_b")

    def flat(a):
        return a.reshape(1, -1)

    small_w = [norm1_g, q_a_norm_g, kv_a_norm_g, rel_bias, sinks, norm2_g, conv_b, final_norm_g]
    small_m = [m_norm1_g, m_q_a_norm_g, m_kv_a_norm_g, m_rel_bias, m_sinks, m_norm2_g, m_conv_b, m_final_norm_g]
    small_v = [v_norm1_g, v_q_a_norm_g, v_kv_a_norm_g, v_rel_bias, v_sinks, v_norm2_g, v_conv_b, v_final_norm_g]
    cat = lambda parts: jnp.pad(jnp.concatenate([flat(a) for a in parts], axis=1), ((0, 0), (0, pad + 1)))[None]
    sm = _adamw(recv_small, cat(small_w), cat(small_m), cat(small_v), "adamw_small")

    loss = sm[0][0, 0, n_small - 1]
    order =["norm1_g", "w_in", "q_a_norm_g", "w_q_b", "kv_a_norm_g", "w_kv_b", "rel_bias", "sinks", "w_out", "norm2_g", "w_up",
             "conv_w", "conv_b", "w_down", "final_norm_g"]
    small_names = ["norm1_g", "q_a_norm_g", "kv_a_norm_g", "rel_bias", "sinks", "norm2_g", "conv_b", "final_norm_g"]
    offs, o = {}, 0
    for n, a in zip(small_names, small_w):
        offs[n] = (o, a.size, a.shape)
        o += a.size
    outs = [loss, grad_x[None]]
    for kind in range(4):
        for n in order:
            if n in big:
                outs.append(tr(big[n][kind]) if n in transposed else big[n][kind])
            else:
                o, size, shape = offs[n]
                outs.append(sm[kind][0, 0, o:o + size].reshape(shape))
    return tuple(outs)
```

```python
import math

import jax
import jax.numpy as jnp
from jax import lax
from jax.experimental import pallas as pl
from jax.experimental.pallas import tpu as pltpu

F32 = jnp.float32
BF16 = jnp.bfloat16

N_DEV = 8
D_MODEL = 1024
EPS = 1e-6
H_A, QK_NOPE, QK_ROPE, V_DIM, Q_LORA, KV_LORA = 8, 128, 64, 128, 256, 128
QK_HEAD = QK_NOPE + QK_ROPE
HEAD_PAD = 256
ROPE_THETA = 10000.0
H_B, KV_B, GROUP, HD_B, WINDOW, Q_BLOCK = 16, 4, 4, 64, 128, 128
SPAN = Q_BLOCK + 2 * WINDOW
NUM_BUCKETS, MAX_DISTANCE = 32, 128
D_FF = 2816
ADAM_LR, ADAM_B1, ADAM_B2, ADAM_EPS, ADAM_WD, ADAM_STEP = 0.001, 0.9, 0.999, 1e-08, 0.01, 10

W_IN_SIZES = (Q_LORA, KV_LORA + QK_ROPE, H_B * HD_B, KV_B * HD_B, KV_B * HD_B, D_MODEL, D_MODEL)
W_IN_COLS = sum(W_IN_SIZES)
PROJ_P = 4096
PROJ_GA, PROJ_GB, PROJ_QB, PROJ_QLAT, PROJ_KB, PROJ_VB, PROJ_CKV, PROJ_KROPE = 0, 1, 2, 12, 13, 14, 30, 31

VMEM_LIMIT = 56 * 1024 * 1024

NN = (((1,), (0,)), ((), ()))
NT = (((1,), (1,)), ((), ()))
TN = (((0,), (0,)), ((), ()))


def _pcall(body, *, name, grid, in_specs, out_specs, out_shape, scratch_shapes=(), dims=None, comm=None, aliases=None, two_level=False):
    if comm is None:
        params = pltpu.CompilerParams(dimension_semantics=dims, vmem_limit_bytes=VMEM_LIMIT)
        return pl.pallas_call(body, name=name, grid=grid, in_specs=in_specs, out_specs=out_specs, out_shape=out_shape,
                              scratch_shapes=list(scratch_shapes), input_output_aliases=aliases or {}, compiler_params=params)
    assert not aliases
    stacked, replicated = comm
    arrs = [*stacked, *replicated]
    n_st, n_arr = len(stacked), len(arrs)
    single = not isinstance(out_specs, (list, tuple))
    o_specs, o_shape = ([out_specs], [out_shape]) if single else (list(out_specs), list(out_shape))
    n_in, n_out = len(in_specs), len(o_specs)

    def wrapped(*refs):
        c_in = refs[n_in:n_in + n_arr]
        c_out = refs[n_in + n_arr + n_out:n_in + 2 * n_arr + n_out]
        sems = refs[len(refs) - 3:]
        own = (*refs[:n_in], *refs[n_in + n_arr:n_in + n_arr + n_out], *refs[n_in + 2 * n_arr + n_out:len(refs) - 3])
        if two_level:
            assert n_st == 0
            start, finish = (lambda: _gather2(c_in, c_out, sems, False)), (lambda: _gather2(c_in, c_out, sems, True))
        else:
            start, finish = (lambda: _xchg_start(c_in, c_out, sems, n_st)), (lambda: _xchg_finish(c_in, c_out, sems, n_st))
        if not grid:
            start()
            finish()
            return
        first = last = None
        for d, n in enumerate(grid):
            pid = pl.program_id(d)
            first = (pid == 0) if first is None else first & (pid == 0)
            last = (pid == n - 1) if last is None else last & (pid == n - 1)

        pl.when(first)(start)
        body(*own)
        pl.when(last)(finish)

    params = pltpu.CompilerParams(dimension_semantics=("arbitrary",) * len(grid), vmem_limit_bytes=VMEM_LIMIT)
    call = pl.pallas_call(wrapped, name=name, grid=grid, in_specs=[*in_specs, *[ANY] * n_arr], out_specs=[*o_specs, *[ANY] * n_arr],
                          out_shape=[*o_shape, *_xchg_out_shapes(stacked, replicated)],
                          scratch_shapes=[*scratch_shapes, *_xchg_sems(n_arr)], compiler_params=params)

    def run(*args):
        res = call(*args, *arrs)
        outs, landed = res[:n_out], res[n_out:]
        return (outs[0] if single else outs), landed

    return run


def _dot(a, b, dn):
    return lax.dot_general(a, b, dn, preferred_element_type=F32)


def _tile(n, target):
    best = None
    for t in range(128, min(n, target) + 1, 128):
        if n % t == 0:
            best = t
    return n if best is None else best


def _matmul(a, b, mode, out_dtype, name, residual=None, tm=1024, tn=1024, comm=None, a2=None, epi=None):
    if mode == "nn":
        (M, K), N = a.shape, b.shape[1]
    elif mode == "nt":
        (M, K), N = a.shape, b.shape[0]
    else:
        (K, M), N = a.shape, b.shape[1]
    tm, tn = _tile(M, tm), _tile(N, tn)
    a_spec = pl.BlockSpec((K, tm), lambda i, j: (0, i)) if mode == "tn" else pl.BlockSpec((tm, K), lambda i, j: (i, 0))
    b_spec = pl.BlockSpec((tn, b.shape[1]), lambda i, j: (j, 0)) if mode == "nt" else pl.BlockSpec((K, tn), lambda i, j: (0, j))
    o_spec = pl.BlockSpec((tm, tn), lambda i, j: (i, j))
    in_specs, args = [a_spec, b_spec], [a, b]
    n1 = M // tm
    if a2 is not None and mode == "tn":
        assert M % tm == 0 and a2.shape[1] % tm == 0
        in_specs[0] = pl.BlockSpec((K, tm), lambda i, j: (0, jnp.minimum(i, n1 - 1)))
        in_specs.append(pl.BlockSpec((K, tm), lambda i, j: (0, jnp.maximum(i - n1, 0))))
        args.append(a2)
        M += a2.shape[1]
    elif a2 is not None:
        assert (mode == "nt" and K + a2.shape[1] == b.shape[1]) or (mode == "nn" and K + a2.shape[1] == b.shape[0])
        if mode == "nn":
            b_spec = in_specs[1] = pl.BlockSpec((b.shape[0], tn), lambda i, j: (0, j))
        in_specs.append(pl.BlockSpec((tm, a2.shape[1]), lambda i, j: (i, 0)))
        args.append(a2)
    if residual is not None:
        in_specs.append(o_spec)
        args.append(residual)
    n_mm = len(args)
    scratch = [pltpu.VMEM((tm, K), a.dtype)] if mode == "tn" else []
    if epi is None:
        out_specs, out_shape, is_acc = o_spec, jax.ShapeDtypeStruct((M, N), out_dtype), None
    else:
        assert tn == N
        fn, epi_ins, epi_outs = epi
        in_specs += [mk(tm) for _, mk in epi_ins]
        args += [arr for arr, _ in epi_ins]
        out_specs, out_shape, is_acc = _row_out_specs(epi_outs, M, tm)

    def body(*refs):
        a_ref, b_ref = refs[0], refs[1]
        n_out = 1 if epi is None else len(is_acc)
        out_refs = refs[len(args):len(args) + n_out]
        if mode == "tn":
            at_ref = refs[len(args) + n_out]

            first_col = pl.program_id(1) == 0
            from_a = first_col if a2 is None else first_col & (pl.program_id(0) < n1)

            @pl.when(from_a)
            def _():
                at_ref[...] = a_ref[...].T

            if a2 is not None:
                @pl.when(first_col & (pl.program_id(0) >= n1))
                def _():
                    at_ref[...] = refs[2][...].T

            acc = _dot(at_ref[...], b_ref[...], NN)
        elif a2 is not None and mode == "nt":
            acc = _dot(a_ref[...], b_ref[:, :K], NT) + _dot(refs[2][...], b_ref[:, K:], NT)
        elif a2 is not None:
            acc = _dot(a_ref[...], b_ref[:K, :], NN) + _dot(refs[2][...], b_ref[K:, :], NN)
        else:
            acc = _dot(a_ref[...], b_ref[...], NT if mode == "nt" else NN)
        if residual is not None:
            acc = acc + refs[n_mm - 1][...]
        if epi is None:
            out_refs[0][...] = acc.astype(out_dtype)
        else:
            _store_rows(out_refs, fn(acc, *[_load_f32(r) for r in refs[n_mm:len(args)]]), is_acc)

    return _pcall(body, name=name, grid=(M // tm, N // tn), in_specs=in_specs, out_specs=out_specs,
                  out_shape=out_shape, scratch_shapes=scratch,
                  dims=("arbitrary" if epi is not None else "parallel", "arbitrary"), comm=comm)(*args)


def _rows(arr, width=None, col=0):
    width = arr.shape[1] if width is None else width
    return (arr, lambda tm: pl.BlockSpec((tm, width), lambda i, *_: (i, col)))


def _heads(arr):
    return (arr, lambda tm: pl.BlockSpec((arr.shape[0], tm, arr.shape[2]), lambda i, *_: (0, i, 0)))


def _whole(arr):
    nd = arr.ndim
    return (arr, lambda tm: pl.BlockSpec(arr.shape, lambda i, *_: (0,) * nd))


def _row_out_specs(outs, n_rows, tm):
    out_specs, out_shape, is_acc = [], [], []
    for o in outs:
        if o[0] == "rows":
            out_specs.append(pl.BlockSpec((tm, o[1]), lambda i, *_: (i, 0)))
            out_shape.append(jax.ShapeDtypeStruct((n_rows, o[1]), o[2]))
        elif o[0] == "cols":
            out_specs.append(pl.BlockSpec((tm, o[1]), lambda i, *_, c=o[2]: (i, c)))
            out_shape.append(jax.ShapeDtypeStruct((n_rows, o[3]), o[4]))
        elif o[0] == "heads":
            out_specs.append(pl.BlockSpec((o[1], tm, o[2]), lambda i, *_: (0, i, 0)))
            out_shape.append(jax.ShapeDtypeStruct((o[1], n_rows, o[2]), o[3]))
        elif o[0] == "colblock":
            out_specs.append(pl.BlockSpec((o[1], o[2]), lambda i, *_: (0, i)))
            out_shape.append(jax.ShapeDtypeStruct((o[1], o[3]), o[4]))
        else:
            out_specs.append(pl.BlockSpec((o[1], o[2]), lambda i, *_: (0, 0)))
            out_shape.append(jax.ShapeDtypeStruct((o[1], o[2]), F32))
        is_acc.append(o[0] == "acc")
    return out_specs, out_shape, is_acc


def _load_f32(r):
    v = r[...]
    return v.astype(F32) if v.dtype == BF16 else v


def _store_rows(out_refs, vals, is_acc):
    for r, v, acc in zip(out_refs, vals, is_acc):
        if acc:
            @pl.when(pl.program_id(0) == 0)
            def _():
                r[...] = jnp.zeros_like(r)

            r[...] += v
        else:
            r[...] = v.astype(r.dtype)


def _rowwise(fn, name, n_rows, tm, ins, outs, upcast=True, into=None, gather=None):
    tm = min(tm, n_rows)
    assert n_rows % tm == 0
    in_specs = [mk(tm) for _, mk in ins]
    out_specs, out_shape, is_acc = _row_out_specs(outs, n_rows, tm)
    n_in = len(ins)
    args = [a for a, _ in ins]
    aliases = {}
    if into is not None:
        in_specs.append(ANY)
        args.append(into[0])
        aliases = {n_in: into[1]}

    def body(*refs):
        vals = fn(*[_load_f32(r) if upcast else r[...] for r in refs[:n_in]])
        _store_rows(refs[len(args):], vals, is_acc)

    return _pcall(body, name=name, grid=(n_rows // tm,), in_specs=in_specs, out_specs=out_specs,
                  out_shape=out_shape, dims=("arbitrary",), aliases=aliases,
                  comm=None if gather is None else ([], gather), two_level=True)(*args)


def _rms(x, g):
    r = lax.rsqrt(jnp.mean(x * x, axis=-1, keepdims=True) + EPS)
    return x * r * g


def _rms_bwd(dy, x, g):
    r = lax.rsqrt(jnp.mean(x * x, axis=-1, keepdims=True) + EPS)
    xhat = x * r
    dxhat = dy * g
    dx = r * (dxhat - xhat * jnp.mean(dxhat * xhat, axis=-1, keepdims=True))
    return dx, jnp.sum(dy * xhat, axis=0, keepdims=True)


def _rope_bwd(d1, d2, cos, sin):
    return d1 * cos + d2 * sin, d2 * cos - d1 * sin


def _rope128(x, cos_p, sin_p):
    lane = lax.broadcasted_iota(jnp.int32, x.shape, 1)
    swapped = jnp.where(lane < QK_ROPE // 2, pltpu.roll(x, 128 - QK_ROPE // 2, axis=1), pltpu.roll(x, QK_ROPE // 2, axis=1))
    return x * cos_p + swapped * sin_p


def _sigmoid(x):
    return 1.0 / (1.0 + jnp.exp(-x))


MLA_SCALE = 1.0 / math.sqrt(QK_HEAD)
MLA_PRESCALE = MLA_SCALE * math.log2(math.e)
MLA_TQ, MLA_KC = 1024, 1024


def _mla_fwd(q_full, k_full, kv, S, comm=None):
    tq, kc = min(2 * MLA_TQ, S), min(MLA_KC, S)

    def body(q_ref, k_ref, v_ref, o_ref, lse_ref):
        q = q_ref[0]
        m = jnp.full((tq, 1), -1e30, F32)
        l = jnp.zeros((tq, 1), F32)
        acc = jnp.zeros((tq, V_DIM), F32)
        for c in range(S // kc):
            s = _dot(q, k_ref[0, c * kc:(c + 1) * kc, :], NT)
            m_new = jnp.maximum(m, jnp.max(s, axis=-1, keepdims=True))
            alpha = jnp.exp2(m - m_new)
            p = jnp.exp2(s - m_new)
            l = alpha * l + jnp.sum(p, axis=-1, keepdims=True)
            acc = alpha * acc + _dot(p.astype(BF16), v_ref[c * kc:(c + 1) * kc, :], NN)
            m = m_new
        o_ref[...] = (acc / l).astype(BF16)
        lse_ref[0] = m + jnp.log2(l)

    return _pcall(
        body, name="mla_fwd", grid=(H_A, S // tq),
        in_specs=[pl.BlockSpec((1, tq, HEAD_PAD), lambda h, i: (h, i, 0)),
                  pl.BlockSpec((1, S, HEAD_PAD), lambda h, i: (h, 0, 0)),
                  pl.BlockSpec((S, V_DIM), lambda h, i: (0, 2 * h + 1))],
        out_specs=[pl.BlockSpec((tq, V_DIM), lambda h, i: (i, h)),
                   pl.BlockSpec((1, tq, 1), lambda h, i: (h, i, 0))],
        out_shape=[jax.ShapeDtypeStruct((S, H_A * V_DIM), BF16), jax.ShapeDtypeStruct((H_A, S, 1), F32)],
        dims=("parallel", "parallel"), comm=comm)(q_full, k_full, kv)


def _mla_bwd(q_full, k_full, kv, do_a, o_a, lse, cos_p, sin_back, S, comm=None):
    tq, kc = min(MLA_TQ, S), min(MLA_KC, S)

    def body(q_ref, k_ref, v_ref, do_ref, o_ref, lse_ref, cos_ref, sin_ref, dqn_ref, dqr_ref, dkv_out, dkr_out, dk_ref, dv_ref):
        @pl.when(pl.program_id(1) == 0)
        def _():
            dk_ref[...] = jnp.zeros_like(dk_ref)
            dv_ref[...] = jnp.zeros_like(dv_ref)

        q = q_ref[0]
        do = do_ref[...]
        lse_q = lse_ref[0]
        delta = jnp.sum(do.astype(F32) * o_ref[...].astype(F32), axis=-1, keepdims=True)
        dq = jnp.zeros((tq, HEAD_PAD), F32)
        for c in range(S // kc):
            k = k_ref[0, c * kc:(c + 1) * kc, :]
            v = v_ref[c * kc:(c + 1) * kc, :]
            p = jnp.exp2(_dot(q, k, NT) - lse_q)
            ds = (p * (_dot(do, v, NT) - delta)).astype(BF16)
            dq = dq + _dot(ds, k, NN)
            dk_ref[0, c * kc:(c + 1) * kc, :] += _dot(ds, q, TN)
            dv_ref[0, c * kc:(c + 1) * kc, :] += _dot(p.astype(BF16), do, TN)
        dq = dq * MLA_SCALE
        dqn_ref[...] = dq[:, :QK_NOPE].astype(BF16)
        dqr_ref[...] = _rope128(dq[:, QK_NOPE:], cos_ref[...], sin_ref[...]).astype(BF16)

        @pl.when(pl.program_id(1) == S // tq - 1)
        def _():
            dk = dk_ref[0] * math.log(2.0)
            dkv_out[...] = jnp.concatenate([dk[:, :QK_NOPE], dv_ref[0]], axis=1).astype(BF16)
            dkr_out[0] = dk[:, QK_NOPE:QK_HEAD].astype(BF16)

    return _pcall(
        body, name="mla_bwd", grid=(H_A, S // tq),
        in_specs=[pl.BlockSpec((1, tq, HEAD_PAD), lambda h, i: (h, i, 0)),
                  pl.BlockSpec((1, S, HEAD_PAD), lambda h, i: (h, 0, 0)),
                  pl.BlockSpec((S, V_DIM), lambda h, i: (0, 2 * h + 1)),
                  pl.BlockSpec((tq, V_DIM), lambda h, i: (i, h)),
                  pl.BlockSpec((tq, V_DIM), lambda h, i: (i, h)),
                  pl.BlockSpec((1, tq, 1), lambda h, i: (h, i, 0)),
                  pl.BlockSpec((tq, 128), lambda h, i: (i, 0)), pl.BlockSpec((tq, 128), lambda h, i: (i, 0))],
        out_specs=[pl.BlockSpec((tq, QK_NOPE), lambda h, i: (i, h)),
                   pl.BlockSpec((tq, 128), lambda h, i: (i, h)),
                   pl.BlockSpec((S, QK_NOPE + V_DIM), lambda h, i: (0, h)),
                   pl.BlockSpec((1, S, QK_ROPE), lambda h, i: (h, 0, 0))],
        out_shape=[jax.ShapeDtypeStruct((S, H_A * QK_NOPE), BF16), jax.ShapeDtypeStruct((S, H_A * 128), BF16),
                   jax.ShapeDtypeStruct((S, H_A * (QK_NOPE + V_DIM)), BF16), jax.ShapeDtypeStruct((H_A, S, QK_ROPE), BF16)],
        scratch_shapes=[pltpu.VMEM((1, S, HEAD_PAD), F32), pltpu.VMEM((1, S, V_DIM), F32)],
        dims=("parallel", "arbitrary"), comm=comm)(q_full, k_full, kv, do_a, o_a, lse, cos_p, sin_back)


WIN_SCALE = 1.0 / math.sqrt(HD_B)


WIN_PER_STEP = 4


def _win_specs(S):
    last, B = S // Q_BLOCK - 1, WIN_PER_STEP
    qspec = pl.BlockSpec((B * Q_BLOCK, H_B * HD_B), lambda i: (i, PROJ_QB))
    kspecs = [[pl.BlockSpec((Q_BLOCK, KV_B * HD_B), lambda i, d=d, c=c: (jnp.clip(B * i + d, 0, last), c)) for d in range(-1, B + 1)]
              for c in (PROJ_KB, PROJ_VB)]
    bias_spec = pl.BlockSpec((H_B, SPAN, Q_BLOCK), lambda i: (0, 0, 0))
    sink_spec = pl.BlockSpec((H_B, Q_BLOCK), lambda i: (0, 0))
    return qspec, kspecs, bias_spec, sink_spec


def _win_edge_ok(n, n_blk):
    row = lax.broadcasted_iota(jnp.int32, (SPAN, 1), 0)
    return jnp.logical_not(((n == 0) & (row < WINDOW)) | ((n == n_blk - 1) & (row >= SPAN - WINDOW)))


def _lanes4(pieces):
    return jnp.concatenate(pieces, axis=1)


def _win_probs(kg, q4t, bias_ref, sink_ref, g, edge_ok):
    bias4 = _lanes4([bias_ref[GROUP * g + j] for j in range(GROUP)])
    sink4 = _lanes4([sink_ref[GROUP * g + j:GROUP * g + j + 1, :] for j in range(GROUP)])
    s = jnp.where(edge_ok, _dot(kg, q4t, NN) + bias4, -1e30)
    m = jnp.maximum(jnp.max(s, axis=0, keepdims=True), sink4)
    p = jnp.exp(s - m)
    e_sink = jnp.exp(sink4 - m)
    inv_l = 1.0 / (jnp.sum(p, axis=0, keepdims=True) + e_sink)
    return p * inv_l, e_sink * inv_l


def _group_t(xt, g):
    return _lanes4([xt[HD_B * (GROUP * g + j):HD_B * (GROUP * g + j + 1), :] for j in range(GROUP)])


def _rows_of(ref, b):
    return ref[Q_BLOCK * b:Q_BLOCK * (b + 1), :]


def _win_fwd(proj, o_a, bias_t, sinks_b, S, comm=None):
    n_blk, B = S // Q_BLOCK, WIN_PER_STEP
    qspec, kspecs, bias_spec, sink_spec = _win_specs(S)
    rows = lambda col: pl.BlockSpec((B * Q_BLOCK, H_B * HD_B), lambda i: (i, col))

    def body(q_ref, *refs):
        k_refs, v_refs = refs[:B + 2], refs[B + 2:2 * B + 4]
        bias_ref, sink_ref, ga_ref, gb_ref, oa_ref, o_ref, mixed_ref = refs[2 * B + 4:]
        for b in range(B):
            edge_ok = _win_edge_ok(B * pl.program_id(0) + b, n_blk)
            k = jnp.concatenate([r[...] for r in k_refs[b:b + 3]], axis=0)
            vt = jnp.concatenate([r[...] for r in v_refs[b:b + 3]], axis=0).T
            qt = (_rows_of(q_ref, b).astype(F32) * WIN_SCALE).T.astype(BF16)
            parts = []
            for g in range(KV_B):
                p, _ = _win_probs(k[:, HD_B * g:HD_B * (g + 1)], _group_t(qt, g), bias_ref, sink_ref, g, edge_ok)
                o4t = _dot(vt[HD_B * g:HD_B * (g + 1), :], p.astype(BF16), NN)
                parts += [o4t[:, Q_BLOCK * j:Q_BLOCK * (j + 1)] for j in range(GROUP)]
            ob = jnp.concatenate(parts, axis=0).T
            o_ref[Q_BLOCK * b:Q_BLOCK * (b + 1), :] = ob.astype(BF16)
            ga, gb, oa = (_rows_of(r, b).astype(F32) for r in (ga_ref, gb_ref, oa_ref))
            mixed_ref[Q_BLOCK * b:Q_BLOCK * (b + 1), :] = (_sigmoid(ga) * oa + _sigmoid(gb) * ob).astype(BF16)

    return _pcall(body, name="win_fwd_mix", grid=(n_blk // B,),
                  in_specs=[qspec, *kspecs[0], *kspecs[1], bias_spec, sink_spec, rows(PROJ_GA), rows(PROJ_GB), rows(0)],
                  out_specs=[rows(0), rows(0)],
                  out_shape=[jax.ShapeDtypeStruct((S, H_B * HD_B), BF16)] * 2,
                  dims=("parallel",), comm=comm)(*[proj] * (2 * B + 5), bias_t, sinks_b, proj, proj, o_a)


def _win_bwd(proj, bias_t, sinks_b, do_b, d_proj, S):
    n_blk, B = S // Q_BLOCK, WIN_PER_STEP
    qspec, kspecs, bias_spec, sink_spec = _win_specs(S)

    def body(q_ref, *refs):
        k_refs, v_refs = refs[:B + 2], refs[B + 2:2 * B + 4]
        bias_ref, sink_ref, do_ref, _, dq_ref, dk_ref, dv_ref, dbias_ref, dsink_ref, dsink_acc = refs[2 * B + 4:]
        i = pl.program_id(0)

        @pl.when(i == 0)
        def _():
            dk_ref[...] = jnp.zeros_like(dk_ref)
            dv_ref[...] = jnp.zeros_like(dv_ref)
            dbias_ref[...] = jnp.zeros_like(dbias_ref)
            dsink_acc[...] = jnp.zeros_like(dsink_acc)

        d_bias, d_sink, dk_blocks, dv_blocks = {}, {}, [], []
        for b in range(B):
            edge_ok = _win_edge_ok(B * i + b, n_blk)
            k = jnp.concatenate([r[...] for r in k_refs[b:b + 3]], axis=0)
            v = jnp.concatenate([r[...] for r in v_refs[b:b + 3]], axis=0)
            kt = k.T
            qt = (_rows_of(q_ref, b).astype(F32) * WIN_SCALE).T.astype(BF16)
            dot_ = _rows_of(do_ref, b).astype(F32).T.astype(BF16)
            dq_parts, dks, dvs = [], [], []
            for g in range(KV_B):
                kg, vg = k[:, HD_B * g:HD_B * (g + 1)], v[:, HD_B * g:HD_B * (g + 1)]
                q4t, do4t = _group_t(qt, g), _group_t(dot_, g)
                p, p_sink = _win_probs(kg, q4t, bias_ref, sink_ref, g, edge_ok)
                dp = _dot(vg, do4t, NN)
                delta = jnp.sum(p * dp, axis=0, keepdims=True)
                ds = p * (dp - delta)
                d_bias[g] = ds if b == 0 else d_bias[g] + ds
                d_sink[g] = -p_sink * delta if b == 0 else d_sink[g] - p_sink * delta
                dsb = ds.astype(BF16)
                dq4t = _dot(kt[HD_B * g:HD_B * (g + 1), :], dsb, NN) * WIN_SCALE
                dq_parts += [dq4t[:, Q_BLOCK * j:Q_BLOCK * (j + 1)] for j in range(GROUP)]
                dks.append(_dot(dsb, q4t, NT))
                dvs.append(_dot(p.astype(BF16), do4t, NT))
            dq_ref[Q_BLOCK * b:Q_BLOCK * (b + 1), :] = jnp.concatenate(dq_parts, axis=0).T.astype(BF16)
            dk_blocks.append(jnp.concatenate(dks, axis=1))
            dv_blocks.append(jnp.concatenate(dvs, axis=1))

        for g in range(KV_B):
            for j in range(GROUP):
                dbias_ref[GROUP * g + j] += d_bias[g][:, Q_BLOCK * j:Q_BLOCK * (j + 1)]
            dsink_acc[g:g + 1, :] += d_sink[g]

        def overlap(blocks):
            out = blocks[0]
            for blk in blocks[1:]:
                keep = out.shape[0] - 2 * Q_BLOCK
                out = jnp.concatenate([out[:keep], out[keep:] + blk[:2 * Q_BLOCK], blk[2 * Q_BLOCK:]], axis=0)
            return out

        rows = pl.ds(pl.multiple_of(i * (B * Q_BLOCK), B * Q_BLOCK), (B + 2) * Q_BLOCK)
        dk_ref[rows, :] += overlap(dk_blocks)
        dv_ref[rows, :] += overlap(dv_blocks)

        @pl.when(i == n_blk // B - 1)
        def _():
            acc = dsink_acc[...]
            dsink_ref[...] = jnp.concatenate(
                [jnp.sum(acc[:, Q_BLOCK * j:Q_BLOCK * (j + 1)], axis=1, keepdims=True) for j in range(GROUP)], axis=1)

    whole = lambda shape: pl.BlockSpec(shape, lambda i: (0,) * len(shape))
    return _pcall(
        body, name="win_bwd", grid=(n_blk // B,),
        in_specs=[qspec, *kspecs[0], *kspecs[1], bias_spec, sink_spec, pl.BlockSpec((B * Q_BLOCK, H_B * HD_B), lambda i: (i, 0)), ANY],
        out_specs=[qspec, whole((S + 2 * WINDOW, KV_B * HD_B)),
                   whole((S + 2 * WINDOW, KV_B * HD_B)), whole((H_B, SPAN, Q_BLOCK)), whole((KV_B, GROUP))],
        out_shape=[jax.ShapeDtypeStruct((S, PROJ_P), BF16), jax.ShapeDtypeStruct((S + 2 * WINDOW, KV_B * HD_B), F32),
                   jax.ShapeDtypeStruct((S + 2 * WINDOW, KV_B * HD_B), F32), jax.ShapeDtypeStruct((H_B, SPAN, Q_BLOCK), F32),
                   jax.ShapeDtypeStruct((KV_B, GROUP), F32)],
        scratch_shapes=[pltpu.VMEM((KV_B, GROUP * Q_BLOCK), F32)],
        dims=("arbitrary",), aliases={2 * B + 8: 0})(*[proj] * (2 * B + 5), bias_t, sinks_b, do_b, d_proj)


def _bias_table_bwd(dbias, onehot_t):
    n = onehot_t.shape[1]
    tk = _tile(n, 8192)

    def body(d_ref, oh_ref, o_ref):
        @pl.when(pl.program_id(0) == 0)
        def _():
            o_ref[...] = jnp.zeros_like(o_ref)

        o_ref[...] += lax.dot_general(d_ref[...], oh_ref[...], NT, preferred_element_type=F32, precision=lax.Precision.HIGHEST)

    return _pcall(body, name="bias_table_bwd", grid=(n // tk,),
                  in_specs=[pl.BlockSpec((H_B, tk), lambda j: (0, j)), pl.BlockSpec((NUM_BUCKETS, tk), lambda j: (0, j))],
                  out_specs=pl.BlockSpec((H_B, NUM_BUCKETS), lambda j: (0, 0)),
                  out_shape=jax.ShapeDtypeStruct((H_B, NUM_BUCKETS), F32), dims=("arbitrary",))(dbias, onehot_t)


CONV_STRIP = 128
N_STRIPS = D_FF // CONV_STRIP
CONV_ROWS = 128
HALO = 8


def _strip(rows, half):
    return pl.BlockSpec((rows, CONV_STRIP), lambda j: (0, j + half * N_STRIPS))


def _fill_padded(pad_ref, src_ref, halo, S):
    pad_ref[0:halo, :] = jnp.zeros((halo, CONV_STRIP), F32)
    pad_ref[halo + S:2 * halo + S, :] = jnp.zeros((halo, CONV_STRIP), F32)
    pad_ref[halo:halo + S, :] = src_ref[...].astype(F32)


def _conv_gate_fwd(u, conv_w, conv_b, S):
    R = min(CONV_ROWS, S)

    def body(ug_ref, uv_ref, wg_ref, wv_ref, bg_ref, bv_ref, a_ref, gpad, vpad):
        _fill_padded(gpad, ug_ref, HALO, S)
        _fill_padded(vpad, uv_ref, HALO, S)
        wg, wv, bg, bv = wg_ref[...], wv_ref[...], bg_ref[...], bv_ref[...]

        def conv(pad_ref, r0, w, b):
            dn, mid, up = (pad_ref[pl.ds(r0 + HALO + d, R), :] for d in (-1, 0, 1))
            return dn * w[0:1, :] + mid * w[1:2, :] + up * w[2:3, :] + b

        def step(c, carry):
            r0 = pl.multiple_of(c * R, R)
            g = conv(gpad, r0, wg, bg)
            val = conv(vpad, r0, wv, bv)
            a_ref[pl.ds(r0, R), :] = (g * _sigmoid(g) * val).astype(BF16)
            return carry

        lax.fori_loop(0, S // R, step, 0)

    return _pcall(body, name="conv_gate_fwd", grid=(N_STRIPS,),
                  in_specs=[_strip(S, 0), _strip(S, 1), _strip(3, 0), _strip(3, 1), _strip(1, 0), _strip(1, 1)],
                  out_specs=_strip(S, 0), out_shape=jax.ShapeDtypeStruct((S, D_FF), BF16),
                  scratch_shapes=[pltpu.VMEM((S + 2 * HALO, CONV_STRIP), F32)] * 2,
                  dims=("parallel",))(u, u, conv_w, conv_w, conv_b, conv_b)


def _conv_gate_bwd(u, conv_w, conv_b, da, S):
    R = min(CONV_ROWS, S)
    n = R + 2 * HALO

    def body(ug_ref, uv_ref, wg_ref, wv_ref, bg_ref, bv_ref, da_ref, dug_ref, duv_ref, dwg_ref, dwv_ref, dbg_ref, dbv_ref,
             gpad, vpad, dapad):
        _fill_padded(gpad, ug_ref, 2 * HALO, S)
        _fill_padded(vpad, uv_ref, 2 * HALO, S)
        _fill_padded(dapad, da_ref, HALO, S)
        wg, wv, bg, bv = wg_ref[...], wv_ref[...], bg_ref[...], bv_ref[...]

        def conv(pad_ref, r0, w, b):
            dn, mid, up = (pad_ref[pl.ds(r0 + HALO + d, n), :] for d in (-1, 0, 1))
            return dn * w[0:1, :] + mid * w[1:2, :] + up * w[2:3, :] + b, mid[HALO:HALO + R]

        def conv_bwd(duc, u_mid, w, r0, du_ref):
            dn, mid, up = pltpu.roll(duc, 1, axis=0)[HALO:HALO + R], duc[HALO:HALO + R], pltpu.roll(duc, n - 1, axis=0)[HALO:HALO + R]
            du_ref[pl.ds(r0, R), :] = (up * w[0:1, :] + mid * w[1:2, :] + dn * w[2:3, :]).astype(BF16)
            dw = jnp.concatenate([jnp.sum(up * u_mid, axis=0, keepdims=True), jnp.sum(mid * u_mid, axis=0, keepdims=True),
                                  jnp.sum(dn * u_mid, axis=0, keepdims=True)], axis=0)
            return dw, jnp.sum(mid, axis=0, keepdims=True)

        def step(c, carry):
            dw_g, db_g, dw_v, db_v = carry
            r0 = pl.multiple_of(c * R, R)
            g, ug_mid = conv(gpad, r0, wg, bg)
            val, uv_mid = conv(vpad, r0, wv, bv)
            da_ext = dapad[pl.ds(r0, n), :]
            sg = _sigmoid(g)
            ddw_v, ddb_v = conv_bwd(da_ext * (g * sg), uv_mid, wv, r0, duv_ref)
            ddw_g, ddb_g = conv_bwd(da_ext * val * (sg * (1.0 + g * (1.0 - sg))), ug_mid, wg, r0, dug_ref)
            return dw_g + ddw_g, db_g + ddb_g, dw_v + ddw_v, db_v + ddb_v

        z3, z1 = jnp.zeros((3, CONV_STRIP), F32), jnp.zeros((1, CONV_STRIP), F32)
        dwg_ref[...], dbg_ref[...], dwv_ref[...], dbv_ref[...] = lax.fori_loop(0, S // R, step, (z3, z1, z3, z1))

    half = lambda r, dt: (_strip(r, 0), jax.ShapeDtypeStruct((r, D_FF), dt))
    outs = [half(S, BF16), half(S, BF16), half(3, F32), half(3, F32), half(1, F32), half(1, F32)]
    return _pcall(
        body, name="conv_gate_bwd", grid=(N_STRIPS,),
        in_specs=[_strip(S, 0), _strip(S, 1), _strip(3, 0), _strip(3, 1), _strip(1, 0), _strip(1, 1), _strip(S, 0)],
        out_specs=[o[0] for o in outs], out_shape=[o[1] for o in outs],
        scratch_shapes=[pltpu.VMEM((S + 4 * HALO, CONV_STRIP), F32)] * 2 + [pltpu.VMEM((S + 2 * HALO, CONV_STRIP), F32)],
        dims=("parallel",))(u, u, conv_w, conv_w, conv_b, conv_b, da)


MESH = pl.DeviceIdType.MESH
ANY = pl.BlockSpec(memory_space=pl.ANY)


def _place():
    return lax.axis_index("x"), lax.axis_index("y"), lax.axis_index("c")


def _gather2(ins, outs, sems, finish):
    send_sems, recv_sems, local_sems = sems
    n_arr = len(ins)
    x, y, c = _place()
    me, sibling = (x, y, c), (x, y, 1 - c)
    chips = [(1 - x, y), (x, 1 - y), (1 - x, 1 - y)]

    def slot(a, p):
        return outs[a].at[4 * p[0] + 2 * p[1] + p[2]]

    def copy(a, k, block, to, src=None):
        return pltpu.make_async_remote_copy(
            src_ref=slot(a, block) if src is None else src, dst_ref=slot(a, block),
            send_sem=send_sems.at[a, k], recv_sem=recv_sems.at[a, k], device_id=to, device_id_type=MESH)

    mine = [pltpu.make_async_copy(ins[a], slot(a, me), local_sems.at[a]) for a in range(n_arr)]
    first = []
    for a in range(n_arr):
        first.append(copy(a, 0, me, sibling, src=ins[a]))
        first += [copy(a, 1 + j, me, (*chip, c), src=ins[a]) for j, chip in enumerate(chips)]
    if not finish:
        for cp in mine + first:
            cp.start()
        return
    passed = []
    for j, chip in enumerate(chips):
        for a in range(n_arr):
            copy(a, 1 + j, (*chip, c), me).wait_recv()
            cp = copy(a, 4 + j, (*chip, c), sibling)
            cp.start()
            passed.append(cp)
    for a in range(n_arr):
        copy(a, 0, sibling, me).wait_recv()
        for j, chip in enumerate(chips):
            copy(a, 4 + j, (*chip, 1 - c), me).wait_recv()
    for cp in first + passed:
        cp.wait_send()
    for cp in mine:
        cp.wait()


def _xchg_out_shapes(stacked, replicated):
    return ([jax.ShapeDtypeStruct(s.shape, s.dtype) for s in stacked]
            + [jax.ShapeDtypeStruct((N_DEV, *r.shape), r.dtype) for r in replicated])


def _xchg_sems(n_arr):
    return [pltpu.SemaphoreType.DMA((n_arr, 7)), pltpu.SemaphoreType.DMA((n_arr, 7)), pltpu.SemaphoreType.DMA((n_arr,))]


def _xchg_copies(ins, outs, sems, n_st, with_recv):
    send_sems, recv_sems, local_sems = sems
    n_arr = len(ins)
    x, y, c = _place()
    me = 4 * x + 2 * y + c

    def src(a, idx):
        return ins[a].at[idx] if a < n_st else ins[a]

    mine = [pltpu.make_async_copy(src(a, me), outs[a].at[me], local_sems.at[a]) for a in range(n_arr)]
    pairs = []
    for k in range(1, N_DEV):
        px, py, pc = x ^ (k >> 2), y ^ ((k >> 1) & 1), c ^ (k & 1)
        peer = 4 * px + 2 * py + pc
        for a in range(n_arr):
            sems_k = dict(send_sem=send_sems.at[a, k - 1], recv_sem=recv_sems.at[a, k - 1], device_id_type=MESH)
            send = pltpu.make_async_remote_copy(src_ref=src(a, peer), dst_ref=outs[a].at[me], device_id=(px, py, pc), **sems_k)
            recv = None
            if with_recv:
                recv = pltpu.make_async_remote_copy(src_ref=src(a, peer), dst_ref=outs[a].at[peer], device_id=(x, y, c), **sems_k)
            pairs.append((send, recv))
    return mine, pairs


def _xchg_start(ins, outs, sems, n_st):
    mine, pairs = _xchg_copies(ins, outs, sems, n_st, False)
    for cp in mine:
        cp.start()
    for send, _ in pairs:
        send.start()


def _xchg_finish(ins, outs, sems, n_st):
    mine, pairs = _xchg_copies(ins, outs, sems, n_st, True)
    for _, recv in pairs:
        recv.wait_recv()
    for send, _ in pairs:
        send.wait_send()
    for cp in mine:
        cp.wait()


def _exchange(stacked, replicated, name):
    _, landed = _pcall(lambda: None, name=name, grid=(), in_specs=[], out_specs=[], out_shape=[], comm=(stacked, replicated))()
    return landed


HBM = pl.BlockSpec(memory_space=pltpu.HBM)
SEMS = pl.BlockSpec(memory_space=pltpu.SEMAPHORE)
SIDE_EFFECT = pltpu.SideEffectType.DATAFLOW_SIDE_EFFECTING


N_SPLIT_SEMS = 2 * (N_DEV - 1)


def _split_copies(src, land, sems, with_recv):
    x, y, c = _place()
    me = 4 * x + 2 * y + c
    pairs = []
    for k in range(1, N_DEV):
        px, py, pc = x ^ (k >> 2), y ^ ((k >> 1) & 1), c ^ (k & 1)
        peer = 4 * px + 2 * py + pc
        sems_k = dict(send_sem=sems[k - 1], recv_sem=sems[N_DEV - 1 + k - 1], device_id_type=MESH)
        send = pltpu.make_async_remote_copy(src_ref=src.at[peer], dst_ref=land.at[me], device_id=(px, py, pc), **sems_k)
        recv = None
        if with_recv:
            recv = pltpu.make_async_remote_copy(src_ref=src.at[peer], dst_ref=land.at[peer], device_id=(x, y, c), **sems_k)
        pairs.append((send, recv))
    return pairs


def _exchange_start(stacked, name):
    def body(src, land, *rest):
        for send, _ in _split_copies(src, land, rest[:N_SPLIT_SEMS], False):
            send.start()
        rest[-1][...] = jnp.zeros_like(rest[-1])

    shape = pltpu.HBM(stacked.shape, stacked.dtype)
    res = pl.pallas_call(
        body, name=name, in_specs=[HBM, HBM],
        out_shape=(*[pltpu.SemaphoreType.DMA(())] * N_SPLIT_SEMS, shape, shape, jax.ShapeDtypeStruct((8, 128), F32)),
        out_specs=(*[SEMS] * N_SPLIT_SEMS, HBM, HBM, pl.BlockSpec(memory_space=pltpu.VMEM)),
        input_output_aliases={0: N_SPLIT_SEMS, 1: N_SPLIT_SEMS + 1},
        compiler_params=pltpu.CompilerParams(has_side_effects=SIDE_EFFECT),
    )(pltpu.with_memory_space_constraint(stacked, pltpu.HBM),
      pltpu.with_memory_space_constraint(lax.empty(stacked.shape, stacked.dtype), pltpu.HBM))
    return res[:N_SPLIT_SEMS], res[N_SPLIT_SEMS], res[N_SPLIT_SEMS + 1], res[-1]


def _exchange_wait(sems, src, land, after, name):
    def body(src_ref, land_ref, *rest):
        for send, recv in _split_copies(src_ref, land_ref, rest[:N_SPLIT_SEMS], True):
            send.wait_send()
            recv.wait_recv()

    shape = pltpu.HBM(src.shape, src.dtype)
    return pl.pallas_call(
        body, name=name, in_specs=[HBM, HBM, *[SEMS] * N_SPLIT_SEMS, ANY],
        out_shape=(shape, shape), out_specs=(HBM, HBM), input_output_aliases={0: 0, 1: 1},
        compiler_params=pltpu.CompilerParams(has_side_effects=SIDE_EFFECT))(src, land, *sems, after)[1]


def _adamw(parts, w, m, v, name):
    _, R, C = w.shape
    tr = R if R <= 512 else max(t for t in range(16, 513, 16) if R % t == 0)
    pr = tr if parts.shape[1] == R else -(-R // 16) * 16
    assert pr == tr or tr == R

    def body(p_ref, w_ref, m_ref, v_ref, g_ref, d_ref, nm_ref, nv_ref):
        g = p_ref[0].astype(F32)[:tr]
        for s in range(1, N_DEV):
            g = g + p_ref[s].astype(F32)[:tr]
        m2 = ADAM_B1 * m_ref[0] + (1.0 - ADAM_B1) * g
        v2 = ADAM_B2 * v_ref[0] + (1.0 - ADAM_B2) * (g * g)
        m_hat = m2 / (1.0 - ADAM_B1 ** ADAM_STEP)
        v_hat = v2 / (1.0 - ADAM_B2 ** ADAM_STEP)
        g_ref[0] = g
        d_ref[0] = -ADAM_LR * (m_hat / (jnp.sqrt(v_hat) + ADAM_EPS) + ADAM_WD * w_ref[0])
        nm_ref[0] = m2
        nv_ref[0] = v2

    blk = pl.BlockSpec((1, tr, C), lambda i: (0, i, 0))
    return _pcall(body, name=name, grid=(R // tr,),
                  in_specs=[pl.BlockSpec((N_DEV, pr, C), lambda i: (0, i, 0)), blk, blk, blk],
                  out_specs=[blk] * 4, out_shape=[jax.ShapeDtypeStruct((1, R, C), F32)] * 4,
                  dims=("parallel",))(parts, w, m, v)


def _t5_bucket(rel):
    nb = NUM_BUCKETS // 2
    max_exact = nb // 2
    base = (rel > 0).astype(jnp.int32) * nb
    n = jnp.abs(rel)
    nf = jnp.maximum(n, 1).astype(jnp.float32)
    large = max_exact + (jnp.log(nf / max_exact) / math.log(MAX_DISTANCE / max_exact) * (nb - max_exact)).astype(jnp.int32)
    large = jnp.minimum(large, nb - 1)
    return base + jnp.where(n < max_exact, n, large)


def _unstack_cols(g):
    return jnp.transpose(g, (1, 0, 2)).reshape(g.shape[1], N_DEV * g.shape[2])


def _stack_cols(w, n=N_DEV):
    R = w.shape[0]
    return jnp.transpose(w.reshape(R, n, w.shape[1] // n), (1, 0, 2))


def _stack_halves(g, v):
    return jnp.concatenate([_stack_cols(g, N_DEV // 2), _stack_cols(v, N_DEV // 2)], axis=0)


def kernel(x, positions, norm1_g, w_in, q_a_norm_g, w_q_b, kv_a_norm_g, w_kv_b, rel_bias, sinks, w_out, norm2_g, w_up, conv_w, conv_b, w_down, final_norm_g, loss_target, m_norm1_g, m_w_in, m_q_a_norm_g, m_w_q_b, m_kv_a_norm_g, m_w_kv_b, m_rel_bias, m_sinks, m_w_out, m_norm2_g, m_w_up, m_conv_w, m_conv_b, m_w_down, m_final_norm_g, v_norm1_g, v_w_in, v_q_a_norm_g, v_w_q_b, v_kv_a_norm_g, v_w_kv_b, v_rel_bias, v_sinks, v_w_out, v_norm2_g, v_w_up, v_conv_w, v_conv_b, v_w_down, v_final_norm_g):
    S = x.shape[1]
    x = x[0]
    target = loss_target[0]
    TM = 256

    tr = lambda w: jnp.swapaxes(w, 1, 2)
    half = QK_ROPE // 2
    inv_freq = ROPE_THETA ** (-jnp.arange(half, dtype=F32) / half)
    inv_tile = jnp.concatenate([inv_freq, inv_freq, jnp.zeros((128 - QK_ROPE,), F32)])[None, :]
    sign_tile = jnp.concatenate([-jnp.ones((half,), F32), jnp.ones((half,), F32), jnp.zeros((128 - QK_ROPE,), F32)])[None, :]

    qa = jnp.arange(Q_BLOCK, dtype=jnp.int32)[:, None]
    kc = jnp.arange(SPAN, dtype=jnp.int32)[None, :]
    rel = (kc - WINDOW - qa).T
    in_band = (jnp.abs(rel) <= WINDOW).astype(F32).reshape(1, Q_BLOCK * SPAN)
    onehot_t = (_t5_bucket(rel).reshape(1, Q_BLOCK * SPAN) == jnp.arange(NUM_BUCKETS, dtype=jnp.int32)[:, None]).astype(F32)
    bias_cols = Q_BLOCK * SPAN // (S // TM)

    def norm1_tables(a, g, pos, inv, sign, rb, oh, band):
        ang = pos * inv
        rot = jnp.sin(ang) * sign
        t = lax.dot_general(rb, oh, NN, preferred_element_type=F32, precision=lax.Precision.HIGHEST)
        return _rms(a, g), jnp.cos(ang), rot, -rot, jnp.where(band > 0.5, t, -1e30)

    col_block = lambda arr: (arr, lambda tm: pl.BlockSpec((arr.shape[0], bias_cols), lambda i, *_: (0, i)))
    (h1, cos_p, sin_fwd, sin_back, bias_t), (g_in,) = _rowwise(
        norm1_tables, "norm1_gather", S, TM,
        [_rows(x), _whole(norm1_g), _rows(positions.astype(F32)[:, None]), _whole(inv_tile), _whole(sign_tile),
         _whole(rel_bias.T), col_block(onehot_t), col_block(in_band)],
        [("rows", D_MODEL, BF16)] + [("rows", 128, F32)] * 3 + [("colblock", H_B, bias_cols, Q_BLOCK * SPAN, F32)],
        gather=[tr(w_in)[0].astype(BF16)])
    bias_t = bias_t.reshape(H_B, SPAN, Q_BLOCK)
    late_weights = [w_out[0].astype(BF16), tr(w_up)[0].astype(BF16), conv_w[0]]
    wi = g_in.reshape(W_IN_COLS, D_MODEL)
    c0, c1, c2, c3, c4, c5 = (sum(W_IN_SIZES[:i + 1]) for i in range(6))
    w_in_pt = jnp.concatenate([wi[c4:c5], wi[c5:], wi[c1:c2], wi[:c0], wi[c2:c3], wi[c3:c4],
                               wi[c0:c0 + KV_LORA], wi[c0 + KV_LORA:c1], jnp.zeros((64, D_MODEL), BF16)], axis=0)

    cos, sin = cos_p[:, :half], sin_back[:, :half]
    sinks_b = jnp.broadcast_to(sinks.reshape(H_B, 1), (H_B, Q_BLOCK))

    proj, (g_qb, g_kvb) = _matmul(h1, w_in_pt, "nt", BF16, "proj", comm=([], [tr(w_q_b)[0].astype(BF16), w_kv_b[0].astype(BF16)]))
    wq = g_qb.reshape(H_A, QK_HEAD, Q_LORA)
    w_qb_pt = jnp.concatenate([wq[:, :QK_NOPE].reshape(H_A * QK_NOPE, Q_LORA),
                               jnp.pad(wq[:, QK_NOPE:], ((0, 0), (0, 128 - QK_ROPE), (0, 0))).reshape(H_A * 128, Q_LORA)], axis=0)
    w_kvb = _unstack_cols(g_kvb)

    def lat_fn(qlat, ckv, kr, gq, gkv, cs, sn):
        return _rms(qlat, gq), _rms(ckv, gkv), _rope128(kr, cs, sn)

    qn, ckvn, k_rope = _rowwise(lat_fn, "latents", S, TM,
                                [_rows(proj, 256, PROJ_QLAT), _rows(proj, 128, PROJ_CKV), _rows(proj, 128, PROJ_KROPE),
                                 _whole(q_a_norm_g), _whole(kv_a_norm_g), _rows(cos_p), _rows(sin_fwd)],
                                [("rows", Q_LORA, BF16), ("rows", KV_LORA, BF16), ("rows", 128, BF16)])
    def q_heads_fn(q, cs, sn):
        q = q * MLA_PRESCALE
        return (jnp.concatenate([jnp.concatenate([q[:, 128 * h:128 * (h + 1)], _rope128(q[:, 128 * (H_A + h):128 * (H_A + h + 1)], cs, sn)],
                                                 axis=1)[None] for h in range(H_A)], axis=0),)

    (q_full,) = _matmul(qn, w_qb_pt, "nt", None, "q_up_heads", tm=512, tn=2048,
                        epi=(q_heads_fn, [_rows(cos_p), _rows(sin_fwd)], [("heads", H_A, HEAD_PAD, BF16)]))

    def k_heads_fn(kvf, kr):
        return kvf, jnp.concatenate([jnp.concatenate([kvf[:, 256 * h:256 * h + QK_NOPE], kr], axis=1)[None] for h in range(H_A)], axis=0)

    kv, k_full = _matmul(ckvn, w_kvb, "nn", None, "kv_up_heads", tm=512, tn=2048,
                         epi=(k_heads_fn, [_rows(k_rope)], [("rows", H_A * (QK_NOPE + V_DIM), BF16), ("heads", H_A, HEAD_PAD, BF16)]))
    (o_a, lse), (g_out, g_up, g_cw) = _mla_fwd(q_full, k_full, kv, S, comm=([], late_weights))
    w_out_f = g_out.reshape(D_MODEL, D_MODEL)
    w_up_t = g_up.reshape(2 * D_FF, D_MODEL)
    conv_w_f = _unstack_cols(g_cw)

    (o_b, mixed), (g_down,) = _win_fwd(proj, o_a, bias_t, sinks_b, S, comm=([], [w_down[0].astype(BF16)]))
    w_down_f = g_down.reshape(D_FF, D_MODEL)

    x1, h2 = _matmul(mixed, w_out_f, "nn", None, "out_proj", residual=x, tm=512,
                     epi=(lambda a, g: (a, _rms(a, g)), [_whole(norm2_g)], [("rows", D_MODEL, F32), ("rows", D_MODEL, BF16)]))
    u = _matmul(h2, w_up_t, "nt", BF16, "ffn_up", tn=1408)
    act = _conv_gate_fwd(u, conv_w_f, conv_b, S)

    def final_fn(a, g, t):
        err = _rms(a, g) - t
        loss = 0.5 * jnp.sum(jnp.mean(err * err, axis=-1, keepdims=True), axis=0, keepdims=True)
        dx, dg = _rms_bwd(err * (1.0 / D_MODEL), a, g)
        return dx, dx, dg, jnp.broadcast_to(loss, (1, 128))

    gfin = final_norm_g.reshape(1, D_MODEL)
    dx2, dx2_b, d_gfin, loss_row = _matmul(
        act, w_down_f, "nn", None, "ffn_down_loss", residual=x1, tm=512,
        epi=(final_fn, [_whole(gfin), _rows(target)],
             [("rows", D_MODEL, F32), ("rows", D_MODEL, BF16), ("acc", 1, D_MODEL), ("acc", 1, 128)]))
    d_act = _matmul(dx2_b, w_down_f, "nt", BF16, "ffn_down_dx", tn=1408)
    d_w_down = _matmul(act, dx2_b, "tn", BF16, "ffn_down_dw")
    du_g, du_v, dcw_g, dcw_v, dcb_g, dcb_v = _conv_gate_bwd(u, conv_w_f, conv_b, d_act, S)
    d_conv_b = jnp.concatenate([dcb_g, dcb_v], axis=1)

    def norm_bwd_fn(dh, a, g, dres):
        dx, dg = _rms_bwd(dh, a, g)
        dx = dx + dres
        return dx, dx, dg

    dx1, dx1_b, d_g2 = _matmul(du_g, w_up_t, "nn", None, "ffn_up_dx_norm2_bwd", tm=256, a2=du_v,
                               epi=(norm_bwd_fn, [_rows(x1), _whole(norm2_g), _rows(dx2)],
                                    [("rows", D_MODEL, F32), ("rows", D_MODEL, BF16), ("acc", 1, D_MODEL)]))
    d_w_up_t = _matmul(du_g, h2, "tn", BF16, "ffn_up_dw", tm=256, a2=du_v)
    d_w_out = _matmul(mixed, dx1_b, "tn", BF16, "out_proj_dw", tm=512)

    def gate_bwd_fn(dm, ga, gb, oa, ob):
        sa, sb = _sigmoid(ga), _sigmoid(gb)
        return jnp.concatenate([dm * oa * sa * (1.0 - sa), dm * ob * sb * (1.0 - sb)], axis=1), dm * sa, dm * sb

    d_proj, do_a, do_b = _matmul(
        dx1_b, w_out_f, "nt", None, "out_proj_dx_gate_bwd", tm=512,
        epi=(gate_bwd_fn, [_rows(proj, 1024, PROJ_GA), _rows(proj, 1024, PROJ_GB), _rows(o_a), _rows(o_b)],
             [("cols", 2 * D_MODEL, 0, PROJ_P, BF16), ("rows", D_MODEL, BF16), ("rows", D_MODEL, BF16)]))

    d_proj, dk_acc, dv_acc, d_bias, d_sinks_g = _win_bwd(proj, bias_t, sinks_b, do_b, d_proj, S)
    d_sinks = d_sinks_g.reshape(1, H_B)

    early = [d_w_out.reshape(N_DEV, D_MODEL // N_DEV, D_MODEL), d_w_up_t.reshape(N_DEV, 2 * D_FF // N_DEV, D_MODEL),
             d_w_down.reshape(N_DEV, D_FF // N_DEV, D_MODEL), _stack_halves(dcw_g, dcw_v)]
    (dq_nope, dq_rope, dkv, dkr_heads), recv_early = _mla_bwd(q_full, k_full, kv, do_a, o_a, lse, cos_p, sin_back, S, comm=(early, []))

    d_qn = _matmul(dq_nope, w_qb_pt, "nn", F32, "q_up_dx", a2=dq_rope)
    d_ckvn = _matmul(dkv, w_kvb, "nt", F32, "kv_up_dx")

    def lat_bwd_fn(dqn, dckvn, dkr_h, cs, sn, qlat, ckv, gq, gkv, dkb, dvb):
        dql, dgq = _rms_bwd(dqn, qlat, gq)
        dck, dgkv = _rms_bwd(dckvn, ckv, gkv)
        dkr = dkr_h[0]
        for h in range(1, H_A):
            dkr = dkr + dkr_h[h]
        r1, r2 = _rope_bwd(dkr[:, :half], dkr[:, half:], cs, sn)
        tail = jnp.concatenate([dql, dkb, dvb, dck, r1, r2, jnp.zeros_like(dkr)], axis=1)
        return tail, dgq, dgkv

    shifted = lambda arr: (arr, lambda tm: pl.BlockSpec((tm, arr.shape[1]), lambda i, *_: (i + WINDOW // tm, 0)))
    TL = min(128, S)
    d_proj, d_gq, d_gkv = _rowwise(lat_bwd_fn, "latents_bwd", S, TL,
                                   [_rows(d_qn), _rows(d_ckvn), _heads(dkr_heads), _rows(cos), _rows(sin), _rows(proj, 256, PROJ_QLAT), _rows(proj, 128, PROJ_CKV),
                                    _whole(q_a_norm_g), _whole(kv_a_norm_g), shifted(dk_acc), shifted(dv_acc)],
                                   [("cols", 1024, 3, PROJ_P, BF16), ("acc", 1, Q_LORA), ("acc", 1, KV_LORA)], into=(d_proj, 0))
    dp = _matmul(d_proj, h1, "tn", BF16, "proj_dw", tm=512)

    late = jnp.concatenate([dp[3072:3328], dp[3840:3968], dp[3968:4032], dp[2048:3072], dp[3328:3584],
                            dp[3584:3840], dp[0:1024], dp[1024:2048]], axis=0).reshape(N_DEV, W_IN_COLS // N_DEV, D_MODEL)
    late_sems, late_src, late_land, started = _exchange_start(late, "late_grads_start")

    def norm1_bwd_fn(dh, a, g, dres):
        dx, dg = _rms_bwd(dh, a, g)
        return dx + dres, dg

    grad_x, d_g1 = _matmul(
        d_proj, w_in_pt, "nn", None, "proj_dx_norm1_bwd", tm=512,
        epi=(norm1_bwd_fn, [_rows(x), _whole(norm1_g + started[:1, :1]), _rows(dx1)], [("rows", D_MODEL, F32), ("acc", 1, D_MODEL)]))

    transposed = ("w_in", "w_q_b", "w_up")
    ready_names = ["w_out", "w_up", "w_down", "conv_w"]
    ready_wmv = [(w_out, m_w_out, v_w_out), (tr(w_up), tr(m_w_up), tr(v_w_up)), (w_down, m_w_down, v_w_down), (conv_w, m_conv_w, v_conv_w)]
    recv_early, _ = lax.optimization_barrier((list(recv_early), started))
    big = {n: _adamw(r, *wmv, "adamw_" + n) for n, r, wmv in zip(ready_names, recv_early, ready_wmv)}

    (d_bias, dq_nope_l, dkv_l), _ = lax.optimization_barrier(((d_bias, dq_nope, dkv), started))
    d_rel_bias = _bias_table_bwd(d_bias.reshape(H_B, Q_BLOCK * SPAN), onehot_t).T
    d_w_qb_pt = _matmul(dq_nope_l, qn, "tn", BF16, "q_up_dw", tm=512, a2=dq_rope)
    d_w_kvb = _matmul(ckvn, dkv_l, "tn", BF16, "kv_up_dw", tn=2048)
    d_w_qb_t = jnp.concatenate([d_w_qb_pt[:H_A * QK_NOPE].reshape(H_A, QK_NOPE, Q_LORA),
                                d_w_qb_pt[H_A * QK_NOPE:].reshape(H_A, 128, Q_LORA)[:, :QK_ROPE]], axis=1)

    after = lax.optimization_barrier([big[n][0] for n in ready_names] + [d_rel_bias, d_w_qb_t, d_w_kvb])
    landed = _exchange_wait(late_sems, late_src, late_land, after[-1], "late_grads_wait")
    me = 4 * lax.axis_index("x") + 2 * lax.axis_index("y") + lax.axis_index("c")
    landed = lax.dynamic_update_slice_in_dim(landed, lax.dynamic_slice_in_dim(late, me, 1, axis=0), me, axis=0)
    big["w_in"] = _adamw(landed, tr(w_in), tr(m_w_in), tr(v_w_in), "adamw_w_in")

    small_parts = [d_g1, d_gq, d_gkv, d_rel_bias.reshape(1, NUM_BUCKETS * H_B), d_sinks, d_g2, d_conv_b, d_gfin, loss_row[:, :1]]
    small = jnp.concatenate(small_parts, axis=1)
    n_small = small.shape[1]
    pad = (-n_small) % 128
    small = jnp.pad(small, ((0, 0), (0, pad)))
    small, _ = lax.optimization_barrier((small, [landed, *after]))
    recv_qb, recv_kvb, recv_small = _exchange([after[-2], _stack_cols(after[-1])], [small], "exchange_small_grads")
    big["w_q_b"] = _adamw(recv_qb, tr(w_q_b), tr(m_w_q_b), tr(v_w_q_b), "adamw_w_q_b")
    big["w_kv_b"] = _adamw(recv_kvb, w_kv_b, m_w_kv_b, v_w_kv_b, "adamw_w_kv_b")

    def flat(a):
        return a.reshape(1, -1)

    small_w = [norm1_g, q_a_norm_g, kv_a_norm_g, rel_bias, sinks, norm2_g, conv_b, final_norm_g]
    small_m = [m_norm1_g, m_q_a_norm_g, m_kv_a_norm_g, m_rel_bias, m_sinks, m_norm2_g, m_conv_b, m_final_norm_g]
    small_v = [v_norm1_g, v_q_a_norm_g, v_kv_a_norm_g, v_rel_bias, v_sinks, v_norm2_g, v_conv_b, v_final_norm_g]
    cat = lambda parts: jnp.pad(jnp.concatenate([flat(a) for a in parts], axis=1), ((0, 0), (0, pad + 1)))[None]
    sm = _adamw(recv_small, cat(small_w), cat(small_m), cat(small_v), "adamw_small")

    loss = sm[0][0, 0, n_small - 1]
    order =["norm1_g", "w_in", "q_a_norm_g", "w_q_b", "kv_a_norm_g", "w_kv_b", "rel_bias", "sinks", "w_out", "norm2_g", "w_up",
             "conv_w", "conv_b", "w_down", "final_norm_g"]
    small_names = ["norm1_g", "q_a_norm_g", "kv_a_norm_g", "rel_bias", "sinks", "norm2_g", "conv_b", "final_norm_g"]
    offs, o = {}, 0
    for n, a in zip(small_names, small_w):
        offs[n] = (o, a.size, a.shape)
        o += a.size
    outs = [loss, grad_x[None]]
    for kind in range(4):
        for n in order:
            if n in big:
                outs.append(tr(big[n][kind]) if n in transposed else big[n][kind])
            else:
                o, size, shape = offs[n]
                outs.append(sm[kind][0, 0, o:o + size].reshape(shape))
    return tuple(outs)
```

```python
import math

import jax
import jax.numpy as jnp
from jax import lax
from jax.experimental import pallas as pl
from jax.experimental.pallas import tpu as pltpu

F32 = jnp.float32
BF16 = jnp.bfloat16

N_DEV = 8
D_MODEL = 1024
EPS = 1e-6
H_A, QK_NOPE, QK_ROPE, V_DIM, Q_LORA, KV_LORA = 8, 128, 64, 128, 256, 128
QK_HEAD = QK_NOPE + QK_ROPE
HEAD_PAD = 256
ROPE_THETA = 10000.0
H_B, KV_B, GROUP, HD_B, WINDOW, Q_BLOCK = 16, 4, 4, 64, 128, 128
SPAN = Q_BLOCK + 2 * WINDOW
NUM_BUCKETS, MAX_DISTANCE = 32, 128
D_FF = 2816
ADAM_LR, ADAM_B1, ADAM_B2, ADAM_EPS, ADAM_WD, ADAM_STEP = 0.001, 0.9, 0.999, 1e-08, 0.01, 10

W_IN_SIZES = (Q_LORA, KV_LORA + QK_ROPE, H_B * HD_B, KV_B * HD_B, KV_B * HD_B, D_MODEL, D_MODEL)
W_IN_COLS = sum(W_IN_SIZES)
PROJ_P = 4096
PROJ_GA, PROJ_GB, PROJ_QB, PROJ_QLAT, PROJ_KB, PROJ_VB, PROJ_CKV, PROJ_KROPE = 0, 1, 2, 12, 13, 14, 30, 31

VMEM_LIMIT = 56 * 1024 * 1024

NN = (((1,), (0,)), ((), ()))
NT = (((1,), (1,)), ((), ()))
TN = (((0,), (0,)), ((), ()))


def _pcall(body, *, name, grid, in_specs, out_specs, out_shape, scratch_shapes=(), dims=None, comm=None, aliases=None, two_level=False):
    if comm is None:
        params = pltpu.CompilerParams(dimension_semantics=dims, vmem_limit_bytes=VMEM_LIMIT)
        return pl.pallas_call(body, name=name, grid=grid, in_specs=in_specs, out_specs=out_specs, out_shape=out_shape,
                              scratch_shapes=list(scratch_shapes), input_output_aliases=aliases or {}, compiler_params=params)
    assert not aliases
    stacked, replicated = comm
    arrs = [*stacked, *replicated]
    n_st, n_arr = len(stacked), len(arrs)
    single = not isinstance(out_specs, (list, tuple))
    o_specs, o_shape = ([out_specs], [out_shape]) if single else (list(out_specs), list(out_shape))
    n_in, n_out = len(in_specs), len(o_specs)

    def wrapped(*refs):
        c_in = refs[n_in:n_in + n_arr]
        c_out = refs[n_in + n_arr + n_out:n_in + 2 * n_arr + n_out]
        sems = refs[len(refs) - 3:]
        own = (*refs[:n_in], *refs[n_in + n_arr:n_in + n_arr + n_out], *refs[n_in + 2 * n_arr + n_out:len(refs) - 3])
        if two_level:
            assert n_st == 0
            start, finish = (lambda: _gather2(c_in, c_out, sems, False)), (lambda: _gather2(c_in, c_out, sems, True))
        else:
            start, finish = (lambda: _xchg_start(c_in, c_out, sems, n_st)), (lambda: _xchg_finish(c_in, c_out, sems, n_st))
        if not grid:
            start()
            finish()
            return
        first = last = None
        for d, n in enumerate(grid):
            pid = pl.program_id(d)
            first = (pid == 0) if first is None else first & (pid == 0)
            last = (pid == n - 1) if last is None else last & (pid == n - 1)

        pl.when(first)(start)
        body(*own)
        pl.when(last)(finish)

    params = pltpu.CompilerParams(dimension_semantics=("arbitrary",) * len(grid), vmem_limit_bytes=VMEM_LIMIT)
    call = pl.pallas_call(wrapped, name=name, grid=grid, in_specs=[*in_specs, *[ANY] * n_arr], out_specs=[*o_specs, *[ANY] * n_arr],
                          out_shape=[*o_shape, *_xchg_out_shapes(stacked, replicated)],
                          scratch_shapes=[*scratch_shapes, *_xchg_sems(n_arr)], compiler_params=params)

    def run(*args):
        res = call(*args, *arrs)
        outs, landed = res[:n_out], res[n_out:]
        return (outs[0] if single else outs), landed

    return run


def _dot(a, b, dn):
    return lax.dot_general(a, b, dn, preferred_element_type=F32)


def _tile(n, target):
    best = None
    for t in range(128, min(n, target) + 1, 128):
        if n % t == 0:
            best = t
    return n if best is None else best


def _matmul(a, b, mode, out_dtype, name, residual=None, tm=1024, tn=1024, comm=None, a2=None, epi=None):
    if mode == "nn":
        (M, K), N = a.shape, b.shape[1]
    elif mode == "nt":
        (M, K), N = a.shape, b.shape[0]
    else:
        (K, M), N = a.shape, b.shape[1]
    tm, tn = _tile(M, tm), _tile(N, tn)
    a_spec = pl.BlockSpec((K, tm), lambda i, j: (0, i)) if mode == "tn" else pl.BlockSpec((tm, K), lambda i, j: (i, 0))
    b_spec = pl.BlockSpec((tn, b.shape[1]), lambda i, j: (j, 0)) if mode == "nt" else pl.BlockSpec((K, tn), lambda i, j: (0, j))
    o_spec = pl.BlockSpec((tm, tn), lambda i, j: (i, j))
    in_specs, args = [a_spec, b_spec], [a, b]
    n1 = M // tm
    if a2 is not None and mode == "tn":
        assert M % tm == 0 and a2.shape[1] % tm == 0
        in_specs[0] = pl.BlockSpec((K, tm), lambda i, j: (0, jnp.minimum(i, n1 - 1)))
        in_specs.append(pl.BlockSpec((K, tm), lambda i, j: (0, jnp.maximum(i - n1, 0))))
        args.append(a2)
        M += a2.shape[1]
    elif a2 is not None:
        assert (mode == "nt" and K + a2.shape[1] == b.shape[1]) or (mode == "nn" and K + a2.shape[1] == b.shape[0])
        if mode == "nn":
            b_spec = in_specs[1] = pl.BlockSpec((b.shape[0], tn), lambda i, j: (0, j))
        in_specs.append(pl.BlockSpec((tm, a2.shape[1]), lambda i, j: (i, 0)))
        args.append(a2)
    if residual is not None:
        in_specs.append(o_spec)
        args.append(residual)
    n_mm = len(args)
    scratch = [pltpu.VMEM((tm, K), a.dtype)] if mode == "tn" else []
    if epi is None:
        out_specs, out_shape, is_acc = o_spec, jax.ShapeDtypeStruct((M, N), out_dtype), None
    else:
        assert tn == N
        fn, epi_ins, epi_outs = epi
        in_specs += [mk(tm) for _, mk in epi_ins]
        args += [arr for arr, _ in epi_ins]
        out_specs, out_shape, is_acc = _row_out_specs(epi_outs, M, tm)

    def body(*refs):
        a_ref, b_ref = refs[0], refs[1]
        n_out = 1 if epi is None else len(is_acc)
        out_refs = refs[len(args):len(args) + n_out]
        if mode == "tn":
            at_ref = refs[len(args) + n_out]

            first_col = pl.program_id(1) == 0
            from_a = first_col if a2 is None else first_col & (pl.program_id(0) < n1)

            @pl.when(from_a)
            def _():
                at_ref[...] = a_ref[...].T

            if a2 is not None:
                @pl.when(first_col & (pl.program_id(0) >= n1))
                def _():
                    at_ref[...] = refs[2][...].T

            acc = _dot(at_ref[...], b_ref[...], NN)
        elif a2 is not None and mode == "nt":
            acc = _dot(a_ref[...], b_ref[:, :K], NT) + _dot(refs[2][...], b_ref[:, K:], NT)
        elif a2 is not None:
            acc = _dot(a_ref[...], b_ref[:K, :], NN) + _dot(refs[2][...], b_ref[K:, :], NN)
        else:
            acc = _dot(a_ref[...], b_ref[...], NT if mode == "nt" else NN)
        if residual is not None:
            acc = acc + refs[n_mm - 1][...]
        if epi is None:
            out_refs[0][...] = acc.astype(out_dtype)
        else:
            _store_rows(out_refs, fn(acc, *[_load_f32(r) for r in refs[n_mm:len(args)]]), is_acc)

    return _pcall(body, name=name, grid=(M // tm, N // tn), in_specs=in_specs, out_specs=out_specs,
                  out_shape=out_shape, scratch_shapes=scratch,
                  dims=("arbitrary" if epi is not None else "parallel", "arbitrary"), comm=comm)(*args)


def _rows(arr, width=None, col=0):
    width = arr.shape[1] if width is None else width
    return (arr, lambda tm: pl.BlockSpec((tm, width), lambda i, *_: (i, col)))


def _heads(arr):
    return (arr, lambda tm: pl.BlockSpec((arr.shape[0], tm, arr.shape[2]), lambda i, *_: (0, i, 0)))


def _whole(arr):
    nd = arr.ndim
    return (arr, lambda tm: pl.BlockSpec(arr.shape, lambda i, *_: (0,) * nd))


def _row_out_specs(outs, n_rows, tm):
    out_specs, out_shape, is_acc = [], [], []
    for o in outs:
        if o[0] == "rows":
            out_specs.append(pl.BlockSpec((tm, o[1]), lambda i, *_: (i, 0)))
            out_shape.append(jax.ShapeDtypeStruct((n_rows, o[1]), o[2]))
        elif o[0] == "cols":
            out_specs.append(pl.BlockSpec((tm, o[1]), lambda i, *_, c=o[2]: (i, c)))
            out_shape.append(jax.ShapeDtypeStruct((n_rows, o[3]), o[4]))
        elif o[0] == "heads":
            out_specs.append(pl.BlockSpec((o[1], tm, o[2]), lambda i, *_: (0, i, 0)))
            out_shape.append(jax.ShapeDtypeStruct((o[1], n_rows, o[2]), o[3]))
        elif o[0] == "colblock":
            out_specs.append(pl.BlockSpec((o[1], o[2]), lambda i, *_: (0, i)))
            out_shape.append(jax.ShapeDtypeStruct((o[1], o[3]), o[4]))
        else:
            out_specs.append(pl.BlockSpec((o[1], o[2]), lambda i, *_: (0, 0)))
            out_shape.append(jax.ShapeDtypeStruct((o[1], o[2]), F32))
        is_acc.append(o[0] == "acc")
    return out_specs, out_shape, is_acc


def _load_f32(r):
    v = r[...]
    return v.astype(F32) if v.dtype == BF16 else v


def _store_rows(out_refs, vals, is_acc):
    for r, v, acc in zip(out_refs, vals, is_acc):
        if acc:
            @pl.when(pl.program_id(0) == 0)
            def _():
                r[...] = jnp.zeros_like(r)

            r[...] += v
        else:
            r[...] = v.astype(r.dtype)


def _rowwise(fn, name, n_rows, tm, ins, outs, upcast=True, into=None, gather=None):
    tm = min(tm, n_rows)
    assert n_rows % tm == 0
    in_specs = [mk(tm) for _, mk in ins]
    out_specs, out_shape, is_acc = _row_out_specs(outs, n_rows, tm)
    n_in = len(ins)
    args = [a for a, _ in ins]
    aliases = {}
    if into is not None:
        in_specs.append(ANY)
        args.append(into[0])
        aliases = {n_in: into[1]}

    def body(*refs):
        vals = fn(*[_load_f32(r) if upcast else r[...] for r in refs[:n_in]])
        _store_rows(refs[len(args):], vals, is_acc)

    return _pcall(body, name=name, grid=(n_rows // tm,), in_specs=in_specs, out_specs=out_specs,
                  out_shape=out_shape, dims=("arbitrary",), aliases=aliases,
                  comm=None if gather is None else ([], gather), two_level=True)(*args)


def _rms(x, g):
    r = lax.rsqrt(jnp.mean(x * x, axis=-1, keepdims=True) + EPS)
    return x * r * g


def _rms_bwd(dy, x, g):
    r = lax.rsqrt(jnp.mean(x * x, axis=-1, keepdims=True) + EPS)
    xhat = x * r
    dxhat = dy * g
    dx = r * (dxhat - xhat * jnp.mean(dxhat * xhat, axis=-1, keepdims=True))
    return dx, jnp.sum(dy * xhat, axis=0, keepdims=True)


def _rope_bwd(d1, d2, cos, sin):
    return d1 * cos + d2 * sin, d2 * cos - d1 * sin


def _rope128(x, cos_p, sin_p):
    lane = lax.broadcasted_iota(jnp.int32, x.shape, 1)
    swapped = jnp.where(lane < QK_ROPE // 2, pltpu.roll(x, 128 - QK_ROPE // 2, axis=1), pltpu.roll(x, QK_ROPE // 2, axis=1))
    return x * cos_p + swapped * sin_p


def _sigmoid(x):
    return 1.0 / (1.0 + jnp.exp(-x))


MLA_SCALE = 1.0 / math.sqrt(QK_HEAD)
MLA_PRESCALE = MLA_SCALE * math.log2(math.e)
MLA_TQ, MLA_KC = 1024, 1024


def _mla_fwd(q_full, k_full, kv, S, comm=None):
    tq, kc = min(2 * MLA_TQ, S), min(MLA_KC, S)

    def body(q_ref, k_ref, v_ref, o_ref, lse_ref):
        q = q_ref[0]
        m = jnp.full((tq, 1), -1e30, F32)
        l = jnp.zeros((tq, 1), F32)
        acc = jnp.zeros((tq, V_DIM), F32)
        for c in range(S // kc):
            s = _dot(q, k_ref[0, c * kc:(c + 1) * kc, :], NT)
            m_new = jnp.maximum(m, jnp.max(s, axis=-1, keepdims=True))
            alpha = jnp.exp2(m - m_new)
            p = jnp.exp2(s - m_new)
            l = alpha * l + jnp.sum(p, axis=-1, keepdims=True)
            acc = alpha * acc + _dot(p.astype(BF16), v_ref[c * kc:(c + 1) * kc, :], NN)
            m = m_new
        o_ref[...] = (acc / l).astype(BF16)
        lse_ref[0] = m + jnp.log2(l)

    return _pcall(
        body, name="mla_fwd", grid=(H_A, S // tq),
        in_specs=[pl.BlockSpec((1, tq, HEAD_PAD), lambda h, i: (h, i, 0)),
                  pl.BlockSpec((1, S, HEAD_PAD), lambda h, i: (h, 0, 0)),
                  pl.BlockSpec((S, V_DIM), lambda h, i: (0, 2 * h + 1))],
        out_specs=[pl.BlockSpec((tq, V_DIM), lambda h, i: (i, h)),
                   pl.BlockSpec((1, tq, 1), lambda h, i: (h, i, 0))],
        out_shape=[jax.ShapeDtypeStruct((S, H_A * V_DIM), BF16), jax.ShapeDtypeStruct((H_A, S, 1), F32)],
        dims=("parallel", "parallel"), comm=comm)(q_full, k_full, kv)


def _mla_bwd(q_full, k_full, kv, do_a, o_a, lse, cos_p, sin_back, S, comm=None):
    tq, kc = min(MLA_TQ, S), min(MLA_KC, S)

    def body(q_ref, k_ref, v_ref, do_ref, o_ref, lse_ref, cos_ref, sin_ref, dqn_ref, dqr_ref, dkv_out, dkr_out, dk_ref, dv_ref):
        @pl.when(pl.program_id(1) == 0)
        def _():
            dk_ref[...] = jnp.zeros_like(dk_ref)
            dv_ref[...] = jnp.zeros_like(dv_ref)

        q = q_ref[0]
        do = do_ref[...]
        lse_q = lse_ref[0]
        delta = jnp.sum(do.astype(F32) * o_ref[...].astype(F32), axis=-1, keepdims=True)
        dq = jnp.zeros((tq, HEAD_PAD), F32)
        for c in range(S // kc):
            k = k_ref[0, c * kc:(c + 1) * kc, :]
            v = v_ref[c * kc:(c + 1) * kc, :]
            p = jnp.exp2(_dot(q, k, NT) - lse_q)
            ds = (p * (_dot(do, v, NT) - delta)).astype(BF16)
            dq = dq + _dot(ds, k, NN)
            dk_ref[0, c * kc:(c + 1) * kc, :] += _dot(ds, q, TN)
            dv_ref[0, c * kc:(c + 1) * kc, :] += _dot(p.astype(BF16), do, TN)
        dq = dq * MLA_SCALE
        dqn_ref[...] = dq[:, :QK_NOPE].astype(BF16)
        dqr_ref[...] = _rope128(dq[:, QK_NOPE:], cos_ref[...], sin_ref[...]).astype(BF16)

        @pl.when(pl.program_id(1) == S // tq - 1)
        def _():
            dk = dk_ref[0] * math.log(2.0)
            dkv_out[...] = jnp.concatenate([dk[:, :QK_NOPE], dv_ref[0]], axis=1).astype(BF16)
            dkr_out[0] = dk[:, QK_NOPE:QK_HEAD].astype(BF16)

    return _pcall(
        body, name="mla_bwd", grid=(H_A, S // tq),
        in_specs=[pl.BlockSpec((1, tq, HEAD_PAD), lambda h, i: (h, i, 0)),
                  pl.BlockSpec((1, S, HEAD_PAD), lambda h, i: (h, 0, 0)),
                  pl.BlockSpec((S, V_DIM), lambda h, i: (0, 2 * h + 1)),
                  pl.BlockSpec((tq, V_DIM), lambda h, i: (i, h)),
                  pl.BlockSpec((tq, V_DIM), lambda h, i: (i, h)),
                  pl.BlockSpec((1, tq, 1), lambda h, i: (h, i, 0)),
                  pl.BlockSpec((tq, 128), lambda h, i: (i, 0)), pl.BlockSpec((tq, 128), lambda h, i: (i, 0))],
        out_specs=[pl.BlockSpec((tq, QK_NOPE), lambda h, i: (i, h)),
                   pl.BlockSpec((tq, 128), lambda h, i: (i, h)),
                   pl.BlockSpec((S, QK_NOPE + V_DIM), lambda h, i: (0, h)),
                   pl.BlockSpec((1, S, QK_ROPE), lambda h, i: (h, 0, 0))],
        out_shape=[jax.ShapeDtypeStruct((S, H_A * QK_NOPE), BF16), jax.ShapeDtypeStruct((S, H_A * 128), BF16),
                   jax.ShapeDtypeStruct((S, H_A * (QK_NOPE + V_DIM)), BF16), jax.ShapeDtypeStruct((H_A, S, QK_ROPE), BF16)],
        scratch_shapes=[pltpu.VMEM((1, S, HEAD_PAD), F32), pltpu.VMEM((1, S, V_DIM), F32)],
        dims=("parallel", "arbitrary"), comm=comm)(q_full, k_full, kv, do_a, o_a, lse, cos_p, sin_back)


WIN_SCALE = 1.0 / math.sqrt(HD_B)


WIN_PER_STEP = 4


def _win_specs(S):
    last, B = S // Q_BLOCK - 1, WIN_PER_STEP
    qspec = pl.BlockSpec((B * Q_BLOCK, H_B * HD_B), lambda i: (i, PROJ_QB))
    kspecs = [[pl.BlockSpec((Q_BLOCK, KV_B * HD_B), lambda i, d=d, c=c: (jnp.clip(B * i + d, 0, last), c)) for d in range(-1, B + 1)]
              for c in (PROJ_KB, PROJ_VB)]
    bias_spec = pl.BlockSpec((H_B, SPAN, Q_BLOCK), lambda i: (0, 0, 0))
    sink_spec = pl.BlockSpec((H_B, Q_BLOCK), lambda i: (0, 0))
    return qspec, kspecs, bias_spec, sink_spec


def _win_edge_ok(n, n_blk):
    row = lax.broadcasted_iota(jnp.int32, (SPAN, 1), 0)
    return jnp.logical_not(((n == 0) & (row < WINDOW)) | ((n == n_blk - 1) & (row >= SPAN - WINDOW)))


def _lanes4(pieces):
    return jnp.concatenate(pieces, axis=1)


def _win_probs(kg, q4t, bias_ref, sink_ref, g, edge_ok):
    bias4 = _lanes4([bias_ref[GROUP * g + j] for j in range(GROUP)])
    sink4 = _lanes4([sink_ref[GROUP * g + j:GROUP * g + j + 1, :] for j in range(GROUP)])
    s = jnp.where(edge_ok, _dot(kg, q4t, NN) + bias4, -1e30)
    m = jnp.maximum(jnp.max(s, axis=0, keepdims=True), sink4)
    p = jnp.exp(s - m)
    e_sink = jnp.exp(sink4 - m)
    inv_l = 1.0 / (jnp.sum(p, axis=0, keepdims=True) + e_sink)
    return p * inv_l, e_sink * inv_l


def _group_t(xt, g):
    return _lanes4([xt[HD_B * (GROUP * g + j):HD_B * (GROUP * g + j + 1), :] for j in range(GROUP)])


def _rows_of(ref, b):
    return ref[Q_BLOCK * b:Q_BLOCK * (b + 1), :]


def _win_fwd(proj, o_a, bias_t, sinks_b, S, comm=None):
    n_blk, B = S // Q_BLOCK, WIN_PER_STEP
    qspec, kspecs, bias_spec, sink_spec = _win_specs(S)
    rows = lambda col: pl.BlockSpec((B * Q_BLOCK, H_B * HD_B), lambda i: (i, col))

    def body(q_ref, *refs):
        k_refs, v_refs = refs[:B + 2], refs[B + 2:2 * B + 4]
        bias_ref, sink_ref, ga_ref, gb_ref, oa_ref, o_ref, mixed_ref = refs[2 * B + 4:]
        for b in range(B):
            edge_ok = _win_edge_ok(B * pl.program_id(0) + b, n_blk)
            k = jnp.concatenate([r[...] for r in k_refs[b:b + 3]], axis=0)
            vt = jnp.concatenate([r[...] for r in v_refs[b:b + 3]], axis=0).T
            qt = (_rows_of(q_ref, b).astype(F32) * WIN_SCALE).T.astype(BF16)
            parts = []
            for g in range(KV_B):
                p, _ = _win_probs(k[:, HD_B * g:HD_B * (g + 1)], _group_t(qt, g), bias_ref, sink_ref, g, edge_ok)
                o4t = _dot(vt[HD_B * g:HD_B * (g + 1), :], p.astype(BF16), NN)
                parts += [o4t[:, Q_BLOCK * j:Q_BLOCK * (j + 1)] for j in range(GROUP)]
            ob = jnp.concatenate(parts, axis=0).T
            o_ref[Q_BLOCK * b:Q_BLOCK * (b + 1), :] = ob.astype(BF16)
            ga, gb, oa = (_rows_of(r, b).astype(F32) for r in (ga_ref, gb_ref, oa_ref))
            mixed_ref[Q_BLOCK * b:Q_BLOCK * (b + 1), :] = (_sigmoid(ga) * oa + _sigmoid(gb) * ob).astype(BF16)

    return _pcall(body, name="win_fwd_mix", grid=(n_blk // B,),
                  in_specs=[qspec, *kspecs[0], *kspecs[1], bias_spec, sink_spec, rows(PROJ_GA), rows(PROJ_GB), rows(0)],
                  out_specs=[rows(0), rows(0)],
                  out_shape=[jax.ShapeDtypeStruct((S, H_B * HD_B), BF16)] * 2,
                  dims=("parallel",), comm=comm)(*[proj] * (2 * B + 5), bias_t, sinks_b, proj, proj, o_a)


def _win_bwd(proj, bias_t, sinks_b, do_b, d_proj, S):
    n_blk, B = S // Q_BLOCK, WIN_PER_STEP
    qspec, kspecs, bias_spec, sink_spec = _win_specs(S)

    def body(q_ref, *refs):
        k_refs, v_refs = refs[:B + 2], refs[B + 2:2 * B + 4]
        bias_ref, sink_ref, do_ref, _, dq_ref, dk_ref, dv_ref, dbias_ref, dsink_ref, dsink_acc = refs[2 * B + 4:]
        i = pl.program_id(0)

        @pl.when(i == 0)
        def _():
            dk_ref[...] = jnp.zeros_like(dk_ref)
            dv_ref[...] = jnp.zeros_like(dv_ref)
            dbias_ref[...] = jnp.zeros_like(dbias_ref)
            dsink_acc[...] = jnp.zeros_like(dsink_acc)

        d_bias, d_sink, dk_blocks, dv_blocks = {}, {}, [], []
        for b in range(B):
            edge_ok = _win_edge_ok(B * i + b, n_blk)
            k = jnp.concatenate([r[...] for r in k_refs[b:b + 3]], axis=0)
            v = jnp.concatenate([r[...] for r in v_refs[b:b + 3]], axis=0)
            kt = k.T
            qt = (_rows_of(q_ref, b).astype(F32) * WIN_SCALE).T.astype(BF16)
            dot_ = _rows_of(do_ref, b).astype(F32).T.astype(BF16)
            dq_parts, dks, dvs = [], [], []
            for g in range(KV_B):
                kg, vg = k[:, HD_B * g:HD_B * (g + 1)], v[:, HD_B * g:HD_B * (g + 1)]
                q4t, do4t = _group_t(qt, g), _group_t(dot_, g)
                p, p_sink = _win_probs(kg, q4t, bias_ref, sink_ref, g, edge_ok)
                dp = _dot(vg, do4t, NN)
                delta = jnp.sum(p * dp, axis=0, keepdims=True)
                ds = p * (dp - delta)
                d_bias[g] = ds if b == 0 else d_bias[g] + ds
                d_sink[g] = -p_sink * delta if b == 0 else d_sink[g] - p_sink * delta
                dsb = ds.astype(BF16)
                dq4t = _dot(kt[HD_B * g:HD_B * (g + 1), :], dsb, NN) * WIN_SCALE
                dq_parts += [dq4t[:, Q_BLOCK * j:Q_BLOCK * (j + 1)] for j in range(GROUP)]
                dks.append(_dot(dsb, q4t, NT))
                dvs.append(_dot(p.astype(BF16), do4t, NT))
            dq_ref[Q_BLOCK * b:Q_BLOCK * (b + 1), :] = jnp.concatenate(dq_parts, axis=0).T.astype(BF16)
            dk_blocks.append(jnp.concatenate(dks, axis=1))
            dv_blocks.append(jnp.concatenate(dvs, axis=1))

        for g in range(KV_B):
            for j in range(GROUP):
                dbias_ref[GROUP * g + j] += d_bias[g][:, Q_BLOCK * j:Q_BLOCK * (j + 1)]
            dsink_acc[g:g + 1, :] += d_sink[g]

        def overlap(blocks):
            out = blocks[0]
            for blk in blocks[1:]:
                keep = out.shape[0] - 2 * Q_BLOCK
                out = jnp.concatenate([out[:keep], out[keep:] + blk[:2 * Q_BLOCK], blk[2 * Q_BLOCK:]], axis=0)
            return out

        rows = pl.ds(pl.multiple_of(i * (B * Q_BLOCK), B * Q_BLOCK), (B + 2) * Q_BLOCK)
        dk_ref[rows, :] += overlap(dk_blocks)
        dv_ref[rows, :] += overlap(dv_blocks)

        @pl.when(i == n_blk // B - 1)
        def _():
            acc = dsink_acc[...]
            dsink_ref[...] = jnp.concatenate(
                [jnp.sum(acc[:, Q_BLOCK * j:Q_BLOCK * (j + 1)], axis=1, keepdims=True) for j in range(GROUP)], axis=1)

    whole = lambda shape: pl.BlockSpec(shape, lambda i: (0,) * len(shape))
    return _pcall(
        body, name="win_bwd", grid=(n_blk // B,),
        in_specs=[qspec, *kspecs[0], *kspecs[1], bias_spec, sink_spec, pl.BlockSpec((B * Q_BLOCK, H_B * HD_B), lambda i: (i, 0)), ANY],
        out_specs=[qspec, whole((S + 2 * WINDOW, KV_B * HD_B)),
                   whole((S + 2 * WINDOW, KV_B * HD_B)), whole((H_B, SPAN, Q_BLOCK)), whole((KV_B, GROUP))],
        out_shape=[jax.ShapeDtypeStruct((S, PROJ_P), BF16), jax.ShapeDtypeStruct((S + 2 * WINDOW, KV_B * HD_B), F32),
                   jax.ShapeDtypeStruct((S + 2 * WINDOW, KV_B * HD_B), F32), jax.ShapeDtypeStruct((H_B, SPAN, Q_BLOCK), F32),
                   jax.ShapeDtypeStruct((KV_B, GROUP), F32)],
        scratch_shapes=[pltpu.VMEM((KV_B, GROUP * Q_BLOCK), F32)],
        dims=("arbitrary",), aliases={2 * B + 8: 0})(*[proj] * (2 * B + 5), bias_t, sinks_b, do_b, d_proj)


def _onehot_t(bucket):
    return (bucket == lax.broadcasted_iota(jnp.int32, (NUM_BUCKETS, bucket.shape[1]), 0)).astype(F32)


def _bias_table_bwd(dbias, bucket):
    n = bucket.shape[1]
    tk = _tile(n, 8192)

    def body(d_ref, b_ref, o_ref):
        @pl.when(pl.program_id(0) == 0)
        def _():
            o_ref[...] = jnp.zeros_like(o_ref)

        o_ref[...] += lax.dot_general(d_ref[...], _onehot_t(b_ref[...]), NT, preferred_element_type=F32, precision=lax.Precision.HIGHEST)

    return _pcall(body, name="bias_table_bwd", grid=(n // tk,),
                  in_specs=[pl.BlockSpec((H_B, tk), lambda j: (0, j)), pl.BlockSpec((1, tk), lambda j: (0, j))],
                  out_specs=pl.BlockSpec((H_B, NUM_BUCKETS), lambda j: (0, 0)),
                  out_shape=jax.ShapeDtypeStruct((H_B, NUM_BUCKETS), F32), dims=("arbitrary",))(dbias, bucket)


CONV_STRIP = 128
N_STRIPS = D_FF // CONV_STRIP
CONV_ROWS = 128
HALO = 8


def _strip(rows, half):
    return pl.BlockSpec((rows, CONV_STRIP), lambda j: (0, j + half * N_STRIPS))


def _fill_padded(pad_ref, src_ref, halo, S):
    pad_ref[0:halo, :] = jnp.zeros((halo, CONV_STRIP), F32)
    pad_ref[halo + S:2 * halo + S, :] = jnp.zeros((halo, CONV_STRIP), F32)
    pad_ref[halo:halo + S, :] = src_ref[...].astype(F32)


def _conv_gate_fwd(u, conv_w, conv_b, S):
    R = min(CONV_ROWS, S)

    def body(ug_ref, uv_ref, wg_ref, wv_ref, bg_ref, bv_ref, a_ref, gpad, vpad):
        _fill_padded(gpad, ug_ref, HALO, S)
        _fill_padded(vpad, uv_ref, HALO, S)
        wg, wv, bg, bv = wg_ref[...], wv_ref[...], bg_ref[...], bv_ref[...]

        def conv(pad_ref, r0, w, b):
            dn, mid, up = (pad_ref[pl.ds(r0 + HALO + d, R), :] for d in (-1, 0, 1))
            return dn * w[0:1, :] + mid * w[1:2, :] + up * w[2:3, :] + b

        def step(c, carry):
            r0 = pl.multiple_of(c * R, R)
            g = conv(gpad, r0, wg, bg)
            val = conv(vpad, r0, wv, bv)
            a_ref[pl.ds(r0, R), :] = (g * _sigmoid(g) * val).astype(BF16)
            return carry

        lax.fori_loop(0, S // R, step, 0)

    return _pcall(body, name="conv_gate_fwd", grid=(N_STRIPS,),
                  in_specs=[_strip(S, 0), _strip(S, 1), _strip(3, 0), _strip(3, 1), _strip(1, 0), _strip(1, 1)],
                  out_specs=_strip(S, 0), out_shape=jax.ShapeDtypeStruct((S, D_FF), BF16),
                  scratch_shapes=[pltpu.VMEM((S + 2 * HALO, CONV_STRIP), F32)] * 2,
                  dims=("parallel",))(u, u, conv_w, conv_w, conv_b, conv_b)


def _conv_gate_bwd(u, conv_w, conv_b, da, S):
    R = min(CONV_ROWS, S)
    n = R + 2 * HALO

    def body(ug_ref, uv_ref, wg_ref, wv_ref, bg_ref, bv_ref, da_ref, dug_ref, duv_ref, dwg_ref, dwv_ref, dbg_ref, dbv_ref,
             gpad, vpad, dapad):
        _fill_padded(gpad, ug_ref, 2 * HALO, S)
        _fill_padded(vpad, uv_ref, 2 * HALO, S)
        _fill_padded(dapad, da_ref, HALO, S)
        wg, wv, bg, bv = wg_ref[...], wv_ref[...], bg_ref[...], bv_ref[...]

        def conv(pad_ref, r0, w, b):
            dn, mid, up = (pad_ref[pl.ds(r0 + HALO + d, n), :] for d in (-1, 0, 1))
            return dn * w[0:1, :] + mid * w[1:2, :] + up * w[2:3, :] + b, mid[HALO:HALO + R]

        def conv_bwd(duc, u_mid, w, r0, du_ref):
            dn, mid, up = pltpu.roll(duc, 1, axis=0)[HALO:HALO + R], duc[HALO:HALO + R], pltpu.roll(duc, n - 1, axis=0)[HALO:HALO + R]
            du_ref[pl.ds(r0, R), :] = (up * w[0:1, :] + mid * w[1:2, :] + dn * w[2:3, :]).astype(BF16)
            dw = jnp.concatenate([jnp.sum(up * u_mid, axis=0, keepdims=True), jnp.sum(mid * u_mid, axis=0, keepdims=True),
                                  jnp.sum(dn * u_mid, axis=0, keepdims=True)], axis=0)
            return dw, jnp.sum(mid, axis=0, keepdims=True)

        def step(c, carry):
            dw_g, db_g, dw_v, db_v = carry
            r0 = pl.multiple_of(c * R, R)
            g, ug_mid = conv(gpad, r0, wg, bg)
            val, uv_mid = conv(vpad, r0, wv, bv)
            da_ext = dapad[pl.ds(r0, n), :]
            sg = _sigmoid(g)
            ddw_v, ddb_v = conv_bwd(da_ext * (g * sg), uv_mid, wv, r0, duv_ref)
            ddw_g, ddb_g = conv_bwd(da_ext * val * (sg * (1.0 + g * (1.0 - sg))), ug_mid, wg, r0, dug_ref)
            return dw_g + ddw_g, db_g + ddb_g, dw_v + ddw_v, db_v + ddb_v

        z3, z1 = jnp.zeros((3, CONV_STRIP), F32), jnp.zeros((1, CONV_STRIP), F32)
        dwg_ref[...], dbg_ref[...], dwv_ref[...], dbv_ref[...] = lax.fori_loop(0, S // R, step, (z3, z1, z3, z1))

    half = lambda r, dt: (_strip(r, 0), jax.ShapeDtypeStruct((r, D_FF), dt))
    outs = [half(S, BF16), half(S, BF16), half(3, F32), half(3, F32), half(1, F32), half(1, F32)]
    return _pcall(
        body, name="conv_gate_bwd", grid=(N_STRIPS,),
        in_specs=[_strip(S, 0), _strip(S, 1), _strip(3, 0), _strip(3, 1), _strip(1, 0), _strip(1, 1), _strip(S, 0)],
        out_specs=[o[0] for o in outs], out_shape=[o[1] for o in outs],
        scratch_shapes=[pltpu.VMEM((S + 4 * HALO, CONV_STRIP), F32)] * 2 + [pltpu.VMEM((S + 2 * HALO, CONV_STRIP), F32)],
        dims=("parallel",))(u, u, conv_w, conv_w, conv_b, conv_b, da)


MESH = pl.DeviceIdType.MESH
ANY = pl.BlockSpec(memory_space=pl.ANY)


def _place():
    return lax.axis_index("x"), lax.axis_index("y"), lax.axis_index("c")


def _gather2(ins, outs, sems, finish):
    send_sems, recv_sems, local_sems = sems
    n_arr = len(ins)
    x, y, c = _place()
    me, sibling = (x, y, c), (x, y, 1 - c)
    chips = [(1 - x, y), (x, 1 - y), (1 - x, 1 - y)]

    def slot(a, p):
        return outs[a].at[4 * p[0] + 2 * p[1] + p[2]]

    def copy(a, k, block, to, src=None):
        return pltpu.make_async_remote_copy(
            src_ref=slot(a, block) if src is None else src, dst_ref=slot(a, block),
            send_sem=send_sems.at[a, k], recv_sem=recv_sems.at[a, k], device_id=to, device_id_type=MESH)

    mine = [pltpu.make_async_copy(ins[a], slot(a, me), local_sems.at[a]) for a in range(n_arr)]
    first = []
    for a in range(n_arr):
        first.append(copy(a, 0, me, sibling, src=ins[a]))
        first += [copy(a, 1 + j, me, (*chip, c), src=ins[a]) for j, chip in enumerate(chips)]
    if not finish:
        for cp in mine + first:
            cp.start()
        return
    passed = []
    for j, chip in enumerate(chips):
        for a in range(n_arr):
            copy(a, 1 + j, (*chip, c), me).wait_recv()
            cp = copy(a, 4 + j, (*chip, c), sibling)
            cp.start()
            passed.append(cp)
    for a in range(n_arr):
        copy(a, 0, sibling, me).wait_recv()
        for j, chip in enumerate(chips):
            copy(a, 4 + j, (*chip, 1 - c), me).wait_recv()
    for cp in first + passed:
        cp.wait_send()
    for cp in mine:
        cp.wait()


def _xchg_out_shapes(stacked, replicated):
    return ([jax.ShapeDtypeStruct(s.shape, s.dtype) for s in stacked]
            + [jax.ShapeDtypeStruct((N_DEV, *r.shape), r.dtype) for r in replicated])


def _xchg_sems(n_arr):
    return [pltpu.SemaphoreType.DMA((n_arr, 7)), pltpu.SemaphoreType.DMA((n_arr, 7)), pltpu.SemaphoreType.DMA((n_arr,))]


def _xchg_copies(ins, outs, sems, n_st, with_recv):
    send_sems, recv_sems, local_sems = sems
    n_arr = len(ins)
    x, y, c = _place()
    me = 4 * x + 2 * y + c

    def src(a, idx):
        return ins[a].at[idx] if a < n_st else ins[a]

    mine = [pltpu.make_async_copy(src(a, me), outs[a].at[me], local_sems.at[a]) for a in range(n_arr)]
    pairs = []
    for k in range(1, N_DEV):
        px, py, pc = x ^ (k >> 2), y ^ ((k >> 1) & 1), c ^ (k & 1)
        peer = 4 * px + 2 * py + pc
        for a in range(n_arr):
            sems_k = dict(send_sem=send_sems.at[a, k - 1], recv_sem=recv_sems.at[a, k - 1], device_id_type=MESH)
            send = pltpu.make_async_remote_copy(src_ref=src(a, peer), dst_ref=outs[a].at[me], device_id=(px, py, pc), **sems_k)
            recv = None
            if with_recv:
                recv = pltpu.make_async_remote_copy(src_ref=src(a, peer), dst_ref=outs[a].at[peer], device_id=(x, y, c), **sems_k)
            pairs.append((send, recv))
    return mine, pairs


def _xchg_start(ins, outs, sems, n_st):
    mine, pairs = _xchg_copies(ins, outs, sems, n_st, False)
    for cp in mine:
        cp.start()
    for send, _ in pairs:
        send.start()


def _xchg_finish(ins, outs, sems, n_st):
    mine, pairs = _xchg_copies(ins, outs, sems, n_st, True)
    for _, recv in pairs:
        recv.wait_recv()
    for send, _ in pairs:
        send.wait_send()
    for cp in mine:
        cp.wait()


def _exchange(stacked, replicated, name):
    _, landed = _pcall(lambda: None, name=name, grid=(), in_specs=[], out_specs=[], out_shape=[], comm=(stacked, replicated))()
    return landed


HBM = pl.BlockSpec(memory_space=pltpu.HBM)
SEMS = pl.BlockSpec(memory_space=pltpu.SEMAPHORE)
SIDE_EFFECT = pltpu.SideEffectType.DATAFLOW_SIDE_EFFECTING


N_SPLIT_SEMS = 2 * (N_DEV - 1)


def _split_copies(src, land, sems, with_recv):
    x, y, c = _place()
    me = 4 * x + 2 * y + c
    pairs = []
    for k in range(1, N_DEV):
        px, py, pc = x ^ (k >> 2), y ^ ((k >> 1) & 1), c ^ (k & 1)
        peer = 4 * px + 2 * py + pc
        sems_k = dict(send_sem=sems[k - 1], recv_sem=sems[N_DEV - 1 + k - 1], device_id_type=MESH)
        send = pltpu.make_async_remote_copy(src_ref=src.at[peer], dst_ref=land.at[me], device_id=(px, py, pc), **sems_k)
        recv = None
        if with_recv:
            recv = pltpu.make_async_remote_copy(src_ref=src.at[peer], dst_ref=land.at[peer], device_id=(x, y, c), **sems_k)
        pairs.append((send, recv))
    return pairs


def _exchange_start(stacked, name):
    def body(src, land, *rest):
        for send, _ in _split_copies(src, land, rest[:N_SPLIT_SEMS], False):
            send.start()
        rest[-1][...] = jnp.zeros_like(rest[-1])

    shape = pltpu.HBM(stacked.shape, stacked.dtype)
    res = pl.pallas_call(
        body, name=name, in_specs=[HBM, HBM],
        out_shape=(*[pltpu.SemaphoreType.DMA(())] * N_SPLIT_SEMS, shape, shape, jax.ShapeDtypeStruct((8, 128), F32)),
        out_specs=(*[SEMS] * N_SPLIT_SEMS, HBM, HBM, pl.BlockSpec(memory_space=pltpu.VMEM)),
        input_output_aliases={0: N_SPLIT_SEMS, 1: N_SPLIT_SEMS + 1},
        compiler_params=pltpu.CompilerParams(has_side_effects=SIDE_EFFECT),
    )(pltpu.with_memory_space_constraint(stacked, pltpu.HBM),
      pltpu.with_memory_space_constraint(lax.empty(stacked.shape, stacked.dtype), pltpu.HBM))
    return res[:N_SPLIT_SEMS], res[N_SPLIT_SEMS], res[N_SPLIT_SEMS + 1], res[-1]


def _exchange_wait(sems, src, land, after, name):
    def body(src_ref, land_ref, *rest):
        for send, recv in _split_copies(src_ref, land_ref, rest[:N_SPLIT_SEMS], True):
            send.wait_send()
            recv.wait_recv()

    shape = pltpu.HBM(src.shape, src.dtype)
    return pl.pallas_call(
        body, name=name, in_specs=[HBM, HBM, *[SEMS] * N_SPLIT_SEMS, ANY],
        out_shape=(shape, shape), out_specs=(HBM, HBM), input_output_aliases={0: 0, 1: 1},
        compiler_params=pltpu.CompilerParams(has_side_effects=SIDE_EFFECT))(src, land, *sems, after)[1]


def _adamw(parts, w, m, v, name):
    _, R, C = w.shape
    tr = R if R <= 512 else max(t for t in range(16, 513, 16) if R % t == 0)
    pr = tr if parts.shape[1] == R else -(-R // 16) * 16
    assert pr == tr or tr == R

    def body(p_ref, w_ref, m_ref, v_ref, g_ref, d_ref, nm_ref, nv_ref):
        g = p_ref[0].astype(F32)[:tr]
        for s in range(1, N_DEV):
            g = g + p_ref[s].astype(F32)[:tr]
        m2 = ADAM_B1 * m_ref[0] + (1.0 - ADAM_B1) * g
        v2 = ADAM_B2 * v_ref[0] + (1.0 - ADAM_B2) * (g * g)
        m_hat = m2 / (1.0 - ADAM_B1 ** ADAM_STEP)
        v_hat = v2 / (1.0 - ADAM_B2 ** ADAM_STEP)
        g_ref[0] = g
        d_ref[0] = -ADAM_LR * (m_hat / (jnp.sqrt(v_hat) + ADAM_EPS) + ADAM_WD * w_ref[0])
        nm_ref[0] = m2
        nv_ref[0] = v2

    blk = pl.BlockSpec((1, tr, C), lambda i: (0, i, 0))
    return _pcall(body, name=name, grid=(R // tr,),
                  in_specs=[pl.BlockSpec((N_DEV, pr, C), lambda i: (0, i, 0)), blk, blk, blk],
                  out_specs=[blk] * 4, out_shape=[jax.ShapeDtypeStruct((1, R, C), F32)] * 4,
                  dims=("parallel",))(parts, w, m, v)


def _t5_bucket(rel):
    nb = NUM_BUCKETS // 2
    max_exact = nb // 2
    base = (rel > 0).astype(jnp.int32) * nb
    n = jnp.abs(rel)
    nf = jnp.maximum(n, 1).astype(jnp.float32)
    large = max_exact + (jnp.log(nf / max_exact) / math.log(MAX_DISTANCE / max_exact) * (nb - max_exact)).astype(jnp.int32)
    large = jnp.minimum(large, nb - 1)
    return base + jnp.where(n < max_exact, n, large)


def _unstack_cols(g):
    return jnp.transpose(g, (1, 0, 2)).reshape(g.shape[1], N_DEV * g.shape[2])


def _stack_cols(w, n=N_DEV):
    R = w.shape[0]
    return jnp.transpose(w.reshape(R, n, w.shape[1] // n), (1, 0, 2))


def _stack_halves(g, v):
    return jnp.concatenate([_stack_cols(g, N_DEV // 2), _stack_cols(v, N_DEV // 2)], axis=0)


def kernel(x, positions, norm1_g, w_in, q_a_norm_g, w_q_b, kv_a_norm_g, w_kv_b, rel_bias, sinks, w_out, norm2_g, w_up, conv_w, conv_b, w_down, final_norm_g, loss_target, m_norm1_g, m_w_in, m_q_a_norm_g, m_w_q_b, m_kv_a_norm_g, m_w_kv_b, m_rel_bias, m_sinks, m_w_out, m_norm2_g, m_w_up, m_conv_w, m_conv_b, m_w_down, m_final_norm_g, v_norm1_g, v_w_in, v_q_a_norm_g, v_w_q_b, v_kv_a_norm_g, v_w_kv_b, v_rel_bias, v_sinks, v_w_out, v_norm2_g, v_w_up, v_conv_w, v_conv_b, v_w_down, v_final_norm_g):
    S = x.shape[1]
    x = x[0]
    target = loss_target[0]
    TM = 256

    tr = lambda w: jnp.swapaxes(w, 1, 2)
    half = QK_ROPE // 2
    inv_freq = ROPE_THETA ** (-jnp.arange(half, dtype=F32) / half)
    inv_tile = jnp.concatenate([inv_freq, inv_freq, jnp.zeros((128 - QK_ROPE,), F32)])[None, :]
    sign_tile = jnp.concatenate([-jnp.ones((half,), F32), jnp.ones((half,), F32), jnp.zeros((128 - QK_ROPE,), F32)])[None, :]

    qa = jnp.arange(Q_BLOCK, dtype=jnp.int32)[:, None]
    kc = jnp.arange(SPAN, dtype=jnp.int32)[None, :]
    rel = (kc - WINDOW - qa).T
    in_band = (jnp.abs(rel) <= WINDOW).astype(F32).reshape(1, Q_BLOCK * SPAN)
    bucket = _t5_bucket(rel).astype(jnp.int32).reshape(1, Q_BLOCK * SPAN)
    bias_cols = Q_BLOCK * SPAN // (S // TM)

    def norm1_tables(a, g, pos, inv, sign, rb, bk, band):
        ang = pos * inv
        rot = jnp.sin(ang) * sign
        t = lax.dot_general(rb, _onehot_t(bk), NN, preferred_element_type=F32, precision=lax.Precision.HIGHEST)
        return _rms(a, g), jnp.cos(ang), rot, -rot, jnp.where(band > 0.5, t, -1e30)

    col_block = lambda arr: (arr, lambda tm: pl.BlockSpec((arr.shape[0], bias_cols), lambda i, *_: (0, i)))
    (h1, cos_p, sin_fwd, sin_back, bias_t), (g_in,) = _rowwise(
        norm1_tables, "norm1_gather", S, TM,
        [_rows(x), _whole(norm1_g), _rows(positions.astype(F32)[:, None]), _whole(inv_tile), _whole(sign_tile),
         _whole(rel_bias.T), col_block(bucket), col_block(in_band)],
        [("rows", D_MODEL, BF16)] + [("rows", 128, F32)] * 3 + [("colblock", H_B, bias_cols, Q_BLOCK * SPAN, F32)],
        gather=[tr(w_in)[0].astype(BF16)])
    bias_t = bias_t.reshape(H_B, SPAN, Q_BLOCK)
    late_weights = [w_out[0].astype(BF16), tr(w_up)[0].astype(BF16), conv_w[0]]
    wi = g_in.reshape(W_IN_COLS, D_MODEL)
    c0, c1, c2, c3, c4, c5 = (sum(W_IN_SIZES[:i + 1]) for i in range(6))
    w_in_pt = jnp.concatenate([wi[c4:c5], wi[c5:], wi[c1:c2], wi[:c0], wi[c2:c3], wi[c3:c4],
                               wi[c0:c0 + KV_LORA], wi[c0 + KV_LORA:c1], jnp.zeros((64, D_MODEL), BF16)], axis=0)

    cos, sin = cos_p[:, :half], sin_back[:, :half]
    sinks_b = jnp.broadcast_to(sinks.reshape(H_B, 1), (H_B, Q_BLOCK))

    proj, (g_qb, g_kvb) = _matmul(h1, w_in_pt, "nt", BF16, "proj", comm=([], [tr(w_q_b)[0].astype(BF16), w_kv_b[0].astype(BF16)]))
    wq = g_qb.reshape(H_A, QK_HEAD, Q_LORA)
    w_qb_pt = jnp.concatenate([wq[:, :QK_NOPE].reshape(H_A * QK_NOPE, Q_LORA),
                               jnp.pad(wq[:, QK_NOPE:], ((0, 0), (0, 128 - QK_ROPE), (0, 0))).reshape(H_A * 128, Q_LORA)], axis=0)
    w_kvb = _unstack_cols(g_kvb)

    def lat_fn(qlat, ckv, kr, gq, gkv, cs, sn):
        return _rms(qlat, gq), _rms(ckv, gkv), _rope128(kr, cs, sn)

    qn, ckvn, k_rope = _rowwise(lat_fn, "latents", S, TM,
                                [_rows(proj, 256, PROJ_QLAT), _rows(proj, 128, PROJ_CKV), _rows(proj, 128, PROJ_KROPE),
                                 _whole(q_a_norm_g), _whole(kv_a_norm_g), _rows(cos_p), _rows(sin_fwd)],
                                [("rows", Q_LORA, BF16), ("rows", KV_LORA, BF16), ("rows", 128, BF16)])
    def q_heads_fn(q, cs, sn):
        q = q * MLA_PRESCALE
        return (jnp.concatenate([jnp.concatenate([q[:, 128 * h:128 * (h + 1)], _rope128(q[:, 128 * (H_A + h):128 * (H_A + h + 1)], cs, sn)],
                                                 axis=1)[None] for h in range(H_A)], axis=0),)

    (q_full,) = _matmul(qn, w_qb_pt, "nt", None, "q_up_heads", tm=512, tn=2048,
                        epi=(q_heads_fn, [_rows(cos_p), _rows(sin_fwd)], [("heads", H_A, HEAD_PAD, BF16)]))

    def k_heads_fn(kvf, kr):
        return kvf, jnp.concatenate([jnp.concatenate([kvf[:, 256 * h:256 * h + QK_NOPE], kr], axis=1)[None] for h in range(H_A)], axis=0)

    kv, k_full = _matmul(ckvn, w_kvb, "nn", None, "kv_up_heads", tm=512, tn=2048,
                         epi=(k_heads_fn, [_rows(k_rope)], [("rows", H_A * (QK_NOPE + V_DIM), BF16), ("heads", H_A, HEAD_PAD, BF16)]))
    (o_a, lse), (g_out, g_up, g_cw) = _mla_fwd(q_full, k_full, kv, S, comm=([], late_weights))
    w_out_f = g_out.reshape(D_MODEL, D_MODEL)
    w_up_t = g_up.reshape(2 * D_FF, D_MODEL)
    conv_w_f = _unstack_cols(g_cw)

    (o_b, mixed), (g_down,) = _win_fwd(proj, o_a, bias_t, sinks_b, S, comm=([], [w_down[0].astype(BF16)]))
    w_down_f = g_down.reshape(D_FF, D_MODEL)

    x1, h2 = _matmul(mixed, w_out_f, "nn", None, "out_proj", residual=x, tm=512,
                     epi=(lambda a, g: (a, _rms(a, g)), [_whole(norm2_g)], [("rows", D_MODEL, F32), ("rows", D_MODEL, BF16)]))
    u = _matmul(h2, w_up_t, "nt", BF16, "ffn_up", tn=1408)
    act = _conv_gate_fwd(u, conv_w_f, conv_b, S)

    def final_fn(a, g, t):
        err = _rms(a, g) - t
        loss = 0.5 * jnp.sum(jnp.mean(err * err, axis=-1, keepdims=True), axis=0, keepdims=True)
        dx, dg = _rms_bwd(err * (1.0 / D_MODEL), a, g)
        return dx, dx, dg, jnp.broadcast_to(loss, (1, 128))

    gfin = final_norm_g.reshape(1, D_MODEL)
    dx2, dx2_b, d_gfin, loss_row = _matmul(
        act, w_down_f, "nn", None, "ffn_down_loss", residual=x1, tm=512,
        epi=(final_fn, [_whole(gfin), _rows(target)],
             [("rows", D_MODEL, F32), ("rows", D_MODEL, BF16), ("acc", 1, D_MODEL), ("acc", 1, 128)]))
    d_act = _matmul(dx2_b, w_down_f, "nt", BF16, "ffn_down_dx", tn=1408)
    d_w_down = _matmul(act, dx2_b, "tn", BF16, "ffn_down_dw")
    du_g, du_v, dcw_g, dcw_v, dcb_g, dcb_v = _conv_gate_bwd(u, conv_w_f, conv_b, d_act, S)
    d_conv_b = jnp.concatenate([dcb_g, dcb_v], axis=1)

    def norm_bwd_fn(dh, a, g, dres):
        dx, dg = _rms_bwd(dh, a, g)
        dx = dx + dres
        return dx, dx, dg

    dx1, dx1_b, d_g2 = _matmul(du_g, w_up_t, "nn", None, "ffn_up_dx_norm2_bwd", tm=256, a2=du_v,
                               epi=(norm_bwd_fn, [_rows(x1), _whole(norm2_g), _rows(dx2)],
                                    [("rows", D_MODEL, F32), ("rows", D_MODEL, BF16), ("acc", 1, D_MODEL)]))
    d_w_up_t = _matmul(du_g, h2, "tn", BF16, "ffn_up_dw", tm=256, a2=du_v)
    d_w_out = _matmul(mixed, dx1_b, "tn", BF16, "out_proj_dw", tm=512)

    def gate_bwd_fn(dm, ga, gb, oa, ob):
        sa, sb = _sigmoid(ga), _sigmoid(gb)
        return jnp.concatenate([dm * oa * sa * (1.0 - sa), dm * ob * sb * (1.0 - sb)], axis=1), dm * sa, dm * sb

    d_proj, do_a, do_b = _matmul(
        dx1_b, w_out_f, "nt", None, "out_proj_dx_gate_bwd", tm=512,
        epi=(gate_bwd_fn, [_rows(proj, 1024, PROJ_GA), _rows(proj, 1024, PROJ_GB), _rows(o_a), _rows(o_b)],
             [("cols", 2 * D_MODEL, 0, PROJ_P, BF16), ("rows", D_MODEL, BF16), ("rows", D_MODEL, BF16)]))

    d_proj, dk_acc, dv_acc, d_bias, d_sinks_g = _win_bwd(proj, bias_t, sinks_b, do_b, d_proj, S)
    d_sinks = d_sinks_g.reshape(1, H_B)

    early = [d_w_out.reshape(N_DEV, D_MODEL // N_DEV, D_MODEL), d_w_up_t.reshape(N_DEV, 2 * D_FF // N_DEV, D_MODEL),
             d_w_down.reshape(N_DEV, D_FF // N_DEV, D_MODEL), _stack_halves(dcw_g, dcw_v)]
    (dq_nope, dq_rope, dkv, dkr_heads), recv_early = _mla_bwd(q_full, k_full, kv, do_a, o_a, lse, cos_p, sin_back, S, comm=(early, []))

    d_qn = _matmul(dq_nope, w_qb_pt, "nn", F32, "q_up_dx", a2=dq_rope)
    d_ckvn = _matmul(dkv, w_kvb, "nt", F32, "kv_up_dx")

    def lat_bwd_fn(dqn, dckvn, dkr_h, cs, sn, qlat, ckv, gq, gkv, dkb, dvb):
        dql, dgq = _rms_bwd(dqn, qlat, gq)
        dck, dgkv = _rms_bwd(dckvn, ckv, gkv)
        dkr = dkr_h[0]
        for h in range(1, H_A):
            dkr = dkr + dkr_h[h]
        r1, r2 = _rope_bwd(dkr[:, :half], dkr[:, half:], cs, sn)
        tail = jnp.concatenate([dql, dkb, dvb, dck, r1, r2, jnp.zeros_like(dkr)], axis=1)
        return tail, dgq, dgkv

    shifted = lambda arr: (arr, lambda tm: pl.BlockSpec((tm, arr.shape[1]), lambda i, *_: (i + WINDOW // tm, 0)))
    TL = min(128, S)
    d_proj, d_gq, d_gkv = _rowwise(lat_bwd_fn, "latents_bwd", S, TL,
                                   [_rows(d_qn), _rows(d_ckvn), _heads(dkr_heads), _rows(cos), _rows(sin), _rows(proj, 256, PROJ_QLAT), _rows(proj, 128, PROJ_CKV),
                                    _whole(q_a_norm_g), _whole(kv_a_norm_g), shifted(dk_acc), shifted(dv_acc)],
                                   [("cols", 1024, 3, PROJ_P, BF16), ("acc", 1, Q_LORA), ("acc", 1, KV_LORA)], into=(d_proj, 0))
    dp = _matmul(d_proj, h1, "tn", BF16, "proj_dw", tm=512)

    late = jnp.concatenate([dp[3072:3328], dp[3840:3968], dp[3968:4032], dp[2048:3072], dp[3328:3584],
                            dp[3584:3840], dp[0:1024], dp[1024:2048]], axis=0).reshape(N_DEV, W_IN_COLS // N_DEV, D_MODEL)
    late_sems, late_src, late_land, started = _exchange_start(late, "late_grads_start")

    def norm1_bwd_fn(dh, a, g, dres):
        dx, dg = _rms_bwd(dh, a, g)
        return dx + dres, dg

    grad_x, d_g1 = _matmul(
        d_proj, w_in_pt, "nn", None, "proj_dx_norm1_bwd", tm=512,
        epi=(norm1_bwd_fn, [_rows(x), _whole(norm1_g + started[:1, :1]), _rows(dx1)], [("rows", D_MODEL, F32), ("acc", 1, D_MODEL)]))

    transposed = ("w_in", "w_q_b", "w_up")
    ready_names = ["w_out", "w_up", "w_down", "conv_w"]
    ready_wmv = [(w_out, m_w_out, v_w_out), (tr(w_up), tr(m_w_up), tr(v_w_up)), (w_down, m_w_down, v_w_down), (conv_w, m_conv_w, v_conv_w)]
    recv_early, _ = lax.optimization_barrier((list(recv_early), started))
    big = {n: _adamw(r, *wmv, "adamw_" + n) for n, r, wmv in zip(ready_names, recv_early, ready_wmv)}

    (d_bias, dq_nope_l, dkv_l), _ = lax.optimization_barrier(((d_bias, dq_nope, dkv), started))
    d_rel_bias = _bias_table_bwd(d_bias.reshape(H_B, Q_BLOCK * SPAN), bucket).T
    d_w_qb_pt = _matmul(dq_nope_l, qn, "tn", BF16, "q_up_dw", tm=512, a2=dq_rope)
    d_w_kvb = _matmul(ckvn, dkv_l, "tn", BF16, "kv_up_dw", tn=2048)
    d_w_qb_t = jnp.concatenate([d_w_qb_pt[:H_A * QK_NOPE].reshape(H_A, QK_NOPE, Q_LORA),
                                d_w_qb_pt[H_A * QK_NOPE:].reshape(H_A, 128, Q_LORA)[:, :QK_ROPE]], axis=1)

    after = lax.optimization_barrier([big[n][0] for n in ready_names] + [d_rel_bias, d_w_qb_t, d_w_kvb])
    landed = _exchange_wait(late_sems, late_src, late_land, after[-1], "late_grads_wait")
    me = 4 * lax.axis_index("x") + 2 * lax.axis_index("y") + lax.axis_index("c")
    landed = lax.dynamic_update_slice_in_dim(landed, lax.dynamic_slice_in_dim(late, me, 1, axis=0), me, axis=0)
    big["w_in"] = _adamw(landed, tr(w_in), tr(m_w_in), tr(v_w_in), "adamw_w_in")

    small_parts = [d_g1, d_gq, d_gkv, d_rel_bias.reshape(1, NUM_BUCKETS * H_B), d_sinks, d_g2, d_conv_b, d_gfin, loss_row[:, :1]]
    small = jnp.concatenate(small_parts, axis=1)
    n_small = small.shape[1]
    pad = (-n_small) % 128
    small = jnp.pad(small, ((0, 0), (0, pad)))
    small, _ = lax.optimization_barrier((small, [landed, *after]))
    recv_qb, recv_kvb, recv_small = _exchange([after[-2], _stack_cols(after[-1])], [small], "exchange_small_grads")
    big["w_q_b"] = _adamw(recv_qb, tr(w_q_b), tr(m_w_q_b), tr(v_w_q_b), "adamw_w_q_b")
    big["w_kv_b"] = _adamw(recv_kvb, w_kv_b, m_w_kv_b, v_w_kv_b, "adamw_w_kv_b")

    def flat(a):
        return a.reshape(1, -1)

    small_w = [norm1_g, q_a_norm_g, kv_a_norm_g, rel_bias, sinks, norm2_g, conv_b, final_norm_g]
    small_m = [m_norm1_g, m_q_a_norm_g, m_kv_a_norm_g, m_rel_bias, m_sinks, m_norm2_g, m_conv_b, m_final_norm_g]
    small_v = [v_norm1_g, v_q_a_norm_g, v_kv_a_norm_g, v_rel_bias, v_sinks, v_norm2_g, v_conv_b, v_final_norm_g]
    cat = lambda parts: jnp.pad(jnp.concatenate([flat(a) for a in parts], axis=1), ((0, 0), (0, pad + 1)))[None]
    sm = _adamw(recv_small, cat(small_w), cat(small_m), cat(small_v), "adamw_small")

    loss = sm[0][0, 0, n_small - 1]
    order =["norm1_g", "w_in", "q_a_norm_g", "w_q_b", "kv_a_norm_g", "w_kv_b", "rel_bias", "sinks", "w_out", "norm2_g", "w_up",
             "conv_w", "conv_b", "w_down", "final_norm_g"]
    small_names = ["norm1_g", "q_a_norm_g", "kv_a_norm_g", "rel_bias", "sinks", "norm2_g", "conv_b", "final_norm_g"]
    offs, o = {}, 0
    for n, a in zip(small_names, small_w):
        offs[n] = (o, a.size, a.shape)
        o += a.size
    outs = [loss, grad_x[None]]
    for kind in range(4):
        for n in order:
            if n in big:
                outs.append(tr(big[n][kind]) if n in transposed else big[n][kind])
            else:
                o, size, shape = offs[n]
                outs.append(sm[kind][0, 0, o:o + size].reshape(shape))
    return tuple(outs)
```
